```python
import jax
import jax.numpy as jnp
from jax import lax
import numpy as np

D_MODEL = 1024
BATCH = 1
SEQ = 16384
DEPTH = 1
DEC_BATCH = 128
DEC_SEQ = 1
PAST_LEN = 16384
PAGE_SIZE = 128

N_META = 16
ATTN_WIDTH = 512
N_HEADS = 8
HEAD_DIM = 64
N_KV_HEADS = 2
GQA_GROUP = N_HEADS // N_KV_HEADS
KV_WIDTH = N_KV_HEADS * HEAD_DIM
WINDOW = 128
BLOCK = WINDOW
POOL_WIDTH = D_MODEL - ATTN_WIDTH
POOL_WINDOWS = (2, 4, 8, 16)
N_POOL_GROUPS = len(POOL_WINDOWS)
POOL_GROUP_DIM = POOL_WIDTH // N_POOL_GROUPS
POOL_STATE = max(POOL_WINDOWS) - 1
IN_WIDTH = ATTN_WIDTH + 2 * KV_WIDTH + POOL_WIDTH
MIX_WIDTH = ATTN_WIDTH + POOL_WIDTH
CACHE_ROWS = N_META + WINDOW
N_EXPERTS = 32
TOP_K = 4
D_EXPERT = D_MODEL
SWIGLU_ALPHA = 1.702
SWIGLU_LIMIT = 7.0
MOE_BLOCK = 128
NORM_EPS = 1e-5

kernel_name = "hymba_swa_pool_moe_step"

F32 = jnp.float32


def rmsnorm(x, g):
    xf = x.astype(F32)
    y = xf * lax.rsqrt(jnp.mean(xf * xf, axis=-1, keepdims=True) + NORM_EPS)
    return (y * g.astype(F32)).astype(x.dtype)


def alibi_slopes():
    h = jnp.arange(1, N_HEADS + 1, dtype=F32)
    return jnp.exp2(-8.0 * h / N_HEADS).reshape(N_KV_HEADS, GQA_GROUP)


def split_mixing(z):
    lead = z.shape[:-1]
    a, b, c = ATTN_WIDTH, ATTN_WIDTH + KV_WIDTH, ATTN_WIDTH + 2 * KV_WIDTH
    q = z[..., :a].reshape(lead + (N_KV_HEADS, GQA_GROUP, HEAD_DIM))
    k = z[..., a:b].reshape(lead + (N_KV_HEADS, HEAD_DIM))
    v = z[..., b:c].reshape(lead + (N_KV_HEADS, HEAD_DIM))
    p = z[..., c:]
    return q, k, v, p


def sink_attend(q, k, v, valid, dist, sinks):
    scores = jnp.einsum('...qkgd,...skd->...kgqs', q, k, preferred_element_type=F32) * (HEAD_DIM ** -0.5)
    bias = -alibi_slopes()[:, :, None, None] * dist[..., None, None, :, :]
    scores = jnp.where(valid[..., None, None, :, :], scores + bias, -jnp.inf)
    sink = jnp.broadcast_to(sinks.astype(F32).reshape(N_KV_HEADS, GQA_GROUP, 1, 1), scores.shape[:-1] + (1,))
    probs = jax.nn.softmax(jnp.concatenate([scores, sink], axis=-1), axis=-1)[..., :-1]
    return jnp.einsum('...kgqs,...skd->...qkgd', probs.astype(v.dtype), v)


def causal_multiscale_pool(p, pos):
    pf = p.astype(F32)
    cs = jnp.cumsum(pf, axis=1)
    means = []
    for gi, w in enumerate(POOL_WINDOWS):
        c = cs[..., gi * POOL_GROUP_DIM:(gi + 1) * POOL_GROUP_DIM]
        c_prev = jnp.pad(c, ((0, 0), (w, 0), (0, 0)))[:, :-w]
        cnt = jnp.minimum(pos + 1, w).astype(F32)[None, :, None]
        means.append((c - c_prev) / cnt)
    return (jnp.concatenate(means, axis=-1) - pf).astype(p.dtype)


def pool_mixer(pooled, w_pool, pool_scale):
    n, t, _ = pooled.shape
    g = pooled.reshape(n, t, N_POOL_GROUPS, POOL_GROUP_DIM)
    y = jnp.einsum('ntgc,gcd->ntgd', g, w_pool).reshape(n, t, POOL_WIDTH)
    return y * pool_scale


def routed_ffn(h, router_w, router_b, w_gate_up, b_gate_up, w_down, b_down):
    n = h.shape[0]
    logits = jnp.dot(h, router_w, preferred_element_type=F32) + router_b.astype(F32)
    top_val, top_idx = lax.top_k(logits, TOP_K)
    gates = jax.nn.softmax(top_val, axis=-1)
    nk = n * TOP_K
    flat_e = top_idx.reshape(nk).astype(jnp.int32)
    order = jnp.argsort(flat_e)
    sorted_e = flat_e[order]
    counts = jnp.bincount(flat_e, length=N_EXPERTS).astype(jnp.int32)
    padded = (counts + MOE_BLOCK - 1) // MOE_BLOCK * MOE_BLOCK
    pad_end = jnp.cumsum(padded)
    pad_start = pad_end - padded
    start = jnp.cumsum(counts) - counts
    dest = pad_start[sorted_e] + jnp.arange(nk, dtype=jnp.int32) - start[sorted_e]
    n_blocks = -(-nk // MOE_BLOCK) + N_EXPERTS
    cap = n_blocks * MOE_BLOCK
    slot_tok = jnp.full((cap,), n, jnp.int32).at[dest].set((order // TOP_K).astype(jnp.int32))
    slot_gate = jnp.zeros((cap,), F32).at[dest].set(gates.reshape(nk)[order])
    block_start = jnp.arange(n_blocks, dtype=jnp.int32) * MOE_BLOCK
    block_expert = jnp.minimum(jnp.searchsorted(pad_end, block_start, side='right'), N_EXPERTS - 1)
    h_pad = jnp.concatenate([h, jnp.zeros((1, h.shape[1]), h.dtype)], axis=0)
    xb = h_pad[slot_tok].reshape(n_blocks, MOE_BLOCK, D_MODEL)

    def expert_block(args):
        x_blk, e = args
        gu = x_blk @ w_gate_up[e] + b_gate_up[e]
        gate = jnp.minimum(gu[:, :D_EXPERT], SWIGLU_LIMIT)
        up = jnp.clip(gu[:, D_EXPERT:], -SWIGLU_LIMIT, SWIGLU_LIMIT)
        act = gate * jax.nn.sigmoid(SWIGLU_ALPHA * gate) * (up + 1.0)
        return act @ w_down[e] + b_down[e]

    yb = lax.map(expert_block, (xb, block_expert)).reshape(cap, D_MODEL)
    y = jnp.zeros((n + 1, D_MODEL), F32).at[slot_tok].add(yb.astype(F32) * slot_gate[:, None])
    return y[:n].astype(h.dtype)


def channel_mixer(x, l, norm_ffn, router_w, router_b, w_gate_up, b_gate_up, w_down, b_down):
    h = rmsnorm(x, norm_ffn[l])
    y = routed_ffn(h.reshape(-1, D_MODEL), router_w[l], router_b[l], w_gate_up[l], b_gate_up[l],
                   w_down[l], b_down[l])
    return x + y.reshape(x.shape)


def prompt_token_mixers(x, sinks, w_in, w_pool, pool_scale, w_out, norm_attn):
    b, t, _ = x.shape
    h = rmsnorm(x, norm_attn)
    q, k, v, p = split_mixing(h @ w_in)
    mpos = jnp.arange(N_META)
    o_meta = sink_attend(q[:, :N_META], k[:, :N_META], v[:, :N_META],
                         mpos[:, None] >= mpos[None, :], jnp.zeros((N_META, N_META), F32), sinks)
    nb = SEQ // BLOCK
    qb = q[:, N_META:].reshape(b, nb, BLOCK, N_KV_HEADS, GQA_GROUP, HEAD_DIM)
    kb = k[:, N_META:].reshape(b, nb, BLOCK, N_KV_HEADS, HEAD_DIM)
    vb = v[:, N_META:].reshape(b, nb, BLOCK, N_KV_HEADS, HEAD_DIM)
    def band(a, meta):
        prev = jnp.concatenate([jnp.zeros_like(a[:, :1]), a[:, :-1]], axis=1)
        meta_b = jnp.broadcast_to(meta[:, None], (b, nb) + meta.shape[1:])
        return jnp.concatenate([meta_b, prev, a], axis=2)
    keys, vals = band(kb, k[:, :N_META]), band(vb, v[:, :N_META])
    i = jnp.arange(BLOCK)[:, None]
    j = jnp.arange(2 * BLOCK)[None, :]
    rel = i + BLOCK - j
    band_valid = (rel >= 0) & (rel <= WINDOW)
    not_before_start = (jnp.arange(nb)[:, None, None] > 0) | (j[None] >= BLOCK)
    valid = jnp.concatenate([jnp.ones((nb, BLOCK, N_META), bool), band_valid[None] & not_before_start], axis=-1)
    dist = jnp.concatenate([jnp.zeros((BLOCK, N_META), F32), rel.astype(F32)], axis=-1)
    o_real = sink_attend(qb, keys, vals, valid, dist, sinks).reshape(b, SEQ, ATTN_WIDTH)
    o_attn = jnp.concatenate([o_meta.reshape(b, N_META, ATTN_WIDTH), o_real], axis=1)
    o_pool = pool_mixer(causal_multiscale_pool(p, jnp.arange(t)), w_pool, pool_scale)
    x = x + jnp.concatenate([o_attn, o_pool.astype(o_attn.dtype)], axis=-1) @ w_out
    new_k = jnp.concatenate([k[:, :N_META], k[:, -WINDOW:]], axis=1)
    new_v = jnp.concatenate([v[:, :N_META], v[:, -WINDOW:]], axis=1)
    new_pool = p[:, -POOL_STATE:]
    return x, new_k, new_v, new_pool


def sample_token_mixers(x, cache_k, cache_v, state_pool, sinks, w_in, w_pool, pool_scale, w_out, norm_attn):
    nq = x.shape[1]
    h = rmsnorm(x, norm_attn)
    q, k, v, p = split_mixing(h @ w_in)
    keys = jnp.concatenate([cache_k.astype(k.dtype), k], axis=1)
    vals = jnp.concatenate([cache_v.astype(v.dtype), v], axis=1)
    q_pos = PAST_LEN + jnp.arange(nq)
    k_pos = jnp.concatenate([jnp.arange(N_META), PAST_LEN - WINDOW + jnp.arange(WINDOW), q_pos])
    is_meta = jnp.arange(CACHE_ROWS + nq) < N_META
    d = q_pos[:, None] - k_pos[None, :]
    valid = is_meta[None, :] | ((k_pos[None, :] >= N_META) & (d >= 0) & (d <= WINDOW))
    dist = jnp.where(is_meta[None, :], 0, d).astype(F32)
    o_attn = sink_attend(q, keys, vals, valid, dist, sinks).reshape(x.shape[0], nq, ATTN_WIDTH)
    buf = jnp.concatenate([state_pool.astype(p.dtype), p], axis=1)
    buf_pos = PAST_LEN - POOL_STATE + jnp.arange(POOL_STATE + nq)
    pooled = causal_multiscale_pool(buf, buf_pos)[:, POOL_STATE:]
    o_pool = pool_mixer(pooled, w_pool, pool_scale)
    x = x + jnp.concatenate([o_attn, o_pool.astype(o_attn.dtype)], axis=-1) @ w_out
    new_k = jnp.concatenate([cache_k[:, :N_META].astype(k.dtype), keys[:, N_META:][:, -WINDOW:]], axis=1)
    new_v = jnp.concatenate([cache_v[:, :N_META].astype(v.dtype), vals[:, N_META:][:, -WINDOW:]], axis=1)
    new_pool = buf[:, -POOL_STATE:]
    return x, new_k, new_v, new_pool


def setup_inputs(seed: int = 0) -> dict:
    key = jax.random.key(seed)
    ks = jax.random.split(key, 24)
    nrm = lambda k, s, sc=1.0: jax.random.normal(k, s, F32) * sc
    return {
        "x_prompt": nrm(ks[0], (BATCH, SEQ, D_MODEL)),
        "x_sample": nrm(ks[1], (DEC_BATCH, DEC_SEQ, D_MODEL)),
        "cache_k": nrm(ks[2], (DEPTH, DEC_BATCH, CACHE_ROWS, N_KV_HEADS, HEAD_DIM)),
        "cache_v": nrm(ks[3], (DEPTH, DEC_BATCH, CACHE_ROWS, N_KV_HEADS, HEAD_DIM)),
        "state_pool": nrm(ks[4], (DEPTH, DEC_BATCH, POOL_STATE, POOL_WIDTH)),
        "meta_tokens": nrm(ks[5], (N_META, D_MODEL)),
        "norm_attn": 1.0 + nrm(ks[6], (DEPTH, D_MODEL), 0.05),
        "w_in": nrm(ks[7], (DEPTH, D_MODEL, IN_WIDTH), D_MODEL ** -0.5),
        "attn_sinks": nrm(ks[8], (DEPTH, N_HEADS), 0.5),
        "w_pool": nrm(ks[9], (DEPTH, N_POOL_GROUPS, POOL_GROUP_DIM, POOL_GROUP_DIM), POOL_GROUP_DIM ** -0.5),
        "pool_scale": 1.0 + nrm(ks[10], (DEPTH, POOL_WIDTH), 0.1),
        "w_out": nrm(ks[11], (DEPTH, MIX_WIDTH, D_MODEL), MIX_WIDTH ** -0.5),
        "norm_ffn": 1.0 + nrm(ks[12], (DEPTH, D_MODEL), 0.05),
        "router_w": nrm(ks[13], (DEPTH, D_MODEL, N_EXPERTS), D_MODEL ** -0.5),
        "router_b": nrm(ks[14], (DEPTH, N_EXPERTS), 0.01),
        "w_gate_up": nrm(ks[15], (DEPTH, N_EXPERTS, D_MODEL, 2 * D_EXPERT), D_MODEL ** -0.5),
        "b_gate_up": nrm(ks[16], (DEPTH, N_EXPERTS, 2 * D_EXPERT), 0.01),
        "w_down": nrm(ks[17], (DEPTH, N_EXPERTS, D_EXPERT, D_MODEL), D_EXPERT ** -0.5),
        "b_down": nrm(ks[18], (DEPTH, N_EXPERTS, D_MODEL), 0.01),
        "norm_final": 1.0 + nrm(ks[19], (D_MODEL,), 0.05),
    }


def reference(x_prompt, x_sample, cache_k, cache_v, state_pool, meta_tokens, norm_attn, w_in, attn_sinks,
              w_pool, pool_scale, w_out, norm_ffn, router_w, router_b, w_gate_up, b_gate_up, w_down, b_down,
              norm_final):
    meta = jnp.broadcast_to(meta_tokens.astype(x_prompt.dtype)[None], (x_prompt.shape[0], N_META, D_MODEL))
    xp = jnp.concatenate([meta, x_prompt], axis=1)
    xs = x_sample
    kp, vp, pp, ksm, vsm, psm = [], [], [], [], [], []
    for l in range(DEPTH):
        xp, nk, nv, npool = prompt_token_mixers(xp, attn_sinks[l], w_in[l], w_pool[l], pool_scale[l],
                                                w_out[l], norm_attn[l])
        kp.append(nk); vp.append(nv); pp.append(npool)
        xp = channel_mixer(xp, l, norm_ffn, router_w, router_b, w_gate_up, b_gate_up, w_down, b_down)
        xs, nk, nv, npool = sample_token_mixers(xs, cache_k[l], cache_v[l], state_pool[l], attn_sinks[l],
                                                w_in[l], w_pool[l], pool_scale[l], w_out[l], norm_attn[l])
        ksm.append(nk); vsm.append(nv); psm.append(npool)
        xs = channel_mixer(xs, l, norm_ffn, router_w, router_b, w_gate_up, b_gate_up, w_down, b_down)
    y_prompt = rmsnorm(xp[:, N_META:], norm_final)
    y_sample = rmsnorm(xs, norm_final)
    return (y_prompt, y_sample, jnp.stack(kp), jnp.stack(vp), jnp.stack(pp),
            jnp.stack(ksm), jnp.stack(vsm), jnp.stack(psm))
```

```python
import functools

import jax
import jax.numpy as jnp
import numpy as np
from jax import lax
from jax.experimental import pallas as pl
from jax.experimental.pallas import tpu as pltpu

F32 = jnp.float32
BF16 = jnp.bfloat16

D_MODEL = 1024
N_META = 16
N_HEADS = 8
HEAD_DIM = 64
N_KV_HEADS = 2
GQA_GROUP = N_HEADS // N_KV_HEADS
ATTN_WIDTH = N_HEADS * HEAD_DIM
KV_WIDTH = N_KV_HEADS * HEAD_DIM
WINDOW = 128
POOL_WIDTH = D_MODEL - ATTN_WIDTH
POOL_WINDOWS = (2, 4, 8, 16)
POOL_GROUP_DIM = POOL_WIDTH // len(POOL_WINDOWS)
POOL_STATE = max(POOL_WINDOWS) - 1
N_EXPERTS = 32
TOP_K = 4
D_EXPERT = D_MODEL
SWIGLU_ALPHA = 1.702
SWIGLU_LIMIT = 7.0
NORM_EPS = 1e-5
PAST_LEN = 16384

LANES = 128
QSUB = 64
KEYS_SUB = QSUB + WINDOW
META_PAD = 64
NKEY = META_PAD + KEYS_SUB
MASKED = -1e30
PROMPT_BLOCK = 512
ROUTE_BLOCK = 384
EXPERT_ROWS = 256
COMBINE_BLOCK = 512
VMEM_LIMIT = 56 * 1024 * 1024


def _rms(x, g):
    return x * lax.rsqrt(jnp.mean(x * x, axis=-1, keepdims=True) + NORM_EPS) * g


def _dup_halves(a):
    lane = lax.broadcasted_iota(jnp.int32, a.shape, 1)
    r = pltpu.roll(a, HEAD_DIM, axis=1)
    lo = lane < HEAD_DIM
    return jnp.where(lo, a, r), jnp.where(lo, r, a)


def _pool_means(pext_ref, n):
    outs = []
    for gi, w in enumerate(POOL_WINDOWS):
        xg = pext_ref[:, gi * POOL_GROUP_DIM:(gi + 1) * POOL_GROUP_DIM]
        s = xg
        sh = 1
        while sh < w:
            s = s + pltpu.roll(s, sh, axis=0)
            sh *= 2
        outs.append(s[16:] * (1.0 / w) - xg[16:])
    return outs


def _prompt_kernel(x_ref, meta_ref, gattn_ref, win_ref, wpool_ref, pscale_ref, wout_ref, gffn_ref,
                   rwt_ref, rb_ref, sink_ref, tbl_ref,
                   x1_ref, h2_ref, lgt_ref, kmeta_ref, vmeta_ref, ktail_ref, vtail_ref, ptail_ref,
                   k2buf, v2buf, km2, vm2, qbuf, obuf, pext):
    tb = x_ref.shape[0]
    pid = pl.program_id(0)

    @pl.when(pid == 0)
    def _():
        hm = _rms(meta_ref[...], gattn_ref[...]).astype(BF16)
        km = jnp.dot(hm, win_ref[:, ATTN_WIDTH:ATTN_WIDTH + KV_WIDTH], preferred_element_type=F32)
        vm = jnp.dot(hm, win_ref[:, ATTN_WIDTH + KV_WIDTH:ATTN_WIDTH + 2 * KV_WIDTH],
                     preferred_element_type=F32)
        pm = jnp.dot(hm, win_ref[:, ATTN_WIDTH + 2 * KV_WIDTH:], preferred_element_type=F32)
        kmeta_ref[...] = km
        vmeta_ref[...] = vm
        zpad = jnp.zeros((META_PAD - N_META, LANES), F32)
        k0, k1 = _dup_halves(jnp.concatenate([km, zpad], axis=0))
        v0, v1 = _dup_halves(jnp.concatenate([vm, zpad], axis=0))
        km2[0] = k0.astype(BF16)
        km2[1] = k1.astype(BF16)
        vm2[0] = v0.astype(BF16)
        vm2[1] = v1.astype(BF16)
        k2buf[:, 0:WINDOW, :] = jnp.zeros((2, WINDOW, LANES), BF16)
        v2buf[:, 0:WINDOW, :] = jnp.zeros((2, WINDOW, LANES), BF16)
        pext[0:16, :] = pm

    h = _rms(x_ref[...], gattn_ref[...]).astype(BF16)
    q = jnp.dot(h, win_ref[:, 0:ATTN_WIDTH], preferred_element_type=F32) * (HEAD_DIM ** -0.5)
    lane_t = lax.broadcasted_iota(jnp.int32, (tb, LANES), 1)
    for c in range(N_HEADS // 2):
        tile = q[:, c * LANES:(c + 1) * LANES]
        for a in range(2):
            keep = (lane_t < HEAD_DIM) if a == 0 else (lane_t >= HEAD_DIM)
            piece = jnp.where(keep, tile, 0.0).astype(BF16).reshape(tb // QSUB, QSUB, LANES)
            row = ((c % 2) * 2 + a) * QSUB
            qbuf[c // 2, :, row:row + QSUB, :] = piece
    k = jnp.dot(h, win_ref[:, ATTN_WIDTH:ATTN_WIDTH + KV_WIDTH], preferred_element_type=F32)
    v = jnp.dot(h, win_ref[:, ATTN_WIDTH + KV_WIDTH:ATTN_WIDTH + 2 * KV_WIDTH], preferred_element_type=F32)
    p = jnp.dot(h, win_ref[:, ATTN_WIDTH + 2 * KV_WIDTH:], preferred_element_type=F32)
    ktail_ref[...] = k[tb - WINDOW:]
    vtail_ref[...] = v[tb - WINDOW:]
    ptail_ref[...] = p[tb - 16:]
    k0, k1 = _dup_halves(k)
    v0, v1 = _dup_halves(v)
    k2buf[0, WINDOW:, :] = k0.astype(BF16)
    k2buf[1, WINDOW:, :] = k1.astype(BF16)
    v2buf[0, WINDOW:, :] = v0.astype(BF16)
    v2buf[1, WINDOW:, :] = v1.astype(BF16)
    pext[16:, :] = p

    lane_q = lax.broadcasted_iota(jnp.int32, (QSUB, LANES), 1)
    lo_q = lane_q < HEAD_DIM

    def sub_block(u, carry):
        r0 = pl.multiple_of(u * QSUB, QSUB)
        sel = jnp.where(jnp.logical_and(pid == 0, u < WINDOW // QSUB), u + 1, 0)
        for g in range(N_KV_HEADS):
            qm = qbuf[g, u]
            kwin = jnp.concatenate([km2[g], k2buf[g, pl.ds(r0, KEYS_SUB), :]], axis=0)
            vwin = jnp.concatenate([vm2[g], v2buf[g, pl.ds(r0, KEYS_SUB), :]], axis=0)
            s = lax.dot_general(qm, kwin, (((1,), (1,)), ((), ())), preferred_element_type=F32)
            s = s + tbl_ref[sel, g]
            sink = sink_ref[g]
            m = jnp.maximum(jnp.max(s, axis=1, keepdims=True), sink)
            e = jnp.exp(s - m)
            den = jnp.sum(e, axis=1, keepdims=True) + jnp.exp(sink - m)
            r = jnp.dot(e.astype(BF16), vwin, preferred_element_type=F32) / den
            o0 = jnp.where(lo_q, r[0:QSUB], r[QSUB:2 * QSUB])
            o1 = jnp.where(lo_q, r[2 * QSUB:3 * QSUB], r[3 * QSUB:])
            obuf[pl.ds(r0, QSUB), (2 * g) * LANES:(2 * g + 1) * LANES] = o0.astype(BF16)
            obuf[pl.ds(r0, QSUB), (2 * g + 1) * LANES:(2 * g + 2) * LANES] = o1.astype(BF16)
        return carry

    lax.fori_loop(0, tb // QSUB, sub_block, 0)

    pooled = _pool_means(pext, tb)
    for gi in range(len(POOL_WINDOWS)):
        y = jnp.dot(pooled[gi].astype(BF16), wpool_ref[gi], preferred_element_type=F32)
        y = y * pscale_ref[:, gi * POOL_GROUP_DIM:(gi + 1) * POOL_GROUP_DIM]
        obuf[:, ATTN_WIDTH + gi * POOL_GROUP_DIM:ATTN_WIDTH + (gi + 1) * POOL_GROUP_DIM] = y.astype(BF16)

    k2buf[:, 0:WINDOW, :] = k2buf[:, tb:tb + WINDOW, :]
    v2buf[:, 0:WINDOW, :] = v2buf[:, tb:tb + WINDOW, :]
    pext[0:16, :] = pext[tb:tb + 16, :]

    x1 = x_ref[...] + jnp.dot(obuf[...], wout_ref[...], preferred_element_type=F32)
    x1_ref[...] = x1
    h2 = _rms(x1, gffn_ref[...])
    h2_ref[...] = h2.astype(BF16)
    lgt_ref[...] = lax.dot_general(rwt_ref[...], h2, (((1,), (1,)), ((), ())),
                                   preferred_element_type=F32,
                                   precision=lax.Precision.HIGHEST) + rb_ref[...]


def _attn_tables(sinks):
    i = np.arange(QSUB)[:, None]
    j = np.arange(NKEY)[None, :]
    jb = j - META_PAD
    rel = i + WINDOW - jb
    band_ok = (jb >= 0) & (rel >= 0) & (rel <= WINDOW)
    meta_ok = (j < N_META) & (i >= 0)
    slopes = np.exp2(-8.0 * np.arange(1, N_HEADS + 1) / N_HEADS)
    tbl = np.empty((3, N_KV_HEADS, GQA_GROUP * QSUB, NKEY), np.float32)
    for var in range(3):
        ok = band_ok if var == 0 else band_ok & (jb >= WINDOW - (var - 1) * QSUB)
        for g in range(N_KV_HEADS):
            for a in range(GQA_GROUP):
                hd = g * GQA_GROUP + a
                bias = np.where(ok, -slopes[hd] * rel, MASKED)
                bias = np.where(meta_ok, 0.0, bias)
                tbl[var, g, a * QSUB:(a + 1) * QSUB] = bias
    sink_col = jnp.repeat(sinks.astype(F32).reshape(N_KV_HEADS, GQA_GROUP, 1), QSUB, axis=2)
    return jnp.asarray(tbl), sink_col.reshape(N_KV_HEADS, GQA_GROUP * QSUB, 1)


def _prompt_mixer(x, meta, gattn, win, wpool, pscale, wout, gffn, rwt, rb, sinks, h2_rows, lgt_cols):
    seq = x.shape[0]
    tb = PROMPT_BLOCK
    assert seq % tb == 0 and tb % WINDOW == 0
    tbl, sink_col = _attn_tables(sinks)
    full = lambda *shape: pl.BlockSpec(shape, lambda i: (0,) * len(shape))
    in_width = win.shape[1]
    return pl.pallas_call(
        _prompt_kernel,
        grid=(seq // tb,),
        in_specs=[
            pl.BlockSpec((tb, D_MODEL), lambda i: (i, 0)),
            full(N_META, D_MODEL), full(1, D_MODEL), full(D_MODEL, in_width),
            full(len(POOL_WINDOWS), POOL_GROUP_DIM, POOL_GROUP_DIM), full(1, POOL_WIDTH),
            full(D_MODEL, D_MODEL), full(1, D_MODEL), full(N_EXPERTS, D_MODEL), full(N_EXPERTS, 1),
            full(N_KV_HEADS, GQA_GROUP * QSUB, 1), full(3, N_KV_HEADS, GQA_GROUP * QSUB, NKEY),
        ],
        out_specs=[
            pl.BlockSpec((tb, D_MODEL), lambda i: (i, 0)),
            pl.BlockSpec((tb, D_MODEL), lambda i: (i, 0)),
            pl.BlockSpec((N_EXPERTS, tb), lambda i: (0, i)),
            full(N_META, KV_WIDTH), full(N_META, KV_WIDTH),
            full(WINDOW, KV_WIDTH), full(WINDOW, KV_WIDTH), full(16, POOL_WIDTH),
        ],
        out_shape=[
            jax.ShapeDtypeStruct((seq, D_MODEL), F32),
            jax.ShapeDtypeStruct((h2_rows, D_MODEL), BF16),
            jax.ShapeDtypeStruct((N_EXPERTS, lgt_cols), F32),
            jax.ShapeDtypeStruct((N_META, KV_WIDTH), F32),
            jax.ShapeDtypeStruct((N_META, KV_WIDTH), F32),
            jax.ShapeDtypeStruct((WINDOW, KV_WIDTH), F32),
            jax.ShapeDtypeStruct((WINDOW, KV_WIDTH), F32),
            jax.ShapeDtypeStruct((16, POOL_WIDTH), F32),
        ],
        scratch_shapes=[
            pltpu.VMEM((N_KV_HEADS, WINDOW + tb, LANES), BF16),
            pltpu.VMEM((N_KV_HEADS, WINDOW + tb, LANES), BF16),
            pltpu.VMEM((N_KV_HEADS, META_PAD, LANES), BF16),
            pltpu.VMEM((N_KV_HEADS, META_PAD, LANES), BF16),
            pltpu.VMEM((N_KV_HEADS, tb // QSUB, GQA_GROUP * QSUB, LANES), BF16),
            pltpu.VMEM((tb, D_MODEL), BF16),
            pltpu.VMEM((16 + tb, POOL_WIDTH), F32),
        ],
        compiler_params=pltpu.CompilerParams(dimension_semantics=("arbitrary",),
                                             vmem_limit_bytes=VMEM_LIMIT),
        name="prompt_mixer",
    )(x, meta, gattn, win, wpool, pscale, wout, gffn, rwt, rb, sink_col, tbl)


def _sample_kernel(x_ref, ck_ref, cv_ref, sp_ref, gattn_ref, win_ref, wpool_ref, pscale_ref, wout_ref,
                   gffn_ref, rwt_ref, rb_ref, sinkc_ref, bias_ref,
                   x1_ref, h2_ref, lgt_ref, knew_ref, vnew_ref, pnew_ref,
                   qm_buf, r_buf, obuf):
    nb = x_ref.shape[0]
    x = x_ref[...]
    h = _rms(x, gattn_ref[...]).astype(BF16)
    q = jnp.dot(h, win_ref[:, 0:ATTN_WIDTH], preferred_element_type=F32) * (HEAD_DIM ** -0.5)
    k = jnp.dot(h, win_ref[:, ATTN_WIDTH:ATTN_WIDTH + KV_WIDTH], preferred_element_type=F32)
    v = jnp.dot(h, win_ref[:, ATTN_WIDTH + KV_WIDTH:ATTN_WIDTH + 2 * KV_WIDTH], preferred_element_type=F32)
    p = jnp.dot(h, win_ref[:, ATTN_WIDTH + 2 * KV_WIDTH:], preferred_element_type=F32)
    knew_ref[...] = k
    vnew_ref[...] = v
    pnew_ref[...] = p

    lane = lax.broadcasted_iota(jnp.int32, (nb, LANES), 1)
    lo = lane < HEAD_DIM
    for hd in range(N_HEADS):
        tile = q[:, (hd // 2) * LANES:(hd // 2 + 1) * LANES]
        if (hd % 2) != (hd // GQA_GROUP):
            tile = pltpu.roll(tile, HEAD_DIM, axis=1)
        keep_lo = (hd // GQA_GROUP) == 0
        qm_buf[:, hd, :] = jnp.where(lo if keep_lo else jnp.logical_not(lo), tile, 0.0)

    def per_batch(b, carry):
        qm = qm_buf[b]
        kb = ck_ref[b].astype(BF16)
        vb = cv_ref[b].astype(BF16)
        s = lax.dot_general(qm.astype(BF16), kb, (((1,), (1,)), ((), ())), preferred_element_type=F32)
        s = s + bias_ref[...]
        kn = knew_ref[pl.ds(b, 1), :]
        vn = vnew_ref[pl.ds(b, 1), :]
        s_self = jnp.sum(qm * kn, axis=1, keepdims=True)
        sink = sinkc_ref[...]
        m = jnp.maximum(jnp.maximum(jnp.max(s, axis=1, keepdims=True), s_self), sink)
        e = jnp.exp(s - m)
        e_self = jnp.exp(s_self - m)
        den = jnp.sum(e, axis=1, keepdims=True) + e_self + jnp.exp(sink - m)
        r = jnp.dot(e.astype(BF16), vb, preferred_element_type=F32)
        r = r + e_self * vn
        r_buf[b] = r / den
        return carry

    lax.fori_loop(0, nb, per_batch, 0)

    for c in range(N_HEADS // 2):
        halves = []
        for a in range(2):
            hd = 2 * c + a
            t = r_buf[:, hd, :]
            if (hd // GQA_GROUP) != a:
                t = pltpu.roll(t, HEAD_DIM, axis=1)
            halves.append(t)
        obuf[:, c * LANES:(c + 1) * LANES] = jnp.where(lo, halves[0], halves[1]).astype(BF16)

    for gi, w in enumerate(POOL_WINDOWS):
        cols = slice(gi * POOL_GROUP_DIM, (gi + 1) * POOL_GROUP_DIM)
        pg = p[:, cols]
        acc = pg
        for d in range(1, w):
            acc = acc + sp_ref[:, POOL_STATE - d, cols]
        pooled = acc * (1.0 / w) - pg
        y = jnp.dot(pooled.astype(BF16), wpool_ref[gi], preferred_element_type=F32) * pscale_ref[:, cols]
        obuf[:, ATTN_WIDTH + gi * POOL_GROUP_DIM:ATTN_WIDTH + (gi + 1) * POOL_GROUP_DIM] = y.astype(BF16)

    x1 = x + jnp.dot(obuf[...], wout_ref[...], preferred_element_type=F32)
    x1_ref[...] = x1
    h2 = _rms(x1, gffn_ref[...])
    h2_ref[...] = h2.astype(BF16)
    lgt_ref[...] = lax.dot_general(rwt_ref[...], h2, (((1,), (1,)), ((), ())),
                                   preferred_element_type=F32,
                                   precision=lax.Precision.HIGHEST) + rb_ref[...]


def _sample_mixer(x, ck, cv, sp, gattn, win, wpool, pscale, wout, gffn, rwt, rb, sinks):
    nb = x.shape[0]
    rows = ck.shape[1]
    slopes = np.exp2(-8.0 * np.arange(1, N_HEADS + 1) / N_HEADS)
    dist = np.concatenate([np.zeros(N_META), WINDOW - np.arange(WINDOW)])
    bias = jnp.asarray((-slopes[:, None] * dist[None, :]).astype(np.float32))
    vm = pl.BlockSpec(memory_space=pltpu.VMEM)
    return pl.pallas_call(
        _sample_kernel,
        in_specs=[vm] * 14,
        out_specs=[vm] * 6,
        out_shape=[
            jax.ShapeDtypeStruct((nb, D_MODEL), F32),
            jax.ShapeDtypeStruct((nb, D_MODEL), BF16),
            jax.ShapeDtypeStruct((N_EXPERTS, nb), F32),
            jax.ShapeDtypeStruct((nb, KV_WIDTH), F32),
            jax.ShapeDtypeStruct((nb, KV_WIDTH), F32),
            jax.ShapeDtypeStruct((nb, POOL_WIDTH), F32),
        ],
        scratch_shapes=[
            pltpu.VMEM((nb, N_HEADS, LANES), F32),
            pltpu.VMEM((nb, N_HEADS, LANES), F32),
            pltpu.VMEM((nb, D_MODEL), BF16),
        ],
        compiler_params=pltpu.CompilerParams(vmem_limit_bytes=VMEM_LIMIT),
        name="sample_mixer",
    )(x, ck, cv, sp, gattn, win, wpool, pscale, wout, gffn, rwt, rb,
      sinks.astype(F32).reshape(N_HEADS, 1), bias)


def _router_kernel(lg_ref, tri_ref, eidx_ref, rank_ref, gate_ref, cnt_ref, carry):
    tr = lg_ref.shape[1]

    @pl.when(pl.program_id(0) == 0)
    def _():
        carry[...] = jnp.zeros_like(carry)

    work = lg_ref[...]
    eio = lax.broadcasted_iota(jnp.int32, work.shape, 0).astype(F32)
    sels, vals, idxs = [], [], []
    for _k in range(TOP_K):
        mx = jnp.max(work, axis=0, keepdims=True)
        idx = jnp.min(jnp.where(work == mx, eio, float(N_EXPERTS)), axis=0, keepdims=True)
        sel = eio == idx
        sels.append(sel)
        vals.append(mx)
        idxs.append(idx)
        work = jnp.where(sel, -jnp.inf, work)
    exps = [jnp.exp(vk - vals[0]) for vk in vals]
    tot = exps[0] + exps[1] + exps[2] + exps[3]
    onehot = jnp.zeros(work.shape, F32)
    for sel in sels:
        onehot = onehot + sel.astype(F32)
    before = jnp.dot(onehot.astype(BF16), tri_ref[...], preferred_element_type=F32) + carry[...]
    for kk in range(TOP_K):
        eidx_ref[pl.ds(kk, 1), :] = idxs[kk].astype(jnp.int32)
        gate_ref[pl.ds(kk, 1), :] = exps[kk] / tot
        rk = jnp.sum(jnp.where(sels[kk], before, 0.0), axis=0, keepdims=True)
        rank_ref[pl.ds(kk, 1), :] = rk.astype(jnp.int32)
    carry[...] = carry[...] + jnp.sum(onehot, axis=1, keepdims=True)
    cnt_ref[...] = carry[...].astype(jnp.int32)


def _router(logits_t):
    n = logits_t.shape[1]
    tr = ROUTE_BLOCK
    assert n % tr == 0
    tri = jnp.asarray(np.triu(np.ones((tr, tr), np.float32), k=1), BF16)
    return pl.pallas_call(
        _router_kernel,
        grid=(n // tr,),
        in_specs=[pl.BlockSpec((N_EXPERTS, tr), lambda i: (0, i)),
                  pl.BlockSpec((tr, tr), lambda i: (0, 0))],
        out_specs=[pl.BlockSpec((TOP_K, tr), lambda i: (0, i)),
                   pl.BlockSpec((TOP_K, tr), lambda i: (0, i)),
                   pl.BlockSpec((TOP_K, tr), lambda i: (0, i)),
                   pl.BlockSpec((N_EXPERTS, 1), lambda i: (0, 0))],
        out_shape=[jax.ShapeDtypeStruct((TOP_K, n), jnp.int32),
                   jax.ShapeDtypeStruct((TOP_K, n), jnp.int32),
                   jax.ShapeDtypeStruct((TOP_K, n), F32),
                   jax.ShapeDtypeStruct((N_EXPERTS, 1), jnp.int32)],
        scratch_shapes=[pltpu.VMEM((N_EXPERTS, 1), F32)],
        compiler_params=pltpu.CompilerParams(dimension_semantics=("arbitrary",)),
        name="router",
    )(logits_t, tri)


def _expert_kernel(bexp_ref, nused_ref, x_ref, wgu_ref, bgu_ref, wd_ref, bd_ref, y_ref, wgu_bf, wd_bf):
    i = pl.program_id(0)
    prev = bexp_ref[jnp.maximum(i - 1, 0)]
    fresh = jnp.logical_or(i == 0, bexp_ref[i] != prev)

    @pl.when(jnp.logical_and(fresh, i < nused_ref[0]))
    def _():
        chunk = 32

        def cast_gu(r, c):
            r0 = pl.multiple_of(r * chunk, chunk)
            wgu_bf[pl.ds(r0, chunk), :] = wgu_ref[0, pl.ds(r0, chunk), :].astype(BF16)
            return c

        def cast_d(r, c):
            r0 = pl.multiple_of(r * chunk, chunk)
            wd_bf[pl.ds(r0, chunk), :] = wd_ref[0, pl.ds(r0, chunk), :].astype(BF16)
            return c

        lax.fori_loop(0, D_MODEL // chunk, cast_gu, 0)
        lax.fori_loop(0, D_EXPERT // chunk, cast_d, 0)

    @pl.when(i < nused_ref[0])
    def _():
        x = x_ref[...]
        g = jnp.dot(x, wgu_bf[:, 0:D_EXPERT], preferred_element_type=F32) + bgu_ref[0, :, 0:D_EXPERT]
        u = jnp.dot(x, wgu_bf[:, D_EXPERT:], preferred_element_type=F32) + bgu_ref[0, :, D_EXPERT:]
        g = jnp.minimum(g, SWIGLU_LIMIT)
        u = jnp.clip(u, -SWIGLU_LIMIT, SWIGLU_LIMIT)
        act = g * (1.0 / (1.0 + jnp.exp(-SWIGLU_ALPHA * g))) * (u + 1.0)
        y_ref[...] = jnp.dot(act.astype(BF16), wd_bf[...], preferred_element_type=F32) + bd_ref[0]


def _experts(xs, block_expert, n_used, wgu, bgu, wd, bd):
    cap = xs.shape[0]
    rb = EXPERT_ROWS
    nblk = cap // rb
    grid_spec = pltpu.PrefetchScalarGridSpec(
        num_scalar_prefetch=2,
        grid=(nblk,),
        in_specs=[
            pl.BlockSpec((rb, D_MODEL), lambda i, be, nu: (i, 0)),
            pl.BlockSpec((1, D_MODEL, 2 * D_EXPERT), lambda i, be, nu: (be[i], 0, 0)),
            pl.BlockSpec((1, 1, 2 * D_EXPERT), lambda i, be, nu: (be[i], 0, 0)),
            pl.BlockSpec((1, D_EXPERT, D_MODEL), lambda i, be, nu: (be[i], 0, 0)),
            pl.BlockSpec((1, 1, D_MODEL), lambda i, be, nu: (be[i], 0, 0)),
        ],
        out_specs=pl.BlockSpec((rb, D_MODEL), lambda i, be, nu: (i, 0)),
        scratch_shapes=[pltpu.VMEM((D_MODEL, 2 * D_EXPERT), BF16),
                        pltpu.VMEM((D_EXPERT, D_MODEL), BF16)],
    )
    return pl.pallas_call(
        _expert_kernel,
        grid_spec=grid_spec,
        out_shape=jax.ShapeDtypeStruct((cap, D_MODEL), F32),
        compiler_params=pltpu.CompilerParams(dimension_semantics=("arbitrary",),
                                             vmem_limit_bytes=VMEM_LIMIT),
        name="experts",
    )(block_expert, n_used, xs, wgu, bgu.reshape(N_EXPERTS, 1, 2 * D_EXPERT), wd,
      bd.reshape(N_EXPERTS, 1, D_MODEL))


def _combine_kernel(x1_ref, yg_ref, gate_ref, gfin_ref, out_ref):
    acc = x1_ref[...]
    for kk in range(TOP_K):
        acc = acc + yg_ref[kk] * gate_ref[:, kk:kk + 1]
    out_ref[...] = _rms(acc, gfin_ref[...])


def _combine(x1, yg, gates_t, gfin, block):
    n = x1.shape[0]
    assert n % block == 0
    return pl.pallas_call(
        _combine_kernel,
        grid=(n // block,),
        in_specs=[pl.BlockSpec((block, D_MODEL), lambda i: (i, 0)),
                  pl.BlockSpec((TOP_K, block, D_MODEL), lambda i: (0, i, 0)),
                  pl.BlockSpec((block, TOP_K), lambda i: (i, 0)),
                  pl.BlockSpec((1, D_MODEL), lambda i: (0, 0))],
        out_specs=pl.BlockSpec((block, D_MODEL), lambda i: (i, 0)),
        out_shape=jax.ShapeDtypeStruct((n, D_MODEL), F32),
        compiler_params=pltpu.CompilerParams(dimension_semantics=("arbitrary",),
                                             vmem_limit_bytes=VMEM_LIMIT),
        name="combine",
    )(x1, yg, gates_t, gfin)


def kernel(x_prompt, x_sample, cache_k, cache_v, state_pool, meta_tokens, norm_attn, w_in, attn_sinks,
           w_pool, pool_scale, w_out, norm_ffn, router_w, router_b, w_gate_up, b_gate_up, w_down, b_down,
           norm_final):
    assert w_in.shape[0] == 1, "single-layer trunk"
    bsz, seq, _ = x_prompt.shape
    assert bsz == 1
    nb = x_sample.shape[0]
    n_tok = seq + nb
    gattn = norm_attn[0].reshape(1, D_MODEL)
    gffn = norm_ffn[0].reshape(1, D_MODEL)
    win = w_in[0].astype(BF16)
    wpool = w_pool[0].astype(BF16)
    wout = w_out[0].astype(BF16)
    pscale = pool_scale[0].reshape(1, POOL_WIDTH)
    rwt = router_w[0].T
    rb = router_b[0].reshape(N_EXPERTS, 1)
    sinks = attn_sinks[0]

    h2_rows = n_tok + 8
    (x1_p, h2_all, lgt_all, kmeta, vmeta, ktail, vtail, ptail) = _prompt_mixer(
        x_prompt[0], meta_tokens, gattn, win, wpool, pscale, wout, gffn, rwt, rb, sinks, h2_rows, n_tok)
    ck = cache_k[0].reshape(nb, N_META + WINDOW, KV_WIDTH)
    cv = cache_v[0].reshape(nb, N_META + WINDOW, KV_WIDTH)
    (x1_s, h2_s, lgt_s, knew, vnew, pnew) = _sample_mixer(
        x_sample[:, 0], ck, cv, state_pool[0], gattn, win, wpool, pscale, wout, gffn, rwt, rb, sinks)
    h2_all = lax.dynamic_update_slice(h2_all, jnp.concatenate([h2_s, jnp.zeros((8, D_MODEL), BF16)]),
                                      (seq, 0))
    lgt_all = lax.dynamic_update_slice(lgt_all, lgt_s, (0, seq))

    eidx, rank, gates, counts = _router(lgt_all)
    counts = counts[:, 0]
    rbk = EXPERT_ROWS
    padded = (counts + rbk - 1) // rbk * rbk
    pad_end = jnp.cumsum(padded)
    pad_start = pad_end - padded
    dest = pad_start[eidx] + rank
    nblk = -(-(n_tok * TOP_K) // rbk) + N_EXPERTS
    cap = nblk * rbk
    tok = jnp.broadcast_to(jnp.arange(n_tok, dtype=jnp.int32)[None], (TOP_K, n_tok))
    slot_tok = jnp.full((cap,), n_tok, jnp.int32).at[dest.reshape(-1)].set(
        tok.reshape(-1), unique_indices=True)
    block_start = jnp.arange(nblk, dtype=jnp.int32) * rbk
    block_expert = jnp.minimum(jnp.searchsorted(pad_end, block_start, side="right"),
                               N_EXPERTS - 1).astype(jnp.int32)
    n_used = (pad_end[-1] // rbk).astype(jnp.int32).reshape(1)
    last_e = block_expert[jnp.maximum(n_used[0] - 1, 0)]
    block_expert = jnp.where(block_start // rbk < n_used[0], block_expert, last_e)

    xs = jnp.take(h2_all, slot_tok, axis=0)
    yb = _experts(xs, block_expert, n_used, w_gate_up[0], b_gate_up[0], w_down[0], b_down[0])

    gfin = norm_final.reshape(1, D_MODEL)
    yg_p = jnp.take(yb, dest[:, :seq], axis=0)
    yg_s = jnp.take(yb, dest[:, seq:], axis=0)
    y_prompt = _combine(x1_p, yg_p, gates[:, :seq].T, gfin, COMBINE_BLOCK)
    y_sample = _combine(x1_s, yg_s, gates[:, seq:].T, gfin, nb)

    kv_shape = (1, 1, N_META + WINDOW, N_KV_HEADS, HEAD_DIM)
    new_k_p = jnp.concatenate([kmeta, ktail], axis=0).reshape(kv_shape)
    new_v_p = jnp.concatenate([vmeta, vtail], axis=0).reshape(kv_shape)
    new_pool_p = ptail[16 - POOL_STATE:].reshape(1, 1, POOL_STATE, POOL_WIDTH)
    new_k_s = jnp.concatenate([ck[:, :N_META], ck[:, N_META + 1:], knew[:, None]], axis=1).reshape(
        (1, nb, N_META + WINDOW, N_KV_HEADS, HEAD_DIM))
    new_v_s = jnp.concatenate([cv[:, :N_META], cv[:, N_META + 1:], vnew[:, None]], axis=1).reshape(
        (1, nb, N_META + WINDOW, N_KV_HEADS, HEAD_DIM))
    new_pool_s = jnp.concatenate([state_pool[0][:, 1:], pnew[:, None]], axis=1)[None]
    return (y_prompt[None], y_sample[:, None], new_k_p, new_v_p, new_pool_p, new_k_s, new_v_s, new_pool_s)
```

```python
import functools

import jax
import jax.numpy as jnp
import numpy as np
from jax import lax
from jax.experimental import pallas as pl
from jax.experimental.pallas import tpu as pltpu

F32 = jnp.float32
BF16 = jnp.bfloat16

D_MODEL = 1024
N_META = 16
N_HEADS = 8
HEAD_DIM = 64
N_KV_HEADS = 2
GQA_GROUP = N_HEADS // N_KV_HEADS
ATTN_WIDTH = N_HEADS * HEAD_DIM
KV_WIDTH = N_KV_HEADS * HEAD_DIM
WINDOW = 128
POOL_WIDTH = D_MODEL - ATTN_WIDTH
POOL_WINDOWS = (2, 4, 8, 16)
POOL_GROUP_DIM = POOL_WIDTH // len(POOL_WINDOWS)
POOL_STATE = max(POOL_WINDOWS) - 1
N_EXPERTS = 32
TOP_K = 4
D_EXPERT = D_MODEL
SWIGLU_ALPHA = 1.702
SWIGLU_LIMIT = 7.0
NORM_EPS = 1e-5
PAST_LEN = 16384

LANES = 128
QSUB = 64
KEYS_SUB = QSUB + WINDOW
META_PAD = 64
NKEY = META_PAD + KEYS_SUB
MASKED = -1e30
PROMPT_BLOCK = 512
ROUTE_BLOCK = 384
EXPERT_ROWS = 256
COMBINE_TILE = 128
WIN = 32
WIN_ALIGN = 16
WIN_ROWS = WIN + WIN_ALIGN
VMEM_LIMIT = 56 * 1024 * 1024


def _rms(x, g):
    return x * lax.rsqrt(jnp.mean(x * x, axis=-1, keepdims=True) + NORM_EPS) * g


def _dup_halves(a):
    lane = lax.broadcasted_iota(jnp.int32, a.shape, 1)
    r = pltpu.roll(a, HEAD_DIM, axis=1)
    lo = lane < HEAD_DIM
    return jnp.where(lo, a, r), jnp.where(lo, r, a)


def _pool_means(pext_ref, n):
    outs = []
    for gi, w in enumerate(POOL_WINDOWS):
        xg = pext_ref[:, gi * POOL_GROUP_DIM:(gi + 1) * POOL_GROUP_DIM]
        s = xg
        sh = 1
        while sh < w:
            s = s + pltpu.roll(s, sh, axis=0)
            sh *= 2
        outs.append(s[16:] * (1.0 / w) - xg[16:])
    return outs


def _prompt_kernel(x_ref, meta_ref, gattn_ref, win_ref, wpool_ref, pscale_ref, wout_ref, gffn_ref,
                   rwt_ref, rb_ref, sink_ref, tbl_ref,
                   x1_ref, h2_ref, lgt_ref, kmeta_ref, vmeta_ref, ktail_ref, vtail_ref, ptail_ref,
                   k2buf, v2buf, km2, vm2, qbuf, obuf, pext):
    tb = x_ref.shape[0]
    pid = pl.program_id(0)

    @pl.when(pid == 0)
    def _():
        hm = _rms(meta_ref[...], gattn_ref[...]).astype(BF16)
        km = jnp.dot(hm, win_ref[:, ATTN_WIDTH:ATTN_WIDTH + KV_WIDTH], preferred_element_type=F32)
        vm = jnp.dot(hm, win_ref[:, ATTN_WIDTH + KV_WIDTH:ATTN_WIDTH + 2 * KV_WIDTH],
                     preferred_element_type=F32)
        pm = jnp.dot(hm, win_ref[:, ATTN_WIDTH + 2 * KV_WIDTH:], preferred_element_type=F32)
        kmeta_ref[...] = km
        vmeta_ref[...] = vm
        zpad = jnp.zeros((META_PAD - N_META, LANES), F32)
        k0, k1 = _dup_halves(jnp.concatenate([km, zpad], axis=0))
        v0, v1 = _dup_halves(jnp.concatenate([vm, zpad], axis=0))
        km2[0] = k0.astype(BF16)
        km2[1] = k1.astype(BF16)
        vm2[0] = v0.astype(BF16)
        vm2[1] = v1.astype(BF16)
        k2buf[:, 0:WINDOW, :] = jnp.zeros((2, WINDOW, LANES), BF16)
        v2buf[:, 0:WINDOW, :] = jnp.zeros((2, WINDOW, LANES), BF16)
        pext[0:16, :] = pm

    h = _rms(x_ref[...], gattn_ref[...]).astype(BF16)
    q = jnp.dot(h, win_ref[:, 0:ATTN_WIDTH], preferred_element_type=F32) * (HEAD_DIM ** -0.5)
    lane_t = lax.broadcasted_iota(jnp.int32, (tb, LANES), 1)
    for c in range(N_HEADS // 2):
        tile = q[:, c * LANES:(c + 1) * LANES]
        for a in range(2):
            keep = (lane_t < HEAD_DIM) if a == 0 else (lane_t >= HEAD_DIM)
            piece = jnp.where(keep, tile, 0.0).astype(BF16).reshape(tb // QSUB, QSUB, LANES)
            row = ((c % 2) * 2 + a) * QSUB
            qbuf[c // 2, :, row:row + QSUB, :] = piece
    k = jnp.dot(h, win_ref[:, ATTN_WIDTH:ATTN_WIDTH + KV_WIDTH], preferred_element_type=F32)
    v = jnp.dot(h, win_ref[:, ATTN_WIDTH + KV_WIDTH:ATTN_WIDTH + 2 * KV_WIDTH], preferred_element_type=F32)
    p = jnp.dot(h, win_ref[:, ATTN_WIDTH + 2 * KV_WIDTH:], preferred_element_type=F32)
    ktail_ref[...] = k[tb - WINDOW:]
    vtail_ref[...] = v[tb - WINDOW:]
    ptail_ref[...] = p[tb - 16:]
    k0, k1 = _dup_halves(k)
    v0, v1 = _dup_halves(v)
    k2buf[0, WINDOW:, :] = k0.astype(BF16)
    k2buf[1, WINDOW:, :] = k1.astype(BF16)
    v2buf[0, WINDOW:, :] = v0.astype(BF16)
    v2buf[1, WINDOW:, :] = v1.astype(BF16)
    pext[16:, :] = p

    lane_q = lax.broadcasted_iota(jnp.int32, (QSUB, LANES), 1)
    lo_q = lane_q < HEAD_DIM

    def sub_block(u, carry):
        r0 = pl.multiple_of(u * QSUB, QSUB)
        sel = jnp.where(jnp.logical_and(pid == 0, u < WINDOW // QSUB), u + 1, 0)
        for g in range(N_KV_HEADS):
            qm = qbuf[g, u]
            kwin = jnp.concatenate([km2[g], k2buf[g, pl.ds(r0, KEYS_SUB), :]], axis=0)
            vwin = jnp.concatenate([vm2[g], v2buf[g, pl.ds(r0, KEYS_SUB), :]], axis=0)
            s = lax.dot_general(qm, kwin, (((1,), (1,)), ((), ())), preferred_element_type=F32)
            s = s + tbl_ref[sel, g]
            sink = sink_ref[g]
            m = jnp.maximum(jnp.max(s, axis=1, keepdims=True), sink)
            e = jnp.exp(s - m)
            den = jnp.sum(e, axis=1, keepdims=True) + jnp.exp(sink - m)
            r = jnp.dot(e.astype(BF16), vwin, preferred_element_type=F32) / den
            o0 = jnp.where(lo_q, r[0:QSUB], r[QSUB:2 * QSUB])
            o1 = jnp.where(lo_q, r[2 * QSUB:3 * QSUB], r[3 * QSUB:])
            obuf[pl.ds(r0, QSUB), (2 * g) * LANES:(2 * g + 1) * LANES] = o0.astype(BF16)
            obuf[pl.ds(r0, QSUB), (2 * g + 1) * LANES:(2 * g + 2) * LANES] = o1.astype(BF16)
        return carry

    lax.fori_loop(0, tb // QSUB, sub_block, 0)

    pooled = _pool_means(pext, tb)
    for gi in range(len(POOL_WINDOWS)):
        y = jnp.dot(pooled[gi].astype(BF16), wpool_ref[gi], preferred_element_type=F32)
        y = y * pscale_ref[:, gi * POOL_GROUP_DIM:(gi + 1) * POOL_GROUP_DIM]
        obuf[:, ATTN_WIDTH + gi * POOL_GROUP_DIM:ATTN_WIDTH + (gi + 1) * POOL_GROUP_DIM] = y.astype(BF16)

    k2buf[:, 0:WINDOW, :] = k2buf[:, tb:tb + WINDOW, :]
    v2buf[:, 0:WINDOW, :] = v2buf[:, tb:tb + WINDOW, :]
    pext[0:16, :] = pext[tb:tb + 16, :]

    x1 = x_ref[...] + jnp.dot(obuf[...], wout_ref[...], preferred_element_type=F32)
    x1_ref[...] = x1
    h2 = _rms(x1, gffn_ref[...])
    h2_ref[...] = h2.astype(BF16)
    lgt_ref[...] = lax.dot_general(rwt_ref[...], h2, (((1,), (1,)), ((), ())),
                                   preferred_element_type=F32,
                                   precision=lax.Precision.HIGHEST) + rb_ref[...]


def _attn_tables(sinks):
    i = np.arange(QSUB)[:, None]
    j = np.arange(NKEY)[None, :]
    jb = j - META_PAD
    rel = i + WINDOW - jb
    band_ok = (jb >= 0) & (rel >= 0) & (rel <= WINDOW)
    meta_ok = (j < N_META) & (i >= 0)
    slopes = np.exp2(-8.0 * np.arange(1, N_HEADS + 1) / N_HEADS)
    tbl = np.empty((3, N_KV_HEADS, GQA_GROUP * QSUB, NKEY), np.float32)
    for var in range(3):
        ok = band_ok if var == 0 else band_ok & (jb >= WINDOW - (var - 1) * QSUB)
        for g in range(N_KV_HEADS):
            for a in range(GQA_GROUP):
                hd = g * GQA_GROUP + a
                bias = np.where(ok, -slopes[hd] * rel, MASKED)
                bias = np.where(meta_ok, 0.0, bias)
                tbl[var, g, a * QSUB:(a + 1) * QSUB] = bias
    sink_col = jnp.repeat(sinks.astype(F32).reshape(N_KV_HEADS, GQA_GROUP, 1), QSUB, axis=2)
    return jnp.asarray(tbl), sink_col.reshape(N_KV_HEADS, GQA_GROUP * QSUB, 1)


def _prompt_mixer(x, meta, gattn, win, wpool, pscale, wout, gffn, rwt, rb, sinks, h2_rows, lgt_cols):
    seq = x.shape[0]
    tb = PROMPT_BLOCK
    assert seq % tb == 0 and tb % WINDOW == 0
    tbl, sink_col = _attn_tables(sinks)
    full = lambda *shape: pl.BlockSpec(shape, lambda i: (0,) * len(shape))
    in_width = win.shape[1]
    return pl.pallas_call(
        _prompt_kernel,
        grid=(seq // tb,),
        in_specs=[
            pl.BlockSpec((tb, D_MODEL), lambda i: (i, 0)),
            full(N_META, D_MODEL), full(1, D_MODEL), full(D_MODEL, in_width),
            full(len(POOL_WINDOWS), POOL_GROUP_DIM, POOL_GROUP_DIM), full(1, POOL_WIDTH),
            full(D_MODEL, D_MODEL), full(1, D_MODEL), full(N_EXPERTS, D_MODEL), full(N_EXPERTS, 1),
            full(N_KV_HEADS, GQA_GROUP * QSUB, 1), full(3, N_KV_HEADS, GQA_GROUP * QSUB, NKEY),
        ],
        out_specs=[
            pl.BlockSpec((tb, D_MODEL), lambda i: (i, 0)),
            pl.BlockSpec((tb, D_MODEL), lambda i: (i, 0)),
            pl.BlockSpec((N_EXPERTS, tb), lambda i: (0, i)),
            full(N_META, KV_WIDTH), full(N_META, KV_WIDTH),
            full(WINDOW, KV_WIDTH), full(WINDOW, KV_WIDTH), full(16, POOL_WIDTH),
        ],
        out_shape=[
            jax.ShapeDtypeStruct((seq, D_MODEL), F32),
            jax.ShapeDtypeStruct((h2_rows, D_MODEL), BF16),
            jax.ShapeDtypeStruct((N_EXPERTS, lgt_cols), F32),
            jax.ShapeDtypeStruct((N_META, KV_WIDTH), F32),
            jax.ShapeDtypeStruct((N_META, KV_WIDTH), F32),
            jax.ShapeDtypeStruct((WINDOW, KV_WIDTH), F32),
            jax.ShapeDtypeStruct((WINDOW, KV_WIDTH), F32),
            jax.ShapeDtypeStruct((16, POOL_WIDTH), F32),
        ],
        scratch_shapes=[
            pltpu.VMEM((N_KV_HEADS, WINDOW + tb, LANES), BF16),
            pltpu.VMEM((N_KV_HEADS, WINDOW + tb, LANES), BF16),
            pltpu.VMEM((N_KV_HEADS, META_PAD, LANES), BF16),
            pltpu.VMEM((N_KV_HEADS, META_PAD, LANES), BF16),
            pltpu.VMEM((N_KV_HEADS, tb // QSUB, GQA_GROUP * QSUB, LANES), BF16),
            pltpu.VMEM((tb, D_MODEL), BF16),
            pltpu.VMEM((16 + tb, POOL_WIDTH), F32),
        ],
        compiler_params=pltpu.CompilerParams(dimension_semantics=("arbitrary",),
                                             vmem_limit_bytes=VMEM_LIMIT),
        name="prompt_mixer",
    )(x, meta, gattn, win, wpool, pscale, wout, gffn, rwt, rb, sink_col, tbl)


def _sample_kernel(x_ref, ck_ref, cv_ref, sp_ref, gattn_ref, win_ref, wpool_ref, pscale_ref, wout_ref,
                   gffn_ref, rwt_ref, rb_ref, sinkc_ref, bias_ref,
                   x1_ref, h2_ref, lgt_ref, knew_ref, vnew_ref, pnew_ref,
                   qm_buf, r_buf, obuf):
    nb = x_ref.shape[0]
    x = x_ref[...]
    h = _rms(x, gattn_ref[...]).astype(BF16)
    q = jnp.dot(h, win_ref[:, 0:ATTN_WIDTH], preferred_element_type=F32) * (HEAD_DIM ** -0.5)
    k = jnp.dot(h, win_ref[:, ATTN_WIDTH:ATTN_WIDTH + KV_WIDTH], preferred_element_type=F32)
    v = jnp.dot(h, win_ref[:, ATTN_WIDTH + KV_WIDTH:ATTN_WIDTH + 2 * KV_WIDTH], preferred_element_type=F32)
    p = jnp.dot(h, win_ref[:, ATTN_WIDTH + 2 * KV_WIDTH:], preferred_element_type=F32)
    knew_ref[...] = k
    vnew_ref[...] = v
    pnew_ref[...] = p

    lane = lax.broadcasted_iota(jnp.int32, (nb, LANES), 1)
    lo = lane < HEAD_DIM
    for hd in range(N_HEADS):
        tile = q[:, (hd // 2) * LANES:(hd // 2 + 1) * LANES]
        if (hd % 2) != (hd // GQA_GROUP):
            tile = pltpu.roll(tile, HEAD_DIM, axis=1)
        keep_lo = (hd // GQA_GROUP) == 0
        qm_buf[:, hd, :] = jnp.where(lo if keep_lo else jnp.logical_not(lo), tile, 0.0)

    def per_batch(b, carry):
        qm = qm_buf[b]
        kb = ck_ref[b].astype(BF16)
        vb = cv_ref[b].astype(BF16)
        s = lax.dot_general(qm.astype(BF16), kb, (((1,), (1,)), ((), ())), preferred_element_type=F32)
        s = s + bias_ref[...]
        kn = knew_ref[pl.ds(b, 1), :]
        vn = vnew_ref[pl.ds(b, 1), :]
        s_self = jnp.sum(qm * kn, axis=1, keepdims=True)
        sink = sinkc_ref[...]
        m = jnp.maximum(jnp.maximum(jnp.max(s, axis=1, keepdims=True), s_self), sink)
        e = jnp.exp(s - m)
        e_self = jnp.exp(s_self - m)
        den = jnp.sum(e, axis=1, keepdims=True) + e_self + jnp.exp(sink - m)
        r = jnp.dot(e.astype(BF16), vb, preferred_element_type=F32)
        r = r + e_self * vn
        r_buf[b] = r / den
        return carry

    lax.fori_loop(0, nb, per_batch, 0)

    for c in range(N_HEADS // 2):
        halves = []
        for a in range(2):
            hd = 2 * c + a
            t = r_buf[:, hd, :]
            if (hd // GQA_GROUP) != a:
                t = pltpu.roll(t, HEAD_DIM, axis=1)
            halves.append(t)
        obuf[:, c * LANES:(c + 1) * LANES] = jnp.where(lo, halves[0], halves[1]).astype(BF16)

    for gi, w in enumerate(POOL_WINDOWS):
        cols = slice(gi * POOL_GROUP_DIM, (gi + 1) * POOL_GROUP_DIM)
        pg = p[:, cols]
        acc = pg
        for d in range(1, w):
            acc = acc + sp_ref[:, POOL_STATE - d, cols]
        pooled = acc * (1.0 / w) - pg
        y = jnp.dot(pooled.astype(BF16), wpool_ref[gi], preferred_element_type=F32) * pscale_ref[:, cols]
        obuf[:, ATTN_WIDTH + gi * POOL_GROUP_DIM:ATTN_WIDTH + (gi + 1) * POOL_GROUP_DIM] = y.astype(BF16)

    x1 = x + jnp.dot(obuf[...], wout_ref[...], preferred_element_type=F32)
    x1_ref[...] = x1
    h2 = _rms(x1, gffn_ref[...])
    h2_ref[...] = h2.astype(BF16)
    lgt_ref[...] = lax.dot_general(rwt_ref[...], h2, (((1,), (1,)), ((), ())),
                                   preferred_element_type=F32,
                                   precision=lax.Precision.HIGHEST) + rb_ref[...]


def _sample_mixer(x, ck, cv, sp, gattn, win, wpool, pscale, wout, gffn, rwt, rb, sinks):
    nb = x.shape[0]
    rows = ck.shape[1]
    slopes = np.exp2(-8.0 * np.arange(1, N_HEADS + 1) / N_HEADS)
    dist = np.concatenate([np.zeros(N_META), WINDOW - np.arange(WINDOW)])
    bias = jnp.asarray((-slopes[:, None] * dist[None, :]).astype(np.float32))
    vm = pl.BlockSpec(memory_space=pltpu.VMEM)
    return pl.pallas_call(
        _sample_kernel,
        in_specs=[vm] * 14,
        out_specs=[vm] * 6,
        out_shape=[
            jax.ShapeDtypeStruct((nb, D_MODEL), F32),
            jax.ShapeDtypeStruct((nb, D_MODEL), BF16),
            jax.ShapeDtypeStruct((N_EXPERTS, nb), F32),
            jax.ShapeDtypeStruct((nb, KV_WIDTH), F32),
            jax.ShapeDtypeStruct((nb, KV_WIDTH), F32),
            jax.ShapeDtypeStruct((nb, POOL_WIDTH), F32),
        ],
        scratch_shapes=[
            pltpu.VMEM((nb, N_HEADS, LANES), F32),
            pltpu.VMEM((nb, N_HEADS, LANES), F32),
            pltpu.VMEM((nb, D_MODEL), BF16),
        ],
        compiler_params=pltpu.CompilerParams(vmem_limit_bytes=VMEM_LIMIT),
        name="sample_mixer",
    )(x, ck, cv, sp, gattn, win, wpool, pscale, wout, gffn, rwt, rb,
      sinks.astype(F32).reshape(N_HEADS, 1), bias)


def _router_kernel(lg_ref, tri_ref, eidx_ref, rank_ref, gate_ref, col_ref, lrank_ref, tcar_ref, cnt_ref,
                   carry):
    tr = lg_ref.shape[1]

    @pl.when(pl.program_id(0) == 0)
    def _():
        carry[...] = jnp.zeros_like(carry)

    work = lg_ref[...]
    eio = lax.broadcasted_iota(jnp.int32, work.shape, 0).astype(F32)
    sels, vals, idxs = [], [], []
    for _k in range(TOP_K):
        mx = jnp.max(work, axis=0, keepdims=True)
        idx = jnp.min(jnp.where(work == mx, eio, float(N_EXPERTS)), axis=0, keepdims=True)
        sel = eio == idx
        sels.append(sel)
        vals.append(mx)
        idxs.append(idx)
        work = jnp.where(sel, -jnp.inf, work)
    exps = [jnp.exp(vk - vals[0]) for vk in vals]
    tot = exps[0] + exps[1] + exps[2] + exps[3]
    onehot = jnp.zeros(work.shape, F32)
    for sel in sels:
        onehot = onehot + sel.astype(F32)
    before = jnp.dot(onehot.astype(BF16), tri_ref[...], preferred_element_type=F32) + carry[...]
    for kk in range(TOP_K):
        eidx_ref[pl.ds(kk, 1), :] = idxs[kk].astype(jnp.int32)
        gate_ref[pl.ds(kk, 1), :] = exps[kk] / tot
        rk = jnp.sum(jnp.where(sels[kk], before, 0.0), axis=0, keepdims=True)
        rank_ref[pl.ds(kk, 1), :] = rk.astype(jnp.int32)
    for j in range(tr // COMBINE_TILE):
        cols = slice(j * COMBINE_TILE, (j + 1) * COMBINE_TILE)
        tc = before[:, j * COMBINE_TILE:j * COMBINE_TILE + 1]
        tcar_ref[j] = tc.astype(jnp.int32)
        slack = tc - WIN_ALIGN * jnp.floor(tc * (1.0 / WIN_ALIGN))
        local = before[:, cols] - tc
        for kk in range(TOP_K):
            selk = sels[kk][:, cols]
            lr = jnp.sum(jnp.where(selk, local, 0.0), axis=0, keepdims=True)
            sl = jnp.sum(jnp.where(selk, slack, 0.0), axis=0, keepdims=True)
            lrank_ref[pl.ds(kk, 1), cols] = lr.astype(jnp.int32)
            col_ref[pl.ds(kk, 1), cols] = (idxs[kk][:, cols] * float(WIN_ROWS) + sl + lr).astype(jnp.int32)
    carry[...] = carry[...] + jnp.sum(onehot, axis=1, keepdims=True)
    cnt_ref[...] = carry[...].astype(jnp.int32)


def _router(logits_t):
    n = logits_t.shape[1]
    tr = ROUTE_BLOCK
    assert n % tr == 0
    tri = jnp.asarray(np.triu(np.ones((tr, tr), np.float32), k=1), BF16)
    return pl.pallas_call(
        _router_kernel,
        grid=(n // tr,),
        in_specs=[pl.BlockSpec((N_EXPERTS, tr), lambda i: (0, i)),
                  pl.BlockSpec((tr, tr), lambda i: (0, 0))],
        out_specs=[pl.BlockSpec((TOP_K, tr), lambda i: (0, i)),
                   pl.BlockSpec((TOP_K, tr), lambda i: (0, i)),
                   pl.BlockSpec((TOP_K, tr), lambda i: (0, i)),
                   pl.BlockSpec((TOP_K, tr), lambda i: (0, i)),
                   pl.BlockSpec((TOP_K, tr), lambda i: (0, i)),
                   pl.BlockSpec((tr // COMBINE_TILE, N_EXPERTS, 1), lambda i: (i, 0, 0)),
                   pl.BlockSpec((N_EXPERTS, 1), lambda i: (0, 0))],
        out_shape=[jax.ShapeDtypeStruct((TOP_K, n), jnp.int32),
                   jax.ShapeDtypeStruct((TOP_K, n), jnp.int32),
                   jax.ShapeDtypeStruct((TOP_K, n), F32),
                   jax.ShapeDtypeStruct((TOP_K, n), jnp.int32),
                   jax.ShapeDtypeStruct((TOP_K, n), jnp.int32),
                   jax.ShapeDtypeStruct((n // COMBINE_TILE, N_EXPERTS, 1), jnp.int32),
                   jax.ShapeDtypeStruct((N_EXPERTS, 1), jnp.int32)],
        scratch_shapes=[pltpu.VMEM((N_EXPERTS, 1), F32)],
        compiler_params=pltpu.CompilerParams(dimension_semantics=("arbitrary",)),
        name="router",
    )(logits_t, tri)


def _expert_kernel(bexp_ref, nvalid_ref, x_ref, wgu_ref, bgu_ref, wd_ref, bd_ref, y_ref, wgu_bf, wd_bf):
    i = pl.program_id(0)
    prev = bexp_ref[jnp.maximum(i - 1, 0)]
    fresh = jnp.logical_or(i == 0, bexp_ref[i] != prev)
    nvalid = nvalid_ref[i]

    @pl.when(jnp.logical_and(fresh, nvalid > 0))
    def _():
        chunk = 32

        def cast_gu(r, c):
            r0 = pl.multiple_of(r * chunk, chunk)
            wgu_bf[pl.ds(r0, chunk), :] = wgu_ref[0, pl.ds(r0, chunk), :].astype(BF16)
            return c

        def cast_d(r, c):
            r0 = pl.multiple_of(r * chunk, chunk)
            wd_bf[pl.ds(r0, chunk), :] = wd_ref[0, pl.ds(r0, chunk), :].astype(BF16)
            return c

        lax.fori_loop(0, D_MODEL // chunk, cast_gu, 0)
        lax.fori_loop(0, D_EXPERT // chunk, cast_d, 0)

    @pl.when(nvalid > 0)
    def _():
        x = x_ref[...]
        g = jnp.dot(x, wgu_bf[:, 0:D_EXPERT], preferred_element_type=F32) + bgu_ref[0, :, 0:D_EXPERT]
        u = jnp.dot(x, wgu_bf[:, D_EXPERT:], preferred_element_type=F32) + bgu_ref[0, :, D_EXPERT:]
        g = jnp.minimum(g, SWIGLU_LIMIT)
        u = jnp.clip(u, -SWIGLU_LIMIT, SWIGLU_LIMIT)
        act = g * (1.0 / (1.0 + jnp.exp(-SWIGLU_ALPHA * g))) * (u + 1.0)
        y = jnp.dot(act.astype(BF16), wd_bf[...], preferred_element_type=F32) + bd_ref[0]
        row = lax.broadcasted_iota(jnp.int32, y.shape, 0)
        y_ref[...] = jnp.where(row < nvalid, y, 0.0).astype(BF16)

    @pl.when(nvalid == 0)
    def _():
        y_ref[...] = jnp.zeros_like(y_ref)


def _experts(xs, block_expert, nvalid, wgu, bgu, wd, bd):
    rb = EXPERT_ROWS
    nblk = xs.shape[0] // rb + 1
    last = xs.shape[0] // rb - 1
    grid_spec = pltpu.PrefetchScalarGridSpec(
        num_scalar_prefetch=2,
        grid=(nblk,),
        in_specs=[
            pl.BlockSpec((rb, D_MODEL), lambda i, be, nu: (jnp.minimum(i, last), 0)),
            pl.BlockSpec((1, D_MODEL, 2 * D_EXPERT), lambda i, be, nu: (be[i], 0, 0)),
            pl.BlockSpec((1, 1, 2 * D_EXPERT), lambda i, be, nu: (be[i], 0, 0)),
            pl.BlockSpec((1, D_EXPERT, D_MODEL), lambda i, be, nu: (be[i], 0, 0)),
            pl.BlockSpec((1, 1, D_MODEL), lambda i, be, nu: (be[i], 0, 0)),
        ],
        out_specs=pl.BlockSpec((rb, D_MODEL), lambda i, be, nu: (i, 0)),
        scratch_shapes=[pltpu.VMEM((D_MODEL, 2 * D_EXPERT), BF16),
                        pltpu.VMEM((D_EXPERT, D_MODEL), BF16)],
    )
    return pl.pallas_call(
        _expert_kernel,
        grid_spec=grid_spec,
        out_shape=jax.ShapeDtypeStruct((nblk * rb, D_MODEL), BF16),
        compiler_params=pltpu.CompilerParams(dimension_semantics=("arbitrary",),
                                             vmem_limit_bytes=VMEM_LIMIT),
        name="experts",
    )(block_expert, nvalid, xs, wgu, bgu.reshape(N_EXPERTS, 1, 2 * D_EXPERT), wd,
      bd.reshape(N_EXPERTS, 1, D_MODEL))


def _combine_kernel(n_prompt_tiles, offa_ref, nchunk_ref,
                    x1p_ref, x1s_ref, col_ref, lrank_ref, gate_ref, gfin_ref, yb_hbm,
                    outp_ref, outs_ref, ybuf, acc_ref, sem):
    i = pl.program_id(0)
    n_tiles = pl.num_programs(0)

    def window_copy(tile, chunk, e, slot):
        base = pl.multiple_of(offa_ref[tile * N_EXPERTS + e] + chunk * WIN, WIN_ALIGN)
        return pltpu.make_async_copy(yb_hbm.at[pl.ds(base, WIN_ROWS), :],
                                     ybuf.at[slot, pl.ds(e * WIN_ROWS, WIN_ROWS), :],
                                     sem.at[slot])

    def start_windows(tile, chunk, slot):
        for e in range(N_EXPERTS):
            window_copy(tile, chunk, e, slot).start()

    def wait_windows(tile, chunk, slot):
        for e in range(N_EXPERTS):
            window_copy(tile, chunk, e, slot).wait()

    slot = i % 2

    @pl.when(i == 0)
    def _():
        start_windows(0, 0, 0)

    @pl.when(i + 1 < n_tiles)
    def _():
        start_windows(i + 1, 0, 1 - slot)

    wait_windows(i, 0, slot)
    lane = lax.broadcasted_iota(jnp.int32, (COMBINE_TILE, N_EXPERTS * WIN_ROWS), 1)

    def moe_rows(chunk, buf):
        g = jnp.zeros(lane.shape, F32)
        for kk in range(TOP_K):
            lr = lrank_ref[:, kk:kk + 1]
            in_chunk = jnp.logical_and(lr >= chunk * WIN, lr < chunk * WIN + WIN)
            colk = jnp.where(in_chunk, col_ref[:, kk:kk + 1] - chunk * WIN, -1)
            g = jnp.where(lane == colk, gate_ref[:, kk:kk + 1], g)
        hi = g.astype(BF16)
        lo = (g - hi.astype(F32)).astype(BF16)
        y = ybuf[buf]
        return (jnp.dot(hi, y, preferred_element_type=F32) + jnp.dot(lo, y, preferred_element_type=F32))

    acc_ref[...] = moe_rows(0, slot)

    def extra_chunk(j, c):
        start_windows(i, j, 2)
        wait_windows(i, j, 2)
        acc_ref[...] += moe_rows(j, 2)
        return c

    lax.fori_loop(1, nchunk_ref[i], extra_chunk, 0)

    @pl.when(i < n_prompt_tiles)
    def _():
        outp_ref[...] = _rms(x1p_ref[...] + acc_ref[...], gfin_ref[...])

    @pl.when(i >= n_prompt_tiles)
    def _():
        outs_ref[...] = _rms(x1s_ref[...] + acc_ref[...], gfin_ref[...])


def _combine(x1_p, x1_s, col_t, lrank_t, gates_t, gfin, yb, offa, nchunk):
    ct = COMBINE_TILE
    n_p, n_s = x1_p.shape[0] // ct, x1_s.shape[0] // ct
    assert x1_p.shape[0] % ct == 0 and x1_s.shape[0] % ct == 0 and n_s >= 1
    tok_spec = pl.BlockSpec((ct, TOP_K), lambda i, o, c: (i, 0))
    grid_spec = pltpu.PrefetchScalarGridSpec(
        num_scalar_prefetch=2,
        grid=(n_p + n_s,),
        in_specs=[
            pl.BlockSpec((ct, D_MODEL), lambda i, o, c: (jnp.minimum(i, n_p - 1), 0)),
            pl.BlockSpec((ct, D_MODEL), lambda i, o, c: (jnp.maximum(i - n_p, 0), 0)),
            tok_spec, tok_spec, tok_spec,
            pl.BlockSpec((1, D_MODEL), lambda i, o, c: (0, 0)),
            pl.BlockSpec(memory_space=pl.ANY),
        ],
        out_specs=[
            pl.BlockSpec((ct, D_MODEL), lambda i, o, c: (jnp.minimum(i, n_p - 1), 0)),
            pl.BlockSpec((ct, D_MODEL), lambda i, o, c: (jnp.maximum(i - n_p, 0), 0)),
        ],
        scratch_shapes=[pltpu.VMEM((3, N_EXPERTS * WIN_ROWS, D_MODEL), BF16),
                        pltpu.VMEM((ct, D_MODEL), F32),
                        pltpu.SemaphoreType.DMA((3,))],
    )
    return pl.pallas_call(
        functools.partial(_combine_kernel, n_p),
        grid_spec=grid_spec,
        out_shape=[jax.ShapeDtypeStruct(x1_p.shape, F32), jax.ShapeDtypeStruct(x1_s.shape, F32)],
        compiler_params=pltpu.CompilerParams(dimension_semantics=("arbitrary",),
                                             vmem_limit_bytes=VMEM_LIMIT),
        name="combine",
    )(offa, nchunk, x1_p, x1_s, col_t, lrank_t, gates_t, gfin, yb)


def kernel(x_prompt, x_sample, cache_k, cache_v, state_pool, meta_tokens, norm_attn, w_in, attn_sinks,
           w_pool, pool_scale, w_out, norm_ffn, router_w, router_b, w_gate_up, b_gate_up, w_down, b_down,
           norm_final):
    assert w_in.shape[0] == 1, "single-layer trunk"
    bsz, seq, _ = x_prompt.shape
    assert bsz == 1
    nb = x_sample.shape[0]
    n_tok = seq + nb
    gattn = norm_attn[0].reshape(1, D_MODEL)
    gffn = norm_ffn[0].reshape(1, D_MODEL)
    win = w_in[0].astype(BF16)
    wpool = w_pool[0].astype(BF16)
    wout = w_out[0].astype(BF16)
    pscale = pool_scale[0].reshape(1, POOL_WIDTH)
    rwt = router_w[0].T
    rb = router_b[0].reshape(N_EXPERTS, 1)
    sinks = attn_sinks[0]

    h2_rows = n_tok + 8
    (x1_p, h2_all, lgt_all, kmeta, vmeta, ktail, vtail, ptail) = _prompt_mixer(
        x_prompt[0], meta_tokens, gattn, win, wpool, pscale, wout, gffn, rwt, rb, sinks, h2_rows, n_tok)
    ck = cache_k[0].reshape(nb, N_META + WINDOW, KV_WIDTH)
    cv = cache_v[0].reshape(nb, N_META + WINDOW, KV_WIDTH)
    (x1_s, h2_s, lgt_s, knew, vnew, pnew) = _sample_mixer(
        x_sample[:, 0], ck, cv, state_pool[0], gattn, win, wpool, pscale, wout, gffn, rwt, rb, sinks)
    h2_all = lax.dynamic_update_slice(h2_all, jnp.concatenate([h2_s, jnp.zeros((8, D_MODEL), BF16)]),
                                      (seq, 0))
    lgt_all = lax.dynamic_update_slice(lgt_all, lgt_s, (0, seq))

    eidx, rank, gates, col, lrank, tcar, counts = _router(lgt_all)
    counts = counts[:, 0]
    tcar = tcar[:, :, 0]
    rbk = EXPERT_ROWS
    padded = (counts + rbk - 1) // rbk * rbk
    pad_end = jnp.cumsum(padded)
    pad_start = pad_end - padded
    dest = pad_start[eidx] + rank
    nblk = -(-(n_tok * TOP_K) // rbk) + N_EXPERTS
    cap = nblk * rbk
    tok = jnp.broadcast_to(jnp.arange(n_tok, dtype=jnp.int32)[None], (TOP_K, n_tok))
    slot_tok = jnp.full((cap,), n_tok, jnp.int32).at[dest.reshape(-1)].set(
        tok.reshape(-1), unique_indices=True)
    block_start = jnp.arange(nblk + 1, dtype=jnp.int32) * rbk
    block_expert = jnp.minimum(jnp.sum(pad_end[None, :] <= block_start[:, None], axis=1),
                               N_EXPERTS - 1).astype(jnp.int32)
    nvalid = jnp.clip(counts[block_expert] - (block_start - pad_start[block_expert]), 0, rbk)
    nvalid = jnp.where(block_start < pad_end[-1], nvalid, 0).astype(jnp.int32)
    last_e = block_expert[jnp.maximum(pad_end[-1] // rbk - 1, 0)]
    block_expert = jnp.where(block_start < pad_end[-1], block_expert, last_e)

    xs = jnp.take(h2_all, slot_tok, axis=0, mode="clip")
    yb = _experts(xs, block_expert, nvalid, w_gate_up[0], b_gate_up[0], w_down[0], b_down[0])

    offa = (pad_start[None, :] + (tcar - tcar % WIN_ALIGN)).astype(jnp.int32).reshape(-1)
    run_len = jnp.concatenate([tcar[1:], counts[None, :]], axis=0) - tcar
    nchunk = jnp.maximum(jnp.max((run_len + WIN - 1) // WIN, axis=1), 1).astype(jnp.int32)
    gfin = norm_final.reshape(1, D_MODEL)
    y_prompt, y_sample = _combine(x1_p, x1_s, col.T, lrank.T, gates.T, gfin, yb, offa, nchunk)

    kv_shape = (1, 1, N_META + WINDOW, N_KV_HEADS, HEAD_DIM)
    new_k_p = jnp.concatenate([kmeta, ktail], axis=0).reshape(kv_shape)
    new_v_p = jnp.concatenate([vmeta, vtail], axis=0).reshape(kv_shape)
    new_pool_p = ptail[16 - POOL_STATE:].reshape(1, 1, POOL_STATE, POOL_WIDTH)
    new_k_s = jnp.concatenate([ck[:, :N_META], ck[:, N_META + 1:], knew[:, None]], axis=1).reshape(
        (1, nb, N_META + WINDOW, N_KV_HEADS, HEAD_DIM))
    new_v_s = jnp.concatenate([cv[:, :N_META], cv[:, N_META + 1:], vnew[:, None]], axis=1).reshape(
        (1, nb, N_META + WINDOW, N_KV_HEADS, HEAD_DIM))
    new_pool_s = jnp.concatenate([state_pool[0][:, 1:], pnew[:, None]], axis=1)[None]
    return (y_prompt[None], y_sample[:, None], new_k_p, new_v_p, new_pool_p, new_k_s, new_v_s, new_pool_s)
```

```python
import functools

import jax
import jax.numpy as jnp
import numpy as np
from jax import lax
from jax.experimental import pallas as pl
from jax.experimental.pallas import tpu as pltpu

F32 = jnp.float32
BF16 = jnp.bfloat16

D_MODEL = 1024
N_META = 16
N_HEADS = 8
HEAD_DIM = 64
N_KV_HEADS = 2
GQA_GROUP = N_HEADS // N_KV_HEADS
ATTN_WIDTH = N_HEADS * HEAD_DIM
KV_WIDTH = N_KV_HEADS * HEAD_DIM
WINDOW = 128
POOL_WIDTH = D_MODEL - ATTN_WIDTH
POOL_WINDOWS = (2, 4, 8, 16)
POOL_GROUP_DIM = POOL_WIDTH // len(POOL_WINDOWS)
POOL_STATE = max(POOL_WINDOWS) - 1
N_EXPERTS = 32
TOP_K = 4
D_EXPERT = D_MODEL
SWIGLU_ALPHA = 1.702
SWIGLU_LIMIT = 7.0
NORM_EPS = 1e-5
PAST_LEN = 16384

LANES = 128
QSUB = 64
KEYS_SUB = QSUB + WINDOW
META_PAD = 64
NKEY = META_PAD + KEYS_SUB
MASKED = -1e30
PROMPT_BLOCK = 512
ROUTE_BLOCK = 384
EXPERT_ROWS = 256
DISPATCH_TILE = 128
COMBINE_TILE = 128
WIN = 32
WIN_ALIGN = 16
WIN_ROWS = WIN + WIN_ALIGN
VMEM_LIMIT = 56 * 1024 * 1024


def _rms(x, g):
    return x * lax.rsqrt(jnp.mean(x * x, axis=-1, keepdims=True) + NORM_EPS) * g


def _dup_halves(a):
    lane = lax.broadcasted_iota(jnp.int32, a.shape, 1)
    r = pltpu.roll(a, HEAD_DIM, axis=1)
    lo = lane < HEAD_DIM
    return jnp.where(lo, a, r), jnp.where(lo, r, a)


def _pool_means(pext_ref, n):
    outs = []
    for gi, w in enumerate(POOL_WINDOWS):
        xg = pext_ref[:, gi * POOL_GROUP_DIM:(gi + 1) * POOL_GROUP_DIM]
        s = xg
        sh = 1
        while sh < w:
            s = s + pltpu.roll(s, sh, axis=0)
            sh *= 2
        outs.append(s[16:] * (1.0 / w) - xg[16:])
    return outs


def _pack_bf16_pairs(h):
    m = h.shape[1] // 2
    lo = pltpu.bitcast(h[:, :m].astype(BF16).astype(F32), jnp.uint32)
    hi = pltpu.bitcast(h[:, m:].astype(BF16).astype(F32), jnp.uint32)
    return lax.shift_right_logical(lo, jnp.uint32(16)) | (hi & jnp.uint32(0xFFFF0000))


def _unpack_bf16_pairs(w):
    lo = pltpu.bitcast(lax.shift_left(w, jnp.uint32(16)), F32).astype(BF16)
    hi = pltpu.bitcast(w & jnp.uint32(0xFFFF0000), F32).astype(BF16)
    return lo, hi


def _prompt_kernel(x_ref, meta_ref, gattn_ref, win_ref, wpool_ref, pscale_ref, wout_ref, gffn_ref,
                   rwt_ref, rb_ref, sink_ref, tbl_ref, tail_h2_ref, tail_lgt_ref,
                   x1_ref, h2_ref, lgt_ref, kmeta_ref, vmeta_ref, ktail_ref, vtail_ref, ptail_ref,
                   k2buf, v2buf, km2, vm2, qbuf, obuf, pext):
    pid = pl.program_id(0)
    n_main = pl.num_programs(0) - 1
    refs = (x_ref, meta_ref, gattn_ref, win_ref, wpool_ref, pscale_ref, wout_ref, gffn_ref,
            rwt_ref, rb_ref, sink_ref, tbl_ref,
            x1_ref, h2_ref, lgt_ref, kmeta_ref, vmeta_ref, ktail_ref, vtail_ref, ptail_ref,
            k2buf, v2buf, km2, vm2, qbuf, obuf, pext)

    @pl.when(pid < n_main)
    def _():
        _prompt_block(*refs)

    @pl.when(pid == n_main)
    def _():
        h2_ref[0:tail_h2_ref.shape[0], :] = tail_h2_ref[...]
        lgt_ref[:, 0:tail_lgt_ref.shape[1]] = tail_lgt_ref[...]


def _prompt_block(x_ref, meta_ref, gattn_ref, win_ref, wpool_ref, pscale_ref, wout_ref, gffn_ref,
                  rwt_ref, rb_ref, sink_ref, tbl_ref,
                  x1_ref, h2_ref, lgt_ref, kmeta_ref, vmeta_ref, ktail_ref, vtail_ref, ptail_ref,
                  k2buf, v2buf, km2, vm2, qbuf, obuf, pext):
    tb = x_ref.shape[0]
    pid = pl.program_id(0)

    @pl.when(pid == 0)
    def _():
        hm = _rms(meta_ref[...], gattn_ref[...]).astype(BF16)
        km = jnp.dot(hm, win_ref[:, ATTN_WIDTH:ATTN_WIDTH + KV_WIDTH], preferred_element_type=F32)
        vm = jnp.dot(hm, win_ref[:, ATTN_WIDTH + KV_WIDTH:ATTN_WIDTH + 2 * KV_WIDTH],
                     preferred_element_type=F32)
        pm = jnp.dot(hm, win_ref[:, ATTN_WIDTH + 2 * KV_WIDTH:], preferred_element_type=F32)
        kmeta_ref[...] = km
        vmeta_ref[...] = vm
        zpad = jnp.zeros((META_PAD - N_META, LANES), F32)
        k0, k1 = _dup_halves(jnp.concatenate([km, zpad], axis=0))
        v0, v1 = _dup_halves(jnp.concatenate([vm, zpad], axis=0))
        km2[0] = k0.astype(BF16)
        km2[1] = k1.astype(BF16)
        vm2[0] = v0.astype(BF16)
        vm2[1] = v1.astype(BF16)
        k2buf[:, 0:WINDOW, :] = jnp.zeros((2, WINDOW, LANES), BF16)
        v2buf[:, 0:WINDOW, :] = jnp.zeros((2, WINDOW, LANES), BF16)
        pext[0:16, :] = pm

    h = _rms(x_ref[...], gattn_ref[...]).astype(BF16)
    q = jnp.dot(h, win_ref[:, 0:ATTN_WIDTH], preferred_element_type=F32) * (HEAD_DIM ** -0.5)
    lane_t = lax.broadcasted_iota(jnp.int32, (tb, LANES), 1)
    for c in range(N_HEADS // 2):
        tile = q[:, c * LANES:(c + 1) * LANES]
        for a in range(2):
            keep = (lane_t < HEAD_DIM) if a == 0 else (lane_t >= HEAD_DIM)
            piece = jnp.where(keep, tile, 0.0).astype(BF16).reshape(tb // QSUB, QSUB, LANES)
            row = ((c % 2) * 2 + a) * QSUB
            qbuf[c // 2, :, row:row + QSUB, :] = piece
    k = jnp.dot(h, win_ref[:, ATTN_WIDTH:ATTN_WIDTH + KV_WIDTH], preferred_element_type=F32)
    v = jnp.dot(h, win_ref[:, ATTN_WIDTH + KV_WIDTH:ATTN_WIDTH + 2 * KV_WIDTH], preferred_element_type=F32)
    p = jnp.dot(h, win_ref[:, ATTN_WIDTH + 2 * KV_WIDTH:], preferred_element_type=F32)
    ktail_ref[...] = k[tb - WINDOW:]
    vtail_ref[...] = v[tb - WINDOW:]
    ptail_ref[...] = p[tb - 16:]
    k0, k1 = _dup_halves(k)
    v0, v1 = _dup_halves(v)
    k2buf[0, WINDOW:, :] = k0.astype(BF16)
    k2buf[1, WINDOW:, :] = k1.astype(BF16)
    v2buf[0, WINDOW:, :] = v0.astype(BF16)
    v2buf[1, WINDOW:, :] = v1.astype(BF16)
    pext[16:, :] = p

    lane_q = lax.broadcasted_iota(jnp.int32, (QSUB, LANES), 1)
    lo_q = lane_q < HEAD_DIM

    def sub_block(u, carry):
        r0 = pl.multiple_of(u * QSUB, QSUB)
        sel = jnp.where(jnp.logical_and(pid == 0, u < WINDOW // QSUB), u + 1, 0)
        for g in range(N_KV_HEADS):
            qm = qbuf[g, u]
            kwin = jnp.concatenate([km2[g], k2buf[g, pl.ds(r0, KEYS_SUB), :]], axis=0)
            vwin = jnp.concatenate([vm2[g], v2buf[g, pl.ds(r0, KEYS_SUB), :]], axis=0)
            s = lax.dot_general(qm, kwin, (((1,), (1,)), ((), ())), preferred_element_type=F32)
            s = s + tbl_ref[sel, g]
            sink = sink_ref[g]
            m = jnp.maximum(jnp.max(s, axis=1, keepdims=True), sink)
            e = jnp.exp(s - m)
            den = jnp.sum(e, axis=1, keepdims=True) + jnp.exp(sink - m)
            r = jnp.dot(e.astype(BF16), vwin, preferred_element_type=F32) / den
            o0 = jnp.where(lo_q, r[0:QSUB], r[QSUB:2 * QSUB])
            o1 = jnp.where(lo_q, r[2 * QSUB:3 * QSUB], r[3 * QSUB:])
            obuf[pl.ds(r0, QSUB), (2 * g) * LANES:(2 * g + 1) * LANES] = o0.astype(BF16)
            obuf[pl.ds(r0, QSUB), (2 * g + 1) * LANES:(2 * g + 2) * LANES] = o1.astype(BF16)
        return carry

    lax.fori_loop(0, tb // QSUB, sub_block, 0)

    pooled = _pool_means(pext, tb)
    for gi in range(len(POOL_WINDOWS)):
        y = jnp.dot(pooled[gi].astype(BF16), wpool_ref[gi], preferred_element_type=F32)
        y = y * pscale_ref[:, gi * POOL_GROUP_DIM:(gi + 1) * POOL_GROUP_DIM]
        obuf[:, ATTN_WIDTH + gi * POOL_GROUP_DIM:ATTN_WIDTH + (gi + 1) * POOL_GROUP_DIM] = y.astype(BF16)

    k2buf[:, 0:WINDOW, :] = k2buf[:, tb:tb + WINDOW, :]
    v2buf[:, 0:WINDOW, :] = v2buf[:, tb:tb + WINDOW, :]
    pext[0:16, :] = pext[tb:tb + 16, :]

    x1 = x_ref[...] + jnp.dot(obuf[...], wout_ref[...], preferred_element_type=F32)
    x1_ref[...] = x1
    h2 = _rms(x1, gffn_ref[...])
    h2_ref[...] = _pack_bf16_pairs(h2)
    lgt_ref[...] = lax.dot_general(rwt_ref[...], h2, (((1,), (1,)), ((), ())),
                                   preferred_element_type=F32,
                                   precision=lax.Precision.HIGHEST) + rb_ref[...]


def _attn_tables(sinks):
    i = np.arange(QSUB)[:, None]
    j = np.arange(NKEY)[None, :]
    jb = j - META_PAD
    rel = i + WINDOW - jb
    band_ok = (jb >= 0) & (rel >= 0) & (rel <= WINDOW)
    meta_ok = (j < N_META) & (i >= 0)
    slopes = np.exp2(-8.0 * np.arange(1, N_HEADS + 1) / N_HEADS)
    tbl = np.empty((3, N_KV_HEADS, GQA_GROUP * QSUB, NKEY), np.float32)
    for var in range(3):
        ok = band_ok if var == 0 else band_ok & (jb >= WINDOW - (var - 1) * QSUB)
        for g in range(N_KV_HEADS):
            for a in range(GQA_GROUP):
                hd = g * GQA_GROUP + a
                bias = np.where(ok, -slopes[hd] * rel, MASKED)
                bias = np.where(meta_ok, 0.0, bias)
                tbl[var, g, a * QSUB:(a + 1) * QSUB] = bias
    sink_col = jnp.repeat(sinks.astype(F32).reshape(N_KV_HEADS, GQA_GROUP, 1), QSUB, axis=2)
    return jnp.asarray(tbl), sink_col.reshape(N_KV_HEADS, GQA_GROUP * QSUB, 1)


def _prompt_mixer(x, meta, gattn, win, wpool, pscale, wout, gffn, rwt, rb, sinks, tail_h2, tail_lgt):
    seq = x.shape[0]
    tb = PROMPT_BLOCK
    n_tail = tail_h2.shape[0]
    assert seq % tb == 0 and tb % WINDOW == 0 and n_tail <= tb
    nblk = seq // tb
    n_tok = seq + n_tail
    tbl, sink_col = _attn_tables(sinks)
    full = lambda *shape: pl.BlockSpec(shape, lambda i: (0,) * len(shape))
    main = lambda i: (jnp.minimum(i, nblk - 1), 0)
    in_width = win.shape[1]
    return pl.pallas_call(
        _prompt_kernel,
        grid=(nblk + 1,),
        in_specs=[
            pl.BlockSpec((tb, D_MODEL), main),
            full(N_META, D_MODEL), full(1, D_MODEL), full(D_MODEL, in_width),
            full(len(POOL_WINDOWS), POOL_GROUP_DIM, POOL_GROUP_DIM), full(1, POOL_WIDTH),
            full(D_MODEL, D_MODEL), full(1, D_MODEL), full(N_EXPERTS, D_MODEL), full(N_EXPERTS, 1),
            full(N_KV_HEADS, GQA_GROUP * QSUB, 1), full(3, N_KV_HEADS, GQA_GROUP * QSUB, NKEY),
            full(n_tail, D_MODEL // 2), full(N_EXPERTS, n_tail),
        ],
        out_specs=[
            pl.BlockSpec((tb, D_MODEL), main),
            pl.BlockSpec((tb, D_MODEL // 2), lambda i: (i, 0)),
            pl.BlockSpec((N_EXPERTS, tb), lambda i: (0, i)),
            full(N_META, KV_WIDTH), full(N_META, KV_WIDTH),
            full(WINDOW, KV_WIDTH), full(WINDOW, KV_WIDTH), full(16, POOL_WIDTH),
        ],
        out_shape=[
            jax.ShapeDtypeStruct((seq, D_MODEL), F32),
            jax.ShapeDtypeStruct((n_tok, D_MODEL // 2), jnp.uint32),
            jax.ShapeDtypeStruct((N_EXPERTS, n_tok), F32),
            jax.ShapeDtypeStruct((N_META, KV_WIDTH), F32),
            jax.ShapeDtypeStruct((N_META, KV_WIDTH), F32),
            jax.ShapeDtypeStruct((WINDOW, KV_WIDTH), F32),
            jax.ShapeDtypeStruct((WINDOW, KV_WIDTH), F32),
            jax.ShapeDtypeStruct((16, POOL_WIDTH), F32),
        ],
        scratch_shapes=[
            pltpu.VMEM((N_KV_HEADS, WINDOW + tb, LANES), BF16),
            pltpu.VMEM((N_KV_HEADS, WINDOW + tb, LANES), BF16),
            pltpu.VMEM((N_KV_HEADS, META_PAD, LANES), BF16),
            pltpu.VMEM((N_KV_HEADS, META_PAD, LANES), BF16),
            pltpu.VMEM((N_KV_HEADS, tb // QSUB, GQA_GROUP * QSUB, LANES), BF16),
            pltpu.VMEM((tb, D_MODEL), BF16),
            pltpu.VMEM((16 + tb, POOL_WIDTH), F32),
        ],
        compiler_params=pltpu.CompilerParams(dimension_semantics=("arbitrary",),
                                             vmem_limit_bytes=VMEM_LIMIT),
        name="prompt_mixer",
    )(x, meta, gattn, win, wpool, pscale, wout, gffn, rwt, rb, sink_col, tbl, tail_h2, tail_lgt)


def _sample_kernel(x_ref, ck_ref, cv_ref, sp_ref, gattn_ref, win_ref, wpool_ref, pscale_ref, wout_ref,
                   gffn_ref, rwt_ref, rb_ref, sinkc_ref, bias_ref,
                   x1_ref, h2_ref, lgt_ref, knew_ref, vnew_ref, pnew_ref,
                   qm_buf, r_buf, obuf):
    nb = x_ref.shape[0]
    x = x_ref[...]
    h = _rms(x, gattn_ref[...]).astype(BF16)
    q = jnp.dot(h, win_ref[:, 0:ATTN_WIDTH], preferred_element_type=F32) * (HEAD_DIM ** -0.5)
    k = jnp.dot(h, win_ref[:, ATTN_WIDTH:ATTN_WIDTH + KV_WIDTH], preferred_element_type=F32)
    v = jnp.dot(h, win_ref[:, ATTN_WIDTH + KV_WIDTH:ATTN_WIDTH + 2 * KV_WIDTH], preferred_element_type=F32)
    p = jnp.dot(h, win_ref[:, ATTN_WIDTH + 2 * KV_WIDTH:], preferred_element_type=F32)
    knew_ref[...] = k
    vnew_ref[...] = v
    pnew_ref[...] = p

    lane = lax.broadcasted_iota(jnp.int32, (nb, LANES), 1)
    lo = lane < HEAD_DIM
    for hd in range(N_HEADS):
        tile = q[:, (hd // 2) * LANES:(hd // 2 + 1) * LANES]
        if (hd % 2) != (hd // GQA_GROUP):
            tile = pltpu.roll(tile, HEAD_DIM, axis=1)
        keep_lo = (hd // GQA_GROUP) == 0
        qm_buf[:, hd, :] = jnp.where(lo if keep_lo else jnp.logical_not(lo), tile, 0.0)

    def per_batch(b, carry):
        qm = qm_buf[b]
        kb = ck_ref[b].astype(BF16)
        vb = cv_ref[b].astype(BF16)
        s = lax.dot_general(qm.astype(BF16), kb, (((1,), (1,)), ((), ())), preferred_element_type=F32)
        s = s + bias_ref[...]
        kn = knew_ref[pl.ds(b, 1), :]
        vn = vnew_ref[pl.ds(b, 1), :]
        s_self = jnp.sum(qm * kn, axis=1, keepdims=True)
        sink = sinkc_ref[...]
        m = jnp.maximum(jnp.maximum(jnp.max(s, axis=1, keepdims=True), s_self), sink)
        e = jnp.exp(s - m)
        e_self = jnp.exp(s_self - m)
        den = jnp.sum(e, axis=1, keepdims=True) + e_self + jnp.exp(sink - m)
        r = jnp.dot(e.astype(BF16), vb, preferred_element_type=F32)
        r = r + e_self * vn
        r_buf[b] = r / den
        return carry

    lax.fori_loop(0, nb, per_batch, 0)

    for c in range(N_HEADS // 2):
        halves = []
        for a in range(2):
            hd = 2 * c + a
            t = r_buf[:, hd, :]
            if (hd // GQA_GROUP) != a:
                t = pltpu.roll(t, HEAD_DIM, axis=1)
            halves.append(t)
        obuf[:, c * LANES:(c + 1) * LANES] = jnp.where(lo, halves[0], halves[1]).astype(BF16)

    for gi, w in enumerate(POOL_WINDOWS):
        cols = slice(gi * POOL_GROUP_DIM, (gi + 1) * POOL_GROUP_DIM)
        pg = p[:, cols]
        acc = pg
        for d in range(1, w):
            acc = acc + sp_ref[:, POOL_STATE - d, cols]
        pooled = acc * (1.0 / w) - pg
        y = jnp.dot(pooled.astype(BF16), wpool_ref[gi], preferred_element_type=F32) * pscale_ref[:, cols]
        obuf[:, ATTN_WIDTH + gi * POOL_GROUP_DIM:ATTN_WIDTH + (gi + 1) * POOL_GROUP_DIM] = y.astype(BF16)

    x1 = x + jnp.dot(obuf[...], wout_ref[...], preferred_element_type=F32)
    x1_ref[...] = x1
    h2 = _rms(x1, gffn_ref[...])
    h2_ref[...] = _pack_bf16_pairs(h2)
    lgt_ref[...] = lax.dot_general(rwt_ref[...], h2, (((1,), (1,)), ((), ())),
                                   preferred_element_type=F32,
                                   precision=lax.Precision.HIGHEST) + rb_ref[...]


def _sample_mixer(x, ck, cv, sp, gattn, win, wpool, pscale, wout, gffn, rwt, rb, sinks):
    nb = x.shape[0]
    rows = ck.shape[1]
    slopes = np.exp2(-8.0 * np.arange(1, N_HEADS + 1) / N_HEADS)
    dist = np.concatenate([np.zeros(N_META), WINDOW - np.arange(WINDOW)])
    bias = jnp.asarray((-slopes[:, None] * dist[None, :]).astype(np.float32))
    vm = pl.BlockSpec(memory_space=pltpu.VMEM)
    return pl.pallas_call(
        _sample_kernel,
        in_specs=[vm] * 14,
        out_specs=[vm] * 6,
        out_shape=[
            jax.ShapeDtypeStruct((nb, D_MODEL), F32),
            jax.ShapeDtypeStruct((nb, D_MODEL // 2), jnp.uint32),
            jax.ShapeDtypeStruct((N_EXPERTS, nb), F32),
            jax.ShapeDtypeStruct((nb, KV_WIDTH), F32),
            jax.ShapeDtypeStruct((nb, KV_WIDTH), F32),
            jax.ShapeDtypeStruct((nb, POOL_WIDTH), F32),
        ],
        scratch_shapes=[
            pltpu.VMEM((nb, N_HEADS, LANES), F32),
            pltpu.VMEM((nb, N_HEADS, LANES), F32),
            pltpu.VMEM((nb, D_MODEL), BF16),
        ],
        compiler_params=pltpu.CompilerParams(vmem_limit_bytes=VMEM_LIMIT),
        name="sample_mixer",
    )(x, ck, cv, sp, gattn, win, wpool, pscale, wout, gffn, rwt, rb,
      sinks.astype(F32).reshape(N_HEADS, 1), bias)


def _router_kernel(lg_ref, tri_ref, eidx_ref, rank_ref, gate_ref, col_ref, lrank_ref, tcar_ref, cnt_ref,
                   carry):
    tr = lg_ref.shape[1]

    @pl.when(pl.program_id(0) == 0)
    def _():
        carry[...] = jnp.zeros_like(carry)

    work = lg_ref[...]
    eio = lax.broadcasted_iota(jnp.int32, work.shape, 0).astype(F32)
    sels, vals, idxs = [], [], []
    for _k in range(TOP_K):
        mx = jnp.max(work, axis=0, keepdims=True)
        idx = jnp.min(jnp.where(work == mx, eio, float(N_EXPERTS)), axis=0, keepdims=True)
        sel = eio == idx
        sels.append(sel)
        vals.append(mx)
        idxs.append(idx)
        work = jnp.where(sel, -jnp.inf, work)
    exps = [jnp.exp(vk - vals[0]) for vk in vals]
    tot = exps[0] + exps[1] + exps[2] + exps[3]
    onehot = jnp.zeros(work.shape, F32)
    for sel in sels:
        onehot = onehot + sel.astype(F32)
    before = jnp.dot(onehot.astype(BF16), tri_ref[...], preferred_element_type=F32) + carry[...]
    for kk in range(TOP_K):
        eidx_ref[pl.ds(kk, 1), :] = idxs[kk].astype(jnp.int32)
        gate_ref[pl.ds(kk, 1), :] = exps[kk] / tot
        rk = jnp.sum(jnp.where(sels[kk], before, 0.0), axis=0, keepdims=True)
        rank_ref[pl.ds(kk, 1), :] = rk.astype(jnp.int32)
    for j in range(tr // COMBINE_TILE):
        cols = slice(j * COMBINE_TILE, (j + 1) * COMBINE_TILE)
        tc = before[:, j * COMBINE_TILE:j * COMBINE_TILE + 1]
        tcar_ref[j] = tc.astype(jnp.int32)
        slack = tc - WIN_ALIGN * jnp.floor(tc * (1.0 / WIN_ALIGN))
        local = before[:, cols] - tc
        for kk in range(TOP_K):
            selk = sels[kk][:, cols]
            lr = jnp.sum(jnp.where(selk, local, 0.0), axis=0, keepdims=True)
            sl = jnp.sum(jnp.where(selk, slack, 0.0), axis=0, keepdims=True)
            lrank_ref[pl.ds(kk, 1), cols] = lr.astype(jnp.int32)
            col_ref[pl.ds(kk, 1), cols] = (idxs[kk][:, cols] * float(WIN_ROWS) + sl + lr).astype(jnp.int32)
    carry[...] = carry[...] + jnp.sum(onehot, axis=1, keepdims=True)
    cnt_ref[...] = carry[...].astype(jnp.int32)


def _router(logits_t):
    n = logits_t.shape[1]
    tr = ROUTE_BLOCK
    assert n % tr == 0
    tri = jnp.asarray(np.triu(np.ones((tr, tr), np.float32), k=1), BF16)
    return pl.pallas_call(
        _router_kernel,
        grid=(n // tr,),
        in_specs=[pl.BlockSpec((N_EXPERTS, tr), lambda i: (0, i)),
                  pl.BlockSpec((tr, tr), lambda i: (0, 0))],
        out_specs=[pl.BlockSpec((TOP_K, tr), lambda i: (0, i)),
                   pl.BlockSpec((TOP_K, tr), lambda i: (0, i)),
                   pl.BlockSpec((TOP_K, tr), lambda i: (0, i)),
                   pl.BlockSpec((TOP_K, tr), lambda i: (0, i)),
                   pl.BlockSpec((TOP_K, tr), lambda i: (0, i)),
                   pl.BlockSpec((tr // COMBINE_TILE, N_EXPERTS, 1), lambda i: (i, 0, 0)),
                   pl.BlockSpec((N_EXPERTS, 1), lambda i: (0, 0))],
        out_shape=[jax.ShapeDtypeStruct((TOP_K, n), jnp.int32),
                   jax.ShapeDtypeStruct((TOP_K, n), jnp.int32),
                   jax.ShapeDtypeStruct((TOP_K, n), F32),
                   jax.ShapeDtypeStruct((TOP_K, n), jnp.int32),
                   jax.ShapeDtypeStruct((TOP_K, n), jnp.int32),
                   jax.ShapeDtypeStruct((n // COMBINE_TILE, N_EXPERTS, 1), jnp.int32),
                   jax.ShapeDtypeStruct((N_EXPERTS, 1), jnp.int32)],
        scratch_shapes=[pltpu.VMEM((N_EXPERTS, 1), F32)],
        compiler_params=pltpu.CompilerParams(dimension_semantics=("arbitrary",)),
        name="router",
    )(logits_t, tri)


def _dispatch_kernel(pstart_ref, eidx_ref, rank_ref, h2_hbm, xs_hbm, sem):
    i = pl.program_id(0)
    dt = eidx_ref.shape[1]
    slot = i % 2

    def row_copy(src_row, dst_row, s):
        return pltpu.make_async_copy(h2_hbm.at[pl.ds(src_row, 1)], xs_hbm.at[pl.ds(dst_row, 1)], sem.at[s])

    def issue(t, c):
        for kk in range(TOP_K):
            dst = pstart_ref[eidx_ref[kk, t]] + rank_ref[kk, t]
            row_copy(i * dt + t, dst, slot).start()
        return c

    lax.fori_loop(0, dt, issue, 0)

    def drain(s):
        n = dt * TOP_K
        pltpu.make_async_copy(h2_hbm.at[pl.ds(0, n)], xs_hbm.at[pl.ds(0, n)], sem.at[s]).wait()

    @pl.when(i > 0)
    def _():
        drain(1 - slot)

    @pl.when(i == pl.num_programs(0) - 1)
    def _():
        drain(slot)


def _dispatch(h2w, eidx, rank, pad_start, cap):
    n_tok = h2w.shape[0]
    dt = DISPATCH_TILE
    assert n_tok % dt == 0
    smem_tile = pl.BlockSpec((TOP_K, dt), lambda i, ps: (0, i), memory_space=pltpu.SMEM)
    grid_spec = pltpu.PrefetchScalarGridSpec(
        num_scalar_prefetch=1,
        grid=(n_tok // dt,),
        in_specs=[smem_tile, smem_tile, pl.BlockSpec(memory_space=pl.ANY)],
        out_specs=pl.BlockSpec(memory_space=pl.ANY),
        scratch_shapes=[pltpu.SemaphoreType.DMA((2,))],
    )
    return pl.pallas_call(
        _dispatch_kernel,
        grid_spec=grid_spec,
        out_shape=jax.ShapeDtypeStruct((cap, h2w.shape[1]), h2w.dtype),
        compiler_params=pltpu.CompilerParams(dimension_semantics=("arbitrary",)),
        name="dispatch",
    )(pad_start, eidx, rank, h2w)


def _expert_kernel(bexp_ref, nvalid_ref, x_ref, wgu_ref, bgu_ref, wd_ref, bd_ref, y_ref, wgu_bf, wd_bf):
    i = pl.program_id(0)
    prev = bexp_ref[jnp.maximum(i - 1, 0)]
    fresh = jnp.logical_or(i == 0, bexp_ref[i] != prev)
    nvalid = nvalid_ref[i]

    @pl.when(jnp.logical_and(fresh, nvalid > 0))
    def _():
        chunk = 32

        def cast_gu(r, c):
            r0 = pl.multiple_of(r * chunk, chunk)
            wgu_bf[pl.ds(r0, chunk), :] = wgu_ref[0, pl.ds(r0, chunk), :].astype(BF16)
            return c

        def cast_d(r, c):
            r0 = pl.multiple_of(r * chunk, chunk)
            wd_bf[pl.ds(r0, chunk), :] = wd_ref[0, pl.ds(r0, chunk), :].astype(BF16)
            return c

        lax.fori_loop(0, D_MODEL // chunk, cast_gu, 0)
        lax.fori_loop(0, D_EXPERT // chunk, cast_d, 0)

    @pl.when(nvalid > 0)
    def _():
        xw = x_ref[...]
        xw = jnp.where(lax.broadcasted_iota(jnp.int32, xw.shape, 0) < nvalid, xw, jnp.uint32(0))
        xlo, xhi = _unpack_bf16_pairs(xw)
        half = D_MODEL // 2
        g = (jnp.dot(xlo, wgu_bf[0:half, 0:D_EXPERT], preferred_element_type=F32)
             + jnp.dot(xhi, wgu_bf[half:, 0:D_EXPERT], preferred_element_type=F32)
             + bgu_ref[0, :, 0:D_EXPERT])
        u = (jnp.dot(xlo, wgu_bf[0:half, D_EXPERT:], preferred_element_type=F32)
             + jnp.dot(xhi, wgu_bf[half:, D_EXPERT:], preferred_element_type=F32)
             + bgu_ref[0, :, D_EXPERT:])
        g = jnp.minimum(g, SWIGLU_LIMIT)
        u = jnp.clip(u, -SWIGLU_LIMIT, SWIGLU_LIMIT)
        act = g * (1.0 / (1.0 + jnp.exp(-SWIGLU_ALPHA * g))) * (u + 1.0)
        y = jnp.dot(act.astype(BF16), wd_bf[...], preferred_element_type=F32) + bd_ref[0]
        row = lax.broadcasted_iota(jnp.int32, y.shape, 0)
        y_ref[...] = jnp.where(row < nvalid, y, 0.0).astype(BF16)

    @pl.when(nvalid == 0)
    def _():
        y_ref[...] = jnp.zeros_like(y_ref)


def _experts(xs, block_expert, nvalid, wgu, bgu, wd, bd):
    rb = EXPERT_ROWS
    nblk = xs.shape[0] // rb + 1
    last = xs.shape[0] // rb - 1
    grid_spec = pltpu.PrefetchScalarGridSpec(
        num_scalar_prefetch=2,
        grid=(nblk,),
        in_specs=[
            pl.BlockSpec((rb, D_MODEL // 2), lambda i, be, nu: (jnp.minimum(i, last), 0)),
            pl.BlockSpec((1, D_MODEL, 2 * D_EXPERT), lambda i, be, nu: (be[i], 0, 0)),
            pl.BlockSpec((1, 1, 2 * D_EXPERT), lambda i, be, nu: (be[i], 0, 0)),
            pl.BlockSpec((1, D_EXPERT, D_MODEL), lambda i, be, nu: (be[i], 0, 0)),
            pl.BlockSpec((1, 1, D_MODEL), lambda i, be, nu: (be[i], 0, 0)),
        ],
        out_specs=pl.BlockSpec((rb, D_MODEL), lambda i, be, nu: (i, 0)),
        scratch_shapes=[pltpu.VMEM((D_MODEL, 2 * D_EXPERT), BF16),
                        pltpu.VMEM((D_EXPERT, D_MODEL), BF16)],
    )
    return pl.pallas_call(
        _expert_kernel,
        grid_spec=grid_spec,
        out_shape=jax.ShapeDtypeStruct((nblk * rb, D_MODEL), BF16),
        compiler_params=pltpu.CompilerParams(dimension_semantics=("arbitrary",),
                                             vmem_limit_bytes=VMEM_LIMIT),
        name="experts",
    )(block_expert, nvalid, xs, wgu, bgu.reshape(N_EXPERTS, 1, 2 * D_EXPERT), wd,
      bd.reshape(N_EXPERTS, 1, D_MODEL))


def _combine_kernel(n_prompt_tiles, offa_ref, nchunk_ref,
                    x1p_ref, x1s_ref, col_ref, lrank_ref, gate_ref, gfin_ref, yb_hbm,
                    outp_ref, outs_ref, ybuf, acc_ref, sem):
    i = pl.program_id(0)
    n_tiles = pl.num_programs(0)

    def window_copy(tile, chunk, e, slot):
        base = pl.multiple_of(offa_ref[tile * N_EXPERTS + e] + chunk * WIN, WIN_ALIGN)
        return pltpu.make_async_copy(yb_hbm.at[pl.ds(base, WIN_ROWS), :],
                                     ybuf.at[slot, pl.ds(e * WIN_ROWS, WIN_ROWS), :],
                                     sem.at[slot])

    def start_windows(tile, chunk, slot):
        for e in range(N_EXPERTS):
            window_copy(tile, chunk, e, slot).start()

    def wait_windows(tile, chunk, slot):
        for e in range(N_EXPERTS):
            window_copy(tile, chunk, e, slot).wait()

    slot = i % 2

    @pl.when(i == 0)
    def _():
        start_windows(0, 0, 0)

    @pl.when(i + 1 < n_tiles)
    def _():
        start_windows(i + 1, 0, 1 - slot)

    wait_windows(i, 0, slot)
    lane = lax.broadcasted_iota(jnp.int32, (COMBINE_TILE, N_EXPERTS * WIN_ROWS), 1)

    def moe_rows(chunk, buf):
        g = jnp.zeros(lane.shape, F32)
        for kk in range(TOP_K):
            lr = lrank_ref[:, kk:kk + 1]
            in_chunk = jnp.logical_and(lr >= chunk * WIN, lr < chunk * WIN + WIN)
            colk = jnp.where(in_chunk, col_ref[:, kk:kk + 1] - chunk * WIN, -1)
            g = jnp.where(lane == colk, gate_ref[:, kk:kk + 1], g)
        hi = g.astype(BF16)
        lo = (g - hi.astype(F32)).astype(BF16)
        y = ybuf[buf]
        return (jnp.dot(hi, y, preferred_element_type=F32) + jnp.dot(lo, y, preferred_element_type=F32))

    acc_ref[...] = moe_rows(0, slot)

    def extra_chunk(j, c):
        start_windows(i, j, 2)
        wait_windows(i, j, 2)
        acc_ref[...] += moe_rows(j, 2)
        return c

    lax.fori_loop(1, nchunk_ref[i], extra_chunk, 0)

    @pl.when(i < n_prompt_tiles)
    def _():
        outp_ref[...] = _rms(x1p_ref[...] + acc_ref[...], gfin_ref[...])

    @pl.when(i >= n_prompt_tiles)
    def _():
        outs_ref[...] = _rms(x1s_ref[...] + acc_ref[...], gfin_ref[...])


def _combine(x1_p, x1_s, col_t, lrank_t, gates_t, gfin, yb, offa, nchunk):
    ct = COMBINE_TILE
    n_p, n_s = x1_p.shape[0] // ct, x1_s.shape[0] // ct
    assert x1_p.shape[0] % ct == 0 and x1_s.shape[0] % ct == 0 and n_s >= 1
    tok_spec = pl.BlockSpec((ct, TOP_K), lambda i, o, c: (i, 0))
    grid_spec = pltpu.PrefetchScalarGridSpec(
        num_scalar_prefetch=2,
        grid=(n_p + n_s,),
        in_specs=[
            pl.BlockSpec((ct, D_MODEL), lambda i, o, c: (jnp.minimum(i, n_p - 1), 0)),
            pl.BlockSpec((ct, D_MODEL), lambda i, o, c: (jnp.maximum(i - n_p, 0), 0)),
            tok_spec, tok_spec, tok_spec,
            pl.BlockSpec((1, D_MODEL), lambda i, o, c: (0, 0)),
            pl.BlockSpec(memory_space=pl.ANY),
        ],
        out_specs=[
            pl.BlockSpec((ct, D_MODEL), lambda i, o, c: (jnp.minimum(i, n_p - 1), 0)),
            pl.BlockSpec((ct, D_MODEL), lambda i, o, c: (jnp.maximum(i - n_p, 0), 0)),
        ],
        scratch_shapes=[pltpu.VMEM((3, N_EXPERTS * WIN_ROWS, D_MODEL), BF16),
                        pltpu.VMEM((ct, D_MODEL), F32),
                        pltpu.SemaphoreType.DMA((3,))],
    )
    return pl.pallas_call(
        functools.partial(_combine_kernel, n_p),
        grid_spec=grid_spec,
        out_shape=[jax.ShapeDtypeStruct(x1_p.shape, F32), jax.ShapeDtypeStruct(x1_s.shape, F32)],
        compiler_params=pltpu.CompilerParams(dimension_semantics=("arbitrary",),
                                             vmem_limit_bytes=VMEM_LIMIT),
        name="combine",
    )(offa, nchunk, x1_p, x1_s, col_t, lrank_t, gates_t, gfin, yb)


def kernel(x_prompt, x_sample, cache_k, cache_v, state_pool, meta_tokens, norm_attn, w_in, attn_sinks,
           w_pool, pool_scale, w_out, norm_ffn, router_w, router_b, w_gate_up, b_gate_up, w_down, b_down,
           norm_final):
    assert w_in.shape[0] == 1, "single-layer trunk"
    bsz, seq, _ = x_prompt.shape
    assert bsz == 1
    nb = x_sample.shape[0]
    n_tok = seq + nb
    gattn = norm_attn[0].reshape(1, D_MODEL)
    gffn = norm_ffn[0].reshape(1, D_MODEL)
    win = w_in[0].astype(BF16)
    wpool = w_pool[0].astype(BF16)
    wout = w_out[0].astype(BF16)
    pscale = pool_scale[0].reshape(1, POOL_WIDTH)
    rwt = router_w[0].T
    rb = router_b[0].reshape(N_EXPERTS, 1)
    sinks = attn_sinks[0]

    ck = cache_k[0].reshape(nb, N_META + WINDOW, KV_WIDTH)
    cv = cache_v[0].reshape(nb, N_META + WINDOW, KV_WIDTH)
    (x1_s, h2_s, lgt_s, knew, vnew, pnew) = _sample_mixer(
        x_sample[:, 0], ck, cv, state_pool[0], gattn, win, wpool, pscale, wout, gffn, rwt, rb, sinks)
    (x1_p, h2_all, lgt_all, kmeta, vmeta, ktail, vtail, ptail) = _prompt_mixer(
        x_prompt[0], meta_tokens, gattn, win, wpool, pscale, wout, gffn, rwt, rb, sinks, h2_s, lgt_s)

    eidx, rank, gates, col, lrank, tcar, counts = _router(lgt_all)
    counts = counts[:, 0]
    tcar = tcar[:, :, 0]
    rbk = EXPERT_ROWS
    padded = (counts + rbk - 1) // rbk * rbk
    pad_end = jnp.cumsum(padded)
    pad_start = (pad_end - padded).astype(jnp.int32)
    nblk = -(-(n_tok * TOP_K) // rbk) + N_EXPERTS
    cap = nblk * rbk
    block_start = jnp.arange(nblk + 1, dtype=jnp.int32) * rbk
    block_expert = jnp.minimum(jnp.sum(pad_end[None, :] <= block_start[:, None], axis=1),
                               N_EXPERTS - 1).astype(jnp.int32)
    nvalid = jnp.clip(counts[block_expert] - (block_start - pad_start[block_expert]), 0, rbk)
    nvalid = jnp.where(block_start < pad_end[-1], nvalid, 0).astype(jnp.int32)
    last_e = block_expert[jnp.maximum(pad_end[-1] // rbk - 1, 0)]
    block_expert = jnp.where(block_start < pad_end[-1], block_expert, last_e)

    xs = _dispatch(h2_all, eidx, rank, pad_start, cap)
    yb = _experts(xs, block_expert, nvalid, w_gate_up[0], b_gate_up[0], w_down[0], b_down[0])

    offa = (pad_start[None, :] + (tcar - tcar % WIN_ALIGN)).astype(jnp.int32).reshape(-1)
    run_len = jnp.concatenate([tcar[1:], counts[None, :]], axis=0) - tcar
    nchunk = jnp.maximum(jnp.max((run_len + WIN - 1) // WIN, axis=1), 1).astype(jnp.int32)
    gfin = norm_final.reshape(1, D_MODEL)
    y_prompt, y_sample = _combine(x1_p, x1_s, col.T, lrank.T, gates.T, gfin, yb, offa, nchunk)

    kv_shape = (1, 1, N_META + WINDOW, N_KV_HEADS, HEAD_DIM)
    new_k_p = jnp.concatenate([kmeta, ktail], axis=0).reshape(kv_shape)
    new_v_p = jnp.concatenate([vmeta, vtail], axis=0).reshape(kv_shape)
    new_pool_p = ptail[16 - POOL_STATE:].reshape(1, 1, POOL_STATE, POOL_WIDTH)
    new_k_s = jnp.concatenate([ck[:, :N_META], ck[:, N_META + 1:], knew[:, None]], axis=1).reshape(
        (1, nb, N_META + WINDOW, N_KV_HEADS, HEAD_DIM))
    new_v_s = jnp.concatenate([cv[:, :N_META], cv[:, N_META + 1:], vnew[:, None]], axis=1).reshape(
        (1, nb, N_META + WINDOW, N_KV_HEADS, HEAD_DIM))
    new_pool_s = jnp.concatenate([state_pool[0][:, 1:], pnew[:, None]], axis=1)[None]
    return (y_prompt[None], y_sample[:, None], new_k_p, new_v_p, new_pool_p, new_k_s, new_v_s, new_pool_s)
```

```python
import functools

import jax
import jax.numpy as jnp
import numpy as np
from jax import lax
from jax.experimental import pallas as pl
from jax.experimental.pallas import tpu as pltpu

F32 = jnp.float32
BF16 = jnp.bfloat16

D_MODEL = 1024
N_META = 16
N_HEADS = 8
HEAD_DIM = 64
N_KV_HEADS = 2
GQA_GROUP = N_HEADS // N_KV_HEADS
ATTN_WIDTH = N_HEADS * HEAD_DIM
KV_WIDTH = N_KV_HEADS * HEAD_DIM
WINDOW = 128
POOL_WIDTH = D_MODEL - ATTN_WIDTH
POOL_WINDOWS = (2, 4, 8, 16)
POOL_GROUP_DIM = POOL_WIDTH // len(POOL_WINDOWS)
POOL_STATE = max(POOL_WINDOWS) - 1
N_EXPERTS = 32
TOP_K = 4
D_EXPERT = D_MODEL
SWIGLU_ALPHA = 1.702
SWIGLU_LIMIT = 7.0
NORM_EPS = 1e-5
PAST_LEN = 16384

LANES = 128
QSUB = 64
KEYS_SUB = QSUB + WINDOW
META_PAD = 64
NKEY = META_PAD + KEYS_SUB
MASKED = -1e30
PROMPT_BLOCK = 512
ROUTE_BLOCK = 384
EXPERT_ROWS = 256
DISPATCH_TILE = 128
PACK_CHUNKS = D_MODEL // 2 // LANES
COMBINE_TILE = 128
WIN = 32
WIN_ALIGN = 16
WIN_ROWS = WIN + WIN_ALIGN
VMEM_LIMIT = 56 * 1024 * 1024


def _rms(x, g):
    return x * lax.rsqrt(jnp.mean(x * x, axis=-1, keepdims=True) + NORM_EPS) * g


def _dup_halves(a):
    lane = lax.broadcasted_iota(jnp.int32, a.shape, 1)
    r = pltpu.roll(a, HEAD_DIM, axis=1)
    lo = lane < HEAD_DIM
    return jnp.where(lo, a, r), jnp.where(lo, r, a)


def _pool_means(pext_ref, n):
    outs = []
    for gi, w in enumerate(POOL_WINDOWS):
        xg = pext_ref[:, gi * POOL_GROUP_DIM:(gi + 1) * POOL_GROUP_DIM]
        s = xg
        sh = 1
        while sh < w:
            s = s + pltpu.roll(s, sh, axis=0)
            sh *= 2
        outs.append(s[16:] * (1.0 / w) - xg[16:])
    return outs


def _pack_bf16_pairs(h):
    m = h.shape[1] // 2
    lo = pltpu.bitcast(h[:, :m].astype(BF16).astype(F32), jnp.uint32)
    hi = pltpu.bitcast(h[:, m:].astype(BF16).astype(F32), jnp.uint32)
    return lax.shift_right_logical(lo, jnp.uint32(16)) | (hi & jnp.uint32(0xFFFF0000))


def _unpack_bf16_pairs(w):
    lo = pltpu.bitcast(lax.shift_left(w, jnp.uint32(16)), F32).astype(BF16)
    hi = pltpu.bitcast(w & jnp.uint32(0xFFFF0000), F32).astype(BF16)
    return lo, hi


def _prompt_kernel(x_ref, meta_ref, gattn_ref, win_ref, wpool_ref, pscale_ref, wout_ref, gffn_ref,
                   rwt_ref, rb_ref, sink_ref, tbl_ref, tail_h2_ref, tail_lgt_ref,
                   x1_ref, h2_ref, lgt_ref, kmeta_ref, vmeta_ref, ktail_ref, vtail_ref, ptail_ref,
                   k2buf, v2buf, km2, vm2, qbuf, obuf, pext):
    pid = pl.program_id(0)
    n_main = pl.num_programs(0) - 1
    refs = (x_ref, meta_ref, gattn_ref, win_ref, wpool_ref, pscale_ref, wout_ref, gffn_ref,
            rwt_ref, rb_ref, sink_ref, tbl_ref,
            x1_ref, h2_ref, lgt_ref, kmeta_ref, vmeta_ref, ktail_ref, vtail_ref, ptail_ref,
            k2buf, v2buf, km2, vm2, qbuf, obuf, pext)

    @pl.when(pid < n_main)
    def _():
        _prompt_block(*refs)

    @pl.when(pid == n_main)
    def _():
        h2_ref[0:tail_h2_ref.shape[0], :] = tail_h2_ref[...]
        lgt_ref[:, 0:tail_lgt_ref.shape[1]] = tail_lgt_ref[...]


def _prompt_block(x_ref, meta_ref, gattn_ref, win_ref, wpool_ref, pscale_ref, wout_ref, gffn_ref,
                  rwt_ref, rb_ref, sink_ref, tbl_ref,
                  x1_ref, h2_ref, lgt_ref, kmeta_ref, vmeta_ref, ktail_ref, vtail_ref, ptail_ref,
                  k2buf, v2buf, km2, vm2, qbuf, obuf, pext):
    tb = x_ref.shape[0]
    pid = pl.program_id(0)

    @pl.when(pid == 0)
    def _():
        hm = _rms(meta_ref[...], gattn_ref[...]).astype(BF16)
        km = jnp.dot(hm, win_ref[:, ATTN_WIDTH:ATTN_WIDTH + KV_WIDTH], preferred_element_type=F32)
        vm = jnp.dot(hm, win_ref[:, ATTN_WIDTH + KV_WIDTH:ATTN_WIDTH + 2 * KV_WIDTH],
                     preferred_element_type=F32)
        pm = jnp.dot(hm, win_ref[:, ATTN_WIDTH + 2 * KV_WIDTH:], preferred_element_type=F32)
        kmeta_ref[...] = km
        vmeta_ref[...] = vm
        zpad = jnp.zeros((META_PAD - N_META, LANES), F32)
        k0, k1 = _dup_halves(jnp.concatenate([km, zpad], axis=0))
        v0, v1 = _dup_halves(jnp.concatenate([vm, zpad], axis=0))
        km2[0] = k0.astype(BF16)
        km2[1] = k1.astype(BF16)
        vm2[0] = v0.astype(BF16)
        vm2[1] = v1.astype(BF16)
        k2buf[:, 0:WINDOW, :] = jnp.zeros((2, WINDOW, LANES), BF16)
        v2buf[:, 0:WINDOW, :] = jnp.zeros((2, WINDOW, LANES), BF16)
        pext[0:16, :] = pm

    h = _rms(x_ref[...], gattn_ref[...]).astype(BF16)
    q = jnp.dot(h, win_ref[:, 0:ATTN_WIDTH], preferred_element_type=F32) * (HEAD_DIM ** -0.5)
    lane_t = lax.broadcasted_iota(jnp.int32, (tb, LANES), 1)
    for c in range(N_HEADS // 2):
        tile = q[:, c * LANES:(c + 1) * LANES]
        for a in range(2):
            keep = (lane_t < HEAD_DIM) if a == 0 else (lane_t >= HEAD_DIM)
            piece = jnp.where(keep, tile, 0.0).astype(BF16).reshape(tb // QSUB, QSUB, LANES)
            row = ((c % 2) * 2 + a) * QSUB
            qbuf[c // 2, :, row:row + QSUB, :] = piece
    k = jnp.dot(h, win_ref[:, ATTN_WIDTH:ATTN_WIDTH + KV_WIDTH], preferred_element_type=F32)
    v = jnp.dot(h, win_ref[:, ATTN_WIDTH + KV_WIDTH:ATTN_WIDTH + 2 * KV_WIDTH], preferred_element_type=F32)
    p = jnp.dot(h, win_ref[:, ATTN_WIDTH + 2 * KV_WIDTH:], preferred_element_type=F32)
    ktail_ref[...] = k[tb - WINDOW:]
    vtail_ref[...] = v[tb - WINDOW:]
    ptail_ref[...] = p[tb - 16:]
    k0, k1 = _dup_halves(k)
    v0, v1 = _dup_halves(v)
    k2buf[0, WINDOW:, :] = k0.astype(BF16)
    k2buf[1, WINDOW:, :] = k1.astype(BF16)
    v2buf[0, WINDOW:, :] = v0.astype(BF16)
    v2buf[1, WINDOW:, :] = v1.astype(BF16)
    pext[16:, :] = p

    lane_q = lax.broadcasted_iota(jnp.int32, (QSUB, LANES), 1)
    lo_q = lane_q < HEAD_DIM

    def sub_block(u, carry):
        r0 = pl.multiple_of(u * QSUB, QSUB)
        sel = jnp.where(jnp.logical_and(pid == 0, u < WINDOW // QSUB), u + 1, 0)
        for g in range(N_KV_HEADS):
            qm = qbuf[g, u]
            kwin = jnp.concatenate([km2[g], k2buf[g, pl.ds(r0, KEYS_SUB), :]], axis=0)
            vwin = jnp.concatenate([vm2[g], v2buf[g, pl.ds(r0, KEYS_SUB), :]], axis=0)
            s = lax.dot_general(qm, kwin, (((1,), (1,)), ((), ())), preferred_element_type=F32)
            s = s + tbl_ref[sel, g]
            sink = sink_ref[g]
            m = jnp.maximum(jnp.max(s, axis=1, keepdims=True), sink)
            e = jnp.exp(s - m)
            den = jnp.sum(e, axis=1, keepdims=True) + jnp.exp(sink - m)
            r = jnp.dot(e.astype(BF16), vwin, preferred_element_type=F32) / den
            o0 = jnp.where(lo_q, r[0:QSUB], r[QSUB:2 * QSUB])
            o1 = jnp.where(lo_q, r[2 * QSUB:3 * QSUB], r[3 * QSUB:])
            obuf[pl.ds(r0, QSUB), (2 * g) * LANES:(2 * g + 1) * LANES] = o0.astype(BF16)
            obuf[pl.ds(r0, QSUB), (2 * g + 1) * LANES:(2 * g + 2) * LANES] = o1.astype(BF16)
        return carry

    lax.fori_loop(0, tb // QSUB, sub_block, 0)

    pooled = _pool_means(pext, tb)
    for gi in range(len(POOL_WINDOWS)):
        y = jnp.dot(pooled[gi].astype(BF16), wpool_ref[gi], preferred_element_type=F32)
        y = y * pscale_ref[:, gi * POOL_GROUP_DIM:(gi + 1) * POOL_GROUP_DIM]
        obuf[:, ATTN_WIDTH + gi * POOL_GROUP_DIM:ATTN_WIDTH + (gi + 1) * POOL_GROUP_DIM] = y.astype(BF16)

    k2buf[:, 0:WINDOW, :] = k2buf[:, tb:tb + WINDOW, :]
    v2buf[:, 0:WINDOW, :] = v2buf[:, tb:tb + WINDOW, :]
    pext[0:16, :] = pext[tb:tb + 16, :]

    x1 = x_ref[...] + jnp.dot(obuf[...], wout_ref[...], preferred_element_type=F32)
    x1_ref[...] = x1
    h2 = _rms(x1, gffn_ref[...])
    h2_ref[...] = h2.astype(BF16)
    lgt_ref[...] = lax.dot_general(rwt_ref[...], h2, (((1,), (1,)), ((), ())),
                                   preferred_element_type=F32,
                                   precision=lax.Precision.HIGHEST) + rb_ref[...]


def _attn_tables(sinks):
    i = np.arange(QSUB)[:, None]
    j = np.arange(NKEY)[None, :]
    jb = j - META_PAD
    rel = i + WINDOW - jb
    band_ok = (jb >= 0) & (rel >= 0) & (rel <= WINDOW)
    meta_ok = (j < N_META) & (i >= 0)
    slopes = np.exp2(-8.0 * np.arange(1, N_HEADS + 1) / N_HEADS)
    tbl = np.empty((3, N_KV_HEADS, GQA_GROUP * QSUB, NKEY), np.float32)
    for var in range(3):
        ok = band_ok if var == 0 else band_ok & (jb >= WINDOW - (var - 1) * QSUB)
        for g in range(N_KV_HEADS):
            for a in range(GQA_GROUP):
                hd = g * GQA_GROUP + a
                bias = np.where(ok, -slopes[hd] * rel, MASKED)
                bias = np.where(meta_ok, 0.0, bias)
                tbl[var, g, a * QSUB:(a + 1) * QSUB] = bias
    sink_col = jnp.repeat(sinks.astype(F32).reshape(N_KV_HEADS, GQA_GROUP, 1), QSUB, axis=2)
    return jnp.asarray(tbl), sink_col.reshape(N_KV_HEADS, GQA_GROUP * QSUB, 1)


def _prompt_mixer(x, meta, gattn, win, wpool, pscale, wout, gffn, rwt, rb, sinks, tail_h2, tail_lgt):
    seq = x.shape[0]
    tb = PROMPT_BLOCK
    n_tail = tail_h2.shape[0]
    assert seq % tb == 0 and tb % WINDOW == 0 and n_tail <= tb
    nblk = seq // tb
    n_tok = seq + n_tail
    tbl, sink_col = _attn_tables(sinks)
    full = lambda *shape: pl.BlockSpec(shape, lambda i: (0,) * len(shape))
    main = lambda i: (jnp.minimum(i, nblk - 1), 0)
    in_width = win.shape[1]
    return pl.pallas_call(
        _prompt_kernel,
        grid=(nblk + 1,),
        in_specs=[
            pl.BlockSpec((tb, D_MODEL), main),
            full(N_META, D_MODEL), full(1, D_MODEL), full(D_MODEL, in_width),
            full(len(POOL_WINDOWS), POOL_GROUP_DIM, POOL_GROUP_DIM), full(1, POOL_WIDTH),
            full(D_MODEL, D_MODEL), full(1, D_MODEL), full(N_EXPERTS, D_MODEL), full(N_EXPERTS, 1),
            full(N_KV_HEADS, GQA_GROUP * QSUB, 1), full(3, N_KV_HEADS, GQA_GROUP * QSUB, NKEY),
            full(n_tail, D_MODEL), full(N_EXPERTS, n_tail),
        ],
        out_specs=[
            pl.BlockSpec((tb, D_MODEL), main),
            pl.BlockSpec((tb, D_MODEL), lambda i: (i, 0)),
            pl.BlockSpec((N_EXPERTS, tb), lambda i: (0, i)),
            full(N_META, KV_WIDTH), full(N_META, KV_WIDTH),
            full(WINDOW, KV_WIDTH), full(WINDOW, KV_WIDTH), full(16, POOL_WIDTH),
        ],
        out_shape=[
            jax.ShapeDtypeStruct((seq, D_MODEL), F32),
            jax.ShapeDtypeStruct((n_tok, D_MODEL), BF16),
            jax.ShapeDtypeStruct((N_EXPERTS, n_tok), F32),
            jax.ShapeDtypeStruct((N_META, KV_WIDTH), F32),
            jax.ShapeDtypeStruct((N_META, KV_WIDTH), F32),
            jax.ShapeDtypeStruct((WINDOW, KV_WIDTH), F32),
            jax.ShapeDtypeStruct((WINDOW, KV_WIDTH), F32),
            jax.ShapeDtypeStruct((16, POOL_WIDTH), F32),
        ],
        scratch_shapes=[
            pltpu.VMEM((N_KV_HEADS, WINDOW + tb, LANES), BF16),
            pltpu.VMEM((N_KV_HEADS, WINDOW + tb, LANES), BF16),
            pltpu.VMEM((N_KV_HEADS, META_PAD, LANES), BF16),
            pltpu.VMEM((N_KV_HEADS, META_PAD, LANES), BF16),
            pltpu.VMEM((N_KV_HEADS, tb // QSUB, GQA_GROUP * QSUB, LANES), BF16),
            pltpu.VMEM((tb, D_MODEL), BF16),
            pltpu.VMEM((16 + tb, POOL_WIDTH), F32),
        ],
        compiler_params=pltpu.CompilerParams(dimension_semantics=("arbitrary",),
                                             vmem_limit_bytes=VMEM_LIMIT),
        name="prompt_mixer",
    )(x, meta, gattn, win, wpool, pscale, wout, gffn, rwt, rb, sink_col, tbl, tail_h2, tail_lgt)


def _sample_kernel(x_ref, ck_ref, cv_ref, sp_ref, gattn_ref, win_ref, wpool_ref, pscale_ref, wout_ref,
                   gffn_ref, rwt_ref, rb_ref, sinkc_ref, bias_ref,
                   x1_ref, h2_ref, lgt_ref, knew_ref, vnew_ref, pnew_ref,
                   qm_buf, r_buf, obuf):
    nb = x_ref.shape[0]
    x = x_ref[...]
    h = _rms(x, gattn_ref[...]).astype(BF16)
    q = jnp.dot(h, win_ref[:, 0:ATTN_WIDTH], preferred_element_type=F32) * (HEAD_DIM ** -0.5)
    k = jnp.dot(h, win_ref[:, ATTN_WIDTH:ATTN_WIDTH + KV_WIDTH], preferred_element_type=F32)
    v = jnp.dot(h, win_ref[:, ATTN_WIDTH + KV_WIDTH:ATTN_WIDTH + 2 * KV_WIDTH], preferred_element_type=F32)
    p = jnp.dot(h, win_ref[:, ATTN_WIDTH + 2 * KV_WIDTH:], preferred_element_type=F32)
    knew_ref[...] = k
    vnew_ref[...] = v
    pnew_ref[...] = p

    lane = lax.broadcasted_iota(jnp.int32, (nb, LANES), 1)
    lo = lane < HEAD_DIM
    for hd in range(N_HEADS):
        tile = q[:, (hd // 2) * LANES:(hd // 2 + 1) * LANES]
        if (hd % 2) != (hd // GQA_GROUP):
            tile = pltpu.roll(tile, HEAD_DIM, axis=1)
        keep_lo = (hd // GQA_GROUP) == 0
        qm_buf[:, hd, :] = jnp.where(lo if keep_lo else jnp.logical_not(lo), tile, 0.0)

    def per_batch(b, carry):
        qm = qm_buf[b]
        kb = ck_ref[b].astype(BF16)
        vb = cv_ref[b].astype(BF16)
        s = lax.dot_general(qm.astype(BF16), kb, (((1,), (1,)), ((), ())), preferred_element_type=F32)
        s = s + bias_ref[...]
        kn = knew_ref[pl.ds(b, 1), :]
        vn = vnew_ref[pl.ds(b, 1), :]
        s_self = jnp.sum(qm * kn, axis=1, keepdims=True)
        sink = sinkc_ref[...]
        m = jnp.maximum(jnp.maximum(jnp.max(s, axis=1, keepdims=True), s_self), sink)
        e = jnp.exp(s - m)
        e_self = jnp.exp(s_self - m)
        den = jnp.sum(e, axis=1, keepdims=True) + e_self + jnp.exp(sink - m)
        r = jnp.dot(e.astype(BF16), vb, preferred_element_type=F32)
        r = r + e_self * vn
        r_buf[b] = r / den
        return carry

    lax.fori_loop(0, nb, per_batch, 0)

    for c in range(N_HEADS // 2):
        halves = []
        for a in range(2):
            hd = 2 * c + a
            t = r_buf[:, hd, :]
            if (hd // GQA_GROUP) != a:
                t = pltpu.roll(t, HEAD_DIM, axis=1)
            halves.append(t)
        obuf[:, c * LANES:(c + 1) * LANES] = jnp.where(lo, halves[0], halves[1]).astype(BF16)

    for gi, w in enumerate(POOL_WINDOWS):
        cols = slice(gi * POOL_GROUP_DIM, (gi + 1) * POOL_GROUP_DIM)
        pg = p[:, cols]
        acc = pg
        for d in range(1, w):
            acc = acc + sp_ref[:, POOL_STATE - d, cols]
        pooled = acc * (1.0 / w) - pg
        y = jnp.dot(pooled.astype(BF16), wpool_ref[gi], preferred_element_type=F32) * pscale_ref[:, cols]
        obuf[:, ATTN_WIDTH + gi * POOL_GROUP_DIM:ATTN_WIDTH + (gi + 1) * POOL_GROUP_DIM] = y.astype(BF16)

    x1 = x + jnp.dot(obuf[...], wout_ref[...], preferred_element_type=F32)
    x1_ref[...] = x1
    h2 = _rms(x1, gffn_ref[...])
    h2_ref[...] = h2.astype(BF16)
    lgt_ref[...] = lax.dot_general(rwt_ref[...], h2, (((1,), (1,)), ((), ())),
                                   preferred_element_type=F32,
                                   precision=lax.Precision.HIGHEST) + rb_ref[...]


def _sample_mixer(x, ck, cv, sp, gattn, win, wpool, pscale, wout, gffn, rwt, rb, sinks):
    nb = x.shape[0]
    rows = ck.shape[1]
    slopes = np.exp2(-8.0 * np.arange(1, N_HEADS + 1) / N_HEADS)
    dist = np.concatenate([np.zeros(N_META), WINDOW - np.arange(WINDOW)])
    bias = jnp.asarray((-slopes[:, None] * dist[None, :]).astype(np.float32))
    vm = pl.BlockSpec(memory_space=pltpu.VMEM)
    return pl.pallas_call(
        _sample_kernel,
        in_specs=[vm] * 14,
        out_specs=[vm] * 6,
        out_shape=[
            jax.ShapeDtypeStruct((nb, D_MODEL), F32),
            jax.ShapeDtypeStruct((nb, D_MODEL), BF16),
            jax.ShapeDtypeStruct((N_EXPERTS, nb), F32),
            jax.ShapeDtypeStruct((nb, KV_WIDTH), F32),
            jax.ShapeDtypeStruct((nb, KV_WIDTH), F32),
            jax.ShapeDtypeStruct((nb, POOL_WIDTH), F32),
        ],
        scratch_shapes=[
            pltpu.VMEM((nb, N_HEADS, LANES), F32),
            pltpu.VMEM((nb, N_HEADS, LANES), F32),
            pltpu.VMEM((nb, D_MODEL), BF16),
        ],
        compiler_params=pltpu.CompilerParams(vmem_limit_bytes=VMEM_LIMIT),
        name="sample_mixer",
    )(x, ck, cv, sp, gattn, win, wpool, pscale, wout, gffn, rwt, rb,
      sinks.astype(F32).reshape(N_HEADS, 1), bias)


def _router_kernel(lg_ref, tri_ref, low_ref, eidx_ref, rank_ref, gate_ref, col_ref, lrank_ref, lpos_ref,
                   tcar_ref, cnt_ref, carry):
    tr = lg_ref.shape[1]

    @pl.when(pl.program_id(0) == 0)
    def _():
        carry[...] = jnp.zeros_like(carry)

    work = lg_ref[...]
    eio = lax.broadcasted_iota(jnp.int32, work.shape, 0).astype(F32)
    sels, vals, idxs = [], [], []
    for _k in range(TOP_K):
        mx = jnp.max(work, axis=0, keepdims=True)
        idx = jnp.min(jnp.where(work == mx, eio, float(N_EXPERTS)), axis=0, keepdims=True)
        sel = eio == idx
        sels.append(sel)
        vals.append(mx)
        idxs.append(idx)
        work = jnp.where(sel, -jnp.inf, work)
    exps = [jnp.exp(vk - vals[0]) for vk in vals]
    tot = exps[0] + exps[1] + exps[2] + exps[3]
    onehot = jnp.zeros(work.shape, F32)
    for sel in sels:
        onehot = onehot + sel.astype(F32)
    before = jnp.dot(onehot.astype(BF16), tri_ref[...], preferred_element_type=F32) + carry[...]
    for kk in range(TOP_K):
        eidx_ref[pl.ds(kk, 1), :] = idxs[kk].astype(jnp.int32)
        gate_ref[pl.ds(kk, 1), :] = exps[kk] / tot
        rk = jnp.sum(jnp.where(sels[kk], before, 0.0), axis=0, keepdims=True)
        rank_ref[pl.ds(kk, 1), :] = rk.astype(jnp.int32)
    for j in range(tr // COMBINE_TILE):
        cols = slice(j * COMBINE_TILE, (j + 1) * COMBINE_TILE)
        tc = before[:, j * COMBINE_TILE:j * COMBINE_TILE + 1]
        tcar_ref[j] = tc.astype(jnp.int32)
        slack = tc - WIN_ALIGN * jnp.floor(tc * (1.0 / WIN_ALIGN))
        local = before[:, cols] - tc
        tile_cnt = jnp.broadcast_to(jnp.sum(onehot[:, cols], axis=1, keepdims=True), local.shape)
        run_start = jnp.dot(low_ref[...], tile_cnt.astype(BF16), preferred_element_type=F32)
        for kk in range(TOP_K):
            selk = sels[kk][:, cols]
            lr = jnp.sum(jnp.where(selk, local, 0.0), axis=0, keepdims=True)
            sl = jnp.sum(jnp.where(selk, slack, 0.0), axis=0, keepdims=True)
            lp = jnp.sum(jnp.where(selk, run_start + local, 0.0), axis=0, keepdims=True)
            lpos_ref[pl.ds(kk, 1), cols] = lp.astype(jnp.int32)
            lrank_ref[pl.ds(kk, 1), cols] = lr.astype(jnp.int32)
            col_ref[pl.ds(kk, 1), cols] = (idxs[kk][:, cols] * float(WIN_ROWS) + sl + lr).astype(jnp.int32)
    carry[...] = carry[...] + jnp.sum(onehot, axis=1, keepdims=True)
    cnt_ref[...] = carry[...].astype(jnp.int32)


def _router(logits_t):
    n = logits_t.shape[1]
    tr = ROUTE_BLOCK
    assert n % tr == 0
    tri = jnp.asarray(np.triu(np.ones((tr, tr), np.float32), k=1), BF16)
    low = jnp.asarray(np.tril(np.ones((N_EXPERTS, N_EXPERTS), np.float32), k=-1), BF16)
    per_tok = pl.BlockSpec((TOP_K, tr), lambda i: (0, i))
    return pl.pallas_call(
        _router_kernel,
        grid=(n // tr,),
        in_specs=[pl.BlockSpec((N_EXPERTS, tr), lambda i: (0, i)),
                  pl.BlockSpec((tr, tr), lambda i: (0, 0)),
                  pl.BlockSpec((N_EXPERTS, N_EXPERTS), lambda i: (0, 0))],
        out_specs=[per_tok, per_tok, per_tok, per_tok, per_tok, per_tok,
                   pl.BlockSpec((tr // COMBINE_TILE, N_EXPERTS, 1), lambda i: (i, 0, 0)),
                   pl.BlockSpec((N_EXPERTS, 1), lambda i: (0, 0))],
        out_shape=[jax.ShapeDtypeStruct((TOP_K, n), jnp.int32),
                   jax.ShapeDtypeStruct((TOP_K, n), jnp.int32),
                   jax.ShapeDtypeStruct((TOP_K, n), F32),
                   jax.ShapeDtypeStruct((TOP_K, n), jnp.int32),
                   jax.ShapeDtypeStruct((TOP_K, n), jnp.int32),
                   jax.ShapeDtypeStruct((TOP_K, n), jnp.int32),
                   jax.ShapeDtypeStruct((n // COMBINE_TILE, N_EXPERTS, 1), jnp.int32),
                   jax.ShapeDtypeStruct((N_EXPERTS, 1), jnp.int32)],
        scratch_shapes=[pltpu.VMEM((N_EXPERTS, 1), F32)],
        compiler_params=pltpu.CompilerParams(dimension_semantics=("arbitrary",)),
        name="router",
    )(logits_t, tri, low)


def _dispatch_kernel(lstart_ref, cnt_ref, dst_ref, h2_ref, lpos_ref, xs_hbm, stg, sem):
    i = pl.program_id(0)
    n_tiles = pl.num_programs(0)
    dt = h2_ref.shape[0]
    rows = dt * TOP_K
    slot = i % 2

    def drain(s):
        pltpu.make_async_copy(stg.at[s], xs_hbm.at[pl.ds(0, rows)], sem.at[s]).wait()

    @pl.when(i >= 2)
    def _():
        drain(slot)

    rid = lax.broadcasted_iota(jnp.int32, (rows, dt), 0)
    hit = rid == lpos_ref[0:1, :]
    for kk in range(1, TOP_K):
        hit = jnp.logical_or(hit, rid == lpos_ref[kk:kk + 1, :])
    perm = jnp.where(hit, 1.0, 0.0).astype(BF16)
    srt = jnp.dot(perm, h2_ref[...], preferred_element_type=F32)
    packed = _pack_bf16_pairs(srt)
    for c in range(PACK_CHUNKS):
        stg[slot, :, c, :] = packed[:, c * LANES:(c + 1) * LANES]

    for e in range(N_EXPERTS):
        n = cnt_ref[i * N_EXPERTS + e]
        src0 = lstart_ref[i * N_EXPERTS + e]
        dst0 = dst_ref[i * N_EXPERTS + e]
        off = jnp.int32(0)
        piece = dt
        while piece >= 1:
            take = (n & piece) != 0

            @pl.when(take)
            def _(off=off, piece=piece):
                pltpu.make_async_copy(stg.at[slot, pl.ds(src0 + off, piece)],
                                      xs_hbm.at[pl.ds(dst0 + off, piece)], sem.at[slot]).start()

            off = off + jnp.where(take, piece, 0)
            piece //= 2

    @pl.when(i == n_tiles - 1)
    def _():
        drain(slot)

        @pl.when(n_tiles >= 2)
        def _():
            drain(1 - slot)


def _dispatch(h2, lpos, lstart, cnt, dst, cap):
    n_tok = h2.shape[0]
    dt = DISPATCH_TILE
    assert n_tok % dt == 0
    grid_spec = pltpu.PrefetchScalarGridSpec(
        num_scalar_prefetch=3,
        grid=(n_tok // dt,),
        in_specs=[pl.BlockSpec((dt, D_MODEL), lambda i, a, b, c: (i, 0)),
                  pl.BlockSpec((TOP_K, dt), lambda i, a, b, c: (0, i)),
                  ],
        out_specs=pl.BlockSpec(memory_space=pl.ANY),
        scratch_shapes=[pltpu.VMEM((2, dt * TOP_K, PACK_CHUNKS, LANES), jnp.uint32),
                        pltpu.SemaphoreType.DMA((2,))],
    )
    return pl.pallas_call(
        _dispatch_kernel,
        grid_spec=grid_spec,
        out_shape=jax.ShapeDtypeStruct((cap, PACK_CHUNKS, LANES), jnp.uint32),
        compiler_params=pltpu.CompilerParams(dimension_semantics=("arbitrary",),
                                             vmem_limit_bytes=VMEM_LIMIT),
        name="dispatch",
    )(lstart, cnt, dst, h2, lpos)


def _expert_kernel(bexp_ref, nvalid_ref, x_ref, wgu_ref, bgu_ref, wd_ref, bd_ref, y_ref, wgu_bf, wd_bf):
    i = pl.program_id(0)
    prev = bexp_ref[jnp.maximum(i - 1, 0)]
    fresh = jnp.logical_or(i == 0, bexp_ref[i] != prev)
    nvalid = nvalid_ref[i]

    @pl.when(jnp.logical_and(fresh, nvalid > 0))
    def _():
        chunk = 32

        def cast_gu(r, c):
            r0 = pl.multiple_of(r * chunk, chunk)
            wgu_bf[pl.ds(r0, chunk), :] = wgu_ref[0, pl.ds(r0, chunk), :].astype(BF16)
            return c

        def cast_d(r, c):
            r0 = pl.multiple_of(r * chunk, chunk)
            wd_bf[pl.ds(r0, chunk), :] = wd_ref[0, pl.ds(r0, chunk), :].astype(BF16)
            return c

        lax.fori_loop(0, D_MODEL // chunk, cast_gu, 0)
        lax.fori_loop(0, D_EXPERT // chunk, cast_d, 0)

    @pl.when(nvalid > 0)
    def _():
        xw = jnp.concatenate([x_ref[:, c, :] for c in range(PACK_CHUNKS)], axis=1)
        xw = jnp.where(lax.broadcasted_iota(jnp.int32, xw.shape, 0) < nvalid, xw, jnp.uint32(0))
        xlo, xhi = _unpack_bf16_pairs(xw)
        half = D_MODEL // 2
        g = (jnp.dot(xlo, wgu_bf[0:half, 0:D_EXPERT], preferred_element_type=F32)
             + jnp.dot(xhi, wgu_bf[half:, 0:D_EXPERT], preferred_element_type=F32)
             + bgu_ref[0, :, 0:D_EXPERT])
        u = (jnp.dot(xlo, wgu_bf[0:half, D_EXPERT:], preferred_element_type=F32)
             + jnp.dot(xhi, wgu_bf[half:, D_EXPERT:], preferred_element_type=F32)
             + bgu_ref[0, :, D_EXPERT:])
        g = jnp.minimum(g, SWIGLU_LIMIT)
        u = jnp.clip(u, -SWIGLU_LIMIT, SWIGLU_LIMIT)
        act = g * (1.0 / (1.0 + jnp.exp(-SWIGLU_ALPHA * g))) * (u + 1.0)
        y = jnp.dot(act.astype(BF16), wd_bf[...], preferred_element_type=F32) + bd_ref[0]
        row = lax.broadcasted_iota(jnp.int32, y.shape, 0)
        y_ref[...] = jnp.where(row < nvalid, y, 0.0).astype(BF16)

    @pl.when(nvalid == 0)
    def _():
        y_ref[...] = jnp.zeros_like(y_ref)


def _experts(xs, block_expert, nvalid, wgu, bgu, wd, bd):
    rb = EXPERT_ROWS
    nblk = xs.shape[0] // rb + 1
    last = xs.shape[0] // rb - 1
    grid_spec = pltpu.PrefetchScalarGridSpec(
        num_scalar_prefetch=2,
        grid=(nblk,),
        in_specs=[
            pl.BlockSpec((rb, PACK_CHUNKS, LANES), lambda i, be, nu: (jnp.minimum(i, last), 0, 0)),
            pl.BlockSpec((1, D_MODEL, 2 * D_EXPERT), lambda i, be, nu: (be[i], 0, 0)),
            pl.BlockSpec((1, 1, 2 * D_EXPERT), lambda i, be, nu: (be[i], 0, 0)),
            pl.BlockSpec((1, D_EXPERT, D_MODEL), lambda i, be, nu: (be[i], 0, 0)),
            pl.BlockSpec((1, 1, D_MODEL), lambda i, be, nu: (be[i], 0, 0)),
        ],
        out_specs=pl.BlockSpec((rb, D_MODEL), lambda i, be, nu: (i, 0)),
        scratch_shapes=[pltpu.VMEM((D_MODEL, 2 * D_EXPERT), BF16),
                        pltpu.VMEM((D_EXPERT, D_MODEL), BF16)],
    )
    return pl.pallas_call(
        _expert_kernel,
        grid_spec=grid_spec,
        out_shape=jax.ShapeDtypeStruct((nblk * rb, D_MODEL), BF16),
        compiler_params=pltpu.CompilerParams(dimension_semantics=("arbitrary",),
                                             vmem_limit_bytes=VMEM_LIMIT),
        name="experts",
    )(block_expert, nvalid, xs, wgu, bgu.reshape(N_EXPERTS, 1, 2 * D_EXPERT), wd,
      bd.reshape(N_EXPERTS, 1, D_MODEL))


def _combine_kernel(n_prompt_tiles, offa_ref, nchunk_ref,
                    x1p_ref, x1s_ref, col_ref, lrank_ref, gate_ref, gfin_ref, yb_hbm,
                    outp_ref, outs_ref, ybuf, acc_ref, sem):
    i = pl.program_id(0)
    n_tiles = pl.num_programs(0)

    def window_copy(tile, chunk, e, slot):
        base = pl.multiple_of(offa_ref[tile * N_EXPERTS + e] + chunk * WIN, WIN_ALIGN)
        return pltpu.make_async_copy(yb_hbm.at[pl.ds(base, WIN_ROWS), :],
                                     ybuf.at[slot, pl.ds(e * WIN_ROWS, WIN_ROWS), :],
                                     sem.at[slot])

    def start_windows(tile, chunk, slot):
        for e in range(N_EXPERTS):
            window_copy(tile, chunk, e, slot).start()

    def wait_windows(tile, chunk, slot):
        for e in range(N_EXPERTS):
            window_copy(tile, chunk, e, slot).wait()

    slot = i % 2

    @pl.when(i == 0)
    def _():
        start_windows(0, 0, 0)

    @pl.when(i + 1 < n_tiles)
    def _():
        start_windows(i + 1, 0, 1 - slot)

    wait_windows(i, 0, slot)
    lane = lax.broadcasted_iota(jnp.int32, (COMBINE_TILE, N_EXPERTS * WIN_ROWS), 1)

    def moe_rows(chunk, buf):
        g = jnp.zeros(lane.shape, F32)
        for kk in range(TOP_K):
            lr = lrank_ref[:, kk:kk + 1]
            in_chunk = jnp.logical_and(lr >= chunk * WIN, lr < chunk * WIN + WIN)
            colk = jnp.where(in_chunk, col_ref[:, kk:kk + 1] - chunk * WIN, -1)
            g = jnp.where(lane == colk, gate_ref[:, kk:kk + 1], g)
        hi = g.astype(BF16)
        lo = (g - hi.astype(F32)).astype(BF16)
        y = ybuf[buf]
        return (jnp.dot(hi, y, preferred_element_type=F32) + jnp.dot(lo, y, preferred_element_type=F32))

    acc_ref[...] = moe_rows(0, slot)

    def extra_chunk(j, c):
        start_windows(i, j, 2)
        wait_windows(i, j, 2)
        acc_ref[...] += moe_rows(j, 2)
        return c

    lax.fori_loop(1, nchunk_ref[i], extra_chunk, 0)

    @pl.when(i < n_prompt_tiles)
    def _():
        outp_ref[...] = _rms(x1p_ref[...] + acc_ref[...], gfin_ref[...])

    @pl.when(i >= n_prompt_tiles)
    def _():
        outs_ref[...] = _rms(x1s_ref[...] + acc_ref[...], gfin_ref[...])


def _combine(x1_p, x1_s, col_t, lrank_t, gates_t, gfin, yb, offa, nchunk):
    ct = COMBINE_TILE
    n_p, n_s = x1_p.shape[0] // ct, x1_s.shape[0] // ct
    assert x1_p.shape[0] % ct == 0 and x1_s.shape[0] % ct == 0 and n_s >= 1
    tok_spec = pl.BlockSpec((ct, TOP_K), lambda i, o, c: (i, 0))
    grid_spec = pltpu.PrefetchScalarGridSpec(
        num_scalar_prefetch=2,
        grid=(n_p + n_s,),
        in_specs=[
            pl.BlockSpec((ct, D_MODEL), lambda i, o, c: (jnp.minimum(i, n_p - 1), 0)),
            pl.BlockSpec((ct, D_MODEL), lambda i, o, c: (jnp.maximum(i - n_p, 0), 0)),
            tok_spec, tok_spec, tok_spec,
            pl.BlockSpec((1, D_MODEL), lambda i, o, c: (0, 0)),
            pl.BlockSpec(memory_space=pl.ANY),
        ],
        out_specs=[
            pl.BlockSpec((ct, D_MODEL), lambda i, o, c: (jnp.minimum(i, n_p - 1), 0)),
            pl.BlockSpec((ct, D_MODEL), lambda i, o, c: (jnp.maximum(i - n_p, 0), 0)),
        ],
        scratch_shapes=[pltpu.VMEM((3, N_EXPERTS * WIN_ROWS, D_MODEL), BF16),
                        pltpu.VMEM((ct, D_MODEL), F32),
                        pltpu.SemaphoreType.DMA((3,))],
    )
    return pl.pallas_call(
        functools.partial(_combine_kernel, n_p),
        grid_spec=grid_spec,
        out_shape=[jax.ShapeDtypeStruct(x1_p.shape, F32), jax.ShapeDtypeStruct(x1_s.shape, F32)],
        compiler_params=pltpu.CompilerParams(dimension_semantics=("arbitrary",),
                                             vmem_limit_bytes=VMEM_LIMIT),
        name="combine",
    )(offa, nchunk, x1_p, x1_s, col_t, lrank_t, gates_t, gfin, yb)


def kernel(x_prompt, x_sample, cache_k, cache_v, state_pool, meta_tokens, norm_attn, w_in, attn_sinks,
           w_pool, pool_scale, w_out, norm_ffn, router_w, router_b, w_gate_up, b_gate_up, w_down, b_down,
           norm_final):
    assert w_in.shape[0] == 1, "single-layer trunk"
    bsz, seq, _ = x_prompt.shape
    assert bsz == 1
    nb = x_sample.shape[0]
    n_tok = seq + nb
    gattn = norm_attn[0].reshape(1, D_MODEL)
    gffn = norm_ffn[0].reshape(1, D_MODEL)
    win = w_in[0].astype(BF16)
    wpool = w_pool[0].astype(BF16)
    wout = w_out[0].astype(BF16)
    pscale = pool_scale[0].reshape(1, POOL_WIDTH)
    rwt = router_w[0].T
    rb = router_b[0].reshape(N_EXPERTS, 1)
    sinks = attn_sinks[0]

    ck = cache_k[0].reshape(nb, N_META + WINDOW, KV_WIDTH)
    cv = cache_v[0].reshape(nb, N_META + WINDOW, KV_WIDTH)
    (x1_s, h2_s, lgt_s, knew, vnew, pnew) = _sample_mixer(
        x_sample[:, 0], ck, cv, state_pool[0], gattn, win, wpool, pscale, wout, gffn, rwt, rb, sinks)
    (x1_p, h2_all, lgt_all, kmeta, vmeta, ktail, vtail, ptail) = _prompt_mixer(
        x_prompt[0], meta_tokens, gattn, win, wpool, pscale, wout, gffn, rwt, rb, sinks, h2_s, lgt_s)

    _eidx, _rank, gates, col, lrank, lpos, tcar, counts = _router(lgt_all)
    counts = counts[:, 0]
    tcar = tcar[:, :, 0]
    rbk = EXPERT_ROWS
    padded = (counts + rbk - 1) // rbk * rbk
    pad_end = jnp.cumsum(padded)
    pad_start = (pad_end - padded).astype(jnp.int32)
    nblk = -(-(n_tok * TOP_K) // rbk) + N_EXPERTS
    cap = nblk * rbk
    block_start = jnp.arange(nblk + 1, dtype=jnp.int32) * rbk
    block_expert = jnp.minimum(jnp.sum(pad_end[None, :] <= block_start[:, None], axis=1),
                               N_EXPERTS - 1).astype(jnp.int32)
    nvalid = jnp.clip(counts[block_expert] - (block_start - pad_start[block_expert]), 0, rbk)
    nvalid = jnp.where(block_start < pad_end[-1], nvalid, 0).astype(jnp.int32)
    last_e = block_expert[jnp.maximum(pad_end[-1] // rbk - 1, 0)]
    block_expert = jnp.where(block_start < pad_end[-1], block_expert, last_e)

    run_len = jnp.concatenate([tcar[1:], counts[None, :]], axis=0) - tcar
    run_lstart = jnp.cumsum(run_len, axis=1) - run_len
    run_dst = pad_start[None, :] + tcar
    flat = lambda a: a.astype(jnp.int32).reshape(-1)
    xs = _dispatch(h2_all, lpos, flat(run_lstart), flat(run_len), flat(run_dst), cap)
    yb = _experts(xs, block_expert, nvalid, w_gate_up[0], b_gate_up[0], w_down[0], b_down[0])

    offa = (pad_start[None, :] + (tcar - tcar % WIN_ALIGN)).astype(jnp.int32).reshape(-1)
    nchunk = jnp.maximum(jnp.max((run_len + WIN - 1) // WIN, axis=1), 1).astype(jnp.int32)
    gfin = norm_final.reshape(1, D_MODEL)
    y_prompt, y_sample = _combine(x1_p, x1_s, col.T, lrank.T, gates.T, gfin, yb, offa, nchunk)

    kv_shape = (1, 1, N_META + WINDOW, N_KV_HEADS, HEAD_DIM)
    new_k_p = jnp.concatenate([kmeta, ktail], axis=0).reshape(kv_shape)
    new_v_p = jnp.concatenate([vmeta, vtail], axis=0).reshape(kv_shape)
    new_pool_p = ptail[16 - POOL_STATE:].reshape(1, 1, POOL_STATE, POOL_WIDTH)
    new_k_s = jnp.concatenate([ck[:, :N_META], ck[:, N_META + 1:], knew[:, None]], axis=1).reshape(
        (1, nb, N_META + WINDOW, N_KV_HEADS, HEAD_DIM))
    new_v_s = jnp.concatenate([cv[:, :N_META], cv[:, N_META + 1:], vnew[:, None]], axis=1).reshape(
        (1, nb, N_META + WINDOW, N_KV_HEADS, HEAD_DIM))
    new_pool_s = jnp.concatenate([state_pool[0][:, 1:], pnew[:, None]], axis=1)[None]
    return (y_prompt[None], y_sample[:, None], new_k_p, new_v_p, new_pool_p, new_k_s, new_v_s, new_pool_s)
```

```python
import functools

import jax
import jax.numpy as jnp
import numpy as np
from jax import lax
from jax.experimental import pallas as pl
from jax.experimental.pallas import tpu as pltpu

F32 = jnp.float32
BF16 = jnp.bfloat16

D_MODEL = 1024
N_META = 16
N_HEADS = 8
HEAD_DIM = 64
N_KV_HEADS = 2
GQA_GROUP = N_HEADS // N_KV_HEADS
ATTN_WIDTH = N_HEADS * HEAD_DIM
KV_WIDTH = N_KV_HEADS * HEAD_DIM
WINDOW = 128
POOL_WIDTH = D_MODEL - ATTN_WIDTH
POOL_WINDOWS = (2, 4, 8, 16)
POOL_GROUP_DIM = POOL_WIDTH // len(POOL_WINDOWS)
POOL_STATE = max(POOL_WINDOWS) - 1
N_EXPERTS = 32
TOP_K = 4
D_EXPERT = D_MODEL
SWIGLU_ALPHA = 1.702
SWIGLU_LIMIT = 7.0
NORM_EPS = 1e-5
PAST_LEN = 16384

LANES = 128
QSUB = 64
KEYS_SUB = QSUB + WINDOW
META_PAD = 64
NKEY = META_PAD + KEYS_SUB
MASKED = -1e30
PROMPT_BLOCK = 512
ROUTE_BLOCK = 384
EXPERT_ROWS = 256
DISPATCH_TILE = 128
PACK_CHUNKS = D_MODEL // 2 // LANES
COMBINE_TILE = 128
WIN = 32
WIN_ALIGN = 16
WIN_ROWS = WIN + WIN_ALIGN
VMEM_LIMIT = 56 * 1024 * 1024


def _rms(x, g):
    return x * lax.rsqrt(jnp.mean(x * x, axis=-1, keepdims=True) + NORM_EPS) * g


def _router_logits(rwt_ref, h2, h2_hi):
    nt = (((1,), (1,)), ((), ()))
    h2_lo = (h2 - h2_hi.astype(F32)).astype(BF16)
    return (lax.dot_general(rwt_ref[0], h2_hi, nt, preferred_element_type=F32)
            + lax.dot_general(rwt_ref[0], h2_lo, nt, preferred_element_type=F32)
            + lax.dot_general(rwt_ref[1], h2_hi, nt, preferred_element_type=F32))


def _dup_halves(a):
    lane = lax.broadcasted_iota(jnp.int32, a.shape, 1)
    r = pltpu.roll(a, HEAD_DIM, axis=1)
    lo = lane < HEAD_DIM
    return jnp.where(lo, a, r), jnp.where(lo, r, a)


def _pool_means(pext_ref, n):
    outs = []
    for gi, w in enumerate(POOL_WINDOWS):
        xg = pext_ref[:, gi * POOL_GROUP_DIM:(gi + 1) * POOL_GROUP_DIM]
        s = xg
        sh = 1
        while sh < w:
            s = s + pltpu.roll(s, sh, axis=0)
            sh *= 2
        outs.append(s[16:] * (1.0 / w) - xg[16:])
    return outs


def _pack_bf16_pairs(h):
    m = h.shape[1] // 2
    lo = pltpu.bitcast(h[:, :m].astype(BF16).astype(F32), jnp.uint32)
    hi = pltpu.bitcast(h[:, m:].astype(BF16).astype(F32), jnp.uint32)
    return lax.shift_right_logical(lo, jnp.uint32(16)) | (hi & jnp.uint32(0xFFFF0000))


def _unpack_bf16_pairs(w):
    lo = pltpu.bitcast(lax.shift_left(w, jnp.uint32(16)), F32).astype(BF16)
    hi = pltpu.bitcast(w & jnp.uint32(0xFFFF0000), F32).astype(BF16)
    return lo, hi


def _prompt_kernel(x_ref, meta_ref, gattn_ref, win_ref, wpool_ref, pscale_ref, wout_ref, gffn_ref,
                   rwt_ref, rb_ref, sink_ref, tbl_ref, tail_h2_ref, tail_lgt_ref,
                   x1_ref, h2_ref, lgt_ref, kmeta_ref, vmeta_ref, ktail_ref, vtail_ref, ptail_ref,
                   k2buf, v2buf, km2, vm2, qbuf, obuf, pext):
    pid = pl.program_id(0)
    n_main = pl.num_programs(0) - 1
    refs = (x_ref, meta_ref, gattn_ref, win_ref, wpool_ref, pscale_ref, wout_ref, gffn_ref,
            rwt_ref, rb_ref, sink_ref, tbl_ref,
            x1_ref, h2_ref, lgt_ref, kmeta_ref, vmeta_ref, ktail_ref, vtail_ref, ptail_ref,
            k2buf, v2buf, km2, vm2, qbuf, obuf, pext)

    @pl.when(pid < n_main)
    def _():
        _prompt_block(*refs)

    @pl.when(pid == n_main)
    def _():
        h2_ref[0:tail_h2_ref.shape[0], :] = tail_h2_ref[...]
        lgt_ref[:, 0:tail_lgt_ref.shape[1]] = tail_lgt_ref[...]


def _prompt_block(x_ref, meta_ref, gattn_ref, win_ref, wpool_ref, pscale_ref, wout_ref, gffn_ref,
                  rwt_ref, rb_ref, sink_ref, tbl_ref,
                  x1_ref, h2_ref, lgt_ref, kmeta_ref, vmeta_ref, ktail_ref, vtail_ref, ptail_ref,
                  k2buf, v2buf, km2, vm2, qbuf, obuf, pext):
    tb = x_ref.shape[0]
    pid = pl.program_id(0)

    @pl.when(pid == 0)
    def _():
        hm = _rms(meta_ref[...], gattn_ref[...]).astype(BF16)
        km = jnp.dot(hm, win_ref[:, ATTN_WIDTH:ATTN_WIDTH + KV_WIDTH], preferred_element_type=F32)
        vm = jnp.dot(hm, win_ref[:, ATTN_WIDTH + KV_WIDTH:ATTN_WIDTH + 2 * KV_WIDTH],
                     preferred_element_type=F32)
        pm = jnp.dot(hm, win_ref[:, ATTN_WIDTH + 2 * KV_WIDTH:], preferred_element_type=F32)
        kmeta_ref[...] = km
        vmeta_ref[...] = vm
        zpad = jnp.zeros((META_PAD - N_META, LANES), F32)
        k0, k1 = _dup_halves(jnp.concatenate([km, zpad], axis=0))
        v0, v1 = _dup_halves(jnp.concatenate([vm, zpad], axis=0))
        km2[0] = k0.astype(BF16)
        km2[1] = k1.astype(BF16)
        vm2[0, :, 0:LANES] = v0.astype(BF16)
        vm2[1, :, 0:LANES] = v1.astype(BF16)
        vm2[:, :, LANES:] = jnp.ones((N_KV_HEADS, META_PAD, LANES), BF16)
        k2buf[:, 0:WINDOW, :] = jnp.zeros((2, WINDOW, LANES), BF16)
        v2buf[:, 0:WINDOW, 0:LANES] = jnp.zeros((2, WINDOW, LANES), BF16)
        v2buf[:, :, LANES:] = jnp.ones((N_KV_HEADS, WINDOW + tb, LANES), BF16)
        pext[0:16, :] = pm

    h = _rms(x_ref[...], gattn_ref[...]).astype(BF16)
    q = jnp.dot(h, win_ref[:, 0:ATTN_WIDTH], preferred_element_type=F32) * (HEAD_DIM ** -0.5)
    lane_t = lax.broadcasted_iota(jnp.int32, (tb, LANES), 1)
    for c in range(N_HEADS // 2):
        tile = q[:, c * LANES:(c + 1) * LANES]
        for a in range(2):
            keep = (lane_t < HEAD_DIM) if a == 0 else (lane_t >= HEAD_DIM)
            piece = jnp.where(keep, tile, 0.0).astype(BF16).reshape(tb // QSUB, QSUB, LANES)
            row = ((c % 2) * 2 + a) * QSUB
            qbuf[c // 2, :, row:row + QSUB, :] = piece
    k = jnp.dot(h, win_ref[:, ATTN_WIDTH:ATTN_WIDTH + KV_WIDTH], preferred_element_type=F32)
    v = jnp.dot(h, win_ref[:, ATTN_WIDTH + KV_WIDTH:ATTN_WIDTH + 2 * KV_WIDTH], preferred_element_type=F32)
    p = jnp.dot(h, win_ref[:, ATTN_WIDTH + 2 * KV_WIDTH:], preferred_element_type=F32)
    ktail_ref[...] = k[tb - WINDOW:]
    vtail_ref[...] = v[tb - WINDOW:]
    ptail_ref[...] = p[tb - 16:]
    k0, k1 = _dup_halves(k)
    v0, v1 = _dup_halves(v)
    k2buf[0, WINDOW:, :] = k0.astype(BF16)
    k2buf[1, WINDOW:, :] = k1.astype(BF16)
    v2buf[0, WINDOW:, 0:LANES] = v0.astype(BF16)
    v2buf[1, WINDOW:, 0:LANES] = v1.astype(BF16)
    pext[16:, :] = p

    lane_q = lax.broadcasted_iota(jnp.int32, (QSUB, LANES), 1)
    lo_q = lane_q < HEAD_DIM

    for u in range(tb // QSUB):
        r0 = u * QSUB
        sel = jnp.where(pid == 0, u + 1, 0) if u < WINDOW // QSUB else 0
        for g in range(N_KV_HEADS):
            qm = qbuf[g, u]
            kwin = jnp.concatenate([km2[g], k2buf[g, r0:r0 + KEYS_SUB, :]], axis=0)
            vwin = jnp.concatenate([vm2[g], v2buf[g, r0:r0 + KEYS_SUB, :]], axis=0)
            s = lax.dot_general(qm, kwin, (((1,), (1,)), ((), ())), preferred_element_type=F32)
            s = s + tbl_ref[sel, g]
            sink = sink_ref[g]
            m = jnp.maximum(jnp.max(s, axis=1, keepdims=True), sink)
            e = jnp.exp(s - m).astype(BF16)
            r = jnp.dot(e, vwin, preferred_element_type=F32)
            o = r[:, 0:LANES] / (r[:, LANES:] + jnp.exp(sink - m))
            o0 = jnp.where(lo_q, o[0:QSUB], o[QSUB:2 * QSUB])
            o1 = jnp.where(lo_q, o[2 * QSUB:3 * QSUB], o[3 * QSUB:])
            obuf[r0:r0 + QSUB, (2 * g) * LANES:(2 * g + 1) * LANES] = o0.astype(BF16)
            obuf[r0:r0 + QSUB, (2 * g + 1) * LANES:(2 * g + 2) * LANES] = o1.astype(BF16)

    pooled = _pool_means(pext, tb)
    for gi in range(len(POOL_WINDOWS)):
        y = jnp.dot(pooled[gi].astype(BF16), wpool_ref[gi], preferred_element_type=F32)
        y = y * pscale_ref[:, gi * POOL_GROUP_DIM:(gi + 1) * POOL_GROUP_DIM]
        obuf[:, ATTN_WIDTH + gi * POOL_GROUP_DIM:ATTN_WIDTH + (gi + 1) * POOL_GROUP_DIM] = y.astype(BF16)

    k2buf[:, 0:WINDOW, :] = k2buf[:, tb:tb + WINDOW, :]
    v2buf[:, 0:WINDOW, 0:LANES] = v2buf[:, tb:tb + WINDOW, 0:LANES]
    pext[0:16, :] = pext[tb:tb + 16, :]

    x1 = x_ref[...] + jnp.dot(obuf[...], wout_ref[...], preferred_element_type=F32)
    x1_ref[...] = x1
    h2 = _rms(x1, gffn_ref[...])
    h2_hi = h2.astype(BF16)
    h2_ref[...] = h2_hi
    lgt_ref[...] = _router_logits(rwt_ref, h2, h2_hi) + rb_ref[...]


def _attn_tables(sinks):
    i = np.arange(QSUB)[:, None]
    j = np.arange(NKEY)[None, :]
    jb = j - META_PAD
    rel = i + WINDOW - jb
    band_ok = (jb >= 0) & (rel >= 0) & (rel <= WINDOW)
    meta_ok = (j < N_META) & (i >= 0)
    slopes = np.exp2(-8.0 * np.arange(1, N_HEADS + 1) / N_HEADS)
    tbl = np.empty((3, N_KV_HEADS, GQA_GROUP * QSUB, NKEY), np.float32)
    for var in range(3):
        ok = band_ok if var == 0 else band_ok & (jb >= WINDOW - (var - 1) * QSUB)
        for g in range(N_KV_HEADS):
            for a in range(GQA_GROUP):
                hd = g * GQA_GROUP + a
                bias = np.where(ok, -slopes[hd] * rel, MASKED)
                bias = np.where(meta_ok, 0.0, bias)
                tbl[var, g, a * QSUB:(a + 1) * QSUB] = bias
    sink_col = jnp.repeat(sinks.astype(F32).reshape(N_KV_HEADS, GQA_GROUP, 1), QSUB, axis=2)
    return jnp.asarray(tbl), sink_col.reshape(N_KV_HEADS, GQA_GROUP * QSUB, 1)


def _prompt_mixer(x, meta, gattn, win, wpool, pscale, wout, gffn, rwt, rb, sinks, tail_h2, tail_lgt):
    seq = x.shape[0]
    tb = PROMPT_BLOCK
    n_tail = tail_h2.shape[0]
    assert seq % tb == 0 and tb % WINDOW == 0 and n_tail <= tb
    nblk = seq // tb
    n_tok = seq + n_tail
    tbl, sink_col = _attn_tables(sinks)
    full = lambda *shape: pl.BlockSpec(shape, lambda i: (0,) * len(shape))
    main = lambda i: (jnp.minimum(i, nblk - 1), 0)
    in_width = win.shape[1]
    return pl.pallas_call(
        _prompt_kernel,
        grid=(nblk + 1,),
        in_specs=[
            pl.BlockSpec((tb, D_MODEL), main),
            full(N_META, D_MODEL), full(1, D_MODEL), full(D_MODEL, in_width),
            full(len(POOL_WINDOWS), POOL_GROUP_DIM, POOL_GROUP_DIM), full(1, POOL_WIDTH),
            full(D_MODEL, D_MODEL), full(1, D_MODEL), full(2, N_EXPERTS, D_MODEL), full(N_EXPERTS, 1),
            full(N_KV_HEADS, GQA_GROUP * QSUB, 1), full(3, N_KV_HEADS, GQA_GROUP * QSUB, NKEY),
            full(n_tail, D_MODEL), full(N_EXPERTS, n_tail),
        ],
        out_specs=[
            pl.BlockSpec((tb, D_MODEL), main),
            pl.BlockSpec((tb, D_MODEL), lambda i: (i, 0)),
            pl.BlockSpec((N_EXPERTS, tb), lambda i: (0, i)),
            full(N_META, KV_WIDTH), full(N_META, KV_WIDTH),
            full(WINDOW, KV_WIDTH), full(WINDOW, KV_WIDTH), full(16, POOL_WIDTH),
        ],
        out_shape=[
            jax.ShapeDtypeStruct((seq, D_MODEL), F32),
            jax.ShapeDtypeStruct((n_tok, D_MODEL), BF16),
            jax.ShapeDtypeStruct((N_EXPERTS, n_tok), F32),
            jax.ShapeDtypeStruct((N_META, KV_WIDTH), F32),
            jax.ShapeDtypeStruct((N_META, KV_WIDTH), F32),
            jax.ShapeDtypeStruct((WINDOW, KV_WIDTH), F32),
            jax.ShapeDtypeStruct((WINDOW, KV_WIDTH), F32),
            jax.ShapeDtypeStruct((16, POOL_WIDTH), F32),
        ],
        scratch_shapes=[
            pltpu.VMEM((N_KV_HEADS, WINDOW + tb, LANES), BF16),
            pltpu.VMEM((N_KV_HEADS, WINDOW + tb, 2 * LANES), BF16),
            pltpu.VMEM((N_KV_HEADS, META_PAD, LANES), BF16),
            pltpu.VMEM((N_KV_HEADS, META_PAD, 2 * LANES), BF16),
            pltpu.VMEM((N_KV_HEADS, tb // QSUB, GQA_GROUP * QSUB, LANES), BF16),
            pltpu.VMEM((tb, D_MODEL), BF16),
            pltpu.VMEM((16 + tb, POOL_WIDTH), F32),
        ],
        compiler_params=pltpu.CompilerParams(dimension_semantics=("arbitrary",),
                                             vmem_limit_bytes=VMEM_LIMIT),
        name="prompt_mixer",
    )(x, meta, gattn, win, wpool, pscale, wout, gffn, rwt, rb, sink_col, tbl, tail_h2, tail_lgt)


def _sample_kernel(x_ref, ck_ref, cv_ref, sp_ref, gattn_ref, win_ref, wpool_ref, pscale_ref, wout_ref,
                   gffn_ref, rwt_ref, rb_ref, sinkc_ref, bias_ref,
                   x1_ref, h2_ref, lgt_ref, knew_ref, vnew_ref, pnew_ref,
                   qm_buf, r_buf, obuf):
    nb = x_ref.shape[0]
    x = x_ref[...]
    h = _rms(x, gattn_ref[...]).astype(BF16)
    q = jnp.dot(h, win_ref[:, 0:ATTN_WIDTH], preferred_element_type=F32) * (HEAD_DIM ** -0.5)
    k = jnp.dot(h, win_ref[:, ATTN_WIDTH:ATTN_WIDTH + KV_WIDTH], preferred_element_type=F32)
    v = jnp.dot(h, win_ref[:, ATTN_WIDTH + KV_WIDTH:ATTN_WIDTH + 2 * KV_WIDTH], preferred_element_type=F32)
    p = jnp.dot(h, win_ref[:, ATTN_WIDTH + 2 * KV_WIDTH:], preferred_element_type=F32)
    knew_ref[...] = k
    vnew_ref[...] = v
    pnew_ref[...] = p

    lane = lax.broadcasted_iota(jnp.int32, (nb, LANES), 1)
    lo = lane < HEAD_DIM
    for hd in range(N_HEADS):
        tile = q[:, (hd // 2) * LANES:(hd // 2 + 1) * LANES]
        if (hd % 2) != (hd // GQA_GROUP):
            tile = pltpu.roll(tile, HEAD_DIM, axis=1)
        keep_lo = (hd // GQA_GROUP) == 0
        qm_buf[:, hd, :] = jnp.where(lo if keep_lo else jnp.logical_not(lo), tile, 0.0)

    def per_batch(b, carry):
        qm = qm_buf[b]
        kb = ck_ref[b].astype(BF16)
        vb = cv_ref[b].astype(BF16)
        s = lax.dot_general(qm.astype(BF16), kb, (((1,), (1,)), ((), ())), preferred_element_type=F32)
        s = s + bias_ref[...]
        kn = knew_ref[pl.ds(b, 1), :]
        vn = vnew_ref[pl.ds(b, 1), :]
        s_self = jnp.sum(qm * kn, axis=1, keepdims=True)
        sink = sinkc_ref[...]
        m = jnp.maximum(jnp.maximum(jnp.max(s, axis=1, keepdims=True), s_self), sink)
        e = jnp.exp(s - m)
        e_self = jnp.exp(s_self - m)
        den = jnp.sum(e, axis=1, keepdims=True) + e_self + jnp.exp(sink - m)
        r = jnp.dot(e.astype(BF16), vb, preferred_element_type=F32)
        r = r + e_self * vn
        r_buf[b] = r / den
        return carry

    lax.fori_loop(0, nb, per_batch, 0)

    for c in range(N_HEADS // 2):
        halves = []
        for a in range(2):
            hd = 2 * c + a
            t = r_buf[:, hd, :]
            if (hd // GQA_GROUP) != a:
                t = pltpu.roll(t, HEAD_DIM, axis=1)
            halves.append(t)
        obuf[:, c * LANES:(c + 1) * LANES] = jnp.where(lo, halves[0], halves[1]).astype(BF16)

    for gi, w in enumerate(POOL_WINDOWS):
        cols = slice(gi * POOL_GROUP_DIM, (gi + 1) * POOL_GROUP_DIM)
        pg = p[:, cols]
        acc = pg
        for d in range(1, w):
            acc = acc + sp_ref[:, POOL_STATE - d, cols]
        pooled = acc * (1.0 / w) - pg
        y = jnp.dot(pooled.astype(BF16), wpool_ref[gi], preferred_element_type=F32) * pscale_ref[:, cols]
        obuf[:, ATTN_WIDTH + gi * POOL_GROUP_DIM:ATTN_WIDTH + (gi + 1) * POOL_GROUP_DIM] = y.astype(BF16)

    x1 = x + jnp.dot(obuf[...], wout_ref[...], preferred_element_type=F32)
    x1_ref[...] = x1
    h2 = _rms(x1, gffn_ref[...])
    h2_hi = h2.astype(BF16)
    h2_ref[...] = h2_hi
    lgt_ref[...] = _router_logits(rwt_ref, h2, h2_hi) + rb_ref[...]


def _sample_mixer(x, ck, cv, sp, gattn, win, wpool, pscale, wout, gffn, rwt, rb, sinks):
    nb = x.shape[0]
    rows = ck.shape[1]
    slopes = np.exp2(-8.0 * np.arange(1, N_HEADS + 1) / N_HEADS)
    dist = np.concatenate([np.zeros(N_META), WINDOW - np.arange(WINDOW)])
    bias = jnp.asarray((-slopes[:, None] * dist[None, :]).astype(np.float32))
    vm = pl.BlockSpec(memory_space=pltpu.VMEM)
    return pl.pallas_call(
        _sample_kernel,
        in_specs=[vm] * 14,
        out_specs=[vm] * 6,
        out_shape=[
            jax.ShapeDtypeStruct((nb, D_MODEL), F32),
            jax.ShapeDtypeStruct((nb, D_MODEL), BF16),
            jax.ShapeDtypeStruct((N_EXPERTS, nb), F32),
            jax.ShapeDtypeStruct((nb, KV_WIDTH), F32),
            jax.ShapeDtypeStruct((nb, KV_WIDTH), F32),
            jax.ShapeDtypeStruct((nb, POOL_WIDTH), F32),
        ],
        scratch_shapes=[
            pltpu.VMEM((nb, N_HEADS, LANES), F32),
            pltpu.VMEM((nb, N_HEADS, LANES), F32),
            pltpu.VMEM((nb, D_MODEL), BF16),
        ],
        compiler_params=pltpu.CompilerParams(vmem_limit_bytes=VMEM_LIMIT),
        name="sample_mixer",
    )(x, ck, cv, sp, gattn, win, wpool, pscale, wout, gffn, rwt, rb,
      sinks.astype(F32).reshape(N_HEADS, 1), bias)


def _router_kernel(lg_ref, tri_ref, low_ref, eidx_ref, rank_ref, gate_ref, col_ref, lrank_ref, lpos_ref,
                   tcar_ref, cnt_ref, carry):
    tr = lg_ref.shape[1]

    @pl.when(pl.program_id(0) == 0)
    def _():
        carry[...] = jnp.zeros_like(carry)

    work = lg_ref[...]
    eio = lax.broadcasted_iota(jnp.int32, work.shape, 0).astype(F32)
    sels, vals, idxs = [], [], []
    for _k in range(TOP_K):
        mx = jnp.max(work, axis=0, keepdims=True)
        idx = jnp.min(jnp.where(work == mx, eio, float(N_EXPERTS)), axis=0, keepdims=True)
        sel = eio == idx
        sels.append(sel)
        vals.append(mx)
        idxs.append(idx)
        work = jnp.where(sel, -jnp.inf, work)
    exps = [jnp.exp(vk - vals[0]) for vk in vals]
    tot = exps[0] + exps[1] + exps[2] + exps[3]
    onehot = jnp.zeros(work.shape, F32)
    for sel in sels:
        onehot = onehot + sel.astype(F32)
    before = jnp.dot(onehot.astype(BF16), tri_ref[...], preferred_element_type=F32) + carry[...]
    for kk in range(TOP_K):
        eidx_ref[pl.ds(kk, 1), :] = idxs[kk].astype(jnp.int32)
        gate_ref[pl.ds(kk, 1), :] = exps[kk] / tot
        rk = jnp.sum(jnp.where(sels[kk], before, 0.0), axis=0, keepdims=True)
        rank_ref[pl.ds(kk, 1), :] = rk.astype(jnp.int32)
    for j in range(tr // COMBINE_TILE):
        cols = slice(j * COMBINE_TILE, (j + 1) * COMBINE_TILE)
        tc = before[:, j * COMBINE_TILE:j * COMBINE_TILE + 1]
        tcar_ref[j] = tc.astype(jnp.int32)
        slack = tc - WIN_ALIGN * jnp.floor(tc * (1.0 / WIN_ALIGN))
        local = before[:, cols] - tc
        tile_cnt = jnp.broadcast_to(jnp.sum(onehot[:, cols], axis=1, keepdims=True), local.shape)
        run_start = jnp.dot(low_ref[...], tile_cnt.astype(BF16), preferred_element_type=F32)
        for kk in range(TOP_K):
            selk = sels[kk][:, cols]
            lr = jnp.sum(jnp.where(selk, local, 0.0), axis=0, keepdims=True)
            sl = jnp.sum(jnp.where(selk, slack, 0.0), axis=0, keepdims=True)
            lp = jnp.sum(jnp.where(selk, run_start + local, 0.0), axis=0, keepdims=True)
            lpos_ref[pl.ds(kk, 1), cols] = lp.astype(jnp.int32)
            lrank_ref[pl.ds(kk, 1), cols] = lr.astype(jnp.int32)
            col_ref[pl.ds(kk, 1), cols] = (idxs[kk][:, cols] * float(WIN_ROWS) + sl + lr).astype(jnp.int32)
    carry[...] = carry[...] + jnp.sum(onehot, axis=1, keepdims=True)
    cnt_ref[...] = carry[...].astype(jnp.int32)


def _router(logits_t):
    n = logits_t.shape[1]
    tr = ROUTE_BLOCK
    assert n % tr == 0
    tri = jnp.asarray(np.triu(np.ones((tr, tr), np.float32), k=1), BF16)
    low = jnp.asarray(np.tril(np.ones((N_EXPERTS, N_EXPERTS), np.float32), k=-1), BF16)
    per_tok = pl.BlockSpec((TOP_K, tr), lambda i: (0, i))
    return pl.pallas_call(
        _router_kernel,
        grid=(n // tr,),
        in_specs=[pl.BlockSpec((N_EXPERTS, tr), lambda i: (0, i)),
                  pl.BlockSpec((tr, tr), lambda i: (0, 0)),
                  pl.BlockSpec((N_EXPERTS, N_EXPERTS), lambda i: (0, 0))],
        out_specs=[per_tok, per_tok, per_tok, per_tok, per_tok, per_tok,
                   pl.BlockSpec((tr // COMBINE_TILE, N_EXPERTS, 1), lambda i: (i, 0, 0)),
                   pl.BlockSpec((N_EXPERTS, 1), lambda i: (0, 0))],
        out_shape=[jax.ShapeDtypeStruct((TOP_K, n), jnp.int32),
                   jax.ShapeDtypeStruct((TOP_K, n), jnp.int32),
                   jax.ShapeDtypeStruct((TOP_K, n), F32),
                   jax.ShapeDtypeStruct((TOP_K, n), jnp.int32),
                   jax.ShapeDtypeStruct((TOP_K, n), jnp.int32),
                   jax.ShapeDtypeStruct((TOP_K, n), jnp.int32),
                   jax.ShapeDtypeStruct((n // COMBINE_TILE, N_EXPERTS, 1), jnp.int32),
                   jax.ShapeDtypeStruct((N_EXPERTS, 1), jnp.int32)],
        scratch_shapes=[pltpu.VMEM((N_EXPERTS, 1), F32)],
        compiler_params=pltpu.CompilerParams(dimension_semantics=("arbitrary",)),
        name="router",
    )(logits_t, tri, low)


def _dispatch_kernel(lstart_ref, cnt_ref, dst_ref, h2_ref, lpos_ref, xs_hbm, stg, sem):
    i = pl.program_id(0)
    n_tiles = pl.num_programs(0)
    dt = h2_ref.shape[0]
    rows = dt * TOP_K
    slot = i % 2

    def drain(s):
        pltpu.make_async_copy(stg.at[s], xs_hbm.at[pl.ds(0, rows)], sem.at[s]).wait()

    @pl.when(i >= 2)
    def _():
        drain(slot)

    rid = lax.broadcasted_iota(jnp.int32, (rows, dt), 0)
    hit = rid == lpos_ref[0:1, :]
    for kk in range(1, TOP_K):
        hit = jnp.logical_or(hit, rid == lpos_ref[kk:kk + 1, :])
    perm = jnp.where(hit, 1.0, 0.0).astype(BF16)
    srt = jnp.dot(perm, h2_ref[...], preferred_element_type=F32)
    packed = _pack_bf16_pairs(srt)
    for c in range(PACK_CHUNKS):
        stg[slot, :, c, :] = packed[:, c * LANES:(c + 1) * LANES]

    for e in range(N_EXPERTS):
        n = cnt_ref[i * N_EXPERTS + e]
        src0 = lstart_ref[i * N_EXPERTS + e]
        dst0 = dst_ref[i * N_EXPERTS + e]
        off = jnp.int32(0)
        piece = dt
        while piece >= 1:
            take = (n & piece) != 0

            @pl.when(take)
            def _(off=off, piece=piece):
                pltpu.make_async_copy(stg.at[slot, pl.ds(src0 + off, piece)],
                                      xs_hbm.at[pl.ds(dst0 + off, piece)], sem.at[slot]).start()

            off = off + jnp.where(take, piece, 0)
            piece //= 2

    @pl.when(i == n_tiles - 1)
    def _():
        drain(slot)

        @pl.when(n_tiles >= 2)
        def _():
            drain(1 - slot)


def _dispatch(h2, lpos, lstart, cnt, dst, cap):
    n_tok = h2.shape[0]
    dt = DISPATCH_TILE
    assert n_tok % dt == 0
    grid_spec = pltpu.PrefetchScalarGridSpec(
        num_scalar_prefetch=3,
        grid=(n_tok // dt,),
        in_specs=[pl.BlockSpec((dt, D_MODEL), lambda i, a, b, c: (i, 0)),
                  pl.BlockSpec((TOP_K, dt), lambda i, a, b, c: (0, i)),
                  ],
        out_specs=pl.BlockSpec(memory_space=pl.ANY),
        scratch_shapes=[pltpu.VMEM((2, dt * TOP_K, PACK_CHUNKS, LANES), jnp.uint32),
                        pltpu.SemaphoreType.DMA((2,))],
    )
    return pl.pallas_call(
        _dispatch_kernel,
        grid_spec=grid_spec,
        out_shape=jax.ShapeDtypeStruct((cap, PACK_CHUNKS, LANES), jnp.uint32),
        compiler_params=pltpu.CompilerParams(dimension_semantics=("arbitrary",),
                                             vmem_limit_bytes=VMEM_LIMIT),
        name="dispatch",
    )(lstart, cnt, dst, h2, lpos)


def _expert_kernel(bexp_ref, nvalid_ref, x_ref, wgu_ref, bgu_ref, wd_ref, bd_ref, y_ref, wgu_bf, wd_bf):
    i = pl.program_id(0)
    prev = bexp_ref[jnp.maximum(i - 1, 0)]
    fresh = jnp.logical_or(i == 0, bexp_ref[i] != prev)
    nvalid = nvalid_ref[i]

    @pl.when(jnp.logical_and(fresh, nvalid > 0))
    def _():
        chunk = 32

        def cast_gu(r, c):
            r0 = pl.multiple_of(r * chunk, chunk)
            wgu_bf[pl.ds(r0, chunk), :] = wgu_ref[0, pl.ds(r0, chunk), :].astype(BF16)
            return c

        def cast_d(r, c):
            r0 = pl.multiple_of(r * chunk, chunk)
            wd_bf[pl.ds(r0, chunk), :] = wd_ref[0, pl.ds(r0, chunk), :].astype(BF16)
            return c

        lax.fori_loop(0, D_MODEL // chunk, cast_gu, 0)
        lax.fori_loop(0, D_EXPERT // chunk, cast_d, 0)

    @pl.when(nvalid > 0)
    def _():
        xw = jnp.concatenate([x_ref[:, c, :] for c in range(PACK_CHUNKS)], axis=1)
        xw = jnp.where(lax.broadcasted_iota(jnp.int32, xw.shape, 0) < nvalid, xw, jnp.uint32(0))
        xlo, xhi = _unpack_bf16_pairs(xw)
        half = D_MODEL // 2
        g = (jnp.dot(xlo, wgu_bf[0:half, 0:D_EXPERT], preferred_element_type=F32)
             + jnp.dot(xhi, wgu_bf[half:, 0:D_EXPERT], preferred_element_type=F32)
             + bgu_ref[0, :, 0:D_EXPERT])
        u = (jnp.dot(xlo, wgu_bf[0:half, D_EXPERT:], preferred_element_type=F32)
             + jnp.dot(xhi, wgu_bf[half:, D_EXPERT:], preferred_element_type=F32)
             + bgu_ref[0, :, D_EXPERT:])
        g = jnp.minimum(g, SWIGLU_LIMIT)
        u = jnp.clip(u, -SWIGLU_LIMIT, SWIGLU_LIMIT)
        act = g * (1.0 / (1.0 + jnp.exp(-SWIGLU_ALPHA * g))) * (u + 1.0)
        y = jnp.dot(act.astype(BF16), wd_bf[...], preferred_element_type=F32) + bd_ref[0]
        row = lax.broadcasted_iota(jnp.int32, y.shape, 0)
        y_ref[...] = jnp.where(row < nvalid, y, 0.0).astype(BF16)

    @pl.when(nvalid == 0)
    def _():
        y_ref[...] = jnp.zeros_like(y_ref)


def _experts(xs, block_expert, nvalid, wgu, bgu, wd, bd):
    rb = EXPERT_ROWS
    nblk = xs.shape[0] // rb + 1
    last = xs.shape[0] // rb - 1
    grid_spec = pltpu.PrefetchScalarGridSpec(
        num_scalar_prefetch=2,
        grid=(nblk,),
        in_specs=[
            pl.BlockSpec((rb, PACK_CHUNKS, LANES), lambda i, be, nu: (jnp.minimum(i, last), 0, 0)),
            pl.BlockSpec((1, D_MODEL, 2 * D_EXPERT), lambda i, be, nu: (be[i], 0, 0)),
            pl.BlockSpec((1, 1, 2 * D_EXPERT), lambda i, be, nu: (be[i], 0, 0)),
            pl.BlockSpec((1, D_EXPERT, D_MODEL), lambda i, be, nu: (be[i], 0, 0)),
            pl.BlockSpec((1, 1, D_MODEL), lambda i, be, nu: (be[i], 0, 0)),
        ],
        out_specs=pl.BlockSpec((rb, D_MODEL), lambda i, be, nu: (i, 0)),
        scratch_shapes=[pltpu.VMEM((D_MODEL, 2 * D_EXPERT), BF16),
                        pltpu.VMEM((D_EXPERT, D_MODEL), BF16)],
    )
    return pl.pallas_call(
        _expert_kernel,
        grid_spec=grid_spec,
        out_shape=jax.ShapeDtypeStruct((nblk * rb, D_MODEL), BF16),
        compiler_params=pltpu.CompilerParams(dimension_semantics=("arbitrary",),
                                             vmem_limit_bytes=VMEM_LIMIT),
        name="experts",
    )(block_expert, nvalid, xs, wgu, bgu.reshape(N_EXPERTS, 1, 2 * D_EXPERT), wd,
      bd.reshape(N_EXPERTS, 1, D_MODEL))


def _combine_kernel(n_prompt_tiles, offa_ref, nchunk_ref,
                    x1p_ref, x1s_ref, col_ref, lrank_ref, gate_ref, gfin_ref, yb_hbm,
                    outp_ref, outs_ref, ybuf, acc_ref, sem):
    i = pl.program_id(0)
    n_tiles = pl.num_programs(0)

    def window_copy(tile, chunk, e, slot):
        base = pl.multiple_of(offa_ref[tile * N_EXPERTS + e] + chunk * WIN, WIN_ALIGN)
        return pltpu.make_async_copy(yb_hbm.at[pl.ds(base, WIN_ROWS), :],
                                     ybuf.at[slot, pl.ds(e * WIN_ROWS, WIN_ROWS), :],
                                     sem.at[slot])

    def start_windows(tile, chunk, slot):
        for e in range(N_EXPERTS):
            window_copy(tile, chunk, e, slot).start()

    def wait_windows(tile, chunk, slot):
        for e in range(N_EXPERTS):
            window_copy(tile, chunk, e, slot).wait()

    slot = i % 2

    @pl.when(i == 0)
    def _():
        start_windows(0, 0, 0)

    @pl.when(i + 1 < n_tiles)
    def _():
        start_windows(i + 1, 0, 1 - slot)

    wait_windows(i, 0, slot)
    lane = lax.broadcasted_iota(jnp.int32, (COMBINE_TILE, N_EXPERTS * WIN_ROWS), 1)

    def moe_rows(chunk, buf):
        g = jnp.zeros(lane.shape, F32)
        for kk in range(TOP_K):
            lr = lrank_ref[:, kk:kk + 1]
            in_chunk = jnp.logical_and(lr >= chunk * WIN, lr < chunk * WIN + WIN)
            colk = jnp.where(in_chunk, col_ref[:, kk:kk + 1] - chunk * WIN, -1)
            g = jnp.where(lane == colk, gate_ref[:, kk:kk + 1], g)
        hi = g.astype(BF16)
        lo = (g - hi.astype(F32)).astype(BF16)
        y = ybuf[buf]
        return (jnp.dot(hi, y, preferred_element_type=F32) + jnp.dot(lo, y, preferred_element_type=F32))

    acc_ref[...] = moe_rows(0, slot)

    def extra_chunk(j, c):
        start_windows(i, j, 2)
        wait_windows(i, j, 2)
        acc_ref[...] += moe_rows(j, 2)
        return c

    lax.fori_loop(1, nchunk_ref[i], extra_chunk, 0)

    @pl.when(i < n_prompt_tiles)
    def _():
        outp_ref[...] = _rms(x1p_ref[...] + acc_ref[...], gfin_ref[...])

    @pl.when(i >= n_prompt_tiles)
    def _():
        outs_ref[...] = _rms(x1s_ref[...] + acc_ref[...], gfin_ref[...])


def _combine(x1_p, x1_s, col_t, lrank_t, gates_t, gfin, yb, offa, nchunk):
    ct = COMBINE_TILE
    n_p, n_s = x1_p.shape[0] // ct, x1_s.shape[0] // ct
    assert x1_p.shape[0] % ct == 0 and x1_s.shape[0] % ct == 0 and n_s >= 1
    tok_spec = pl.BlockSpec((ct, TOP_K), lambda i, o, c: (i, 0))
    grid_spec = pltpu.PrefetchScalarGridSpec(
        num_scalar_prefetch=2,
        grid=(n_p + n_s,),
        in_specs=[
            pl.BlockSpec((ct, D_MODEL), lambda i, o, c: (jnp.minimum(i, n_p - 1), 0)),
            pl.BlockSpec((ct, D_MODEL), lambda i, o, c: (jnp.maximum(i - n_p, 0), 0)),
            tok_spec, tok_spec, tok_spec,
            pl.BlockSpec((1, D_MODEL), lambda i, o, c: (0, 0)),
            pl.BlockSpec(memory_space=pl.ANY),
        ],
        out_specs=[
            pl.BlockSpec((ct, D_MODEL), lambda i, o, c: (jnp.minimum(i, n_p - 1), 0)),
            pl.BlockSpec((ct, D_MODEL), lambda i, o, c: (jnp.maximum(i - n_p, 0), 0)),
        ],
        scratch_shapes=[pltpu.VMEM((3, N_EXPERTS * WIN_ROWS, D_MODEL), BF16),
                        pltpu.VMEM((ct, D_MODEL), F32),
                        pltpu.SemaphoreType.DMA((3,))],
    )
    return pl.pallas_call(
        functools.partial(_combine_kernel, n_p),
        grid_spec=grid_spec,
        out_shape=[jax.ShapeDtypeStruct(x1_p.shape, F32), jax.ShapeDtypeStruct(x1_s.shape, F32)],
        compiler_params=pltpu.CompilerParams(dimension_semantics=("arbitrary",),
                                             vmem_limit_bytes=VMEM_LIMIT),
        name="combine",
    )(offa, nchunk, x1_p, x1_s, col_t, lrank_t, gates_t, gfin, yb)


def kernel(x_prompt, x_sample, cache_k, cache_v, state_pool, meta_tokens, norm_attn, w_in, attn_sinks,
           w_pool, pool_scale, w_out, norm_ffn, router_w, router_b, w_gate_up, b_gate_up, w_down, b_down,
           norm_final):
    assert w_in.shape[0] == 1, "single-layer trunk"
    bsz, seq, _ = x_prompt.shape
    assert bsz == 1
    nb = x_sample.shape[0]
    n_tok = seq + nb
    gattn = norm_attn[0].reshape(1, D_MODEL)
    gffn = norm_ffn[0].reshape(1, D_MODEL)
    win = w_in[0].astype(BF16)
    wpool = w_pool[0].astype(BF16)
    wout = w_out[0].astype(BF16)
    pscale = pool_scale[0].reshape(1, POOL_WIDTH)
    rw_t = router_w[0].T
    rw_hi = rw_t.astype(BF16)
    rwt = jnp.stack([rw_hi, (rw_t - rw_hi.astype(F32)).astype(BF16)])
    rb = router_b[0].reshape(N_EXPERTS, 1)
    sinks = attn_sinks[0]

    ck = cache_k[0].reshape(nb, N_META + WINDOW, KV_WIDTH)
    cv = cache_v[0].reshape(nb, N_META + WINDOW, KV_WIDTH)
    (x1_s, h2_s, lgt_s, knew, vnew, pnew) = _sample_mixer(
        x_sample[:, 0], ck, cv, state_pool[0], gattn, win, wpool, pscale, wout, gffn, rwt, rb, sinks)
    (x1_p, h2_all, lgt_all, kmeta, vmeta, ktail, vtail, ptail) = _prompt_mixer(
        x_prompt[0], meta_tokens, gattn, win, wpool, pscale, wout, gffn, rwt, rb, sinks, h2_s, lgt_s)

    _eidx, _rank, gates, col, lrank, lpos, tcar, counts = _router(lgt_all)
    counts = counts[:, 0]
    tcar = tcar[:, :, 0]
    rbk = EXPERT_ROWS
    padded = (counts + rbk - 1) // rbk * rbk
    pad_end = jnp.cumsum(padded)
    pad_start = (pad_end - padded).astype(jnp.int32)
    nblk = -(-(n_tok * TOP_K) // rbk) + N_EXPERTS
    cap = nblk * rbk
    block_start = jnp.arange(nblk + 1, dtype=jnp.int32) * rbk
    block_expert = jnp.minimum(jnp.sum(pad_end[None, :] <= block_start[:, None], axis=1),
                               N_EXPERTS - 1).astype(jnp.int32)
    nvalid = jnp.clip(counts[block_expert] - (block_start - pad_start[block_expert]), 0, rbk)
    nvalid = jnp.where(block_start < pad_end[-1], nvalid, 0).astype(jnp.int32)
    last_e = block_expert[jnp.maximum(pad_end[-1] // rbk - 1, 0)]
    block_expert = jnp.where(block_start < pad_end[-1], block_expert, last_e)

    run_len = jnp.concatenate([tcar[1:], counts[None, :]], axis=0) - tcar
    run_lstart = jnp.cumsum(run_len, axis=1) - run_len
    run_dst = pad_start[None, :] + tcar
    flat = lambda a: a.astype(jnp.int32).reshape(-1)
    xs = _dispatch(h2_all, lpos, flat(run_lstart), flat(run_len), flat(run_dst), cap)
    yb = _experts(xs, block_expert, nvalid, w_gate_up[0], b_gate_up[0], w_down[0], b_down[0])

    offa = (pad_start[None, :] + (tcar - tcar % WIN_ALIGN)).astype(jnp.int32).reshape(-1)
    nchunk = jnp.maximum(jnp.max((run_len + WIN - 1) // WIN, axis=1), 1).astype(jnp.int32)
    gfin = norm_final.reshape(1, D_MODEL)
    y_prompt, y_sample = _combine(x1_p, x1_s, col.T, lrank.T, gates.T, gfin, yb, offa, nchunk)

    kv_shape = (1, 1, N_META + WINDOW, N_KV_HEADS, HEAD_DIM)
    new_k_p = jnp.concatenate([kmeta, ktail], axis=0).reshape(kv_shape)
    new_v_p = jnp.concatenate([vmeta, vtail], axis=0).reshape(kv_shape)
    new_pool_p = ptail[16 - POOL_STATE:].reshape(1, 1, POOL_STATE, POOL_WIDTH)
    new_k_s = jnp.concatenate([ck[:, :N_META], ck[:, N_META + 1:], knew[:, None]], axis=1).reshape(
        (1, nb, N_META + WINDOW, N_KV_HEADS, HEAD_DIM))
    new_v_s = jnp.concatenate([cv[:, :N_META], cv[:, N_META + 1:], vnew[:, None]], axis=1).reshape(
        (1, nb, N_META + WINDOW, N_KV_HEADS, HEAD_DIM))
    new_pool_s = jnp.concatenate([state_pool[0][:, 1:], pnew[:, None]], axis=1)[None]
    return (y_prompt[None], y_sample[:, None], new_k_p, new_v_p, new_pool_p, new_k_s, new_v_s, new_pool_s)
```

```python
import functools

import jax
import jax.numpy as jnp
import numpy as np
from jax import lax
from jax.experimental import pallas as pl
from jax.experimental.pallas import tpu as pltpu

F32 = jnp.float32
BF16 = jnp.bfloat16

D_MODEL = 1024
N_META = 16
N_HEADS = 8
HEAD_DIM = 64
N_KV_HEADS = 2
GQA_GROUP = N_HEADS // N_KV_HEADS
ATTN_WIDTH = N_HEADS * HEAD_DIM
KV_WIDTH = N_KV_HEADS * HEAD_DIM
WINDOW = 128
POOL_WIDTH = D_MODEL - ATTN_WIDTH
POOL_WINDOWS = (2, 4, 8, 16)
POOL_GROUP_DIM = POOL_WIDTH // len(POOL_WINDOWS)
POOL_STATE = max(POOL_WINDOWS) - 1
N_EXPERTS = 32
TOP_K = 4
D_EXPERT = D_MODEL
SWIGLU_ALPHA = 1.702
SWIGLU_LIMIT = 7.0
NORM_EPS = 1e-5
PAST_LEN = 16384

LANES = 128
QSUB = 64
KEYS_SUB = QSUB + WINDOW
META_PAD = 64
NKEY = META_PAD + KEYS_SUB
MASKED = -1e30
PROMPT_BLOCK = 512
ROUTE_BLOCK = 384
EXPERT_ROWS = 512
DISPATCH_TILE = 128
PACK_CHUNKS = D_MODEL // 2 // LANES
COMBINE_TILE = 128
WIN = 32
WIN_ALIGN = 16
WIN_ROWS = WIN + WIN_ALIGN
VMEM_LIMIT = 56 * 1024 * 1024


def _rms(x, g):
    return x * lax.rsqrt(jnp.mean(x * x, axis=-1, keepdims=True) + NORM_EPS) * g


def _router_logits(rwt_ref, h2, h2_hi):
    nt = (((1,), (1,)), ((), ()))
    h2_lo = (h2 - h2_hi.astype(F32)).astype(BF16)
    return (lax.dot_general(rwt_ref[0], h2_hi, nt, preferred_element_type=F32)
            + lax.dot_general(rwt_ref[0], h2_lo, nt, preferred_element_type=F32)
            + lax.dot_general(rwt_ref[1], h2_hi, nt, preferred_element_type=F32))


def _dup_halves(a):
    lane = lax.broadcasted_iota(jnp.int32, a.shape, 1)
    r = pltpu.roll(a, HEAD_DIM, axis=1)
    lo = lane < HEAD_DIM
    return jnp.where(lo, a, r), jnp.where(lo, r, a)


def _pool_means(pext_ref, n):
    outs = []
    for gi, w in enumerate(POOL_WINDOWS):
        xg = pext_ref[:, gi * POOL_GROUP_DIM:(gi + 1) * POOL_GROUP_DIM]
        s = xg
        sh = 1
        while sh < w:
            s = s + pltpu.roll(s, sh, axis=0)
            sh *= 2
        outs.append(s[16:] * (1.0 / w) - xg[16:])
    return outs


def _pack_bf16_pairs(h):
    m = h.shape[1] // 2
    lo = pltpu.bitcast(h[:, :m].astype(BF16).astype(F32), jnp.uint32)
    hi = pltpu.bitcast(h[:, m:].astype(BF16).astype(F32), jnp.uint32)
    return lax.shift_right_logical(lo, jnp.uint32(16)) | (hi & jnp.uint32(0xFFFF0000))


def _unpack_bf16_pairs(w):
    lo = pltpu.bitcast(lax.shift_left(w, jnp.uint32(16)), F32).astype(BF16)
    hi = pltpu.bitcast(w & jnp.uint32(0xFFFF0000), F32).astype(BF16)
    return lo, hi


def _prompt_kernel(x_ref, meta_ref, gattn_ref, win_ref, wpool_ref, pscale_ref, wout_ref, gffn_ref,
                   rwt_ref, rb_ref, sink_ref, tbl_ref, tail_h2_ref, tail_lgt_ref,
                   x1_ref, h2_ref, lgt_ref, kmeta_ref, vmeta_ref, ktail_ref, vtail_ref, ptail_ref,
                   k2buf, v2buf, km2, vm2, qbuf, obuf, pext):
    pid = pl.program_id(0)
    n_main = pl.num_programs(0) - 1
    refs = (x_ref, meta_ref, gattn_ref, win_ref, wpool_ref, pscale_ref, wout_ref, gffn_ref,
            rwt_ref, rb_ref, sink_ref, tbl_ref,
            x1_ref, h2_ref, lgt_ref, kmeta_ref, vmeta_ref, ktail_ref, vtail_ref, ptail_ref,
            k2buf, v2buf, km2, vm2, qbuf, obuf, pext)

    @pl.when(pid < n_main)
    def _():
        _prompt_block(*refs)

    @pl.when(pid == n_main)
    def _():
        h2_ref[0:tail_h2_ref.shape[0], :] = tail_h2_ref[...]
        lgt_ref[:, 0:tail_lgt_ref.shape[1]] = tail_lgt_ref[...]


def _prompt_block(x_ref, meta_ref, gattn_ref, win_ref, wpool_ref, pscale_ref, wout_ref, gffn_ref,
                  rwt_ref, rb_ref, sink_ref, tbl_ref,
                  x1_ref, h2_ref, lgt_ref, kmeta_ref, vmeta_ref, ktail_ref, vtail_ref, ptail_ref,
                  k2buf, v2buf, km2, vm2, qbuf, obuf, pext):
    tb = x_ref.shape[0]
    pid = pl.program_id(0)

    @pl.when(pid == 0)
    def _():
        hm = _rms(meta_ref[...], gattn_ref[...]).astype(BF16)
        km = jnp.dot(hm, win_ref[:, ATTN_WIDTH:ATTN_WIDTH + KV_WIDTH], preferred_element_type=F32)
        vm = jnp.dot(hm, win_ref[:, ATTN_WIDTH + KV_WIDTH:ATTN_WIDTH + 2 * KV_WIDTH],
                     preferred_element_type=F32)
        pm = jnp.dot(hm, win_ref[:, ATTN_WIDTH + 2 * KV_WIDTH:], preferred_element_type=F32)
        kmeta_ref[...] = km
        vmeta_ref[...] = vm
        zpad = jnp.zeros((META_PAD - N_META, LANES), F32)
        k0, k1 = _dup_halves(jnp.concatenate([km, zpad], axis=0))
        v0, v1 = _dup_halves(jnp.concatenate([vm, zpad], axis=0))
        km2[0] = k0.astype(BF16)
        km2[1] = k1.astype(BF16)
        vm2[0, :, 0:LANES] = v0.astype(BF16)
        vm2[1, :, 0:LANES] = v1.astype(BF16)
        vm2[:, :, LANES:] = jnp.ones((N_KV_HEADS, META_PAD, LANES), BF16)
        k2buf[:, 0:WINDOW, :] = jnp.zeros((2, WINDOW, LANES), BF16)
        v2buf[:, 0:WINDOW, 0:LANES] = jnp.zeros((2, WINDOW, LANES), BF16)
        v2buf[:, :, LANES:] = jnp.ones((N_KV_HEADS, WINDOW + tb, LANES), BF16)
        pext[0:16, :] = pm

    h = _rms(x_ref[...], gattn_ref[...]).astype(BF16)
    q = jnp.dot(h, win_ref[:, 0:ATTN_WIDTH], preferred_element_type=F32) * (HEAD_DIM ** -0.5)
    lane_t = lax.broadcasted_iota(jnp.int32, (tb, LANES), 1)
    for c in range(N_HEADS // 2):
        tile = q[:, c * LANES:(c + 1) * LANES]
        for a in range(2):
            keep = (lane_t < HEAD_DIM) if a == 0 else (lane_t >= HEAD_DIM)
            piece = jnp.where(keep, tile, 0.0).astype(BF16).reshape(tb // QSUB, QSUB, LANES)
            row = ((c % 2) * 2 + a) * QSUB
            qbuf[c // 2, :, row:row + QSUB, :] = piece
    k = jnp.dot(h, win_ref[:, ATTN_WIDTH:ATTN_WIDTH + KV_WIDTH], preferred_element_type=F32)
    v = jnp.dot(h, win_ref[:, ATTN_WIDTH + KV_WIDTH:ATTN_WIDTH + 2 * KV_WIDTH], preferred_element_type=F32)
    p = jnp.dot(h, win_ref[:, ATTN_WIDTH + 2 * KV_WIDTH:], preferred_element_type=F32)
    ktail_ref[...] = k[tb - WINDOW:]
    vtail_ref[...] = v[tb - WINDOW:]
    ptail_ref[...] = p[tb - 16:]
    k0, k1 = _dup_halves(k)
    v0, v1 = _dup_halves(v)
    k2buf[0, WINDOW:, :] = k0.astype(BF16)
    k2buf[1, WINDOW:, :] = k1.astype(BF16)
    v2buf[0, WINDOW:, 0:LANES] = v0.astype(BF16)
    v2buf[1, WINDOW:, 0:LANES] = v1.astype(BF16)
    pext[16:, :] = p

    lane_q = lax.broadcasted_iota(jnp.int32, (QSUB, LANES), 1)
    lo_q = lane_q < HEAD_DIM

    for u in range(tb // QSUB):
        r0 = u * QSUB
        sel = jnp.where(pid == 0, u + 1, 0) if u < WINDOW // QSUB else 0
        for g in range(N_KV_HEADS):
            qm = qbuf[g, u]
            kwin = jnp.concatenate([km2[g], k2buf[g, r0:r0 + KEYS_SUB, :]], axis=0)
            vwin = jnp.concatenate([vm2[g], v2buf[g, r0:r0 + KEYS_SUB, :]], axis=0)
            s = lax.dot_general(qm, kwin, (((1,), (1,)), ((), ())), preferred_element_type=F32)
            s = s + tbl_ref[sel, g]
            sink = sink_ref[g]
            m = jnp.maximum(jnp.max(s, axis=1, keepdims=True), sink)
            e = jnp.exp(s - m).astype(BF16)
            r = jnp.dot(e, vwin, preferred_element_type=F32)
            o = r[:, 0:LANES] / (r[:, LANES:] + jnp.exp(sink - m))
            o0 = jnp.where(lo_q, o[0:QSUB], o[QSUB:2 * QSUB])
            o1 = jnp.where(lo_q, o[2 * QSUB:3 * QSUB], o[3 * QSUB:])
            obuf[r0:r0 + QSUB, (2 * g) * LANES:(2 * g + 1) * LANES] = o0.astype(BF16)
            obuf[r0:r0 + QSUB, (2 * g + 1) * LANES:(2 * g + 2) * LANES] = o1.astype(BF16)

    pooled = _pool_means(pext, tb)
    for gi in range(len(POOL_WINDOWS)):
        y = jnp.dot(pooled[gi].astype(BF16), wpool_ref[gi], preferred_element_type=F32)
        y = y * pscale_ref[:, gi * POOL_GROUP_DIM:(gi + 1) * POOL_GROUP_DIM]
        obuf[:, ATTN_WIDTH + gi * POOL_GROUP_DIM:ATTN_WIDTH + (gi + 1) * POOL_GROUP_DIM] = y.astype(BF16)

    k2buf[:, 0:WINDOW, :] = k2buf[:, tb:tb + WINDOW, :]
    v2buf[:, 0:WINDOW, 0:LANES] = v2buf[:, tb:tb + WINDOW, 0:LANES]
    pext[0:16, :] = pext[tb:tb + 16, :]

    x1 = x_ref[...] + jnp.dot(obuf[...], wout_ref[...], preferred_element_type=F32)
    x1_ref[...] = x1
    h2 = _rms(x1, gffn_ref[...])
    h2_hi = h2.astype(BF16)
    h2_ref[...] = h2_hi
    lgt_ref[...] = _router_logits(rwt_ref, h2, h2_hi) + rb_ref[...]


def _attn_tables(sinks):
    i = np.arange(QSUB)[:, None]
    j = np.arange(NKEY)[None, :]
    jb = j - META_PAD
    rel = i + WINDOW - jb
    band_ok = (jb >= 0) & (rel >= 0) & (rel <= WINDOW)
    meta_ok = (j < N_META) & (i >= 0)
    slopes = np.exp2(-8.0 * np.arange(1, N_HEADS + 1) / N_HEADS)
    tbl = np.empty((3, N_KV_HEADS, GQA_GROUP * QSUB, NKEY), np.float32)
    for var in range(3):
        ok = band_ok if var == 0 else band_ok & (jb >= WINDOW - (var - 1) * QSUB)
        for g in range(N_KV_HEADS):
            for a in range(GQA_GROUP):
                hd = g * GQA_GROUP + a
                bias = np.where(ok, -slopes[hd] * rel, MASKED)
                bias = np.where(meta_ok, 0.0, bias)
                tbl[var, g, a * QSUB:(a + 1) * QSUB] = bias
    sink_col = jnp.repeat(sinks.astype(F32).reshape(N_KV_HEADS, GQA_GROUP, 1), QSUB, axis=2)
    return jnp.asarray(tbl), sink_col.reshape(N_KV_HEADS, GQA_GROUP * QSUB, 1)


def _prompt_mixer(x, meta, gattn, win, wpool, pscale, wout, gffn, rwt, rb, sinks, tail_h2, tail_lgt):
    seq = x.shape[0]
    tb = PROMPT_BLOCK
    n_tail = tail_h2.shape[0]
    assert seq % tb == 0 and tb % WINDOW == 0 and n_tail <= tb
    nblk = seq // tb
    n_tok = seq + n_tail
    tbl, sink_col = _attn_tables(sinks)
    full = lambda *shape: pl.BlockSpec(shape, lambda i: (0,) * len(shape))
    main = lambda i: (jnp.minimum(i, nblk - 1), 0)
    in_width = win.shape[1]
    return pl.pallas_call(
        _prompt_kernel,
        grid=(nblk + 1,),
        in_specs=[
            pl.BlockSpec((tb, D_MODEL), main),
            full(N_META, D_MODEL), full(1, D_MODEL), full(D_MODEL, in_width),
            full(len(POOL_WINDOWS), POOL_GROUP_DIM, POOL_GROUP_DIM), full(1, POOL_WIDTH),
            full(D_MODEL, D_MODEL), full(1, D_MODEL), full(2, N_EXPERTS, D_MODEL), full(N_EXPERTS, 1),
            full(N_KV_HEADS, GQA_GROUP * QSUB, 1), full(3, N_KV_HEADS, GQA_GROUP * QSUB, NKEY),
            full(n_tail, D_MODEL), full(N_EXPERTS, n_tail),
        ],
        out_specs=[
            pl.BlockSpec((tb, D_MODEL), main),
            pl.BlockSpec((tb, D_MODEL), lambda i: (i, 0)),
            pl.BlockSpec((N_EXPERTS, tb), lambda i: (0, i)),
            full(N_META, KV_WIDTH), full(N_META, KV_WIDTH),
            full(WINDOW, KV_WIDTH), full(WINDOW, KV_WIDTH), full(16, POOL_WIDTH),
        ],
        out_shape=[
            jax.ShapeDtypeStruct((seq, D_MODEL), F32),
            jax.ShapeDtypeStruct((n_tok, D_MODEL), BF16),
            jax.ShapeDtypeStruct((N_EXPERTS, n_tok), F32),
            jax.ShapeDtypeStruct((N_META, KV_WIDTH), F32),
            jax.ShapeDtypeStruct((N_META, KV_WIDTH), F32),
            jax.ShapeDtypeStruct((WINDOW, KV_WIDTH), F32),
            jax.ShapeDtypeStruct((WINDOW, KV_WIDTH), F32),
            jax.ShapeDtypeStruct((16, POOL_WIDTH), F32),
        ],
        scratch_shapes=[
            pltpu.VMEM((N_KV_HEADS, WINDOW + tb, LANES), BF16),
            pltpu.VMEM((N_KV_HEADS, WINDOW + tb, 2 * LANES), BF16),
            pltpu.VMEM((N_KV_HEADS, META_PAD, LANES), BF16),
            pltpu.VMEM((N_KV_HEADS, META_PAD, 2 * LANES), BF16),
            pltpu.VMEM((N_KV_HEADS, tb // QSUB, GQA_GROUP * QSUB, LANES), BF16),
            pltpu.VMEM((tb, D_MODEL), BF16),
            pltpu.VMEM((16 + tb, POOL_WIDTH), F32),
        ],
        compiler_params=pltpu.CompilerParams(dimension_semantics=("arbitrary",),
                                             vmem_limit_bytes=VMEM_LIMIT),
        name="prompt_mixer",
    )(x, meta, gattn, win, wpool, pscale, wout, gffn, rwt, rb, sink_col, tbl, tail_h2, tail_lgt)


def _sample_kernel(x_ref, ck_ref, cv_ref, sp_ref, gattn_ref, win_ref, wpool_ref, pscale_ref, wout_ref,
                   gffn_ref, rwt_ref, rb_ref, sinkc_ref, bias_ref,
                   x1_ref, h2_ref, lgt_ref, knew_ref, vnew_ref, pnew_ref,
                   qm_buf, r_buf, obuf):
    nb = x_ref.shape[0]
    x = x_ref[...]
    h = _rms(x, gattn_ref[...]).astype(BF16)
    q = jnp.dot(h, win_ref[:, 0:ATTN_WIDTH], preferred_element_type=F32) * (HEAD_DIM ** -0.5)
    k = jnp.dot(h, win_ref[:, ATTN_WIDTH:ATTN_WIDTH + KV_WIDTH], preferred_element_type=F32)
    v = jnp.dot(h, win_ref[:, ATTN_WIDTH + KV_WIDTH:ATTN_WIDTH + 2 * KV_WIDTH], preferred_element_type=F32)
    p = jnp.dot(h, win_ref[:, ATTN_WIDTH + 2 * KV_WIDTH:], preferred_element_type=F32)
    knew_ref[...] = k
    vnew_ref[...] = v
    pnew_ref[...] = p

    lane = lax.broadcasted_iota(jnp.int32, (nb, LANES), 1)
    lo = lane < HEAD_DIM
    for hd in range(N_HEADS):
        tile = q[:, (hd // 2) * LANES:(hd // 2 + 1) * LANES]
        if (hd % 2) != (hd // GQA_GROUP):
            tile = pltpu.roll(tile, HEAD_DIM, axis=1)
        keep_lo = (hd // GQA_GROUP) == 0
        qm_buf[:, hd, :] = jnp.where(lo if keep_lo else jnp.logical_not(lo), tile, 0.0)

    def per_batch(b, carry):
        qm = qm_buf[b]
        kb = ck_ref[b].astype(BF16)
        vb = cv_ref[b].astype(BF16)
        s = lax.dot_general(qm.astype(BF16), kb, (((1,), (1,)), ((), ())), preferred_element_type=F32)
        s = s + bias_ref[...]
        kn = knew_ref[pl.ds(b, 1), :]
        vn = vnew_ref[pl.ds(b, 1), :]
        s_self = jnp.sum(qm * kn, axis=1, keepdims=True)
        sink = sinkc_ref[...]
        m = jnp.maximum(jnp.maximum(jnp.max(s, axis=1, keepdims=True), s_self), sink)
        e = jnp.exp(s - m)
        e_self = jnp.exp(s_self - m)
        den = jnp.sum(e, axis=1, keepdims=True) + e_self + jnp.exp(sink - m)
        r = jnp.dot(e.astype(BF16), vb, preferred_element_type=F32)
        r = r + e_self * vn
        r_buf[b] = r / den
        return carry

    lax.fori_loop(0, nb, per_batch, 0)

    for c in range(N_HEADS // 2):
        halves = []
        for a in range(2):
            hd = 2 * c + a
            t = r_buf[:, hd, :]
            if (hd // GQA_GROUP) != a:
                t = pltpu.roll(t, HEAD_DIM, axis=1)
            halves.append(t)
        obuf[:, c * LANES:(c + 1) * LANES] = jnp.where(lo, halves[0], halves[1]).astype(BF16)

    for gi, w in enumerate(POOL_WINDOWS):
        cols = slice(gi * POOL_GROUP_DIM, (gi + 1) * POOL_GROUP_DIM)
        pg = p[:, cols]
        acc = pg
        for d in range(1, w):
            acc = acc + sp_ref[:, POOL_STATE - d, cols]
        pooled = acc * (1.0 / w) - pg
        y = jnp.dot(pooled.astype(BF16), wpool_ref[gi], preferred_element_type=F32) * pscale_ref[:, cols]
        obuf[:, ATTN_WIDTH + gi * POOL_GROUP_DIM:ATTN_WIDTH + (gi + 1) * POOL_GROUP_DIM] = y.astype(BF16)

    x1 = x + jnp.dot(obuf[...], wout_ref[...], preferred_element_type=F32)
    x1_ref[...] = x1
    h2 = _rms(x1, gffn_ref[...])
    h2_hi = h2.astype(BF16)
    h2_ref[...] = h2_hi
    lgt_ref[...] = _router_logits(rwt_ref, h2, h2_hi) + rb_ref[...]


def _sample_mixer(x, ck, cv, sp, gattn, win, wpool, pscale, wout, gffn, rwt, rb, sinks):
    nb = x.shape[0]
    rows = ck.shape[1]
    slopes = np.exp2(-8.0 * np.arange(1, N_HEADS + 1) / N_HEADS)
    dist = np.concatenate([np.zeros(N_META), WINDOW - np.arange(WINDOW)])
    bias = jnp.asarray((-slopes[:, None] * dist[None, :]).astype(np.float32))
    vm = pl.BlockSpec(memory_space=pltpu.VMEM)
    return pl.pallas_call(
        _sample_kernel,
        in_specs=[vm] * 14,
        out_specs=[vm] * 6,
        out_shape=[
            jax.ShapeDtypeStruct((nb, D_MODEL), F32),
            jax.ShapeDtypeStruct((nb, D_MODEL), BF16),
            jax.ShapeDtypeStruct((N_EXPERTS, nb), F32),
            jax.ShapeDtypeStruct((nb, KV_WIDTH), F32),
            jax.ShapeDtypeStruct((nb, KV_WIDTH), F32),
            jax.ShapeDtypeStruct((nb, POOL_WIDTH), F32),
        ],
        scratch_shapes=[
            pltpu.VMEM((nb, N_HEADS, LANES), F32),
            pltpu.VMEM((nb, N_HEADS, LANES), F32),
            pltpu.VMEM((nb, D_MODEL), BF16),
        ],
        compiler_params=pltpu.CompilerParams(vmem_limit_bytes=VMEM_LIMIT),
        name="sample_mixer",
    )(x, ck, cv, sp, gattn, win, wpool, pscale, wout, gffn, rwt, rb,
      sinks.astype(F32).reshape(N_HEADS, 1), bias)


def _router_kernel(lg_ref, tri_ref, low_ref, eidx_ref, rank_ref, gate_ref, col_ref, lrank_ref, lpos_ref,
                   tcar_ref, cnt_ref, carry):
    tr = lg_ref.shape[1]

    @pl.when(pl.program_id(0) == 0)
    def _():
        carry[...] = jnp.zeros_like(carry)

    work = lg_ref[...]
    eio = lax.broadcasted_iota(jnp.int32, work.shape, 0).astype(F32)
    sels, vals, idxs = [], [], []
    for _k in range(TOP_K):
        mx = jnp.max(work, axis=0, keepdims=True)
        idx = jnp.min(jnp.where(work == mx, eio, float(N_EXPERTS)), axis=0, keepdims=True)
        sel = eio == idx
        sels.append(sel)
        vals.append(mx)
        idxs.append(idx)
        work = jnp.where(sel, -jnp.inf, work)
    exps = [jnp.exp(vk - vals[0]) for vk in vals]
    tot = exps[0] + exps[1] + exps[2] + exps[3]
    onehot = jnp.zeros(work.shape, F32)
    for sel in sels:
        onehot = onehot + sel.astype(F32)
    before = jnp.dot(onehot.astype(BF16), tri_ref[...], preferred_element_type=F32) + carry[...]
    for kk in range(TOP_K):
        eidx_ref[pl.ds(kk, 1), :] = idxs[kk].astype(jnp.int32)
        gate_ref[pl.ds(kk, 1), :] = exps[kk] / tot
        rk = jnp.sum(jnp.where(sels[kk], before, 0.0), axis=0, keepdims=True)
        rank_ref[pl.ds(kk, 1), :] = rk.astype(jnp.int32)
    for j in range(tr // COMBINE_TILE):
        cols = slice(j * COMBINE_TILE, (j + 1) * COMBINE_TILE)
        tc = before[:, j * COMBINE_TILE:j * COMBINE_TILE + 1]
        tcar_ref[j] = tc.astype(jnp.int32)
        slack = tc - WIN_ALIGN * jnp.floor(tc * (1.0 / WIN_ALIGN))
        local = before[:, cols] - tc
        tile_cnt = jnp.broadcast_to(jnp.sum(onehot[:, cols], axis=1, keepdims=True), local.shape)
        run_start = jnp.dot(low_ref[...], tile_cnt.astype(BF16), preferred_element_type=F32)
        for kk in range(TOP_K):
            selk = sels[kk][:, cols]
            lr = jnp.sum(jnp.where(selk, local, 0.0), axis=0, keepdims=True)
            sl = jnp.sum(jnp.where(selk, slack, 0.0), axis=0, keepdims=True)
            lp = jnp.sum(jnp.where(selk, run_start + local, 0.0), axis=0, keepdims=True)
            lpos_ref[pl.ds(kk, 1), cols] = lp.astype(jnp.int32)
            lrank_ref[pl.ds(kk, 1), cols] = lr.astype(jnp.int32)
            col_ref[pl.ds(kk, 1), cols] = (idxs[kk][:, cols] * float(WIN_ROWS) + sl + lr).astype(jnp.int32)
    carry[...] = carry[...] + jnp.sum(onehot, axis=1, keepdims=True)
    cnt_ref[...] = carry[...].astype(jnp.int32)


def _router(logits_t):
    n = logits_t.shape[1]
    tr = ROUTE_BLOCK
    assert n % tr == 0
    tri = jnp.asarray(np.triu(np.ones((tr, tr), np.float32), k=1), BF16)
    low = jnp.asarray(np.tril(np.ones((N_EXPERTS, N_EXPERTS), np.float32), k=-1), BF16)
    per_tok = pl.BlockSpec((TOP_K, tr), lambda i: (0, i))
    return pl.pallas_call(
        _router_kernel,
        grid=(n // tr,),
        in_specs=[pl.BlockSpec((N_EXPERTS, tr), lambda i: (0, i)),
                  pl.BlockSpec((tr, tr), lambda i: (0, 0)),
                  pl.BlockSpec((N_EXPERTS, N_EXPERTS), lambda i: (0, 0))],
        out_specs=[per_tok, per_tok, per_tok, per_tok, per_tok, per_tok,
                   pl.BlockSpec((tr // COMBINE_TILE, N_EXPERTS, 1), lambda i: (i, 0, 0)),
                   pl.BlockSpec((N_EXPERTS, 1), lambda i: (0, 0))],
        out_shape=[jax.ShapeDtypeStruct((TOP_K, n), jnp.int32),
                   jax.ShapeDtypeStruct((TOP_K, n), jnp.int32),
                   jax.ShapeDtypeStruct((TOP_K, n), F32),
                   jax.ShapeDtypeStruct((TOP_K, n), jnp.int32),
                   jax.ShapeDtypeStruct((TOP_K, n), jnp.int32),
                   jax.ShapeDtypeStruct((TOP_K, n), jnp.int32),
                   jax.ShapeDtypeStruct((n // COMBINE_TILE, N_EXPERTS, 1), jnp.int32),
                   jax.ShapeDtypeStruct((N_EXPERTS, 1), jnp.int32)],
        scratch_shapes=[pltpu.VMEM((N_EXPERTS, 1), F32)],
        compiler_params=pltpu.CompilerParams(dimension_semantics=("arbitrary",)),
        name="router",
    )(logits_t, tri, low)


def _dispatch_kernel(lstart_ref, cnt_ref, dst_ref, h2_ref, lpos_ref, xs_hbm, stg, sem):
    i = pl.program_id(0)
    n_tiles = pl.num_programs(0)
    dt = h2_ref.shape[0]
    rows = dt * TOP_K
    slot = i % 2

    def drain(s):
        pltpu.make_async_copy(stg.at[s], xs_hbm.at[pl.ds(0, rows)], sem.at[s]).wait()

    @pl.when(i >= 2)
    def _():
        drain(slot)

    rid = lax.broadcasted_iota(jnp.int32, (rows, dt), 0)
    hit = rid == lpos_ref[0:1, :]
    for kk in range(1, TOP_K):
        hit = jnp.logical_or(hit, rid == lpos_ref[kk:kk + 1, :])
    perm = jnp.where(hit, 1.0, 0.0).astype(BF16)
    srt = jnp.dot(perm, h2_ref[...], preferred_element_type=F32)
    packed = _pack_bf16_pairs(srt)
    for c in range(PACK_CHUNKS):
        stg[slot, :, c, :] = packed[:, c * LANES:(c + 1) * LANES]

    for e in range(N_EXPERTS):
        n = cnt_ref[i * N_EXPERTS + e]
        src0 = lstart_ref[i * N_EXPERTS + e]
        dst0 = dst_ref[i * N_EXPERTS + e]
        off = jnp.int32(0)
        piece = dt
        while piece >= 1:
            take = (n & piece) != 0

            @pl.when(take)
            def _(off=off, piece=piece):
                pltpu.make_async_copy(stg.at[slot, pl.ds(src0 + off, piece)],
                                      xs_hbm.at[pl.ds(dst0 + off, piece)], sem.at[slot]).start()

            off = off + jnp.where(take, piece, 0)
            piece //= 2

    @pl.when(i == n_tiles - 1)
    def _():
        drain(slot)

        @pl.when(n_tiles >= 2)
        def _():
            drain(1 - slot)


def _dispatch(h2, lpos, lstart, cnt, dst, cap):
    n_tok = h2.shape[0]
    dt = DISPATCH_TILE
    assert n_tok % dt == 0
    grid_spec = pltpu.PrefetchScalarGridSpec(
        num_scalar_prefetch=3,
        grid=(n_tok // dt,),
        in_specs=[pl.BlockSpec((dt, D_MODEL), lambda i, a, b, c: (i, 0)),
                  pl.BlockSpec((TOP_K, dt), lambda i, a, b, c: (0, i)),
                  ],
        out_specs=pl.BlockSpec(memory_space=pl.ANY),
        scratch_shapes=[pltpu.VMEM((2, dt * TOP_K, PACK_CHUNKS, LANES), jnp.uint32),
                        pltpu.SemaphoreType.DMA((2,))],
    )
    return pl.pallas_call(
        _dispatch_kernel,
        grid_spec=grid_spec,
        out_shape=jax.ShapeDtypeStruct((cap, PACK_CHUNKS, LANES), jnp.uint32),
        compiler_params=pltpu.CompilerParams(dimension_semantics=("arbitrary",),
                                             vmem_limit_bytes=VMEM_LIMIT),
        name="dispatch",
    )(lstart, cnt, dst, h2, lpos)


def _expert_kernel(bexp_ref, nvalid_ref, x_ref, wgu_ref, bgu_ref, wd_ref, bd_ref, y_ref, wgu_bf, wd_bf):
    i = pl.program_id(0)
    prev = bexp_ref[jnp.maximum(i - 1, 0)]
    fresh = jnp.logical_or(i == 0, bexp_ref[i] != prev)
    nvalid = nvalid_ref[i]

    @pl.when(jnp.logical_and(fresh, nvalid > 0))
    def _():
        chunk = 32

        def cast_gu(r, c):
            r0 = pl.multiple_of(r * chunk, chunk)
            wgu_bf[pl.ds(r0, chunk), :] = wgu_ref[0, pl.ds(r0, chunk), :].astype(BF16)
            return c

        def cast_d(r, c):
            r0 = pl.multiple_of(r * chunk, chunk)
            wd_bf[pl.ds(r0, chunk), :] = wd_ref[0, pl.ds(r0, chunk), :].astype(BF16)
            return c

        lax.fori_loop(0, D_MODEL // chunk, cast_gu, 0)
        lax.fori_loop(0, D_EXPERT // chunk, cast_d, 0)

    @pl.when(nvalid > 0)
    def _():
        xw = jnp.concatenate([x_ref[:, c, :] for c in range(PACK_CHUNKS)], axis=1)
        xw = jnp.where(lax.broadcasted_iota(jnp.int32, xw.shape, 0) < nvalid, xw, jnp.uint32(0))
        xlo, xhi = _unpack_bf16_pairs(xw)
        half = D_MODEL // 2
        g = (jnp.dot(xlo, wgu_bf[0:half, 0:D_EXPERT], preferred_element_type=F32)
             + jnp.dot(xhi, wgu_bf[half:, 0:D_EXPERT], preferred_element_type=F32)
             + bgu_ref[0, :, 0:D_EXPERT])
        u = (jnp.dot(xlo, wgu_bf[0:half, D_EXPERT:], preferred_element_type=F32)
             + jnp.dot(xhi, wgu_bf[half:, D_EXPERT:], preferred_element_type=F32)
             + bgu_ref[0, :, D_EXPERT:])
        g = jnp.minimum(g, SWIGLU_LIMIT)
        u = jnp.clip(u, -SWIGLU_LIMIT, SWIGLU_LIMIT)
        act = g * (1.0 / (1.0 + jnp.exp(-SWIGLU_ALPHA * g))) * (u + 1.0)
        y = jnp.dot(act.astype(BF16), wd_bf[...], preferred_element_type=F32) + bd_ref[0]
        row = lax.broadcasted_iota(jnp.int32, y.shape, 0)
        y_ref[...] = jnp.where(row < nvalid, y, 0.0).astype(BF16)

    @pl.when(nvalid == 0)
    def _():
        y_ref[...] = jnp.zeros_like(y_ref)


def _experts(xs, block_expert, nvalid, wgu, bgu, wd, bd):
    rb = EXPERT_ROWS
    nblk = xs.shape[0] // rb + 1
    last = xs.shape[0] // rb - 1
    grid_spec = pltpu.PrefetchScalarGridSpec(
        num_scalar_prefetch=2,
        grid=(nblk,),
        in_specs=[
            pl.BlockSpec((rb, PACK_CHUNKS, LANES), lambda i, be, nu: (jnp.minimum(i, last), 0, 0)),
            pl.BlockSpec((1, D_MODEL, 2 * D_EXPERT), lambda i, be, nu: (be[i], 0, 0)),
            pl.BlockSpec((1, 1, 2 * D_EXPERT), lambda i, be, nu: (be[i], 0, 0)),
            pl.BlockSpec((1, D_EXPERT, D_MODEL), lambda i, be, nu: (be[i], 0, 0)),
            pl.BlockSpec((1, 1, D_MODEL), lambda i, be, nu: (be[i], 0, 0)),
        ],
        out_specs=pl.BlockSpec((rb, D_MODEL), lambda i, be, nu: (i, 0)),
        scratch_shapes=[pltpu.VMEM((D_MODEL, 2 * D_EXPERT), BF16),
                        pltpu.VMEM((D_EXPERT, D_MODEL), BF16)],
    )
    return pl.pallas_call(
        _expert_kernel,
        grid_spec=grid_spec,
        out_shape=jax.ShapeDtypeStruct((nblk * rb, D_MODEL), BF16),
        compiler_params=pltpu.CompilerParams(dimension_semantics=("arbitrary",),
                                             vmem_limit_bytes=VMEM_LIMIT),
        name="experts",
    )(block_expert, nvalid, xs, wgu, bgu.reshape(N_EXPERTS, 1, 2 * D_EXPERT), wd,
      bd.reshape(N_EXPERTS, 1, D_MODEL))


def _combine_kernel(n_prompt_tiles, offa_ref, nchunk_ref,
                    x1p_ref, x1s_ref, col_ref, lrank_ref, gate_ref, gfin_ref, yb_hbm,
                    outp_ref, outs_ref, ybuf, acc_ref, sem):
    i = pl.program_id(0)
    n_tiles = pl.num_programs(0)

    def window_copy(tile, chunk, e, slot):
        base = pl.multiple_of(offa_ref[tile * N_EXPERTS + e] + chunk * WIN, WIN_ALIGN)
        return pltpu.make_async_copy(yb_hbm.at[pl.ds(base, WIN_ROWS), :],
                                     ybuf.at[slot, pl.ds(e * WIN_ROWS, WIN_ROWS), :],
                                     sem.at[slot])

    def start_windows(tile, chunk, slot):
        for e in range(N_EXPERTS):
            window_copy(tile, chunk, e, slot).start()

    def wait_windows(tile, chunk, slot):
        for e in range(N_EXPERTS):
            window_copy(tile, chunk, e, slot).wait()

    slot = i % 2

    @pl.when(i == 0)
    def _():
        start_windows(0, 0, 0)

    @pl.when(i + 1 < n_tiles)
    def _():
        start_windows(i + 1, 0, 1 - slot)

    wait_windows(i, 0, slot)
    lane = lax.broadcasted_iota(jnp.int32, (COMBINE_TILE, N_EXPERTS * WIN_ROWS), 1)

    def moe_rows(chunk, buf):
        g = jnp.zeros(lane.shape, F32)
        for kk in range(TOP_K):
            lr = lrank_ref[:, kk:kk + 1]
            in_chunk = jnp.logical_and(lr >= chunk * WIN, lr < chunk * WIN + WIN)
            colk = jnp.where(in_chunk, col_ref[:, kk:kk + 1] - chunk * WIN, -1)
            g = jnp.where(lane == colk, gate_ref[:, kk:kk + 1], g)
        hi = g.astype(BF16)
        lo = (g - hi.astype(F32)).astype(BF16)
        y = ybuf[buf]
        return (jnp.dot(hi, y, preferred_element_type=F32) + jnp.dot(lo, y, preferred_element_type=F32))

    acc_ref[...] = moe_rows(0, slot)

    def extra_chunk(j, c):
        start_windows(i, j, 2)
        wait_windows(i, j, 2)
        acc_ref[...] += moe_rows(j, 2)
        return c

    lax.fori_loop(1, nchunk_ref[i], extra_chunk, 0)

    @pl.when(i < n_prompt_tiles)
    def _():
        outp_ref[...] = _rms(x1p_ref[...] + acc_ref[...], gfin_ref[...])

    @pl.when(i >= n_prompt_tiles)
    def _():
        outs_ref[...] = _rms(x1s_ref[...] + acc_ref[...], gfin_ref[...])


def _combine(x1_p, x1_s, col_t, lrank_t, gates_t, gfin, yb, offa, nchunk):
    ct = COMBINE_TILE
    n_p, n_s = x1_p.shape[0] // ct, x1_s.shape[0] // ct
    assert x1_p.shape[0] % ct == 0 and x1_s.shape[0] % ct == 0 and n_s >= 1
    tok_spec = pl.BlockSpec((ct, TOP_K), lambda i, o, c: (i, 0))
    grid_spec = pltpu.PrefetchScalarGridSpec(
        num_scalar_prefetch=2,
        grid=(n_p + n_s,),
        in_specs=[
            pl.BlockSpec((ct, D_MODEL), lambda i, o, c: (jnp.minimum(i, n_p - 1), 0)),
            pl.BlockSpec((ct, D_MODEL), lambda i, o, c: (jnp.maximum(i - n_p, 0), 0)),
            tok_spec, tok_spec, tok_spec,
            pl.BlockSpec((1, D_MODEL), lambda i, o, c: (0, 0)),
            pl.BlockSpec(memory_space=pl.ANY),
        ],
        out_specs=[
            pl.BlockSpec((ct, D_MODEL), lambda i, o, c: (jnp.minimum(i, n_p - 1), 0)),
            pl.BlockSpec((ct, D_MODEL), lambda i, o, c: (jnp.maximum(i - n_p, 0), 0)),
        ],
        scratch_shapes=[pltpu.VMEM((3, N_EXPERTS * WIN_ROWS, D_MODEL), BF16),
                        pltpu.VMEM((ct, D_MODEL), F32),
                        pltpu.SemaphoreType.DMA((3,))],
    )
    return pl.pallas_call(
        functools.partial(_combine_kernel, n_p),
        grid_spec=grid_spec,
        out_shape=[jax.ShapeDtypeStruct(x1_p.shape, F32), jax.ShapeDtypeStruct(x1_s.shape, F32)],
        compiler_params=pltpu.CompilerParams(dimension_semantics=("arbitrary",),
                                             vmem_limit_bytes=VMEM_LIMIT),
        name="combine",
    )(offa, nchunk, x1_p, x1_s, col_t, lrank_t, gates_t, gfin, yb)


def kernel(x_prompt, x_sample, cache_k, cache_v, state_pool, meta_tokens, norm_attn, w_in, attn_sinks,
           w_pool, pool_scale, w_out, norm_ffn, router_w, router_b, w_gate_up, b_gate_up, w_down, b_down,
           norm_final):
    assert w_in.shape[0] == 1, "single-layer trunk"
    bsz, seq, _ = x_prompt.shape
    assert bsz == 1
    nb = x_sample.shape[0]
    n_tok = seq + nb
    gattn = norm_attn[0].reshape(1, D_MODEL)
    gffn = norm_ffn[0].reshape(1, D_MODEL)
    win = w_in[0].astype(BF16)
    wpool = w_pool[0].astype(BF16)
    wout = w_out[0].astype(BF16)
    pscale = pool_scale[0].reshape(1, POOL_WIDTH)
    rw_t = router_w[0].T
    rw_hi = rw_t.astype(BF16)
    rwt = jnp.stack([rw_hi, (rw_t - rw_hi.astype(F32)).astype(BF16)])
    rb = router_b[0].reshape(N_EXPERTS, 1)
    sinks = attn_sinks[0]

    ck = cache_k[0].reshape(nb, N_META + WINDOW, KV_WIDTH)
    cv = cache_v[0].reshape(nb, N_META + WINDOW, KV_WIDTH)
    (x1_s, h2_s, lgt_s, knew, vnew, pnew) = _sample_mixer(
        x_sample[:, 0], ck, cv, state_pool[0], gattn, win, wpool, pscale, wout, gffn, rwt, rb, sinks)
    (x1_p, h2_all, lgt_all, kmeta, vmeta, ktail, vtail, ptail) = _prompt_mixer(
        x_prompt[0], meta_tokens, gattn, win, wpool, pscale, wout, gffn, rwt, rb, sinks, h2_s, lgt_s)

    _eidx, _rank, gates, col, lrank, lpos, tcar, counts = _router(lgt_all)
    counts = counts[:, 0]
    tcar = tcar[:, :, 0]
    rbk = EXPERT_ROWS
    padded = (counts + rbk - 1) // rbk * rbk
    pad_end = jnp.cumsum(padded)
    pad_start = (pad_end - padded).astype(jnp.int32)
    nblk = -(-(n_tok * TOP_K) // rbk) + N_EXPERTS
    cap = nblk * rbk
    block_start = jnp.arange(nblk + 1, dtype=jnp.int32) * rbk
    block_expert = jnp.minimum(jnp.sum(pad_end[None, :] <= block_start[:, None], axis=1),
                               N_EXPERTS - 1).astype(jnp.int32)
    nvalid = jnp.clip(counts[block_expert] - (block_start - pad_start[block_expert]), 0, rbk)
    nvalid = jnp.where(block_start < pad_end[-1], nvalid, 0).astype(jnp.int32)
    last_e = block_expert[jnp.maximum(pad_end[-1] // rbk - 1, 0)]
    block_expert = jnp.where(block_start < pad_end[-1], block_expert, last_e)

    run_len = jnp.concatenate([tcar[1:], counts[None, :]], axis=0) - tcar
    run_lstart = jnp.cumsum(run_len, axis=1) - run_len
    run_dst = pad_start[None, :] + tcar
    flat = lambda a: a.astype(jnp.int32).reshape(-1)
    xs = _dispatch(h2_all, lpos, flat(run_lstart), flat(run_len), flat(run_dst), cap)
    yb = _experts(xs, block_expert, nvalid, w_gate_up[0], b_gate_up[0], w_down[0], b_down[0])

    offa = (pad_start[None, :] + (tcar - tcar % WIN_ALIGN)).astype(jnp.int32).reshape(-1)
    nchunk = jnp.maximum(jnp.max((run_len + WIN - 1) // WIN, axis=1), 1).astype(jnp.int32)
    gfin = norm_final.reshape(1, D_MODEL)
    y_prompt, y_sample = _combine(x1_p, x1_s, col.T, lrank.T, gates.T, gfin, yb, offa, nchunk)

    kv_shape = (1, 1, N_META + WINDOW, N_KV_HEADS, HEAD_DIM)
    new_k_p = jnp.concatenate([kmeta, ktail], axis=0).reshape(kv_shape)
    new_v_p = jnp.concatenate([vmeta, vtail], axis=0).reshape(kv_shape)
    new_pool_p = ptail[16 - POOL_STATE:].reshape(1, 1, POOL_STATE, POOL_WIDTH)
    new_k_s = jnp.concatenate([ck[:, :N_META], ck[:, N_META + 1:], knew[:, None]], axis=1).reshape(
        (1, nb, N_META + WINDOW, N_KV_HEADS, HEAD_DIM))
    new_v_s = jnp.concatenate([cv[:, :N_META], cv[:, N_META + 1:], vnew[:, None]], axis=1).reshape(
        (1, nb, N_META + WINDOW, N_KV_HEADS, HEAD_DIM))
    new_pool_s = jnp.concatenate([state_pool[0][:, 1:], pnew[:, None]], axis=1)[None]
    return (y_prompt[None], y_sample[:, None], new_k_p, new_v_p, new_pool_p, new_k_s, new_v_s, new_pool_s)
```

```python
import functools

import jax
import jax.numpy as jnp
import numpy as np
from jax import lax
from jax.experimental import pallas as pl
from jax.experimental.pallas import tpu as pltpu

F32 = jnp.float32
BF16 = jnp.bfloat16

D_MODEL = 1024
N_META = 16
N_HEADS = 8
HEAD_DIM = 64
N_KV_HEADS = 2
GQA_GROUP = N_HEADS // N_KV_HEADS
ATTN_WIDTH = N_HEADS * HEAD_DIM
KV_WIDTH = N_KV_HEADS * HEAD_DIM
WINDOW = 128
POOL_WIDTH = D_MODEL - ATTN_WIDTH
POOL_WINDOWS = (2, 4, 8, 16)
POOL_GROUP_DIM = POOL_WIDTH // len(POOL_WINDOWS)
POOL_STATE = max(POOL_WINDOWS) - 1
N_EXPERTS = 32
TOP_K = 4
D_EXPERT = D_MODEL
SWIGLU_ALPHA = 1.702
SWIGLU_LIMIT = 7.0
NORM_EPS = 1e-5
PAST_LEN = 16384

LANES = 128
QSUB = 64
KEYS_SUB = QSUB + WINDOW
META_PAD = 64
NKEY = META_PAD + KEYS_SUB
MASKED = -1e30
PROMPT_BLOCK = 512
ROUTE_BLOCK = 384
EXPERT_ROWS = 512
DISPATCH_TILE = 128
PACK_CHUNKS = D_MODEL // 2 // LANES
COMBINE_TILE = 128
WIN = 32
WIN_ALIGN = 16
WIN_ROWS = WIN + WIN_ALIGN
VMEM_LIMIT = 56 * 1024 * 1024


def _rms(x, g):
    return x * lax.rsqrt(jnp.mean(x * x, axis=-1, keepdims=True) + NORM_EPS) * g


def _router_logits(rwt_ref, h2, h2_hi):
    nt = (((1,), (1,)), ((), ()))
    h2_lo = (h2 - h2_hi.astype(F32)).astype(BF16)
    return (lax.dot_general(rwt_ref[0], h2_hi, nt, preferred_element_type=F32)
            + lax.dot_general(rwt_ref[0], h2_lo, nt, preferred_element_type=F32)
            + lax.dot_general(rwt_ref[1], h2_hi, nt, preferred_element_type=F32))


def _dup_halves(a):
    lane = lax.broadcasted_iota(jnp.int32, a.shape, 1)
    r = pltpu.roll(a, HEAD_DIM, axis=1)
    lo = lane < HEAD_DIM
    return jnp.where(lo, a, r), jnp.where(lo, r, a)


def _pool_means(pext_ref, n):
    outs = []
    for gi, w in enumerate(POOL_WINDOWS):
        xg = pext_ref[:, gi * POOL_GROUP_DIM:(gi + 1) * POOL_GROUP_DIM]
        s = xg
        sh = 1
        while sh < w:
            s = s + pltpu.roll(s, sh, axis=0)
            sh *= 2
        outs.append(s[16:] * (1.0 / w) - xg[16:])
    return outs


def _pack_bf16_pairs(h):
    m = h.shape[1] // 2
    lo = pltpu.bitcast(h[:, :m].astype(BF16).astype(F32), jnp.uint32)
    hi = pltpu.bitcast(h[:, m:].astype(BF16).astype(F32), jnp.uint32)
    return lax.shift_right_logical(lo, jnp.uint32(16)) | (hi & jnp.uint32(0xFFFF0000))


def _unpack_bf16_pairs(w):
    lo = pltpu.bitcast(lax.shift_left(w, jnp.uint32(16)), F32).astype(BF16)
    hi = pltpu.bitcast(w & jnp.uint32(0xFFFF0000), F32).astype(BF16)
    return lo, hi


def _prompt_kernel(x_ref, meta_ref, gattn_ref, win_ref, wpool_ref, pscale_ref, wout_ref, gffn_ref,
                   rwt_ref, rb_ref, sink_ref, tbl_ref, tail_h2_ref, tail_lgt_ref,
                   x1_ref, h2_ref, lgt_ref, kmeta_ref, vmeta_ref, ktail_ref, vtail_ref, ptail_ref,
                   k2buf, v2buf, km2, vm2, qbuf, obuf, pext):
    pid = pl.program_id(0)
    n_main = pl.num_programs(0) - 1
    refs = (x_ref, meta_ref, gattn_ref, win_ref, wpool_ref, pscale_ref, wout_ref, gffn_ref,
            rwt_ref, rb_ref, sink_ref, tbl_ref,
            x1_ref, h2_ref, lgt_ref, kmeta_ref, vmeta_ref, ktail_ref, vtail_ref, ptail_ref,
            k2buf, v2buf, km2, vm2, qbuf, obuf, pext)

    @pl.when(pid < n_main)
    def _():
        _prompt_block(*refs)

    @pl.when(pid == n_main)
    def _():
        h2_ref[0:tail_h2_ref.shape[0], :] = tail_h2_ref[...]
        lgt_ref[:, 0:tail_lgt_ref.shape[1]] = tail_lgt_ref[...]


def _prompt_block(x_ref, meta_ref, gattn_ref, win_ref, wpool_ref, pscale_ref, wout_ref, gffn_ref,
                  rwt_ref, rb_ref, sink_ref, tbl_ref,
                  x1_ref, h2_ref, lgt_ref, kmeta_ref, vmeta_ref, ktail_ref, vtail_ref, ptail_ref,
                  k2buf, v2buf, km2, vm2, qbuf, obuf, pext):
    tb = x_ref.shape[0]
    pid = pl.program_id(0)

    @pl.when(pid == 0)
    def _():
        hm = _rms(meta_ref[...], gattn_ref[...]).astype(BF16)
        km = jnp.dot(hm, win_ref[:, ATTN_WIDTH:ATTN_WIDTH + KV_WIDTH], preferred_element_type=F32)
        vm = jnp.dot(hm, win_ref[:, ATTN_WIDTH + KV_WIDTH:ATTN_WIDTH + 2 * KV_WIDTH],
                     preferred_element_type=F32)
        pm = jnp.dot(hm, win_ref[:, ATTN_WIDTH + 2 * KV_WIDTH:], preferred_element_type=F32)
        kmeta_ref[...] = km
        vmeta_ref[...] = vm
        zpad = jnp.zeros((META_PAD - N_META, LANES), F32)
        k0, k1 = _dup_halves(jnp.concatenate([km, zpad], axis=0))
        v0, v1 = _dup_halves(jnp.concatenate([vm, zpad], axis=0))
        km2[0] = k0.astype(BF16)
        km2[1] = k1.astype(BF16)
        vm2[0, :, 0:LANES] = v0.astype(BF16)
        vm2[1, :, 0:LANES] = v1.astype(BF16)
        vm2[:, :, LANES:] = jnp.ones((N_KV_HEADS, META_PAD, LANES), BF16)
        k2buf[:, 0:WINDOW, :] = jnp.zeros((2, WINDOW, LANES), BF16)
        v2buf[:, 0:WINDOW, 0:LANES] = jnp.zeros((2, WINDOW, LANES), BF16)
        v2buf[:, :, LANES:] = jnp.ones((N_KV_HEADS, WINDOW + tb, LANES), BF16)
        pext[0:16, :] = pm

    h = _rms(x_ref[...], gattn_ref[...]).astype(BF16)
    q = jnp.dot(h, win_ref[:, 0:ATTN_WIDTH], preferred_element_type=F32) * (HEAD_DIM ** -0.5)
    lane_t = lax.broadcasted_iota(jnp.int32, (tb, LANES), 1)
    for c in range(N_HEADS // 2):
        tile = q[:, c * LANES:(c + 1) * LANES]
        for a in range(2):
            keep = (lane_t < HEAD_DIM) if a == 0 else (lane_t >= HEAD_DIM)
            piece = jnp.where(keep, tile, 0.0).astype(BF16).reshape(tb // QSUB, QSUB, LANES)
            row = ((c % 2) * 2 + a) * QSUB
            qbuf[c // 2, :, row:row + QSUB, :] = piece
    k = jnp.dot(h, win_ref[:, ATTN_WIDTH:ATTN_WIDTH + KV_WIDTH], preferred_element_type=F32)
    v = jnp.dot(h, win_ref[:, ATTN_WIDTH + KV_WIDTH:ATTN_WIDTH + 2 * KV_WIDTH], preferred_element_type=F32)
    p = jnp.dot(h, win_ref[:, ATTN_WIDTH + 2 * KV_WIDTH:], preferred_element_type=F32)
    ktail_ref[...] = k[tb - WINDOW:]
    vtail_ref[...] = v[tb - WINDOW:]
    ptail_ref[...] = p[tb - 16:]
    k0, k1 = _dup_halves(k)
    v0, v1 = _dup_halves(v)
    k2buf[0, WINDOW:, :] = k0.astype(BF16)
    k2buf[1, WINDOW:, :] = k1.astype(BF16)
    v2buf[0, WINDOW:, 0:LANES] = v0.astype(BF16)
    v2buf[1, WINDOW:, 0:LANES] = v1.astype(BF16)
    pext[16:, :] = p

    lane_q = lax.broadcasted_iota(jnp.int32, (QSUB, LANES), 1)
    lo_q = lane_q < HEAD_DIM

    for u in range(tb // QSUB):
        r0 = u * QSUB
        sel = jnp.where(pid == 0, u + 1, 0) if u < WINDOW // QSUB else 0
        for g in range(N_KV_HEADS):
            qm = qbuf[g, u]
            kwin = jnp.concatenate([km2[g], k2buf[g, r0:r0 + KEYS_SUB, :]], axis=0)
            vwin = jnp.concatenate([vm2[g], v2buf[g, r0:r0 + KEYS_SUB, :]], axis=0)
            s = lax.dot_general(qm, kwin, (((1,), (1,)), ((), ())), preferred_element_type=F32)
            s = s + tbl_ref[sel, g]
            sink = sink_ref[g]
            m = jnp.maximum(jnp.max(s, axis=1, keepdims=True), sink)
            e = jnp.exp(s - m).astype(BF16)
            r = jnp.dot(e, vwin, preferred_element_type=F32)
            o = r[:, 0:LANES] / (r[:, LANES:] + jnp.exp(sink - m))
            o0 = jnp.where(lo_q, o[0:QSUB], o[QSUB:2 * QSUB])
            o1 = jnp.where(lo_q, o[2 * QSUB:3 * QSUB], o[3 * QSUB:])
            obuf[r0:r0 + QSUB, (2 * g) * LANES:(2 * g + 1) * LANES] = o0.astype(BF16)
            obuf[r0:r0 + QSUB, (2 * g + 1) * LANES:(2 * g + 2) * LANES] = o1.astype(BF16)

    pooled = _pool_means(pext, tb)
    for gi in range(len(POOL_WINDOWS)):
        y = jnp.dot(pooled[gi].astype(BF16), wpool_ref[gi], preferred_element_type=F32)
        y = y * pscale_ref[:, gi * POOL_GROUP_DIM:(gi + 1) * POOL_GROUP_DIM]
        obuf[:, ATTN_WIDTH + gi * POOL_GROUP_DIM:ATTN_WIDTH + (gi + 1) * POOL_GROUP_DIM] = y.astype(BF16)

    k2buf[:, 0:WINDOW, :] = k2buf[:, tb:tb + WINDOW, :]
    v2buf[:, 0:WINDOW, 0:LANES] = v2buf[:, tb:tb + WINDOW, 0:LANES]
    pext[0:16, :] = pext[tb:tb + 16, :]

    x1 = x_ref[...] + jnp.dot(obuf[...], wout_ref[...], preferred_element_type=F32)
    x1_ref[...] = x1
    h2 = _rms(x1, gffn_ref[...])
    h2_hi = h2.astype(BF16)
    h2_ref[...] = h2_hi
    lgt_ref[...] = _router_logits(rwt_ref, h2, h2_hi) + rb_ref[...]


def _attn_tables(sinks):
    i = np.arange(QSUB)[:, None]
    j = np.arange(NKEY)[None, :]
    jb = j - META_PAD
    rel = i + WINDOW - jb
    band_ok = (jb >= 0) & (rel >= 0) & (rel <= WINDOW)
    meta_ok = (j < N_META) & (i >= 0)
    slopes = np.exp2(-8.0 * np.arange(1, N_HEADS + 1) / N_HEADS)
    tbl = np.empty((3, N_KV_HEADS, GQA_GROUP * QSUB, NKEY), np.float32)
    for var in range(3):
        ok = band_ok if var == 0 else band_ok & (jb >= WINDOW - (var - 1) * QSUB)
        for g in range(N_KV_HEADS):
            for a in range(GQA_GROUP):
                hd = g * GQA_GROUP + a
                bias = np.where(ok, -slopes[hd] * rel, MASKED)
                bias = np.where(meta_ok, 0.0, bias)
                tbl[var, g, a * QSUB:(a + 1) * QSUB] = bias
    sink_col = jnp.repeat(sinks.astype(F32).reshape(N_KV_HEADS, GQA_GROUP, 1), QSUB, axis=2)
    return jnp.asarray(tbl), sink_col.reshape(N_KV_HEADS, GQA_GROUP * QSUB, 1)


def _prompt_mixer(x, meta, gattn, win, wpool, pscale, wout, gffn, rwt, rb, sinks, tail_h2, tail_lgt):
    seq = x.shape[0]
    tb = PROMPT_BLOCK
    n_tail = tail_h2.shape[0]
    assert seq % tb == 0 and tb % WINDOW == 0 and n_tail <= tb
    nblk = seq // tb
    n_tok = seq + n_tail
    tbl, sink_col = _attn_tables(sinks)
    full = lambda *shape: pl.BlockSpec(shape, lambda i: (0,) * len(shape))
    main = lambda i: (jnp.minimum(i, nblk - 1), 0)
    in_width = win.shape[1]
    return pl.pallas_call(
        _prompt_kernel,
        grid=(nblk + 1,),
        in_specs=[
            pl.BlockSpec((tb, D_MODEL), main),
            full(N_META, D_MODEL), full(1, D_MODEL), full(D_MODEL, in_width),
            full(len(POOL_WINDOWS), POOL_GROUP_DIM, POOL_GROUP_DIM), full(1, POOL_WIDTH),
            full(D_MODEL, D_MODEL), full(1, D_MODEL), full(2, N_EXPERTS, D_MODEL), full(N_EXPERTS, 1),
            full(N_KV_HEADS, GQA_GROUP * QSUB, 1), full(3, N_KV_HEADS, GQA_GROUP * QSUB, NKEY),
            full(n_tail, D_MODEL), full(N_EXPERTS, n_tail),
        ],
        out_specs=[
            pl.BlockSpec((tb, D_MODEL), main),
            pl.BlockSpec((tb, D_MODEL), lambda i: (i, 0)),
            pl.BlockSpec((N_EXPERTS, tb), lambda i: (0, i)),
            full(N_META, KV_WIDTH), full(N_META, KV_WIDTH),
            full(WINDOW, KV_WIDTH), full(WINDOW, KV_WIDTH), full(16, POOL_WIDTH),
        ],
        out_shape=[
            jax.ShapeDtypeStruct((seq, D_MODEL), F32),
            jax.ShapeDtypeStruct((n_tok, D_MODEL), BF16),
            jax.ShapeDtypeStruct((N_EXPERTS, n_tok), F32),
            jax.ShapeDtypeStruct((N_META, KV_WIDTH), F32),
            jax.ShapeDtypeStruct((N_META, KV_WIDTH), F32),
            jax.ShapeDtypeStruct((WINDOW, KV_WIDTH), F32),
            jax.ShapeDtypeStruct((WINDOW, KV_WIDTH), F32),
            jax.ShapeDtypeStruct((16, POOL_WIDTH), F32),
        ],
        scratch_shapes=[
            pltpu.VMEM((N_KV_HEADS, WINDOW + tb, LANES), BF16),
            pltpu.VMEM((N_KV_HEADS, WINDOW + tb, 2 * LANES), BF16),
            pltpu.VMEM((N_KV_HEADS, META_PAD, LANES), BF16),
            pltpu.VMEM((N_KV_HEADS, META_PAD, 2 * LANES), BF16),
            pltpu.VMEM((N_KV_HEADS, tb // QSUB, GQA_GROUP * QSUB, LANES), BF16),
            pltpu.VMEM((tb, D_MODEL), BF16),
            pltpu.VMEM((16 + tb, POOL_WIDTH), F32),
        ],
        compiler_params=pltpu.CompilerParams(dimension_semantics=("arbitrary",),
                                             vmem_limit_bytes=VMEM_LIMIT),
        name="prompt_mixer",
    )(x, meta, gattn, win, wpool, pscale, wout, gffn, rwt, rb, sink_col, tbl, tail_h2, tail_lgt)


def _sample_kernel(x_ref, ck_ref, cv_ref, sp_ref, gattn_ref, win_ref, wpool_ref, pscale_ref, wout_ref,
                   gffn_ref, rwt_ref, rb_ref, sinkc_ref, bias_ref,
                   x1_ref, h2_ref, lgt_ref, knew_ref, vnew_ref, pnew_ref,
                   qm_buf, r_buf, obuf):
    nb = x_ref.shape[0]
    x = x_ref[...]
    h = _rms(x, gattn_ref[...]).astype(BF16)
    q = jnp.dot(h, win_ref[:, 0:ATTN_WIDTH], preferred_element_type=F32) * (HEAD_DIM ** -0.5)
    k = jnp.dot(h, win_ref[:, ATTN_WIDTH:ATTN_WIDTH + KV_WIDTH], preferred_element_type=F32)
    v = jnp.dot(h, win_ref[:, ATTN_WIDTH + KV_WIDTH:ATTN_WIDTH + 2 * KV_WIDTH], preferred_element_type=F32)
    p = jnp.dot(h, win_ref[:, ATTN_WIDTH + 2 * KV_WIDTH:], preferred_element_type=F32)
    knew_ref[...] = k
    vnew_ref[...] = v
    pnew_ref[...] = p

    lane = lax.broadcasted_iota(jnp.int32, (nb, LANES), 1)
    lo = lane < HEAD_DIM
    for hd in range(N_HEADS):
        tile = q[:, (hd // 2) * LANES:(hd // 2 + 1) * LANES]
        if (hd % 2) != (hd // GQA_GROUP):
            tile = pltpu.roll(tile, HEAD_DIM, axis=1)
        keep_lo = (hd // GQA_GROUP) == 0
        qm_buf[:, hd, :] = jnp.where(lo if keep_lo else jnp.logical_not(lo), tile, 0.0)

    def per_batch(b, carry):
        qm = qm_buf[b]
        kb = ck_ref[b].astype(BF16)
        vb = cv_ref[b].astype(BF16)
        s = lax.dot_general(qm.astype(BF16), kb, (((1,), (1,)), ((), ())), preferred_element_type=F32)
        s = s + bias_ref[...]
        kn = knew_ref[pl.ds(b, 1), :]
        vn = vnew_ref[pl.ds(b, 1), :]
        s_self = jnp.sum(qm * kn, axis=1, keepdims=True)
        sink = sinkc_ref[...]
        m = jnp.maximum(jnp.maximum(jnp.max(s, axis=1, keepdims=True), s_self), sink)
        e = jnp.exp(s - m)
        e_self = jnp.exp(s_self - m)
        den = jnp.sum(e, axis=1, keepdims=True) + e_self + jnp.exp(sink - m)
        r = jnp.dot(e.astype(BF16), vb, preferred_element_type=F32)
        r = r + e_self * vn
        r_buf[b] = r / den
        return carry

    lax.fori_loop(0, nb, per_batch, 0)

    for c in range(N_HEADS // 2):
        halves = []
        for a in range(2):
            hd = 2 * c + a
            t = r_buf[:, hd, :]
            if (hd // GQA_GROUP) != a:
                t = pltpu.roll(t, HEAD_DIM, axis=1)
            halves.append(t)
        obuf[:, c * LANES:(c + 1) * LANES] = jnp.where(lo, halves[0], halves[1]).astype(BF16)

    for gi, w in enumerate(POOL_WINDOWS):
        cols = slice(gi * POOL_GROUP_DIM, (gi + 1) * POOL_GROUP_DIM)
        pg = p[:, cols]
        acc = pg
        for d in range(1, w):
            acc = acc + sp_ref[:, POOL_STATE - d, cols]
        pooled = acc * (1.0 / w) - pg
        y = jnp.dot(pooled.astype(BF16), wpool_ref[gi], preferred_element_type=F32) * pscale_ref[:, cols]
        obuf[:, ATTN_WIDTH + gi * POOL_GROUP_DIM:ATTN_WIDTH + (gi + 1) * POOL_GROUP_DIM] = y.astype(BF16)

    x1 = x + jnp.dot(obuf[...], wout_ref[...], preferred_element_type=F32)
    x1_ref[...] = x1
    h2 = _rms(x1, gffn_ref[...])
    h2_hi = h2.astype(BF16)
    h2_ref[...] = h2_hi
    lgt_ref[...] = _router_logits(rwt_ref, h2, h2_hi) + rb_ref[...]


def _sample_mixer(x, ck, cv, sp, gattn, win, wpool, pscale, wout, gffn, rwt, rb, sinks):
    nb = x.shape[0]
    rows = ck.shape[1]
    slopes = np.exp2(-8.0 * np.arange(1, N_HEADS + 1) / N_HEADS)
    dist = np.concatenate([np.zeros(N_META), WINDOW - np.arange(WINDOW)])
    bias = jnp.asarray((-slopes[:, None] * dist[None, :]).astype(np.float32))
    vm = pl.BlockSpec(memory_space=pltpu.VMEM)
    return pl.pallas_call(
        _sample_kernel,
        in_specs=[vm] * 14,
        out_specs=[vm] * 6,
        out_shape=[
            jax.ShapeDtypeStruct((nb, D_MODEL), F32),
            jax.ShapeDtypeStruct((nb, D_MODEL), BF16),
            jax.ShapeDtypeStruct((N_EXPERTS, nb), F32),
            jax.ShapeDtypeStruct((nb, KV_WIDTH), F32),
            jax.ShapeDtypeStruct((nb, KV_WIDTH), F32),
            jax.ShapeDtypeStruct((nb, POOL_WIDTH), F32),
        ],
        scratch_shapes=[
            pltpu.VMEM((nb, N_HEADS, LANES), F32),
            pltpu.VMEM((nb, N_HEADS, LANES), F32),
            pltpu.VMEM((nb, D_MODEL), BF16),
        ],
        compiler_params=pltpu.CompilerParams(vmem_limit_bytes=VMEM_LIMIT),
        name="sample_mixer",
    )(x, ck, cv, sp, gattn, win, wpool, pscale, wout, gffn, rwt, rb,
      sinks.astype(F32).reshape(N_HEADS, 1), bias)


def _router_kernel(lg_ref, tri_ref, low_ref, eidx_ref, rank_ref, gate_ref, col_ref, lrank_ref, lpos_ref,
                   tcar_ref, cnt_ref, carry):
    tr = lg_ref.shape[1]

    @pl.when(pl.program_id(0) == 0)
    def _():
        carry[...] = jnp.zeros_like(carry)

    work = lg_ref[...]
    eio = lax.broadcasted_iota(jnp.int32, work.shape, 0).astype(F32)
    sels, vals, idxs = [], [], []
    for _k in range(TOP_K):
        mx = jnp.max(work, axis=0, keepdims=True)
        idx = jnp.min(jnp.where(work == mx, eio, float(N_EXPERTS)), axis=0, keepdims=True)
        sel = eio == idx
        sels.append(sel)
        vals.append(mx)
        idxs.append(idx)
        work = jnp.where(sel, -jnp.inf, work)
    exps = [jnp.exp(vk - vals[0]) for vk in vals]
    tot = exps[0] + exps[1] + exps[2] + exps[3]
    onehot = jnp.zeros(work.shape, F32)
    for sel in sels:
        onehot = onehot + sel.astype(F32)
    before = jnp.dot(onehot.astype(BF16), tri_ref[...], preferred_element_type=F32) + carry[...]
    for kk in range(TOP_K):
        eidx_ref[pl.ds(kk, 1), :] = idxs[kk].astype(jnp.int32)
        gate_ref[pl.ds(kk, 1), :] = exps[kk] / tot
        rk = jnp.sum(jnp.where(sels[kk], before, 0.0), axis=0, keepdims=True)
        rank_ref[pl.ds(kk, 1), :] = rk.astype(jnp.int32)
    for j in range(tr // COMBINE_TILE):
        cols = slice(j * COMBINE_TILE, (j + 1) * COMBINE_TILE)
        tc = before[:, j * COMBINE_TILE:j * COMBINE_TILE + 1]
        tcar_ref[j] = tc.astype(jnp.int32)
        slack = tc - WIN_ALIGN * jnp.floor(tc * (1.0 / WIN_ALIGN))
        local = before[:, cols] - tc
        tile_cnt = jnp.broadcast_to(jnp.sum(onehot[:, cols], axis=1, keepdims=True), local.shape)
        run_start = jnp.dot(low_ref[...], tile_cnt.astype(BF16), preferred_element_type=F32)
        for kk in range(TOP_K):
            selk = sels[kk][:, cols]
            lr = jnp.sum(jnp.where(selk, local, 0.0), axis=0, keepdims=True)
            sl = jnp.sum(jnp.where(selk, slack, 0.0), axis=0, keepdims=True)
            lp = jnp.sum(jnp.where(selk, run_start + local, 0.0), axis=0, keepdims=True)
            lpos_ref[pl.ds(kk, 1), cols] = lp.astype(jnp.int32)
            lrank_ref[pl.ds(kk, 1), cols] = lr.astype(jnp.int32)
            col_ref[pl.ds(kk, 1), cols] = (idxs[kk][:, cols] * float(WIN_ROWS) + sl + lr).astype(jnp.int32)
    carry[...] = carry[...] + jnp.sum(onehot, axis=1, keepdims=True)
    cnt_ref[...] = carry[...].astype(jnp.int32)


def _router(logits_t):
    n = logits_t.shape[1]
    tr = ROUTE_BLOCK
    assert n % tr == 0
    tri = jnp.asarray(np.triu(np.ones((tr, tr), np.float32), k=1), BF16)
    low = jnp.asarray(np.tril(np.ones((N_EXPERTS, N_EXPERTS), np.float32), k=-1), BF16)
    per_tok = pl.BlockSpec((TOP_K, tr), lambda i: (0, i))
    return pl.pallas_call(
        _router_kernel,
        grid=(n // tr,),
        in_specs=[pl.BlockSpec((N_EXPERTS, tr), lambda i: (0, i)),
                  pl.BlockSpec((tr, tr), lambda i: (0, 0)),
                  pl.BlockSpec((N_EXPERTS, N_EXPERTS), lambda i: (0, 0))],
        out_specs=[per_tok, per_tok, per_tok, per_tok, per_tok, per_tok,
                   pl.BlockSpec((tr // COMBINE_TILE, N_EXPERTS, 1), lambda i: (i, 0, 0)),
                   pl.BlockSpec((N_EXPERTS, 1), lambda i: (0, 0))],
        out_shape=[jax.ShapeDtypeStruct((TOP_K, n), jnp.int32),
                   jax.ShapeDtypeStruct((TOP_K, n), jnp.int32),
                   jax.ShapeDtypeStruct((TOP_K, n), F32),
                   jax.ShapeDtypeStruct((TOP_K, n), jnp.int32),
                   jax.ShapeDtypeStruct((TOP_K, n), jnp.int32),
                   jax.ShapeDtypeStruct((TOP_K, n), jnp.int32),
                   jax.ShapeDtypeStruct((n // COMBINE_TILE, N_EXPERTS, 1), jnp.int32),
                   jax.ShapeDtypeStruct((N_EXPERTS, 1), jnp.int32)],
        scratch_shapes=[pltpu.VMEM((N_EXPERTS, 1), F32)],
        compiler_params=pltpu.CompilerParams(dimension_semantics=("arbitrary",)),
        name="router",
    )(logits_t, tri, low)


def _dispatch_kernel(lstart_ref, cnt_ref, dst_ref, h2_ref, lpos_ref, xs_hbm, stg, sem):
    i = pl.program_id(0)
    n_tiles = pl.num_programs(0)
    dt = h2_ref.shape[0]
    rows = dt * TOP_K
    slot = i % 2

    def drain(s):
        pltpu.make_async_copy(stg.at[s], xs_hbm.at[pl.ds(0, rows)], sem.at[s]).wait()

    @pl.when(i >= 2)
    def _():
        drain(slot)

    rid = lax.broadcasted_iota(jnp.int32, (rows, dt), 0)
    hit = rid == lpos_ref[0:1, :]
    for kk in range(1, TOP_K):
        hit = jnp.logical_or(hit, rid == lpos_ref[kk:kk + 1, :])
    perm = jnp.where(hit, 1.0, 0.0).astype(BF16)
    srt = jnp.dot(perm, h2_ref[...], preferred_element_type=F32)
    packed = _pack_bf16_pairs(srt)
    for c in range(PACK_CHUNKS):
        stg[slot, :, c, :] = packed[:, c * LANES:(c + 1) * LANES]

    for e in range(N_EXPERTS):
        n = cnt_ref[i * N_EXPERTS + e]
        src0 = lstart_ref[i * N_EXPERTS + e]
        dst0 = dst_ref[i * N_EXPERTS + e]
        off = jnp.int32(0)
        piece = dt
        while piece >= 1:
            take = (n & piece) != 0

            @pl.when(take)
            def _(off=off, piece=piece):
                pltpu.make_async_copy(stg.at[slot, pl.ds(src0 + off, piece)],
                                      xs_hbm.at[pl.ds(dst0 + off, piece)], sem.at[slot]).start()

            off = off + jnp.where(take, piece, 0)
            piece //= 2

    @pl.when(i == n_tiles - 1)
    def _():
        drain(slot)

        @pl.when(n_tiles >= 2)
        def _():
            drain(1 - slot)


def _dispatch(h2, lpos, lstart, cnt, dst, cap):
    n_tok = h2.shape[0]
    dt = DISPATCH_TILE
    assert n_tok % dt == 0
    grid_spec = pltpu.PrefetchScalarGridSpec(
        num_scalar_prefetch=3,
        grid=(n_tok // dt,),
        in_specs=[pl.BlockSpec((dt, D_MODEL), lambda i, a, b, c: (i, 0)),
                  pl.BlockSpec((TOP_K, dt), lambda i, a, b, c: (0, i)),
                  ],
        out_specs=pl.BlockSpec(memory_space=pl.ANY),
        scratch_shapes=[pltpu.VMEM((2, dt * TOP_K, PACK_CHUNKS, LANES), jnp.uint32),
                        pltpu.SemaphoreType.DMA((2,))],
    )
    return pl.pallas_call(
        _dispatch_kernel,
        grid_spec=grid_spec,
        out_shape=jax.ShapeDtypeStruct((cap, PACK_CHUNKS, LANES), jnp.uint32),
        compiler_params=pltpu.CompilerParams(dimension_semantics=("arbitrary",),
                                             vmem_limit_bytes=VMEM_LIMIT),
        name="dispatch",
    )(lstart, cnt, dst, h2, lpos)


def _expert_kernel(n_xblocks, bexp_ref, nvalid_ref, epos_ref, elist_ref,
                   x_hbm, wgu_hbm, bgu_ref, wd_hbm, bd_ref, y_ref,
                   wgu_f32, wd_f32, wgu_bf, wd_bf, xbuf, xsem, wsem):
    i = pl.program_id(0)
    rb = y_ref.shape[0]
    nvalid = nvalid_ref[i]
    pos = epos_ref[i]
    fresh = jnp.logical_or(i == 0, pos != epos_ref[jnp.maximum(i - 1, 0)])
    slot = i % 2

    def x_copies(blk, s):
        return [pltpu.make_async_copy(x_hbm.at[pl.ds(blk * rb, rb), c, :],
                                      xbuf.at[s, :, pl.ds(c * LANES, LANES)], xsem.at[s])
                for c in range(PACK_CHUNKS)]

    def w_copies(p):
        e = elist_ref[p]
        s = p % 2
        return [pltpu.make_async_copy(wgu_hbm.at[e], wgu_f32.at[s], wsem.at[s, 0]),
                pltpu.make_async_copy(wd_hbm.at[e], wd_f32.at[s], wsem.at[s, 1])]

    @pl.when(i == 0)
    def _():
        for cp in x_copies(0, 0):
            cp.start()

        @pl.when(nvalid > 0)
        def _():
            for cp in w_copies(0):
                cp.start()

    @pl.when(i + 1 < n_xblocks)
    def _():
        for cp in x_copies(i + 1, 1 - slot):
            cp.start()

    @pl.when(jnp.logical_and(fresh, nvalid > 0))
    def _():
        @pl.when(elist_ref[pos + 1] >= 0)
        def _():
            for cp in w_copies(pos + 1):
                cp.start()

        for cp in w_copies(pos):
            cp.wait()
        ws = pos % 2
        chunk = 32

        def cast_gu(r, c):
            r0 = pl.multiple_of(r * chunk, chunk)
            wgu_bf[pl.ds(r0, chunk), :] = wgu_f32[ws, pl.ds(r0, chunk), :].astype(BF16)
            return c

        def cast_d(r, c):
            r0 = pl.multiple_of(r * chunk, chunk)
            wd_bf[pl.ds(r0, chunk), :] = wd_f32[ws, pl.ds(r0, chunk), :].astype(BF16)
            return c

        lax.fori_loop(0, D_MODEL // chunk, cast_gu, 0)
        lax.fori_loop(0, D_EXPERT // chunk, cast_d, 0)

    @pl.when(i < n_xblocks)
    def _():
        for cp in x_copies(i, slot):
            cp.wait()

    @pl.when(nvalid > 0)
    def _():
        xw = xbuf[slot]
        xw = jnp.where(lax.broadcasted_iota(jnp.int32, xw.shape, 0) < nvalid, xw, jnp.uint32(0))
        xlo, xhi = _unpack_bf16_pairs(xw)
        half = D_MODEL // 2
        g = (jnp.dot(xlo, wgu_bf[0:half, 0:D_EXPERT], preferred_element_type=F32)
             + jnp.dot(xhi, wgu_bf[half:, 0:D_EXPERT], preferred_element_type=F32)
             + bgu_ref[0, :, 0:D_EXPERT])
        u = (jnp.dot(xlo, wgu_bf[0:half, D_EXPERT:], preferred_element_type=F32)
             + jnp.dot(xhi, wgu_bf[half:, D_EXPERT:], preferred_element_type=F32)
             + bgu_ref[0, :, D_EXPERT:])
        g = jnp.minimum(g, SWIGLU_LIMIT)
        u = jnp.clip(u, -SWIGLU_LIMIT, SWIGLU_LIMIT)
        act = g * (1.0 / (1.0 + jnp.exp(-SWIGLU_ALPHA * g))) * (u + 1.0)
        y = jnp.dot(act.astype(BF16), wd_bf[...], preferred_element_type=F32) + bd_ref[0]
        row = lax.broadcasted_iota(jnp.int32, y.shape, 0)
        y_ref[...] = jnp.where(row < nvalid, y, 0.0).astype(BF16)

    @pl.when(nvalid == 0)
    def _():
        y_ref[...] = jnp.zeros_like(y_ref)


def _experts(xs, block_expert, nvalid, block_pos, expert_list, wgu, bgu, wd, bd):
    rb = EXPERT_ROWS
    n_xblocks = xs.shape[0] // rb
    nblk = n_xblocks + 1
    any_space = pl.BlockSpec(memory_space=pl.ANY)
    grid_spec = pltpu.PrefetchScalarGridSpec(
        num_scalar_prefetch=4,
        grid=(nblk,),
        in_specs=[
            any_space,
            any_space,
            pl.BlockSpec((1, 1, 2 * D_EXPERT), lambda i, be, nu, ep, el: (be[i], 0, 0)),
            any_space,
            pl.BlockSpec((1, 1, D_MODEL), lambda i, be, nu, ep, el: (be[i], 0, 0)),
        ],
        out_specs=pl.BlockSpec((rb, D_MODEL), lambda i, be, nu, ep, el: (i, 0)),
        scratch_shapes=[pltpu.VMEM((2, D_MODEL, 2 * D_EXPERT), F32),
                        pltpu.VMEM((2, D_EXPERT, D_MODEL), F32),
                        pltpu.VMEM((D_MODEL, 2 * D_EXPERT), BF16),
                        pltpu.VMEM((D_EXPERT, D_MODEL), BF16),
                        pltpu.VMEM((2, rb, D_MODEL // 2), jnp.uint32),
                        pltpu.SemaphoreType.DMA((2,)),
                        pltpu.SemaphoreType.DMA((2, 2))],
    )
    return pl.pallas_call(
        functools.partial(_expert_kernel, n_xblocks),
        grid_spec=grid_spec,
        out_shape=jax.ShapeDtypeStruct((nblk * rb, D_MODEL), BF16),
        compiler_params=pltpu.CompilerParams(dimension_semantics=("arbitrary",),
                                             vmem_limit_bytes=VMEM_LIMIT),
        name="experts",
    )(block_expert, nvalid, block_pos, expert_list, xs, wgu, bgu.reshape(N_EXPERTS, 1, 2 * D_EXPERT), wd,
      bd.reshape(N_EXPERTS, 1, D_MODEL))


def _combine_kernel(n_prompt_tiles, offa_ref, nchunk_ref,
                    x1p_ref, x1s_ref, col_ref, lrank_ref, gate_ref, gfin_ref, yb_hbm,
                    outp_ref, outs_ref, ybuf, acc_ref, sem):
    i = pl.program_id(0)
    n_tiles = pl.num_programs(0)

    def window_copy(tile, chunk, e, slot):
        base = pl.multiple_of(offa_ref[tile * N_EXPERTS + e] + chunk * WIN, WIN_ALIGN)
        return pltpu.make_async_copy(yb_hbm.at[pl.ds(base, WIN_ROWS), :],
                                     ybuf.at[slot, pl.ds(e * WIN_ROWS, WIN_ROWS), :],
                                     sem.at[slot])

    def start_windows(tile, chunk, slot):
        for e in range(N_EXPERTS):
            window_copy(tile, chunk, e, slot).start()

    def wait_windows(tile, chunk, slot):
        for e in range(N_EXPERTS):
            window_copy(tile, chunk, e, slot).wait()

    slot = i % 2

    @pl.when(i == 0)
    def _():
        start_windows(0, 0, 0)

    @pl.when(i + 1 < n_tiles)
    def _():
        start_windows(i + 1, 0, 1 - slot)

    wait_windows(i, 0, slot)
    lane = lax.broadcasted_iota(jnp.int32, (COMBINE_TILE, N_EXPERTS * WIN_ROWS), 1)

    def moe_rows(chunk, buf):
        g = jnp.zeros(lane.shape, F32)
        for kk in range(TOP_K):
            lr = lrank_ref[:, kk:kk + 1]
            in_chunk = jnp.logical_and(lr >= chunk * WIN, lr < chunk * WIN + WIN)
            colk = jnp.where(in_chunk, col_ref[:, kk:kk + 1] - chunk * WIN, -1)
            g = jnp.where(lane == colk, gate_ref[:, kk:kk + 1], g)
        hi = g.astype(BF16)
        lo = (g - hi.astype(F32)).astype(BF16)
        y = ybuf[buf]
        return (jnp.dot(hi, y, preferred_element_type=F32) + jnp.dot(lo, y, preferred_element_type=F32))

    acc_ref[...] = moe_rows(0, slot)

    def extra_chunk(j, c):
        start_windows(i, j, 2)
        wait_windows(i, j, 2)
        acc_ref[...] += moe_rows(j, 2)
        return c

    lax.fori_loop(1, nchunk_ref[i], extra_chunk, 0)

    @pl.when(i < n_prompt_tiles)
    def _():
        outp_ref[...] = _rms(x1p_ref[...] + acc_ref[...], gfin_ref[...])

    @pl.when(i >= n_prompt_tiles)
    def _():
        outs_ref[...] = _rms(x1s_ref[...] + acc_ref[...], gfin_ref[...])


def _combine(x1_p, x1_s, col_t, lrank_t, gates_t, gfin, yb, offa, nchunk):
    ct = COMBINE_TILE
    n_p, n_s = x1_p.shape[0] // ct, x1_s.shape[0] // ct
    assert x1_p.shape[0] % ct == 0 and x1_s.shape[0] % ct == 0 and n_s >= 1
    tok_spec = pl.BlockSpec((ct, TOP_K), lambda i, o, c: (i, 0))
    grid_spec = pltpu.PrefetchScalarGridSpec(
        num_scalar_prefetch=2,
        grid=(n_p + n_s,),
        in_specs=[
            pl.BlockSpec((ct, D_MODEL), lambda i, o, c: (jnp.minimum(i, n_p - 1), 0)),
            pl.BlockSpec((ct, D_MODEL), lambda i, o, c: (jnp.maximum(i - n_p, 0), 0)),
            tok_spec, tok_spec, tok_spec,
            pl.BlockSpec((1, D_MODEL), lambda i, o, c: (0, 0)),
            pl.BlockSpec(memory_space=pl.ANY),
        ],
        out_specs=[
            pl.BlockSpec((ct, D_MODEL), lambda i, o, c: (jnp.minimum(i, n_p - 1), 0)),
            pl.BlockSpec((ct, D_MODEL), lambda i, o, c: (jnp.maximum(i - n_p, 0), 0)),
        ],
        scratch_shapes=[pltpu.VMEM((3, N_EXPERTS * WIN_ROWS, D_MODEL), BF16),
                        pltpu.VMEM((ct, D_MODEL), F32),
                        pltpu.SemaphoreType.DMA((3,))],
    )
    return pl.pallas_call(
        functools.partial(_combine_kernel, n_p),
        grid_spec=grid_spec,
        out_shape=[jax.ShapeDtypeStruct(x1_p.shape, F32), jax.ShapeDtypeStruct(x1_s.shape, F32)],
        compiler_params=pltpu.CompilerParams(dimension_semantics=("arbitrary",),
                                             vmem_limit_bytes=VMEM_LIMIT),
        name="combine",
    )(offa, nchunk, x1_p, x1_s, col_t, lrank_t, gates_t, gfin, yb)


def kernel(x_prompt, x_sample, cache_k, cache_v, state_pool, meta_tokens, norm_attn, w_in, attn_sinks,
           w_pool, pool_scale, w_out, norm_ffn, router_w, router_b, w_gate_up, b_gate_up, w_down, b_down,
           norm_final):
    assert w_in.shape[0] == 1, "single-layer trunk"
    bsz, seq, _ = x_prompt.shape
    assert bsz == 1
    nb = x_sample.shape[0]
    n_tok = seq + nb
    gattn = norm_attn[0].reshape(1, D_MODEL)
    gffn = norm_ffn[0].reshape(1, D_MODEL)
    win = w_in[0].astype(BF16)
    wpool = w_pool[0].astype(BF16)
    wout = w_out[0].astype(BF16)
    pscale = pool_scale[0].reshape(1, POOL_WIDTH)
    rw_t = router_w[0].T
    rw_hi = rw_t.astype(BF16)
    rwt = jnp.stack([rw_hi, (rw_t - rw_hi.astype(F32)).astype(BF16)])
    rb = router_b[0].reshape(N_EXPERTS, 1)
    sinks = attn_sinks[0]

    ck = cache_k[0].reshape(nb, N_META + WINDOW, KV_WIDTH)
    cv = cache_v[0].reshape(nb, N_META + WINDOW, KV_WIDTH)
    (x1_s, h2_s, lgt_s, knew, vnew, pnew) = _sample_mixer(
        x_sample[:, 0], ck, cv, state_pool[0], gattn, win, wpool, pscale, wout, gffn, rwt, rb, sinks)
    (x1_p, h2_all, lgt_all, kmeta, vmeta, ktail, vtail, ptail) = _prompt_mixer(
        x_prompt[0], meta_tokens, gattn, win, wpool, pscale, wout, gffn, rwt, rb, sinks, h2_s, lgt_s)

    _eidx, _rank, gates, col, lrank, lpos, tcar, counts = _router(lgt_all)
    counts = counts[:, 0]
    tcar = tcar[:, :, 0]
    rbk = EXPERT_ROWS
    padded = (counts + rbk - 1) // rbk * rbk
    pad_end = jnp.cumsum(padded)
    pad_start = (pad_end - padded).astype(jnp.int32)
    nblk = -(-(n_tok * TOP_K) // rbk) + N_EXPERTS
    cap = nblk * rbk
    block_start = jnp.arange(nblk + 1, dtype=jnp.int32) * rbk
    block_expert = jnp.minimum(jnp.sum(pad_end[None, :] <= block_start[:, None], axis=1),
                               N_EXPERTS - 1).astype(jnp.int32)
    nvalid = jnp.clip(counts[block_expert] - (block_start - pad_start[block_expert]), 0, rbk)
    nvalid = jnp.where(block_start < pad_end[-1], nvalid, 0).astype(jnp.int32)
    last_e = block_expert[jnp.maximum(pad_end[-1] // rbk - 1, 0)]
    block_expert = jnp.where(block_start < pad_end[-1], block_expert, last_e)

    run_len = jnp.concatenate([tcar[1:], counts[None, :]], axis=0) - tcar
    run_lstart = jnp.cumsum(run_len, axis=1) - run_len
    run_dst = pad_start[None, :] + tcar
    flat = lambda a: a.astype(jnp.int32).reshape(-1)
    xs = _dispatch(h2_all, lpos, flat(run_lstart), flat(run_len), flat(run_dst), cap)
    has_rows = counts > 0
    expert_pos = jnp.cumsum(has_rows.astype(jnp.int32)) - 1
    eids = jnp.arange(N_EXPERTS, dtype=jnp.int32)
    at_pos = has_rows[None, :] & (expert_pos[None, :] == jnp.arange(N_EXPERTS + 1, dtype=jnp.int32)[:, None])
    expert_list = jnp.where(jnp.any(at_pos, axis=1), jnp.sum(jnp.where(at_pos, eids[None, :], 0), axis=1),
                            -1).astype(jnp.int32)
    block_pos = jnp.sum(jnp.where(block_expert[:, None] == eids[None, :], expert_pos[None, :], 0),
                        axis=1).astype(jnp.int32)
    yb = _experts(xs, block_expert, nvalid, block_pos, expert_list,
                  w_gate_up[0], b_gate_up[0], w_down[0], b_down[0])

    offa = (pad_start[None, :] + (tcar - tcar % WIN_ALIGN)).astype(jnp.int32).reshape(-1)
    nchunk = jnp.maximum(jnp.max((run_len + WIN - 1) // WIN, axis=1), 1).astype(jnp.int32)
    gfin = norm_final.reshape(1, D_MODEL)
    y_prompt, y_sample = _combine(x1_p, x1_s, col.T, lrank.T, gates.T, gfin, yb, offa, nchunk)

    kv_shape = (1, 1, N_META + WINDOW, N_KV_HEADS, HEAD_DIM)
    new_k_p = jnp.concatenate([kmeta, ktail], axis=0).reshape(kv_shape)
    new_v_p = jnp.concatenate([vmeta, vtail], axis=0).reshape(kv_shape)
    new_pool_p = ptail[16 - POOL_STATE:].reshape(1, 1, POOL_STATE, POOL_WIDTH)
    new_k_s = jnp.concatenate([ck[:, :N_META], ck[:, N_META + 1:], knew[:, None]], axis=1).reshape(
        (1, nb, N_META + WINDOW, N_KV_HEADS, HEAD_DIM))
    new_v_s = jnp.concatenate([cv[:, :N_META], cv[:, N_META + 1:], vnew[:, None]], axis=1).reshape(
        (1, nb, N_META + WINDOW, N_KV_HEADS, HEAD_DIM))
    new_pool_s = jnp.concatenate([state_pool[0][:, 1:], pnew[:, None]], axis=1)[None]
    return (y_prompt[None], y_sample[:, None], new_k_p, new_v_p, new_pool_p, new_k_s, new_v_s, new_pool_s)
```

```python
import functools

import jax
import jax.numpy as jnp
import numpy as np
from jax import lax
from jax.experimental import pallas as pl
from jax.experimental.pallas import tpu as pltpu

F32 = jnp.float32
BF16 = jnp.bfloat16

D_MODEL = 1024
N_META = 16
N_HEADS = 8
HEAD_DIM = 64
N_KV_HEADS = 2
GQA_GROUP = N_HEADS // N_KV_HEADS
ATTN_WIDTH = N_HEADS * HEAD_DIM
KV_WIDTH = N_KV_HEADS * HEAD_DIM
WINDOW = 128
POOL_WIDTH = D_MODEL - ATTN_WIDTH
POOL_WINDOWS = (2, 4, 8, 16)
POOL_GROUP_DIM = POOL_WIDTH // len(POOL_WINDOWS)
POOL_STATE = max(POOL_WINDOWS) - 1
N_EXPERTS = 32
TOP_K = 4
D_EXPERT = D_MODEL
SWIGLU_ALPHA = 1.702
SWIGLU_LIMIT = 7.0
NORM_EPS = 1e-5
PAST_LEN = 16384

LANES = 128
QSUB = 64
KEYS_SUB = QSUB + WINDOW
META_PAD = 64
NKEY = META_PAD + KEYS_SUB
MASKED = -1e30
PROMPT_BLOCK = 512
ROUTE_BLOCK = 384
EXPERT_ROWS = 512
DISPATCH_TILE = 128
PACK_CHUNKS = D_MODEL // 2 // LANES
COMBINE_TILE = 128
WIN = 32
WIN_ALIGN = 16
WIN_ROWS = WIN + WIN_ALIGN
VMEM_LIMIT = 56 * 1024 * 1024


def _rms(x, g):
    return x * lax.rsqrt(jnp.mean(x * x, axis=-1, keepdims=True) + NORM_EPS) * g


def _router_logits(rwt_ref, h2, h2_hi):
    nt = (((1,), (1,)), ((), ()))
    h2_lo = (h2 - h2_hi.astype(F32)).astype(BF16)
    return (lax.dot_general(rwt_ref[0], h2_hi, nt, preferred_element_type=F32)
            + lax.dot_general(rwt_ref[0], h2_lo, nt, preferred_element_type=F32)
            + lax.dot_general(rwt_ref[1], h2_hi, nt, preferred_element_type=F32))


def _dup_halves(a):
    lane = lax.broadcasted_iota(jnp.int32, a.shape, 1)
    r = pltpu.roll(a, HEAD_DIM, axis=1)
    lo = lane < HEAD_DIM
    return jnp.where(lo, a, r), jnp.where(lo, r, a)


def _pool_means(pext_ref, n):
    outs = []
    for gi, w in enumerate(POOL_WINDOWS):
        xg = pext_ref[:, gi * POOL_GROUP_DIM:(gi + 1) * POOL_GROUP_DIM]
        s = xg
        sh = 1
        while sh < w:
            s = s + pltpu.roll(s, sh, axis=0)
            sh *= 2
        outs.append(s[16:] * (1.0 / w) - xg[16:])
    return outs


def _pack_bf16_pairs(h):
    m = h.shape[1] // 2
    lo = pltpu.bitcast(h[:, :m].astype(BF16).astype(F32), jnp.uint32)
    hi = pltpu.bitcast(h[:, m:].astype(BF16).astype(F32), jnp.uint32)
    return lax.shift_right_logical(lo, jnp.uint32(16)) | (hi & jnp.uint32(0xFFFF0000))


def _unpack_bf16_pairs(w):
    lo = pltpu.bitcast(lax.shift_left(w, jnp.uint32(16)), F32).astype(BF16)
    hi = pltpu.bitcast(w & jnp.uint32(0xFFFF0000), F32).astype(BF16)
    return lo, hi


def _prompt_kernel(x_ref, meta_ref, gattn_ref, win_ref, wpool_ref, pscale_ref, wout_ref, gffn_ref,
                   rwt_ref, rb_ref, sink_ref, tbl_ref, tail_h2_ref, tail_lgt_ref,
                   x1_ref, h2_ref, lgt_ref, kmeta_ref, vmeta_ref, ktail_ref, vtail_ref, ptail_ref,
                   k2buf, v2buf, km2, vm2, qbuf, obuf, pext):
    pid = pl.program_id(0)
    n_main = pl.num_programs(0) - 1
    refs = (x_ref, meta_ref, gattn_ref, win_ref, wpool_ref, pscale_ref, wout_ref, gffn_ref,
            rwt_ref, rb_ref, sink_ref, tbl_ref,
            x1_ref, h2_ref, lgt_ref, kmeta_ref, vmeta_ref, ktail_ref, vtail_ref, ptail_ref,
            k2buf, v2buf, km2, vm2, qbuf, obuf, pext)

    @pl.when(pid < n_main)
    def _():
        _prompt_block(*refs)

    @pl.when(pid == n_main)
    def _():
        h2_ref[0:tail_h2_ref.shape[0], :] = tail_h2_ref[...]
        lgt_ref[:, 0:tail_lgt_ref.shape[1]] = tail_lgt_ref[...]


def _prompt_block(x_ref, meta_ref, gattn_ref, win_ref, wpool_ref, pscale_ref, wout_ref, gffn_ref,
                  rwt_ref, rb_ref, sink_ref, tbl_ref,
                  x1_ref, h2_ref, lgt_ref, kmeta_ref, vmeta_ref, ktail_ref, vtail_ref, ptail_ref,
                  k2buf, v2buf, km2, vm2, qbuf, obuf, pext):
    tb = x_ref.shape[0]
    pid = pl.program_id(0)

    @pl.when(pid == 0)
    def _():
        hm = _rms(meta_ref[...], gattn_ref[...]).astype(BF16)
        km = jnp.dot(hm, win_ref[:, ATTN_WIDTH:ATTN_WIDTH + KV_WIDTH], preferred_element_type=F32)
        vm = jnp.dot(hm, win_ref[:, ATTN_WIDTH + KV_WIDTH:ATTN_WIDTH + 2 * KV_WIDTH],
                     preferred_element_type=F32)
        pm = jnp.dot(hm, win_ref[:, ATTN_WIDTH + 2 * KV_WIDTH:], preferred_element_type=F32)
        kmeta_ref[...] = km
        vmeta_ref[...] = vm
        zpad = jnp.zeros((META_PAD - N_META, LANES), F32)
        k0, k1 = _dup_halves(jnp.concatenate([km, zpad], axis=0))
        v0, v1 = _dup_halves(jnp.concatenate([vm, zpad], axis=0))
        km2[0] = k0.astype(BF16)
        km2[1] = k1.astype(BF16)
        vm2[0, :, 0:LANES] = v0.astype(BF16)
        vm2[1, :, 0:LANES] = v1.astype(BF16)
        vm2[:, :, LANES:] = jnp.ones((N_KV_HEADS, META_PAD, LANES), BF16)
        k2buf[:, 0:WINDOW, :] = jnp.zeros((2, WINDOW, LANES), BF16)
        v2buf[:, 0:WINDOW, 0:LANES] = jnp.zeros((2, WINDOW, LANES), BF16)
        v2buf[:, :, LANES:] = jnp.ones((N_KV_HEADS, WINDOW + tb, LANES), BF16)
        pext[0:16, :] = pm

    h = _rms(x_ref[...], gattn_ref[...]).astype(BF16)
    q = jnp.dot(h, win_ref[:, 0:ATTN_WIDTH], preferred_element_type=F32) * (HEAD_DIM ** -0.5)
    lane_t = lax.broadcasted_iota(jnp.int32, (tb, LANES), 1)
    for c in range(N_HEADS // 2):
        tile = q[:, c * LANES:(c + 1) * LANES]
        for a in range(2):
            keep = (lane_t < HEAD_DIM) if a == 0 else (lane_t >= HEAD_DIM)
            piece = jnp.where(keep, tile, 0.0).astype(BF16).reshape(tb // QSUB, QSUB, LANES)
            row = ((c % 2) * 2 + a) * QSUB
            qbuf[c // 2, :, row:row + QSUB, :] = piece
    k = jnp.dot(h, win_ref[:, ATTN_WIDTH:ATTN_WIDTH + KV_WIDTH], preferred_element_type=F32)
    v = jnp.dot(h, win_ref[:, ATTN_WIDTH + KV_WIDTH:ATTN_WIDTH + 2 * KV_WIDTH], preferred_element_type=F32)
    p = jnp.dot(h, win_ref[:, ATTN_WIDTH + 2 * KV_WIDTH:], preferred_element_type=F32)
    ktail_ref[...] = k[tb - WINDOW:]
    vtail_ref[...] = v[tb - WINDOW:]
    ptail_ref[...] = p[tb - 16:]
    k0, k1 = _dup_halves(k)
    v0, v1 = _dup_halves(v)
    k2buf[0, WINDOW:, :] = k0.astype(BF16)
    k2buf[1, WINDOW:, :] = k1.astype(BF16)
    v2buf[0, WINDOW:, 0:LANES] = v0.astype(BF16)
    v2buf[1, WINDOW:, 0:LANES] = v1.astype(BF16)
    pext[16:, :] = p

    lane_q = lax.broadcasted_iota(jnp.int32, (QSUB, LANES), 1)
    lo_q = lane_q < HEAD_DIM

    for u in range(tb // QSUB):
        r0 = u * QSUB
        sel = jnp.where(pid == 0, u + 1, 0) if u < WINDOW // QSUB else 0
        for g in range(N_KV_HEADS):
            qm = qbuf[g, u]
            kwin = jnp.concatenate([km2[g], k2buf[g, r0:r0 + KEYS_SUB, :]], axis=0)
            vwin = jnp.concatenate([vm2[g], v2buf[g, r0:r0 + KEYS_SUB, :]], axis=0)
            s = lax.dot_general(qm, kwin, (((1,), (1,)), ((), ())), preferred_element_type=F32)
            s = s + tbl_ref[sel, g]
            sink = sink_ref[g]
            m = jnp.maximum(jnp.max(s, axis=1, keepdims=True), sink)
            e = jnp.exp(s - m).astype(BF16)
            r = jnp.dot(e, vwin, preferred_element_type=F32)
            o = r[:, 0:LANES] / (r[:, LANES:] + jnp.exp(sink - m))
            o0 = jnp.where(lo_q, o[0:QSUB], o[QSUB:2 * QSUB])
            o1 = jnp.where(lo_q, o[2 * QSUB:3 * QSUB], o[3 * QSUB:])
            obuf[r0:r0 + QSUB, (2 * g) * LANES:(2 * g + 1) * LANES] = o0.astype(BF16)
            obuf[r0:r0 + QSUB, (2 * g + 1) * LANES:(2 * g + 2) * LANES] = o1.astype(BF16)

    pooled = _pool_means(pext, tb)
    for gi in range(len(POOL_WINDOWS)):
        y = jnp.dot(pooled[gi].astype(BF16), wpool_ref[gi], preferred_element_type=F32)
        y = y * pscale_ref[:, gi * POOL_GROUP_DIM:(gi + 1) * POOL_GROUP_DIM]
        obuf[:, ATTN_WIDTH + gi * POOL_GROUP_DIM:ATTN_WIDTH + (gi + 1) * POOL_GROUP_DIM] = y.astype(BF16)

    k2buf[:, 0:WINDOW, :] = k2buf[:, tb:tb + WINDOW, :]
    v2buf[:, 0:WINDOW, 0:LANES] = v2buf[:, tb:tb + WINDOW, 0:LANES]
    pext[0:16, :] = pext[tb:tb + 16, :]

    x1 = x_ref[...] + jnp.dot(obuf[...], wout_ref[...], preferred_element_type=F32)
    x1_ref[...] = x1
    h2 = _rms(x1, gffn_ref[...])
    h2_hi = h2.astype(BF16)
    h2_ref[...] = h2_hi
    lgt_ref[...] = _router_logits(rwt_ref, h2, h2_hi) + rb_ref[...]


def _attn_tables(sinks):
    i = np.arange(QSUB)[:, None]
    j = np.arange(NKEY)[None, :]
    jb = j - META_PAD
    rel = i + WINDOW - jb
    band_ok = (jb >= 0) & (rel >= 0) & (rel <= WINDOW)
    meta_ok = (j < N_META) & (i >= 0)
    slopes = np.exp2(-8.0 * np.arange(1, N_HEADS + 1) / N_HEADS)
    tbl = np.empty((3, N_KV_HEADS, GQA_GROUP * QSUB, NKEY), np.float32)
    for var in range(3):
        ok = band_ok if var == 0 else band_ok & (jb >= WINDOW - (var - 1) * QSUB)
        for g in range(N_KV_HEADS):
            for a in range(GQA_GROUP):
                hd = g * GQA_GROUP + a
                bias = np.where(ok, -slopes[hd] * rel, MASKED)
                bias = np.where(meta_ok, 0.0, bias)
                tbl[var, g, a * QSUB:(a + 1) * QSUB] = bias
    sink_col = jnp.repeat(sinks.astype(F32).reshape(N_KV_HEADS, GQA_GROUP, 1), QSUB, axis=2)
    return jnp.asarray(tbl), sink_col.reshape(N_KV_HEADS, GQA_GROUP * QSUB, 1)


def _prompt_mixer(x, meta, gattn, win, wpool, pscale, wout, gffn, rwt, rb, sinks, tail_h2, tail_lgt):
    seq = x.shape[0]
    tb = PROMPT_BLOCK
    n_tail = tail_h2.shape[0]
    assert seq % tb == 0 and tb % WINDOW == 0 and n_tail <= tb
    nblk = seq // tb
    n_tok = seq + n_tail
    tbl, sink_col = _attn_tables(sinks)
    full = lambda *shape: pl.BlockSpec(shape, lambda i: (0,) * len(shape))
    main = lambda i: (jnp.minimum(i, nblk - 1), 0)
    in_width = win.shape[1]
    return pl.pallas_call(
        _prompt_kernel,
        grid=(nblk + 1,),
        in_specs=[
            pl.BlockSpec((tb, D_MODEL), main),
            full(N_META, D_MODEL), full(1, D_MODEL), full(D_MODEL, in_width),
            full(len(POOL_WINDOWS), POOL_GROUP_DIM, POOL_GROUP_DIM), full(1, POOL_WIDTH),
            full(D_MODEL, D_MODEL), full(1, D_MODEL), full(2, N_EXPERTS, D_MODEL), full(N_EXPERTS, 1),
            full(N_KV_HEADS, GQA_GROUP * QSUB, 1), full(3, N_KV_HEADS, GQA_GROUP * QSUB, NKEY),
            full(n_tail, D_MODEL), full(N_EXPERTS, n_tail),
        ],
        out_specs=[
            pl.BlockSpec((tb, D_MODEL), main),
            pl.BlockSpec((tb, D_MODEL), lambda i: (i, 0)),
            pl.BlockSpec((N_EXPERTS, tb), lambda i: (0, i)),
            full(N_META, KV_WIDTH), full(N_META, KV_WIDTH),
            full(WINDOW, KV_WIDTH), full(WINDOW, KV_WIDTH), full(16, POOL_WIDTH),
        ],
        out_shape=[
            jax.ShapeDtypeStruct((seq, D_MODEL), F32),
            jax.ShapeDtypeStruct((n_tok, D_MODEL), BF16),
            jax.ShapeDtypeStruct((N_EXPERTS, n_tok), F32),
            jax.ShapeDtypeStruct((N_META, KV_WIDTH), F32),
            jax.ShapeDtypeStruct((N_META, KV_WIDTH), F32),
            jax.ShapeDtypeStruct((WINDOW, KV_WIDTH), F32),
            jax.ShapeDtypeStruct((WINDOW, KV_WIDTH), F32),
            jax.ShapeDtypeStruct((16, POOL_WIDTH), F32),
        ],
        scratch_shapes=[
            pltpu.VMEM((N_KV_HEADS, WINDOW + tb, LANES), BF16),
            pltpu.VMEM((N_KV_HEADS, WINDOW + tb, 2 * LANES), BF16),
            pltpu.VMEM((N_KV_HEADS, META_PAD, LANES), BF16),
            pltpu.VMEM((N_KV_HEADS, META_PAD, 2 * LANES), BF16),
            pltpu.VMEM((N_KV_HEADS, tb // QSUB, GQA_GROUP * QSUB, LANES), BF16),
            pltpu.VMEM((tb, D_MODEL), BF16),
            pltpu.VMEM((16 + tb, POOL_WIDTH), F32),
        ],
        compiler_params=pltpu.CompilerParams(dimension_semantics=("arbitrary",),
                                             vmem_limit_bytes=VMEM_LIMIT),
        name="prompt_mixer",
    )(x, meta, gattn, win, wpool, pscale, wout, gffn, rwt, rb, sink_col, tbl, tail_h2, tail_lgt)


def _sample_kernel(x_ref, ck_ref, cv_ref, sp_ref, gattn_ref, win_ref, wpool_ref, pscale_ref, wout_ref,
                   gffn_ref, rwt_ref, rb_ref, sinkc_ref, bias_ref,
                   x1_ref, h2_ref, lgt_ref, knew_ref, vnew_ref, pnew_ref,
                   qm_buf, r_buf, obuf):
    nb = x_ref.shape[0]
    x = x_ref[...]
    h = _rms(x, gattn_ref[...]).astype(BF16)
    q = jnp.dot(h, win_ref[:, 0:ATTN_WIDTH], preferred_element_type=F32) * (HEAD_DIM ** -0.5)
    k = jnp.dot(h, win_ref[:, ATTN_WIDTH:ATTN_WIDTH + KV_WIDTH], preferred_element_type=F32)
    v = jnp.dot(h, win_ref[:, ATTN_WIDTH + KV_WIDTH:ATTN_WIDTH + 2 * KV_WIDTH], preferred_element_type=F32)
    p = jnp.dot(h, win_ref[:, ATTN_WIDTH + 2 * KV_WIDTH:], preferred_element_type=F32)
    knew_ref[...] = k
    vnew_ref[...] = v
    pnew_ref[...] = p

    lane = lax.broadcasted_iota(jnp.int32, (nb, LANES), 1)
    lo = lane < HEAD_DIM
    for hd in range(N_HEADS):
        tile = q[:, (hd // 2) * LANES:(hd // 2 + 1) * LANES]
        if (hd % 2) != (hd // GQA_GROUP):
            tile = pltpu.roll(tile, HEAD_DIM, axis=1)
        keep_lo = (hd // GQA_GROUP) == 0
        qm_buf[:, hd, :] = jnp.where(lo if keep_lo else jnp.logical_not(lo), tile, 0.0)

    def per_batch(b, carry):
        qm = qm_buf[b]
        kb = ck_ref[b].astype(BF16)
        vb = cv_ref[b].astype(BF16)
        s = lax.dot_general(qm.astype(BF16), kb, (((1,), (1,)), ((), ())), preferred_element_type=F32)
        s = s + bias_ref[...]
        kn = knew_ref[pl.ds(b, 1), :]
        vn = vnew_ref[pl.ds(b, 1), :]
        s_self = jnp.sum(qm * kn, axis=1, keepdims=True)
        sink = sinkc_ref[...]
        m = jnp.maximum(jnp.maximum(jnp.max(s, axis=1, keepdims=True), s_self), sink)
        e = jnp.exp(s - m)
        e_self = jnp.exp(s_self - m)
        den = jnp.sum(e, axis=1, keepdims=True) + e_self + jnp.exp(sink - m)
        r = jnp.dot(e.astype(BF16), vb, preferred_element_type=F32)
        r = r + e_self * vn
        r_buf[b] = r / den
        return carry

    lax.fori_loop(0, nb, per_batch, 0)

    for c in range(N_HEADS // 2):
        halves = []
        for a in range(2):
            hd = 2 * c + a
            t = r_buf[:, hd, :]
            if (hd // GQA_GROUP) != a:
                t = pltpu.roll(t, HEAD_DIM, axis=1)
            halves.append(t)
        obuf[:, c * LANES:(c + 1) * LANES] = jnp.where(lo, halves[0], halves[1]).astype(BF16)

    for gi, w in enumerate(POOL_WINDOWS):
        cols = slice(gi * POOL_GROUP_DIM, (gi + 1) * POOL_GROUP_DIM)
        pg = p[:, cols]
        acc = pg
        for d in range(1, w):
            acc = acc + sp_ref[:, POOL_STATE - d, cols]
        pooled = acc * (1.0 / w) - pg
        y = jnp.dot(pooled.astype(BF16), wpool_ref[gi], preferred_element_type=F32) * pscale_ref[:, cols]
        obuf[:, ATTN_WIDTH + gi * POOL_GROUP_DIM:ATTN_WIDTH + (gi + 1) * POOL_GROUP_DIM] = y.astype(BF16)

    x1 = x + jnp.dot(obuf[...], wout_ref[...], preferred_element_type=F32)
    x1_ref[...] = x1
    h2 = _rms(x1, gffn_ref[...])
    h2_hi = h2.astype(BF16)
    h2_ref[...] = h2_hi
    lgt_ref[...] = _router_logits(rwt_ref, h2, h2_hi) + rb_ref[...]


def _sample_mixer(x, ck, cv, sp, gattn, win, wpool, pscale, wout, gffn, rwt, rb, sinks):
    nb = x.shape[0]
    rows = ck.shape[1]
    slopes = np.exp2(-8.0 * np.arange(1, N_HEADS + 1) / N_HEADS)
    dist = np.concatenate([np.zeros(N_META), WINDOW - np.arange(WINDOW)])
    bias = jnp.asarray((-slopes[:, None] * dist[None, :]).astype(np.float32))
    vm = pl.BlockSpec(memory_space=pltpu.VMEM)
    return pl.pallas_call(
        _sample_kernel,
        in_specs=[vm] * 14,
        out_specs=[vm] * 6,
        out_shape=[
            jax.ShapeDtypeStruct((nb, D_MODEL), F32),
            jax.ShapeDtypeStruct((nb, D_MODEL), BF16),
            jax.ShapeDtypeStruct((N_EXPERTS, nb), F32),
            jax.ShapeDtypeStruct((nb, KV_WIDTH), F32),
            jax.ShapeDtypeStruct((nb, KV_WIDTH), F32),
            jax.ShapeDtypeStruct((nb, POOL_WIDTH), F32),
        ],
        scratch_shapes=[
            pltpu.VMEM((nb, N_HEADS, LANES), F32),
            pltpu.VMEM((nb, N_HEADS, LANES), F32),
            pltpu.VMEM((nb, D_MODEL), BF16),
        ],
        compiler_params=pltpu.CompilerParams(vmem_limit_bytes=VMEM_LIMIT),
        name="sample_mixer",
    )(x, ck, cv, sp, gattn, win, wpool, pscale, wout, gffn, rwt, rb,
      sinks.astype(F32).reshape(N_HEADS, 1), bias)


def _router_kernel(lg_ref, tri_ref, low_ref, eidx_ref, rank_ref, gate_ref, col_ref, lrank_ref, lpos_ref,
                   tcar_ref, cnt_ref, carry):
    tr = lg_ref.shape[1]

    @pl.when(pl.program_id(0) == 0)
    def _():
        carry[...] = jnp.zeros_like(carry)

    work = lg_ref[...]
    eio = lax.broadcasted_iota(jnp.int32, work.shape, 0).astype(F32)
    sels, vals, idxs = [], [], []
    for _k in range(TOP_K):
        mx = jnp.max(work, axis=0, keepdims=True)
        idx = jnp.min(jnp.where(work == mx, eio, float(N_EXPERTS)), axis=0, keepdims=True)
        sel = eio == idx
        sels.append(sel)
        vals.append(mx)
        idxs.append(idx)
        work = jnp.where(sel, -jnp.inf, work)
    exps = [jnp.exp(vk - vals[0]) for vk in vals]
    tot = exps[0] + exps[1] + exps[2] + exps[3]
    onehot = jnp.zeros(work.shape, F32)
    for sel in sels:
        onehot = onehot + sel.astype(F32)
    before = jnp.dot(onehot.astype(BF16), tri_ref[...], preferred_element_type=F32) + carry[...]
    for kk in range(TOP_K):
        eidx_ref[pl.ds(kk, 1), :] = idxs[kk].astype(jnp.int32)
        gate_ref[pl.ds(kk, 1), :] = exps[kk] / tot
        rk = jnp.sum(jnp.where(sels[kk], before, 0.0), axis=0, keepdims=True)
        rank_ref[pl.ds(kk, 1), :] = rk.astype(jnp.int32)
    for j in range(tr // COMBINE_TILE):
        cols = slice(j * COMBINE_TILE, (j + 1) * COMBINE_TILE)
        tc = before[:, j * COMBINE_TILE:j * COMBINE_TILE + 1]
        tcar_ref[j] = tc.astype(jnp.int32)
        slack = tc - WIN_ALIGN * jnp.floor(tc * (1.0 / WIN_ALIGN))
        local = before[:, cols] - tc
        tile_cnt = jnp.broadcast_to(jnp.sum(onehot[:, cols], axis=1, keepdims=True), local.shape)
        run_start = jnp.dot(low_ref[...], tile_cnt.astype(BF16), preferred_element_type=F32)
        for kk in range(TOP_K):
            selk = sels[kk][:, cols]
            lr = jnp.sum(jnp.where(selk, local, 0.0), axis=0, keepdims=True)
            sl = jnp.sum(jnp.where(selk, slack, 0.0), axis=0, keepdims=True)
            lp = jnp.sum(jnp.where(selk, run_start + local, 0.0), axis=0, keepdims=True)
            lpos_ref[pl.ds(kk, 1), cols] = lp.astype(jnp.int32)
            lrank_ref[pl.ds(kk, 1), cols] = lr.astype(jnp.int32)
            col_ref[pl.ds(kk, 1), cols] = (idxs[kk][:, cols] * float(WIN_ROWS) + sl + lr).astype(jnp.int32)
    carry[...] = carry[...] + jnp.sum(onehot, axis=1, keepdims=True)
    cnt_ref[...] = carry[...].astype(jnp.int32)


def _router(logits_t):
    n = logits_t.shape[1]
    tr = ROUTE_BLOCK
    assert n % tr == 0
    tri = jnp.asarray(np.triu(np.ones((tr, tr), np.float32), k=1), BF16)
    low = jnp.asarray(np.tril(np.ones((N_EXPERTS, N_EXPERTS), np.float32), k=-1), BF16)
    per_tok = pl.BlockSpec((TOP_K, tr), lambda i: (0, i))
    return pl.pallas_call(
        _router_kernel,
        grid=(n // tr,),
        in_specs=[pl.BlockSpec((N_EXPERTS, tr), lambda i: (0, i)),
                  pl.BlockSpec((tr, tr), lambda i: (0, 0)),
                  pl.BlockSpec((N_EXPERTS, N_EXPERTS), lambda i: (0, 0))],
        out_specs=[per_tok, per_tok, per_tok, per_tok, per_tok, per_tok,
                   pl.BlockSpec((tr // COMBINE_TILE, N_EXPERTS, 1), lambda i: (i, 0, 0)),
                   pl.BlockSpec((N_EXPERTS, 1), lambda i: (0, 0))],
        out_shape=[jax.ShapeDtypeStruct((TOP_K, n), jnp.int32),
                   jax.ShapeDtypeStruct((TOP_K, n), jnp.int32),
                   jax.ShapeDtypeStruct((TOP_K, n), F32),
                   jax.ShapeDtypeStruct((TOP_K, n), jnp.int32),
                   jax.ShapeDtypeStruct((TOP_K, n), jnp.int32),
                   jax.ShapeDtypeStruct((TOP_K, n), jnp.int32),
                   jax.ShapeDtypeStruct((n // COMBINE_TILE, N_EXPERTS, 1), jnp.int32),
                   jax.ShapeDtypeStruct((N_EXPERTS, 1), jnp.int32)],
        scratch_shapes=[pltpu.VMEM((N_EXPERTS, 1), F32)],
        compiler_params=pltpu.CompilerParams(dimension_semantics=("arbitrary",)),
        name="router",
    )(logits_t, tri, low)


def _dispatch_kernel(lstart_ref, cnt_ref, dst_ref, h2_ref, lpos_ref, xs_hbm, stg, sem):
    i = pl.program_id(0)
    n_tiles = pl.num_programs(0)
    dt = h2_ref.shape[0]
    rows = dt * TOP_K
    slot = i % 2

    def drain(s):
        pltpu.make_async_copy(stg.at[s], xs_hbm.at[pl.ds(0, rows)], sem.at[s]).wait()

    @pl.when(i >= 2)
    def _():
        drain(slot)

    rid = lax.broadcasted_iota(jnp.int32, (rows, dt), 0)
    hit = rid == lpos_ref[0:1, :]
    for kk in range(1, TOP_K):
        hit = jnp.logical_or(hit, rid == lpos_ref[kk:kk + 1, :])
    perm = jnp.where(hit, 1.0, 0.0).astype(BF16)
    srt = jnp.dot(perm, h2_ref[...], preferred_element_type=F32)
    packed = _pack_bf16_pairs(srt)
    for c in range(PACK_CHUNKS):
        stg[slot, :, c, :] = packed[:, c * LANES:(c + 1) * LANES]

    def copy_pieces(n, src0, dst0, off, pieces):
        for piece in pieces:
            take = (n & piece) != 0

            @pl.when(take)
            def _(off=off, piece=piece):
                pltpu.make_async_copy(stg.at[slot, pl.ds(src0 + off, piece)],
                                      xs_hbm.at[pl.ds(dst0 + off, piece)], sem.at[slot]).start()

            off = off + jnp.where(take, piece, 0)

    small = [p for p in (16, 8, 4, 2, 1) if p <= dt]
    large = [p for p in (128, 64, 32) if p <= dt]
    assert dt <= 128
    for e in range(N_EXPERTS):
        n = cnt_ref[i * N_EXPERTS + e]
        src0 = lstart_ref[i * N_EXPERTS + e]
        dst0 = dst_ref[i * N_EXPERTS + e]
        n_large = n & ~jnp.int32(31)

        @pl.when(n_large != 0)
        def _(n=n, src0=src0, dst0=dst0):
            copy_pieces(n, src0, dst0, jnp.int32(0), large)

        copy_pieces(n, src0, dst0, n_large, small)

    @pl.when(i == n_tiles - 1)
    def _():
        drain(slot)

        @pl.when(n_tiles >= 2)
        def _():
            drain(1 - slot)


def _dispatch(h2, lpos, lstart, cnt, dst, cap):
    n_tok = h2.shape[0]
    dt = DISPATCH_TILE
    assert n_tok % dt == 0
    grid_spec = pltpu.PrefetchScalarGridSpec(
        num_scalar_prefetch=3,
        grid=(n_tok // dt,),
        in_specs=[pl.BlockSpec((dt, D_MODEL), lambda i, a, b, c: (i, 0)),
                  pl.BlockSpec((TOP_K, dt), lambda i, a, b, c: (0, i)),
                  ],
        out_specs=pl.BlockSpec(memory_space=pl.ANY),
        scratch_shapes=[pltpu.VMEM((2, dt * TOP_K, PACK_CHUNKS, LANES), jnp.uint32),
                        pltpu.SemaphoreType.DMA((2,))],
    )
    return pl.pallas_call(
        _dispatch_kernel,
        grid_spec=grid_spec,
        out_shape=jax.ShapeDtypeStruct((cap, PACK_CHUNKS, LANES), jnp.uint32),
        compiler_params=pltpu.CompilerParams(dimension_semantics=("arbitrary",),
                                             vmem_limit_bytes=VMEM_LIMIT),
        name="dispatch",
    )(lstart, cnt, dst, h2, lpos)


def _expert_kernel(n_xblocks, bexp_ref, nvalid_ref, epos_ref, elist_ref,
                   x_hbm, wgu_hbm, bgu_ref, wd_hbm, bd_ref, y_ref,
                   wgu_f32, wd_f32, wgu_bf, wd_bf, xbuf, xsem, wsem):
    i = pl.program_id(0)
    rb = y_ref.shape[0]
    nvalid = nvalid_ref[i]
    pos = epos_ref[i]
    fresh = jnp.logical_or(i == 0, pos != epos_ref[jnp.maximum(i - 1, 0)])
    slot = i % 2

    def x_copies(blk, s):
        return [pltpu.make_async_copy(x_hbm.at[pl.ds(blk * rb, rb), c, :],
                                      xbuf.at[s, :, pl.ds(c * LANES, LANES)], xsem.at[s])
                for c in range(PACK_CHUNKS)]

    def w_copies(p):
        e = elist_ref[p]
        s = p % 2
        return [pltpu.make_async_copy(wgu_hbm.at[e], wgu_f32.at[s], wsem.at[s, 0]),
                pltpu.make_async_copy(wd_hbm.at[e], wd_f32.at[s], wsem.at[s, 1])]

    @pl.when(i == 0)
    def _():
        for cp in x_copies(0, 0):
            cp.start()

        @pl.when(nvalid > 0)
        def _():
            for cp in w_copies(0):
                cp.start()

    @pl.when(i + 1 < n_xblocks)
    def _():
        for cp in x_copies(i + 1, 1 - slot):
            cp.start()

    @pl.when(jnp.logical_and(fresh, nvalid > 0))
    def _():
        @pl.when(elist_ref[pos + 1] >= 0)
        def _():
            for cp in w_copies(pos + 1):
                cp.start()

        for cp in w_copies(pos):
            cp.wait()
        ws = pos % 2
        chunk = 32

        def cast_gu(r, c):
            r0 = pl.multiple_of(r * chunk, chunk)
            wgu_bf[pl.ds(r0, chunk), :] = wgu_f32[ws, pl.ds(r0, chunk), :].astype(BF16)
            return c

        def cast_d(r, c):
            r0 = pl.multiple_of(r * chunk, chunk)
            wd_bf[pl.ds(r0, chunk), :] = wd_f32[ws, pl.ds(r0, chunk), :].astype(BF16)
            return c

        lax.fori_loop(0, D_MODEL // chunk, cast_gu, 0)
        lax.fori_loop(0, D_EXPERT // chunk, cast_d, 0)

    @pl.when(i < n_xblocks)
    def _():
        for cp in x_copies(i, slot):
            cp.wait()

    @pl.when(nvalid > 0)
    def _():
        xw = xbuf[slot]
        xw = jnp.where(lax.broadcasted_iota(jnp.int32, xw.shape, 0) < nvalid, xw, jnp.uint32(0))
        xlo, xhi = _unpack_bf16_pairs(xw)
        half = D_MODEL // 2
        g = (jnp.dot(xlo, wgu_bf[0:half, 0:D_EXPERT], preferred_element_type=F32)
             + jnp.dot(xhi, wgu_bf[half:, 0:D_EXPERT], preferred_element_type=F32)
             + bgu_ref[0, :, 0:D_EXPERT])
        u = (jnp.dot(xlo, wgu_bf[0:half, D_EXPERT:], preferred_element_type=F32)
             + jnp.dot(xhi, wgu_bf[half:, D_EXPERT:], preferred_element_type=F32)
             + bgu_ref[0, :, D_EXPERT:])
        g = jnp.minimum(g, SWIGLU_LIMIT)
        u = jnp.clip(u, -SWIGLU_LIMIT, SWIGLU_LIMIT)
        act = g * (1.0 / (1.0 + jnp.exp(-SWIGLU_ALPHA * g))) * (u + 1.0)
        y = jnp.dot(act.astype(BF16), wd_bf[...], preferred_element_type=F32) + bd_ref[0]
        row = lax.broadcasted_iota(jnp.int32, y.shape, 0)
        y_ref[...] = jnp.where(row < nvalid, y, 0.0).astype(BF16)

    @pl.when(nvalid == 0)
    def _():
        y_ref[...] = jnp.zeros_like(y_ref)


def _experts(xs, block_expert, nvalid, block_pos, expert_list, wgu, bgu, wd, bd):
    rb = EXPERT_ROWS
    n_xblocks = xs.shape[0] // rb
    nblk = n_xblocks + 1
    any_space = pl.BlockSpec(memory_space=pl.ANY)
    grid_spec = pltpu.PrefetchScalarGridSpec(
        num_scalar_prefetch=4,
        grid=(nblk,),
        in_specs=[
            any_space,
            any_space,
            pl.BlockSpec((1, 1, 2 * D_EXPERT), lambda i, be, nu, ep, el: (be[i], 0, 0)),
            any_space,
            pl.BlockSpec((1, 1, D_MODEL), lambda i, be, nu, ep, el: (be[i], 0, 0)),
        ],
        out_specs=pl.BlockSpec((rb, D_MODEL), lambda i, be, nu, ep, el: (i, 0)),
        scratch_shapes=[pltpu.VMEM((2, D_MODEL, 2 * D_EXPERT), F32),
                        pltpu.VMEM((2, D_EXPERT, D_MODEL), F32),
                        pltpu.VMEM((D_MODEL, 2 * D_EXPERT), BF16),
                        pltpu.VMEM((D_EXPERT, D_MODEL), BF16),
                        pltpu.VMEM((2, rb, D_MODEL // 2), jnp.uint32),
                        pltpu.SemaphoreType.DMA((2,)),
                        pltpu.SemaphoreType.DMA((2, 2))],
    )
    return pl.pallas_call(
        functools.partial(_expert_kernel, n_xblocks),
        grid_spec=grid_spec,
        out_shape=jax.ShapeDtypeStruct((nblk * rb, D_MODEL), BF16),
        compiler_params=pltpu.CompilerParams(dimension_semantics=("arbitrary",),
                                             vmem_limit_bytes=VMEM_LIMIT),
        name="experts",
    )(block_expert, nvalid, block_pos, expert_list, xs, wgu, bgu.reshape(N_EXPERTS, 1, 2 * D_EXPERT), wd,
      bd.reshape(N_EXPERTS, 1, D_MODEL))


def _combine_kernel(n_prompt_tiles, offa_ref, nchunk_ref,
                    x1p_ref, x1s_ref, col_ref, lrank_ref, gate_ref, gfin_ref, yb_hbm,
                    outp_ref, outs_ref, ybuf, acc_ref, sem):
    i = pl.program_id(0)
    n_tiles = pl.num_programs(0)

    def window_copy(tile, chunk, e, slot):
        base = pl.multiple_of(offa_ref[tile * N_EXPERTS + e] + chunk * WIN, WIN_ALIGN)
        return pltpu.make_async_copy(yb_hbm.at[pl.ds(base, WIN_ROWS), :],
                                     ybuf.at[slot, pl.ds(e * WIN_ROWS, WIN_ROWS), :],
                                     sem.at[slot])

    def start_windows(tile, chunk, slot):
        for e in range(N_EXPERTS):
            window_copy(tile, chunk, e, slot).start()

    def wait_windows(tile, chunk, slot):
        for e in range(N_EXPERTS):
            window_copy(tile, chunk, e, slot).wait()

    slot = i % 2

    @pl.when(i == 0)
    def _():
        start_windows(0, 0, 0)

    @pl.when(i + 1 < n_tiles)
    def _():
        start_windows(i + 1, 0, 1 - slot)

    wait_windows(i, 0, slot)
    lane = lax.broadcasted_iota(jnp.int32, (COMBINE_TILE, N_EXPERTS * WIN_ROWS), 1)

    def moe_rows(chunk, buf):
        g = jnp.zeros(lane.shape, F32)
        for kk in range(TOP_K):
            lr = lrank_ref[:, kk:kk + 1]
            in_chunk = jnp.logical_and(lr >= chunk * WIN, lr < chunk * WIN + WIN)
            colk = jnp.where(in_chunk, col_ref[:, kk:kk + 1] - chunk * WIN, -1)
            g = jnp.where(lane == colk, gate_ref[:, kk:kk + 1], g)
        hi = g.astype(BF16)
        lo = (g - hi.astype(F32)).astype(BF16)
        r = jnp.dot(jnp.concatenate([hi, lo], axis=0), ybuf[buf], preferred_element_type=F32)
        return r[0:COMBINE_TILE] + r[COMBINE_TILE:]

    acc_ref[...] = moe_rows(0, slot)

    def extra_chunk(j, c):
        start_windows(i, j, 2)
        wait_windows(i, j, 2)
        acc_ref[...] += moe_rows(j, 2)
        return c

    lax.fori_loop(1, nchunk_ref[i], extra_chunk, 0)

    @pl.when(i < n_prompt_tiles)
    def _():
        outp_ref[...] = _rms(x1p_ref[...] + acc_ref[...], gfin_ref[...])

    @pl.when(i >= n_prompt_tiles)
    def _():
        outs_ref[...] = _rms(x1s_ref[...] + acc_ref[...], gfin_ref[...])


def _combine(x1_p, x1_s, col_t, lrank_t, gates_t, gfin, yb, offa, nchunk):
    ct = COMBINE_TILE
    n_p, n_s = x1_p.shape[0] // ct, x1_s.shape[0] // ct
    assert x1_p.shape[0] % ct == 0 and x1_s.shape[0] % ct == 0 and n_s >= 1
    tok_spec = pl.BlockSpec((ct, TOP_K), lambda i, o, c: (i, 0))
    grid_spec = pltpu.PrefetchScalarGridSpec(
        num_scalar_prefetch=2,
        grid=(n_p + n_s,),
        in_specs=[
            pl.BlockSpec((ct, D_MODEL), lambda i, o, c: (jnp.minimum(i, n_p - 1), 0)),
            pl.BlockSpec((ct, D_MODEL), lambda i, o, c: (jnp.maximum(i - n_p, 0), 0)),
            tok_spec, tok_spec, tok_spec,
            pl.BlockSpec((1, D_MODEL), lambda i, o, c: (0, 0)),
            pl.BlockSpec(memory_space=pl.ANY),
        ],
        out_specs=[
            pl.BlockSpec((ct, D_MODEL), lambda i, o, c: (jnp.minimum(i, n_p - 1), 0)),
            pl.BlockSpec((ct, D_MODEL), lambda i, o, c: (jnp.maximum(i - n_p, 0), 0)),
        ],
        scratch_shapes=[pltpu.VMEM((3, N_EXPERTS * WIN_ROWS, D_MODEL), BF16),
                        pltpu.VMEM((ct, D_MODEL), F32),
                        pltpu.SemaphoreType.DMA((3,))],
    )
    return pl.pallas_call(
        functools.partial(_combine_kernel, n_p),
        grid_spec=grid_spec,
        out_shape=[jax.ShapeDtypeStruct(x1_p.shape, F32), jax.ShapeDtypeStruct(x1_s.shape, F32)],
        compiler_params=pltpu.CompilerParams(dimension_semantics=("arbitrary",),
                                             vmem_limit_bytes=VMEM_LIMIT),
        name="combine",
    )(offa, nchunk, x1_p, x1_s, col_t, lrank_t, gates_t, gfin, yb)


def kernel(x_prompt, x_sample, cache_k, cache_v, state_pool, meta_tokens, norm_attn, w_in, attn_sinks,
           w_pool, pool_scale, w_out, norm_ffn, router_w, router_b, w_gate_up, b_gate_up, w_down, b_down,
           norm_final):
    assert w_in.shape[0] == 1, "single-layer trunk"
    bsz, seq, _ = x_prompt.shape
    assert bsz == 1
    nb = x_sample.shape[0]
    n_tok = seq + nb
    gattn = norm_attn[0].reshape(1, D_MODEL)
    gffn = norm_ffn[0].reshape(1, D_MODEL)
    win = w_in[0].astype(BF16)
    wpool = w_pool[0].astype(BF16)
    wout = w_out[0].astype(BF16)
    pscale = pool_scale[0].reshape(1, POOL_WIDTH)
    rw_t = router_w[0].T
    rw_hi = rw_t.astype(BF16)
    rwt = jnp.stack([rw_hi, (rw_t - rw_hi.astype(F32)).astype(BF16)])
    rb = router_b[0].reshape(N_EXPERTS, 1)
    sinks = attn_sinks[0]

    ck = cache_k[0].reshape(nb, N_META + WINDOW, KV_WIDTH)
    cv = cache_v[0].reshape(nb, N_META + WINDOW, KV_WIDTH)
    (x1_s, h2_s, lgt_s, knew, vnew, pnew) = _sample_mixer(
        x_sample[:, 0], ck, cv, state_pool[0], gattn, win, wpool, pscale, wout, gffn, rwt, rb, sinks)
    (x1_p, h2_all, lgt_all, kmeta, vmeta, ktail, vtail, ptail) = _prompt_mixer(
        x_prompt[0], meta_tokens, gattn, win, wpool, pscale, wout, gffn, rwt, rb, sinks, h2_s, lgt_s)

    _eidx, _rank, gates, col, lrank, lpos, tcar, counts = _router(lgt_all)
    counts = counts[:, 0]
    tcar = tcar[:, :, 0]
    rbk = EXPERT_ROWS
    eids = jnp.arange(N_EXPERTS, dtype=jnp.int32)
    earlier = eids[None, :] < eids[:, None]
    excl_sum = lambda a: jnp.sum(jnp.where(earlier, a[..., None, :], 0), axis=-1)
    padded = (counts + rbk - 1) // rbk * rbk
    pad_start = excl_sum(padded).astype(jnp.int32)
    pad_end = pad_start + padded
    nblk = -(-(n_tok * TOP_K) // rbk) + N_EXPERTS
    cap = nblk * rbk
    block_start = jnp.arange(nblk + 1, dtype=jnp.int32) * rbk
    owns = (pad_start[None, :] <= block_start[:, None]) & (block_start[:, None] < pad_end[None, :])
    nvalid = jnp.sum(jnp.where(owns, jnp.clip(counts[None, :] - (block_start[:, None] - pad_start[None, :]),
                                              0, rbk), 0), axis=1).astype(jnp.int32)
    has_rows = counts > 0
    last_e = jnp.max(jnp.where(has_rows, eids, 0))
    block_expert = jnp.where(jnp.any(owns, axis=1), jnp.sum(jnp.where(owns, eids[None, :], 0), axis=1),
                             last_e).astype(jnp.int32)

    run_len = jnp.concatenate([tcar[1:], counts[None, :]], axis=0) - tcar
    run_lstart = excl_sum(run_len)
    run_dst = pad_start[None, :] + tcar
    flat = lambda a: a.astype(jnp.int32).reshape(-1)
    xs = _dispatch(h2_all, lpos, flat(run_lstart), flat(run_len), flat(run_dst), cap)
    expert_pos = excl_sum(has_rows.astype(jnp.int32))
    at_pos = has_rows[None, :] & (expert_pos[None, :] == jnp.arange(N_EXPERTS + 1, dtype=jnp.int32)[:, None])
    expert_list = jnp.where(jnp.any(at_pos, axis=1), jnp.sum(jnp.where(at_pos, eids[None, :], 0), axis=1),
                            -1).astype(jnp.int32)
    block_pos = jnp.sum(jnp.where(block_expert[:, None] == eids[None, :], expert_pos[None, :], 0),
                        axis=1).astype(jnp.int32)
    yb = _experts(xs, block_expert, nvalid, block_pos, expert_list,
                  w_gate_up[0], b_gate_up[0], w_down[0], b_down[0])

    offa = (pad_start[None, :] + (tcar - tcar % WIN_ALIGN)).astype(jnp.int32).reshape(-1)
    nchunk = jnp.maximum(jnp.max((run_len + WIN - 1) // WIN, axis=1), 1).astype(jnp.int32)
    gfin = norm_final.reshape(1, D_MODEL)
    y_prompt, y_sample = _combine(x1_p, x1_s, col.T, lrank.T, gates.T, gfin, yb, offa, nchunk)

    kv_shape = (1, 1, N_META + WINDOW, N_KV_HEADS, HEAD_DIM)
    new_k_p = jnp.concatenate([kmeta, ktail], axis=0).reshape(kv_shape)
    new_v_p = jnp.concatenate([vmeta, vtail], axis=0).reshape(kv_shape)
    new_pool_p = ptail[16 - POOL_STATE:].reshape(1, 1, POOL_STATE, POOL_WIDTH)
    new_k_s = jnp.concatenate([ck[:, :N_META], ck[:, N_META + 1:], knew[:, None]], axis=1).reshape(
        (1, nb, N_META + WINDOW, N_KV_HEADS, HEAD_DIM))
    new_v_s = jnp.concatenate([cv[:, :N_META], cv[:, N_META + 1:], vnew[:, None]], axis=1).reshape(
        (1, nb, N_META + WINDOW, N_KV_HEADS, HEAD_DIM))
    new_pool_s = jnp.concatenate([state_pool[0][:, 1:], pnew[:, None]], axis=1)[None]
    return (y_prompt[None], y_sample[:, None], new_k_p, new_v_p, new_pool_p, new_k_s, new_v_s, new_pool_s)
```

```python
import functools

import jax
import jax.numpy as jnp
import numpy as np
from jax import lax
from jax.experimental import pallas as pl
from jax.experimental.pallas import tpu as pltpu

F32 = jnp.float32
BF16 = jnp.bfloat16

D_MODEL = 1024
N_META = 16
N_HEADS = 8
HEAD_DIM = 64
N_KV_HEADS = 2
GQA_GROUP = N_HEADS // N_KV_HEADS
ATTN_WIDTH = N_HEADS * HEAD_DIM
KV_WIDTH = N_KV_HEADS * HEAD_DIM
WINDOW = 128
POOL_WIDTH = D_MODEL - ATTN_WIDTH
POOL_WINDOWS = (2, 4, 8, 16)
POOL_GROUP_DIM = POOL_WIDTH // len(POOL_WINDOWS)
POOL_STATE = max(POOL_WINDOWS) - 1
N_EXPERTS = 32
TOP_K = 4
D_EXPERT = D_MODEL
SWIGLU_ALPHA = 1.702
SWIGLU_LIMIT = 7.0
NORM_EPS = 1e-5
PAST_LEN = 16384

LANES = 128
QSUB = 64
KEYS_SUB = QSUB + WINDOW
META_PAD = 64
NKEY = META_PAD + KEYS_SUB
MASKED = -1e30
PROMPT_BLOCK = 512
ROUTE_BLOCK = 384
EXPERT_ROWS = 512
DISPATCH_TILE = 128
PACK_CHUNKS = D_MODEL // 2 // LANES
COMBINE_TILE = 128
WIN = 32
WIN_ALIGN = 16
WIN_ROWS = WIN + WIN_ALIGN
VMEM_LIMIT = 56 * 1024 * 1024


def _rms(x, g):
    return x * lax.rsqrt(jnp.mean(x * x, axis=-1, keepdims=True) + NORM_EPS) * g


def _router_logits(rwt_ref, h2, h2_hi):
    nt = (((1,), (1,)), ((), ()))
    h2_lo = (h2 - h2_hi.astype(F32)).astype(BF16)
    return (lax.dot_general(rwt_ref[0], h2_hi, nt, preferred_element_type=F32)
            + lax.dot_general(rwt_ref[0], h2_lo, nt, preferred_element_type=F32)
            + lax.dot_general(rwt_ref[1], h2_hi, nt, preferred_element_type=F32))


def _dup_halves(a):
    lane = lax.broadcasted_iota(jnp.int32, a.shape, 1)
    r = pltpu.roll(a, HEAD_DIM, axis=1)
    lo = lane < HEAD_DIM
    return jnp.where(lo, a, r), jnp.where(lo, r, a)


def _pool_means(pext_ref, n):
    outs = []
    for gi, w in enumerate(POOL_WINDOWS):
        xg = pext_ref[:, gi * POOL_GROUP_DIM:(gi + 1) * POOL_GROUP_DIM]
        s = xg
        sh = 1
        while sh < w:
            s = s + pltpu.roll(s, sh, axis=0)
            sh *= 2
        outs.append(s[16:] * (1.0 / w) - xg[16:])
    return outs


def _pack_bf16_pairs(h):
    m = h.shape[1] // 2
    lo = pltpu.bitcast(h[:, :m].astype(BF16).astype(F32), jnp.uint32)
    hi = pltpu.bitcast(h[:, m:].astype(BF16).astype(F32), jnp.uint32)
    return lax.shift_right_logical(lo, jnp.uint32(16)) | (hi & jnp.uint32(0xFFFF0000))


def _unpack_bf16_pairs(w):
    lo = pltpu.bitcast(lax.shift_left(w, jnp.uint32(16)), F32).astype(BF16)
    hi = pltpu.bitcast(w & jnp.uint32(0xFFFF0000), F32).astype(BF16)
    return lo, hi


def _prompt_kernel(x_ref, meta_ref, gattn_ref, win_ref, wpool_ref, pscale_ref, wout_ref, gffn_ref,
                   rwt_ref, rb_ref, sink_ref, tbl_ref, tail_h2_ref, tail_lgt_ref,
                   x1_ref, h2_ref, lgt_ref, kmeta_ref, vmeta_ref, ktail_ref, vtail_ref, ptail_ref,
                   k2buf, v2buf, km2, vm2, qbuf, obuf, pext):
    pid = pl.program_id(0)
    n_main = pl.num_programs(0) - 1
    refs = (x_ref, meta_ref, gattn_ref, win_ref, wpool_ref, pscale_ref, wout_ref, gffn_ref,
            rwt_ref, rb_ref, sink_ref, tbl_ref,
            x1_ref, h2_ref, lgt_ref, kmeta_ref, vmeta_ref, ktail_ref, vtail_ref, ptail_ref,
            k2buf, v2buf, km2, vm2, qbuf, obuf, pext)

    @pl.when(pid < n_main)
    def _():
        _prompt_block(*refs)

    @pl.when(pid == n_main)
    def _():
        h2_ref[0:tail_h2_ref.shape[0], :] = tail_h2_ref[...]
        lgt_ref[:, 0:tail_lgt_ref.shape[1]] = tail_lgt_ref[...]


def _prompt_block(x_ref, meta_ref, gattn_ref, win_ref, wpool_ref, pscale_ref, wout_ref, gffn_ref,
                  rwt_ref, rb_ref, sink_ref, tbl_ref,
                  x1_ref, h2_ref, lgt_ref, kmeta_ref, vmeta_ref, ktail_ref, vtail_ref, ptail_ref,
                  k2buf, v2buf, km2, vm2, qbuf, obuf, pext):
    tb = x_ref.shape[0]
    pid = pl.program_id(0)

    @pl.when(pid == 0)
    def _():
        hm = _rms(meta_ref[...], gattn_ref[...]).astype(BF16)
        km = jnp.dot(hm, win_ref[:, ATTN_WIDTH:ATTN_WIDTH + KV_WIDTH], preferred_element_type=F32)
        vm = jnp.dot(hm, win_ref[:, ATTN_WIDTH + KV_WIDTH:ATTN_WIDTH + 2 * KV_WIDTH],
                     preferred_element_type=F32)
        pm = jnp.dot(hm, win_ref[:, ATTN_WIDTH + 2 * KV_WIDTH:], preferred_element_type=F32)
        kmeta_ref[...] = km
        vmeta_ref[...] = vm
        zpad = jnp.zeros((META_PAD - N_META, LANES), F32)
        k0, k1 = _dup_halves(jnp.concatenate([km, zpad], axis=0))
        v0, v1 = _dup_halves(jnp.concatenate([vm, zpad], axis=0))
        km2[0] = k0.astype(BF16)
        km2[1] = k1.astype(BF16)
        vm2[0, :, 0:LANES] = v0.astype(BF16)
        vm2[1, :, 0:LANES] = v1.astype(BF16)
        vm2[:, :, LANES:] = jnp.ones((N_KV_HEADS, META_PAD, LANES), BF16)
        k2buf[:, 0:WINDOW, :] = jnp.zeros((2, WINDOW, LANES), BF16)
        v2buf[:, 0:WINDOW, 0:LANES] = jnp.zeros((2, WINDOW, LANES), BF16)
        v2buf[:, :, LANES:] = jnp.ones((N_KV_HEADS, WINDOW + tb, LANES), BF16)
        pext[0:16, :] = pm

    h = _rms(x_ref[...], gattn_ref[...]).astype(BF16)
    q = jnp.dot(h, win_ref[:, 0:ATTN_WIDTH], preferred_element_type=F32) * (HEAD_DIM ** -0.5)
    lane_t = lax.broadcasted_iota(jnp.int32, (tb, LANES), 1)
    for c in range(N_HEADS // 2):
        tile = q[:, c * LANES:(c + 1) * LANES]
        for a in range(2):
            keep = (lane_t < HEAD_DIM) if a == 0 else (lane_t >= HEAD_DIM)
            piece = jnp.where(keep, tile, 0.0).astype(BF16).reshape(tb // QSUB, QSUB, LANES)
            row = ((c % 2) * 2 + a) * QSUB
            qbuf[c // 2, :, row:row + QSUB, :] = piece
    k = jnp.dot(h, win_ref[:, ATTN_WIDTH:ATTN_WIDTH + KV_WIDTH], preferred_element_type=F32)
    v = jnp.dot(h, win_ref[:, ATTN_WIDTH + KV_WIDTH:ATTN_WIDTH + 2 * KV_WIDTH], preferred_element_type=F32)
    p = jnp.dot(h, win_ref[:, ATTN_WIDTH + 2 * KV_WIDTH:], preferred_element_type=F32)
    ktail_ref[...] = k[tb - WINDOW:]
    vtail_ref[...] = v[tb - WINDOW:]
    ptail_ref[...] = p[tb - 16:]
    k0, k1 = _dup_halves(k)
    v0, v1 = _dup_halves(v)
    k2buf[0, WINDOW:, :] = k0.astype(BF16)
    k2buf[1, WINDOW:, :] = k1.astype(BF16)
    v2buf[0, WINDOW:, 0:LANES] = v0.astype(BF16)
    v2buf[1, WINDOW:, 0:LANES] = v1.astype(BF16)
    pext[16:, :] = p

    lane_q = lax.broadcasted_iota(jnp.int32, (QSUB, LANES), 1)
    lo_q = lane_q < HEAD_DIM

    for u in range(tb // QSUB):
        r0 = u * QSUB
        sel = jnp.where(pid == 0, u + 1, 0) if u < WINDOW // QSUB else 0
        for g in range(N_KV_HEADS):
            qm = qbuf[g, u]
            kwin = jnp.concatenate([km2[g], k2buf[g, r0:r0 + KEYS_SUB, :]], axis=0)
            vwin = jnp.concatenate([vm2[g], v2buf[g, r0:r0 + KEYS_SUB, :]], axis=0)
            s = lax.dot_general(qm, kwin, (((1,), (1,)), ((), ())), preferred_element_type=F32)
            s = s + tbl_ref[sel, g]
            sink = sink_ref[g]
            m = jnp.maximum(jnp.max(s, axis=1, keepdims=True), sink)
            e = jnp.exp(s - m).astype(BF16)
            r = jnp.dot(e, vwin, preferred_element_type=F32)
            o = r[:, 0:LANES] / (r[:, LANES:] + jnp.exp(sink - m))
            o0 = jnp.where(lo_q, o[0:QSUB], o[QSUB:2 * QSUB])
            o1 = jnp.where(lo_q, o[2 * QSUB:3 * QSUB], o[3 * QSUB:])
            obuf[r0:r0 + QSUB, (2 * g) * LANES:(2 * g + 1) * LANES] = o0.astype(BF16)
            obuf[r0:r0 + QSUB, (2 * g + 1) * LANES:(2 * g + 2) * LANES] = o1.astype(BF16)

    pooled = _pool_means(pext, tb)
    for gi in range(len(POOL_WINDOWS)):
        y = jnp.dot(pooled[gi].astype(BF16), wpool_ref[gi], preferred_element_type=F32)
        y = y * pscale_ref[:, gi * POOL_GROUP_DIM:(gi + 1) * POOL_GROUP_DIM]
        obuf[:, ATTN_WIDTH + gi * POOL_GROUP_DIM:ATTN_WIDTH + (gi + 1) * POOL_GROUP_DIM] = y.astype(BF16)

    k2buf[:, 0:WINDOW, :] = k2buf[:, tb:tb + WINDOW, :]
    v2buf[:, 0:WINDOW, 0:LANES] = v2buf[:, tb:tb + WINDOW, 0:LANES]
    pext[0:16, :] = pext[tb:tb + 16, :]

    x1 = x_ref[...] + jnp.dot(obuf[...], wout_ref[...], preferred_element_type=F32)
    x1_ref[...] = x1
    h2 = _rms(x1, gffn_ref[...])
    h2_hi = h2.astype(BF16)
    h2_ref[...] = h2_hi
    lgt_ref[...] = _router_logits(rwt_ref, h2, h2_hi) + rb_ref[...]


def _attn_tables(sinks):
    i = np.arange(QSUB)[:, None]
    j = np.arange(NKEY)[None, :]
    jb = j - META_PAD
    rel = i + WINDOW - jb
    band_ok = (jb >= 0) & (rel >= 0) & (rel <= WINDOW)
    meta_ok = (j < N_META) & (i >= 0)
    slopes = np.exp2(-8.0 * np.arange(1, N_HEADS + 1) / N_HEADS)
    tbl = np.empty((3, N_KV_HEADS, GQA_GROUP * QSUB, NKEY), np.float32)
    for var in range(3):
        ok = band_ok if var == 0 else band_ok & (jb >= WINDOW - (var - 1) * QSUB)
        for g in range(N_KV_HEADS):
            for a in range(GQA_GROUP):
                hd = g * GQA_GROUP + a
                bias = np.where(ok, -slopes[hd] * rel, MASKED)
                bias = np.where(meta_ok, 0.0, bias)
                tbl[var, g, a * QSUB:(a + 1) * QSUB] = bias
    sink_col = jnp.repeat(sinks.astype(F32).reshape(N_KV_HEADS, GQA_GROUP, 1), QSUB, axis=2)
    return jnp.asarray(tbl), sink_col.reshape(N_KV_HEADS, GQA_GROUP * QSUB, 1)


def _prompt_mixer(x, meta, gattn, win, wpool, pscale, wout, gffn, rwt, rb, sinks, tail_h2, tail_lgt):
    seq = x.shape[0]
    tb = PROMPT_BLOCK
    n_tail = tail_h2.shape[0]
    assert seq % tb == 0 and tb % WINDOW == 0 and n_tail <= tb
    nblk = seq // tb
    n_tok = seq + n_tail
    tbl, sink_col = _attn_tables(sinks)
    full = lambda *shape: pl.BlockSpec(shape, lambda i: (0,) * len(shape))
    main = lambda i: (jnp.minimum(i, nblk - 1), 0)
    in_width = win.shape[1]
    return pl.pallas_call(
        _prompt_kernel,
        grid=(nblk + 1,),
        in_specs=[
            pl.BlockSpec((tb, D_MODEL), main),
            full(N_META, D_MODEL), full(1, D_MODEL), full(D_MODEL, in_width),
            full(len(POOL_WINDOWS), POOL_GROUP_DIM, POOL_GROUP_DIM), full(1, POOL_WIDTH),
            full(D_MODEL, D_MODEL), full(1, D_MODEL), full(2, N_EXPERTS, D_MODEL), full(N_EXPERTS, 1),
            full(N_KV_HEADS, GQA_GROUP * QSUB, 1), full(3, N_KV_HEADS, GQA_GROUP * QSUB, NKEY),
            full(n_tail, D_MODEL), full(N_EXPERTS, n_tail),
        ],
        out_specs=[
            pl.BlockSpec((tb, D_MODEL), main),
            pl.BlockSpec((tb, D_MODEL), lambda i: (i, 0)),
            pl.BlockSpec((N_EXPERTS, tb), lambda i: (0, i)),
            full(N_META, KV_WIDTH), full(N_META, KV_WIDTH),
            full(WINDOW, KV_WIDTH), full(WINDOW, KV_WIDTH), full(16, POOL_WIDTH),
        ],
        out_shape=[
            jax.ShapeDtypeStruct((seq, D_MODEL), F32),
            jax.ShapeDtypeStruct((n_tok, D_MODEL), BF16),
            jax.ShapeDtypeStruct((N_EXPERTS, n_tok), F32),
            jax.ShapeDtypeStruct((N_META, KV_WIDTH), F32),
            jax.ShapeDtypeStruct((N_META, KV_WIDTH), F32),
            jax.ShapeDtypeStruct((WINDOW, KV_WIDTH), F32),
            jax.ShapeDtypeStruct((WINDOW, KV_WIDTH), F32),
            jax.ShapeDtypeStruct((16, POOL_WIDTH), F32),
        ],
        scratch_shapes=[
            pltpu.VMEM((N_KV_HEADS, WINDOW + tb, LANES), BF16),
            pltpu.VMEM((N_KV_HEADS, WINDOW + tb, 2 * LANES), BF16),
            pltpu.VMEM((N_KV_HEADS, META_PAD, LANES), BF16),
            pltpu.VMEM((N_KV_HEADS, META_PAD, 2 * LANES), BF16),
            pltpu.VMEM((N_KV_HEADS, tb // QSUB, GQA_GROUP * QSUB, LANES), BF16),
            pltpu.VMEM((tb, D_MODEL), BF16),
            pltpu.VMEM((16 + tb, POOL_WIDTH), F32),
        ],
        compiler_params=pltpu.CompilerParams(dimension_semantics=("arbitrary",),
                                             vmem_limit_bytes=VMEM_LIMIT),
        name="prompt_mixer",
    )(x, meta, gattn, win, wpool, pscale, wout, gffn, rwt, rb, sink_col, tbl, tail_h2, tail_lgt)


def _sample_kernel(x_ref, ck_ref, cv_ref, sp_ref, gattn_ref, win_ref, wpool_ref, pscale_ref, wout_ref,
                   gffn_ref, rwt_ref, rb_ref, sinkc_ref, bias_ref,
                   x1_ref, h2_ref, lgt_ref, knew_ref, vnew_ref, pnew_ref,
                   qm_buf, r_buf, obuf):
    nb = x_ref.shape[0]
    x = x_ref[...]
    h = _rms(x, gattn_ref[...]).astype(BF16)
    q = jnp.dot(h, win_ref[:, 0:ATTN_WIDTH], preferred_element_type=F32) * (HEAD_DIM ** -0.5)
    k = jnp.dot(h, win_ref[:, ATTN_WIDTH:ATTN_WIDTH + KV_WIDTH], preferred_element_type=F32)
    v = jnp.dot(h, win_ref[:, ATTN_WIDTH + KV_WIDTH:ATTN_WIDTH + 2 * KV_WIDTH], preferred_element_type=F32)
    p = jnp.dot(h, win_ref[:, ATTN_WIDTH + 2 * KV_WIDTH:], preferred_element_type=F32)
    knew_ref[...] = k
    vnew_ref[...] = v
    pnew_ref[...] = p

    lane = lax.broadcasted_iota(jnp.int32, (nb, LANES), 1)
    lo = lane < HEAD_DIM
    for hd in range(N_HEADS):
        tile = q[:, (hd // 2) * LANES:(hd // 2 + 1) * LANES]
        if (hd % 2) != (hd // GQA_GROUP):
            tile = pltpu.roll(tile, HEAD_DIM, axis=1)
        keep_lo = (hd // GQA_GROUP) == 0
        qm_buf[:, hd, :] = jnp.where(lo if keep_lo else jnp.logical_not(lo), tile, 0.0)

    def per_batch(b, carry):
        qm = qm_buf[b]
        kb = ck_ref[b].astype(BF16)
        vb = cv_ref[b].astype(BF16)
        s = lax.dot_general(qm.astype(BF16), kb, (((1,), (1,)), ((), ())), preferred_element_type=F32)
        s = s + bias_ref[...]
        kn = knew_ref[pl.ds(b, 1), :]
        vn = vnew_ref[pl.ds(b, 1), :]
        s_self = jnp.sum(qm * kn, axis=1, keepdims=True)
        sink = sinkc_ref[...]
        m = jnp.maximum(jnp.maximum(jnp.max(s, axis=1, keepdims=True), s_self), sink)
        e = jnp.exp(s - m)
        e_self = jnp.exp(s_self - m)
        den = jnp.sum(e, axis=1, keepdims=True) + e_self + jnp.exp(sink - m)
        r = jnp.dot(e.astype(BF16), vb, preferred_element_type=F32)
        r = r + e_self * vn
        r_buf[b] = r / den
        return carry

    lax.fori_loop(0, nb, per_batch, 0)

    for c in range(N_HEADS // 2):
        halves = []
        for a in range(2):
            hd = 2 * c + a
            t = r_buf[:, hd, :]
            if (hd // GQA_GROUP) != a:
                t = pltpu.roll(t, HEAD_DIM, axis=1)
            halves.append(t)
        obuf[:, c * LANES:(c + 1) * LANES] = jnp.where(lo, halves[0], halves[1]).astype(BF16)

    for gi, w in enumerate(POOL_WINDOWS):
        cols = slice(gi * POOL_GROUP_DIM, (gi + 1) * POOL_GROUP_DIM)
        pg = p[:, cols]
        acc = pg
        for d in range(1, w):
            acc = acc + sp_ref[:, POOL_STATE - d, cols]
        pooled = acc * (1.0 / w) - pg
        y = jnp.dot(pooled.astype(BF16), wpool_ref[gi], preferred_element_type=F32) * pscale_ref[:, cols]
        obuf[:, ATTN_WIDTH + gi * POOL_GROUP_DIM:ATTN_WIDTH + (gi + 1) * POOL_GROUP_DIM] = y.astype(BF16)

    x1 = x + jnp.dot(obuf[...], wout_ref[...], preferred_element_type=F32)
    x1_ref[...] = x1
    h2 = _rms(x1, gffn_ref[...])
    h2_hi = h2.astype(BF16)
    h2_ref[...] = h2_hi
    lgt_ref[...] = _router_logits(rwt_ref, h2, h2_hi) + rb_ref[...]


def _sample_mixer(x, ck, cv, sp, gattn, win, wpool, pscale, wout, gffn, rwt, rb, sinks):
    nb = x.shape[0]
    rows = ck.shape[1]
    slopes = np.exp2(-8.0 * np.arange(1, N_HEADS + 1) / N_HEADS)
    dist = np.concatenate([np.zeros(N_META), WINDOW - np.arange(WINDOW)])
    bias = jnp.asarray((-slopes[:, None] * dist[None, :]).astype(np.float32))
    vm = pl.BlockSpec(memory_space=pltpu.VMEM)
    return pl.pallas_call(
        _sample_kernel,
        in_specs=[vm] * 14,
        out_specs=[vm] * 6,
        out_shape=[
            jax.ShapeDtypeStruct((nb, D_MODEL), F32),
            jax.ShapeDtypeStruct((nb, D_MODEL), BF16),
            jax.ShapeDtypeStruct((N_EXPERTS, nb), F32),
            jax.ShapeDtypeStruct((nb, KV_WIDTH), F32),
            jax.ShapeDtypeStruct((nb, KV_WIDTH), F32),
            jax.ShapeDtypeStruct((nb, POOL_WIDTH), F32),
        ],
        scratch_shapes=[
            pltpu.VMEM((nb, N_HEADS, LANES), F32),
            pltpu.VMEM((nb, N_HEADS, LANES), F32),
            pltpu.VMEM((nb, D_MODEL), BF16),
        ],
        compiler_params=pltpu.CompilerParams(vmem_limit_bytes=VMEM_LIMIT),
        name="sample_mixer",
    )(x, ck, cv, sp, gattn, win, wpool, pscale, wout, gffn, rwt, rb,
      sinks.astype(F32).reshape(N_HEADS, 1), bias)


def _router_kernel(lg_ref, tri_ref, low_ref, eidx_ref, rank_ref, gate_ref, col_ref, lrank_ref, lpos_ref,
                   tcar_ref, cnt_ref, carry):
    tr = lg_ref.shape[1]

    @pl.when(pl.program_id(0) == 0)
    def _():
        carry[...] = jnp.zeros_like(carry)

    work = lg_ref[...]
    eio = lax.broadcasted_iota(jnp.int32, work.shape, 0).astype(F32)
    sels, vals, idxs = [], [], []
    for _k in range(TOP_K):
        mx = jnp.max(work, axis=0, keepdims=True)
        idx = jnp.min(jnp.where(work == mx, eio, float(N_EXPERTS)), axis=0, keepdims=True)
        sel = eio == idx
        sels.append(sel)
        vals.append(mx)
        idxs.append(idx)
        work = jnp.where(sel, -jnp.inf, work)
    exps = [jnp.exp(vk - vals[0]) for vk in vals]
    tot = exps[0] + exps[1] + exps[2] + exps[3]
    onehot = jnp.zeros(work.shape, F32)
    for sel in sels:
        onehot = onehot + sel.astype(F32)
    before = jnp.dot(onehot.astype(BF16), tri_ref[...], preferred_element_type=F32) + carry[...]
    for kk in range(TOP_K):
        eidx_ref[pl.ds(kk, 1), :] = idxs[kk].astype(jnp.int32)
        gate_ref[pl.ds(kk, 1), :] = exps[kk] / tot
        rk = jnp.sum(jnp.where(sels[kk], before, 0.0), axis=0, keepdims=True)
        rank_ref[pl.ds(kk, 1), :] = rk.astype(jnp.int32)
    for j in range(tr // COMBINE_TILE):
        cols = slice(j * COMBINE_TILE, (j + 1) * COMBINE_TILE)
        tc = before[:, j * COMBINE_TILE:j * COMBINE_TILE + 1]
        tcar_ref[j] = tc.astype(jnp.int32)
        slack = tc - WIN_ALIGN * jnp.floor(tc * (1.0 / WIN_ALIGN))
        local = before[:, cols] - tc
        tile_cnt = jnp.broadcast_to(jnp.sum(onehot[:, cols], axis=1, keepdims=True), local.shape)
        run_start = jnp.dot(low_ref[...], tile_cnt.astype(BF16), preferred_element_type=F32)
        for kk in range(TOP_K):
            selk = sels[kk][:, cols]
            lr = jnp.sum(jnp.where(selk, local, 0.0), axis=0, keepdims=True)
            sl = jnp.sum(jnp.where(selk, slack, 0.0), axis=0, keepdims=True)
            lp = jnp.sum(jnp.where(selk, run_start + local, 0.0), axis=0, keepdims=True)
            lpos_ref[pl.ds(kk, 1), cols] = lp.astype(jnp.int32)
            lrank_ref[pl.ds(kk, 1), cols] = lr.astype(jnp.int32)
            col_ref[pl.ds(kk, 1), cols] = (idxs[kk][:, cols] * float(WIN_ROWS) + sl + lr).astype(jnp.int32)
    carry[...] = carry[...] + jnp.sum(onehot, axis=1, keepdims=True)
    cnt_ref[...] = carry[...].astype(jnp.int32)


def _router(logits_t):
    n = logits_t.shape[1]
    tr = ROUTE_BLOCK
    assert n % tr == 0
    tri = jnp.asarray(np.triu(np.ones((tr, tr), np.float32), k=1), BF16)
    low = jnp.asarray(np.tril(np.ones((N_EXPERTS, N_EXPERTS), np.float32), k=-1), BF16)
    per_tok = pl.BlockSpec((TOP_K, tr), lambda i: (0, i))
    return pl.pallas_call(
        _router_kernel,
        grid=(n // tr,),
        in_specs=[pl.BlockSpec((N_EXPERTS, tr), lambda i: (0, i)),
                  pl.BlockSpec((tr, tr), lambda i: (0, 0)),
                  pl.BlockSpec((N_EXPERTS, N_EXPERTS), lambda i: (0, 0))],
        out_specs=[per_tok, per_tok, per_tok, per_tok, per_tok, per_tok,
                   pl.BlockSpec((tr // COMBINE_TILE, N_EXPERTS, 1), lambda i: (i, 0, 0)),
                   pl.BlockSpec((N_EXPERTS, 1), lambda i: (0, 0))],
        out_shape=[jax.ShapeDtypeStruct((TOP_K, n), jnp.int32),
                   jax.ShapeDtypeStruct((TOP_K, n), jnp.int32),
                   jax.ShapeDtypeStruct((TOP_K, n), F32),
                   jax.ShapeDtypeStruct((TOP_K, n), jnp.int32),
                   jax.ShapeDtypeStruct((TOP_K, n), jnp.int32),
                   jax.ShapeDtypeStruct((TOP_K, n), jnp.int32),
                   jax.ShapeDtypeStruct((n // COMBINE_TILE, N_EXPERTS, 1), jnp.int32),
                   jax.ShapeDtypeStruct((N_EXPERTS, 1), jnp.int32)],
        scratch_shapes=[pltpu.VMEM((N_EXPERTS, 1), F32)],
        compiler_params=pltpu.CompilerParams(dimension_semantics=("arbitrary",)),
        name="router",
    )(logits_t, tri, low)


def _dispatch_kernel(lstart_ref, cnt_ref, dst_ref, h2_ref, lpos_ref, xs_hbm, stg, sem):
    i = pl.program_id(0)
    n_tiles = pl.num_programs(0)
    dt = h2_ref.shape[0]
    rows = dt * TOP_K
    slot = i % 2

    def drain(s):
        pltpu.make_async_copy(stg.at[s], xs_hbm.at[pl.ds(0, rows)], sem.at[s]).wait()

    @pl.when(i >= 2)
    def _():
        drain(slot)

    rid = lax.broadcasted_iota(jnp.int32, (rows, dt), 0)
    hit = rid == lpos_ref[0:1, :]
    for kk in range(1, TOP_K):
        hit = jnp.logical_or(hit, rid == lpos_ref[kk:kk + 1, :])
    perm = jnp.where(hit, 1.0, 0.0).astype(BF16)
    srt = jnp.dot(perm, h2_ref[...], preferred_element_type=F32)
    packed = _pack_bf16_pairs(srt)
    for c in range(PACK_CHUNKS):
        stg[slot, :, c, :] = packed[:, c * LANES:(c + 1) * LANES]

    def copy_pieces(n, src0, dst0, off, pieces):
        for piece in pieces:
            take = (n & piece) != 0

            @pl.when(take)
            def _(off=off, piece=piece):
                pltpu.make_async_copy(stg.at[slot, pl.ds(src0 + off, piece)],
                                      xs_hbm.at[pl.ds(dst0 + off, piece)], sem.at[slot]).start()

            off = off + jnp.where(take, piece, 0)

    small = [p for p in (16, 8, 4, 2, 1) if p <= dt]
    large = [p for p in (128, 64, 32) if p <= dt]
    assert dt <= 128
    for e in range(N_EXPERTS):
        n = cnt_ref[i * N_EXPERTS + e]
        src0 = lstart_ref[i * N_EXPERTS + e]
        dst0 = dst_ref[i * N_EXPERTS + e]
        n_large = n & ~jnp.int32(31)

        @pl.when(n_large != 0)
        def _(n=n, src0=src0, dst0=dst0):
            copy_pieces(n, src0, dst0, jnp.int32(0), large)

        copy_pieces(n, src0, dst0, n_large, small)

    @pl.when(i == n_tiles - 1)
    def _():
        drain(slot)

        @pl.when(n_tiles >= 2)
        def _():
            drain(1 - slot)


def _dispatch(h2, lpos, lstart, cnt, dst, cap):
    n_tok = h2.shape[0]
    dt = DISPATCH_TILE
    assert n_tok % dt == 0
    grid_spec = pltpu.PrefetchScalarGridSpec(
        num_scalar_prefetch=3,
        grid=(n_tok // dt,),
        in_specs=[pl.BlockSpec((dt, D_MODEL), lambda i, a, b, c: (i, 0)),
                  pl.BlockSpec((TOP_K, dt), lambda i, a, b, c: (0, i)),
                  ],
        out_specs=pl.BlockSpec(memory_space=pl.ANY),
        scratch_shapes=[pltpu.VMEM((2, dt * TOP_K, PACK_CHUNKS, LANES), jnp.uint32),
                        pltpu.SemaphoreType.DMA((2,))],
    )
    return pl.pallas_call(
        _dispatch_kernel,
        grid_spec=grid_spec,
        out_shape=jax.ShapeDtypeStruct((cap, PACK_CHUNKS, LANES), jnp.uint32),
        compiler_params=pltpu.CompilerParams(dimension_semantics=("arbitrary",),
                                             vmem_limit_bytes=VMEM_LIMIT),
        name="dispatch",
    )(lstart, cnt, dst, h2, lpos)


def _expert_kernel(n_xblocks, bexp_ref, nvalid_ref, epos_ref, elist_ref,
                   x_hbm, wgu_hbm, bgu_ref, wd_hbm, bd_ref, y_ref,
                   wgu_f32, wd_f32, wgu_bf, wd_bf, xbuf, xsem, wsem):
    i = pl.program_id(0)
    rb = y_ref.shape[0]
    nvalid = nvalid_ref[i]
    pos = epos_ref[i]
    fresh = jnp.logical_or(i == 0, pos != epos_ref[jnp.maximum(i - 1, 0)])
    slot = i % 2

    def x_copies(blk, s):
        return [pltpu.make_async_copy(x_hbm.at[pl.ds(blk * rb, rb), c, :],
                                      xbuf.at[s, :, pl.ds(c * LANES, LANES)], xsem.at[s])
                for c in range(PACK_CHUNKS)]

    def w_copies(p):
        e = elist_ref[p]
        s = p % 2
        return [pltpu.make_async_copy(wgu_hbm.at[e], wgu_f32.at[s], wsem.at[s, 0]),
                pltpu.make_async_copy(wd_hbm.at[e], wd_f32.at[s], wsem.at[s, 1])]

    @pl.when(i == 0)
    def _():
        for cp in x_copies(0, 0):
            cp.start()

        @pl.when(nvalid > 0)
        def _():
            for cp in w_copies(0):
                cp.start()

    @pl.when(i + 1 < n_xblocks)
    def _():
        for cp in x_copies(i + 1, 1 - slot):
            cp.start()

    @pl.when(jnp.logical_and(fresh, nvalid > 0))
    def _():
        @pl.when(elist_ref[pos + 1] >= 0)
        def _():
            for cp in w_copies(pos + 1):
                cp.start()

        for cp in w_copies(pos):
            cp.wait()
        ws = pos % 2
        chunk = 32

        def cast_gu(r, c):
            r0 = pl.multiple_of(r * chunk, chunk)
            wgu_bf[pl.ds(r0, chunk), :] = wgu_f32[ws, pl.ds(r0, chunk), :].astype(BF16)
            return c

        def cast_d(r, c):
            r0 = pl.multiple_of(r * chunk, chunk)
            wd_bf[pl.ds(r0, chunk), :] = wd_f32[ws, pl.ds(r0, chunk), :].astype(BF16)
            return c

        lax.fori_loop(0, D_MODEL // chunk, cast_gu, 0)
        lax.fori_loop(0, D_EXPERT // chunk, cast_d, 0)

    @pl.when(i < n_xblocks)
    def _():
        for cp in x_copies(i, slot):
            cp.wait()

    @pl.when(nvalid > 0)
    def _():
        xw = xbuf[slot]
        xw = jnp.where(lax.broadcasted_iota(jnp.int32, xw.shape, 0) < nvalid, xw, jnp.uint32(0))
        xlo, xhi = _unpack_bf16_pairs(xw)
        half = D_MODEL // 2
        g = (jnp.dot(xlo, wgu_bf[0:half, 0:D_EXPERT], preferred_element_type=F32)
             + jnp.dot(xhi, wgu_bf[half:, 0:D_EXPERT], preferred_element_type=F32)
             + bgu_ref[0, :, 0:D_EXPERT])
        u = (jnp.dot(xlo, wgu_bf[0:half, D_EXPERT:], preferred_element_type=F32)
             + jnp.dot(xhi, wgu_bf[half:, D_EXPERT:], preferred_element_type=F32)
             + bgu_ref[0, :, D_EXPERT:])
        g = jnp.minimum(g, SWIGLU_LIMIT)
        u = jnp.clip(u, -SWIGLU_LIMIT, SWIGLU_LIMIT)
        act = g * (1.0 / (1.0 + jnp.exp(-SWIGLU_ALPHA * g))) * (u + 1.0)
        y = jnp.dot(act.astype(BF16), wd_bf[...], preferred_element_type=F32) + bd_ref[0]
        row = lax.broadcasted_iota(jnp.int32, y.shape, 0)
        y_ref[...] = jnp.where(row < nvalid, y, 0.0).astype(BF16)

    @pl.when(nvalid == 0)
    def _():
        y_ref[...] = jnp.zeros_like(y_ref)


def _experts(xs, block_expert, nvalid, block_pos, expert_list, wgu, bgu, wd, bd):
    rb = EXPERT_ROWS
    n_xblocks = xs.shape[0] // rb
    nblk = n_xblocks + 1
    any_space = pl.BlockSpec(memory_space=pl.ANY)
    grid_spec = pltpu.PrefetchScalarGridSpec(
        num_scalar_prefetch=4,
        grid=(nblk,),
        in_specs=[
            any_space,
            any_space,
            pl.BlockSpec((1, 1, 2 * D_EXPERT), lambda i, be, nu, ep, el: (be[i], 0, 0)),
            any_space,
            pl.BlockSpec((1, 1, D_MODEL), lambda i, be, nu, ep, el: (be[i], 0, 0)),
        ],
        out_specs=pl.BlockSpec((rb, D_MODEL), lambda i, be, nu, ep, el: (i, 0)),
        scratch_shapes=[pltpu.VMEM((2, D_MODEL, 2 * D_EXPERT), F32),
                        pltpu.VMEM((2, D_EXPERT, D_MODEL), F32),
                        pltpu.VMEM((D_MODEL, 2 * D_EXPERT), BF16),
                        pltpu.VMEM((D_EXPERT, D_MODEL), BF16),
                        pltpu.VMEM((2, rb, D_MODEL // 2), jnp.uint32),
                        pltpu.SemaphoreType.DMA((2,)),
                        pltpu.SemaphoreType.DMA((2, 2))],
    )
    return pl.pallas_call(
        functools.partial(_expert_kernel, n_xblocks),
        grid_spec=grid_spec,
        out_shape=jax.ShapeDtypeStruct((nblk * rb, D_MODEL), BF16),
        compiler_params=pltpu.CompilerParams(dimension_semantics=("arbitrary",),
                                             vmem_limit_bytes=VMEM_LIMIT),
        name="experts",
    )(block_expert, nvalid, block_pos, expert_list, xs, wgu, bgu.reshape(N_EXPERTS, 1, 2 * D_EXPERT), wd,
      bd.reshape(N_EXPERTS, 1, D_MODEL))


def _combine_kernel(n_prompt_tiles, offa_ref, nchunk_ref,
                    x1p_ref, x1s_ref, info_ref, info_next_ref, gfin_ref, yb_hbm,
                    outp_ref, outs_ref, ybuf, gbuf0, gbuf1, acc_ref, sem):
    i = pl.program_id(0)
    n_tiles = pl.num_programs(0)

    def window_copy(tile, chunk, e, slot):
        base = pl.multiple_of(offa_ref[tile * N_EXPERTS + e] + chunk * WIN, WIN_ALIGN)
        return pltpu.make_async_copy(yb_hbm.at[pl.ds(base, WIN_ROWS), :],
                                     ybuf.at[slot, pl.ds(e * WIN_ROWS, WIN_ROWS), :],
                                     sem.at[slot])

    def start_windows(tile, chunk, slot):
        for e in range(N_EXPERTS):
            window_copy(tile, chunk, e, slot).start()

    def wait_windows(slot):
        pltpu.make_async_copy(yb_hbm.at[pl.ds(0, N_EXPERTS * WIN_ROWS), :], ybuf.at[slot],
                              sem.at[slot]).wait()

    slot = i % 2

    @pl.when(i == 0)
    def _():
        start_windows(0, 0, 0)

    lane = lax.broadcasted_iota(jnp.int32, (COMBINE_TILE, N_EXPERTS * WIN_ROWS), 1)

    def gate_matrix(ref, chunk):
        g = jnp.zeros(lane.shape, F32)
        for kk in range(TOP_K):
            lr = ref[:, TOP_K + kk:TOP_K + kk + 1]
            in_chunk = jnp.logical_and(lr >= chunk * WIN, lr < chunk * WIN + WIN)
            colk = jnp.where(in_chunk, ref[:, kk:kk + 1] - chunk * WIN, -1.0).astype(jnp.int32)
            g = jnp.where(lane == colk, ref[:, 2 * TOP_K + kk:2 * TOP_K + kk + 1], g)
        hi = g.astype(BF16)
        lo = (g - hi.astype(F32)).astype(BF16)
        return jnp.concatenate([hi, lo], axis=0)

    def moe_rows(gm, buf):
        r = jnp.dot(gm, ybuf[buf], preferred_element_type=F32)
        return r[0:COMBINE_TILE] + r[COMBINE_TILE:]

    @pl.when(i == 0)
    def _():
        gbuf0[...] = gate_matrix(info_ref, 0)

    def main(s):
        start_windows(jnp.minimum(i + 1, n_tiles - 1), 0, 1 - s)
        wait_windows(s)
        g_cur, g_nxt = (gbuf0, gbuf1) if s == 0 else (gbuf1, gbuf0)
        acc_ref[...] = moe_rows(g_cur[...], s)
        g_nxt[...] = gate_matrix(info_next_ref, 0)

    for s in range(2):
        @pl.when(slot == s)
        def _(s=s):
            main(s)

    @pl.when(i == n_tiles - 1)
    def _():
        wait_windows(1 - slot)

    def extra_chunk(j, c):
        start_windows(i, j, 2)
        wait_windows(2)
        acc_ref[...] += moe_rows(gate_matrix(info_ref, j), 2)
        return c

    lax.fori_loop(1, nchunk_ref[i], extra_chunk, 0)

    @pl.when(i < n_prompt_tiles)
    def _():
        outp_ref[...] = _rms(x1p_ref[...] + acc_ref[...], gfin_ref[...])

    @pl.when(i >= n_prompt_tiles)
    def _():
        outs_ref[...] = _rms(x1s_ref[...] + acc_ref[...], gfin_ref[...])


def _combine(x1_p, x1_s, tok_info, gfin, yb, offa, nchunk):
    ct = COMBINE_TILE
    n_p, n_s = x1_p.shape[0] // ct, x1_s.shape[0] // ct
    assert x1_p.shape[0] % ct == 0 and x1_s.shape[0] % ct == 0 and n_s >= 1
    n_info = tok_info.shape[1]
    grid_spec = pltpu.PrefetchScalarGridSpec(
        num_scalar_prefetch=2,
        grid=(n_p + n_s,),
        in_specs=[
            pl.BlockSpec((ct, D_MODEL), lambda i, o, c: (jnp.minimum(i, n_p - 1), 0)),
            pl.BlockSpec((ct, D_MODEL), lambda i, o, c: (jnp.maximum(i - n_p, 0), 0)),
            pl.BlockSpec((ct, n_info), lambda i, o, c: (i, 0)),
            pl.BlockSpec((ct, n_info), lambda i, o, c: (jnp.minimum(i + 1, n_p + n_s - 1), 0)),
            pl.BlockSpec((1, D_MODEL), lambda i, o, c: (0, 0)),
            pl.BlockSpec(memory_space=pl.ANY),
        ],
        out_specs=[
            pl.BlockSpec((ct, D_MODEL), lambda i, o, c: (jnp.minimum(i, n_p - 1), 0)),
            pl.BlockSpec((ct, D_MODEL), lambda i, o, c: (jnp.maximum(i - n_p, 0), 0)),
        ],
        scratch_shapes=[pltpu.VMEM((3, N_EXPERTS * WIN_ROWS, D_MODEL), BF16),
                        pltpu.VMEM((2 * ct, N_EXPERTS * WIN_ROWS), BF16),
                        pltpu.VMEM((2 * ct, N_EXPERTS * WIN_ROWS), BF16),
                        pltpu.VMEM((ct, D_MODEL), F32),
                        pltpu.SemaphoreType.DMA((3,))],
    )
    return pl.pallas_call(
        functools.partial(_combine_kernel, n_p),
        grid_spec=grid_spec,
        out_shape=[jax.ShapeDtypeStruct(x1_p.shape, F32), jax.ShapeDtypeStruct(x1_s.shape, F32)],
        compiler_params=pltpu.CompilerParams(dimension_semantics=("arbitrary",),
                                             vmem_limit_bytes=VMEM_LIMIT),
        name="combine",
    )(offa, nchunk, x1_p, x1_s, tok_info, tok_info, gfin, yb)


def kernel(x_prompt, x_sample, cache_k, cache_v, state_pool, meta_tokens, norm_attn, w_in, attn_sinks,
           w_pool, pool_scale, w_out, norm_ffn, router_w, router_b, w_gate_up, b_gate_up, w_down, b_down,
           norm_final):
    assert w_in.shape[0] == 1, "single-layer trunk"
    bsz, seq, _ = x_prompt.shape
    assert bsz == 1
    nb = x_sample.shape[0]
    n_tok = seq + nb
    gattn = norm_attn[0].reshape(1, D_MODEL)
    gffn = norm_ffn[0].reshape(1, D_MODEL)
    win = w_in[0].astype(BF16)
    wpool = w_pool[0].astype(BF16)
    wout = w_out[0].astype(BF16)
    pscale = pool_scale[0].reshape(1, POOL_WIDTH)
    rw_t = router_w[0].T
    rw_hi = rw_t.astype(BF16)
    rwt = jnp.stack([rw_hi, (rw_t - rw_hi.astype(F32)).astype(BF16)])
    rb = router_b[0].reshape(N_EXPERTS, 1)
    sinks = attn_sinks[0]

    ck = cache_k[0].reshape(nb, N_META + WINDOW, KV_WIDTH)
    cv = cache_v[0].reshape(nb, N_META + WINDOW, KV_WIDTH)
    (x1_s, h2_s, lgt_s, knew, vnew, pnew) = _sample_mixer(
        x_sample[:, 0], ck, cv, state_pool[0], gattn, win, wpool, pscale, wout, gffn, rwt, rb, sinks)
    (x1_p, h2_all, lgt_all, kmeta, vmeta, ktail, vtail, ptail) = _prompt_mixer(
        x_prompt[0], meta_tokens, gattn, win, wpool, pscale, wout, gffn, rwt, rb, sinks, h2_s, lgt_s)

    _eidx, _rank, gates, col, lrank, lpos, tcar, counts = _router(lgt_all)
    counts = counts[:, 0]
    tcar = tcar[:, :, 0]
    rbk = EXPERT_ROWS
    eids = jnp.arange(N_EXPERTS, dtype=jnp.int32)
    earlier = eids[None, :] < eids[:, None]
    excl_sum = lambda a: jnp.sum(jnp.where(earlier, a[..., None, :], 0), axis=-1)
    padded = (counts + rbk - 1) // rbk * rbk
    pad_start = excl_sum(padded).astype(jnp.int32)
    pad_end = pad_start + padded
    nblk = -(-(n_tok * TOP_K) // rbk) + N_EXPERTS
    cap = nblk * rbk
    block_start = jnp.arange(nblk + 1, dtype=jnp.int32) * rbk
    owns = (pad_start[None, :] <= block_start[:, None]) & (block_start[:, None] < pad_end[None, :])
    nvalid = jnp.sum(jnp.where(owns, jnp.clip(counts[None, :] - (block_start[:, None] - pad_start[None, :]),
                                              0, rbk), 0), axis=1).astype(jnp.int32)
    has_rows = counts > 0
    last_e = jnp.max(jnp.where(has_rows, eids, 0))
    block_expert = jnp.where(jnp.any(owns, axis=1), jnp.sum(jnp.where(owns, eids[None, :], 0), axis=1),
                             last_e).astype(jnp.int32)

    run_len = jnp.concatenate([tcar[1:], counts[None, :]], axis=0) - tcar
    run_lstart = excl_sum(run_len)
    run_dst = pad_start[None, :] + tcar
    flat = lambda a: a.astype(jnp.int32).reshape(-1)
    xs = _dispatch(h2_all, lpos, flat(run_lstart), flat(run_len), flat(run_dst), cap)
    expert_pos = excl_sum(has_rows.astype(jnp.int32))
    at_pos = has_rows[None, :] & (expert_pos[None, :] == jnp.arange(N_EXPERTS + 1, dtype=jnp.int32)[:, None])
    expert_list = jnp.where(jnp.any(at_pos, axis=1), jnp.sum(jnp.where(at_pos, eids[None, :], 0), axis=1),
                            -1).astype(jnp.int32)
    block_pos = jnp.sum(jnp.where(block_expert[:, None] == eids[None, :], expert_pos[None, :], 0),
                        axis=1).astype(jnp.int32)
    yb = _experts(xs, block_expert, nvalid, block_pos, expert_list,
                  w_gate_up[0], b_gate_up[0], w_down[0], b_down[0])

    offa = (pad_start[None, :] + (tcar - tcar % WIN_ALIGN)).astype(jnp.int32).reshape(-1)
    nchunk = jnp.maximum(jnp.max((run_len + WIN - 1) // WIN, axis=1), 1).astype(jnp.int32)
    gfin = norm_final.reshape(1, D_MODEL)
    tok_info = jnp.concatenate([col.astype(F32), lrank.astype(F32), gates], axis=0).T
    y_prompt, y_sample = _combine(x1_p, x1_s, tok_info, gfin, yb, offa, nchunk)

    kv_shape = (1, 1, N_META + WINDOW, N_KV_HEADS, HEAD_DIM)
    new_k_p = jnp.concatenate([kmeta, ktail], axis=0).reshape(kv_shape)
    new_v_p = jnp.concatenate([vmeta, vtail], axis=0).reshape(kv_shape)
    new_pool_p = ptail[16 - POOL_STATE:].reshape(1, 1, POOL_STATE, POOL_WIDTH)
    new_k_s = jnp.concatenate([ck[:, :N_META], ck[:, N_META + 1:], knew[:, None]], axis=1).reshape(
        (1, nb, N_META + WINDOW, N_KV_HEADS, HEAD_DIM))
    new_v_s = jnp.concatenate([cv[:, :N_META], cv[:, N_META + 1:], vnew[:, None]], axis=1).reshape(
        (1, nb, N_META + WINDOW, N_KV_HEADS, HEAD_DIM))
    new_pool_s = jnp.concatenate([state_pool[0][:, 1:], pnew[:, None]], axis=1)[None]
    return (y_prompt[None], y_sample[:, None], new_k_p, new_v_p, new_pool_p, new_k_s, new_v_s, new_pool_s)
```

```python
import functools

import jax
import jax.numpy as jnp
import numpy as np
from jax import lax
from jax.experimental import pallas as pl
from jax.experimental.pallas import tpu as pltpu

F32 = jnp.float32
BF16 = jnp.bfloat16

D_MODEL = 1024
N_META = 16
N_HEADS = 8
HEAD_DIM = 64
N_KV_HEADS = 2
GQA_GROUP = N_HEADS // N_KV_HEADS
ATTN_WIDTH = N_HEADS * HEAD_DIM
KV_WIDTH = N_KV_HEADS * HEAD_DIM
WINDOW = 128
POOL_WIDTH = D_MODEL - ATTN_WIDTH
POOL_WINDOWS = (2, 4, 8, 16)
POOL_GROUP_DIM = POOL_WIDTH // len(POOL_WINDOWS)
POOL_STATE = max(POOL_WINDOWS) - 1
N_EXPERTS = 32
TOP_K = 4
D_EXPERT = D_MODEL
SWIGLU_ALPHA = 1.702
SWIGLU_LIMIT = 7.0
NORM_EPS = 1e-5
PAST_LEN = 16384

LANES = 128
QSUB = 64
KEYS_SUB = QSUB + WINDOW
META_PAD = 64
NKEY = META_PAD + KEYS_SUB
MASKED = -1e30
PROMPT_BLOCK = 512
ROUTE_BLOCK = 384
EXPERT_ROWS = 512
DISPATCH_TILE = 128
PACK_CHUNKS = D_MODEL // 2 // LANES
COMBINE_TILE = 128
WIN = 32
WIN_ALIGN = 16
WIN_ROWS = WIN + WIN_ALIGN
VMEM_LIMIT = 56 * 1024 * 1024


def _rms(x, g):
    return x * lax.rsqrt(jnp.mean(x * x, axis=-1, keepdims=True) + NORM_EPS) * g


def _router_logits(rwt_ref, h2, h2_hi):
    nt = (((1,), (1,)), ((), ()))
    h2_lo = (h2 - h2_hi.astype(F32)).astype(BF16)
    return (lax.dot_general(rwt_ref[0], h2_hi, nt, preferred_element_type=F32)
            + lax.dot_general(rwt_ref[0], h2_lo, nt, preferred_element_type=F32)
            + lax.dot_general(rwt_ref[1], h2_hi, nt, preferred_element_type=F32))


def _dup_halves(a):
    lane = lax.broadcasted_iota(jnp.int32, a.shape, 1)
    r = pltpu.roll(a, HEAD_DIM, axis=1)
    lo = lane < HEAD_DIM
    return jnp.where(lo, a, r), jnp.where(lo, r, a)


def _pool_means(pext_ref, n):
    outs = []
    for gi, w in enumerate(POOL_WINDOWS):
        xg = pext_ref[:, gi * POOL_GROUP_DIM:(gi + 1) * POOL_GROUP_DIM]
        s = xg
        sh = 1
        while sh < w:
            s = s + pltpu.roll(s, sh, axis=0)
            sh *= 2
        outs.append(s[16:] * (1.0 / w) - xg[16:])
    return outs


def _pack_bf16_pairs(h):
    m = h.shape[1] // 2
    lo = pltpu.bitcast(h[:, :m].astype(BF16).astype(F32), jnp.uint32)
    hi = pltpu.bitcast(h[:, m:].astype(BF16).astype(F32), jnp.uint32)
    return lax.shift_right_logical(lo, jnp.uint32(16)) | (hi & jnp.uint32(0xFFFF0000))


def _unpack_bf16_pairs(w):
    lo = pltpu.bitcast(lax.shift_left(w, jnp.uint32(16)), F32).astype(BF16)
    hi = pltpu.bitcast(w & jnp.uint32(0xFFFF0000), F32).astype(BF16)
    return lo, hi


def _prompt_kernel(x_ref, meta_ref, gattn_ref, win_ref, wpool_ref, pscale_ref, wout_ref, gffn_ref,
                   rwt_ref, rb_ref, sink_ref, tbl_ref, tail_h2_ref, tail_lgt_ref,
                   x1_ref, h2_ref, lgt_ref, kmeta_ref, vmeta_ref, ktail_ref, vtail_ref, ptail_ref,
                   k2buf, v2buf, km2, vm2, qbuf, obuf, pext):
    pid = pl.program_id(0)
    n_main = pl.num_programs(0) - 1
    refs = (x_ref, meta_ref, gattn_ref, win_ref, wpool_ref, pscale_ref, wout_ref, gffn_ref,
            rwt_ref, rb_ref, sink_ref, tbl_ref,
            x1_ref, h2_ref, lgt_ref, kmeta_ref, vmeta_ref, ktail_ref, vtail_ref, ptail_ref,
            k2buf, v2buf, km2, vm2, qbuf, obuf, pext)

    @pl.when(pid < n_main)
    def _():
        _prompt_block(*refs)

    @pl.when(pid == n_main)
    def _():
        h2_ref[0:tail_h2_ref.shape[0], :] = tail_h2_ref[...]
        lgt_ref[:, 0:tail_lgt_ref.shape[1]] = tail_lgt_ref[...]


def _prompt_block(x_ref, meta_ref, gattn_ref, win_ref, wpool_ref, pscale_ref, wout_ref, gffn_ref,
                  rwt_ref, rb_ref, sink_ref, tbl_ref,
                  x1_ref, h2_ref, lgt_ref, kmeta_ref, vmeta_ref, ktail_ref, vtail_ref, ptail_ref,
                  k2buf, v2buf, km2, vm2, qbuf, obuf, pext):
    tb = x_ref.shape[0]
    pid = pl.program_id(0)

    @pl.when(pid == 0)
    def _():
        hm = _rms(meta_ref[...], gattn_ref[...]).astype(BF16)
        km = jnp.dot(hm, win_ref[:, ATTN_WIDTH:ATTN_WIDTH + KV_WIDTH], preferred_element_type=F32)
        vm = jnp.dot(hm, win_ref[:, ATTN_WIDTH + KV_WIDTH:ATTN_WIDTH + 2 * KV_WIDTH],
                     preferred_element_type=F32)
        pm = jnp.dot(hm, win_ref[:, ATTN_WIDTH + 2 * KV_WIDTH:], preferred_element_type=F32)
        kmeta_ref[...] = km
        vmeta_ref[...] = vm
        zpad = jnp.zeros((META_PAD - N_META, LANES), F32)
        k0, k1 = _dup_halves(jnp.concatenate([km, zpad], axis=0))
        v0, v1 = _dup_halves(jnp.concatenate([vm, zpad], axis=0))
        km2[0] = k0.astype(BF16)
        km2[1] = k1.astype(BF16)
        vm2[0, :, 0:LANES] = v0.astype(BF16)
        vm2[1, :, 0:LANES] = v1.astype(BF16)
        vm2[:, :, LANES:] = jnp.ones((N_KV_HEADS, META_PAD, LANES), BF16)
        k2buf[:, 0:WINDOW, :] = jnp.zeros((2, WINDOW, LANES), BF16)
        v2buf[:, 0:WINDOW, 0:LANES] = jnp.zeros((2, WINDOW, LANES), BF16)
        v2buf[:, :, LANES:] = jnp.ones((N_KV_HEADS, WINDOW + tb, LANES), BF16)
        pext[0:16, :] = pm

    h = _rms(x_ref[...], gattn_ref[...]).astype(BF16)
    q = jnp.dot(h, win_ref[:, 0:ATTN_WIDTH], preferred_element_type=F32) * (HEAD_DIM ** -0.5)
    lane_t = lax.broadcasted_iota(jnp.int32, (tb, LANES), 1)
    for c in range(N_HEADS // 2):
        tile = q[:, c * LANES:(c + 1) * LANES]
        for a in range(2):
            keep = (lane_t < HEAD_DIM) if a == 0 else (lane_t >= HEAD_DIM)
            piece = jnp.where(keep, tile, 0.0).astype(BF16).reshape(tb // QSUB, QSUB, LANES)
            row = ((c % 2) * 2 + a) * QSUB
            qbuf[c // 2, :, row:row + QSUB, :] = piece
    k = jnp.dot(h, win_ref[:, ATTN_WIDTH:ATTN_WIDTH + KV_WIDTH], preferred_element_type=F32)
    v = jnp.dot(h, win_ref[:, ATTN_WIDTH + KV_WIDTH:ATTN_WIDTH + 2 * KV_WIDTH], preferred_element_type=F32)
    p = jnp.dot(h, win_ref[:, ATTN_WIDTH + 2 * KV_WIDTH:], preferred_element_type=F32)
    ktail_ref[...] = k[tb - WINDOW:]
    vtail_ref[...] = v[tb - WINDOW:]
    ptail_ref[...] = p[tb - 16:]
    k0, k1 = _dup_halves(k)
    v0, v1 = _dup_halves(v)
    k2buf[0, WINDOW:, :] = k0.astype(BF16)
    k2buf[1, WINDOW:, :] = k1.astype(BF16)
    v2buf[0, WINDOW:, 0:LANES] = v0.astype(BF16)
    v2buf[1, WINDOW:, 0:LANES] = v1.astype(BF16)
    pext[16:, :] = p

    lane_q = lax.broadcasted_iota(jnp.int32, (QSUB, LANES), 1)
    lo_q = lane_q < HEAD_DIM

    for u in range(tb // QSUB):
        r0 = u * QSUB
        sel = jnp.where(pid == 0, u + 1, 0) if u < WINDOW // QSUB else 0
        for g in range(N_KV_HEADS):
            qm = qbuf[g, u]
            kwin = jnp.concatenate([km2[g], k2buf[g, r0:r0 + KEYS_SUB, :]], axis=0)
            vwin = jnp.concatenate([vm2[g], v2buf[g, r0:r0 + KEYS_SUB, :]], axis=0)
            s = lax.dot_general(qm, kwin, (((1,), (1,)), ((), ())), preferred_element_type=F32)
            s = s + tbl_ref[sel, g]
            sink = sink_ref[g]
            m = jnp.maximum(jnp.max(s, axis=1, keepdims=True), sink)
            e = jnp.exp(s - m).astype(BF16)
            r = jnp.dot(e, vwin, preferred_element_type=F32)
            o = r[:, 0:LANES] / (r[:, LANES:] + jnp.exp(sink - m))
            o0 = jnp.where(lo_q, o[0:QSUB], o[QSUB:2 * QSUB])
            o1 = jnp.where(lo_q, o[2 * QSUB:3 * QSUB], o[3 * QSUB:])
            obuf[r0:r0 + QSUB, (2 * g) * LANES:(2 * g + 1) * LANES] = o0.astype(BF16)
            obuf[r0:r0 + QSUB, (2 * g + 1) * LANES:(2 * g + 2) * LANES] = o1.astype(BF16)

    pooled = _pool_means(pext, tb)
    for gi in range(len(POOL_WINDOWS)):
        y = jnp.dot(pooled[gi].astype(BF16), wpool_ref[gi], preferred_element_type=F32)
        y = y * pscale_ref[:, gi * POOL_GROUP_DIM:(gi + 1) * POOL_GROUP_DIM]
        obuf[:, ATTN_WIDTH + gi * POOL_GROUP_DIM:ATTN_WIDTH + (gi + 1) * POOL_GROUP_DIM] = y.astype(BF16)

    k2buf[:, 0:WINDOW, :] = k2buf[:, tb:tb + WINDOW, :]
    v2buf[:, 0:WINDOW, 0:LANES] = v2buf[:, tb:tb + WINDOW, 0:LANES]
    pext[0:16, :] = pext[tb:tb + 16, :]

    x1 = x_ref[...] + jnp.dot(obuf[...], wout_ref[...], preferred_element_type=F32)
    x1_ref[...] = x1
    h2 = _rms(x1, gffn_ref[...])
    h2_hi = h2.astype(BF16)
    h2_ref[...] = h2_hi
    lgt_ref[...] = _router_logits(rwt_ref, h2, h2_hi) + rb_ref[...]


def _attn_tables(sinks):
    i = np.arange(QSUB)[:, None]
    j = np.arange(NKEY)[None, :]
    jb = j - META_PAD
    rel = i + WINDOW - jb
    band_ok = (jb >= 0) & (rel >= 0) & (rel <= WINDOW)
    meta_ok = (j < N_META) & (i >= 0)
    slopes = np.exp2(-8.0 * np.arange(1, N_HEADS + 1) / N_HEADS)
    tbl = np.empty((3, N_KV_HEADS, GQA_GROUP * QSUB, NKEY), np.float32)
    for var in range(3):
        ok = band_ok if var == 0 else band_ok & (jb >= WINDOW - (var - 1) * QSUB)
        for g in range(N_KV_HEADS):
            for a in range(GQA_GROUP):
                hd = g * GQA_GROUP + a
                bias = np.where(ok, -slopes[hd] * rel, MASKED)
                bias = np.where(meta_ok, 0.0, bias)
                tbl[var, g, a * QSUB:(a + 1) * QSUB] = bias
    sink_col = jnp.repeat(sinks.astype(F32).reshape(N_KV_HEADS, GQA_GROUP, 1), QSUB, axis=2)
    return jnp.asarray(tbl), sink_col.reshape(N_KV_HEADS, GQA_GROUP * QSUB, 1)


def _prompt_mixer(x, meta, gattn, win, wpool, pscale, wout, gffn, rwt, rb, sinks, tail_h2, tail_lgt):
    seq = x.shape[0]
    tb = PROMPT_BLOCK
    n_tail = tail_h2.shape[0]
    assert seq % tb == 0 and tb % WINDOW == 0 and n_tail <= tb
    nblk = seq // tb
    n_tok = seq + n_tail
    tbl, sink_col = _attn_tables(sinks)
    full = lambda *shape: pl.BlockSpec(shape, lambda i: (0,) * len(shape))
    main = lambda i: (jnp.minimum(i, nblk - 1), 0)
    in_width = win.shape[1]
    return pl.pallas_call(
        _prompt_kernel,
        grid=(nblk + 1,),
        in_specs=[
            pl.BlockSpec((tb, D_MODEL), main),
            full(N_META, D_MODEL), full(1, D_MODEL), full(D_MODEL, in_width),
            full(len(POOL_WINDOWS), POOL_GROUP_DIM, POOL_GROUP_DIM), full(1, POOL_WIDTH),
            full(D_MODEL, D_MODEL), full(1, D_MODEL), full(2, N_EXPERTS, D_MODEL), full(N_EXPERTS, 1),
            full(N_KV_HEADS, GQA_GROUP * QSUB, 1), full(3, N_KV_HEADS, GQA_GROUP * QSUB, NKEY),
            full(n_tail, D_MODEL), full(N_EXPERTS, n_tail),
        ],
        out_specs=[
            pl.BlockSpec((tb, D_MODEL), main),
            pl.BlockSpec((tb, D_MODEL), lambda i: (i, 0)),
            pl.BlockSpec((N_EXPERTS, tb), lambda i: (0, i)),
            full(N_META, KV_WIDTH), full(N_META, KV_WIDTH),
            full(WINDOW, KV_WIDTH), full(WINDOW, KV_WIDTH), full(16, POOL_WIDTH),
        ],
        out_shape=[
            jax.ShapeDtypeStruct((seq, D_MODEL), F32),
            jax.ShapeDtypeStruct((n_tok, D_MODEL), BF16),
            jax.ShapeDtypeStruct((N_EXPERTS, n_tok), F32),
            jax.ShapeDtypeStruct((N_META, KV_WIDTH), F32),
            jax.ShapeDtypeStruct((N_META, KV_WIDTH), F32),
            jax.ShapeDtypeStruct((WINDOW, KV_WIDTH), F32),
            jax.ShapeDtypeStruct((WINDOW, KV_WIDTH), F32),
            jax.ShapeDtypeStruct((16, POOL_WIDTH), F32),
        ],
        scratch_shapes=[
            pltpu.VMEM((N_KV_HEADS, WINDOW + tb, LANES), BF16),
            pltpu.VMEM((N_KV_HEADS, WINDOW + tb, 2 * LANES), BF16),
            pltpu.VMEM((N_KV_HEADS, META_PAD, LANES), BF16),
            pltpu.VMEM((N_KV_HEADS, META_PAD, 2 * LANES), BF16),
            pltpu.VMEM((N_KV_HEADS, tb // QSUB, GQA_GROUP * QSUB, LANES), BF16),
            pltpu.VMEM((tb, D_MODEL), BF16),
            pltpu.VMEM((16 + tb, POOL_WIDTH), F32),
        ],
        compiler_params=pltpu.CompilerParams(dimension_semantics=("arbitrary",),
                                             vmem_limit_bytes=VMEM_LIMIT),
        name="prompt_mixer",
    )(x, meta, gattn, win, wpool, pscale, wout, gffn, rwt, rb, sink_col, tbl, tail_h2, tail_lgt)


def _sample_kernel(x_ref, ck_ref, cv_ref, sp_ref, gattn_ref, win_ref, wpool_ref, pscale_ref, wout_ref,
                   gffn_ref, rwt_ref, rb_ref, sinkc_ref, bias_ref,
                   x1_ref, h2_ref, lgt_ref, knew_ref, vnew_ref, pnew_ref,
                   qm_buf, r_buf, obuf):
    nb = x_ref.shape[0]
    x = x_ref[...]
    h = _rms(x, gattn_ref[...]).astype(BF16)
    q = jnp.dot(h, win_ref[:, 0:ATTN_WIDTH], preferred_element_type=F32) * (HEAD_DIM ** -0.5)
    k = jnp.dot(h, win_ref[:, ATTN_WIDTH:ATTN_WIDTH + KV_WIDTH], preferred_element_type=F32)
    v = jnp.dot(h, win_ref[:, ATTN_WIDTH + KV_WIDTH:ATTN_WIDTH + 2 * KV_WIDTH], preferred_element_type=F32)
    p = jnp.dot(h, win_ref[:, ATTN_WIDTH + 2 * KV_WIDTH:], preferred_element_type=F32)
    knew_ref[...] = k
    vnew_ref[...] = v
    pnew_ref[...] = p

    lane = lax.broadcasted_iota(jnp.int32, (nb, LANES), 1)
    lo = lane < HEAD_DIM
    for hd in range(N_HEADS):
        tile = q[:, (hd // 2) * LANES:(hd // 2 + 1) * LANES]
        if (hd % 2) != (hd // GQA_GROUP):
            tile = pltpu.roll(tile, HEAD_DIM, axis=1)
        keep_lo = (hd // GQA_GROUP) == 0
        qm_buf[:, hd, :] = jnp.where(lo if keep_lo else jnp.logical_not(lo), tile, 0.0)

    def per_batch(b):
        qm = qm_buf[b]
        kb = ck_ref[b].astype(BF16)
        vb = cv_ref[b].astype(BF16)
        s = lax.dot_general(qm.astype(BF16), kb, (((1,), (1,)), ((), ())), preferred_element_type=F32)
        s = s + bias_ref[...]
        kn = knew_ref[pl.ds(b, 1), :]
        vn = vnew_ref[pl.ds(b, 1), :]
        s_self = jnp.sum(qm * kn, axis=1, keepdims=True)
        sink = sinkc_ref[...]
        m = jnp.maximum(jnp.maximum(jnp.max(s, axis=1, keepdims=True), s_self), sink)
        e = jnp.exp(s - m)
        e_self = jnp.exp(s_self - m)
        den = jnp.sum(e, axis=1, keepdims=True) + e_self + jnp.exp(sink - m)
        r = jnp.dot(e.astype(BF16), vb, preferred_element_type=F32)
        r = r + e_self * vn
        r_buf[b] = r / den

    unroll = 8
    assert nb % unroll == 0

    def batch_group(gidx, carry):
        for j in range(unroll):
            per_batch(gidx * unroll + j)
        return carry

    lax.fori_loop(0, nb // unroll, batch_group, 0)

    for c in range(N_HEADS // 2):
        halves = []
        for a in range(2):
            hd = 2 * c + a
            t = r_buf[:, hd, :]
            if (hd // GQA_GROUP) != a:
                t = pltpu.roll(t, HEAD_DIM, axis=1)
            halves.append(t)
        obuf[:, c * LANES:(c + 1) * LANES] = jnp.where(lo, halves[0], halves[1]).astype(BF16)

    for gi, w in enumerate(POOL_WINDOWS):
        cols = slice(gi * POOL_GROUP_DIM, (gi + 1) * POOL_GROUP_DIM)
        pg = p[:, cols]
        acc = pg
        for d in range(1, w):
            acc = acc + sp_ref[:, POOL_STATE - d, cols]
        pooled = acc * (1.0 / w) - pg
        y = jnp.dot(pooled.astype(BF16), wpool_ref[gi], preferred_element_type=F32) * pscale_ref[:, cols]
        obuf[:, ATTN_WIDTH + gi * POOL_GROUP_DIM:ATTN_WIDTH + (gi + 1) * POOL_GROUP_DIM] = y.astype(BF16)

    x1 = x + jnp.dot(obuf[...], wout_ref[...], preferred_element_type=F32)
    x1_ref[...] = x1
    h2 = _rms(x1, gffn_ref[...])
    h2_hi = h2.astype(BF16)
    h2_ref[...] = h2_hi
    lgt_ref[...] = _router_logits(rwt_ref, h2, h2_hi) + rb_ref[...]


def _sample_mixer(x, ck, cv, sp, gattn, win, wpool, pscale, wout, gffn, rwt, rb, sinks):
    nb = x.shape[0]
    rows = ck.shape[1]
    slopes = np.exp2(-8.0 * np.arange(1, N_HEADS + 1) / N_HEADS)
    dist = np.concatenate([np.zeros(N_META), WINDOW - np.arange(WINDOW)])
    bias = jnp.asarray((-slopes[:, None] * dist[None, :]).astype(np.float32))
    vm = pl.BlockSpec(memory_space=pltpu.VMEM)
    return pl.pallas_call(
        _sample_kernel,
        in_specs=[vm] * 14,
        out_specs=[vm] * 6,
        out_shape=[
            jax.ShapeDtypeStruct((nb, D_MODEL), F32),
            jax.ShapeDtypeStruct((nb, D_MODEL), BF16),
            jax.ShapeDtypeStruct((N_EXPERTS, nb), F32),
            jax.ShapeDtypeStruct((nb, KV_WIDTH), F32),
            jax.ShapeDtypeStruct((nb, KV_WIDTH), F32),
            jax.ShapeDtypeStruct((nb, POOL_WIDTH), F32),
        ],
        scratch_shapes=[
            pltpu.VMEM((nb, N_HEADS, LANES), F32),
            pltpu.VMEM((nb, N_HEADS, LANES), F32),
            pltpu.VMEM((nb, D_MODEL), BF16),
        ],
        compiler_params=pltpu.CompilerParams(vmem_limit_bytes=VMEM_LIMIT),
        name="sample_mixer",
    )(x, ck, cv, sp, gattn, win, wpool, pscale, wout, gffn, rwt, rb,
      sinks.astype(F32).reshape(N_HEADS, 1), bias)


def _router_kernel(lg_ref, tri_ref, low_ref, eidx_ref, rank_ref, gate_ref, col_ref, lrank_ref, lpos_ref,
                   tcar_ref, cnt_ref, carry):
    tr = lg_ref.shape[1]

    @pl.when(pl.program_id(0) == 0)
    def _():
        carry[...] = jnp.zeros_like(carry)

    work = lg_ref[...]
    eio = lax.broadcasted_iota(jnp.int32, work.shape, 0).astype(F32)
    sels, vals, idxs = [], [], []
    for _k in range(TOP_K):
        mx = jnp.max(work, axis=0, keepdims=True)
        idx = jnp.min(jnp.where(work == mx, eio, float(N_EXPERTS)), axis=0, keepdims=True)
        sel = eio == idx
        sels.append(sel)
        vals.append(mx)
        idxs.append(idx)
        work = jnp.where(sel, -jnp.inf, work)
    exps = [jnp.exp(vk - vals[0]) for vk in vals]
    tot = exps[0] + exps[1] + exps[2] + exps[3]
    onehot = jnp.zeros(work.shape, F32)
    for sel in sels:
        onehot = onehot + sel.astype(F32)
    before = jnp.dot(onehot.astype(BF16), tri_ref[...], preferred_element_type=F32) + carry[...]
    for kk in range(TOP_K):
        eidx_ref[pl.ds(kk, 1), :] = idxs[kk].astype(jnp.int32)
        gate_ref[pl.ds(kk, 1), :] = exps[kk] / tot
        rk = jnp.sum(jnp.where(sels[kk], before, 0.0), axis=0, keepdims=True)
        rank_ref[pl.ds(kk, 1), :] = rk.astype(jnp.int32)
    for j in range(tr // COMBINE_TILE):
        cols = slice(j * COMBINE_TILE, (j + 1) * COMBINE_TILE)
        tc = before[:, j * COMBINE_TILE:j * COMBINE_TILE + 1]
        tcar_ref[j] = tc.astype(jnp.int32)
        slack = tc - WIN_ALIGN * jnp.floor(tc * (1.0 / WIN_ALIGN))
        local = before[:, cols] - tc
        tile_cnt = jnp.broadcast_to(jnp.sum(onehot[:, cols], axis=1, keepdims=True), local.shape)
        run_start = jnp.dot(low_ref[...], tile_cnt.astype(BF16), preferred_element_type=F32)
        for kk in range(TOP_K):
            selk = sels[kk][:, cols]
            lr = jnp.sum(jnp.where(selk, local, 0.0), axis=0, keepdims=True)
            sl = jnp.sum(jnp.where(selk, slack, 0.0), axis=0, keepdims=True)
            lp = jnp.sum(jnp.where(selk, run_start + local, 0.0), axis=0, keepdims=True)
            lpos_ref[pl.ds(kk, 1), cols] = lp.astype(jnp.int32)
            lrank_ref[pl.ds(kk, 1), cols] = lr.astype(jnp.int32)
            col_ref[pl.ds(kk, 1), cols] = (idxs[kk][:, cols] * float(WIN_ROWS) + sl + lr).astype(jnp.int32)
    carry[...] = carry[...] + jnp.sum(onehot, axis=1, keepdims=True)
    cnt_ref[...] = carry[...].astype(jnp.int32)


def _router(logits_t):
    n = logits_t.shape[1]
    tr = ROUTE_BLOCK
    assert n % tr == 0
    tri = jnp.asarray(np.triu(np.ones((tr, tr), np.float32), k=1), BF16)
    low = jnp.asarray(np.tril(np.ones((N_EXPERTS, N_EXPERTS), np.float32), k=-1), BF16)
    per_tok = pl.BlockSpec((TOP_K, tr), lambda i: (0, i))
    return pl.pallas_call(
        _router_kernel,
        grid=(n // tr,),
        in_specs=[pl.BlockSpec((N_EXPERTS, tr), lambda i: (0, i)),
                  pl.BlockSpec((tr, tr), lambda i: (0, 0)),
                  pl.BlockSpec((N_EXPERTS, N_EXPERTS), lambda i: (0, 0))],
        out_specs=[per_tok, per_tok, per_tok, per_tok, per_tok, per_tok,
                   pl.BlockSpec((tr // COMBINE_TILE, N_EXPERTS, 1), lambda i: (i, 0, 0)),
                   pl.BlockSpec((N_EXPERTS, 1), lambda i: (0, 0))],
        out_shape=[jax.ShapeDtypeStruct((TOP_K, n), jnp.int32),
                   jax.ShapeDtypeStruct((TOP_K, n), jnp.int32),
                   jax.ShapeDtypeStruct((TOP_K, n), F32),
                   jax.ShapeDtypeStruct((TOP_K, n), jnp.int32),
                   jax.ShapeDtypeStruct((TOP_K, n), jnp.int32),
                   jax.ShapeDtypeStruct((TOP_K, n), jnp.int32),
                   jax.ShapeDtypeStruct((n // COMBINE_TILE, N_EXPERTS, 1), jnp.int32),
                   jax.ShapeDtypeStruct((N_EXPERTS, 1), jnp.int32)],
        scratch_shapes=[pltpu.VMEM((N_EXPERTS, 1), F32)],
        compiler_params=pltpu.CompilerParams(dimension_semantics=("arbitrary",)),
        name="router",
    )(logits_t, tri, low)


def _dispatch_kernel(lstart_ref, cnt_ref, dst_ref, h2_ref, lpos_ref, xs_hbm, stg, sem):
    i = pl.program_id(0)
    n_tiles = pl.num_programs(0)
    dt = h2_ref.shape[0]
    rows = dt * TOP_K
    slot = i % 2

    def drain(s):
        pltpu.make_async_copy(stg.at[s], xs_hbm.at[pl.ds(0, rows)], sem.at[s]).wait()

    @pl.when(i >= 2)
    def _():
        drain(slot)

    rid = lax.broadcasted_iota(jnp.int32, (rows, dt), 0)
    hit = rid == lpos_ref[0:1, :]
    for kk in range(1, TOP_K):
        hit = jnp.logical_or(hit, rid == lpos_ref[kk:kk + 1, :])
    perm = jnp.where(hit, 1.0, 0.0).astype(BF16)
    srt = jnp.dot(perm, h2_ref[...], preferred_element_type=F32)
    packed = _pack_bf16_pairs(srt)
    for c in range(PACK_CHUNKS):
        stg[slot, :, c, :] = packed[:, c * LANES:(c + 1) * LANES]

    def copy_pieces(n, src0, dst0, off, pieces):
        for piece in pieces:
            take = (n & piece) != 0

            @pl.when(take)
            def _(off=off, piece=piece):
                pltpu.make_async_copy(stg.at[slot, pl.ds(src0 + off, piece)],
                                      xs_hbm.at[pl.ds(dst0 + off, piece)], sem.at[slot]).start()

            off = off + jnp.where(take, piece, 0)

    small = [p for p in (16, 8, 4, 2, 1) if p <= dt]
    large = [p for p in (128, 64, 32) if p <= dt]
    assert dt <= 128
    for e in range(N_EXPERTS):
        n = cnt_ref[i * N_EXPERTS + e]
        src0 = lstart_ref[i * N_EXPERTS + e]
        dst0 = dst_ref[i * N_EXPERTS + e]
        n_large = n & ~jnp.int32(31)

        @pl.when(n_large != 0)
        def _(n=n, src0=src0, dst0=dst0):
            copy_pieces(n, src0, dst0, jnp.int32(0), large)

        copy_pieces(n, src0, dst0, n_large, small)

    @pl.when(i == n_tiles - 1)
    def _():
        drain(slot)

        @pl.when(n_tiles >= 2)
        def _():
            drain(1 - slot)


def _dispatch(h2, lpos, lstart, cnt, dst, cap):
    n_tok = h2.shape[0]
    dt = DISPATCH_TILE
    assert n_tok % dt == 0
    grid_spec = pltpu.PrefetchScalarGridSpec(
        num_scalar_prefetch=3,
        grid=(n_tok // dt,),
        in_specs=[pl.BlockSpec((dt, D_MODEL), lambda i, a, b, c: (i, 0)),
                  pl.BlockSpec((TOP_K, dt), lambda i, a, b, c: (0, i)),
                  ],
        out_specs=pl.BlockSpec(memory_space=pl.ANY),
        scratch_shapes=[pltpu.VMEM((2, dt * TOP_K, PACK_CHUNKS, LANES), jnp.uint32),
                        pltpu.SemaphoreType.DMA((2,))],
    )
    return pl.pallas_call(
        _dispatch_kernel,
        grid_spec=grid_spec,
        out_shape=jax.ShapeDtypeStruct((cap, PACK_CHUNKS, LANES), jnp.uint32),
        compiler_params=pltpu.CompilerParams(dimension_semantics=("arbitrary",),
                                             vmem_limit_bytes=VMEM_LIMIT),
        name="dispatch",
    )(lstart, cnt, dst, h2, lpos)


def _expert_kernel(n_xblocks, bexp_ref, nvalid_ref, epos_ref, elist_ref,
                   x_hbm, wgu_hbm, bgu_ref, wd_hbm, bd_ref, y_ref,
                   wgu_f32, wd_f32, wgu_bf, wd_bf, xbuf, xsem, wsem):
    i = pl.program_id(0)
    rb = y_ref.shape[0]
    nvalid = nvalid_ref[i]
    pos = epos_ref[i]
    fresh = jnp.logical_or(i == 0, pos != epos_ref[jnp.maximum(i - 1, 0)])
    slot = i % 2

    def x_copies(blk, s):
        return [pltpu.make_async_copy(x_hbm.at[pl.ds(blk * rb, rb), c, :],
                                      xbuf.at[s, :, pl.ds(c * LANES, LANES)], xsem.at[s])
                for c in range(PACK_CHUNKS)]

    def w_copies(p):
        e = elist_ref[p]
        s = p % 2
        return [pltpu.make_async_copy(wgu_hbm.at[e], wgu_f32.at[s], wsem.at[s, 0]),
                pltpu.make_async_copy(wd_hbm.at[e], wd_f32.at[s], wsem.at[s, 1])]

    @pl.when(i == 0)
    def _():
        for cp in x_copies(0, 0):
            cp.start()

        @pl.when(nvalid > 0)
        def _():
            for cp in w_copies(0):
                cp.start()

    @pl.when(i + 1 < n_xblocks)
    def _():
        for cp in x_copies(i + 1, 1 - slot):
            cp.start()

    @pl.when(jnp.logical_and(fresh, nvalid > 0))
    def _():
        @pl.when(elist_ref[pos + 1] >= 0)
        def _():
            for cp in w_copies(pos + 1):
                cp.start()

        for cp in w_copies(pos):
            cp.wait()
        ws = pos % 2
        chunk = 32

        def cast_gu(r, c):
            r0 = pl.multiple_of(r * chunk, chunk)
            wgu_bf[pl.ds(r0, chunk), :] = wgu_f32[ws, pl.ds(r0, chunk), :].astype(BF16)
            return c

        def cast_d(r, c):
            r0 = pl.multiple_of(r * chunk, chunk)
            wd_bf[pl.ds(r0, chunk), :] = wd_f32[ws, pl.ds(r0, chunk), :].astype(BF16)
            return c

        lax.fori_loop(0, D_MODEL // chunk, cast_gu, 0)
        lax.fori_loop(0, D_EXPERT // chunk, cast_d, 0)

    @pl.when(i < n_xblocks)
    def _():
        for cp in x_copies(i, slot):
            cp.wait()

    def ffn(rows):
        xw = xbuf[slot, 0:rows, :]
        xw = jnp.where(lax.broadcasted_iota(jnp.int32, xw.shape, 0) < nvalid, xw, jnp.uint32(0))
        xlo, xhi = _unpack_bf16_pairs(xw)
        half = D_MODEL // 2
        g = (jnp.dot(xlo, wgu_bf[0:half, 0:D_EXPERT], preferred_element_type=F32)
             + jnp.dot(xhi, wgu_bf[half:, 0:D_EXPERT], preferred_element_type=F32)
             + bgu_ref[0, :, 0:D_EXPERT])
        u = (jnp.dot(xlo, wgu_bf[0:half, D_EXPERT:], preferred_element_type=F32)
             + jnp.dot(xhi, wgu_bf[half:, D_EXPERT:], preferred_element_type=F32)
             + bgu_ref[0, :, D_EXPERT:])
        g = jnp.minimum(g, SWIGLU_LIMIT)
        u = jnp.clip(u, -SWIGLU_LIMIT, SWIGLU_LIMIT)
        act = g * (1.0 / (1.0 + jnp.exp(-SWIGLU_ALPHA * g))) * (u + 1.0)
        y = jnp.dot(act.astype(BF16), wd_bf[...], preferred_element_type=F32) + bd_ref[0]
        row = lax.broadcasted_iota(jnp.int32, y.shape, 0)
        y_ref[0:rows, :] = jnp.where(row < nvalid, y, 0.0).astype(BF16)
        if rows < rb:
            y_ref[rows:, :] = jnp.zeros((rb - rows, D_MODEL), BF16)

    @pl.when(nvalid > rb // 2)
    def _():
        ffn(rb)

    @pl.when(jnp.logical_and(nvalid > 0, nvalid <= rb // 2))
    def _():
        ffn(rb // 2)

    @pl.when(nvalid == 0)
    def _():
        y_ref[...] = jnp.zeros_like(y_ref)


def _experts(xs, block_expert, nvalid, block_pos, expert_list, wgu, bgu, wd, bd):
    rb = EXPERT_ROWS
    n_xblocks = xs.shape[0] // rb
    nblk = n_xblocks + 1
    any_space = pl.BlockSpec(memory_space=pl.ANY)
    grid_spec = pltpu.PrefetchScalarGridSpec(
        num_scalar_prefetch=4,
        grid=(nblk,),
        in_specs=[
            any_space,
            any_space,
            pl.BlockSpec((1, 1, 2 * D_EXPERT), lambda i, be, nu, ep, el: (be[i], 0, 0)),
            any_space,
            pl.BlockSpec((1, 1, D_MODEL), lambda i, be, nu, ep, el: (be[i], 0, 0)),
        ],
        out_specs=pl.BlockSpec((rb, D_MODEL), lambda i, be, nu, ep, el: (i, 0)),
        scratch_shapes=[pltpu.VMEM((2, D_MODEL, 2 * D_EXPERT), F32),
                        pltpu.VMEM((2, D_EXPERT, D_MODEL), F32),
                        pltpu.VMEM((D_MODEL, 2 * D_EXPERT), BF16),
                        pltpu.VMEM((D_EXPERT, D_MODEL), BF16),
                        pltpu.VMEM((2, rb, D_MODEL // 2), jnp.uint32),
                        pltpu.SemaphoreType.DMA((2,)),
                        pltpu.SemaphoreType.DMA((2, 2))],
    )
    return pl.pallas_call(
        functools.partial(_expert_kernel, n_xblocks),
        grid_spec=grid_spec,
        out_shape=jax.ShapeDtypeStruct((nblk * rb, D_MODEL), BF16),
        compiler_params=pltpu.CompilerParams(dimension_semantics=("arbitrary",),
                                             vmem_limit_bytes=VMEM_LIMIT),
        name="experts",
    )(block_expert, nvalid, block_pos, expert_list, xs, wgu, bgu.reshape(N_EXPERTS, 1, 2 * D_EXPERT), wd,
      bd.reshape(N_EXPERTS, 1, D_MODEL))


def _combine_kernel(n_prompt_tiles, offa_ref, nchunk_ref,
                    x1p_ref, x1s_ref, info_ref, info_next_ref, gfin_ref, yb_hbm,
                    outp_ref, outs_ref, ybuf, gbuf0, gbuf1, acc_ref, sem):
    i = pl.program_id(0)
    n_tiles = pl.num_programs(0)

    def window_copy(tile, chunk, e, slot):
        base = pl.multiple_of(offa_ref[tile * N_EXPERTS + e] + chunk * WIN, WIN_ALIGN)
        return pltpu.make_async_copy(yb_hbm.at[pl.ds(base, WIN_ROWS), :],
                                     ybuf.at[slot, pl.ds(e * WIN_ROWS, WIN_ROWS), :],
                                     sem.at[slot])

    def start_windows(tile, chunk, slot):
        for e in range(N_EXPERTS):
            window_copy(tile, chunk, e, slot).start()

    def wait_windows(slot):
        pltpu.make_async_copy(yb_hbm.at[pl.ds(0, N_EXPERTS * WIN_ROWS), :], ybuf.at[slot],
                              sem.at[slot]).wait()

    slot = i % 2

    @pl.when(i == 0)
    def _():
        start_windows(0, 0, 0)

    lane = lax.broadcasted_iota(jnp.int32, (COMBINE_TILE, N_EXPERTS * WIN_ROWS), 1)

    def gate_matrix(ref, chunk):
        g = jnp.zeros(lane.shape, F32)
        for kk in range(TOP_K):
            lr = ref[:, TOP_K + kk:TOP_K + kk + 1]
            in_chunk = jnp.logical_and(lr >= chunk * WIN, lr < chunk * WIN + WIN)
            colk = jnp.where(in_chunk, ref[:, kk:kk + 1] - chunk * WIN, -1.0).astype(jnp.int32)
            g = jnp.where(lane == colk, ref[:, 2 * TOP_K + kk:2 * TOP_K + kk + 1], g)
        hi = g.astype(BF16)
        lo = (g - hi.astype(F32)).astype(BF16)
        return jnp.concatenate([hi, lo], axis=0)

    def moe_rows(gm, buf):
        r = jnp.dot(gm, ybuf[buf], preferred_element_type=F32)
        return r[0:COMBINE_TILE] + r[COMBINE_TILE:]

    @pl.when(i == 0)
    def _():
        gbuf0[...] = gate_matrix(info_ref, 0)

    def main(s):
        start_windows(jnp.minimum(i + 1, n_tiles - 1), 0, 1 - s)
        wait_windows(s)
        g_cur, g_nxt = (gbuf0, gbuf1) if s == 0 else (gbuf1, gbuf0)
        acc_ref[...] = moe_rows(g_cur[...], s)
        g_nxt[...] = gate_matrix(info_next_ref, 0)

    for s in range(2):
        @pl.when(slot == s)
        def _(s=s):
            main(s)

    @pl.when(i == n_tiles - 1)
    def _():
        wait_windows(1 - slot)

    def extra_chunk(j, c):
        start_windows(i, j, 2)
        wait_windows(2)
        acc_ref[...] += moe_rows(gate_matrix(info_ref, j), 2)
        return c

    lax.fori_loop(1, nchunk_ref[i], extra_chunk, 0)

    @pl.when(i < n_prompt_tiles)
    def _():
        outp_ref[...] = _rms(x1p_ref[...] + acc_ref[...], gfin_ref[...])

    @pl.when(i >= n_prompt_tiles)
    def _():
        outs_ref[...] = _rms(x1s_ref[...] + acc_ref[...], gfin_ref[...])


def _combine(x1_p, x1_s, tok_info, gfin, yb, offa, nchunk):
    ct = COMBINE_TILE
    n_p, n_s = x1_p.shape[0] // ct, x1_s.shape[0] // ct
    assert x1_p.shape[0] % ct == 0 and x1_s.shape[0] % ct == 0 and n_s >= 1
    n_info = tok_info.shape[1]
    grid_spec = pltpu.PrefetchScalarGridSpec(
        num_scalar_prefetch=2,
        grid=(n_p + n_s,),
        in_specs=[
            pl.BlockSpec((ct, D_MODEL), lambda i, o, c: (jnp.minimum(i, n_p - 1), 0)),
            pl.BlockSpec((ct, D_MODEL), lambda i, o, c: (jnp.maximum(i - n_p, 0), 0)),
            pl.BlockSpec((ct, n_info), lambda i, o, c: (i, 0)),
            pl.BlockSpec((ct, n_info), lambda i, o, c: (jnp.minimum(i + 1, n_p + n_s - 1), 0)),
            pl.BlockSpec((1, D_MODEL), lambda i, o, c: (0, 0)),
            pl.BlockSpec(memory_space=pl.ANY),
        ],
        out_specs=[
            pl.BlockSpec((ct, D_MODEL), lambda i, o, c: (jnp.minimum(i, n_p - 1), 0)),
            pl.BlockSpec((ct, D_MODEL), lambda i, o, c: (jnp.maximum(i - n_p, 0), 0)),
        ],
        scratch_shapes=[pltpu.VMEM((3, N_EXPERTS * WIN_ROWS, D_MODEL), BF16),
                        pltpu.VMEM((2 * ct, N_EXPERTS * WIN_ROWS), BF16),
                        pltpu.VMEM((2 * ct, N_EXPERTS * WIN_ROWS), BF16),
                        pltpu.VMEM((ct, D_MODEL), F32),
                        pltpu.SemaphoreType.DMA((3,))],
    )
    return pl.pallas_call(
        functools.partial(_combine_kernel, n_p),
        grid_spec=grid_spec,
        out_shape=[jax.ShapeDtypeStruct(x1_p.shape, F32), jax.ShapeDtypeStruct(x1_s.shape, F32)],
        compiler_params=pltpu.CompilerParams(dimension_semantics=("arbitrary",),
                                             vmem_limit_bytes=VMEM_LIMIT),
        name="combine",
    )(offa, nchunk, x1_p, x1_s, tok_info, tok_info, gfin, yb)


def kernel(x_prompt, x_sample, cache_k, cache_v, state_pool, meta_tokens, norm_attn, w_in, attn_sinks,
           w_pool, pool_scale, w_out, norm_ffn, router_w, router_b, w_gate_up, b_gate_up, w_down, b_down,
           norm_final):
    assert w_in.shape[0] == 1, "single-layer trunk"
    bsz, seq, _ = x_prompt.shape
    assert bsz == 1
    nb = x_sample.shape[0]
    n_tok = seq + nb
    gattn = norm_attn[0].reshape(1, D_MODEL)
    gffn = norm_ffn[0].reshape(1, D_MODEL)
    win = w_in[0].astype(BF16)
    wpool = w_pool[0].astype(BF16)
    wout = w_out[0].astype(BF16)
    pscale = pool_scale[0].reshape(1, POOL_WIDTH)
    rw_t = router_w[0].T
    rw_hi = rw_t.astype(BF16)
    rwt = jnp.stack([rw_hi, (rw_t - rw_hi.astype(F32)).astype(BF16)])
    rb = router_b[0].reshape(N_EXPERTS, 1)
    sinks = attn_sinks[0]

    ck = cache_k[0].reshape(nb, N_META + WINDOW, KV_WIDTH)
    cv = cache_v[0].reshape(nb, N_META + WINDOW, KV_WIDTH)
    (x1_s, h2_s, lgt_s, knew, vnew, pnew) = _sample_mixer(
        x_sample[:, 0], ck, cv, state_pool[0], gattn, win, wpool, pscale, wout, gffn, rwt, rb, sinks)
    (x1_p, h2_all, lgt_all, kmeta, vmeta, ktail, vtail, ptail) = _prompt_mixer(
        x_prompt[0], meta_tokens, gattn, win, wpool, pscale, wout, gffn, rwt, rb, sinks, h2_s, lgt_s)

    _eidx, _rank, gates, col, lrank, lpos, tcar, counts = _router(lgt_all)
    counts = counts[:, 0]
    tcar = tcar[:, :, 0]
    rbk = EXPERT_ROWS
    eids = jnp.arange(N_EXPERTS, dtype=jnp.int32)
    earlier = eids[None, :] < eids[:, None]
    excl_sum = lambda a: jnp.sum(jnp.where(earlier, a[..., None, :], 0), axis=-1)
    padded = (counts + rbk - 1) // rbk * rbk
    pad_start = excl_sum(padded).astype(jnp.int32)
    pad_end = pad_start + padded
    nblk = -(-(n_tok * TOP_K) // rbk) + N_EXPERTS
    cap = nblk * rbk
    block_start = jnp.arange(nblk + 1, dtype=jnp.int32) * rbk
    owns = (pad_start[None, :] <= block_start[:, None]) & (block_start[:, None] < pad_end[None, :])
    nvalid = jnp.sum(jnp.where(owns, jnp.clip(counts[None, :] - (block_start[:, None] - pad_start[None, :]),
                                              0, rbk), 0), axis=1).astype(jnp.int32)
    has_rows = counts > 0
    last_e = jnp.max(jnp.where(has_rows, eids, 0))
    block_expert = jnp.where(jnp.any(owns, axis=1), jnp.sum(jnp.where(owns, eids[None, :], 0), axis=1),
                             last_e).astype(jnp.int32)

    run_len = jnp.concatenate([tcar[1:], counts[None, :]], axis=0) - tcar
    run_lstart = excl_sum(run_len)
    run_dst = pad_start[None, :] + tcar
    flat = lambda a: a.astype(jnp.int32).reshape(-1)
    xs = _dispatch(h2_all, lpos, flat(run_lstart), flat(run_len), flat(run_dst), cap)
    expert_pos = excl_sum(has_rows.astype(jnp.int32))
    at_pos = has_rows[None, :] & (expert_pos[None, :] == jnp.arange(N_EXPERTS + 1, dtype=jnp.int32)[:, None])
    expert_list = jnp.where(jnp.any(at_pos, axis=1), jnp.sum(jnp.where(at_pos, eids[None, :], 0), axis=1),
                            -1).astype(jnp.int32)
    block_pos = jnp.sum(jnp.where(block_expert[:, None] == eids[None, :], expert_pos[None, :], 0),
                        axis=1).astype(jnp.int32)
    yb = _experts(xs, block_expert, nvalid, block_pos, expert_list,
                  w_gate_up[0], b_gate_up[0], w_down[0], b_down[0])

    offa = (pad_start[None, :] + (tcar - tcar % WIN_ALIGN)).astype(jnp.int32).reshape(-1)
    nchunk = jnp.maximum(jnp.max((run_len + WIN - 1) // WIN, axis=1), 1).astype(jnp.int32)
    gfin = norm_final.reshape(1, D_MODEL)
    tok_info = jnp.concatenate([col.astype(F32), lrank.astype(F32), gates], axis=0).T
    y_prompt, y_sample = _combine(x1_p, x1_s, tok_info, gfin, yb, offa, nchunk)

    kv_shape = (1, 1, N_META + WINDOW, N_KV_HEADS, HEAD_DIM)
    new_k_p = jnp.concatenate([kmeta, ktail], axis=0).reshape(kv_shape)
    new_v_p = jnp.concatenate([vmeta, vtail], axis=0).reshape(kv_shape)
    new_pool_p = ptail[16 - POOL_STATE:].reshape(1, 1, POOL_STATE, POOL_WIDTH)
    new_k_s = jnp.concatenate([ck[:, :N_META], ck[:, N_META + 1:], knew[:, None]], axis=1).reshape(
        (1, nb, N_META + WINDOW, N_KV_HEADS, HEAD_DIM))
    new_v_s = jnp.concatenate([cv[:, :N_META], cv[:, N_META + 1:], vnew[:, None]], axis=1).reshape(
        (1, nb, N_META + WINDOW, N_KV_HEADS, HEAD_DIM))
    new_pool_s = jnp.concatenate([state_pool[0][:, 1:], pnew[:, None]], axis=1)[None]
    return (y_prompt[None], y_sample[:, None], new_k_p, new_v_p, new_pool_p, new_k_s, new_v_s, new_pool_s)
```

```python
import functools

import jax
import jax.numpy as jnp
import numpy as np
from jax import lax
from jax.experimental import pallas as pl
from jax.experimental.pallas import tpu as pltpu

F32 = jnp.float32
BF16 = jnp.bfloat16

D_MODEL = 1024
N_META = 16
N_HEADS = 8
HEAD_DIM = 64
N_KV_HEADS = 2
GQA_GROUP = N_HEADS // N_KV_HEADS
ATTN_WIDTH = N_HEADS * HEAD_DIM
KV_WIDTH = N_KV_HEADS * HEAD_DIM
WINDOW = 128
POOL_WIDTH = D_MODEL - ATTN_WIDTH
POOL_WINDOWS = (2, 4, 8, 16)
POOL_GROUP_DIM = POOL_WIDTH // len(POOL_WINDOWS)
POOL_STATE = max(POOL_WINDOWS) - 1
N_EXPERTS = 32
TOP_K = 4
D_EXPERT = D_MODEL
SWIGLU_ALPHA = 1.702
SWIGLU_LIMIT = 7.0
NORM_EPS = 1e-5
PAST_LEN = 16384

LANES = 128
QSUB = 64
KEYS_SUB = QSUB + WINDOW
META_PAD = 64
NKEY = META_PAD + KEYS_SUB
MASKED = -1e30
PROMPT_BLOCK = 1024
ROUTE_BLOCK = 384
EXPERT_ROWS = 512
DISPATCH_TILE = 128
PACK_CHUNKS = D_MODEL // 2 // LANES
COMBINE_TILE = 128
WIN = 32
WIN_ALIGN = 16
WIN_ROWS = WIN + WIN_ALIGN
VMEM_LIMIT = 56 * 1024 * 1024


def _rms(x, g):
    return x * lax.rsqrt(jnp.mean(x * x, axis=-1, keepdims=True) + NORM_EPS) * g


def _router_logits(rwt_ref, h2, h2_hi):
    nt = (((1,), (1,)), ((), ()))
    h2_lo = (h2 - h2_hi.astype(F32)).astype(BF16)
    return (lax.dot_general(rwt_ref[0], h2_hi, nt, preferred_element_type=F32)
            + lax.dot_general(rwt_ref[0], h2_lo, nt, preferred_element_type=F32)
            + lax.dot_general(rwt_ref[1], h2_hi, nt, preferred_element_type=F32))


def _dup_halves(a):
    lane = lax.broadcasted_iota(jnp.int32, a.shape, 1)
    r = pltpu.roll(a, HEAD_DIM, axis=1)
    lo = lane < HEAD_DIM
    return jnp.where(lo, a, r), jnp.where(lo, r, a)


def _pool_means(pext_ref, n):
    outs = []
    for gi, w in enumerate(POOL_WINDOWS):
        xg = pext_ref[:, gi * POOL_GROUP_DIM:(gi + 1) * POOL_GROUP_DIM]
        s = xg
        sh = 1
        while sh < w:
            s = s + pltpu.roll(s, sh, axis=0)
            sh *= 2
        outs.append(s[16:] * (1.0 / w) - xg[16:])
    return outs


def _pack_bf16_pairs(h):
    m = h.shape[1] // 2
    lo = pltpu.bitcast(h[:, :m].astype(BF16).astype(F32), jnp.uint32)
    hi = pltpu.bitcast(h[:, m:].astype(BF16).astype(F32), jnp.uint32)
    return lax.shift_right_logical(lo, jnp.uint32(16)) | (hi & jnp.uint32(0xFFFF0000))


def _unpack_bf16_pairs(w):
    lo = pltpu.bitcast(lax.shift_left(w, jnp.uint32(16)), F32).astype(BF16)
    hi = pltpu.bitcast(w & jnp.uint32(0xFFFF0000), F32).astype(BF16)
    return lo, hi


def _prompt_kernel(x_ref, meta_ref, gattn_ref, win_ref, wpool_ref, pscale_ref, wout_ref, gffn_ref,
                   rwt_ref, rb_ref, sink_ref, tbl_ref, tail_h2_ref, tail_lgt_ref,
                   x1_ref, h2_ref, lgt_ref, kmeta_ref, vmeta_ref, ktail_ref, vtail_ref, ptail_ref,
                   k2buf, v2buf, km2, vm2, qbuf, obuf, pext):
    pid = pl.program_id(0)
    n_main = pl.num_programs(0) - 1
    refs = (x_ref, meta_ref, gattn_ref, win_ref, wpool_ref, pscale_ref, wout_ref, gffn_ref,
            rwt_ref, rb_ref, sink_ref, tbl_ref,
            x1_ref, h2_ref, lgt_ref, kmeta_ref, vmeta_ref, ktail_ref, vtail_ref, ptail_ref,
            k2buf, v2buf, km2, vm2, qbuf, obuf, pext)

    @pl.when(pid < n_main)
    def _():
        _prompt_block(*refs)

    @pl.when(pid == n_main)
    def _():
        h2_ref[0:tail_h2_ref.shape[0], :] = tail_h2_ref[...]
        lgt_ref[:, 0:tail_lgt_ref.shape[1]] = tail_lgt_ref[...]


def _prompt_block(x_ref, meta_ref, gattn_ref, win_ref, wpool_ref, pscale_ref, wout_ref, gffn_ref,
                  rwt_ref, rb_ref, sink_ref, tbl_ref,
                  x1_ref, h2_ref, lgt_ref, kmeta_ref, vmeta_ref, ktail_ref, vtail_ref, ptail_ref,
                  k2buf, v2buf, km2, vm2, qbuf, obuf, pext):
    tb = x_ref.shape[0]
    pid = pl.program_id(0)

    @pl.when(pid == 0)
    def _():
        hm = _rms(meta_ref[...], gattn_ref[...]).astype(BF16)
        km = jnp.dot(hm, win_ref[:, ATTN_WIDTH:ATTN_WIDTH + KV_WIDTH], preferred_element_type=F32)
        vm = jnp.dot(hm, win_ref[:, ATTN_WIDTH + KV_WIDTH:ATTN_WIDTH + 2 * KV_WIDTH],
                     preferred_element_type=F32)
        pm = jnp.dot(hm, win_ref[:, ATTN_WIDTH + 2 * KV_WIDTH:], preferred_element_type=F32)
        kmeta_ref[...] = km
        vmeta_ref[...] = vm
        zpad = jnp.zeros((META_PAD - N_META, LANES), F32)
        k0, k1 = _dup_halves(jnp.concatenate([km, zpad], axis=0))
        v0, v1 = _dup_halves(jnp.concatenate([vm, zpad], axis=0))
        km2[0] = k0.astype(BF16)
        km2[1] = k1.astype(BF16)
        vm2[0, :, 0:LANES] = v0.astype(BF16)
        vm2[1, :, 0:LANES] = v1.astype(BF16)
        vm2[:, :, LANES:] = jnp.ones((N_KV_HEADS, META_PAD, LANES), BF16)
        k2buf[:, 0:WINDOW, :] = jnp.zeros((2, WINDOW, LANES), BF16)
        v2buf[:, 0:WINDOW, 0:LANES] = jnp.zeros((2, WINDOW, LANES), BF16)
        v2buf[:, :, LANES:] = jnp.ones((N_KV_HEADS, WINDOW + tb, LANES), BF16)
        pext[0:16, :] = pm

    h = _rms(x_ref[...], gattn_ref[...]).astype(BF16)
    q = jnp.dot(h, win_ref[:, 0:ATTN_WIDTH], preferred_element_type=F32) * (HEAD_DIM ** -0.5)
    lane_t = lax.broadcasted_iota(jnp.int32, (tb, LANES), 1)
    for c in range(N_HEADS // 2):
        tile = q[:, c * LANES:(c + 1) * LANES]
        for a in range(2):
            keep = (lane_t < HEAD_DIM) if a == 0 else (lane_t >= HEAD_DIM)
            piece = jnp.where(keep, tile, 0.0).astype(BF16).reshape(tb // QSUB, QSUB, LANES)
            row = ((c % 2) * 2 + a) * QSUB
            qbuf[c // 2, :, row:row + QSUB, :] = piece
    k = jnp.dot(h, win_ref[:, ATTN_WIDTH:ATTN_WIDTH + KV_WIDTH], preferred_element_type=F32)
    v = jnp.dot(h, win_ref[:, ATTN_WIDTH + KV_WIDTH:ATTN_WIDTH + 2 * KV_WIDTH], preferred_element_type=F32)
    p = jnp.dot(h, win_ref[:, ATTN_WIDTH + 2 * KV_WIDTH:], preferred_element_type=F32)
    ktail_ref[...] = k[tb - WINDOW:]
    vtail_ref[...] = v[tb - WINDOW:]
    ptail_ref[...] = p[tb - 16:]
    k0, k1 = _dup_halves(k)
    v0, v1 = _dup_halves(v)
    k2buf[0, WINDOW:, :] = k0.astype(BF16)
    k2buf[1, WINDOW:, :] = k1.astype(BF16)
    v2buf[0, WINDOW:, 0:LANES] = v0.astype(BF16)
    v2buf[1, WINDOW:, 0:LANES] = v1.astype(BF16)
    pext[16:, :] = p

    lane_q = lax.broadcasted_iota(jnp.int32, (QSUB, LANES), 1)
    lo_q = lane_q < HEAD_DIM

    for u in range(tb // QSUB):
        r0 = u * QSUB
        sel = jnp.where(pid == 0, u + 1, 0) if u < WINDOW // QSUB else 0
        for g in range(N_KV_HEADS):
            qm = qbuf[g, u]
            kwin = jnp.concatenate([km2[g], k2buf[g, r0:r0 + KEYS_SUB, :]], axis=0)
            vwin = jnp.concatenate([vm2[g], v2buf[g, r0:r0 + KEYS_SUB, :]], axis=0)
            s = lax.dot_general(qm, kwin, (((1,), (1,)), ((), ())), preferred_element_type=F32)
            s = s + tbl_ref[sel, g]
            sink = sink_ref[g]
            m = jnp.maximum(jnp.max(s, axis=1, keepdims=True), sink)
            e = jnp.exp(s - m).astype(BF16)
            r = jnp.dot(e, vwin, preferred_element_type=F32)
            o = r[:, 0:LANES] / (r[:, LANES:] + jnp.exp(sink - m))
            o0 = jnp.where(lo_q, o[0:QSUB], o[QSUB:2 * QSUB])
            o1 = jnp.where(lo_q, o[2 * QSUB:3 * QSUB], o[3 * QSUB:])
            obuf[r0:r0 + QSUB, (2 * g) * LANES:(2 * g + 1) * LANES] = o0.astype(BF16)
            obuf[r0:r0 + QSUB, (2 * g + 1) * LANES:(2 * g + 2) * LANES] = o1.astype(BF16)

    pooled = _pool_means(pext, tb)
    for gi in range(len(POOL_WINDOWS)):
        y = jnp.dot(pooled[gi].astype(BF16), wpool_ref[gi], preferred_element_type=F32)
        y = y * pscale_ref[:, gi * POOL_GROUP_DIM:(gi + 1) * POOL_GROUP_DIM]
        obuf[:, ATTN_WIDTH + gi * POOL_GROUP_DIM:ATTN_WIDTH + (gi + 1) * POOL_GROUP_DIM] = y.astype(BF16)

    k2buf[:, 0:WINDOW, :] = k2buf[:, tb:tb + WINDOW, :]
    v2buf[:, 0:WINDOW, 0:LANES] = v2buf[:, tb:tb + WINDOW, 0:LANES]
    pext[0:16, :] = pext[tb:tb + 16, :]

    x1 = x_ref[...] + jnp.dot(obuf[...], wout_ref[...], preferred_element_type=F32)
    x1_ref[...] = x1
    h2 = _rms(x1, gffn_ref[...])
    h2_hi = h2.astype(BF16)
    h2_ref[...] = h2_hi
    lgt_ref[...] = _router_logits(rwt_ref, h2, h2_hi) + rb_ref[...]


def _attn_tables(sinks):
    i = np.arange(QSUB)[:, None]
    j = np.arange(NKEY)[None, :]
    jb = j - META_PAD
    rel = i + WINDOW - jb
    band_ok = (jb >= 0) & (rel >= 0) & (rel <= WINDOW)
    meta_ok = (j < N_META) & (i >= 0)
    slopes = np.exp2(-8.0 * np.arange(1, N_HEADS + 1) / N_HEADS)
    tbl = np.empty((3, N_KV_HEADS, GQA_GROUP * QSUB, NKEY), np.float32)
    for var in range(3):
        ok = band_ok if var == 0 else band_ok & (jb >= WINDOW - (var - 1) * QSUB)
        for g in range(N_KV_HEADS):
            for a in range(GQA_GROUP):
                hd = g * GQA_GROUP + a
                bias = np.where(ok, -slopes[hd] * rel, MASKED)
                bias = np.where(meta_ok, 0.0, bias)
                tbl[var, g, a * QSUB:(a + 1) * QSUB] = bias
    sink_col = jnp.repeat(sinks.astype(F32).reshape(N_KV_HEADS, GQA_GROUP, 1), QSUB, axis=2)
    return jnp.asarray(tbl), sink_col.reshape(N_KV_HEADS, GQA_GROUP * QSUB, 1)


def _prompt_mixer(x, meta, gattn, win, wpool, pscale, wout, gffn, rwt, rb, sinks, tail_h2, tail_lgt):
    seq = x.shape[0]
    tb = PROMPT_BLOCK
    n_tail = tail_h2.shape[0]
    assert seq % tb == 0 and tb % WINDOW == 0 and n_tail <= tb
    nblk = seq // tb
    n_tok = seq + n_tail
    tbl, sink_col = _attn_tables(sinks)
    full = lambda *shape: pl.BlockSpec(shape, lambda i: (0,) * len(shape))
    main = lambda i: (jnp.minimum(i, nblk - 1), 0)
    in_width = win.shape[1]
    return pl.pallas_call(
        _prompt_kernel,
        grid=(nblk + 1,),
        in_specs=[
            pl.BlockSpec((tb, D_MODEL), main),
            full(N_META, D_MODEL), full(1, D_MODEL), full(D_MODEL, in_width),
            full(len(POOL_WINDOWS), POOL_GROUP_DIM, POOL_GROUP_DIM), full(1, POOL_WIDTH),
            full(D_MODEL, D_MODEL), full(1, D_MODEL), full(2, N_EXPERTS, D_MODEL), full(N_EXPERTS, 1),
            full(N_KV_HEADS, GQA_GROUP * QSUB, 1), full(3, N_KV_HEADS, GQA_GROUP * QSUB, NKEY),
            full(n_tail, D_MODEL), full(N_EXPERTS, n_tail),
        ],
        out_specs=[
            pl.BlockSpec((tb, D_MODEL), main),
            pl.BlockSpec((tb, D_MODEL), lambda i: (i, 0)),
            pl.BlockSpec((N_EXPERTS, tb), lambda i: (0, i)),
            full(N_META, KV_WIDTH), full(N_META, KV_WIDTH),
            full(WINDOW, KV_WIDTH), full(WINDOW, KV_WIDTH), full(16, POOL_WIDTH),
        ],
        out_shape=[
            jax.ShapeDtypeStruct((seq, D_MODEL), F32),
            jax.ShapeDtypeStruct((n_tok, D_MODEL), BF16),
            jax.ShapeDtypeStruct((N_EXPERTS, n_tok), F32),
            jax.ShapeDtypeStruct((N_META, KV_WIDTH), F32),
            jax.ShapeDtypeStruct((N_META, KV_WIDTH), F32),
            jax.ShapeDtypeStruct((WINDOW, KV_WIDTH), F32),
            jax.ShapeDtypeStruct((WINDOW, KV_WIDTH), F32),
            jax.ShapeDtypeStruct((16, POOL_WIDTH), F32),
        ],
        scratch_shapes=[
            pltpu.VMEM((N_KV_HEADS, WINDOW + tb, LANES), BF16),
            pltpu.VMEM((N_KV_HEADS, WINDOW + tb, 2 * LANES), BF16),
            pltpu.VMEM((N_KV_HEADS, META_PAD, LANES), BF16),
            pltpu.VMEM((N_KV_HEADS, META_PAD, 2 * LANES), BF16),
            pltpu.VMEM((N_KV_HEADS, tb // QSUB, GQA_GROUP * QSUB, LANES), BF16),
            pltpu.VMEM((tb, D_MODEL), BF16),
            pltpu.VMEM((16 + tb, POOL_WIDTH), F32),
        ],
        compiler_params=pltpu.CompilerParams(dimension_semantics=("arbitrary",),
                                             vmem_limit_bytes=VMEM_LIMIT),
        name="prompt_mixer",
    )(x, meta, gattn, win, wpool, pscale, wout, gffn, rwt, rb, sink_col, tbl, tail_h2, tail_lgt)


def _sample_kernel(x_ref, ck_ref, cv_ref, sp_ref, gattn_ref, win_ref, wpool_ref, pscale_ref, wout_ref,
                   gffn_ref, rwt_ref, rb_ref, sinkc_ref, bias_ref,
                   x1_ref, h2_ref, lgt_ref, knew_ref, vnew_ref, pnew_ref,
                   qm_buf, r_buf, obuf):
    nb = x_ref.shape[0]
    x = x_ref[...]
    h = _rms(x, gattn_ref[...]).astype(BF16)
    q = jnp.dot(h, win_ref[:, 0:ATTN_WIDTH], preferred_element_type=F32) * (HEAD_DIM ** -0.5)
    k = jnp.dot(h, win_ref[:, ATTN_WIDTH:ATTN_WIDTH + KV_WIDTH], preferred_element_type=F32)
    v = jnp.dot(h, win_ref[:, ATTN_WIDTH + KV_WIDTH:ATTN_WIDTH + 2 * KV_WIDTH], preferred_element_type=F32)
    p = jnp.dot(h, win_ref[:, ATTN_WIDTH + 2 * KV_WIDTH:], preferred_element_type=F32)
    knew_ref[...] = k
    vnew_ref[...] = v
    pnew_ref[...] = p

    lane = lax.broadcasted_iota(jnp.int32, (nb, LANES), 1)
    lo = lane < HEAD_DIM
    for hd in range(N_HEADS):
        tile = q[:, (hd // 2) * LANES:(hd // 2 + 1) * LANES]
        if (hd % 2) != (hd // GQA_GROUP):
            tile = pltpu.roll(tile, HEAD_DIM, axis=1)
        keep_lo = (hd // GQA_GROUP) == 0
        qm_buf[:, hd, :] = jnp.where(lo if keep_lo else jnp.logical_not(lo), tile, 0.0)

    def per_batch(b):
        qm = qm_buf[b]
        kb = ck_ref[b].astype(BF16)
        vb = cv_ref[b].astype(BF16)
        s = lax.dot_general(qm.astype(BF16), kb, (((1,), (1,)), ((), ())), preferred_element_type=F32)
        s = s + bias_ref[...]
        kn = knew_ref[pl.ds(b, 1), :]
        vn = vnew_ref[pl.ds(b, 1), :]
        s_self = jnp.sum(qm * kn, axis=1, keepdims=True)
        sink = sinkc_ref[...]
        m = jnp.maximum(jnp.maximum(jnp.max(s, axis=1, keepdims=True), s_self), sink)
        e = jnp.exp(s - m)
        e_self = jnp.exp(s_self - m)
        den = jnp.sum(e, axis=1, keepdims=True) + e_self + jnp.exp(sink - m)
        r = jnp.dot(e.astype(BF16), vb, preferred_element_type=F32)
        r = r + e_self * vn
        r_buf[b] = r / den

    unroll = 8
    assert nb % unroll == 0

    def batch_group(gidx, carry):
        for j in range(unroll):
            per_batch(gidx * unroll + j)
        return carry

    lax.fori_loop(0, nb // unroll, batch_group, 0)

    for c in range(N_HEADS // 2):
        halves = []
        for a in range(2):
            hd = 2 * c + a
            t = r_buf[:, hd, :]
            if (hd // GQA_GROUP) != a:
                t = pltpu.roll(t, HEAD_DIM, axis=1)
            halves.append(t)
        obuf[:, c * LANES:(c + 1) * LANES] = jnp.where(lo, halves[0], halves[1]).astype(BF16)

    for gi, w in enumerate(POOL_WINDOWS):
        cols = slice(gi * POOL_GROUP_DIM, (gi + 1) * POOL_GROUP_DIM)
        pg = p[:, cols]
        acc = pg
        for d in range(1, w):
            acc = acc + sp_ref[:, POOL_STATE - d, cols]
        pooled = acc * (1.0 / w) - pg
        y = jnp.dot(pooled.astype(BF16), wpool_ref[gi], preferred_element_type=F32) * pscale_ref[:, cols]
        obuf[:, ATTN_WIDTH + gi * POOL_GROUP_DIM:ATTN_WIDTH + (gi + 1) * POOL_GROUP_DIM] = y.astype(BF16)

    x1 = x + jnp.dot(obuf[...], wout_ref[...], preferred_element_type=F32)
    x1_ref[...] = x1
    h2 = _rms(x1, gffn_ref[...])
    h2_hi = h2.astype(BF16)
    h2_ref[...] = h2_hi
    lgt_ref[...] = _router_logits(rwt_ref, h2, h2_hi) + rb_ref[...]


def _sample_mixer(x, ck, cv, sp, gattn, win, wpool, pscale, wout, gffn, rwt, rb, sinks):
    nb = x.shape[0]
    rows = ck.shape[1]
    slopes = np.exp2(-8.0 * np.arange(1, N_HEADS + 1) / N_HEADS)
    dist = np.concatenate([np.zeros(N_META), WINDOW - np.arange(WINDOW)])
    bias = jnp.asarray((-slopes[:, None] * dist[None, :]).astype(np.float32))
    vm = pl.BlockSpec(memory_space=pltpu.VMEM)
    return pl.pallas_call(
        _sample_kernel,
        in_specs=[vm] * 14,
        out_specs=[vm] * 6,
        out_shape=[
            jax.ShapeDtypeStruct((nb, D_MODEL), F32),
            jax.ShapeDtypeStruct((nb, D_MODEL), BF16),
            jax.ShapeDtypeStruct((N_EXPERTS, nb), F32),
            jax.ShapeDtypeStruct((nb, KV_WIDTH), F32),
            jax.ShapeDtypeStruct((nb, KV_WIDTH), F32),
            jax.ShapeDtypeStruct((nb, POOL_WIDTH), F32),
        ],
        scratch_shapes=[
            pltpu.VMEM((nb, N_HEADS, LANES), F32),
            pltpu.VMEM((nb, N_HEADS, LANES), F32),
            pltpu.VMEM((nb, D_MODEL), BF16),
        ],
        compiler_params=pltpu.CompilerParams(vmem_limit_bytes=VMEM_LIMIT),
        name="sample_mixer",
    )(x, ck, cv, sp, gattn, win, wpool, pscale, wout, gffn, rwt, rb,
      sinks.astype(F32).reshape(N_HEADS, 1), bias)


def _router_kernel(lg_ref, tri_ref, low_ref, eidx_ref, rank_ref, gate_ref, col_ref, lrank_ref, lpos_ref,
                   tcar_ref, cnt_ref, carry):
    tr = lg_ref.shape[1]

    @pl.when(pl.program_id(0) == 0)
    def _():
        carry[...] = jnp.zeros_like(carry)

    work = lg_ref[...]
    eio = lax.broadcasted_iota(jnp.int32, work.shape, 0).astype(F32)
    sels, vals, idxs = [], [], []
    for _k in range(TOP_K):
        mx = jnp.max(work, axis=0, keepdims=True)
        idx = jnp.min(jnp.where(work == mx, eio, float(N_EXPERTS)), axis=0, keepdims=True)
        sel = eio == idx
        sels.append(sel)
        vals.append(mx)
        idxs.append(idx)
        work = jnp.where(sel, -jnp.inf, work)
    exps = [jnp.exp(vk - vals[0]) for vk in vals]
    tot = exps[0] + exps[1] + exps[2] + exps[3]
    onehot = jnp.zeros(work.shape, F32)
    for sel in sels:
        onehot = onehot + sel.astype(F32)
    before = jnp.dot(onehot.astype(BF16), tri_ref[...], preferred_element_type=F32) + carry[...]
    for kk in range(TOP_K):
        eidx_ref[pl.ds(kk, 1), :] = idxs[kk].astype(jnp.int32)
        gate_ref[pl.ds(kk, 1), :] = exps[kk] / tot
        rk = jnp.sum(jnp.where(sels[kk], before, 0.0), axis=0, keepdims=True)
        rank_ref[pl.ds(kk, 1), :] = rk.astype(jnp.int32)
    for j in range(tr // COMBINE_TILE):
        cols = slice(j * COMBINE_TILE, (j + 1) * COMBINE_TILE)
        tc = before[:, j * COMBINE_TILE:j * COMBINE_TILE + 1]
        tcar_ref[j] = tc.astype(jnp.int32)
        slack = tc - WIN_ALIGN * jnp.floor(tc * (1.0 / WIN_ALIGN))
        local = before[:, cols] - tc
        tile_cnt = jnp.broadcast_to(jnp.sum(onehot[:, cols], axis=1, keepdims=True), local.shape)
        run_start = jnp.dot(low_ref[...], tile_cnt.astype(BF16), preferred_element_type=F32)
        for kk in range(TOP_K):
            selk = sels[kk][:, cols]
            lr = jnp.sum(jnp.where(selk, local, 0.0), axis=0, keepdims=True)
            sl = jnp.sum(jnp.where(selk, slack, 0.0), axis=0, keepdims=True)
            lp = jnp.sum(jnp.where(selk, run_start + local, 0.0), axis=0, keepdims=True)
            lpos_ref[pl.ds(kk, 1), cols] = lp.astype(jnp.int32)
            lrank_ref[pl.ds(kk, 1), cols] = lr.astype(jnp.int32)
            col_ref[pl.ds(kk, 1), cols] = (idxs[kk][:, cols] * float(WIN_ROWS) + sl + lr).astype(jnp.int32)
    carry[...] = carry[...] + jnp.sum(onehot, axis=1, keepdims=True)
    cnt_ref[...] = carry[...].astype(jnp.int32)


def _router(logits_t):
    n = logits_t.shape[1]
    tr = ROUTE_BLOCK
    assert n % tr == 0
    tri = jnp.asarray(np.triu(np.ones((tr, tr), np.float32), k=1), BF16)
    low = jnp.asarray(np.tril(np.ones((N_EXPERTS, N_EXPERTS), np.float32), k=-1), BF16)
    per_tok = pl.BlockSpec((TOP_K, tr), lambda i: (0, i))
    return pl.pallas_call(
        _router_kernel,
        grid=(n // tr,),
        in_specs=[pl.BlockSpec((N_EXPERTS, tr), lambda i: (0, i)),
                  pl.BlockSpec((tr, tr), lambda i: (0, 0)),
                  pl.BlockSpec((N_EXPERTS, N_EXPERTS), lambda i: (0, 0))],
        out_specs=[per_tok, per_tok, per_tok, per_tok, per_tok, per_tok,
                   pl.BlockSpec((tr // COMBINE_TILE, N_EXPERTS, 1), lambda i: (i, 0, 0)),
                   pl.BlockSpec((N_EXPERTS, 1), lambda i: (0, 0))],
        out_shape=[jax.ShapeDtypeStruct((TOP_K, n), jnp.int32),
                   jax.ShapeDtypeStruct((TOP_K, n), jnp.int32),
                   jax.ShapeDtypeStruct((TOP_K, n), F32),
                   jax.ShapeDtypeStruct((TOP_K, n), jnp.int32),
                   jax.ShapeDtypeStruct((TOP_K, n), jnp.int32),
                   jax.ShapeDtypeStruct((TOP_K, n), jnp.int32),
                   jax.ShapeDtypeStruct((n // COMBINE_TILE, N_EXPERTS, 1), jnp.int32),
                   jax.ShapeDtypeStruct((N_EXPERTS, 1), jnp.int32)],
        scratch_shapes=[pltpu.VMEM((N_EXPERTS, 1), F32)],
        compiler_params=pltpu.CompilerParams(dimension_semantics=("arbitrary",)),
        name="router",
    )(logits_t, tri, low)


def _dispatch_kernel(lstart_ref, cnt_ref, dst_ref, h2_ref, lpos_ref, xs_hbm, stg, sem):
    i = pl.program_id(0)
    n_tiles = pl.num_programs(0)
    dt = h2_ref.shape[0]
    rows = dt * TOP_K
    slot = i % 2

    def drain(s):
        pltpu.make_async_copy(stg.at[s], xs_hbm.at[pl.ds(0, rows)], sem.at[s]).wait()

    @pl.when(i >= 2)
    def _():
        drain(slot)

    rid = lax.broadcasted_iota(jnp.int32, (rows, dt), 0)
    hit = rid == lpos_ref[0:1, :]
    for kk in range(1, TOP_K):
        hit = jnp.logical_or(hit, rid == lpos_ref[kk:kk + 1, :])
    perm = jnp.where(hit, 1.0, 0.0).astype(BF16)
    srt = jnp.dot(perm, h2_ref[...], preferred_element_type=F32)
    packed = _pack_bf16_pairs(srt)
    for c in range(PACK_CHUNKS):
        stg[slot, :, c, :] = packed[:, c * LANES:(c + 1) * LANES]

    def copy_pieces(n, src0, dst0, off, pieces):
        for piece in pieces:
            take = (n & piece) != 0

            @pl.when(take)
            def _(off=off, piece=piece):
                pltpu.make_async_copy(stg.at[slot, pl.ds(src0 + off, piece)],
                                      xs_hbm.at[pl.ds(dst0 + off, piece)], sem.at[slot]).start()

            off = off + jnp.where(take, piece, 0)

    small = [p for p in (16, 8, 4, 2, 1) if p <= dt]
    large = [p for p in (128, 64, 32) if p <= dt]
    assert dt <= 128
    for e in range(N_EXPERTS):
        n = cnt_ref[i * N_EXPERTS + e]
        src0 = lstart_ref[i * N_EXPERTS + e]
        dst0 = dst_ref[i * N_EXPERTS + e]
        n_large = n & ~jnp.int32(31)

        @pl.when(n_large != 0)
        def _(n=n, src0=src0, dst0=dst0):
            copy_pieces(n, src0, dst0, jnp.int32(0), large)

        copy_pieces(n, src0, dst0, n_large, small)

    @pl.when(i == n_tiles - 1)
    def _():
        drain(slot)

        @pl.when(n_tiles >= 2)
        def _():
            drain(1 - slot)


def _dispatch(h2, lpos, lstart, cnt, dst, cap):
    n_tok = h2.shape[0]
    dt = DISPATCH_TILE
    assert n_tok % dt == 0
    grid_spec = pltpu.PrefetchScalarGridSpec(
        num_scalar_prefetch=3,
        grid=(n_tok // dt,),
        in_specs=[pl.BlockSpec((dt, D_MODEL), lambda i, a, b, c: (i, 0)),
                  pl.BlockSpec((TOP_K, dt), lambda i, a, b, c: (0, i)),
                  ],
        out_specs=pl.BlockSpec(memory_space=pl.ANY),
        scratch_shapes=[pltpu.VMEM((2, dt * TOP_K, PACK_CHUNKS, LANES), jnp.uint32),
                        pltpu.SemaphoreType.DMA((2,))],
    )
    return pl.pallas_call(
        _dispatch_kernel,
        grid_spec=grid_spec,
        out_shape=jax.ShapeDtypeStruct((cap, PACK_CHUNKS, LANES), jnp.uint32),
        compiler_params=pltpu.CompilerParams(dimension_semantics=("arbitrary",),
                                             vmem_limit_bytes=VMEM_LIMIT),
        name="dispatch",
    )(lstart, cnt, dst, h2, lpos)


def _expert_kernel(n_xblocks, bexp_ref, nvalid_ref, epos_ref, elist_ref,
                   x_hbm, wgu_hbm, bgu_ref, wd_hbm, bd_ref, y_ref,
                   wgu_f32, wd_f32, wgu_bf, wd_bf, xbuf, xsem, wsem):
    i = pl.program_id(0)
    rb = y_ref.shape[0]
    nvalid = nvalid_ref[i]
    pos = epos_ref[i]
    fresh = jnp.logical_or(i == 0, pos != epos_ref[jnp.maximum(i - 1, 0)])
    slot = i % 2

    def x_copies(blk, s):
        return [pltpu.make_async_copy(x_hbm.at[pl.ds(blk * rb, rb), c, :],
                                      xbuf.at[s, :, pl.ds(c * LANES, LANES)], xsem.at[s])
                for c in range(PACK_CHUNKS)]

    def w_copies(p):
        e = elist_ref[p]
        s = p % 2
        return [pltpu.make_async_copy(wgu_hbm.at[e], wgu_f32.at[s], wsem.at[s, 0]),
                pltpu.make_async_copy(wd_hbm.at[e], wd_f32.at[s], wsem.at[s, 1])]

    @pl.when(i == 0)
    def _():
        for cp in x_copies(0, 0):
            cp.start()

        @pl.when(nvalid > 0)
        def _():
            for cp in w_copies(0):
                cp.start()

    @pl.when(i + 1 < n_xblocks)
    def _():
        for cp in x_copies(i + 1, 1 - slot):
            cp.start()

    @pl.when(jnp.logical_and(fresh, nvalid > 0))
    def _():
        @pl.when(elist_ref[pos + 1] >= 0)
        def _():
            for cp in w_copies(pos + 1):
                cp.start()

        for cp in w_copies(pos):
            cp.wait()
        ws = pos % 2
        chunk = 32

        def cast_gu(r, c):
            r0 = pl.multiple_of(r * chunk, chunk)
            wgu_bf[pl.ds(r0, chunk), :] = wgu_f32[ws, pl.ds(r0, chunk), :].astype(BF16)
            return c

        def cast_d(r, c):
            r0 = pl.multiple_of(r * chunk, chunk)
            wd_bf[pl.ds(r0, chunk), :] = wd_f32[ws, pl.ds(r0, chunk), :].astype(BF16)
            return c

        lax.fori_loop(0, D_MODEL // chunk, cast_gu, 0)
        lax.fori_loop(0, D_EXPERT // chunk, cast_d, 0)

    @pl.when(i < n_xblocks)
    def _():
        for cp in x_copies(i, slot):
            cp.wait()

    def ffn(rows):
        xw = xbuf[slot, 0:rows, :]
        xw = jnp.where(lax.broadcasted_iota(jnp.int32, xw.shape, 0) < nvalid, xw, jnp.uint32(0))
        xlo, xhi = _unpack_bf16_pairs(xw)
        half = D_MODEL // 2
        g = (jnp.dot(xlo, wgu_bf[0:half, 0:D_EXPERT], preferred_element_type=F32)
             + jnp.dot(xhi, wgu_bf[half:, 0:D_EXPERT], preferred_element_type=F32)
             + bgu_ref[0, :, 0:D_EXPERT])
        u = (jnp.dot(xlo, wgu_bf[0:half, D_EXPERT:], preferred_element_type=F32)
             + jnp.dot(xhi, wgu_bf[half:, D_EXPERT:], preferred_element_type=F32)
             + bgu_ref[0, :, D_EXPERT:])
        g = jnp.minimum(g, SWIGLU_LIMIT)
        u = jnp.clip(u, -SWIGLU_LIMIT, SWIGLU_LIMIT)
        act = g * (1.0 / (1.0 + jnp.exp(-SWIGLU_ALPHA * g))) * (u + 1.0)
        y = jnp.dot(act.astype(BF16), wd_bf[...], preferred_element_type=F32) + bd_ref[0]
        row = lax.broadcasted_iota(jnp.int32, y.shape, 0)
        y_ref[0:rows, :] = jnp.where(row < nvalid, y, 0.0).astype(BF16)
        if rows < rb:
            y_ref[rows:, :] = jnp.zeros((rb - rows, D_MODEL), BF16)

    @pl.when(nvalid > rb // 2)
    def _():
        ffn(rb)

    @pl.when(jnp.logical_and(nvalid > 0, nvalid <= rb // 2))
    def _():
        ffn(rb // 2)

    @pl.when(nvalid == 0)
    def _():
        y_ref[...] = jnp.zeros_like(y_ref)


def _experts(xs, block_expert, nvalid, block_pos, expert_list, wgu, bgu, wd, bd):
    rb = EXPERT_ROWS
    n_xblocks = xs.shape[0] // rb
    nblk = n_xblocks + 1
    any_space = pl.BlockSpec(memory_space=pl.ANY)
    grid_spec = pltpu.PrefetchScalarGridSpec(
        num_scalar_prefetch=4,
        grid=(nblk,),
        in_specs=[
            any_space,
            any_space,
            pl.BlockSpec((1, 1, 2 * D_EXPERT), lambda i, be, nu, ep, el: (be[i], 0, 0)),
            any_space,
            pl.BlockSpec((1, 1, D_MODEL), lambda i, be, nu, ep, el: (be[i], 0, 0)),
        ],
        out_specs=pl.BlockSpec((rb, D_MODEL), lambda i, be, nu, ep, el: (i, 0)),
        scratch_shapes=[pltpu.VMEM((2, D_MODEL, 2 * D_EXPERT), F32),
                        pltpu.VMEM((2, D_EXPERT, D_MODEL), F32),
                        pltpu.VMEM((D_MODEL, 2 * D_EXPERT), BF16),
                        pltpu.VMEM((D_EXPERT, D_MODEL), BF16),
                        pltpu.VMEM((2, rb, D_MODEL // 2), jnp.uint32),
                        pltpu.SemaphoreType.DMA((2,)),
                        pltpu.SemaphoreType.DMA((2, 2))],
    )
    return pl.pallas_call(
        functools.partial(_expert_kernel, n_xblocks),
        grid_spec=grid_spec,
        out_shape=jax.ShapeDtypeStruct((nblk * rb, D_MODEL), BF16),
        compiler_params=pltpu.CompilerParams(dimension_semantics=("arbitrary",),
                                             vmem_limit_bytes=VMEM_LIMIT),
        name="experts",
    )(block_expert, nvalid, block_pos, expert_list, xs, wgu, bgu.reshape(N_EXPERTS, 1, 2 * D_EXPERT), wd,
      bd.reshape(N_EXPERTS, 1, D_MODEL))


def _combine_kernel(n_prompt_tiles, offa_ref, nchunk_ref,
                    x1p_ref, x1s_ref, info_ref, info_next_ref, gfin_ref, yb_hbm,
                    outp_ref, outs_ref, ybuf, gbuf0, gbuf1, acc_ref, sem):
    i = pl.program_id(0)
    n_tiles = pl.num_programs(0)

    def window_copy(tile, chunk, e, slot):
        base = pl.multiple_of(offa_ref[tile * N_EXPERTS + e] + chunk * WIN, WIN_ALIGN)
        return pltpu.make_async_copy(yb_hbm.at[pl.ds(base, WIN_ROWS), :],
                                     ybuf.at[slot, pl.ds(e * WIN_ROWS, WIN_ROWS), :],
                                     sem.at[slot])

    def start_windows(tile, chunk, slot):
        for e in range(N_EXPERTS):
            window_copy(tile, chunk, e, slot).start()

    def wait_windows(slot):
        pltpu.make_async_copy(yb_hbm.at[pl.ds(0, N_EXPERTS * WIN_ROWS), :], ybuf.at[slot],
                              sem.at[slot]).wait()

    slot = i % 2

    @pl.when(i == 0)
    def _():
        start_windows(0, 0, 0)

    lane = lax.broadcasted_iota(jnp.int32, (COMBINE_TILE, N_EXPERTS * WIN_ROWS), 1)

    def gate_matrix(ref, chunk):
        g = jnp.zeros(lane.shape, F32)
        for kk in range(TOP_K):
            lr = ref[:, TOP_K + kk:TOP_K + kk + 1]
            in_chunk = jnp.logical_and(lr >= chunk * WIN, lr < chunk * WIN + WIN)
            colk = jnp.where(in_chunk, ref[:, kk:kk + 1] - chunk * WIN, -1.0).astype(jnp.int32)
            g = jnp.where(lane == colk, ref[:, 2 * TOP_K + kk:2 * TOP_K + kk + 1], g)
        hi = g.astype(BF16)
        lo = (g - hi.astype(F32)).astype(BF16)
        return jnp.concatenate([hi, lo], axis=0)

    def moe_rows(gm, buf):
        r = jnp.dot(gm, ybuf[buf], preferred_element_type=F32)
        return r[0:COMBINE_TILE] + r[COMBINE_TILE:]

    @pl.when(i == 0)
    def _():
        gbuf0[...] = gate_matrix(info_ref, 0)

    def main(s):
        start_windows(jnp.minimum(i + 1, n_tiles - 1), 0, 1 - s)
        wait_windows(s)
        g_cur, g_nxt = (gbuf0, gbuf1) if s == 0 else (gbuf1, gbuf0)
        acc_ref[...] = moe_rows(g_cur[...], s)
        g_nxt[...] = gate_matrix(info_next_ref, 0)

    for s in range(2):
        @pl.when(slot == s)
        def _(s=s):
            main(s)

    @pl.when(i == n_tiles - 1)
    def _():
        wait_windows(1 - slot)

    def extra_chunk(j, c):
        start_windows(i, j, 2)
        wait_windows(2)
        acc_ref[...] += moe_rows(gate_matrix(info_ref, j), 2)
        return c

    lax.fori_loop(1, nchunk_ref[i], extra_chunk, 0)

    @pl.when(i < n_prompt_tiles)
    def _():
        outp_ref[...] = _rms(x1p_ref[...] + acc_ref[...], gfin_ref[...])

    @pl.when(i >= n_prompt_tiles)
    def _():
        outs_ref[...] = _rms(x1s_ref[...] + acc_ref[...], gfin_ref[...])


def _combine(x1_p, x1_s, tok_info, gfin, yb, offa, nchunk):
    ct = COMBINE_TILE
    n_p, n_s = x1_p.shape[0] // ct, x1_s.shape[0] // ct
    assert x1_p.shape[0] % ct == 0 and x1_s.shape[0] % ct == 0 and n_s >= 1
    n_info = tok_info.shape[1]
    grid_spec = pltpu.PrefetchScalarGridSpec(
        num_scalar_prefetch=2,
        grid=(n_p + n_s,),
        in_specs=[
            pl.BlockSpec((ct, D_MODEL), lambda i, o, c: (jnp.minimum(i, n_p - 1), 0)),
            pl.BlockSpec((ct, D_MODEL), lambda i, o, c: (jnp.maximum(i - n_p, 0), 0)),
            pl.BlockSpec((ct, n_info), lambda i, o, c: (i, 0)),
            pl.BlockSpec((ct, n_info), lambda i, o, c: (jnp.minimum(i + 1, n_p + n_s - 1), 0)),
            pl.BlockSpec((1, D_MODEL), lambda i, o, c: (0, 0)),
            pl.BlockSpec(memory_space=pl.ANY),
        ],
        out_specs=[
            pl.BlockSpec((ct, D_MODEL), lambda i, o, c: (jnp.minimum(i, n_p - 1), 0)),
            pl.BlockSpec((ct, D_MODEL), lambda i, o, c: (jnp.maximum(i - n_p, 0), 0)),
        ],
        scratch_shapes=[pltpu.VMEM((3, N_EXPERTS * WIN_ROWS, D_MODEL), BF16),
                        pltpu.VMEM((2 * ct, N_EXPERTS * WIN_ROWS), BF16),
                        pltpu.VMEM((2 * ct, N_EXPERTS * WIN_ROWS), BF16),
                        pltpu.VMEM((ct, D_MODEL), F32),
                        pltpu.SemaphoreType.DMA((3,))],
    )
    return pl.pallas_call(
        functools.partial(_combine_kernel, n_p),
        grid_spec=grid_spec,
        out_shape=[jax.ShapeDtypeStruct(x1_p.shape, F32), jax.ShapeDtypeStruct(x1_s.shape, F32)],
        compiler_params=pltpu.CompilerParams(dimension_semantics=("arbitrary",),
                                             vmem_limit_bytes=VMEM_LIMIT),
        name="combine",
    )(offa, nchunk, x1_p, x1_s, tok_info, tok_info, gfin, yb)


def kernel(x_prompt, x_sample, cache_k, cache_v, state_pool, meta_tokens, norm_attn, w_in, attn_sinks,
           w_pool, pool_scale, w_out, norm_ffn, router_w, router_b, w_gate_up, b_gate_up, w_down, b_down,
           norm_final):
    assert w_in.shape[0] == 1, "single-layer trunk"
    bsz, seq, _ = x_prompt.shape
    assert bsz == 1
    nb = x_sample.shape[0]
    n_tok = seq + nb
    gattn = norm_attn[0].reshape(1, D_MODEL)
    gffn = norm_ffn[0].reshape(1, D_MODEL)
    win = w_in[0].astype(BF16)
    wpool = w_pool[0].astype(BF16)
    wout = w_out[0].astype(BF16)
    pscale = pool_scale[0].reshape(1, POOL_WIDTH)
    rw_t = router_w[0].T
    rw_hi = rw_t.astype(BF16)
    rwt = jnp.stack([rw_hi, (rw_t - rw_hi.astype(F32)).astype(BF16)])
    rb = router_b[0].reshape(N_EXPERTS, 1)
    sinks = attn_sinks[0]

    ck = cache_k[0].reshape(nb, N_META + WINDOW, KV_WIDTH)
    cv = cache_v[0].reshape(nb, N_META + WINDOW, KV_WIDTH)
    (x1_s, h2_s, lgt_s, knew, vnew, pnew) = _sample_mixer(
        x_sample[:, 0], ck, cv, state_pool[0], gattn, win, wpool, pscale, wout, gffn, rwt, rb, sinks)
    (x1_p, h2_all, lgt_all, kmeta, vmeta, ktail, vtail, ptail) = _prompt_mixer(
        x_prompt[0], meta_tokens, gattn, win, wpool, pscale, wout, gffn, rwt, rb, sinks, h2_s, lgt_s)

    _eidx, _rank, gates, col, lrank, lpos, tcar, counts = _router(lgt_all)
    counts = counts[:, 0]
    tcar = tcar[:, :, 0]
    rbk = EXPERT_ROWS
    eids = jnp.arange(N_EXPERTS, dtype=jnp.int32)
    earlier = eids[None, :] < eids[:, None]
    excl_sum = lambda a: jnp.sum(jnp.where(earlier, a[..., None, :], 0), axis=-1)
    padded = (counts + rbk - 1) // rbk * rbk
    pad_start = excl_sum(padded).astype(jnp.int32)
    pad_end = pad_start + padded
    nblk = -(-(n_tok * TOP_K) // rbk) + N_EXPERTS
    cap = nblk * rbk
    block_start = jnp.arange(nblk + 1, dtype=jnp.int32) * rbk
    owns = (pad_start[None, :] <= block_start[:, None]) & (block_start[:, None] < pad_end[None, :])
    nvalid = jnp.sum(jnp.where(owns, jnp.clip(counts[None, :] - (block_start[:, None] - pad_start[None, :]),
                                              0, rbk), 0), axis=1).astype(jnp.int32)
    has_rows = counts > 0
    last_e = jnp.max(jnp.where(has_rows, eids, 0))
    block_expert = jnp.where(jnp.any(owns, axis=1), jnp.sum(jnp.where(owns, eids[None, :], 0), axis=1),
                             last_e).astype(jnp.int32)

    run_len = jnp.concatenate([tcar[1:], counts[None, :]], axis=0) - tcar
    run_lstart = excl_sum(run_len)
    run_dst = pad_start[None, :] + tcar
    flat = lambda a: a.astype(jnp.int32).reshape(-1)
    xs = _dispatch(h2_all, lpos, flat(run_lstart), flat(run_len), flat(run_dst), cap)
    expert_pos = excl_sum(has_rows.astype(jnp.int32))
    at_pos = has_rows[None, :] & (expert_pos[None, :] == jnp.arange(N_EXPERTS + 1, dtype=jnp.int32)[:, None])
    expert_list = jnp.where(jnp.any(at_pos, axis=1), jnp.sum(jnp.where(at_pos, eids[None, :], 0), axis=1),
                            -1).astype(jnp.int32)
    block_pos = jnp.sum(jnp.where(block_expert[:, None] == eids[None, :], expert_pos[None, :], 0),
                        axis=1).astype(jnp.int32)
    yb = _experts(xs, block_expert, nvalid, block_pos, expert_list,
                  w_gate_up[0], b_gate_up[0], w_down[0], b_down[0])

    offa = (pad_start[None, :] + (tcar - tcar % WIN_ALIGN)).astype(jnp.int32).reshape(-1)
    nchunk = jnp.maximum(jnp.max((run_len + WIN - 1) // WIN, axis=1), 1).astype(jnp.int32)
    gfin = norm_final.reshape(1, D_MODEL)
    tok_info = jnp.concatenate([col.astype(F32), lrank.astype(F32), gates], axis=0).T
    y_prompt, y_sample = _combine(x1_p, x1_s, tok_info, gfin, yb, offa, nchunk)

    kv_shape = (1, 1, N_META + WINDOW, N_KV_HEADS, HEAD_DIM)
    new_k_p = jnp.concatenate([kmeta, ktail], axis=0).reshape(kv_shape)
    new_v_p = jnp.concatenate([vmeta, vtail], axis=0).reshape(kv_shape)
    new_pool_p = ptail[16 - POOL_STATE:].reshape(1, 1, POOL_STATE, POOL_WIDTH)
    new_k_s = jnp.concatenate([ck[:, :N_META], ck[:, N_META + 1:], knew[:, None]], axis=1).reshape(
        (1, nb, N_META + WINDOW, N_KV_HEADS, HEAD_DIM))
    new_v_s = jnp.concatenate([cv[:, :N_META], cv[:, N_META + 1:], vnew[:, None]], axis=1).reshape(
        (1, nb, N_META + WINDOW, N_KV_HEADS, HEAD_DIM))
    new_pool_s = jnp.concatenate([state_pool[0][:, 1:], pnew[:, None]], axis=1)[None]
    return (y_prompt[None], y_sample[:, None], new_k_p, new_v_p, new_pool_p, new_k_s, new_v_s, new_pool_s)
```

```python
import functools

import jax
import jax.numpy as jnp
import numpy as np
from jax import lax
from jax.experimental import pallas as pl
from jax.experimental.pallas import tpu as pltpu

F32 = jnp.float32
BF16 = jnp.bfloat16

D_MODEL = 1024
N_META = 16
N_HEADS = 8
HEAD_DIM = 64
N_KV_HEADS = 2
GQA_GROUP = N_HEADS // N_KV_HEADS
ATTN_WIDTH = N_HEADS * HEAD_DIM
KV_WIDTH = N_KV_HEADS * HEAD_DIM
WINDOW = 128
POOL_WIDTH = D_MODEL - ATTN_WIDTH
POOL_WINDOWS = (2, 4, 8, 16)
POOL_GROUP_DIM = POOL_WIDTH // len(POOL_WINDOWS)
POOL_STATE = max(POOL_WINDOWS) - 1
N_EXPERTS = 32
TOP_K = 4
D_EXPERT = D_MODEL
SWIGLU_ALPHA = 1.702
SWIGLU_LIMIT = 7.0
NORM_EPS = 1e-5
PAST_LEN = 16384

LANES = 128
QSUB = 64
KEYS_SUB = QSUB + WINDOW
META_PAD = 64
NKEY = META_PAD + KEYS_SUB
MASKED = -1e30
PROMPT_BLOCK = 1024
ROUTE_BLOCK = 384
EXPERT_ROWS = 512
DISPATCH_TILE = 128
PACK_CHUNKS = D_MODEL // 2 // LANES
COMBINE_TILE = 128
WIN = 32
WIN_ALIGN = 16
WIN_ROWS = WIN + WIN_ALIGN
VMEM_LIMIT = 56 * 1024 * 1024


def _rms(x, g):
    return x * lax.rsqrt(jnp.mean(x * x, axis=-1, keepdims=True) + NORM_EPS) * g


def _router_logits(rwt_ref, h2, h2_hi):
    nt = (((1,), (1,)), ((), ()))
    h2_lo = (h2 - h2_hi.astype(F32)).astype(BF16)
    return (lax.dot_general(rwt_ref[0], h2_hi, nt, preferred_element_type=F32)
            + lax.dot_general(rwt_ref[0], h2_lo, nt, preferred_element_type=F32)
            + lax.dot_general(rwt_ref[1], h2_hi, nt, preferred_element_type=F32))


def _dup_halves(a):
    lane = lax.broadcasted_iota(jnp.int32, a.shape, 1)
    r = pltpu.roll(a, HEAD_DIM, axis=1)
    lo = lane < HEAD_DIM
    return jnp.where(lo, a, r), jnp.where(lo, r, a)


def _pool_means(pext_ref, n):
    outs = []
    for gi, w in enumerate(POOL_WINDOWS):
        xg = pext_ref[:, gi * POOL_GROUP_DIM:(gi + 1) * POOL_GROUP_DIM]
        s = xg
        sh = 1
        while sh < w:
            s = s + pltpu.roll(s, sh, axis=0)
            sh *= 2
        outs.append(s[16:] * (1.0 / w) - xg[16:])
    return outs


def _pack_bf16_pairs(h):
    m = h.shape[1] // 2
    lo = pltpu.bitcast(h[:, :m].astype(BF16).astype(F32), jnp.uint32)
    hi = pltpu.bitcast(h[:, m:].astype(BF16).astype(F32), jnp.uint32)
    return lax.shift_right_logical(lo, jnp.uint32(16)) | (hi & jnp.uint32(0xFFFF0000))


def _unpack_bf16_pairs(w):
    lo = pltpu.bitcast(lax.shift_left(w, jnp.uint32(16)), F32).astype(BF16)
    hi = pltpu.bitcast(w & jnp.uint32(0xFFFF0000), F32).astype(BF16)
    return lo, hi


def _prompt_kernel(x_ref, meta_ref, gattn_ref, win_ref, wpool_ref, pscale_ref, wout_ref, gffn_ref,
                   rwt_ref, rb_ref, sink_ref, tbl_ref, tail_h2_ref, tail_lgt_ref,
                   x1_ref, h2_ref, lgt_ref, kmeta_ref, vmeta_ref, ktail_ref, vtail_ref, ptail_ref,
                   k2buf, v2buf, km2, vm2, qbuf, obuf, pext):
    pid = pl.program_id(0)
    n_main = pl.num_programs(0) - 1
    refs = (x_ref, meta_ref, gattn_ref, win_ref, wpool_ref, pscale_ref, wout_ref, gffn_ref,
            rwt_ref, rb_ref, sink_ref, tbl_ref,
            x1_ref, h2_ref, lgt_ref, kmeta_ref, vmeta_ref, ktail_ref, vtail_ref, ptail_ref,
            k2buf, v2buf, km2, vm2, qbuf, obuf, pext)

    @pl.when(pid < n_main)
    def _():
        _prompt_block(*refs)

    @pl.when(pid == n_main)
    def _():
        h2_ref[0:tail_h2_ref.shape[0], :] = tail_h2_ref[...]
        lgt_ref[:, 0:tail_lgt_ref.shape[1]] = tail_lgt_ref[...]


def _prompt_block(x_ref, meta_ref, gattn_ref, win_ref, wpool_ref, pscale_ref, wout_ref, gffn_ref,
                  rwt_ref, rb_ref, sink_ref, tbl_ref,
                  x1_ref, h2_ref, lgt_ref, kmeta_ref, vmeta_ref, ktail_ref, vtail_ref, ptail_ref,
                  k2buf, v2buf, km2, vm2, qbuf, obuf, pext):
    tb = x_ref.shape[0]
    pid = pl.program_id(0)

    @pl.when(pid == 0)
    def _():
        hm = _rms(meta_ref[...], gattn_ref[...]).astype(BF16)
        km = jnp.dot(hm, win_ref[:, ATTN_WIDTH:ATTN_WIDTH + KV_WIDTH], preferred_element_type=F32)
        vm = jnp.dot(hm, win_ref[:, ATTN_WIDTH + KV_WIDTH:ATTN_WIDTH + 2 * KV_WIDTH],
                     preferred_element_type=F32)
        pm = jnp.dot(hm, win_ref[:, ATTN_WIDTH + 2 * KV_WIDTH:], preferred_element_type=F32)
        kmeta_ref[...] = km
        vmeta_ref[...] = vm
        zpad = jnp.zeros((META_PAD - N_META, LANES), F32)
        k0, k1 = _dup_halves(jnp.concatenate([km, zpad], axis=0))
        v0, v1 = _dup_halves(jnp.concatenate([vm, zpad], axis=0))
        km2[0] = k0.astype(BF16)
        km2[1] = k1.astype(BF16)
        vm2[0, :, 0:LANES] = v0.astype(BF16)
        vm2[1, :, 0:LANES] = v1.astype(BF16)
        vm2[:, :, LANES:] = jnp.ones((N_KV_HEADS, META_PAD, LANES), BF16)
        k2buf[:, 0:WINDOW, :] = jnp.zeros((2, WINDOW, LANES), BF16)
        v2buf[:, 0:WINDOW, 0:LANES] = jnp.zeros((2, WINDOW, LANES), BF16)
        v2buf[:, :, LANES:] = jnp.ones((N_KV_HEADS, WINDOW + tb, LANES), BF16)
        pext[0:16, :] = pm

    h = _rms(x_ref[...], gattn_ref[...]).astype(BF16)
    q = jnp.dot(h, win_ref[:, 0:ATTN_WIDTH], preferred_element_type=F32) * (HEAD_DIM ** -0.5)
    lane_t = lax.broadcasted_iota(jnp.int32, (tb, LANES), 1)
    for c in range(N_HEADS // 2):
        tile = q[:, c * LANES:(c + 1) * LANES]
        for a in range(2):
            keep = (lane_t < HEAD_DIM) if a == 0 else (lane_t >= HEAD_DIM)
            piece = jnp.where(keep, tile, 0.0).astype(BF16).reshape(tb // QSUB, QSUB, LANES)
            row = ((c % 2) * 2 + a) * QSUB
            qbuf[c // 2, :, row:row + QSUB, :] = piece
    k = jnp.dot(h, win_ref[:, ATTN_WIDTH:ATTN_WIDTH + KV_WIDTH], preferred_element_type=F32)
    v = jnp.dot(h, win_ref[:, ATTN_WIDTH + KV_WIDTH:ATTN_WIDTH + 2 * KV_WIDTH], preferred_element_type=F32)
    p = jnp.dot(h, win_ref[:, ATTN_WIDTH + 2 * KV_WIDTH:], preferred_element_type=F32)
    ktail_ref[...] = k[tb - WINDOW:]
    vtail_ref[...] = v[tb - WINDOW:]
    ptail_ref[...] = p[tb - 16:]
    k0, k1 = _dup_halves(k)
    v0, v1 = _dup_halves(v)
    k2buf[0, WINDOW:, :] = k0.astype(BF16)
    k2buf[1, WINDOW:, :] = k1.astype(BF16)
    v2buf[0, WINDOW:, 0:LANES] = v0.astype(BF16)
    v2buf[1, WINDOW:, 0:LANES] = v1.astype(BF16)
    pext[16:, :] = p

    lane_q = lax.broadcasted_iota(jnp.int32, (QSUB, LANES), 1)
    lo_q = lane_q < HEAD_DIM

    for u in range(tb // QSUB):
        r0 = u * QSUB
        sel = jnp.where(pid == 0, u + 1, 0) if u < WINDOW // QSUB else 0
        for g in range(N_KV_HEADS):
            qm = qbuf[g, u]
            kwin = jnp.concatenate([km2[g], k2buf[g, r0:r0 + KEYS_SUB, :]], axis=0)
            vwin = jnp.concatenate([vm2[g], v2buf[g, r0:r0 + KEYS_SUB, :]], axis=0)
            s = lax.dot_general(qm, kwin, (((1,), (1,)), ((), ())), preferred_element_type=F32)
            s = s + tbl_ref[sel, g]
            sink = sink_ref[g]
            m = jnp.maximum(jnp.max(s, axis=1, keepdims=True), sink)
            e = jnp.exp(s - m).astype(BF16)
            r = jnp.dot(e, vwin, preferred_element_type=F32)
            o = r[:, 0:LANES] / (r[:, LANES:] + jnp.exp(sink - m))
            o0 = jnp.where(lo_q, o[0:QSUB], o[QSUB:2 * QSUB])
            o1 = jnp.where(lo_q, o[2 * QSUB:3 * QSUB], o[3 * QSUB:])
            obuf[r0:r0 + QSUB, (2 * g) * LANES:(2 * g + 1) * LANES] = o0.astype(BF16)
            obuf[r0:r0 + QSUB, (2 * g + 1) * LANES:(2 * g + 2) * LANES] = o1.astype(BF16)

    pooled = _pool_means(pext, tb)
    for gi in range(len(POOL_WINDOWS)):
        y = jnp.dot(pooled[gi].astype(BF16), wpool_ref[gi], preferred_element_type=F32)
        y = y * pscale_ref[:, gi * POOL_GROUP_DIM:(gi + 1) * POOL_GROUP_DIM]
        obuf[:, ATTN_WIDTH + gi * POOL_GROUP_DIM:ATTN_WIDTH + (gi + 1) * POOL_GROUP_DIM] = y.astype(BF16)

    k2buf[:, 0:WINDOW, :] = k2buf[:, tb:tb + WINDOW, :]
    v2buf[:, 0:WINDOW, 0:LANES] = v2buf[:, tb:tb + WINDOW, 0:LANES]
    pext[0:16, :] = pext[tb:tb + 16, :]

    x1 = x_ref[...] + jnp.dot(obuf[...], wout_ref[...], preferred_element_type=F32)
    x1_ref[...] = x1
    h2 = _rms(x1, gffn_ref[...])
    h2_hi = h2.astype(BF16)
    h2_ref[...] = h2_hi
    lgt_ref[...] = _router_logits(rwt_ref, h2, h2_hi) + rb_ref[...]


def _attn_tables(sinks):
    i = np.arange(QSUB)[:, None]
    j = np.arange(NKEY)[None, :]
    jb = j - META_PAD
    rel = i + WINDOW - jb
    band_ok = (jb >= 0) & (rel >= 0) & (rel <= WINDOW)
    meta_ok = (j < N_META) & (i >= 0)
    slopes = np.exp2(-8.0 * np.arange(1, N_HEADS + 1) / N_HEADS)
    tbl = np.empty((3, N_KV_HEADS, GQA_GROUP * QSUB, NKEY), np.float32)
    for var in range(3):
        ok = band_ok if var == 0 else band_ok & (jb >= WINDOW - (var - 1) * QSUB)
        for g in range(N_KV_HEADS):
            for a in range(GQA_GROUP):
                hd = g * GQA_GROUP + a
                bias = np.where(ok, -slopes[hd] * rel, MASKED)
                bias = np.where(meta_ok, 0.0, bias)
                tbl[var, g, a * QSUB:(a + 1) * QSUB] = bias
    sink_col = jnp.repeat(sinks.astype(F32).reshape(N_KV_HEADS, GQA_GROUP, 1), QSUB, axis=2)
    return jnp.asarray(tbl), sink_col.reshape(N_KV_HEADS, GQA_GROUP * QSUB, 1)


def _prompt_mixer(x, meta, gattn, win, wpool, pscale, wout, gffn, rwt, rb, sinks, tail_h2, tail_lgt):
    seq = x.shape[0]
    tb = PROMPT_BLOCK
    n_tail = tail_h2.shape[0]
    assert seq % tb == 0 and tb % WINDOW == 0 and n_tail <= tb
    nblk = seq // tb
    n_tok = seq + n_tail
    tbl, sink_col = _attn_tables(sinks)
    full = lambda *shape: pl.BlockSpec(shape, lambda i: (0,) * len(shape))
    main = lambda i: (jnp.minimum(i, nblk - 1), 0)
    in_width = win.shape[1]
    return pl.pallas_call(
        _prompt_kernel,
        grid=(nblk + 1,),
        in_specs=[
            pl.BlockSpec((tb, D_MODEL), main),
            full(N_META, D_MODEL), full(1, D_MODEL), full(D_MODEL, in_width),
            full(len(POOL_WINDOWS), POOL_GROUP_DIM, POOL_GROUP_DIM), full(1, POOL_WIDTH),
            full(D_MODEL, D_MODEL), full(1, D_MODEL), full(2, N_EXPERTS, D_MODEL), full(N_EXPERTS, 1),
            full(N_KV_HEADS, GQA_GROUP * QSUB, 1), full(3, N_KV_HEADS, GQA_GROUP * QSUB, NKEY),
            full(n_tail, D_MODEL), full(N_EXPERTS, n_tail),
        ],
        out_specs=[
            pl.BlockSpec((tb, D_MODEL), main),
            pl.BlockSpec((tb, D_MODEL), lambda i: (i, 0)),
            pl.BlockSpec((N_EXPERTS, tb), lambda i: (0, i)),
            full(N_META, KV_WIDTH), full(N_META, KV_WIDTH),
            full(WINDOW, KV_WIDTH), full(WINDOW, KV_WIDTH), full(16, POOL_WIDTH),
        ],
        out_shape=[
            jax.ShapeDtypeStruct((seq, D_MODEL), F32),
            jax.ShapeDtypeStruct((n_tok, D_MODEL), BF16),
            jax.ShapeDtypeStruct((N_EXPERTS, n_tok), F32),
            jax.ShapeDtypeStruct((N_META, KV_WIDTH), F32),
            jax.ShapeDtypeStruct((N_META, KV_WIDTH), F32),
            jax.ShapeDtypeStruct((WINDOW, KV_WIDTH), F32),
            jax.ShapeDtypeStruct((WINDOW, KV_WIDTH), F32),
            jax.ShapeDtypeStruct((16, POOL_WIDTH), F32),
        ],
        scratch_shapes=[
            pltpu.VMEM((N_KV_HEADS, WINDOW + tb, LANES), BF16),
            pltpu.VMEM((N_KV_HEADS, WINDOW + tb, 2 * LANES), BF16),
            pltpu.VMEM((N_KV_HEADS, META_PAD, LANES), BF16),
            pltpu.VMEM((N_KV_HEADS, META_PAD, 2 * LANES), BF16),
            pltpu.VMEM((N_KV_HEADS, tb // QSUB, GQA_GROUP * QSUB, LANES), BF16),
            pltpu.VMEM((tb, D_MODEL), BF16),
            pltpu.VMEM((16 + tb, POOL_WIDTH), F32),
        ],
        compiler_params=pltpu.CompilerParams(dimension_semantics=("arbitrary",),
                                             vmem_limit_bytes=VMEM_LIMIT),
        name="prompt_mixer",
    )(x, meta, gattn, win, wpool, pscale, wout, gffn, rwt, rb, sink_col, tbl, tail_h2, tail_lgt)


def _sample_kernel(x_ref, ck_ref, cv_ref, sp_ref, gattn_ref, win_ref, wpool_ref, pscale_ref, wout_ref,
                   gffn_ref, rwt_ref, rb_ref, sinkc_ref, bias_ref,
                   x1_ref, h2_ref, lgt_ref, knew_ref, vnew_ref, pnew_ref,
                   qm_buf, r_buf, obuf):
    nb = x_ref.shape[0]
    x = x_ref[...]
    h = _rms(x, gattn_ref[...]).astype(BF16)
    q = jnp.dot(h, win_ref[:, 0:ATTN_WIDTH], preferred_element_type=F32) * (HEAD_DIM ** -0.5)
    k = jnp.dot(h, win_ref[:, ATTN_WIDTH:ATTN_WIDTH + KV_WIDTH], preferred_element_type=F32)
    v = jnp.dot(h, win_ref[:, ATTN_WIDTH + KV_WIDTH:ATTN_WIDTH + 2 * KV_WIDTH], preferred_element_type=F32)
    p = jnp.dot(h, win_ref[:, ATTN_WIDTH + 2 * KV_WIDTH:], preferred_element_type=F32)
    knew_ref[...] = k
    vnew_ref[...] = v
    pnew_ref[...] = p

    lane = lax.broadcasted_iota(jnp.int32, (nb, LANES), 1)
    lo = lane < HEAD_DIM
    for hd in range(N_HEADS):
        tile = q[:, (hd // 2) * LANES:(hd // 2 + 1) * LANES]
        if (hd % 2) != (hd // GQA_GROUP):
            tile = pltpu.roll(tile, HEAD_DIM, axis=1)
        keep_lo = (hd // GQA_GROUP) == 0
        qm_buf[:, hd, :] = jnp.where(lo if keep_lo else jnp.logical_not(lo), tile, 0.0)

    def per_batch(b):
        qm = qm_buf[b]
        kb = ck_ref[b].astype(BF16)
        vb = cv_ref[b].astype(BF16)
        s = lax.dot_general(qm.astype(BF16), kb, (((1,), (1,)), ((), ())), preferred_element_type=F32)
        s = s + bias_ref[...]
        kn = knew_ref[pl.ds(b, 1), :]
        vn = vnew_ref[pl.ds(b, 1), :]
        s_self = jnp.sum(qm * kn, axis=1, keepdims=True)
        sink = sinkc_ref[...]
        m = jnp.maximum(jnp.maximum(jnp.max(s, axis=1, keepdims=True), s_self), sink)
        e = jnp.exp(s - m)
        e_self = jnp.exp(s_self - m)
        den = jnp.sum(e, axis=1, keepdims=True) + e_self + jnp.exp(sink - m)
        r = jnp.dot(e.astype(BF16), vb, preferred_element_type=F32)
        r = r + e_self * vn
        r_buf[b] = r / den

    unroll = 8
    assert nb % unroll == 0

    def batch_group(gidx, carry):
        for j in range(unroll):
            per_batch(gidx * unroll + j)
        return carry

    lax.fori_loop(0, nb // unroll, batch_group, 0)

    for c in range(N_HEADS // 2):
        halves = []
        for a in range(2):
            hd = 2 * c + a
            t = r_buf[:, hd, :]
            if (hd // GQA_GROUP) != a:
                t = pltpu.roll(t, HEAD_DIM, axis=1)
            halves.append(t)
        obuf[:, c * LANES:(c + 1) * LANES] = jnp.where(lo, halves[0], halves[1]).astype(BF16)

    for gi, w in enumerate(POOL_WINDOWS):
        cols = slice(gi * POOL_GROUP_DIM, (gi + 1) * POOL_GROUP_DIM)
        pg = p[:, cols]
        acc = pg
        for d in range(1, w):
            acc = acc + sp_ref[:, POOL_STATE - d, cols]
        pooled = acc * (1.0 / w) - pg
        y = jnp.dot(pooled.astype(BF16), wpool_ref[gi], preferred_element_type=F32) * pscale_ref[:, cols]
        obuf[:, ATTN_WIDTH + gi * POOL_GROUP_DIM:ATTN_WIDTH + (gi + 1) * POOL_GROUP_DIM] = y.astype(BF16)

    x1 = x + jnp.dot(obuf[...], wout_ref[...], preferred_element_type=F32)
    x1_ref[...] = x1
    h2 = _rms(x1, gffn_ref[...])
    h2_hi = h2.astype(BF16)
    h2_ref[...] = h2_hi
    lgt_ref[...] = _router_logits(rwt_ref, h2, h2_hi) + rb_ref[...]


def _sample_mixer(x, ck, cv, sp, gattn, win, wpool, pscale, wout, gffn, rwt, rb, sinks):
    nb = x.shape[0]
    rows = ck.shape[1]
    slopes = np.exp2(-8.0 * np.arange(1, N_HEADS + 1) / N_HEADS)
    dist = np.concatenate([np.zeros(N_META), WINDOW - np.arange(WINDOW)])
    bias = jnp.asarray((-slopes[:, None] * dist[None, :]).astype(np.float32))
    vm = pl.BlockSpec(memory_space=pltpu.VMEM)
    return pl.pallas_call(
        _sample_kernel,
        in_specs=[vm] * 14,
        out_specs=[vm] * 6,
        out_shape=[
            jax.ShapeDtypeStruct((nb, D_MODEL), F32),
            jax.ShapeDtypeStruct((nb, D_MODEL), BF16),
            jax.ShapeDtypeStruct((N_EXPERTS, nb), F32),
            jax.ShapeDtypeStruct((nb, KV_WIDTH), F32),
            jax.ShapeDtypeStruct((nb, KV_WIDTH), F32),
            jax.ShapeDtypeStruct((nb, POOL_WIDTH), F32),
        ],
        scratch_shapes=[
            pltpu.VMEM((nb, N_HEADS, LANES), F32),
            pltpu.VMEM((nb, N_HEADS, LANES), F32),
            pltpu.VMEM((nb, D_MODEL), BF16),
        ],
        compiler_params=pltpu.CompilerParams(vmem_limit_bytes=VMEM_LIMIT),
        name="sample_mixer",
    )(x, ck, cv, sp, gattn, win, wpool, pscale, wout, gffn, rwt, rb,
      sinks.astype(F32).reshape(N_HEADS, 1), bias)


def _router_kernel(lg_ref, tri_ref, low_ref, eidx_ref, rank_ref, gate_ref, col_ref, lrank_ref, lpos_ref,
                   tcar_ref, cnt_ref, carry):
    tr = lg_ref.shape[1]

    @pl.when(pl.program_id(0) == 0)
    def _():
        carry[...] = jnp.zeros_like(carry)

    work = lg_ref[...]
    eio = lax.broadcasted_iota(jnp.int32, work.shape, 0).astype(F32)
    sels, vals, idxs = [], [], []
    for _k in range(TOP_K):
        mx = jnp.max(work, axis=0, keepdims=True)
        idx = jnp.min(jnp.where(work == mx, eio, float(N_EXPERTS)), axis=0, keepdims=True)
        sel = eio == idx
        sels.append(sel)
        vals.append(mx)
        idxs.append(idx)
        work = jnp.where(sel, -jnp.inf, work)
    exps = [jnp.exp(vk - vals[0]) for vk in vals]
    tot = exps[0] + exps[1] + exps[2] + exps[3]
    onehot = jnp.zeros(work.shape, F32)
    for sel in sels:
        onehot = onehot + sel.astype(F32)
    before = jnp.dot(onehot.astype(BF16), tri_ref[...], preferred_element_type=F32) + carry[...]
    for kk in range(TOP_K):
        eidx_ref[pl.ds(kk, 1), :] = idxs[kk].astype(jnp.int32)
        gate_ref[pl.ds(kk, 1), :] = exps[kk] / tot
        rk = jnp.sum(jnp.where(sels[kk], before, 0.0), axis=0, keepdims=True)
        rank_ref[pl.ds(kk, 1), :] = rk.astype(jnp.int32)
    for j in range(tr // COMBINE_TILE):
        cols = slice(j * COMBINE_TILE, (j + 1) * COMBINE_TILE)
        tc = before[:, j * COMBINE_TILE:j * COMBINE_TILE + 1]
        tcar_ref[j] = tc.astype(jnp.int32)
        slack = tc - WIN_ALIGN * jnp.floor(tc * (1.0 / WIN_ALIGN))
        local = before[:, cols] - tc
        tile_cnt = jnp.broadcast_to(jnp.sum(onehot[:, cols], axis=1, keepdims=True), local.shape)
        run_start = jnp.dot(low_ref[...], tile_cnt.astype(BF16), preferred_element_type=F32)
        for kk in range(TOP_K):
            selk = sels[kk][:, cols]
            lr = jnp.sum(jnp.where(selk, local, 0.0), axis=0, keepdims=True)
            sl = jnp.sum(jnp.where(selk, slack, 0.0), axis=0, keepdims=True)
            lp = jnp.sum(jnp.where(selk, run_start + local, 0.0), axis=0, keepdims=True)
            lpos_ref[pl.ds(kk, 1), cols] = lp.astype(jnp.int32)
            lrank_ref[pl.ds(kk, 1), cols] = lr.astype(jnp.int32)
            col_ref[pl.ds(kk, 1), cols] = (idxs[kk][:, cols] * float(WIN_ROWS) + sl + lr).astype(jnp.int32)
    carry[...] = carry[...] + jnp.sum(onehot, axis=1, keepdims=True)
    cnt_ref[...] = carry[...].astype(jnp.int32)


def _router(logits_t):
    n = logits_t.shape[1]
    tr = ROUTE_BLOCK
    assert n % tr == 0
    tri = jnp.asarray(np.triu(np.ones((tr, tr), np.float32), k=1), BF16)
    low = jnp.asarray(np.tril(np.ones((N_EXPERTS, N_EXPERTS), np.float32), k=-1), BF16)
    per_tok = pl.BlockSpec((TOP_K, tr), lambda i: (0, i))
    return pl.pallas_call(
        _router_kernel,
        grid=(n // tr,),
        in_specs=[pl.BlockSpec((N_EXPERTS, tr), lambda i: (0, i)),
                  pl.BlockSpec((tr, tr), lambda i: (0, 0)),
                  pl.BlockSpec((N_EXPERTS, N_EXPERTS), lambda i: (0, 0))],
        out_specs=[per_tok, per_tok, per_tok, per_tok, per_tok, per_tok,
                   pl.BlockSpec((tr // COMBINE_TILE, N_EXPERTS, 1), lambda i: (i, 0, 0)),
                   pl.BlockSpec((N_EXPERTS, 1), lambda i: (0, 0))],
        out_shape=[jax.ShapeDtypeStruct((TOP_K, n), jnp.int32),
                   jax.ShapeDtypeStruct((TOP_K, n), jnp.int32),
                   jax.ShapeDtypeStruct((TOP_K, n), F32),
                   jax.ShapeDtypeStruct((TOP_K, n), jnp.int32),
                   jax.ShapeDtypeStruct((TOP_K, n), jnp.int32),
                   jax.ShapeDtypeStruct((TOP_K, n), jnp.int32),
                   jax.ShapeDtypeStruct((n // COMBINE_TILE, N_EXPERTS, 1), jnp.int32),
                   jax.ShapeDtypeStruct((N_EXPERTS, 1), jnp.int32)],
        scratch_shapes=[pltpu.VMEM((N_EXPERTS, 1), F32)],
        compiler_params=pltpu.CompilerParams(dimension_semantics=("arbitrary",)),
        name="router",
    )(logits_t, tri, low)


def _dispatch_kernel(lstart_ref, cnt_ref, dst_ref, h2_ref, lpos_ref, xs_hbm, stg0, stg1, sem):
    i = pl.program_id(0)
    n_tiles = pl.num_programs(0)
    dt = h2_ref.shape[0]
    rows = dt * TOP_K
    slot = i % 2
    stgs = (stg0, stg1)
    pieces = [p for p in (128, 64, 32, 16, 8, 4, 2, 1) if p <= dt]
    assert dt <= 128

    def drain(s):
        pltpu.make_async_copy(stgs[s], xs_hbm.at[pl.ds(0, rows)], sem.at[s]).wait()

    def issue_runs(tile, live, s):
        for e in range(N_EXPERTS):
            n = jnp.where(live, cnt_ref[tile * N_EXPERTS + e], 0)
            src0 = lstart_ref[tile * N_EXPERTS + e]
            dst0 = dst_ref[tile * N_EXPERTS + e]
            for piece in pieces:
                off = n & ~jnp.int32(2 * piece - 1)

                @pl.when((n & piece) != 0)
                def _(off=off, piece=piece, src0=src0, dst0=dst0):
                    pltpu.make_async_copy(stgs[s].at[pl.ds(src0 + off, piece)],
                                          xs_hbm.at[pl.ds(dst0 + off, piece)], sem.at[s]).start()

    def sort_tile(s):
        rid = lax.broadcasted_iota(jnp.int32, (rows, dt), 0)
        hit = rid == lpos_ref[0:1, :]
        for kk in range(1, TOP_K):
            hit = jnp.logical_or(hit, rid == lpos_ref[kk:kk + 1, :])
        perm = jnp.where(hit, 1.0, 0.0).astype(BF16)
        srt = jnp.dot(perm, h2_ref[...], preferred_element_type=F32)
        packed = _pack_bf16_pairs(srt)
        for c in range(PACK_CHUNKS):
            stgs[s][:, c, :] = packed[:, c * LANES:(c + 1) * LANES]

    for s in range(2):
        @pl.when(slot == s)
        def _(s=s):
            @pl.when(i >= 2)
            def _():
                drain(s)

            issue_runs(jnp.maximum(i - 1, 0), i >= 1, 1 - s)
            sort_tile(s)

            @pl.when(i == n_tiles - 1)
            def _():
                issue_runs(i, True, s)
                drain(s)

                @pl.when(n_tiles >= 2)
                def _():
                    drain(1 - s)


def _dispatch(h2, lpos, lstart, cnt, dst, cap):
    n_tok = h2.shape[0]
    dt = DISPATCH_TILE
    assert n_tok % dt == 0
    grid_spec = pltpu.PrefetchScalarGridSpec(
        num_scalar_prefetch=3,
        grid=(n_tok // dt,),
        in_specs=[pl.BlockSpec((dt, D_MODEL), lambda i, a, b, c: (i, 0)),
                  pl.BlockSpec((TOP_K, dt), lambda i, a, b, c: (0, i)),
                  ],
        out_specs=pl.BlockSpec(memory_space=pl.ANY),
        scratch_shapes=[pltpu.VMEM((dt * TOP_K, PACK_CHUNKS, LANES), jnp.uint32),
                        pltpu.VMEM((dt * TOP_K, PACK_CHUNKS, LANES), jnp.uint32),
                        pltpu.SemaphoreType.DMA((2,))],
    )
    return pl.pallas_call(
        _dispatch_kernel,
        grid_spec=grid_spec,
        out_shape=jax.ShapeDtypeStruct((cap, PACK_CHUNKS, LANES), jnp.uint32),
        compiler_params=pltpu.CompilerParams(dimension_semantics=("arbitrary",),
                                             vmem_limit_bytes=VMEM_LIMIT),
        name="dispatch",
    )(lstart, cnt, dst, h2, lpos)


def _expert_kernel(n_xblocks, bexp_ref, nvalid_ref, epos_ref, elist_ref,
                   x_hbm, wgu_hbm, bgu_ref, wd_hbm, bd_ref, y_ref,
                   wgu_f32, wd_f32, wgu_bf, wd_bf, xbuf, xsem, wsem):
    i = pl.program_id(0)
    rb = y_ref.shape[0]
    nvalid = nvalid_ref[i]
    pos = epos_ref[i]
    fresh = jnp.logical_or(i == 0, pos != epos_ref[jnp.maximum(i - 1, 0)])
    slot = i % 2

    def x_copies(blk, s):
        return [pltpu.make_async_copy(x_hbm.at[pl.ds(blk * rb, rb), c, :],
                                      xbuf.at[s, :, pl.ds(c * LANES, LANES)], xsem.at[s])
                for c in range(PACK_CHUNKS)]

    def w_copies(p):
        e = elist_ref[p]
        s = p % 2
        return [pltpu.make_async_copy(wgu_hbm.at[e], wgu_f32.at[s], wsem.at[s, 0]),
                pltpu.make_async_copy(wd_hbm.at[e], wd_f32.at[s], wsem.at[s, 1])]

    @pl.when(i == 0)
    def _():
        for cp in x_copies(0, 0):
            cp.start()

        @pl.when(nvalid > 0)
        def _():
            for cp in w_copies(0):
                cp.start()

    @pl.when(i + 1 < n_xblocks)
    def _():
        for cp in x_copies(i + 1, 1 - slot):
            cp.start()

    @pl.when(jnp.logical_and(fresh, nvalid > 0))
    def _():
        @pl.when(elist_ref[pos + 1] >= 0)
        def _():
            for cp in w_copies(pos + 1):
                cp.start()

        for cp in w_copies(pos):
            cp.wait()
        ws = pos % 2
        chunk = 32

        def cast_gu(r, c):
            r0 = pl.multiple_of(r * chunk, chunk)
            wgu_bf[pl.ds(r0, chunk), :] = wgu_f32[ws, pl.ds(r0, chunk), :].astype(BF16)
            return c

        def cast_d(r, c):
            r0 = pl.multiple_of(r * chunk, chunk)
            wd_bf[pl.ds(r0, chunk), :] = wd_f32[ws, pl.ds(r0, chunk), :].astype(BF16)
            return c

        lax.fori_loop(0, D_MODEL // chunk, cast_gu, 0)
        lax.fori_loop(0, D_EXPERT // chunk, cast_d, 0)

    @pl.when(i < n_xblocks)
    def _():
        for cp in x_copies(i, slot):
            cp.wait()

    def ffn(rows):
        xw = xbuf[slot, 0:rows, :]
        xw = jnp.where(lax.broadcasted_iota(jnp.int32, xw.shape, 0) < nvalid, xw, jnp.uint32(0))
        xlo, xhi = _unpack_bf16_pairs(xw)
        half = D_MODEL // 2
        g = (jnp.dot(xlo, wgu_bf[0:half, 0:D_EXPERT], preferred_element_type=F32)
             + jnp.dot(xhi, wgu_bf[half:, 0:D_EXPERT], preferred_element_type=F32)
             + bgu_ref[0, :, 0:D_EXPERT])
        u = (jnp.dot(xlo, wgu_bf[0:half, D_EXPERT:], preferred_element_type=F32)
             + jnp.dot(xhi, wgu_bf[half:, D_EXPERT:], preferred_element_type=F32)
             + bgu_ref[0, :, D_EXPERT:])
        g = jnp.minimum(g, SWIGLU_LIMIT)
        u = jnp.clip(u, -SWIGLU_LIMIT, SWIGLU_LIMIT)
        act = g * (1.0 / (1.0 + jnp.exp(-SWIGLU_ALPHA * g))) * (u + 1.0)
        y = jnp.dot(act.astype(BF16), wd_bf[...], preferred_element_type=F32) + bd_ref[0]
        row = lax.broadcasted_iota(jnp.int32, y.shape, 0)
        y_ref[0:rows, :] = jnp.where(row < nvalid, y, 0.0).astype(BF16)
        if rows < rb:
            y_ref[rows:, :] = jnp.zeros((rb - rows, D_MODEL), BF16)

    @pl.when(nvalid > rb // 2)
    def _():
        ffn(rb)

    @pl.when(jnp.logical_and(nvalid > 0, nvalid <= rb // 2))
    def _():
        ffn(rb // 2)

    @pl.when(nvalid == 0)
    def _():
        y_ref[...] = jnp.zeros_like(y_ref)


def _experts(xs, block_expert, nvalid, block_pos, expert_list, wgu, bgu, wd, bd):
    rb = EXPERT_ROWS
    n_xblocks = xs.shape[0] // rb
    nblk = n_xblocks + 1
    any_space = pl.BlockSpec(memory_space=pl.ANY)
    grid_spec = pltpu.PrefetchScalarGridSpec(
        num_scalar_prefetch=4,
        grid=(nblk,),
        in_specs=[
            any_space,
            any_space,
            pl.BlockSpec((1, 1, 2 * D_EXPERT), lambda i, be, nu, ep, el: (be[i], 0, 0)),
            any_space,
            pl.BlockSpec((1, 1, D_MODEL), lambda i, be, nu, ep, el: (be[i], 0, 0)),
        ],
        out_specs=pl.BlockSpec((rb, D_MODEL), lambda i, be, nu, ep, el: (i, 0)),
        scratch_shapes=[pltpu.VMEM((2, D_MODEL, 2 * D_EXPERT), F32),
                        pltpu.VMEM((2, D_EXPERT, D_MODEL), F32),
                        pltpu.VMEM((D_MODEL, 2 * D_EXPERT), BF16),
                        pltpu.VMEM((D_EXPERT, D_MODEL), BF16),
                        pltpu.VMEM((2, rb, D_MODEL // 2), jnp.uint32),
                        pltpu.SemaphoreType.DMA((2,)),
                        pltpu.SemaphoreType.DMA((2, 2))],
    )
    return pl.pallas_call(
        functools.partial(_expert_kernel, n_xblocks),
        grid_spec=grid_spec,
        out_shape=jax.ShapeDtypeStruct((nblk * rb, D_MODEL), BF16),
        compiler_params=pltpu.CompilerParams(dimension_semantics=("arbitrary",),
                                             vmem_limit_bytes=VMEM_LIMIT),
        name="experts",
    )(block_expert, nvalid, block_pos, expert_list, xs, wgu, bgu.reshape(N_EXPERTS, 1, 2 * D_EXPERT), wd,
      bd.reshape(N_EXPERTS, 1, D_MODEL))


def _combine_kernel(n_prompt_tiles, offa_ref, nchunk_ref,
                    x1p_ref, x1s_ref, info_ref, info_next_ref, gfin_ref, yb_hbm,
                    outp_ref, outs_ref, ybuf, gbuf0, gbuf1, acc_ref, sem):
    i = pl.program_id(0)
    n_tiles = pl.num_programs(0)

    def window_copy(tile, chunk, e, slot):
        base = pl.multiple_of(offa_ref[tile * N_EXPERTS + e] + chunk * WIN, WIN_ALIGN)
        return pltpu.make_async_copy(yb_hbm.at[pl.ds(base, WIN_ROWS), :],
                                     ybuf.at[slot, pl.ds(e * WIN_ROWS, WIN_ROWS), :],
                                     sem.at[slot])

    def start_windows(tile, chunk, slot):
        for e in range(N_EXPERTS):
            window_copy(tile, chunk, e, slot).start()

    def wait_windows(slot):
        pltpu.make_async_copy(yb_hbm.at[pl.ds(0, N_EXPERTS * WIN_ROWS), :], ybuf.at[slot],
                              sem.at[slot]).wait()

    slot = i % 2

    @pl.when(i == 0)
    def _():
        start_windows(0, 0, 0)

    lane = lax.broadcasted_iota(jnp.int32, (COMBINE_TILE, N_EXPERTS * WIN_ROWS), 1)

    def gate_matrix(ref, chunk):
        g = jnp.zeros(lane.shape, F32)
        for kk in range(TOP_K):
            lr = ref[:, TOP_K + kk:TOP_K + kk + 1]
            in_chunk = jnp.logical_and(lr >= chunk * WIN, lr < chunk * WIN + WIN)
            colk = jnp.where(in_chunk, ref[:, kk:kk + 1] - chunk * WIN, -1.0).astype(jnp.int32)
            g = jnp.where(lane == colk, ref[:, 2 * TOP_K + kk:2 * TOP_K + kk + 1], g)
        hi = g.astype(BF16)
        lo = (g - hi.astype(F32)).astype(BF16)
        return jnp.concatenate([hi, lo], axis=0)

    def moe_rows(gm, buf):
        r = jnp.dot(gm, ybuf[buf], preferred_element_type=F32)
        return r[0:COMBINE_TILE] + r[COMBINE_TILE:]

    @pl.when(i == 0)
    def _():
        gbuf0[...] = gate_matrix(info_ref, 0)

    def main(s):
        start_windows(jnp.minimum(i + 1, n_tiles - 1), 0, 1 - s)
        wait_windows(s)
        g_cur, g_nxt = (gbuf0, gbuf1) if s == 0 else (gbuf1, gbuf0)
        acc_ref[...] = moe_rows(g_cur[...], s)
        g_nxt[...] = gate_matrix(info_next_ref, 0)

    for s in range(2):
        @pl.when(slot == s)
        def _(s=s):
            main(s)

    @pl.when(i == n_tiles - 1)
    def _():
        wait_windows(1 - slot)

    def extra_chunk(j, c):
        start_windows(i, j, 2)
        wait_windows(2)
        acc_ref[...] += moe_rows(gate_matrix(info_ref, j), 2)
        return c

    lax.fori_loop(1, nchunk_ref[i], extra_chunk, 0)

    @pl.when(i < n_prompt_tiles)
    def _():
        outp_ref[...] = _rms(x1p_ref[...] + acc_ref[...], gfin_ref[...])

    @pl.when(i >= n_prompt_tiles)
    def _():
        outs_ref[...] = _rms(x1s_ref[...] + acc_ref[...], gfin_ref[...])


def _combine(x1_p, x1_s, tok_info, gfin, yb, offa, nchunk):
    ct = COMBINE_TILE
    n_p, n_s = x1_p.shape[0] // ct, x1_s.shape[0] // ct
    assert x1_p.shape[0] % ct == 0 and x1_s.shape[0] % ct == 0 and n_s >= 1
    n_info = tok_info.shape[1]
    grid_spec = pltpu.PrefetchScalarGridSpec(
        num_scalar_prefetch=2,
        grid=(n_p + n_s,),
        in_specs=[
            pl.BlockSpec((ct, D_MODEL), lambda i, o, c: (jnp.minimum(i, n_p - 1), 0)),
            pl.BlockSpec((ct, D_MODEL), lambda i, o, c: (jnp.maximum(i - n_p, 0), 0)),
            pl.BlockSpec((ct, n_info), lambda i, o, c: (i, 0)),
            pl.BlockSpec((ct, n_info), lambda i, o, c: (jnp.minimum(i + 1, n_p + n_s - 1), 0)),
            pl.BlockSpec((1, D_MODEL), lambda i, o, c: (0, 0)),
            pl.BlockSpec(memory_space=pl.ANY),
        ],
        out_specs=[
            pl.BlockSpec((ct, D_MODEL), lambda i, o, c: (jnp.minimum(i, n_p - 1), 0)),
            pl.BlockSpec((ct, D_MODEL), lambda i, o, c: (jnp.maximum(i - n_p, 0), 0)),
        ],
        scratch_shapes=[pltpu.VMEM((3, N_EXPERTS * WIN_ROWS, D_MODEL), BF16),
                        pltpu.VMEM((2 * ct, N_EXPERTS * WIN_ROWS), BF16),
                        pltpu.VMEM((2 * ct, N_EXPERTS * WIN_ROWS), BF16),
                        pltpu.VMEM((ct, D_MODEL), F32),
                        pltpu.SemaphoreType.DMA((3,))],
    )
    return pl.pallas_call(
        functools.partial(_combine_kernel, n_p),
        grid_spec=grid_spec,
        out_shape=[jax.ShapeDtypeStruct(x1_p.shape, F32), jax.ShapeDtypeStruct(x1_s.shape, F32)],
        compiler_params=pltpu.CompilerParams(dimension_semantics=("arbitrary",),
                                             vmem_limit_bytes=VMEM_LIMIT),
        name="combine",
    )(offa, nchunk, x1_p, x1_s, tok_info, tok_info, gfin, yb)


def kernel(x_prompt, x_sample, cache_k, cache_v, state_pool, meta_tokens, norm_attn, w_in, attn_sinks,
           w_pool, pool_scale, w_out, norm_ffn, router_w, router_b, w_gate_up, b_gate_up, w_down, b_down,
           norm_final):
    assert w_in.shape[0] == 1, "single-layer trunk"
    bsz, seq, _ = x_prompt.shape
    assert bsz == 1
    nb = x_sample.shape[0]
    n_tok = seq + nb
    gattn = norm_attn[0].reshape(1, D_MODEL)
    gffn = norm_ffn[0].reshape(1, D_MODEL)
    win = w_in[0].astype(BF16)
    wpool = w_pool[0].astype(BF16)
    wout = w_out[0].astype(BF16)
    pscale = pool_scale[0].reshape(1, POOL_WIDTH)
    rw_t = router_w[0].T
    rw_hi = rw_t.astype(BF16)
    rwt = jnp.stack([rw_hi, (rw_t - rw_hi.astype(F32)).astype(BF16)])
    rb = router_b[0].reshape(N_EXPERTS, 1)
    sinks = attn_sinks[0]

    ck = cache_k[0].reshape(nb, N_META + WINDOW, KV_WIDTH)
    cv = cache_v[0].reshape(nb, N_META + WINDOW, KV_WIDTH)
    (x1_s, h2_s, lgt_s, knew, vnew, pnew) = _sample_mixer(
        x_sample[:, 0], ck, cv, state_pool[0], gattn, win, wpool, pscale, wout, gffn, rwt, rb, sinks)
    (x1_p, h2_all, lgt_all, kmeta, vmeta, ktail, vtail, ptail) = _prompt_mixer(
        x_prompt[0], meta_tokens, gattn, win, wpool, pscale, wout, gffn, rwt, rb, sinks, h2_s, lgt_s)

    _eidx, _rank, gates, col, lrank, lpos, tcar, counts = _router(lgt_all)
    counts = counts[:, 0]
    tcar = tcar[:, :, 0]
    rbk = EXPERT_ROWS
    eids = jnp.arange(N_EXPERTS, dtype=jnp.int32)
    earlier = eids[None, :] < eids[:, None]
    excl_sum = lambda a: jnp.sum(jnp.where(earlier, a[..., None, :], 0), axis=-1)
    padded = (counts + rbk - 1) // rbk * rbk
    pad_start = excl_sum(padded).astype(jnp.int32)
    pad_end = pad_start + padded
    nblk = -(-(n_tok * TOP_K) // rbk) + N_EXPERTS
    cap = nblk * rbk
    block_start = jnp.arange(nblk + 1, dtype=jnp.int32) * rbk
    owns = (pad_start[None, :] <= block_start[:, None]) & (block_start[:, None] < pad_end[None, :])
    nvalid = jnp.sum(jnp.where(owns, jnp.clip(counts[None, :] - (block_start[:, None] - pad_start[None, :]),
                                              0, rbk), 0), axis=1).astype(jnp.int32)
    has_rows = counts > 0
    last_e = jnp.max(jnp.where(has_rows, eids, 0))
    block_expert = jnp.where(jnp.any(owns, axis=1), jnp.sum(jnp.where(owns, eids[None, :], 0), axis=1),
                             last_e).astype(jnp.int32)

    run_len = jnp.concatenate([tcar[1:], counts[None, :]], axis=0) - tcar
    run_lstart = excl_sum(run_len)
    run_dst = pad_start[None, :] + tcar
    flat = lambda a: a.astype(jnp.int32).reshape(-1)
    xs = _dispatch(h2_all, lpos, flat(run_lstart), flat(run_len), flat(run_dst), cap)
    expert_pos = excl_sum(has_rows.astype(jnp.int32))
    at_pos = has_rows[None, :] & (expert_pos[None, :] == jnp.arange(N_EXPERTS + 1, dtype=jnp.int32)[:, None])
    expert_list = jnp.where(jnp.any(at_pos, axis=1), jnp.sum(jnp.where(at_pos, eids[None, :], 0), axis=1),
                            -1).astype(jnp.int32)
    block_pos = jnp.sum(jnp.where(block_expert[:, None] == eids[None, :], expert_pos[None, :], 0),
                        axis=1).astype(jnp.int32)
    yb = _experts(xs, block_expert, nvalid, block_pos, expert_list,
                  w_gate_up[0], b_gate_up[0], w_down[0], b_down[0])

    offa = (pad_start[None, :] + (tcar - tcar % WIN_ALIGN)).astype(jnp.int32).reshape(-1)
    nchunk = jnp.maximum(jnp.max((run_len + WIN - 1) // WIN, axis=1), 1).astype(jnp.int32)
    gfin = norm_final.reshape(1, D_MODEL)
    tok_info = jnp.concatenate([col.astype(F32), lrank.astype(F32), gates], axis=0).T
    y_prompt, y_sample = _combine(x1_p, x1_s, tok_info, gfin, yb, offa, nchunk)

    kv_shape = (1, 1, N_META + WINDOW, N_KV_HEADS, HEAD_DIM)
    new_k_p = jnp.concatenate([kmeta, ktail], axis=0).reshape(kv_shape)
    new_v_p = jnp.concatenate([vmeta, vtail], axis=0).reshape(kv_shape)
    new_pool_p = ptail[16 - POOL_STATE:].reshape(1, 1, POOL_STATE, POOL_WIDTH)
    new_k_s = jnp.concatenate([ck[:, :N_META], ck[:, N_META + 1:], knew[:, None]], axis=1).reshape(
        (1, nb, N_META + WINDOW, N_KV_HEADS, HEAD_DIM))
    new_v_s = jnp.concatenate([cv[:, :N_META], cv[:, N_META + 1:], vnew[:, None]], axis=1).reshape(
        (1, nb, N_META + WINDOW, N_KV_HEADS, HEAD_DIM))
    new_pool_s = jnp.concatenate([state_pool[0][:, 1:], pnew[:, None]], axis=1)[None]
    return (y_prompt[None], y_sample[:, None], new_k_p, new_v_p, new_pool_p, new_k_s, new_v_s, new_pool_s)
```

```python
import functools

import jax
import jax.numpy as jnp
import numpy as np
from jax import lax
from jax.experimental import pallas as pl
from jax.experimental.pallas import tpu as pltpu

F32 = jnp.float32
BF16 = jnp.bfloat16

D_MODEL = 1024
N_META = 16
N_HEADS = 8
HEAD_DIM = 64
N_KV_HEADS = 2
GQA_GROUP = N_HEADS // N_KV_HEADS
ATTN_WIDTH = N_HEADS * HEAD_DIM
KV_WIDTH = N_KV_HEADS * HEAD_DIM
WINDOW = 128
POOL_WIDTH = D_MODEL - ATTN_WIDTH
POOL_WINDOWS = (2, 4, 8, 16)
POOL_GROUP_DIM = POOL_WIDTH // len(POOL_WINDOWS)
POOL_STATE = max(POOL_WINDOWS) - 1
N_EXPERTS = 32
TOP_K = 4
D_EXPERT = D_MODEL
SWIGLU_ALPHA = 1.702
SWIGLU_LIMIT = 7.0
NORM_EPS = 1e-5
PAST_LEN = 16384

LANES = 128
QSUB = 64
KEYS_SUB = QSUB + WINDOW
META_PAD = 64
NKEY = META_PAD + KEYS_SUB
MASKED = -1e30
PROMPT_BLOCK = 1024
ROUTE_BLOCK = 384
EXPERT_ROWS = 512
DISPATCH_TILE = 384
PACK_CHUNKS = D_MODEL // 2 // LANES
COMBINE_TILE = 128
WIN = 32
WIN_ALIGN = 16
WIN_ROWS = WIN + WIN_ALIGN
VMEM_LIMIT = 56 * 1024 * 1024


def _rms(x, g):
    return x * lax.rsqrt(jnp.mean(x * x, axis=-1, keepdims=True) + NORM_EPS) * g


def _router_logits(rwt_ref, h2, h2_hi):
    nt = (((1,), (1,)), ((), ()))
    h2_lo = (h2 - h2_hi.astype(F32)).astype(BF16)
    return (lax.dot_general(rwt_ref[0], h2_hi, nt, preferred_element_type=F32)
            + lax.dot_general(rwt_ref[0], h2_lo, nt, preferred_element_type=F32)
            + lax.dot_general(rwt_ref[1], h2_hi, nt, preferred_element_type=F32))


def _dup_halves(a):
    lane = lax.broadcasted_iota(jnp.int32, a.shape, 1)
    r = pltpu.roll(a, HEAD_DIM, axis=1)
    lo = lane < HEAD_DIM
    return jnp.where(lo, a, r), jnp.where(lo, r, a)


def _pool_means(pext_ref, n):
    outs = []
    for gi, w in enumerate(POOL_WINDOWS):
        xg = pext_ref[:, gi * POOL_GROUP_DIM:(gi + 1) * POOL_GROUP_DIM]
        s = xg
        sh = 1
        while sh < w:
            s = s + pltpu.roll(s, sh, axis=0)
            sh *= 2
        outs.append(s[16:] * (1.0 / w) - xg[16:])
    return outs


def _pack_bf16_pairs(h):
    m = h.shape[1] // 2
    lo = pltpu.bitcast(h[:, :m].astype(BF16).astype(F32), jnp.uint32)
    hi = pltpu.bitcast(h[:, m:].astype(BF16).astype(F32), jnp.uint32)
    return lax.shift_right_logical(lo, jnp.uint32(16)) | (hi & jnp.uint32(0xFFFF0000))


def _unpack_bf16_pairs(w):
    lo = pltpu.bitcast(lax.shift_left(w, jnp.uint32(16)), F32).astype(BF16)
    hi = pltpu.bitcast(w & jnp.uint32(0xFFFF0000), F32).astype(BF16)
    return lo, hi


def _prompt_kernel(x_ref, meta_ref, gattn_ref, win_ref, wpool_ref, pscale_ref, wout_ref, gffn_ref,
                   rwt_ref, rb_ref, sink_ref, tbl_ref, tail_h2_ref, tail_lgt_ref,
                   x1_ref, h2_ref, lgt_ref, kmeta_ref, vmeta_ref, ktail_ref, vtail_ref, ptail_ref,
                   k2buf, v2buf, km2, vm2, qbuf, obuf, pext):
    pid = pl.program_id(0)
    n_main = pl.num_programs(0) - 1
    refs = (x_ref, meta_ref, gattn_ref, win_ref, wpool_ref, pscale_ref, wout_ref, gffn_ref,
            rwt_ref, rb_ref, sink_ref, tbl_ref,
            x1_ref, h2_ref, lgt_ref, kmeta_ref, vmeta_ref, ktail_ref, vtail_ref, ptail_ref,
            k2buf, v2buf, km2, vm2, qbuf, obuf, pext)

    @pl.when(pid < n_main)
    def _():
        _prompt_block(*refs)

    @pl.when(pid == n_main)
    def _():
        h2_ref[0:tail_h2_ref.shape[0], :] = tail_h2_ref[...]
        lgt_ref[:, 0:tail_lgt_ref.shape[1]] = tail_lgt_ref[...]


def _prompt_block(x_ref, meta_ref, gattn_ref, win_ref, wpool_ref, pscale_ref, wout_ref, gffn_ref,
                  rwt_ref, rb_ref, sink_ref, tbl_ref,
                  x1_ref, h2_ref, lgt_ref, kmeta_ref, vmeta_ref, ktail_ref, vtail_ref, ptail_ref,
                  k2buf, v2buf, km2, vm2, qbuf, obuf, pext):
    tb = x_ref.shape[0]
    pid = pl.program_id(0)

    @pl.when(pid == 0)
    def _():
        hm = _rms(meta_ref[...], gattn_ref[...]).astype(BF16)
        km = jnp.dot(hm, win_ref[:, ATTN_WIDTH:ATTN_WIDTH + KV_WIDTH], preferred_element_type=F32)
        vm = jnp.dot(hm, win_ref[:, ATTN_WIDTH + KV_WIDTH:ATTN_WIDTH + 2 * KV_WIDTH],
                     preferred_element_type=F32)
        pm = jnp.dot(hm, win_ref[:, ATTN_WIDTH + 2 * KV_WIDTH:], preferred_element_type=F32)
        kmeta_ref[...] = km
        vmeta_ref[...] = vm
        zpad = jnp.zeros((META_PAD - N_META, LANES), F32)
        k0, k1 = _dup_halves(jnp.concatenate([km, zpad], axis=0))
        v0, v1 = _dup_halves(jnp.concatenate([vm, zpad], axis=0))
        km2[0] = k0.astype(BF16)
        km2[1] = k1.astype(BF16)
        vm2[0, :, 0:LANES] = v0.astype(BF16)
        vm2[1, :, 0:LANES] = v1.astype(BF16)
        vm2[:, :, LANES:] = jnp.ones((N_KV_HEADS, META_PAD, LANES), BF16)
        k2buf[:, 0:WINDOW, :] = jnp.zeros((2, WINDOW, LANES), BF16)
        v2buf[:, 0:WINDOW, 0:LANES] = jnp.zeros((2, WINDOW, LANES), BF16)
        v2buf[:, :, LANES:] = jnp.ones((N_KV_HEADS, WINDOW + tb, LANES), BF16)
        pext[0:16, :] = pm

    h = _rms(x_ref[...], gattn_ref[...]).astype(BF16)
    q = jnp.dot(h, win_ref[:, 0:ATTN_WIDTH], preferred_element_type=F32) * (HEAD_DIM ** -0.5)
    lane_t = lax.broadcasted_iota(jnp.int32, (tb, LANES), 1)
    for c in range(N_HEADS // 2):
        tile = q[:, c * LANES:(c + 1) * LANES]
        for a in range(2):
            keep = (lane_t < HEAD_DIM) if a == 0 else (lane_t >= HEAD_DIM)
            piece = jnp.where(keep, tile, 0.0).astype(BF16).reshape(tb // QSUB, QSUB, LANES)
            row = ((c % 2) * 2 + a) * QSUB
            qbuf[c // 2, :, row:row + QSUB, :] = piece
    k = jnp.dot(h, win_ref[:, ATTN_WIDTH:ATTN_WIDTH + KV_WIDTH], preferred_element_type=F32)
    v = jnp.dot(h, win_ref[:, ATTN_WIDTH + KV_WIDTH:ATTN_WIDTH + 2 * KV_WIDTH], preferred_element_type=F32)
    p = jnp.dot(h, win_ref[:, ATTN_WIDTH + 2 * KV_WIDTH:], preferred_element_type=F32)
    ktail_ref[...] = k[tb - WINDOW:]
    vtail_ref[...] = v[tb - WINDOW:]
    ptail_ref[...] = p[tb - 16:]
    k0, k1 = _dup_halves(k)
    v0, v1 = _dup_halves(v)
    k2buf[0, WINDOW:, :] = k0.astype(BF16)
    k2buf[1, WINDOW:, :] = k1.astype(BF16)
    v2buf[0, WINDOW:, 0:LANES] = v0.astype(BF16)
    v2buf[1, WINDOW:, 0:LANES] = v1.astype(BF16)
    pext[16:, :] = p

    lane_q = lax.broadcasted_iota(jnp.int32, (QSUB, LANES), 1)
    lo_q = lane_q < HEAD_DIM

    for u in range(tb // QSUB):
        r0 = u * QSUB
        sel = jnp.where(pid == 0, u + 1, 0) if u < WINDOW // QSUB else 0
        for g in range(N_KV_HEADS):
            qm = qbuf[g, u]
            kwin = jnp.concatenate([km2[g], k2buf[g, r0:r0 + KEYS_SUB, :]], axis=0)
            vwin = jnp.concatenate([vm2[g], v2buf[g, r0:r0 + KEYS_SUB, :]], axis=0)
            s = lax.dot_general(qm, kwin, (((1,), (1,)), ((), ())), preferred_element_type=F32)
            s = s + tbl_ref[sel, g]
            sink = sink_ref[g]
            m = jnp.maximum(jnp.max(s, axis=1, keepdims=True), sink)
            e = jnp.exp(s - m).astype(BF16)
            r = jnp.dot(e, vwin, preferred_element_type=F32)
            o = r[:, 0:LANES] / (r[:, LANES:] + jnp.exp(sink - m))
            o0 = jnp.where(lo_q, o[0:QSUB], o[QSUB:2 * QSUB])
            o1 = jnp.where(lo_q, o[2 * QSUB:3 * QSUB], o[3 * QSUB:])
            obuf[r0:r0 + QSUB, (2 * g) * LANES:(2 * g + 1) * LANES] = o0.astype(BF16)
            obuf[r0:r0 + QSUB, (2 * g + 1) * LANES:(2 * g + 2) * LANES] = o1.astype(BF16)

    pooled = _pool_means(pext, tb)
    for gi in range(len(POOL_WINDOWS)):
        y = jnp.dot(pooled[gi].astype(BF16), wpool_ref[gi], preferred_element_type=F32)
        y = y * pscale_ref[:, gi * POOL_GROUP_DIM:(gi + 1) * POOL_GROUP_DIM]
        obuf[:, ATTN_WIDTH + gi * POOL_GROUP_DIM:ATTN_WIDTH + (gi + 1) * POOL_GROUP_DIM] = y.astype(BF16)

    k2buf[:, 0:WINDOW, :] = k2buf[:, tb:tb + WINDOW, :]
    v2buf[:, 0:WINDOW, 0:LANES] = v2buf[:, tb:tb + WINDOW, 0:LANES]
    pext[0:16, :] = pext[tb:tb + 16, :]

    x1 = x_ref[...] + jnp.dot(obuf[...], wout_ref[...], preferred_element_type=F32)
    x1_ref[...] = x1
    h2 = _rms(x1, gffn_ref[...])
    h2_hi = h2.astype(BF16)
    h2_ref[...] = h2_hi
    lgt_ref[...] = _router_logits(rwt_ref, h2, h2_hi) + rb_ref[...]


def _attn_tables(sinks):
    i = np.arange(QSUB)[:, None]
    j = np.arange(NKEY)[None, :]
    jb = j - META_PAD
    rel = i + WINDOW - jb
    band_ok = (jb >= 0) & (rel >= 0) & (rel <= WINDOW)
    meta_ok = (j < N_META) & (i >= 0)
    slopes = np.exp2(-8.0 * np.arange(1, N_HEADS + 1) / N_HEADS)
    tbl = np.empty((3, N_KV_HEADS, GQA_GROUP * QSUB, NKEY), np.float32)
    for var in range(3):
        ok = band_ok if var == 0 else band_ok & (jb >= WINDOW - (var - 1) * QSUB)
        for g in range(N_KV_HEADS):
            for a in range(GQA_GROUP):
                hd = g * GQA_GROUP + a
                bias = np.where(ok, -slopes[hd] * rel, MASKED)
                bias = np.where(meta_ok, 0.0, bias)
                tbl[var, g, a * QSUB:(a + 1) * QSUB] = bias
    sink_col = jnp.repeat(sinks.astype(F32).reshape(N_KV_HEADS, GQA_GROUP, 1), QSUB, axis=2)
    return jnp.asarray(tbl), sink_col.reshape(N_KV_HEADS, GQA_GROUP * QSUB, 1)


def _prompt_mixer(x, meta, gattn, win, wpool, pscale, wout, gffn, rwt, rb, sinks, tail_h2, tail_lgt):
    seq = x.shape[0]
    tb = PROMPT_BLOCK
    n_tail = tail_h2.shape[0]
    assert seq % tb == 0 and tb % WINDOW == 0 and n_tail <= tb
    nblk = seq // tb
    n_tok = seq + n_tail
    tbl, sink_col = _attn_tables(sinks)
    full = lambda *shape: pl.BlockSpec(shape, lambda i: (0,) * len(shape))
    main = lambda i: (jnp.minimum(i, nblk - 1), 0)
    in_width = win.shape[1]
    return pl.pallas_call(
        _prompt_kernel,
        grid=(nblk + 1,),
        in_specs=[
            pl.BlockSpec((tb, D_MODEL), main),
            full(N_META, D_MODEL), full(1, D_MODEL), full(D_MODEL, in_width),
            full(len(POOL_WINDOWS), POOL_GROUP_DIM, POOL_GROUP_DIM), full(1, POOL_WIDTH),
            full(D_MODEL, D_MODEL), full(1, D_MODEL), full(2, N_EXPERTS, D_MODEL), full(N_EXPERTS, 1),
            full(N_KV_HEADS, GQA_GROUP * QSUB, 1), full(3, N_KV_HEADS, GQA_GROUP * QSUB, NKEY),
            full(n_tail, D_MODEL), full(N_EXPERTS, n_tail),
        ],
        out_specs=[
            pl.BlockSpec((tb, D_MODEL), main),
            pl.BlockSpec((tb, D_MODEL), lambda i: (i, 0)),
            pl.BlockSpec((N_EXPERTS, tb), lambda i: (0, i)),
            full(N_META, KV_WIDTH), full(N_META, KV_WIDTH),
            full(WINDOW, KV_WIDTH), full(WINDOW, KV_WIDTH), full(16, POOL_WIDTH),
        ],
        out_shape=[
            jax.ShapeDtypeStruct((seq, D_MODEL), F32),
            jax.ShapeDtypeStruct((n_tok, D_MODEL), BF16),
            jax.ShapeDtypeStruct((N_EXPERTS, n_tok), F32),
            jax.ShapeDtypeStruct((N_META, KV_WIDTH), F32),
            jax.ShapeDtypeStruct((N_META, KV_WIDTH), F32),
            jax.ShapeDtypeStruct((WINDOW, KV_WIDTH), F32),
            jax.ShapeDtypeStruct((WINDOW, KV_WIDTH), F32),
            jax.ShapeDtypeStruct((16, POOL_WIDTH), F32),
        ],
        scratch_shapes=[
            pltpu.VMEM((N_KV_HEADS, WINDOW + tb, LANES), BF16),
            pltpu.VMEM((N_KV_HEADS, WINDOW + tb, 2 * LANES), BF16),
            pltpu.VMEM((N_KV_HEADS, META_PAD, LANES), BF16),
            pltpu.VMEM((N_KV_HEADS, META_PAD, 2 * LANES), BF16),
            pltpu.VMEM((N_KV_HEADS, tb // QSUB, GQA_GROUP * QSUB, LANES), BF16),
            pltpu.VMEM((tb, D_MODEL), BF16),
            pltpu.VMEM((16 + tb, POOL_WIDTH), F32),
        ],
        compiler_params=pltpu.CompilerParams(dimension_semantics=("arbitrary",),
                                             vmem_limit_bytes=VMEM_LIMIT),
        name="prompt_mixer",
    )(x, meta, gattn, win, wpool, pscale, wout, gffn, rwt, rb, sink_col, tbl, tail_h2, tail_lgt)


def _sample_kernel(x_ref, ck_ref, cv_ref, sp_ref, gattn_ref, win_ref, wpool_ref, pscale_ref, wout_ref,
                   gffn_ref, rwt_ref, rb_ref, sinkc_ref, bias_ref,
                   x1_ref, h2_ref, lgt_ref, knew_ref, vnew_ref, pnew_ref,
                   qm_buf, r_buf, obuf):
    nb = x_ref.shape[0]
    x = x_ref[...]
    h = _rms(x, gattn_ref[...]).astype(BF16)
    q = jnp.dot(h, win_ref[:, 0:ATTN_WIDTH], preferred_element_type=F32) * (HEAD_DIM ** -0.5)
    k = jnp.dot(h, win_ref[:, ATTN_WIDTH:ATTN_WIDTH + KV_WIDTH], preferred_element_type=F32)
    v = jnp.dot(h, win_ref[:, ATTN_WIDTH + KV_WIDTH:ATTN_WIDTH + 2 * KV_WIDTH], preferred_element_type=F32)
    p = jnp.dot(h, win_ref[:, ATTN_WIDTH + 2 * KV_WIDTH:], preferred_element_type=F32)
    knew_ref[...] = k
    vnew_ref[...] = v
    pnew_ref[...] = p

    lane = lax.broadcasted_iota(jnp.int32, (nb, LANES), 1)
    lo = lane < HEAD_DIM
    for hd in range(N_HEADS):
        tile = q[:, (hd // 2) * LANES:(hd // 2 + 1) * LANES]
        if (hd % 2) != (hd // GQA_GROUP):
            tile = pltpu.roll(tile, HEAD_DIM, axis=1)
        keep_lo = (hd // GQA_GROUP) == 0
        qm_buf[:, hd, :] = jnp.where(lo if keep_lo else jnp.logical_not(lo), tile, 0.0)

    def per_batch(b):
        qm = qm_buf[b]
        kb = ck_ref[b].astype(BF16)
        vb = cv_ref[b].astype(BF16)
        s = lax.dot_general(qm.astype(BF16), kb, (((1,), (1,)), ((), ())), preferred_element_type=F32)
        s = s + bias_ref[...]
        kn = knew_ref[pl.ds(b, 1), :]
        vn = vnew_ref[pl.ds(b, 1), :]
        s_self = jnp.sum(qm * kn, axis=1, keepdims=True)
        sink = sinkc_ref[...]
        m = jnp.maximum(jnp.maximum(jnp.max(s, axis=1, keepdims=True), s_self), sink)
        e = jnp.exp(s - m)
        e_self = jnp.exp(s_self - m)
        den = jnp.sum(e, axis=1, keepdims=True) + e_self + jnp.exp(sink - m)
        r = jnp.dot(e.astype(BF16), vb, preferred_element_type=F32)
        r = r + e_self * vn
        r_buf[b] = r / den

    unroll = 8
    assert nb % unroll == 0

    def batch_group(gidx, carry):
        for j in range(unroll):
            per_batch(gidx * unroll + j)
        return carry

    lax.fori_loop(0, nb // unroll, batch_group, 0)

    for c in range(N_HEADS // 2):
        halves = []
        for a in range(2):
            hd = 2 * c + a
            t = r_buf[:, hd, :]
            if (hd // GQA_GROUP) != a:
                t = pltpu.roll(t, HEAD_DIM, axis=1)
            halves.append(t)
        obuf[:, c * LANES:(c + 1) * LANES] = jnp.where(lo, halves[0], halves[1]).astype(BF16)

    for gi, w in enumerate(POOL_WINDOWS):
        cols = slice(gi * POOL_GROUP_DIM, (gi + 1) * POOL_GROUP_DIM)
        pg = p[:, cols]
        acc = pg
        for d in range(1, w):
            acc = acc + sp_ref[:, POOL_STATE - d, cols]
        pooled = acc * (1.0 / w) - pg
        y = jnp.dot(pooled.astype(BF16), wpool_ref[gi], preferred_element_type=F32) * pscale_ref[:, cols]
        obuf[:, ATTN_WIDTH + gi * POOL_GROUP_DIM:ATTN_WIDTH + (gi + 1) * POOL_GROUP_DIM] = y.astype(BF16)

    x1 = x + jnp.dot(obuf[...], wout_ref[...], preferred_element_type=F32)
    x1_ref[...] = x1
    h2 = _rms(x1, gffn_ref[...])
    h2_hi = h2.astype(BF16)
    h2_ref[...] = h2_hi
    lgt_ref[...] = _router_logits(rwt_ref, h2, h2_hi) + rb_ref[...]


def _sample_mixer(x, ck, cv, sp, gattn, win, wpool, pscale, wout, gffn, rwt, rb, sinks):
    nb = x.shape[0]
    rows = ck.shape[1]
    slopes = np.exp2(-8.0 * np.arange(1, N_HEADS + 1) / N_HEADS)
    dist = np.concatenate([np.zeros(N_META), WINDOW - np.arange(WINDOW)])
    bias = jnp.asarray((-slopes[:, None] * dist[None, :]).astype(np.float32))
    vm = pl.BlockSpec(memory_space=pltpu.VMEM)
    return pl.pallas_call(
        _sample_kernel,
        in_specs=[vm] * 14,
        out_specs=[vm] * 6,
        out_shape=[
            jax.ShapeDtypeStruct((nb, D_MODEL), F32),
            jax.ShapeDtypeStruct((nb, D_MODEL), BF16),
            jax.ShapeDtypeStruct((N_EXPERTS, nb), F32),
            jax.ShapeDtypeStruct((nb, KV_WIDTH), F32),
            jax.ShapeDtypeStruct((nb, KV_WIDTH), F32),
            jax.ShapeDtypeStruct((nb, POOL_WIDTH), F32),
        ],
        scratch_shapes=[
            pltpu.VMEM((nb, N_HEADS, LANES), F32),
            pltpu.VMEM((nb, N_HEADS, LANES), F32),
            pltpu.VMEM((nb, D_MODEL), BF16),
        ],
        compiler_params=pltpu.CompilerParams(vmem_limit_bytes=VMEM_LIMIT),
        name="sample_mixer",
    )(x, ck, cv, sp, gattn, win, wpool, pscale, wout, gffn, rwt, rb,
      sinks.astype(F32).reshape(N_HEADS, 1), bias)


def _router_kernel(lg_ref, tri_ref, low_ref, gate_ref, col_ref, lrank_ref, lpos_ref,
                   tcar_ref, cnt_ref, carry):
    tr = lg_ref.shape[1]

    @pl.when(pl.program_id(0) == 0)
    def _():
        carry[...] = jnp.zeros_like(carry)

    work = lg_ref[...]
    eio = lax.broadcasted_iota(jnp.int32, work.shape, 0).astype(F32)
    sels, vals, idxs = [], [], []
    for _k in range(TOP_K):
        mx = jnp.max(work, axis=0, keepdims=True)
        idx = jnp.min(jnp.where(work == mx, eio, float(N_EXPERTS)), axis=0, keepdims=True)
        sel = eio == idx
        sels.append(sel)
        vals.append(mx)
        idxs.append(idx)
        work = jnp.where(sel, -jnp.inf, work)
    exps = [jnp.exp(vk - vals[0]) for vk in vals]
    tot = exps[0] + exps[1] + exps[2] + exps[3]
    onehot = jnp.zeros(work.shape, F32)
    for sel in sels:
        onehot = onehot + sel.astype(F32)
    before = jnp.dot(onehot.astype(BF16), tri_ref[...], preferred_element_type=F32) + carry[...]
    for kk in range(TOP_K):
        gate_ref[pl.ds(kk, 1), :] = exps[kk] / tot
    for j in range(tr // COMBINE_TILE):
        cols = slice(j * COMBINE_TILE, (j + 1) * COMBINE_TILE)
        tc = before[:, j * COMBINE_TILE:j * COMBINE_TILE + 1]
        tcar_ref[j] = tc.astype(jnp.int32)
        slack = tc - WIN_ALIGN * jnp.floor(tc * (1.0 / WIN_ALIGN))
        local = before[:, cols] - tc
        for kk in range(TOP_K):
            selk = sels[kk][:, cols]
            lr = jnp.sum(jnp.where(selk, local, 0.0), axis=0, keepdims=True)
            sl = jnp.sum(jnp.where(selk, slack, 0.0), axis=0, keepdims=True)
            lrank_ref[pl.ds(kk, 1), cols] = lr.astype(jnp.int32)
            col_ref[pl.ds(kk, 1), cols] = (idxs[kk][:, cols] * float(WIN_ROWS) + sl + lr).astype(jnp.int32)
    for j in range(tr // DISPATCH_TILE):
        cols = slice(j * DISPATCH_TILE, (j + 1) * DISPATCH_TILE)
        local = before[:, cols] - before[:, j * DISPATCH_TILE:j * DISPATCH_TILE + 1]
        tile_cnt = jnp.broadcast_to(jnp.sum(onehot[:, cols], axis=1, keepdims=True), local.shape)
        cnt_hi = jnp.floor(tile_cnt * (1.0 / 256.0))
        cnt_lo = tile_cnt - 256.0 * cnt_hi
        run_start = (256.0 * jnp.dot(low_ref[...], cnt_hi.astype(BF16), preferred_element_type=F32)
                     + jnp.dot(low_ref[...], cnt_lo.astype(BF16), preferred_element_type=F32))
        for kk in range(TOP_K):
            lp = jnp.sum(jnp.where(sels[kk][:, cols], run_start + local, 0.0), axis=0, keepdims=True)
            lpos_ref[pl.ds(kk, 1), cols] = lp.astype(jnp.int32)
    carry[...] = carry[...] + jnp.sum(onehot, axis=1, keepdims=True)
    cnt_ref[...] = carry[...].astype(jnp.int32)


def _router(logits_t):
    n = logits_t.shape[1]
    tr = ROUTE_BLOCK
    assert n % tr == 0
    tri = jnp.asarray(np.triu(np.ones((tr, tr), np.float32), k=1), BF16)
    low = jnp.asarray(np.tril(np.ones((N_EXPERTS, N_EXPERTS), np.float32), k=-1), BF16)
    per_tok = pl.BlockSpec((TOP_K, tr), lambda i: (0, i))
    return pl.pallas_call(
        _router_kernel,
        grid=(n // tr,),
        in_specs=[pl.BlockSpec((N_EXPERTS, tr), lambda i: (0, i)),
                  pl.BlockSpec((tr, tr), lambda i: (0, 0)),
                  pl.BlockSpec((N_EXPERTS, N_EXPERTS), lambda i: (0, 0))],
        out_specs=[per_tok, per_tok, per_tok, per_tok,
                   pl.BlockSpec((tr // COMBINE_TILE, N_EXPERTS, 1), lambda i: (i, 0, 0)),
                   pl.BlockSpec((N_EXPERTS, 1), lambda i: (0, 0))],
        out_shape=[jax.ShapeDtypeStruct((TOP_K, n), F32),
                   jax.ShapeDtypeStruct((TOP_K, n), jnp.int32),
                   jax.ShapeDtypeStruct((TOP_K, n), jnp.int32),
                   jax.ShapeDtypeStruct((TOP_K, n), jnp.int32),
                   jax.ShapeDtypeStruct((n // COMBINE_TILE, N_EXPERTS, 1), jnp.int32),
                   jax.ShapeDtypeStruct((N_EXPERTS, 1), jnp.int32)],
        scratch_shapes=[pltpu.VMEM((N_EXPERTS, 1), F32)],
        compiler_params=pltpu.CompilerParams(dimension_semantics=("arbitrary",)),
        name="router",
    )(logits_t, tri, low)


def _dispatch_kernel(lstart_ref, cnt_ref, dst_ref, h2_ref, lpos_ref, xs_hbm, stg0, stg1, sem):
    i = pl.program_id(0)
    n_tiles = pl.num_programs(0)
    dt = h2_ref.shape[0]
    rows = dt * TOP_K
    slot = i % 2
    stgs = (stg0, stg1)
    pieces = [p for p in (256, 128, 64, 32, 16, 8, 4, 2, 1) if p <= dt]
    assert dt < 512

    def drain(s):
        pltpu.make_async_copy(stgs[s], xs_hbm.at[pl.ds(0, rows)], sem.at[s]).wait()

    def issue_runs(tile, live, s):
        for e in range(N_EXPERTS):
            n = jnp.where(live, cnt_ref[tile * N_EXPERTS + e], 0)
            src0 = lstart_ref[tile * N_EXPERTS + e]
            dst0 = dst_ref[tile * N_EXPERTS + e]
            for piece in pieces:
                off = n & ~jnp.int32(2 * piece - 1)

                @pl.when((n & piece) != 0)
                def _(off=off, piece=piece, src0=src0, dst0=dst0):
                    pltpu.make_async_copy(stgs[s].at[pl.ds(src0 + off, piece)],
                                          xs_hbm.at[pl.ds(dst0 + off, piece)], sem.at[s]).start()

    def sort_tile(s):
        rid = lax.broadcasted_iota(jnp.int32, (rows, dt), 0)
        hit = rid == lpos_ref[0:1, :]
        for kk in range(1, TOP_K):
            hit = jnp.logical_or(hit, rid == lpos_ref[kk:kk + 1, :])
        perm = jnp.where(hit, 1.0, 0.0).astype(BF16)
        srt = jnp.dot(perm, h2_ref[...], preferred_element_type=F32)
        packed = _pack_bf16_pairs(srt)
        for c in range(PACK_CHUNKS):
            stgs[s][:, c, :] = packed[:, c * LANES:(c + 1) * LANES]

    for s in range(2):
        @pl.when(slot == s)
        def _(s=s):
            @pl.when(i >= 2)
            def _():
                drain(s)

            issue_runs(jnp.maximum(i - 1, 0), i >= 1, 1 - s)
            sort_tile(s)

            @pl.when(i == n_tiles - 1)
            def _():
                issue_runs(i, True, s)
                drain(s)

                @pl.when(n_tiles >= 2)
                def _():
                    drain(1 - s)


def _dispatch(h2, lpos, lstart, cnt, dst, cap):
    n_tok = h2.shape[0]
    dt = DISPATCH_TILE
    assert n_tok % dt == 0
    grid_spec = pltpu.PrefetchScalarGridSpec(
        num_scalar_prefetch=3,
        grid=(n_tok // dt,),
        in_specs=[pl.BlockSpec((dt, D_MODEL), lambda i, a, b, c: (i, 0)),
                  pl.BlockSpec((TOP_K, dt), lambda i, a, b, c: (0, i)),
                  ],
        out_specs=pl.BlockSpec(memory_space=pl.ANY),
        scratch_shapes=[pltpu.VMEM((dt * TOP_K, PACK_CHUNKS, LANES), jnp.uint32),
                        pltpu.VMEM((dt * TOP_K, PACK_CHUNKS, LANES), jnp.uint32),
                        pltpu.SemaphoreType.DMA((2,))],
    )
    return pl.pallas_call(
        _dispatch_kernel,
        grid_spec=grid_spec,
        out_shape=jax.ShapeDtypeStruct((cap, PACK_CHUNKS, LANES), jnp.uint32),
        compiler_params=pltpu.CompilerParams(dimension_semantics=("arbitrary",),
                                             vmem_limit_bytes=VMEM_LIMIT),
        name="dispatch",
    )(lstart, cnt, dst, h2, lpos)


def _expert_kernel(n_xblocks, bexp_ref, nvalid_ref, epos_ref, elist_ref,
                   x_hbm, wgu_hbm, bgu_ref, wd_hbm, bd_ref, y_ref,
                   wgu_f32, wd_f32, wgu_bf, wd_bf, xbuf, xsem, wsem):
    i = pl.program_id(0)
    rb = y_ref.shape[0]
    nvalid = nvalid_ref[i]
    pos = epos_ref[i]
    fresh = jnp.logical_or(i == 0, pos != epos_ref[jnp.maximum(i - 1, 0)])
    slot = i % 2

    def x_copies(blk, s):
        return [pltpu.make_async_copy(x_hbm.at[pl.ds(blk * rb, rb), c, :],
                                      xbuf.at[s, :, pl.ds(c * LANES, LANES)], xsem.at[s])
                for c in range(PACK_CHUNKS)]

    def w_copies(p):
        e = elist_ref[p]
        s = p % 2
        return [pltpu.make_async_copy(wgu_hbm.at[e], wgu_f32.at[s], wsem.at[s, 0]),
                pltpu.make_async_copy(wd_hbm.at[e], wd_f32.at[s], wsem.at[s, 1])]

    @pl.when(i == 0)
    def _():
        for cp in x_copies(0, 0):
            cp.start()

        @pl.when(nvalid > 0)
        def _():
            for cp in w_copies(0):
                cp.start()

    @pl.when(i + 1 < n_xblocks)
    def _():
        for cp in x_copies(i + 1, 1 - slot):
            cp.start()

    @pl.when(jnp.logical_and(fresh, nvalid > 0))
    def _():
        @pl.when(elist_ref[pos + 1] >= 0)
        def _():
            for cp in w_copies(pos + 1):
                cp.start()

        for cp in w_copies(pos):
            cp.wait()
        ws = pos % 2
        chunk = 32

        def cast_gu(r, c):
            r0 = pl.multiple_of(r * chunk, chunk)
            wgu_bf[pl.ds(r0, chunk), :] = wgu_f32[ws, pl.ds(r0, chunk), :].astype(BF16)
            return c

        def cast_d(r, c):
            r0 = pl.multiple_of(r * chunk, chunk)
            wd_bf[pl.ds(r0, chunk), :] = wd_f32[ws, pl.ds(r0, chunk), :].astype(BF16)
            return c

        lax.fori_loop(0, D_MODEL // chunk, cast_gu, 0)
        lax.fori_loop(0, D_EXPERT // chunk, cast_d, 0)

    @pl.when(i < n_xblocks)
    def _():
        for cp in x_copies(i, slot):
            cp.wait()

    def ffn(rows):
        xw = xbuf[slot, 0:rows, :]
        xw = jnp.where(lax.broadcasted_iota(jnp.int32, xw.shape, 0) < nvalid, xw, jnp.uint32(0))
        xlo, xhi = _unpack_bf16_pairs(xw)
        half = D_MODEL // 2
        g = (jnp.dot(xlo, wgu_bf[0:half, 0:D_EXPERT], preferred_element_type=F32)
             + jnp.dot(xhi, wgu_bf[half:, 0:D_EXPERT], preferred_element_type=F32)
             + bgu_ref[0, :, 0:D_EXPERT])
        u = (jnp.dot(xlo, wgu_bf[0:half, D_EXPERT:], preferred_element_type=F32)
             + jnp.dot(xhi, wgu_bf[half:, D_EXPERT:], preferred_element_type=F32)
             + bgu_ref[0, :, D_EXPERT:])
        g = jnp.minimum(g, SWIGLU_LIMIT)
        u = jnp.clip(u, -SWIGLU_LIMIT, SWIGLU_LIMIT)
        act = g * (1.0 / (1.0 + jnp.exp(-SWIGLU_ALPHA * g))) * (u + 1.0)
        y = jnp.dot(act.astype(BF16), wd_bf[...], preferred_element_type=F32) + bd_ref[0]
        row = lax.broadcasted_iota(jnp.int32, y.shape, 0)
        y_ref[0:rows, :] = jnp.where(row < nvalid, y, 0.0).astype(BF16)
        if rows < rb:
            y_ref[rows:, :] = jnp.zeros((rb - rows, D_MODEL), BF16)

    @pl.when(nvalid > rb // 2)
    def _():
        ffn(rb)

    @pl.when(jnp.logical_and(nvalid > 0, nvalid <= rb // 2))
    def _():
        ffn(rb // 2)

    @pl.when(nvalid == 0)
    def _():
        y_ref[...] = jnp.zeros_like(y_ref)


def _experts(xs, block_expert, nvalid, block_pos, expert_list, wgu, bgu, wd, bd):
    rb = EXPERT_ROWS
    n_xblocks = xs.shape[0] // rb
    nblk = n_xblocks + 1
    any_space = pl.BlockSpec(memory_space=pl.ANY)
    grid_spec = pltpu.PrefetchScalarGridSpec(
        num_scalar_prefetch=4,
        grid=(nblk,),
        in_specs=[
            any_space,
            any_space,
            pl.BlockSpec((1, 1, 2 * D_EXPERT), lambda i, be, nu, ep, el: (be[i], 0, 0)),
            any_space,
            pl.BlockSpec((1, 1, D_MODEL), lambda i, be, nu, ep, el: (be[i], 0, 0)),
        ],
        out_specs=pl.BlockSpec((rb, D_MODEL), lambda i, be, nu, ep, el: (i, 0)),
        scratch_shapes=[pltpu.VMEM((2, D_MODEL, 2 * D_EXPERT), F32),
                        pltpu.VMEM((2, D_EXPERT, D_MODEL), F32),
                        pltpu.VMEM((D_MODEL, 2 * D_EXPERT), BF16),
                        pltpu.VMEM((D_EXPERT, D_MODEL), BF16),
                        pltpu.VMEM((2, rb, D_MODEL // 2), jnp.uint32),
                        pltpu.SemaphoreType.DMA((2,)),
                        pltpu.SemaphoreType.DMA((2, 2))],
    )
    return pl.pallas_call(
        functools.partial(_expert_kernel, n_xblocks),
        grid_spec=grid_spec,
        out_shape=jax.ShapeDtypeStruct((nblk * rb, D_MODEL), BF16),
        compiler_params=pltpu.CompilerParams(dimension_semantics=("arbitrary",),
                                             vmem_limit_bytes=VMEM_LIMIT),
        name="experts",
    )(block_expert, nvalid, block_pos, expert_list, xs, wgu, bgu.reshape(N_EXPERTS, 1, 2 * D_EXPERT), wd,
      bd.reshape(N_EXPERTS, 1, D_MODEL))


def _combine_kernel(n_prompt_tiles, offa_ref, nchunk_ref,
                    x1p_ref, x1s_ref, info_ref, info_next_ref, gfin_ref, yb_hbm,
                    outp_ref, outs_ref, ybuf, gbuf0, gbuf1, acc_ref, sem):
    i = pl.program_id(0)
    n_tiles = pl.num_programs(0)

    def window_copy(tile, chunk, e, slot):
        base = pl.multiple_of(offa_ref[tile * N_EXPERTS + e] + chunk * WIN, WIN_ALIGN)
        return pltpu.make_async_copy(yb_hbm.at[pl.ds(base, WIN_ROWS), :],
                                     ybuf.at[slot, pl.ds(e * WIN_ROWS, WIN_ROWS), :],
                                     sem.at[slot])

    def start_windows(tile, chunk, slot):
        for e in range(N_EXPERTS):
            window_copy(tile, chunk, e, slot).start()

    def wait_windows(slot):
        pltpu.make_async_copy(yb_hbm.at[pl.ds(0, N_EXPERTS * WIN_ROWS), :], ybuf.at[slot],
                              sem.at[slot]).wait()

    slot = i % 2

    @pl.when(i == 0)
    def _():
        start_windows(0, 0, 0)

    lane = lax.broadcasted_iota(jnp.int32, (COMBINE_TILE, N_EXPERTS * WIN_ROWS), 1)

    def gate_matrix(ref, chunk):
        g = jnp.zeros(lane.shape, F32)
        for kk in range(TOP_K):
            lr = ref[:, TOP_K + kk:TOP_K + kk + 1]
            in_chunk = jnp.logical_and(lr >= chunk * WIN, lr < chunk * WIN + WIN)
            colk = jnp.where(in_chunk, ref[:, kk:kk + 1] - chunk * WIN, -1.0).astype(jnp.int32)
            g = jnp.where(lane == colk, ref[:, 2 * TOP_K + kk:2 * TOP_K + kk + 1], g)
        hi = g.astype(BF16)
        lo = (g - hi.astype(F32)).astype(BF16)
        return jnp.concatenate([hi, lo], axis=0)

    def moe_rows(gm, buf):
        r = jnp.dot(gm, ybuf[buf], preferred_element_type=F32)
        return r[0:COMBINE_TILE] + r[COMBINE_TILE:]

    @pl.when(i == 0)
    def _():
        gbuf0[...] = gate_matrix(info_ref, 0)

    def main(s):
        start_windows(jnp.minimum(i + 1, n_tiles - 1), 0, 1 - s)
        wait_windows(s)
        g_cur, g_nxt = (gbuf0, gbuf1) if s == 0 else (gbuf1, gbuf0)
        acc_ref[...] = moe_rows(g_cur[...], s)
        g_nxt[...] = gate_matrix(info_next_ref, 0)

    for s in range(2):
        @pl.when(slot == s)
        def _(s=s):
            main(s)

    @pl.when(i == n_tiles - 1)
    def _():
        wait_windows(1 - slot)

    def extra_chunk(j, c):
        start_windows(i, j, 2)
        wait_windows(2)
        acc_ref[...] += moe_rows(gate_matrix(info_ref, j), 2)
        return c

    lax.fori_loop(1, nchunk_ref[i], extra_chunk, 0)

    @pl.when(i < n_prompt_tiles)
    def _():
        outp_ref[...] = _rms(x1p_ref[...] + acc_ref[...], gfin_ref[...])

    @pl.when(i >= n_prompt_tiles)
    def _():
        outs_ref[...] = _rms(x1s_ref[...] + acc_ref[...], gfin_ref[...])


def _combine(x1_p, x1_s, tok_info, gfin, yb, offa, nchunk):
    ct = COMBINE_TILE
    n_p, n_s = x1_p.shape[0] // ct, x1_s.shape[0] // ct
    assert x1_p.shape[0] % ct == 0 and x1_s.shape[0] % ct == 0 and n_s >= 1
    n_info = tok_info.shape[1]
    grid_spec = pltpu.PrefetchScalarGridSpec(
        num_scalar_prefetch=2,
        grid=(n_p + n_s,),
        in_specs=[
            pl.BlockSpec((ct, D_MODEL), lambda i, o, c: (jnp.minimum(i, n_p - 1), 0)),
            pl.BlockSpec((ct, D_MODEL), lambda i, o, c: (jnp.maximum(i - n_p, 0), 0)),
            pl.BlockSpec((ct, n_info), lambda i, o, c: (i, 0)),
            pl.BlockSpec((ct, n_info), lambda i, o, c: (jnp.minimum(i + 1, n_p + n_s - 1), 0)),
            pl.BlockSpec((1, D_MODEL), lambda i, o, c: (0, 0)),
            pl.BlockSpec(memory_space=pl.ANY),
        ],
        out_specs=[
            pl.BlockSpec((ct, D_MODEL), lambda i, o, c: (jnp.minimum(i, n_p - 1), 0)),
            pl.BlockSpec((ct, D_MODEL), lambda i, o, c: (jnp.maximum(i - n_p, 0), 0)),
        ],
        scratch_shapes=[pltpu.VMEM((3, N_EXPERTS * WIN_ROWS, D_MODEL), BF16),
                        pltpu.VMEM((2 * ct, N_EXPERTS * WIN_ROWS), BF16),
                        pltpu.VMEM((2 * ct, N_EXPERTS * WIN_ROWS), BF16),
                        pltpu.VMEM((ct, D_MODEL), F32),
                        pltpu.SemaphoreType.DMA((3,))],
    )
    return pl.pallas_call(
        functools.partial(_combine_kernel, n_p),
        grid_spec=grid_spec,
        out_shape=[jax.ShapeDtypeStruct(x1_p.shape, F32), jax.ShapeDtypeStruct(x1_s.shape, F32)],
        compiler_params=pltpu.CompilerParams(dimension_semantics=("arbitrary",),
                                             vmem_limit_bytes=VMEM_LIMIT),
        name="combine",
    )(offa, nchunk, x1_p, x1_s, tok_info, tok_info, gfin, yb)


def kernel(x_prompt, x_sample, cache_k, cache_v, state_pool, meta_tokens, norm_attn, w_in, attn_sinks,
           w_pool, pool_scale, w_out, norm_ffn, router_w, router_b, w_gate_up, b_gate_up, w_down, b_down,
           norm_final):
    assert w_in.shape[0] == 1, "single-layer trunk"
    bsz, seq, _ = x_prompt.shape
    assert bsz == 1
    nb = x_sample.shape[0]
    n_tok = seq + nb
    gattn = norm_attn[0].reshape(1, D_MODEL)
    gffn = norm_ffn[0].reshape(1, D_MODEL)
    win = w_in[0].astype(BF16)
    wpool = w_pool[0].astype(BF16)
    wout = w_out[0].astype(BF16)
    pscale = pool_scale[0].reshape(1, POOL_WIDTH)
    rw_t = router_w[0].T
    rw_hi = rw_t.astype(BF16)
    rwt = jnp.stack([rw_hi, (rw_t - rw_hi.astype(F32)).astype(BF16)])
    rb = router_b[0].reshape(N_EXPERTS, 1)
    sinks = attn_sinks[0]

    ck = cache_k[0].reshape(nb, N_META + WINDOW, KV_WIDTH)
    cv = cache_v[0].reshape(nb, N_META + WINDOW, KV_WIDTH)
    (x1_s, h2_s, lgt_s, knew, vnew, pnew) = _sample_mixer(
        x_sample[:, 0], ck, cv, state_pool[0], gattn, win, wpool, pscale, wout, gffn, rwt, rb, sinks)
    (x1_p, h2_all, lgt_all, kmeta, vmeta, ktail, vtail, ptail) = _prompt_mixer(
        x_prompt[0], meta_tokens, gattn, win, wpool, pscale, wout, gffn, rwt, rb, sinks, h2_s, lgt_s)

    gates, col, lrank, lpos, tcar, counts = _router(lgt_all)
    counts = counts[:, 0]
    tcar = tcar[:, :, 0]
    rbk = EXPERT_ROWS
    eids = jnp.arange(N_EXPERTS, dtype=jnp.int32)
    earlier = eids[None, :] < eids[:, None]
    excl_sum = lambda a: jnp.sum(jnp.where(earlier, a[..., None, :], 0), axis=-1)
    padded = (counts + rbk - 1) // rbk * rbk
    pad_start = excl_sum(padded).astype(jnp.int32)
    pad_end = pad_start + padded
    nblk = -(-(n_tok * TOP_K) // rbk) + N_EXPERTS
    cap = nblk * rbk
    block_start = jnp.arange(nblk + 1, dtype=jnp.int32) * rbk
    owns = (pad_start[None, :] <= block_start[:, None]) & (block_start[:, None] < pad_end[None, :])
    nvalid = jnp.sum(jnp.where(owns, jnp.clip(counts[None, :] - (block_start[:, None] - pad_start[None, :]),
                                              0, rbk), 0), axis=1).astype(jnp.int32)
    has_rows = counts > 0
    last_e = jnp.max(jnp.where(has_rows, eids, 0))
    block_expert = jnp.where(jnp.any(owns, axis=1), jnp.sum(jnp.where(owns, eids[None, :], 0), axis=1),
                             last_e).astype(jnp.int32)

    run_len = jnp.concatenate([tcar[1:], counts[None, :]], axis=0) - tcar
    dcar = tcar[::DISPATCH_TILE // COMBINE_TILE]
    drun_len = jnp.concatenate([dcar[1:], counts[None, :]], axis=0) - dcar
    flat = lambda a: a.astype(jnp.int32).reshape(-1)
    xs = _dispatch(h2_all, lpos, flat(excl_sum(drun_len)), flat(drun_len), flat(pad_start[None, :] + dcar),
                   cap)
    expert_pos = excl_sum(has_rows.astype(jnp.int32))
    at_pos = has_rows[None, :] & (expert_pos[None, :] == jnp.arange(N_EXPERTS + 1, dtype=jnp.int32)[:, None])
    expert_list = jnp.where(jnp.any(at_pos, axis=1), jnp.sum(jnp.where(at_pos, eids[None, :], 0), axis=1),
                            -1).astype(jnp.int32)
    block_pos = jnp.sum(jnp.where(block_expert[:, None] == eids[None, :], expert_pos[None, :], 0),
                        axis=1).astype(jnp.int32)
    yb = _experts(xs, block_expert, nvalid, block_pos, expert_list,
                  w_gate_up[0], b_gate_up[0], w_down[0], b_down[0])

    offa = (pad_start[None, :] + (tcar - tcar % WIN_ALIGN)).astype(jnp.int32).reshape(-1)
    nchunk = jnp.maximum(jnp.max((run_len + WIN - 1) // WIN, axis=1), 1).astype(jnp.int32)
    gfin = norm_final.reshape(1, D_MODEL)
    tok_info = jnp.concatenate([col.astype(F32), lrank.astype(F32), gates], axis=0).T
    y_prompt, y_sample = _combine(x1_p, x1_s, tok_info, gfin, yb, offa, nchunk)

    kv_shape = (1, 1, N_META + WINDOW, N_KV_HEADS, HEAD_DIM)
    new_k_p = jnp.concatenate([kmeta, ktail], axis=0).reshape(kv_shape)
    new_v_p = jnp.concatenate([vmeta, vtail], axis=0).reshape(kv_shape)
    new_pool_p = ptail[16 - POOL_STATE:].reshape(1, 1, POOL_STATE, POOL_WIDTH)
    new_k_s = jnp.concatenate([ck[:, :N_META], ck[:, N_META + 1:], knew[:, None]], axis=1).reshape(
        (1, nb, N_META + WINDOW, N_KV_HEADS, HEAD_DIM))
    new_v_s = jnp.concatenate([cv[:, :N_META], cv[:, N_META + 1:], vnew[:, None]], axis=1).reshape(
        (1, nb, N_META + WINDOW, N_KV_HEADS, HEAD_DIM))
    new_pool_s = jnp.concatenate([state_pool[0][:, 1:], pnew[:, None]], axis=1)[None]
    return (y_prompt[None], y_sample[:, None], new_k_p, new_v_p, new_pool_p, new_k_s, new_v_s, new_pool_s)
```

```python
import functools

import jax
import jax.numpy as jnp
import numpy as np
from jax import lax
from jax.experimental import pallas as pl
from jax.experimental.pallas import tpu as pltpu

F32 = jnp.float32
BF16 = jnp.bfloat16

D_MODEL = 1024
N_META = 16
N_HEADS = 8
HEAD_DIM = 64
N_KV_HEADS = 2
GQA_GROUP = N_HEADS // N_KV_HEADS
ATTN_WIDTH = N_HEADS * HEAD_DIM
KV_WIDTH = N_KV_HEADS * HEAD_DIM
WINDOW = 128
POOL_WIDTH = D_MODEL - ATTN_WIDTH
POOL_WINDOWS = (2, 4, 8, 16)
POOL_GROUP_DIM = POOL_WIDTH // len(POOL_WINDOWS)
POOL_STATE = max(POOL_WINDOWS) - 1
N_EXPERTS = 32
TOP_K = 4
D_EXPERT = D_MODEL
SWIGLU_ALPHA = 1.702
SWIGLU_LIMIT = 7.0
NORM_EPS = 1e-5
PAST_LEN = 16384

LANES = 128
QSUB = 64
KEYS_SUB = QSUB + WINDOW
META_PAD = 64
NKEY = META_PAD + KEYS_SUB
MASKED = -1e30
PROMPT_BLOCK = 1024
ROUTE_BLOCK = 384
EXPERT_ROWS = 512
DISPATCH_TILE = 384
PACK_CHUNKS = D_MODEL // 2 // LANES
COMBINE_TILE = 128
WIN = 32
WIN_ALIGN = 16
WIN_ROWS = WIN + WIN_ALIGN
VMEM_LIMIT = 56 * 1024 * 1024


def _rms(x, g):
    return x * lax.rsqrt(jnp.mean(x * x, axis=-1, keepdims=True) + NORM_EPS) * g


def _router_logits(rwt_ref, h2, h2_hi):
    nt = (((1,), (1,)), ((), ()))
    h2_lo = (h2 - h2_hi.astype(F32)).astype(BF16)
    return (lax.dot_general(rwt_ref[0], h2_hi, nt, preferred_element_type=F32)
            + lax.dot_general(rwt_ref[0], h2_lo, nt, preferred_element_type=F32)
            + lax.dot_general(rwt_ref[1], h2_hi, nt, preferred_element_type=F32))


def _dup_halves(a):
    lane = lax.broadcasted_iota(jnp.int32, a.shape, 1)
    r = pltpu.roll(a, HEAD_DIM, axis=1)
    lo = lane < HEAD_DIM
    return jnp.where(lo, a, r), jnp.where(lo, r, a)


def _pool_means(pext_ref, n):
    outs = []
    for gi, w in enumerate(POOL_WINDOWS):
        xg = pext_ref[:, gi * POOL_GROUP_DIM:(gi + 1) * POOL_GROUP_DIM]
        s = xg
        sh = 1
        while sh < w:
            s = s + pltpu.roll(s, sh, axis=0)
            sh *= 2
        outs.append(s[16:] * (1.0 / w) - xg[16:])
    return outs


def _pack_bf16_pairs(h):
    m = h.shape[1] // 2
    lo = pltpu.bitcast(h[:, :m].astype(BF16).astype(F32), jnp.uint32)
    hi = pltpu.bitcast(h[:, m:].astype(BF16).astype(F32), jnp.uint32)
    return lax.shift_right_logical(lo, jnp.uint32(16)) | (hi & jnp.uint32(0xFFFF0000))


def _unpack_bf16_pairs(w):
    lo = pltpu.bitcast(lax.shift_left(w, jnp.uint32(16)), F32).astype(BF16)
    hi = pltpu.bitcast(w & jnp.uint32(0xFFFF0000), F32).astype(BF16)
    return lo, hi


def _prompt_kernel(x_ref, meta_ref, gattn_ref, win_ref, wpool_ref, pscale_ref, wout_ref, gffn_ref,
                   rwt_ref, rb_ref, sink_ref, tbl_ref, tail_h2_ref, tail_lgt_ref,
                   x1_ref, h2_ref, lgt_ref, kmeta_ref, vmeta_ref, ktail_ref, vtail_ref, ptail_ref,
                   k2buf, v2buf, km2, vm2, qbuf, obuf, pext):
    pid = pl.program_id(0)
    n_main = pl.num_programs(0) - 1
    refs = (x_ref, meta_ref, gattn_ref, win_ref, wpool_ref, pscale_ref, wout_ref, gffn_ref,
            rwt_ref, rb_ref, sink_ref, tbl_ref,
            x1_ref, h2_ref, lgt_ref, kmeta_ref, vmeta_ref, ktail_ref, vtail_ref, ptail_ref,
            k2buf, v2buf, km2, vm2, qbuf, obuf, pext)

    @pl.when(pid < n_main)
    def _():
        _prompt_block(*refs)

    @pl.when(pid == n_main)
    def _():
        h2_ref[0:tail_h2_ref.shape[0], :] = tail_h2_ref[...]
        lgt_ref[:, 0:tail_lgt_ref.shape[1]] = tail_lgt_ref[...]


def _prompt_block(x_ref, meta_ref, gattn_ref, win_ref, wpool_ref, pscale_ref, wout_ref, gffn_ref,
                  rwt_ref, rb_ref, sink_ref, tbl_ref,
                  x1_ref, h2_ref, lgt_ref, kmeta_ref, vmeta_ref, ktail_ref, vtail_ref, ptail_ref,
                  k2buf, v2buf, km2, vm2, qbuf, obuf, pext):
    tb = x_ref.shape[0]
    pid = pl.program_id(0)

    @pl.when(pid == 0)
    def _():
        hm = _rms(meta_ref[...], gattn_ref[...]).astype(BF16)
        km = jnp.dot(hm, win_ref[:, ATTN_WIDTH:ATTN_WIDTH + KV_WIDTH], preferred_element_type=F32)
        vm = jnp.dot(hm, win_ref[:, ATTN_WIDTH + KV_WIDTH:ATTN_WIDTH + 2 * KV_WIDTH],
                     preferred_element_type=F32)
        pm = jnp.dot(hm, win_ref[:, ATTN_WIDTH + 2 * KV_WIDTH:], preferred_element_type=F32)
        kmeta_ref[...] = km
        vmeta_ref[...] = vm
        zpad = jnp.zeros((META_PAD - N_META, LANES), F32)
        k0, k1 = _dup_halves(jnp.concatenate([km, zpad], axis=0))
        v0, v1 = _dup_halves(jnp.concatenate([vm, zpad], axis=0))
        km2[0] = k0.astype(BF16)
        km2[1] = k1.astype(BF16)
        vm2[0, :, 0:LANES] = v0.astype(BF16)
        vm2[1, :, 0:LANES] = v1.astype(BF16)
        vm2[:, :, LANES:] = jnp.ones((N_KV_HEADS, META_PAD, LANES), BF16)
        k2buf[:, 0:WINDOW, :] = jnp.zeros((2, WINDOW, LANES), BF16)
        v2buf[:, 0:WINDOW, 0:LANES] = jnp.zeros((2, WINDOW, LANES), BF16)
        v2buf[:, :, LANES:] = jnp.ones((N_KV_HEADS, WINDOW + tb, LANES), BF16)
        pext[0:16, :] = pm

    h = _rms(x_ref[...], gattn_ref[...]).astype(BF16)
    q = jnp.dot(h, win_ref[:, 0:ATTN_WIDTH], preferred_element_type=F32) * (HEAD_DIM ** -0.5)
    lane_t = lax.broadcasted_iota(jnp.int32, (tb, LANES), 1)
    for c in range(N_HEADS // 2):
        tile = q[:, c * LANES:(c + 1) * LANES]
        for a in range(2):
            keep = (lane_t < HEAD_DIM) if a == 0 else (lane_t >= HEAD_DIM)
            piece = jnp.where(keep, tile, 0.0).astype(BF16).reshape(tb // QSUB, QSUB, LANES)
            row = ((c % 2) * 2 + a) * QSUB
            qbuf[c // 2, :, row:row + QSUB, :] = piece
    k = jnp.dot(h, win_ref[:, ATTN_WIDTH:ATTN_WIDTH + KV_WIDTH], preferred_element_type=F32)
    v = jnp.dot(h, win_ref[:, ATTN_WIDTH + KV_WIDTH:ATTN_WIDTH + 2 * KV_WIDTH], preferred_element_type=F32)
    p = jnp.dot(h, win_ref[:, ATTN_WIDTH + 2 * KV_WIDTH:], preferred_element_type=F32)
    ktail_ref[...] = k[tb - WINDOW:]
    vtail_ref[...] = v[tb - WINDOW:]
    ptail_ref[...] = p[tb - 16:]
    k0, k1 = _dup_halves(k)
    v0, v1 = _dup_halves(v)
    k2buf[0, WINDOW:, :] = k0.astype(BF16)
    k2buf[1, WINDOW:, :] = k1.astype(BF16)
    v2buf[0, WINDOW:, 0:LANES] = v0.astype(BF16)
    v2buf[1, WINDOW:, 0:LANES] = v1.astype(BF16)
    pext[16:, :] = p

    lane_q = lax.broadcasted_iota(jnp.int32, (QSUB, LANES), 1)
    lo_q = lane_q < HEAD_DIM

    for u in range(tb // QSUB):
        r0 = u * QSUB
        sel = jnp.where(pid == 0, u + 1, 0) if u < WINDOW // QSUB else 0
        for g in range(N_KV_HEADS):
            qm = qbuf[g, u]
            kwin = jnp.concatenate([km2[g], k2buf[g, r0:r0 + KEYS_SUB, :]], axis=0)
            vwin = jnp.concatenate([vm2[g], v2buf[g, r0:r0 + KEYS_SUB, :]], axis=0)
            s = lax.dot_general(qm, kwin, (((1,), (1,)), ((), ())), preferred_element_type=F32)
            s = s + tbl_ref[sel, g]
            sink = sink_ref[g]
            m = jnp.maximum(jnp.max(s, axis=1, keepdims=True), sink)
            e = jnp.exp(s - m).astype(BF16)
            r = jnp.dot(e, vwin, preferred_element_type=F32)
            o = r[:, 0:LANES] / (r[:, LANES:] + jnp.exp(sink - m))
            o0 = jnp.where(lo_q, o[0:QSUB], o[QSUB:2 * QSUB])
            o1 = jnp.where(lo_q, o[2 * QSUB:3 * QSUB], o[3 * QSUB:])
            obuf[r0:r0 + QSUB, (2 * g) * LANES:(2 * g + 1) * LANES] = o0.astype(BF16)
            obuf[r0:r0 + QSUB, (2 * g + 1) * LANES:(2 * g + 2) * LANES] = o1.astype(BF16)

    pooled = _pool_means(pext, tb)
    for gi in range(len(POOL_WINDOWS)):
        y = jnp.dot(pooled[gi].astype(BF16), wpool_ref[gi], preferred_element_type=F32)
        y = y * pscale_ref[:, gi * POOL_GROUP_DIM:(gi + 1) * POOL_GROUP_DIM]
        obuf[:, ATTN_WIDTH + gi * POOL_GROUP_DIM:ATTN_WIDTH + (gi + 1) * POOL_GROUP_DIM] = y.astype(BF16)

    k2buf[:, 0:WINDOW, :] = k2buf[:, tb:tb + WINDOW, :]
    v2buf[:, 0:WINDOW, 0:LANES] = v2buf[:, tb:tb + WINDOW, 0:LANES]
    pext[0:16, :] = pext[tb:tb + 16, :]

    x1 = x_ref[...] + jnp.dot(obuf[...], wout_ref[...], preferred_element_type=F32)
    x1_ref[...] = x1
    h2 = _rms(x1, gffn_ref[...])
    h2_hi = h2.astype(BF16)
    h2_ref[...] = h2_hi
    lgt_ref[...] = _router_logits(rwt_ref, h2, h2_hi) + rb_ref[...]


def _attn_tables(sinks):
    i = np.arange(QSUB)[:, None]
    j = np.arange(NKEY)[None, :]
    jb = j - META_PAD
    rel = i + WINDOW - jb
    band_ok = (jb >= 0) & (rel >= 0) & (rel <= WINDOW)
    meta_ok = (j < N_META) & (i >= 0)
    slopes = np.exp2(-8.0 * np.arange(1, N_HEADS + 1) / N_HEADS)
    tbl = np.empty((3, N_KV_HEADS, GQA_GROUP * QSUB, NKEY), np.float32)
    for var in range(3):
        ok = band_ok if var == 0 else band_ok & (jb >= WINDOW - (var - 1) * QSUB)
        for g in range(N_KV_HEADS):
            for a in range(GQA_GROUP):
                hd = g * GQA_GROUP + a
                bias = np.where(ok, -slopes[hd] * rel, MASKED)
                bias = np.where(meta_ok, 0.0, bias)
                tbl[var, g, a * QSUB:(a + 1) * QSUB] = bias
    sink_col = jnp.repeat(sinks.astype(F32).reshape(N_KV_HEADS, GQA_GROUP, 1), QSUB, axis=2)
    return jnp.asarray(tbl), sink_col.reshape(N_KV_HEADS, GQA_GROUP * QSUB, 1)


def _prompt_mixer(x, meta, gattn, win, wpool, pscale, wout, gffn, rwt, rb, sinks, tail_h2, tail_lgt):
    seq = x.shape[0]
    tb = PROMPT_BLOCK
    n_tail = tail_h2.shape[0]
    assert seq % tb == 0 and tb % WINDOW == 0 and n_tail <= tb
    nblk = seq // tb
    n_tok = seq + n_tail
    tbl, sink_col = _attn_tables(sinks)
    full = lambda *shape: pl.BlockSpec(shape, lambda i: (0,) * len(shape))
    main = lambda i: (jnp.minimum(i, nblk - 1), 0)
    in_width = win.shape[1]
    return pl.pallas_call(
        _prompt_kernel,
        grid=(nblk + 1,),
        in_specs=[
            pl.BlockSpec((tb, D_MODEL), main),
            full(N_META, D_MODEL), full(1, D_MODEL), full(D_MODEL, in_width),
            full(len(POOL_WINDOWS), POOL_GROUP_DIM, POOL_GROUP_DIM), full(1, POOL_WIDTH),
            full(D_MODEL, D_MODEL), full(1, D_MODEL), full(2, N_EXPERTS, D_MODEL), full(N_EXPERTS, 1),
            full(N_KV_HEADS, GQA_GROUP * QSUB, 1), full(3, N_KV_HEADS, GQA_GROUP * QSUB, NKEY),
            full(n_tail, D_MODEL), full(N_EXPERTS, n_tail),
        ],
        out_specs=[
            pl.BlockSpec((tb, D_MODEL), main),
            pl.BlockSpec((tb, D_MODEL), lambda i: (i, 0)),
            pl.BlockSpec((N_EXPERTS, tb), lambda i: (0, i)),
            full(N_META, KV_WIDTH), full(N_META, KV_WIDTH),
            full(WINDOW, KV_WIDTH), full(WINDOW, KV_WIDTH), full(16, POOL_WIDTH),
        ],
        out_shape=[
            jax.ShapeDtypeStruct((seq, D_MODEL), F32),
            jax.ShapeDtypeStruct((n_tok, D_MODEL), BF16),
            jax.ShapeDtypeStruct((N_EXPERTS, n_tok), F32),
            jax.ShapeDtypeStruct((N_META, KV_WIDTH), F32),
            jax.ShapeDtypeStruct((N_META, KV_WIDTH), F32),
            jax.ShapeDtypeStruct((WINDOW, KV_WIDTH), F32),
            jax.ShapeDtypeStruct((WINDOW, KV_WIDTH), F32),
            jax.ShapeDtypeStruct((16, POOL_WIDTH), F32),
        ],
        scratch_shapes=[
            pltpu.VMEM((N_KV_HEADS, WINDOW + tb, LANES), BF16),
            pltpu.VMEM((N_KV_HEADS, WINDOW + tb, 2 * LANES), BF16),
            pltpu.VMEM((N_KV_HEADS, META_PAD, LANES), BF16),
            pltpu.VMEM((N_KV_HEADS, META_PAD, 2 * LANES), BF16),
            pltpu.VMEM((N_KV_HEADS, tb // QSUB, GQA_GROUP * QSUB, LANES), BF16),
            pltpu.VMEM((tb, D_MODEL), BF16),
            pltpu.VMEM((16 + tb, POOL_WIDTH), F32),
        ],
        compiler_params=pltpu.CompilerParams(dimension_semantics=("arbitrary",),
                                             vmem_limit_bytes=VMEM_LIMIT),
        name="prompt_mixer",
    )(x, meta, gattn, win, wpool, pscale, wout, gffn, rwt, rb, sink_col, tbl, tail_h2, tail_lgt)


def _sample_kernel(x_ref, ck_ref, cv_ref, sp_ref, gattn_ref, win_ref, wpool_ref, pscale_ref, wout_ref,
                   gffn_ref, rwt_ref, rb_ref, sinkc_ref, bias_ref,
                   x1_ref, h2_ref, lgt_ref, knew_ref, vnew_ref, pnew_ref,
                   qm_buf, r_buf, obuf):
    nb = x_ref.shape[0]
    x = x_ref[...]
    h = _rms(x, gattn_ref[...]).astype(BF16)
    q = jnp.dot(h, win_ref[:, 0:ATTN_WIDTH], preferred_element_type=F32) * (HEAD_DIM ** -0.5)
    k = jnp.dot(h, win_ref[:, ATTN_WIDTH:ATTN_WIDTH + KV_WIDTH], preferred_element_type=F32)
    v = jnp.dot(h, win_ref[:, ATTN_WIDTH + KV_WIDTH:ATTN_WIDTH + 2 * KV_WIDTH], preferred_element_type=F32)
    p = jnp.dot(h, win_ref[:, ATTN_WIDTH + 2 * KV_WIDTH:], preferred_element_type=F32)
    knew_ref[...] = k
    vnew_ref[...] = v
    pnew_ref[...] = p

    lane = lax.broadcasted_iota(jnp.int32, (nb, LANES), 1)
    lo = lane < HEAD_DIM
    for hd in range(N_HEADS):
        tile = q[:, (hd // 2) * LANES:(hd // 2 + 1) * LANES]
        if (hd % 2) != (hd // GQA_GROUP):
            tile = pltpu.roll(tile, HEAD_DIM, axis=1)
        keep_lo = (hd // GQA_GROUP) == 0
        qm_buf[:, hd, :] = jnp.where(lo if keep_lo else jnp.logical_not(lo), tile, 0.0)

    def per_batch(b):
        qm = qm_buf[b]
        kb = ck_ref[b].astype(BF16)
        vb = cv_ref[b].astype(BF16)
        s = lax.dot_general(qm.astype(BF16), kb, (((1,), (1,)), ((), ())), preferred_element_type=F32)
        s = s + bias_ref[...]
        kn = knew_ref[pl.ds(b, 1), :]
        vn = vnew_ref[pl.ds(b, 1), :]
        s_self = jnp.sum(qm * kn, axis=1, keepdims=True)
        sink = sinkc_ref[...]
        m = jnp.maximum(jnp.maximum(jnp.max(s, axis=1, keepdims=True), s_self), sink)
        e = jnp.exp(s - m)
        e_self = jnp.exp(s_self - m)
        den = jnp.sum(e, axis=1, keepdims=True) + e_self + jnp.exp(sink - m)
        r = jnp.dot(e.astype(BF16), vb, preferred_element_type=F32)
        r = r + e_self * vn
        r_buf[b] = r / den

    unroll = 8
    assert nb % unroll == 0

    def batch_group(gidx, carry):
        for j in range(unroll):
            per_batch(gidx * unroll + j)
        return carry

    lax.fori_loop(0, nb // unroll, batch_group, 0)

    for c in range(N_HEADS // 2):
        halves = []
        for a in range(2):
            hd = 2 * c + a
            t = r_buf[:, hd, :]
            if (hd // GQA_GROUP) != a:
                t = pltpu.roll(t, HEAD_DIM, axis=1)
            halves.append(t)
        obuf[:, c * LANES:(c + 1) * LANES] = jnp.where(lo, halves[0], halves[1]).astype(BF16)

    for gi, w in enumerate(POOL_WINDOWS):
        cols = slice(gi * POOL_GROUP_DIM, (gi + 1) * POOL_GROUP_DIM)
        pg = p[:, cols]
        acc = pg
        for d in range(1, w):
            acc = acc + sp_ref[:, POOL_STATE - d, cols]
        pooled = acc * (1.0 / w) - pg
        y = jnp.dot(pooled.astype(BF16), wpool_ref[gi], preferred_element_type=F32) * pscale_ref[:, cols]
        obuf[:, ATTN_WIDTH + gi * POOL_GROUP_DIM:ATTN_WIDTH + (gi + 1) * POOL_GROUP_DIM] = y.astype(BF16)

    x1 = x + jnp.dot(obuf[...], wout_ref[...], preferred_element_type=F32)
    x1_ref[...] = x1
    h2 = _rms(x1, gffn_ref[...])
    h2_hi = h2.astype(BF16)
    h2_ref[...] = h2_hi
    lgt_ref[...] = _router_logits(rwt_ref, h2, h2_hi) + rb_ref[...]


def _sample_mixer(x, ck, cv, sp, gattn, win, wpool, pscale, wout, gffn, rwt, rb, sinks):
    nb = x.shape[0]
    rows = ck.shape[1]
    slopes = np.exp2(-8.0 * np.arange(1, N_HEADS + 1) / N_HEADS)
    dist = np.concatenate([np.zeros(N_META), WINDOW - np.arange(WINDOW)])
    bias = jnp.asarray((-slopes[:, None] * dist[None, :]).astype(np.float32))
    vm = pl.BlockSpec(memory_space=pltpu.VMEM)
    return pl.pallas_call(
        _sample_kernel,
        in_specs=[vm] * 14,
        out_specs=[vm] * 6,
        out_shape=[
            jax.ShapeDtypeStruct((nb, D_MODEL), F32),
            jax.ShapeDtypeStruct((nb, D_MODEL), BF16),
            jax.ShapeDtypeStruct((N_EXPERTS, nb), F32),
            jax.ShapeDtypeStruct((nb, KV_WIDTH), F32),
            jax.ShapeDtypeStruct((nb, KV_WIDTH), F32),
            jax.ShapeDtypeStruct((nb, POOL_WIDTH), F32),
        ],
        scratch_shapes=[
            pltpu.VMEM((nb, N_HEADS, LANES), F32),
            pltpu.VMEM((nb, N_HEADS, LANES), F32),
            pltpu.VMEM((nb, D_MODEL), BF16),
        ],
        compiler_params=pltpu.CompilerParams(vmem_limit_bytes=VMEM_LIMIT),
        name="sample_mixer",
    )(x, ck, cv, sp, gattn, win, wpool, pscale, wout, gffn, rwt, rb,
      sinks.astype(F32).reshape(N_HEADS, 1), bias)


def _router_kernel(lg_ref, tri_ref, low_ref, gate_ref, col_ref, lrank_ref, lpos_ref,
                   tcar_ref, cnt_ref, carry):
    tr = lg_ref.shape[1]

    @pl.when(pl.program_id(0) == 0)
    def _():
        carry[...] = jnp.zeros_like(carry)

    work = lg_ref[...]
    eio = lax.broadcasted_iota(jnp.int32, work.shape, 0).astype(F32)
    sels, vals, idxs = [], [], []
    for _k in range(TOP_K):
        mx = jnp.max(work, axis=0, keepdims=True)
        idx = jnp.min(jnp.where(work == mx, eio, float(N_EXPERTS)), axis=0, keepdims=True)
        sel = eio == idx
        sels.append(sel)
        vals.append(mx)
        idxs.append(idx)
        work = jnp.where(sel, -jnp.inf, work)
    exps = [jnp.exp(vk - vals[0]) for vk in vals]
    tot = exps[0] + exps[1] + exps[2] + exps[3]
    onehot = jnp.zeros(work.shape, F32)
    for sel in sels:
        onehot = onehot + sel.astype(F32)
    before = jnp.dot(onehot.astype(BF16), tri_ref[...], preferred_element_type=F32) + carry[...]
    for kk in range(TOP_K):
        gate_ref[pl.ds(kk, 1), :] = exps[kk] / tot
    for j in range(tr // COMBINE_TILE):
        cols = slice(j * COMBINE_TILE, (j + 1) * COMBINE_TILE)
        tc = before[:, j * COMBINE_TILE:j * COMBINE_TILE + 1]
        tcar_ref[j] = tc.astype(jnp.int32)
        slack = tc - WIN_ALIGN * jnp.floor(tc * (1.0 / WIN_ALIGN))
        local = before[:, cols] - tc
        for kk in range(TOP_K):
            selk = sels[kk][:, cols]
            lr = jnp.sum(jnp.where(selk, local, 0.0), axis=0, keepdims=True)
            sl = jnp.sum(jnp.where(selk, slack, 0.0), axis=0, keepdims=True)
            lrank_ref[pl.ds(kk, 1), cols] = lr.astype(jnp.int32)
            col_ref[pl.ds(kk, 1), cols] = (idxs[kk][:, cols] * float(WIN_ROWS) + sl + lr).astype(jnp.int32)
    for j in range(tr // DISPATCH_TILE):
        cols = slice(j * DISPATCH_TILE, (j + 1) * DISPATCH_TILE)
        local = before[:, cols] - before[:, j * DISPATCH_TILE:j * DISPATCH_TILE + 1]
        tile_cnt = jnp.broadcast_to(jnp.sum(onehot[:, cols], axis=1, keepdims=True), local.shape)
        cnt_hi = jnp.floor(tile_cnt * (1.0 / 256.0))
        cnt_lo = tile_cnt - 256.0 * cnt_hi
        run_start = (256.0 * jnp.dot(low_ref[...], cnt_hi.astype(BF16), preferred_element_type=F32)
                     + jnp.dot(low_ref[...], cnt_lo.astype(BF16), preferred_element_type=F32))
        for kk in range(TOP_K):
            lp = jnp.sum(jnp.where(sels[kk][:, cols], run_start + local, 0.0), axis=0, keepdims=True)
            lpos_ref[pl.ds(kk, 1), cols] = lp.astype(jnp.int32)
    carry[...] = carry[...] + jnp.sum(onehot, axis=1, keepdims=True)
    cnt_ref[...] = carry[...].astype(jnp.int32)


def _router(logits_t):
    n = logits_t.shape[1]
    tr = ROUTE_BLOCK
    assert n % tr == 0
    tri = jnp.asarray(np.triu(np.ones((tr, tr), np.float32), k=1), BF16)
    low = jnp.asarray(np.tril(np.ones((N_EXPERTS, N_EXPERTS), np.float32), k=-1), BF16)
    per_tok = pl.BlockSpec((TOP_K, tr), lambda i: (0, i))
    return pl.pallas_call(
        _router_kernel,
        grid=(n // tr,),
        in_specs=[pl.BlockSpec((N_EXPERTS, tr), lambda i: (0, i)),
                  pl.BlockSpec((tr, tr), lambda i: (0, 0)),
                  pl.BlockSpec((N_EXPERTS, N_EXPERTS), lambda i: (0, 0))],
        out_specs=[per_tok, per_tok, per_tok, per_tok,
                   pl.BlockSpec((tr // COMBINE_TILE, N_EXPERTS, 1), lambda i: (i, 0, 0)),
                   pl.BlockSpec((N_EXPERTS, 1), lambda i: (0, 0))],
        out_shape=[jax.ShapeDtypeStruct((TOP_K, n), F32),
                   jax.ShapeDtypeStruct((TOP_K, n), jnp.int32),
                   jax.ShapeDtypeStruct((TOP_K, n), jnp.int32),
                   jax.ShapeDtypeStruct((TOP_K, n), jnp.int32),
                   jax.ShapeDtypeStruct((n // COMBINE_TILE, N_EXPERTS, 1), jnp.int32),
                   jax.ShapeDtypeStruct((N_EXPERTS, 1), jnp.int32)],
        scratch_shapes=[pltpu.VMEM((N_EXPERTS, 1), F32)],
        compiler_params=pltpu.CompilerParams(dimension_semantics=("arbitrary",)),
        name="router",
    )(logits_t, tri, low)


def _dispatch_kernel(lstart_ref, cnt_ref, dst_ref, h2_ref, lpos_ref, xs_hbm, stg0, stg1, sem):
    i = pl.program_id(0)
    n_tiles = pl.num_programs(0)
    dt = h2_ref.shape[0]
    rows = dt * TOP_K
    slot = i % 2
    stgs = (stg0, stg1)
    pieces = [p for p in (256, 128, 64, 32, 16, 8, 4, 2, 1) if p <= dt]
    assert dt < 512

    def drain(s):
        pltpu.make_async_copy(stgs[s], xs_hbm.at[pl.ds(0, rows)], sem.at[s]).wait()

    def issue_runs(tile, live, s):
        for e in range(N_EXPERTS):
            n = jnp.where(live, cnt_ref[tile * N_EXPERTS + e], 0)
            src0 = lstart_ref[tile * N_EXPERTS + e]
            dst0 = dst_ref[tile * N_EXPERTS + e]
            for piece in pieces:
                off = n & ~jnp.int32(2 * piece - 1)

                @pl.when((n & piece) != 0)
                def _(off=off, piece=piece, src0=src0, dst0=dst0):
                    pltpu.make_async_copy(stgs[s].at[pl.ds(src0 + off, piece)],
                                          xs_hbm.at[pl.ds(dst0 + off, piece)], sem.at[s]).start()

    def sort_tile(s):
        rid = lax.broadcasted_iota(jnp.int32, (rows, dt), 0)
        hit = rid == lpos_ref[0:1, :]
        for kk in range(1, TOP_K):
            hit = jnp.logical_or(hit, rid == lpos_ref[kk:kk + 1, :])
        perm = jnp.where(hit, 1.0, 0.0).astype(BF16)
        srt = jnp.dot(perm, h2_ref[...], preferred_element_type=F32)
        packed = _pack_bf16_pairs(srt)
        for c in range(PACK_CHUNKS):
            stgs[s][:, c, :] = packed[:, c * LANES:(c + 1) * LANES]

    for s in range(2):
        @pl.when(slot == s)
        def _(s=s):
            @pl.when(i >= 2)
            def _():
                drain(s)

            issue_runs(jnp.maximum(i - 1, 0), i >= 1, 1 - s)
            sort_tile(s)

            @pl.when(i == n_tiles - 1)
            def _():
                issue_runs(i, True, s)
                drain(s)

                @pl.when(n_tiles >= 2)
                def _():
                    drain(1 - s)


def _dispatch(h2, lpos, lstart, cnt, dst, cap):
    n_tok = h2.shape[0]
    dt = DISPATCH_TILE
    assert n_tok % dt == 0
    grid_spec = pltpu.PrefetchScalarGridSpec(
        num_scalar_prefetch=3,
        grid=(n_tok // dt,),
        in_specs=[pl.BlockSpec((dt, D_MODEL), lambda i, a, b, c: (i, 0)),
                  pl.BlockSpec((TOP_K, dt), lambda i, a, b, c: (0, i)),
                  ],
        out_specs=pl.BlockSpec(memory_space=pl.ANY),
        scratch_shapes=[pltpu.VMEM((dt * TOP_K, PACK_CHUNKS, LANES), jnp.uint32),
                        pltpu.VMEM((dt * TOP_K, PACK_CHUNKS, LANES), jnp.uint32),
                        pltpu.SemaphoreType.DMA((2,))],
    )
    return pl.pallas_call(
        _dispatch_kernel,
        grid_spec=grid_spec,
        out_shape=jax.ShapeDtypeStruct((cap, PACK_CHUNKS, LANES), jnp.uint32),
        compiler_params=pltpu.CompilerParams(dimension_semantics=("arbitrary",),
                                             vmem_limit_bytes=VMEM_LIMIT),
        name="dispatch",
    )(lstart, cnt, dst, h2, lpos)


def _expert_kernel(n_xblocks, bexp_ref, nvalid_ref, epos_ref, elist_ref,
                   x_hbm, wgu_hbm, bgu_ref, wd_hbm, bd_ref, y_ref,
                   wgu_f32, wd_f32, wgu_bf, wd_bf, xbuf, xsem, wsem):
    i = pl.program_id(0)
    rb = y_ref.shape[0]
    nvalid = nvalid_ref[i]
    pos = epos_ref[i]
    fresh = jnp.logical_or(i == 0, pos != epos_ref[jnp.maximum(i - 1, 0)])
    slot = i % 2

    def x_copies(blk, s):
        return [pltpu.make_async_copy(x_hbm.at[pl.ds(blk * rb, rb), c, :],
                                      xbuf.at[s, :, pl.ds(c * LANES, LANES)], xsem.at[s])
                for c in range(PACK_CHUNKS)]

    def w_copies(p):
        e = elist_ref[p]
        s = p % 2
        return [pltpu.make_async_copy(wgu_hbm.at[e], wgu_f32.at[s], wsem.at[s, 0]),
                pltpu.make_async_copy(wd_hbm.at[e], wd_f32.at[s], wsem.at[s, 1])]

    @pl.when(i == 0)
    def _():
        for cp in x_copies(0, 0):
            cp.start()

        @pl.when(nvalid > 0)
        def _():
            for cp in w_copies(0):
                cp.start()

    @pl.when(i + 1 < n_xblocks)
    def _():
        for cp in x_copies(i + 1, 1 - slot):
            cp.start()

    @pl.when(jnp.logical_and(fresh, nvalid > 0))
    def _():
        @pl.when(elist_ref[pos + 1] >= 0)
        def _():
            for cp in w_copies(pos + 1):
                cp.start()

        for cp in w_copies(pos):
            cp.wait()
        ws = pos % 2
        chunk = 32

        def cast_gu(r, c):
            r0 = pl.multiple_of(r * chunk, chunk)
            wgu_bf[pl.ds(r0, chunk), :] = wgu_f32[ws, pl.ds(r0, chunk), :].astype(BF16)
            return c

        def cast_d(r, c):
            r0 = pl.multiple_of(r * chunk, chunk)
            wd_bf[pl.ds(r0, chunk), :] = wd_f32[ws, pl.ds(r0, chunk), :].astype(BF16)
            return c

        lax.fori_loop(0, D_MODEL // chunk, cast_gu, 0)
        lax.fori_loop(0, D_EXPERT // chunk, cast_d, 0)

    @pl.when(i < n_xblocks)
    def _():
        for cp in x_copies(i, slot):
            cp.wait()

    def ffn(rows):
        xw = xbuf[slot, 0:rows, :]
        xw = jnp.where(lax.broadcasted_iota(jnp.int32, xw.shape, 0) < nvalid, xw, jnp.uint32(0))
        xlo, xhi = _unpack_bf16_pairs(xw)
        half = D_MODEL // 2
        g = (jnp.dot(xlo, wgu_bf[0:half, 0:D_EXPERT], preferred_element_type=F32)
             + jnp.dot(xhi, wgu_bf[half:, 0:D_EXPERT], preferred_element_type=F32)
             + bgu_ref[0, :, 0:D_EXPERT])
        u = (jnp.dot(xlo, wgu_bf[0:half, D_EXPERT:], preferred_element_type=F32)
             + jnp.dot(xhi, wgu_bf[half:, D_EXPERT:], preferred_element_type=F32)
             + bgu_ref[0, :, D_EXPERT:])
        g = jnp.minimum(g, SWIGLU_LIMIT)
        u = jnp.clip(u, -SWIGLU_LIMIT, SWIGLU_LIMIT)
        act = g * (1.0 / (1.0 + jnp.exp(-SWIGLU_ALPHA * g))) * (u + 1.0)
        y = jnp.dot(act.astype(BF16), wd_bf[...], preferred_element_type=F32) + bd_ref[0]
        row = lax.broadcasted_iota(jnp.int32, y.shape, 0)
        y_ref[0:rows, :] = jnp.where(row < nvalid, y, 0.0).astype(BF16)
        if rows < rb:
            y_ref[rows:, :] = jnp.zeros((rb - rows, D_MODEL), BF16)

    @pl.when(nvalid > rb // 2)
    def _():
        ffn(rb)

    @pl.when(jnp.logical_and(nvalid > 0, nvalid <= rb // 2))
    def _():
        ffn(rb // 2)

    @pl.when(nvalid == 0)
    def _():
        y_ref[...] = jnp.zeros_like(y_ref)


def _experts(xs, block_expert, nvalid, block_pos, expert_list, wgu, bgu, wd, bd):
    rb = EXPERT_ROWS
    n_xblocks = xs.shape[0] // rb
    nblk = n_xblocks + 1
    any_space = pl.BlockSpec(memory_space=pl.ANY)
    grid_spec = pltpu.PrefetchScalarGridSpec(
        num_scalar_prefetch=4,
        grid=(nblk,),
        in_specs=[
            any_space,
            any_space,
            pl.BlockSpec((1, 1, 2 * D_EXPERT), lambda i, be, nu, ep, el: (be[i], 0, 0)),
            any_space,
            pl.BlockSpec((1, 1, D_MODEL), lambda i, be, nu, ep, el: (be[i], 0, 0)),
        ],
        out_specs=pl.BlockSpec((rb, D_MODEL), lambda i, be, nu, ep, el: (i, 0)),
        scratch_shapes=[pltpu.VMEM((2, D_MODEL, 2 * D_EXPERT), F32),
                        pltpu.VMEM((2, D_EXPERT, D_MODEL), F32),
                        pltpu.VMEM((D_MODEL, 2 * D_EXPERT), BF16),
                        pltpu.VMEM((D_EXPERT, D_MODEL), BF16),
                        pltpu.VMEM((2, rb, D_MODEL // 2), jnp.uint32),
                        pltpu.SemaphoreType.DMA((2,)),
                        pltpu.SemaphoreType.DMA((2, 2))],
    )
    return pl.pallas_call(
        functools.partial(_expert_kernel, n_xblocks),
        grid_spec=grid_spec,
        out_shape=jax.ShapeDtypeStruct((nblk * rb, D_MODEL), BF16),
        compiler_params=pltpu.CompilerParams(dimension_semantics=("arbitrary",),
                                             vmem_limit_bytes=VMEM_LIMIT),
        name="experts",
    )(block_expert, nvalid, block_pos, expert_list, xs, wgu, bgu.reshape(N_EXPERTS, 1, 2 * D_EXPERT), wd,
      bd.reshape(N_EXPERTS, 1, D_MODEL))


def _combine_kernel(n_prompt_tiles, offa_ref, nchunk_ref,
                    x1p_ref, x1s_ref, info_ref, info_next_ref, gfin_ref, yb_hbm,
                    outp_ref, outs_ref, ybuf, gbuf0, gbuf1, acc_ref, sem):
    i = pl.program_id(0)
    n_tiles = pl.num_programs(0)

    def window_copy(tile, chunk, e, slot):
        base = pl.multiple_of(offa_ref[tile * N_EXPERTS + e] + chunk * WIN, WIN_ALIGN)
        return pltpu.make_async_copy(yb_hbm.at[pl.ds(base, WIN_ROWS), :],
                                     ybuf.at[slot, pl.ds(e * WIN_ROWS, WIN_ROWS), :],
                                     sem.at[slot])

    def start_windows(tile, chunk, slot):
        for e in range(N_EXPERTS):
            window_copy(tile, chunk, e, slot).start()

    def wait_windows(slot):
        pltpu.make_async_copy(yb_hbm.at[pl.ds(0, N_EXPERTS * WIN_ROWS), :], ybuf.at[slot],
                              sem.at[slot]).wait()

    slot = i % 2

    @pl.when(i == 0)
    def _():
        start_windows(0, 0, 0)

    lane = lax.broadcasted_iota(jnp.int32, (COMBINE_TILE, N_EXPERTS * WIN_ROWS), 1)

    def gate_matrix(ref, chunk):
        g = jnp.zeros(lane.shape, F32)
        for kk in range(TOP_K):
            lr = ref[:, TOP_K + kk:TOP_K + kk + 1]
            in_chunk = jnp.logical_and(lr >= chunk * WIN, lr < chunk * WIN + WIN)
            colk = jnp.where(in_chunk, ref[:, kk:kk + 1] - chunk * WIN, -1.0).astype(jnp.int32)
            g = jnp.where(lane == colk, ref[:, 2 * TOP_K + kk:2 * TOP_K + kk + 1], g)
        return g.astype(BF16)

    def moe_rows(gm, buf):
        return jnp.dot(gm, ybuf[buf], preferred_element_type=F32)

    @pl.when(i == 0)
    def _():
        gbuf0[...] = gate_matrix(info_ref, 0)

    def main(s):
        start_windows(jnp.minimum(i + 1, n_tiles - 1), 0, 1 - s)
        wait_windows(s)
        g_cur, g_nxt = (gbuf0, gbuf1) if s == 0 else (gbuf1, gbuf0)
        acc_ref[...] = moe_rows(g_cur[...], s)
        g_nxt[...] = gate_matrix(info_next_ref, 0)

    for s in range(2):
        @pl.when(slot == s)
        def _(s=s):
            main(s)

    @pl.when(i == n_tiles - 1)
    def _():
        wait_windows(1 - slot)

    def extra_chunk(j, c):
        start_windows(i, j, 2)
        wait_windows(2)
        acc_ref[...] += moe_rows(gate_matrix(info_ref, j), 2)
        return c

    lax.fori_loop(1, nchunk_ref[i], extra_chunk, 0)

    @pl.when(i < n_prompt_tiles)
    def _():
        outp_ref[...] = _rms(x1p_ref[...] + acc_ref[...], gfin_ref[...])

    @pl.when(i >= n_prompt_tiles)
    def _():
        outs_ref[...] = _rms(x1s_ref[...] + acc_ref[...], gfin_ref[...])


def _combine(x1_p, x1_s, tok_info, gfin, yb, offa, nchunk):
    ct = COMBINE_TILE
    n_p, n_s = x1_p.shape[0] // ct, x1_s.shape[0] // ct
    assert x1_p.shape[0] % ct == 0 and x1_s.shape[0] % ct == 0 and n_s >= 1
    n_info = tok_info.shape[1]
    grid_spec = pltpu.PrefetchScalarGridSpec(
        num_scalar_prefetch=2,
        grid=(n_p + n_s,),
        in_specs=[
            pl.BlockSpec((ct, D_MODEL), lambda i, o, c: (jnp.minimum(i, n_p - 1), 0)),
            pl.BlockSpec((ct, D_MODEL), lambda i, o, c: (jnp.maximum(i - n_p, 0), 0)),
            pl.BlockSpec((ct, n_info), lambda i, o, c: (i, 0)),
            pl.BlockSpec((ct, n_info), lambda i, o, c: (jnp.minimum(i + 1, n_p + n_s - 1), 0)),
            pl.BlockSpec((1, D_MODEL), lambda i, o, c: (0, 0)),
            pl.BlockSpec(memory_space=pl.ANY),
        ],
        out_specs=[
            pl.BlockSpec((ct, D_MODEL), lambda i, o, c: (jnp.minimum(i, n_p - 1), 0)),
            pl.BlockSpec((ct, D_MODEL), lambda i, o, c: (jnp.maximum(i - n_p, 0), 0)),
        ],
        scratch_shapes=[pltpu.VMEM((3, N_EXPERTS * WIN_ROWS, D_MODEL), BF16),
                        pltpu.VMEM((ct, N_EXPERTS * WIN_ROWS), BF16),
                        pltpu.VMEM((ct, N_EXPERTS * WIN_ROWS), BF16),
                        pltpu.VMEM((ct, D_MODEL), F32),
                        pltpu.SemaphoreType.DMA((3,))],
    )
    return pl.pallas_call(
        functools.partial(_combine_kernel, n_p),
        grid_spec=grid_spec,
        out_shape=[jax.ShapeDtypeStruct(x1_p.shape, F32), jax.ShapeDtypeStruct(x1_s.shape, F32)],
        compiler_params=pltpu.CompilerParams(dimension_semantics=("arbitrary",),
                                             vmem_limit_bytes=VMEM_LIMIT),
        name="combine",
    )(offa, nchunk, x1_p, x1_s, tok_info, tok_info, gfin, yb)


def kernel(x_prompt, x_sample, cache_k, cache_v, state_pool, meta_tokens, norm_attn, w_in, attn_sinks,
           w_pool, pool_scale, w_out, norm_ffn, router_w, router_b, w_gate_up, b_gate_up, w_down, b_down,
           norm_final):
    assert w_in.shape[0] == 1, "single-layer trunk"
    bsz, seq, _ = x_prompt.shape
    assert bsz == 1
    nb = x_sample.shape[0]
    n_tok = seq + nb
    gattn = norm_attn[0].reshape(1, D_MODEL)
    gffn = norm_ffn[0].reshape(1, D_MODEL)
    win = w_in[0].astype(BF16)
    wpool = w_pool[0].astype(BF16)
    wout = w_out[0].astype(BF16)
    pscale = pool_scale[0].reshape(1, POOL_WIDTH)
    rw_t = router_w[0].T
    rw_hi = rw_t.astype(BF16)
    rwt = jnp.stack([rw_hi, (rw_t - rw_hi.astype(F32)).astype(BF16)])
    rb = router_b[0].reshape(N_EXPERTS, 1)
    sinks = attn_sinks[0]

    ck = cache_k[0].reshape(nb, N_META + WINDOW, KV_WIDTH)
    cv = cache_v[0].reshape(nb, N_META + WINDOW, KV_WIDTH)
    (x1_s, h2_s, lgt_s, knew, vnew, pnew) = _sample_mixer(
        x_sample[:, 0], ck, cv, state_pool[0], gattn, win, wpool, pscale, wout, gffn, rwt, rb, sinks)
    (x1_p, h2_all, lgt_all, kmeta, vmeta, ktail, vtail, ptail) = _prompt_mixer(
        x_prompt[0], meta_tokens, gattn, win, wpool, pscale, wout, gffn, rwt, rb, sinks, h2_s, lgt_s)

    gates, col, lrank, lpos, tcar, counts = _router(lgt_all)
    counts = counts[:, 0]
    tcar = tcar[:, :, 0]
    rbk = EXPERT_ROWS
    eids = jnp.arange(N_EXPERTS, dtype=jnp.int32)
    earlier = eids[None, :] < eids[:, None]
    excl_sum = lambda a: jnp.sum(jnp.where(earlier, a[..., None, :], 0), axis=-1)
    padded = (counts + rbk - 1) // rbk * rbk
    pad_start = excl_sum(padded).astype(jnp.int32)
    pad_end = pad_start + padded
    nblk = -(-(n_tok * TOP_K) // rbk) + N_EXPERTS
    cap = nblk * rbk
    block_start = jnp.arange(nblk + 1, dtype=jnp.int32) * rbk
    owns = (pad_start[None, :] <= block_start[:, None]) & (block_start[:, None] < pad_end[None, :])
    nvalid = jnp.sum(jnp.where(owns, jnp.clip(counts[None, :] - (block_start[:, None] - pad_start[None, :]),
                                              0, rbk), 0), axis=1).astype(jnp.int32)
    has_rows = counts > 0
    last_e = jnp.max(jnp.where(has_rows, eids, 0))
    block_expert = jnp.where(jnp.any(owns, axis=1), jnp.sum(jnp.where(owns, eids[None, :], 0), axis=1),
                             last_e).astype(jnp.int32)

    run_len = jnp.concatenate([tcar[1:], counts[None, :]], axis=0) - tcar
    dcar = tcar[::DISPATCH_TILE // COMBINE_TILE]
    drun_len = jnp.concatenate([dcar[1:], counts[None, :]], axis=0) - dcar
    flat = lambda a: a.astype(jnp.int32).reshape(-1)
    xs = _dispatch(h2_all, lpos, flat(excl_sum(drun_len)), flat(drun_len), flat(pad_start[None, :] + dcar),
                   cap)
    expert_pos = excl_sum(has_rows.astype(jnp.int32))
    at_pos = has_rows[None, :] & (expert_pos[None, :] == jnp.arange(N_EXPERTS + 1, dtype=jnp.int32)[:, None])
    expert_list = jnp.where(jnp.any(at_pos, axis=1), jnp.sum(jnp.where(at_pos, eids[None, :], 0), axis=1),
                            -1).astype(jnp.int32)
    block_pos = jnp.sum(jnp.where(block_expert[:, None] == eids[None, :], expert_pos[None, :], 0),
                        axis=1).astype(jnp.int32)
    yb = _experts(xs, block_expert, nvalid, block_pos, expert_list,
                  w_gate_up[0], b_gate_up[0], w_down[0], b_down[0])

    offa = (pad_start[None, :] + (tcar - tcar % WIN_ALIGN)).astype(jnp.int32).reshape(-1)
    nchunk = jnp.maximum(jnp.max((run_len + WIN - 1) // WIN, axis=1), 1).astype(jnp.int32)
    gfin = norm_final.reshape(1, D_MODEL)
    tok_info = jnp.concatenate([col.astype(F32), lrank.astype(F32), gates], axis=0).T
    y_prompt, y_sample = _combine(x1_p, x1_s, tok_info, gfin, yb, offa, nchunk)

    kv_shape = (1, 1, N_META + WINDOW, N_KV_HEADS, HEAD_DIM)
    new_k_p = jnp.concatenate([kmeta, ktail], axis=0).reshape(kv_shape)
    new_v_p = jnp.concatenate([vmeta, vtail], axis=0).reshape(kv_shape)
    new_pool_p = ptail[16 - POOL_STATE:].reshape(1, 1, POOL_STATE, POOL_WIDTH)
    new_k_s = jnp.concatenate([ck[:, :N_META], ck[:, N_META + 1:], knew[:, None]], axis=1).reshape(
        (1, nb, N_META + WINDOW, N_KV_HEADS, HEAD_DIM))
    new_v_s = jnp.concatenate([cv[:, :N_META], cv[:, N_META + 1:], vnew[:, None]], axis=1).reshape(
        (1, nb, N_META + WINDOW, N_KV_HEADS, HEAD_DIM))
    new_pool_s = jnp.concatenate([state_pool[0][:, 1:], pnew[:, None]], axis=1)[None]
    return (y_prompt[None], y_sample[:, None], new_k_p, new_v_p, new_pool_p, new_k_s, new_v_s, new_pool_s)
```

```python
import functools

import jax
import jax.numpy as jnp
import numpy as np
from jax import lax
from jax.experimental import pallas as pl
from jax.experimental.pallas import tpu as pltpu

F32 = jnp.float32
BF16 = jnp.bfloat16

D_MODEL = 1024
N_META = 16
N_HEADS = 8
HEAD_DIM = 64
N_KV_HEADS = 2
GQA_GROUP = N_HEADS // N_KV_HEADS
ATTN_WIDTH = N_HEADS * HEAD_DIM
KV_WIDTH = N_KV_HEADS * HEAD_DIM
WINDOW = 128
POOL_WIDTH = D_MODEL - ATTN_WIDTH
POOL_WINDOWS = (2, 4, 8, 16)
POOL_GROUP_DIM = POOL_WIDTH // len(POOL_WINDOWS)
POOL_STATE = max(POOL_WINDOWS) - 1
N_EXPERTS = 32
TOP_K = 4
D_EXPERT = D_MODEL
SWIGLU_ALPHA = 1.702
SWIGLU_LIMIT = 7.0
NORM_EPS = 1e-5
PAST_LEN = 16384

LANES = 128
QSUB = 64
KEYS_SUB = QSUB + WINDOW
META_PAD = 64
NKEY = META_PAD + KEYS_SUB
MASKED = -1e30
PROMPT_BLOCK = 1024
ROUTE_BLOCK = 384
EXPERT_ROWS = 512
DISPATCH_TILE = 384
PACK_CHUNKS = D_MODEL // 2 // LANES
COMBINE_TILE = 128
WIN = 32
WIN_ALIGN = 16
WIN_ROWS = WIN + WIN_ALIGN
VMEM_LIMIT = 56 * 1024 * 1024


def _rms(x, g):
    return x * lax.rsqrt(jnp.mean(x * x, axis=-1, keepdims=True) + NORM_EPS) * g


def _router_logits(rwt_ref, h2, h2_hi):
    nt = (((1,), (1,)), ((), ()))
    h2_lo = (h2 - h2_hi.astype(F32)).astype(BF16)
    return (lax.dot_general(rwt_ref[0], h2_hi, nt, preferred_element_type=F32)
            + lax.dot_general(rwt_ref[0], h2_lo, nt, preferred_element_type=F32)
            + lax.dot_general(rwt_ref[1], h2_hi, nt, preferred_element_type=F32))


def _dup_halves(a):
    lane = lax.broadcasted_iota(jnp.int32, a.shape, 1)
    r = pltpu.roll(a, HEAD_DIM, axis=1)
    lo = lane < HEAD_DIM
    return jnp.where(lo, a, r), jnp.where(lo, r, a)


def _pool_means(pext_ref, n):
    outs = []
    for gi, w in enumerate(POOL_WINDOWS):
        xg = pext_ref[:, gi * POOL_GROUP_DIM:(gi + 1) * POOL_GROUP_DIM]
        s = xg
        sh = 1
        while sh < w:
            s = s + pltpu.roll(s, sh, axis=0)
            sh *= 2
        outs.append(s[16:] * (1.0 / w) - xg[16:])
    return outs


def _pack_bf16_pairs(h):
    m = h.shape[1] // 2
    lo = pltpu.bitcast(h[:, :m].astype(BF16).astype(F32), jnp.uint32)
    hi = pltpu.bitcast(h[:, m:].astype(BF16).astype(F32), jnp.uint32)
    return lax.shift_right_logical(lo, jnp.uint32(16)) | (hi & jnp.uint32(0xFFFF0000))


def _unpack_bf16_pairs(w):
    lo = pltpu.bitcast(lax.shift_left(w, jnp.uint32(16)), F32).astype(BF16)
    hi = pltpu.bitcast(w & jnp.uint32(0xFFFF0000), F32).astype(BF16)
    return lo, hi


def _prompt_kernel(x_ref, meta_ref, gattn_ref, win_ref, wpool_ref, pscale_ref, wout_ref, gffn_ref,
                   rwt_ref, rb_ref, sink_ref, tbl_ref, tail_h2_ref, tail_lgt_ref,
                   x1_ref, h2_ref, lgt_ref, kmeta_ref, vmeta_ref, ktail_ref, vtail_ref, ptail_ref,
                   k2buf, v2buf, km2, vm2, qbuf, obuf, pext):
    pid = pl.program_id(0)
    n_main = pl.num_programs(0) - 1
    refs = (x_ref, meta_ref, gattn_ref, win_ref, wpool_ref, pscale_ref, wout_ref, gffn_ref,
            rwt_ref, rb_ref, sink_ref, tbl_ref,
            x1_ref, h2_ref, lgt_ref, kmeta_ref, vmeta_ref, ktail_ref, vtail_ref, ptail_ref,
            k2buf, v2buf, km2, vm2, qbuf, obuf, pext)

    @pl.when(pid < n_main)
    def _():
        _prompt_block(*refs)

    @pl.when(pid == n_main)
    def _():
        h2_ref[0:tail_h2_ref.shape[0], :] = tail_h2_ref[...]
        lgt_ref[:, 0:tail_lgt_ref.shape[1]] = tail_lgt_ref[...]


def _prompt_block(x_ref, meta_ref, gattn_ref, win_ref, wpool_ref, pscale_ref, wout_ref, gffn_ref,
                  rwt_ref, rb_ref, sink_ref, tbl_ref,
                  x1_ref, h2_ref, lgt_ref, kmeta_ref, vmeta_ref, ktail_ref, vtail_ref, ptail_ref,
                  k2buf, v2buf, km2, vm2, qbuf, obuf, pext):
    tb = x_ref.shape[0]
    pid = pl.program_id(0)

    @pl.when(pid == 0)
    def _():
        hm = _rms(meta_ref[...], gattn_ref[...]).astype(BF16)
        km = jnp.dot(hm, win_ref[:, ATTN_WIDTH:ATTN_WIDTH + KV_WIDTH], preferred_element_type=F32)
        vm = jnp.dot(hm, win_ref[:, ATTN_WIDTH + KV_WIDTH:ATTN_WIDTH + 2 * KV_WIDTH],
                     preferred_element_type=F32)
        pm = jnp.dot(hm, win_ref[:, ATTN_WIDTH + 2 * KV_WIDTH:], preferred_element_type=F32)
        kmeta_ref[...] = km
        vmeta_ref[...] = vm
        zpad = jnp.zeros((META_PAD - N_META, LANES), F32)
        k0, k1 = _dup_halves(jnp.concatenate([km, zpad], axis=0))
        v0, v1 = _dup_halves(jnp.concatenate([vm, zpad], axis=0))
        km2[0] = k0.astype(BF16)
        km2[1] = k1.astype(BF16)
        vm2[0, :, 0:LANES] = v0.astype(BF16)
        vm2[1, :, 0:LANES] = v1.astype(BF16)
        vm2[:, :, LANES:] = jnp.ones((N_KV_HEADS, META_PAD, LANES), BF16)
        k2buf[:, 0:WINDOW, :] = jnp.zeros((2, WINDOW, LANES), BF16)
        v2buf[:, 0:WINDOW, 0:LANES] = jnp.zeros((2, WINDOW, LANES), BF16)
        v2buf[:, :, LANES:] = jnp.ones((N_KV_HEADS, WINDOW + tb, LANES), BF16)
        pext[0:16, :] = pm

    h = _rms(x_ref[...], gattn_ref[...]).astype(BF16)
    q = jnp.dot(h, win_ref[:, 0:ATTN_WIDTH], preferred_element_type=F32) * (HEAD_DIM ** -0.5)
    lane_t = lax.broadcasted_iota(jnp.int32, (tb, LANES), 1)
    for c in range(N_HEADS // 2):
        tile = q[:, c * LANES:(c + 1) * LANES]
        for a in range(2):
            keep = (lane_t < HEAD_DIM) if a == 0 else (lane_t >= HEAD_DIM)
            piece = jnp.where(keep, tile, 0.0).astype(BF16).reshape(tb // QSUB, QSUB, LANES)
            row = ((c % 2) * 2 + a) * QSUB
            qbuf[c // 2, :, row:row + QSUB, :] = piece
    k = jnp.dot(h, win_ref[:, ATTN_WIDTH:ATTN_WIDTH + KV_WIDTH], preferred_element_type=F32)
    v = jnp.dot(h, win_ref[:, ATTN_WIDTH + KV_WIDTH:ATTN_WIDTH + 2 * KV_WIDTH], preferred_element_type=F32)
    p = jnp.dot(h, win_ref[:, ATTN_WIDTH + 2 * KV_WIDTH:], preferred_element_type=F32)
    ktail_ref[...] = k[tb - WINDOW:]
    vtail_ref[...] = v[tb - WINDOW:]
    ptail_ref[...] = p[tb - 16:]
    k0, k1 = _dup_halves(k)
    v0, v1 = _dup_halves(v)
    k2buf[0, WINDOW:, :] = k0.astype(BF16)
    k2buf[1, WINDOW:, :] = k1.astype(BF16)
    v2buf[0, WINDOW:, 0:LANES] = v0.astype(BF16)
    v2buf[1, WINDOW:, 0:LANES] = v1.astype(BF16)
    pext[16:, :] = p

    lane_q = lax.broadcasted_iota(jnp.int32, (QSUB, LANES), 1)
    lo_q = lane_q < HEAD_DIM

    for u in range(tb // QSUB):
        r0 = u * QSUB
        sel = jnp.where(pid == 0, u + 1, 0) if u < WINDOW // QSUB else 0
        for g in range(N_KV_HEADS):
            qm = qbuf[g, u]
            kwin = jnp.concatenate([km2[g], k2buf[g, r0:r0 + KEYS_SUB, :]], axis=0)
            vwin = jnp.concatenate([vm2[g], v2buf[g, r0:r0 + KEYS_SUB, :]], axis=0)
            s = lax.dot_general(qm, kwin, (((1,), (1,)), ((), ())), preferred_element_type=F32)
            s = s + tbl_ref[sel, g]
            sink = sink_ref[g]
            m = jnp.maximum(jnp.max(s, axis=1, keepdims=True), sink)
            e = jnp.exp(s - m).astype(BF16)
            r = jnp.dot(e, vwin, preferred_element_type=F32)
            o = r[:, 0:LANES] / (r[:, LANES:] + jnp.exp(sink - m))
            o0 = jnp.where(lo_q, o[0:QSUB], o[QSUB:2 * QSUB])
            o1 = jnp.where(lo_q, o[2 * QSUB:3 * QSUB], o[3 * QSUB:])
            obuf[r0:r0 + QSUB, (2 * g) * LANES:(2 * g + 1) * LANES] = o0.astype(BF16)
            obuf[r0:r0 + QSUB, (2 * g + 1) * LANES:(2 * g + 2) * LANES] = o1.astype(BF16)

    pooled = _pool_means(pext, tb)
    for gi in range(len(POOL_WINDOWS)):
        y = jnp.dot(pooled[gi].astype(BF16), wpool_ref[gi], preferred_element_type=F32)
        y = y * pscale_ref[:, gi * POOL_GROUP_DIM:(gi + 1) * POOL_GROUP_DIM]
        obuf[:, ATTN_WIDTH + gi * POOL_GROUP_DIM:ATTN_WIDTH + (gi + 1) * POOL_GROUP_DIM] = y.astype(BF16)

    k2buf[:, 0:WINDOW, :] = k2buf[:, tb:tb + WINDOW, :]
    v2buf[:, 0:WINDOW, 0:LANES] = v2buf[:, tb:tb + WINDOW, 0:LANES]
    pext[0:16, :] = pext[tb:tb + 16, :]

    x1 = x_ref[...] + jnp.dot(obuf[...], wout_ref[...], preferred_element_type=F32)
    x1_ref[...] = x1
    h2 = _rms(x1, gffn_ref[...])
    h2_hi = h2.astype(BF16)
    h2_ref[...] = h2_hi
    lgt_ref[...] = _router_logits(rwt_ref, h2, h2_hi) + rb_ref[...]


def _attn_tables(sinks):
    i = np.arange(QSUB)[:, None]
    j = np.arange(NKEY)[None, :]
    jb = j - META_PAD
    rel = i + WINDOW - jb
    band_ok = (jb >= 0) & (rel >= 0) & (rel <= WINDOW)
    meta_ok = (j < N_META) & (i >= 0)
    slopes = np.exp2(-8.0 * np.arange(1, N_HEADS + 1) / N_HEADS)
    tbl = np.empty((3, N_KV_HEADS, GQA_GROUP * QSUB, NKEY), np.float32)
    for var in range(3):
        ok = band_ok if var == 0 else band_ok & (jb >= WINDOW - (var - 1) * QSUB)
        for g in range(N_KV_HEADS):
            for a in range(GQA_GROUP):
                hd = g * GQA_GROUP + a
                bias = np.where(ok, -slopes[hd] * rel, MASKED)
                bias = np.where(meta_ok, 0.0, bias)
                tbl[var, g, a * QSUB:(a + 1) * QSUB] = bias
    sink_col = jnp.repeat(sinks.astype(F32).reshape(N_KV_HEADS, GQA_GROUP, 1), QSUB, axis=2)
    return jnp.asarray(tbl), sink_col.reshape(N_KV_HEADS, GQA_GROUP * QSUB, 1)


def _prompt_mixer(x, meta, gattn, win, wpool, pscale, wout, gffn, rwt, rb, sinks, tail_h2, tail_lgt):
    seq = x.shape[0]
    tb = PROMPT_BLOCK
    n_tail = tail_h2.shape[0]
    assert seq % tb == 0 and tb % WINDOW == 0 and n_tail <= tb
    nblk = seq // tb
    n_tok = seq + n_tail
    tbl, sink_col = _attn_tables(sinks)
    full = lambda *shape: pl.BlockSpec(shape, lambda i: (0,) * len(shape))
    main = lambda i: (jnp.minimum(i, nblk - 1), 0)
    in_width = win.shape[1]
    return pl.pallas_call(
        _prompt_kernel,
        grid=(nblk + 1,),
        in_specs=[
            pl.BlockSpec((tb, D_MODEL), main),
            full(N_META, D_MODEL), full(1, D_MODEL), full(D_MODEL, in_width),
            full(len(POOL_WINDOWS), POOL_GROUP_DIM, POOL_GROUP_DIM), full(1, POOL_WIDTH),
            full(D_MODEL, D_MODEL), full(1, D_MODEL), full(2, N_EXPERTS, D_MODEL), full(N_EXPERTS, 1),
            full(N_KV_HEADS, GQA_GROUP * QSUB, 1), full(3, N_KV_HEADS, GQA_GROUP * QSUB, NKEY),
            full(n_tail, D_MODEL), full(N_EXPERTS, n_tail),
        ],
        out_specs=[
            pl.BlockSpec((tb, D_MODEL), main),
            pl.BlockSpec((tb, D_MODEL), lambda i: (i, 0)),
            pl.BlockSpec((N_EXPERTS, tb), lambda i: (0, i)),
            full(N_META, KV_WIDTH), full(N_META, KV_WIDTH),
            full(WINDOW, KV_WIDTH), full(WINDOW, KV_WIDTH), full(16, POOL_WIDTH),
        ],
        out_shape=[
            jax.ShapeDtypeStruct((seq, D_MODEL), F32),
            jax.ShapeDtypeStruct((n_tok, D_MODEL), BF16),
            jax.ShapeDtypeStruct((N_EXPERTS, n_tok), F32),
            jax.ShapeDtypeStruct((N_META, KV_WIDTH), F32),
            jax.ShapeDtypeStruct((N_META, KV_WIDTH), F32),
            jax.ShapeDtypeStruct((WINDOW, KV_WIDTH), F32),
            jax.ShapeDtypeStruct((WINDOW, KV_WIDTH), F32),
            jax.ShapeDtypeStruct((16, POOL_WIDTH), F32),
        ],
        scratch_shapes=[
            pltpu.VMEM((N_KV_HEADS, WINDOW + tb, LANES), BF16),
            pltpu.VMEM((N_KV_HEADS, WINDOW + tb, 2 * LANES), BF16),
            pltpu.VMEM((N_KV_HEADS, META_PAD, LANES), BF16),
            pltpu.VMEM((N_KV_HEADS, META_PAD, 2 * LANES), BF16),
            pltpu.VMEM((N_KV_HEADS, tb // QSUB, GQA_GROUP * QSUB, LANES), BF16),
            pltpu.VMEM((tb, D_MODEL), BF16),
            pltpu.VMEM((16 + tb, POOL_WIDTH), F32),
        ],
        compiler_params=pltpu.CompilerParams(dimension_semantics=("arbitrary",),
                                             vmem_limit_bytes=VMEM_LIMIT),
        name="prompt_mixer",
    )(x, meta, gattn, win, wpool, pscale, wout, gffn, rwt, rb, sink_col, tbl, tail_h2, tail_lgt)


def _sample_kernel(x_ref, ck_ref, cv_ref, sp_ref, gattn_ref, win_ref, wpool_ref, pscale_ref, wout_ref,
                   gffn_ref, rwt_ref, rb_ref, sinkc_ref, bias_ref,
                   x1_ref, h2_ref, lgt_ref, knew_ref, vnew_ref, pnew_ref,
                   qm_buf, r_buf, obuf):
    nb = x_ref.shape[0]
    x = x_ref[...]
    h = _rms(x, gattn_ref[...]).astype(BF16)
    q = jnp.dot(h, win_ref[:, 0:ATTN_WIDTH], preferred_element_type=F32) * (HEAD_DIM ** -0.5)
    k = jnp.dot(h, win_ref[:, ATTN_WIDTH:ATTN_WIDTH + KV_WIDTH], preferred_element_type=F32)
    v = jnp.dot(h, win_ref[:, ATTN_WIDTH + KV_WIDTH:ATTN_WIDTH + 2 * KV_WIDTH], preferred_element_type=F32)
    p = jnp.dot(h, win_ref[:, ATTN_WIDTH + 2 * KV_WIDTH:], preferred_element_type=F32)
    knew_ref[...] = k
    vnew_ref[...] = v
    pnew_ref[...] = p

    lane = lax.broadcasted_iota(jnp.int32, (nb, LANES), 1)
    lo = lane < HEAD_DIM
    for hd in range(N_HEADS):
        tile = q[:, (hd // 2) * LANES:(hd // 2 + 1) * LANES]
        if (hd % 2) != (hd // GQA_GROUP):
            tile = pltpu.roll(tile, HEAD_DIM, axis=1)
        keep_lo = (hd // GQA_GROUP) == 0
        qm_buf[:, hd, :] = jnp.where(lo if keep_lo else jnp.logical_not(lo), tile, 0.0)

    def per_batch(b):
        qm = qm_buf[b]
        kb = ck_ref[b].astype(BF16)
        vb = cv_ref[b].astype(BF16)
        s = lax.dot_general(qm.astype(BF16), kb, (((1,), (1,)), ((), ())), preferred_element_type=F32)
        s = s + bias_ref[...]
        kn = knew_ref[pl.ds(b, 1), :]
        vn = vnew_ref[pl.ds(b, 1), :]
        s_self = jnp.sum(qm * kn, axis=1, keepdims=True)
        sink = sinkc_ref[...]
        m = jnp.maximum(jnp.maximum(jnp.max(s, axis=1, keepdims=True), s_self), sink)
        e = jnp.exp(s - m)
        e_self = jnp.exp(s_self - m)
        den = jnp.sum(e, axis=1, keepdims=True) + e_self + jnp.exp(sink - m)
        r = jnp.dot(e.astype(BF16), vb, preferred_element_type=F32)
        r = r + e_self * vn
        r_buf[b] = r / den

    unroll = 8
    assert nb % unroll == 0

    def batch_group(gidx, carry):
        for j in range(unroll):
            per_batch(gidx * unroll + j)
        return carry

    lax.fori_loop(0, nb // unroll, batch_group, 0)

    for c in range(N_HEADS // 2):
        halves = []
        for a in range(2):
            hd = 2 * c + a
            t = r_buf[:, hd, :]
            if (hd // GQA_GROUP) != a:
                t = pltpu.roll(t, HEAD_DIM, axis=1)
            halves.append(t)
        obuf[:, c * LANES:(c + 1) * LANES] = jnp.where(lo, halves[0], halves[1]).astype(BF16)

    for gi, w in enumerate(POOL_WINDOWS):
        cols = slice(gi * POOL_GROUP_DIM, (gi + 1) * POOL_GROUP_DIM)
        pg = p[:, cols]
        acc = pg
        for d in range(1, w):
            acc = acc + sp_ref[:, POOL_STATE - d, cols]
        pooled = acc * (1.0 / w) - pg
        y = jnp.dot(pooled.astype(BF16), wpool_ref[gi], preferred_element_type=F32) * pscale_ref[:, cols]
        obuf[:, ATTN_WIDTH + gi * POOL_GROUP_DIM:ATTN_WIDTH + (gi + 1) * POOL_GROUP_DIM] = y.astype(BF16)

    x1 = x + jnp.dot(obuf[...], wout_ref[...], preferred_element_type=F32)
    x1_ref[...] = x1
    h2 = _rms(x1, gffn_ref[...])
    h2_hi = h2.astype(BF16)
    h2_ref[...] = h2_hi
    lgt_ref[...] = _router_logits(rwt_ref, h2, h2_hi) + rb_ref[...]


def _sample_mixer(x, ck, cv, sp, gattn, win, wpool, pscale, wout, gffn, rwt, rb, sinks):
    nb = x.shape[0]
    rows = ck.shape[1]
    slopes = np.exp2(-8.0 * np.arange(1, N_HEADS + 1) / N_HEADS)
    dist = np.concatenate([np.zeros(N_META), WINDOW - np.arange(WINDOW)])
    bias = jnp.asarray((-slopes[:, None] * dist[None, :]).astype(np.float32))
    vm = pl.BlockSpec(memory_space=pltpu.VMEM)
    return pl.pallas_call(
        _sample_kernel,
        in_specs=[vm] * 14,
        out_specs=[vm] * 6,
        out_shape=[
            jax.ShapeDtypeStruct((nb, D_MODEL), F32),
            jax.ShapeDtypeStruct((nb, D_MODEL), BF16),
            jax.ShapeDtypeStruct((N_EXPERTS, nb), F32),
            jax.ShapeDtypeStruct((nb, KV_WIDTH), F32),
            jax.ShapeDtypeStruct((nb, KV_WIDTH), F32),
            jax.ShapeDtypeStruct((nb, POOL_WIDTH), F32),
        ],
        scratch_shapes=[
            pltpu.VMEM((nb, N_HEADS, LANES), F32),
            pltpu.VMEM((nb, N_HEADS, LANES), F32),
            pltpu.VMEM((nb, D_MODEL), BF16),
        ],
        compiler_params=pltpu.CompilerParams(vmem_limit_bytes=VMEM_LIMIT),
        name="sample_mixer",
    )(x, ck, cv, sp, gattn, win, wpool, pscale, wout, gffn, rwt, rb,
      sinks.astype(F32).reshape(N_HEADS, 1), bias)


def _router_kernel(lg_ref, tri_ref, low_ref, gate_ref, col_ref, lrank_ref, lpos_ref,
                   tcar_ref, cnt_ref, carry):
    tr = lg_ref.shape[1]

    @pl.when(pl.program_id(0) == 0)
    def _():
        carry[...] = jnp.zeros_like(carry)

    work = lg_ref[...]
    eio = lax.broadcasted_iota(jnp.int32, work.shape, 0).astype(F32)
    sels, vals, idxs = [], [], []
    for _k in range(TOP_K):
        mx = jnp.max(work, axis=0, keepdims=True)
        idx = jnp.min(jnp.where(work == mx, eio, float(N_EXPERTS)), axis=0, keepdims=True)
        sel = eio == idx
        sels.append(sel)
        vals.append(mx)
        idxs.append(idx)
        work = jnp.where(sel, -jnp.inf, work)
    exps = [jnp.exp(vk - vals[0]) for vk in vals]
    tot = exps[0] + exps[1] + exps[2] + exps[3]
    onehot = jnp.zeros(work.shape, F32)
    for sel in sels:
        onehot = onehot + sel.astype(F32)
    before = jnp.dot(onehot.astype(BF16), tri_ref[...], preferred_element_type=F32) + carry[...]
    for kk in range(TOP_K):
        gate_ref[pl.ds(kk, 1), :] = exps[kk] / tot
    for j in range(tr // COMBINE_TILE):
        cols = slice(j * COMBINE_TILE, (j + 1) * COMBINE_TILE)
        tc = before[:, j * COMBINE_TILE:j * COMBINE_TILE + 1]
        tcar_ref[j] = tc.astype(jnp.int32)
        slack = tc - WIN_ALIGN * jnp.floor(tc * (1.0 / WIN_ALIGN))
        local = before[:, cols] - tc
        for kk in range(TOP_K):
            selk = sels[kk][:, cols]
            lr = jnp.sum(jnp.where(selk, local, 0.0), axis=0, keepdims=True)
            sl = jnp.sum(jnp.where(selk, slack, 0.0), axis=0, keepdims=True)
            lrank_ref[pl.ds(kk, 1), cols] = lr.astype(jnp.int32)
            col_ref[pl.ds(kk, 1), cols] = (idxs[kk][:, cols] * float(WIN_ROWS) + sl + lr).astype(jnp.int32)
    for j in range(tr // DISPATCH_TILE):
        cols = slice(j * DISPATCH_TILE, (j + 1) * DISPATCH_TILE)
        local = before[:, cols] - before[:, j * DISPATCH_TILE:j * DISPATCH_TILE + 1]
        tile_cnt = jnp.broadcast_to(jnp.sum(onehot[:, cols], axis=1, keepdims=True), local.shape)
        cnt_hi = jnp.floor(tile_cnt * (1.0 / 256.0))
        cnt_lo = tile_cnt - 256.0 * cnt_hi
        run_start = (256.0 * jnp.dot(low_ref[...], cnt_hi.astype(BF16), preferred_element_type=F32)
                     + jnp.dot(low_ref[...], cnt_lo.astype(BF16), preferred_element_type=F32))
        for kk in range(TOP_K):
            lp = jnp.sum(jnp.where(sels[kk][:, cols], run_start + local, 0.0), axis=0, keepdims=True)
            lpos_ref[pl.ds(kk, 1), cols] = lp.astype(jnp.int32)
    carry[...] = carry[...] + jnp.sum(onehot, axis=1, keepdims=True)
    cnt_ref[...] = carry[...].astype(jnp.int32)


def _router(logits_t):
    n = logits_t.shape[1]
    tr = ROUTE_BLOCK
    assert n % tr == 0
    tri = jnp.asarray(np.triu(np.ones((tr, tr), np.float32), k=1), BF16)
    low = jnp.asarray(np.tril(np.ones((N_EXPERTS, N_EXPERTS), np.float32), k=-1), BF16)
    per_tok = pl.BlockSpec((TOP_K, tr), lambda i: (0, i))
    return pl.pallas_call(
        _router_kernel,
        grid=(n // tr,),
        in_specs=[pl.BlockSpec((N_EXPERTS, tr), lambda i: (0, i)),
                  pl.BlockSpec((tr, tr), lambda i: (0, 0)),
                  pl.BlockSpec((N_EXPERTS, N_EXPERTS), lambda i: (0, 0))],
        out_specs=[per_tok, per_tok, per_tok, per_tok,
                   pl.BlockSpec((tr // COMBINE_TILE, N_EXPERTS, 1), lambda i: (i, 0, 0)),
                   pl.BlockSpec((N_EXPERTS, 1), lambda i: (0, 0))],
        out_shape=[jax.ShapeDtypeStruct((TOP_K, n), F32),
                   jax.ShapeDtypeStruct((TOP_K, n), jnp.int32),
                   jax.ShapeDtypeStruct((TOP_K, n), jnp.int32),
                   jax.ShapeDtypeStruct((TOP_K, n), jnp.int32),
                   jax.ShapeDtypeStruct((n // COMBINE_TILE, N_EXPERTS, 1), jnp.int32),
                   jax.ShapeDtypeStruct((N_EXPERTS, 1), jnp.int32)],
        scratch_shapes=[pltpu.VMEM((N_EXPERTS, 1), F32)],
        compiler_params=pltpu.CompilerParams(dimension_semantics=("arbitrary",)),
        name="router",
    )(logits_t, tri, low)


def _dispatch_kernel(lstart_ref, cnt_ref, dst_ref, h2_ref, lpos_ref, xs_hbm, stg0, stg1, sem):
    i = pl.program_id(0)
    n_tiles = pl.num_programs(0)
    dt = h2_ref.shape[0]
    rows = dt * TOP_K
    slot = i % 2
    stgs = (stg0, stg1)
    pieces = [p for p in (256, 128, 64, 32, 16, 8, 4, 2, 1) if p <= dt]
    assert dt < 512

    def drain(s):
        pltpu.make_async_copy(stgs[s], xs_hbm.at[pl.ds(0, rows)], sem.at[s]).wait()

    def issue_runs(tile, live, s):
        for e in range(N_EXPERTS):
            n = jnp.where(live, cnt_ref[tile * N_EXPERTS + e], 0)
            src0 = lstart_ref[tile * N_EXPERTS + e]
            dst0 = dst_ref[tile * N_EXPERTS + e]
            for piece in pieces:
                off = n & ~jnp.int32(2 * piece - 1)

                @pl.when((n & piece) != 0)
                def _(off=off, piece=piece, src0=src0, dst0=dst0):
                    pltpu.make_async_copy(stgs[s].at[pl.ds(src0 + off, piece)],
                                          xs_hbm.at[pl.ds(dst0 + off, piece)], sem.at[s]).start()

    def sort_tile(s):
        rid = lax.broadcasted_iota(jnp.int32, (rows, dt), 0)
        hit = rid == lpos_ref[0:1, :]
        for kk in range(1, TOP_K):
            hit = jnp.logical_or(hit, rid == lpos_ref[kk:kk + 1, :])
        perm = jnp.where(hit, 1.0, 0.0).astype(BF16)
        srt = jnp.dot(perm, h2_ref[...], preferred_element_type=F32)
        packed = _pack_bf16_pairs(srt)
        for c in range(PACK_CHUNKS):
            stgs[s][:, c, :] = packed[:, c * LANES:(c + 1) * LANES]

    for s in range(2):
        @pl.when(slot == s)
        def _(s=s):
            @pl.when(i >= 2)
            def _():
                drain(s)

            issue_runs(jnp.maximum(i - 1, 0), i >= 1, 1 - s)
            sort_tile(s)

            @pl.when(i == n_tiles - 1)
            def _():
                issue_runs(i, True, s)
                drain(s)

                @pl.when(n_tiles >= 2)
                def _():
                    drain(1 - s)


def _dispatch(h2, lpos, lstart, cnt, dst, cap):
    n_tok = h2.shape[0]
    dt = DISPATCH_TILE
    assert n_tok % dt == 0
    grid_spec = pltpu.PrefetchScalarGridSpec(
        num_scalar_prefetch=3,
        grid=(n_tok // dt,),
        in_specs=[pl.BlockSpec((dt, D_MODEL), lambda i, a, b, c: (i, 0)),
                  pl.BlockSpec((TOP_K, dt), lambda i, a, b, c: (0, i)),
                  ],
        out_specs=pl.BlockSpec(memory_space=pl.ANY),
        scratch_shapes=[pltpu.VMEM((dt * TOP_K, PACK_CHUNKS, LANES), jnp.uint32),
                        pltpu.VMEM((dt * TOP_K, PACK_CHUNKS, LANES), jnp.uint32),
                        pltpu.SemaphoreType.DMA((2,))],
    )
    return pl.pallas_call(
        _dispatch_kernel,
        grid_spec=grid_spec,
        out_shape=jax.ShapeDtypeStruct((cap, PACK_CHUNKS, LANES), jnp.uint32),
        compiler_params=pltpu.CompilerParams(dimension_semantics=("arbitrary",),
                                             vmem_limit_bytes=VMEM_LIMIT),
        name="dispatch",
    )(lstart, cnt, dst, h2, lpos)


def _expert_kernel(n_xblocks, bexp_ref, nvalid_ref, epos_ref, elist_ref,
                   x_hbm, wgu_hbm, bgu_ref, wd_hbm, bd_ref, y_ref,
                   wgu_f32, wd_f32, wgu_bf, wd_bf, xbuf, xbf0, xbf1, xsem, wsem):
    i = pl.program_id(0)
    rb = y_ref.shape[0]
    nvalid = nvalid_ref[i]
    pos = epos_ref[i]
    fresh = jnp.logical_or(i == 0, pos != epos_ref[jnp.maximum(i - 1, 0)])
    slot = i % 2

    def x_copies(blk, s):
        return [pltpu.make_async_copy(x_hbm.at[pl.ds(blk * rb, rb), c, :],
                                      xbuf.at[s, :, pl.ds(c * LANES, LANES)], xsem.at[s])
                for c in range(PACK_CHUNKS)]

    def w_copies(p):
        e = elist_ref[p]
        s = p % 2
        return [pltpu.make_async_copy(wgu_hbm.at[e], wgu_f32.at[s], wsem.at[s, 0]),
                pltpu.make_async_copy(wd_hbm.at[e], wd_f32.at[s], wsem.at[s, 1])]

    def unpack_block(blk_nvalid, raw_slot, dst):
        xw = xbuf[raw_slot]
        xw = jnp.where(lax.broadcasted_iota(jnp.int32, xw.shape, 0) < blk_nvalid, xw, jnp.uint32(0))
        xlo, xhi = _unpack_bf16_pairs(xw)
        dst[:, 0:D_MODEL // 2] = xlo
        dst[:, D_MODEL // 2:] = xhi

    @pl.when(i == 0)
    def _():
        for cp in x_copies(0, 0):
            cp.start()
        if n_xblocks > 1:
            for cp in x_copies(1, 1):
                cp.start()
        for cp in x_copies(0, 0):
            cp.wait()
        unpack_block(nvalid, 0, xbf0)

        @pl.when(nvalid > 0)
        def _():
            for cp in w_copies(0):
                cp.start()

    @pl.when(i + 2 < n_xblocks)
    def _():
        for cp in x_copies(i + 2, slot):
            cp.start()

    @pl.when(i + 1 < n_xblocks)
    def _():
        for cp in x_copies(i + 1, 1 - slot):
            cp.wait()

    @pl.when(jnp.logical_and(fresh, nvalid > 0))
    def _():
        @pl.when(elist_ref[pos + 1] >= 0)
        def _():
            for cp in w_copies(pos + 1):
                cp.start()

        for cp in w_copies(pos):
            cp.wait()
        ws = pos % 2
        chunk = 32

        def cast_gu(r, c):
            r0 = pl.multiple_of(r * chunk, chunk)
            wgu_bf[pl.ds(r0, chunk), :] = wgu_f32[ws, pl.ds(r0, chunk), :].astype(BF16)
            return c

        def cast_d(r, c):
            r0 = pl.multiple_of(r * chunk, chunk)
            wd_bf[pl.ds(r0, chunk), :] = wd_f32[ws, pl.ds(r0, chunk), :].astype(BF16)
            return c

        lax.fori_loop(0, D_MODEL // chunk, cast_gu, 0)
        lax.fori_loop(0, D_EXPERT // chunk, cast_d, 0)

    nvalid_next = jnp.where(i + 1 < n_xblocks, nvalid_ref[jnp.minimum(i + 1, n_xblocks - 1)], 0)

    def block(s, rows):
        x_cur, x_nxt = (xbf0, xbf1) if s == 0 else (xbf1, xbf0)
        if rows > 0:
            x = x_cur[0:rows, :]
            g = jnp.dot(x, wgu_bf[:, 0:D_EXPERT], preferred_element_type=F32) + bgu_ref[0, :, 0:D_EXPERT]
            u = jnp.dot(x, wgu_bf[:, D_EXPERT:], preferred_element_type=F32) + bgu_ref[0, :, D_EXPERT:]
            g = jnp.minimum(g, SWIGLU_LIMIT)
            u = jnp.clip(u, -SWIGLU_LIMIT, SWIGLU_LIMIT)
            act = g * (1.0 / (1.0 + jnp.exp(-SWIGLU_ALPHA * g))) * (u + 1.0)
            y = jnp.dot(act.astype(BF16), wd_bf[...], preferred_element_type=F32) + bd_ref[0]
            row = lax.broadcasted_iota(jnp.int32, y.shape, 0)
            y_ref[0:rows, :] = jnp.where(row < nvalid, y, 0.0).astype(BF16)
        if rows < rb:
            y_ref[rows:, :] = jnp.zeros((rb - rows, D_MODEL), BF16)
        unpack_block(nvalid_next, 1 - s, x_nxt)

    for s in range(2):
        for rows, lo_excl, hi_incl in ((rb, rb // 2, rb), (rb // 2, 0, rb // 2), (0, -1, 0)):
            @pl.when(jnp.logical_and(slot == s, jnp.logical_and(nvalid > lo_excl, nvalid <= hi_incl)))
            def _(s=s, rows=rows):
                block(s, rows)


def _experts(xs, block_expert, nvalid, block_pos, expert_list, wgu, bgu, wd, bd):
    rb = EXPERT_ROWS
    n_xblocks = xs.shape[0] // rb
    nblk = n_xblocks + 1
    any_space = pl.BlockSpec(memory_space=pl.ANY)
    grid_spec = pltpu.PrefetchScalarGridSpec(
        num_scalar_prefetch=4,
        grid=(nblk,),
        in_specs=[
            any_space,
            any_space,
            pl.BlockSpec((1, 1, 2 * D_EXPERT), lambda i, be, nu, ep, el: (be[i], 0, 0)),
            any_space,
            pl.BlockSpec((1, 1, D_MODEL), lambda i, be, nu, ep, el: (be[i], 0, 0)),
        ],
        out_specs=pl.BlockSpec((rb, D_MODEL), lambda i, be, nu, ep, el: (i, 0)),
        scratch_shapes=[pltpu.VMEM((2, D_MODEL, 2 * D_EXPERT), F32),
                        pltpu.VMEM((2, D_EXPERT, D_MODEL), F32),
                        pltpu.VMEM((D_MODEL, 2 * D_EXPERT), BF16),
                        pltpu.VMEM((D_EXPERT, D_MODEL), BF16),
                        pltpu.VMEM((2, rb, D_MODEL // 2), jnp.uint32),
                        pltpu.VMEM((rb, D_MODEL), BF16),
                        pltpu.VMEM((rb, D_MODEL), BF16),
                        pltpu.SemaphoreType.DMA((2,)),
                        pltpu.SemaphoreType.DMA((2, 2))],
    )
    return pl.pallas_call(
        functools.partial(_expert_kernel, n_xblocks),
        grid_spec=grid_spec,
        out_shape=jax.ShapeDtypeStruct((nblk * rb, D_MODEL), BF16),
        compiler_params=pltpu.CompilerParams(dimension_semantics=("arbitrary",),
                                             vmem_limit_bytes=VMEM_LIMIT),
        name="experts",
    )(block_expert, nvalid, block_pos, expert_list, xs, wgu, bgu.reshape(N_EXPERTS, 1, 2 * D_EXPERT), wd,
      bd.reshape(N_EXPERTS, 1, D_MODEL))


def _combine_kernel(n_prompt_tiles, offa_ref, nchunk_ref,
                    x1p_ref, x1s_ref, info_ref, info_next_ref, gfin_ref, yb_hbm,
                    outp_ref, outs_ref, ybuf, gbuf0, gbuf1, acc_ref, sem):
    i = pl.program_id(0)
    n_tiles = pl.num_programs(0)

    def window_copy(tile, chunk, e, slot):
        base = pl.multiple_of(offa_ref[tile * N_EXPERTS + e] + chunk * WIN, WIN_ALIGN)
        return pltpu.make_async_copy(yb_hbm.at[pl.ds(base, WIN_ROWS), :],
                                     ybuf.at[slot, pl.ds(e * WIN_ROWS, WIN_ROWS), :],
                                     sem.at[slot])

    def start_windows(tile, chunk, slot):
        for e in range(N_EXPERTS):
            window_copy(tile, chunk, e, slot).start()

    def wait_windows(slot):
        pltpu.make_async_copy(yb_hbm.at[pl.ds(0, N_EXPERTS * WIN_ROWS), :], ybuf.at[slot],
                              sem.at[slot]).wait()

    slot = i % 2

    @pl.when(i == 0)
    def _():
        start_windows(0, 0, 0)

    lane = lax.broadcasted_iota(jnp.int32, (COMBINE_TILE, N_EXPERTS * WIN_ROWS), 1)

    def gate_matrix(ref, chunk):
        g = jnp.zeros(lane.shape, F32)
        for kk in range(TOP_K):
            lr = ref[:, TOP_K + kk:TOP_K + kk + 1]
            in_chunk = jnp.logical_and(lr >= chunk * WIN, lr < chunk * WIN + WIN)
            colk = jnp.where(in_chunk, ref[:, kk:kk + 1] - chunk * WIN, -1.0).astype(jnp.int32)
            g = jnp.where(lane == colk, ref[:, 2 * TOP_K + kk:2 * TOP_K + kk + 1], g)
        return g.astype(BF16)

    def moe_rows(gm, buf):
        return jnp.dot(gm, ybuf[buf], preferred_element_type=F32)

    @pl.when(i == 0)
    def _():
        gbuf0[...] = gate_matrix(info_ref, 0)

    def main(s):
        start_windows(jnp.minimum(i + 1, n_tiles - 1), 0, 1 - s)
        wait_windows(s)
        g_cur, g_nxt = (gbuf0, gbuf1) if s == 0 else (gbuf1, gbuf0)
        acc_ref[...] = moe_rows(g_cur[...], s)
        g_nxt[...] = gate_matrix(info_next_ref, 0)

    for s in range(2):
        @pl.when(slot == s)
        def _(s=s):
            main(s)

    @pl.when(i == n_tiles - 1)
    def _():
        wait_windows(1 - slot)

    def extra_chunk(j, c):
        start_windows(i, j, 2)
        wait_windows(2)
        acc_ref[...] += moe_rows(gate_matrix(info_ref, j), 2)
        return c

    lax.fori_loop(1, nchunk_ref[i], extra_chunk, 0)

    @pl.when(i < n_prompt_tiles)
    def _():
        outp_ref[...] = _rms(x1p_ref[...] + acc_ref[...], gfin_ref[...])

    @pl.when(i >= n_prompt_tiles)
    def _():
        outs_ref[...] = _rms(x1s_ref[...] + acc_ref[...], gfin_ref[...])


def _combine(x1_p, x1_s, tok_info, gfin, yb, offa, nchunk):
    ct = COMBINE_TILE
    n_p, n_s = x1_p.shape[0] // ct, x1_s.shape[0] // ct
    assert x1_p.shape[0] % ct == 0 and x1_s.shape[0] % ct == 0 and n_s >= 1
    n_info = tok_info.shape[1]
    grid_spec = pltpu.PrefetchScalarGridSpec(
        num_scalar_prefetch=2,
        grid=(n_p + n_s,),
        in_specs=[
            pl.BlockSpec((ct, D_MODEL), lambda i, o, c: (jnp.minimum(i, n_p - 1), 0)),
            pl.BlockSpec((ct, D_MODEL), lambda i, o, c: (jnp.maximum(i - n_p, 0), 0)),
            pl.BlockSpec((ct, n_info), lambda i, o, c: (i, 0)),
            pl.BlockSpec((ct, n_info), lambda i, o, c: (jnp.minimum(i + 1, n_p + n_s - 1), 0)),
            pl.BlockSpec((1, D_MODEL), lambda i, o, c: (0, 0)),
            pl.BlockSpec(memory_space=pl.ANY),
        ],
        out_specs=[
            pl.BlockSpec((ct, D_MODEL), lambda i, o, c: (jnp.minimum(i, n_p - 1), 0)),
            pl.BlockSpec((ct, D_MODEL), lambda i, o, c: (jnp.maximum(i - n_p, 0), 0)),
        ],
        scratch_shapes=[pltpu.VMEM((3, N_EXPERTS * WIN_ROWS, D_MODEL), BF16),
                        pltpu.VMEM((ct, N_EXPERTS * WIN_ROWS), BF16),
                        pltpu.VMEM((ct, N_EXPERTS * WIN_ROWS), BF16),
                        pltpu.VMEM((ct, D_MODEL), F32),
                        pltpu.SemaphoreType.DMA((3,))],
    )
    return pl.pallas_call(
        functools.partial(_combine_kernel, n_p),
        grid_spec=grid_spec,
        out_shape=[jax.ShapeDtypeStruct(x1_p.shape, F32), jax.ShapeDtypeStruct(x1_s.shape, F32)],
        compiler_params=pltpu.CompilerParams(dimension_semantics=("arbitrary",),
                                             vmem_limit_bytes=VMEM_LIMIT),
        name="combine",
    )(offa, nchunk, x1_p, x1_s, tok_info, tok_info, gfin, yb)


def kernel(x_prompt, x_sample, cache_k, cache_v, state_pool, meta_tokens, norm_attn, w_in, attn_sinks,
           w_pool, pool_scale, w_out, norm_ffn, router_w, router_b, w_gate_up, b_gate_up, w_down, b_down,
           norm_final):
    assert w_in.shape[0] == 1, "single-layer trunk"
    bsz, seq, _ = x_prompt.shape
    assert bsz == 1
    nb = x_sample.shape[0]
    n_tok = seq + nb
    gattn = norm_attn[0].reshape(1, D_MODEL)
    gffn = norm_ffn[0].reshape(1, D_MODEL)
    win = w_in[0].astype(BF16)
    wpool = w_pool[0].astype(BF16)
    wout = w_out[0].astype(BF16)
    pscale = pool_scale[0].reshape(1, POOL_WIDTH)
    rw_t = router_w[0].T
    rw_hi = rw_t.astype(BF16)
    rwt = jnp.stack([rw_hi, (rw_t - rw_hi.astype(F32)).astype(BF16)])
    rb = router_b[0].reshape(N_EXPERTS, 1)
    sinks = attn_sinks[0]

    ck = cache_k[0].reshape(nb, N_META + WINDOW, KV_WIDTH)
    cv = cache_v[0].reshape(nb, N_META + WINDOW, KV_WIDTH)
    (x1_s, h2_s, lgt_s, knew, vnew, pnew) = _sample_mixer(
        x_sample[:, 0], ck, cv, state_pool[0], gattn, win, wpool, pscale, wout, gffn, rwt, rb, sinks)
    (x1_p, h2_all, lgt_all, kmeta, vmeta, ktail, vtail, ptail) = _prompt_mixer(
        x_prompt[0], meta_tokens, gattn, win, wpool, pscale, wout, gffn, rwt, rb, sinks, h2_s, lgt_s)

    gates, col, lrank, lpos, tcar, counts = _router(lgt_all)
    counts = counts[:, 0]
    tcar = tcar[:, :, 0]
    rbk = EXPERT_ROWS
    eids = jnp.arange(N_EXPERTS, dtype=jnp.int32)
    earlier = eids[None, :] < eids[:, None]
    excl_sum = lambda a: jnp.sum(jnp.where(earlier, a[..., None, :], 0), axis=-1)
    padded = (counts + rbk - 1) // rbk * rbk
    pad_start = excl_sum(padded).astype(jnp.int32)
    pad_end = pad_start + padded
    nblk = -(-(n_tok * TOP_K) // rbk) + N_EXPERTS
    cap = nblk * rbk
    block_start = jnp.arange(nblk + 1, dtype=jnp.int32) * rbk
    owns = (pad_start[None, :] <= block_start[:, None]) & (block_start[:, None] < pad_end[None, :])
    nvalid = jnp.sum(jnp.where(owns, jnp.clip(counts[None, :] - (block_start[:, None] - pad_start[None, :]),
                                              0, rbk), 0), axis=1).astype(jnp.int32)
    has_rows = counts > 0
    last_e = jnp.max(jnp.where(has_rows, eids, 0))
    block_expert = jnp.where(jnp.any(owns, axis=1), jnp.sum(jnp.where(owns, eids[None, :], 0), axis=1),
                             last_e).astype(jnp.int32)

    run_len = jnp.concatenate([tcar[1:], counts[None, :]], axis=0) - tcar
    dcar = tcar[::DISPATCH_TILE // COMBINE_TILE]
    drun_len = jnp.concatenate([dcar[1:], counts[None, :]], axis=0) - dcar
    flat = lambda a: a.astype(jnp.int32).reshape(-1)
    xs = _dispatch(h2_all, lpos, flat(excl_sum(drun_len)), flat(drun_len), flat(pad_start[None, :] + dcar),
                   cap)
    expert_pos = excl_sum(has_rows.astype(jnp.int32))
    at_pos = has_rows[None, :] & (expert_pos[None, :] == jnp.arange(N_EXPERTS + 1, dtype=jnp.int32)[:, None])
    expert_list = jnp.where(jnp.any(at_pos, axis=1), jnp.sum(jnp.where(at_pos, eids[None, :], 0), axis=1),
                            -1).astype(jnp.int32)
    block_pos = jnp.sum(jnp.where(block_expert[:, None] == eids[None, :], expert_pos[None, :], 0),
                        axis=1).astype(jnp.int32)
    yb = _experts(xs, block_expert, nvalid, block_pos, expert_list,
                  w_gate_up[0], b_gate_up[0], w_down[0], b_down[0])

    offa = (pad_start[None, :] + (tcar - tcar % WIN_ALIGN)).astype(jnp.int32).reshape(-1)
    nchunk = jnp.maximum(jnp.max((run_len + WIN - 1) // WIN, axis=1), 1).astype(jnp.int32)
    gfin = norm_final.reshape(1, D_MODEL)
    tok_info = jnp.concatenate([col.astype(F32), lrank.astype(F32), gates], axis=0).T
    y_prompt, y_sample = _combine(x1_p, x1_s, tok_info, gfin, yb, offa, nchunk)

    kv_shape = (1, 1, N_META + WINDOW, N_KV_HEADS, HEAD_DIM)
    new_k_p = jnp.concatenate([kmeta, ktail], axis=0).reshape(kv_shape)
    new_v_p = jnp.concatenate([vmeta, vtail], axis=0).reshape(kv_shape)
    new_pool_p = ptail[16 - POOL_STATE:].reshape(1, 1, POOL_STATE, POOL_WIDTH)
    new_k_s = jnp.concatenate([ck[:, :N_META], ck[:, N_META + 1:], knew[:, None]], axis=1).reshape(
        (1, nb, N_META + WINDOW, N_KV_HEADS, HEAD_DIM))
    new_v_s = jnp.concatenate([cv[:, :N_META], cv[:, N_META + 1:], vnew[:, None]], axis=1).reshape(
        (1, nb, N_META + WINDOW, N_KV_HEADS, HEAD_DIM))
    new_pool_s = jnp.concatenate([state_pool[0][:, 1:], pnew[:, None]], axis=1)[None]
    return (y_prompt[None], y_sample[:, None], new_k_p, new_v_p, new_pool_p, new_k_s, new_v_s, new_pool_s)
```

```python
import functools

import jax
import jax.numpy as jnp
import numpy as np
from jax import lax
from jax.experimental import pallas as pl
from jax.experimental.pallas import tpu as pltpu

F32 = jnp.float32
BF16 = jnp.bfloat16

D_MODEL = 1024
N_META = 16
N_HEADS = 8
HEAD_DIM = 64
N_KV_HEADS = 2
GQA_GROUP = N_HEADS // N_KV_HEADS
ATTN_WIDTH = N_HEADS * HEAD_DIM
KV_WIDTH = N_KV_HEADS * HEAD_DIM
WINDOW = 128
POOL_WIDTH = D_MODEL - ATTN_WIDTH
POOL_WINDOWS = (2, 4, 8, 16)
POOL_GROUP_DIM = POOL_WIDTH // len(POOL_WINDOWS)
POOL_STATE = max(POOL_WINDOWS) - 1
N_EXPERTS = 32
TOP_K = 4
D_EXPERT = D_MODEL
SWIGLU_ALPHA = 1.702
SWIGLU_LIMIT = 7.0
NORM_EPS = 1e-5
PAST_LEN = 16384

LANES = 128
QSUB = 64
KEYS_SUB = QSUB + WINDOW
META_PAD = 64
NKEY = META_PAD + KEYS_SUB
MASKED = -1e30
PROMPT_BLOCK = 1024
ROUTE_BLOCK = 384
EXPERT_ROWS = 512
DISPATCH_TILE = 384
PACK_CHUNKS = D_MODEL // 2 // LANES
COMBINE_TILE = 128
WIN = 32
WIN_ALIGN = 16
WIN_ROWS = WIN + WIN_ALIGN
VMEM_LIMIT = 56 * 1024 * 1024


def _rms(x, g):
    return x * lax.rsqrt(jnp.mean(x * x, axis=-1, keepdims=True) + NORM_EPS) * g


def _router_logits(rwt_ref, h2, h2_hi):
    nt = (((1,), (1,)), ((), ()))
    h2_lo = (h2 - h2_hi.astype(F32)).astype(BF16)
    return (lax.dot_general(rwt_ref[0], h2_hi, nt, preferred_element_type=F32)
            + lax.dot_general(rwt_ref[0], h2_lo, nt, preferred_element_type=F32)
            + lax.dot_general(rwt_ref[1], h2_hi, nt, preferred_element_type=F32))


def _dup_halves(a):
    lane = lax.broadcasted_iota(jnp.int32, a.shape, 1)
    r = pltpu.roll(a, HEAD_DIM, axis=1)
    lo = lane < HEAD_DIM
    return jnp.where(lo, a, r), jnp.where(lo, r, a)


def _pool_means(pext_ref, n):
    outs = []
    for gi, w in enumerate(POOL_WINDOWS):
        xg = pext_ref[:, gi * POOL_GROUP_DIM:(gi + 1) * POOL_GROUP_DIM]
        s = xg
        sh = 1
        while sh < w:
            s = s + pltpu.roll(s, sh, axis=0)
            sh *= 2
        outs.append(s[16:] * (1.0 / w) - xg[16:])
    return outs


def _pack_bf16_pairs(h):
    m = h.shape[1] // 2
    lo = pltpu.bitcast(h[:, :m].astype(BF16).astype(F32), jnp.uint32)
    hi = pltpu.bitcast(h[:, m:].astype(BF16).astype(F32), jnp.uint32)
    return lax.shift_right_logical(lo, jnp.uint32(16)) | (hi & jnp.uint32(0xFFFF0000))


def _unpack_bf16_pairs(w):
    lo = pltpu.bitcast(lax.shift_left(w, jnp.uint32(16)), F32).astype(BF16)
    hi = pltpu.bitcast(w & jnp.uint32(0xFFFF0000), F32).astype(BF16)
    return lo, hi


def _prompt_kernel(x_ref, meta_ref, gattn_ref, win_ref, wpool_ref, pscale_ref, wout_ref, gffn_ref,
                   rwt_ref, rb_ref, sink_ref, tbl_ref, tail_h2_ref, tail_lgt_ref,
                   x1_ref, h2_ref, lgt_ref, kmeta_ref, vmeta_ref, ktail_ref, vtail_ref, ptail_ref,
                   k2buf, v2buf, km2, vm2, qbuf, obuf, pext):
    pid = pl.program_id(0)
    n_main = pl.num_programs(0) - 1
    refs = (x_ref, meta_ref, gattn_ref, win_ref, wpool_ref, pscale_ref, wout_ref, gffn_ref,
            rwt_ref, rb_ref, sink_ref, tbl_ref,
            x1_ref, h2_ref, lgt_ref, kmeta_ref, vmeta_ref, ktail_ref, vtail_ref, ptail_ref,
            k2buf, v2buf, km2, vm2, qbuf, obuf, pext)

    @pl.when(pid < n_main)
    def _():
        _prompt_block(*refs)

    @pl.when(pid == n_main)
    def _():
        h2_ref[0:tail_h2_ref.shape[0], :] = tail_h2_ref[...]
        lgt_ref[:, 0:tail_lgt_ref.shape[1]] = tail_lgt_ref[...]


def _prompt_block(x_ref, meta_ref, gattn_ref, win_ref, wpool_ref, pscale_ref, wout_ref, gffn_ref,
                  rwt_ref, rb_ref, sink_ref, tbl_ref,
                  x1_ref, h2_ref, lgt_ref, kmeta_ref, vmeta_ref, ktail_ref, vtail_ref, ptail_ref,
                  k2buf, v2buf, km2, vm2, qbuf, obuf, pext):
    tb = x_ref.shape[0]
    pid = pl.program_id(0)

    @pl.when(pid == 0)
    def _():
        hm = _rms(meta_ref[...], gattn_ref[...]).astype(BF16)
        km = jnp.dot(hm, win_ref[:, ATTN_WIDTH:ATTN_WIDTH + KV_WIDTH], preferred_element_type=F32)
        vm = jnp.dot(hm, win_ref[:, ATTN_WIDTH + KV_WIDTH:ATTN_WIDTH + 2 * KV_WIDTH],
                     preferred_element_type=F32)
        pm = jnp.dot(hm, win_ref[:, ATTN_WIDTH + 2 * KV_WIDTH:], preferred_element_type=F32)
        kmeta_ref[...] = km
        vmeta_ref[...] = vm
        zpad = jnp.zeros((META_PAD - N_META, LANES), F32)
        k0, k1 = _dup_halves(jnp.concatenate([km, zpad], axis=0))
        v0, v1 = _dup_halves(jnp.concatenate([vm, zpad], axis=0))
        km2[0] = k0.astype(BF16)
        km2[1] = k1.astype(BF16)
        vm2[0, :, 0:LANES] = v0.astype(BF16)
        vm2[1, :, 0:LANES] = v1.astype(BF16)
        vm2[:, :, LANES:] = jnp.ones((N_KV_HEADS, META_PAD, LANES), BF16)
        k2buf[:, 0:WINDOW, :] = jnp.zeros((2, WINDOW, LANES), BF16)
        v2buf[:, 0:WINDOW, 0:LANES] = jnp.zeros((2, WINDOW, LANES), BF16)
        v2buf[:, :, LANES:] = jnp.ones((N_KV_HEADS, WINDOW + tb, LANES), BF16)
        pext[0:16, :] = pm

    h = _rms(x_ref[...], gattn_ref[...]).astype(BF16)
    q = jnp.dot(h, win_ref[:, 0:ATTN_WIDTH], preferred_element_type=F32) * (HEAD_DIM ** -0.5)
    lane_t = lax.broadcasted_iota(jnp.int32, (tb, LANES), 1)
    for c in range(N_HEADS // 2):
        tile = q[:, c * LANES:(c + 1) * LANES]
        for a in range(2):
            keep = (lane_t < HEAD_DIM) if a == 0 else (lane_t >= HEAD_DIM)
            piece = jnp.where(keep, tile, 0.0).astype(BF16).reshape(tb // QSUB, QSUB, LANES)
            row = ((c % 2) * 2 + a) * QSUB
            qbuf[c // 2, :, row:row + QSUB, :] = piece
    k = jnp.dot(h, win_ref[:, ATTN_WIDTH:ATTN_WIDTH + KV_WIDTH], preferred_element_type=F32)
    v = jnp.dot(h, win_ref[:, ATTN_WIDTH + KV_WIDTH:ATTN_WIDTH + 2 * KV_WIDTH], preferred_element_type=F32)
    p = jnp.dot(h, win_ref[:, ATTN_WIDTH + 2 * KV_WIDTH:], preferred_element_type=F32)
    ktail_ref[...] = k[tb - WINDOW:]
    vtail_ref[...] = v[tb - WINDOW:]
    ptail_ref[...] = p[tb - 16:]
    k0, k1 = _dup_halves(k)
    v0, v1 = _dup_halves(v)
    k2buf[0, WINDOW:, :] = k0.astype(BF16)
    k2buf[1, WINDOW:, :] = k1.astype(BF16)
    v2buf[0, WINDOW:, 0:LANES] = v0.astype(BF16)
    v2buf[1, WINDOW:, 0:LANES] = v1.astype(BF16)
    pext[16:, :] = p

    lane_q = lax.broadcasted_iota(jnp.int32, (QSUB, LANES), 1)
    lo_q = lane_q < HEAD_DIM

    for u in range(tb // QSUB):
        r0 = u * QSUB
        sel = jnp.where(pid == 0, u + 1, 0) if u < WINDOW // QSUB else 0
        for g in range(N_KV_HEADS):
            qm = qbuf[g, u]
            kwin = jnp.concatenate([km2[g], k2buf[g, r0:r0 + KEYS_SUB, :]], axis=0)
            vwin = jnp.concatenate([vm2[g], v2buf[g, r0:r0 + KEYS_SUB, :]], axis=0)
            s = lax.dot_general(qm, kwin, (((1,), (1,)), ((), ())), preferred_element_type=F32)
            s = s + tbl_ref[sel, g]
            sink = sink_ref[g]
            m = jnp.maximum(jnp.max(s, axis=1, keepdims=True), sink)
            e = jnp.exp(s - m).astype(BF16)
            r = jnp.dot(e, vwin, preferred_element_type=F32)
            o = r[:, 0:LANES] / (r[:, LANES:] + jnp.exp(sink - m))
            o0 = jnp.where(lo_q, o[0:QSUB], o[QSUB:2 * QSUB])
            o1 = jnp.where(lo_q, o[2 * QSUB:3 * QSUB], o[3 * QSUB:])
            obuf[r0:r0 + QSUB, (2 * g) * LANES:(2 * g + 1) * LANES] = o0.astype(BF16)
            obuf[r0:r0 + QSUB, (2 * g + 1) * LANES:(2 * g + 2) * LANES] = o1.astype(BF16)

    pooled = _pool_means(pext, tb)
    for gi in range(len(POOL_WINDOWS)):
        y = jnp.dot(pooled[gi].astype(BF16), wpool_ref[gi], preferred_element_type=F32)
        y = y * pscale_ref[:, gi * POOL_GROUP_DIM:(gi + 1) * POOL_GROUP_DIM]
        obuf[:, ATTN_WIDTH + gi * POOL_GROUP_DIM:ATTN_WIDTH + (gi + 1) * POOL_GROUP_DIM] = y.astype(BF16)

    k2buf[:, 0:WINDOW, :] = k2buf[:, tb:tb + WINDOW, :]
    v2buf[:, 0:WINDOW, 0:LANES] = v2buf[:, tb:tb + WINDOW, 0:LANES]
    pext[0:16, :] = pext[tb:tb + 16, :]

    x1 = x_ref[...] + jnp.dot(obuf[...], wout_ref[...], preferred_element_type=F32)
    x1_ref[...] = x1
    h2 = _rms(x1, gffn_ref[...])
    h2_hi = h2.astype(BF16)
    h2_ref[...] = h2_hi
    lgt_ref[...] = _router_logits(rwt_ref, h2, h2_hi) + rb_ref[...]


def _attn_tables(sinks):
    i = np.arange(QSUB)[:, None]
    j = np.arange(NKEY)[None, :]
    jb = j - META_PAD
    rel = i + WINDOW - jb
    band_ok = (jb >= 0) & (rel >= 0) & (rel <= WINDOW)
    meta_ok = (j < N_META) & (i >= 0)
    slopes = np.exp2(-8.0 * np.arange(1, N_HEADS + 1) / N_HEADS)
    tbl = np.empty((3, N_KV_HEADS, GQA_GROUP * QSUB, NKEY), np.float32)
    for var in range(3):
        ok = band_ok if var == 0 else band_ok & (jb >= WINDOW - (var - 1) * QSUB)
        for g in range(N_KV_HEADS):
            for a in range(GQA_GROUP):
                hd = g * GQA_GROUP + a
                bias = np.where(ok, -slopes[hd] * rel, MASKED)
                bias = np.where(meta_ok, 0.0, bias)
                tbl[var, g, a * QSUB:(a + 1) * QSUB] = bias
    sink_col = jnp.repeat(sinks.astype(F32).reshape(N_KV_HEADS, GQA_GROUP, 1), QSUB, axis=2)
    return jnp.asarray(tbl), sink_col.reshape(N_KV_HEADS, GQA_GROUP * QSUB, 1)


def _prompt_mixer(x, meta, gattn, win, wpool, pscale, wout, gffn, rwt, rb, sinks, tail_h2, tail_lgt):
    seq = x.shape[0]
    tb = PROMPT_BLOCK
    n_tail = tail_h2.shape[0]
    assert seq % tb == 0 and tb % WINDOW == 0 and n_tail <= tb
    nblk = seq // tb
    n_tok = seq + n_tail
    tbl, sink_col = _attn_tables(sinks)
    full = lambda *shape: pl.BlockSpec(shape, lambda i: (0,) * len(shape))
    main = lambda i: (jnp.minimum(i, nblk - 1), 0)
    in_width = win.shape[1]
    return pl.pallas_call(
        _prompt_kernel,
        grid=(nblk + 1,),
        in_specs=[
            pl.BlockSpec((tb, D_MODEL), main),
            full(N_META, D_MODEL), full(1, D_MODEL), full(D_MODEL, in_width),
            full(len(POOL_WINDOWS), POOL_GROUP_DIM, POOL_GROUP_DIM), full(1, POOL_WIDTH),
            full(D_MODEL, D_MODEL), full(1, D_MODEL), full(2, N_EXPERTS, D_MODEL), full(N_EXPERTS, 1),
            full(N_KV_HEADS, GQA_GROUP * QSUB, 1), full(3, N_KV_HEADS, GQA_GROUP * QSUB, NKEY),
            full(n_tail, D_MODEL), full(N_EXPERTS, n_tail),
        ],
        out_specs=[
            pl.BlockSpec((tb, D_MODEL), main),
            pl.BlockSpec((tb, D_MODEL), lambda i: (i, 0)),
            pl.BlockSpec((N_EXPERTS, tb), lambda i: (0, i)),
            full(N_META, KV_WIDTH), full(N_META, KV_WIDTH),
            full(WINDOW, KV_WIDTH), full(WINDOW, KV_WIDTH), full(16, POOL_WIDTH),
        ],
        out_shape=[
            jax.ShapeDtypeStruct((seq, D_MODEL), F32),
            jax.ShapeDtypeStruct((n_tok, D_MODEL), BF16),
            jax.ShapeDtypeStruct((N_EXPERTS, n_tok), F32),
            jax.ShapeDtypeStruct((N_META, KV_WIDTH), F32),
            jax.ShapeDtypeStruct((N_META, KV_WIDTH), F32),
            jax.ShapeDtypeStruct((WINDOW, KV_WIDTH), F32),
            jax.ShapeDtypeStruct((WINDOW, KV_WIDTH), F32),
            jax.ShapeDtypeStruct((16, POOL_WIDTH), F32),
        ],
        scratch_shapes=[
            pltpu.VMEM((N_KV_HEADS, WINDOW + tb, LANES), BF16),
            pltpu.VMEM((N_KV_HEADS, WINDOW + tb, 2 * LANES), BF16),
            pltpu.VMEM((N_KV_HEADS, META_PAD, LANES), BF16),
            pltpu.VMEM((N_KV_HEADS, META_PAD, 2 * LANES), BF16),
            pltpu.VMEM((N_KV_HEADS, tb // QSUB, GQA_GROUP * QSUB, LANES), BF16),
            pltpu.VMEM((tb, D_MODEL), BF16),
            pltpu.VMEM((16 + tb, POOL_WIDTH), F32),
        ],
        compiler_params=pltpu.CompilerParams(dimension_semantics=("arbitrary",),
                                             vmem_limit_bytes=VMEM_LIMIT),
        name="prompt_mixer",
    )(x, meta, gattn, win, wpool, pscale, wout, gffn, rwt, rb, sink_col, tbl, tail_h2, tail_lgt)


def _sample_kernel(x_ref, ck_ref, cv_ref, sp_ref, gattn_ref, win_ref, wpool_ref, pscale_ref, wout_ref,
                   gffn_ref, rwt_ref, rb_ref, sinkc_ref, bias_ref,
                   x1_ref, h2_ref, lgt_ref, knew_ref, vnew_ref, pnew_ref,
                   qm_buf, r_buf, obuf):
    nb = x_ref.shape[0]
    x = x_ref[...]
    h = _rms(x, gattn_ref[...]).astype(BF16)
    q = jnp.dot(h, win_ref[:, 0:ATTN_WIDTH], preferred_element_type=F32) * (HEAD_DIM ** -0.5)
    k = jnp.dot(h, win_ref[:, ATTN_WIDTH:ATTN_WIDTH + KV_WIDTH], preferred_element_type=F32)
    v = jnp.dot(h, win_ref[:, ATTN_WIDTH + KV_WIDTH:ATTN_WIDTH + 2 * KV_WIDTH], preferred_element_type=F32)
    p = jnp.dot(h, win_ref[:, ATTN_WIDTH + 2 * KV_WIDTH:], preferred_element_type=F32)
    knew_ref[...] = k
    vnew_ref[...] = v
    pnew_ref[...] = p

    lane = lax.broadcasted_iota(jnp.int32, (nb, LANES), 1)
    lo = lane < HEAD_DIM
    for hd in range(N_HEADS):
        tile = q[:, (hd // 2) * LANES:(hd // 2 + 1) * LANES]
        if (hd % 2) != (hd // GQA_GROUP):
            tile = pltpu.roll(tile, HEAD_DIM, axis=1)
        keep_lo = (hd // GQA_GROUP) == 0
        qm_buf[:, hd, :] = jnp.where(lo if keep_lo else jnp.logical_not(lo), tile, 0.0)

    unroll = 8
    assert nb % unroll == 0
    nt = (((1,), (1,)), ((), ()))
    bias = jnp.concatenate([bias_ref[...]] * unroll, axis=0)
    sink = jnp.concatenate([sinkc_ref[...]] * unroll, axis=0)

    def batch_group(gidx, carry):
        bs = [gidx * unroll + j for j in range(unroll)]
        qms = [qm_buf[b] for b in bs]
        s = jnp.concatenate(
            [lax.dot_general(qm.astype(BF16), ck_ref[b].astype(BF16), nt, preferred_element_type=F32)
             for qm, b in zip(qms, bs)], axis=0) + bias
        s_self = jnp.concatenate(
            [jnp.sum(qm * knew_ref[pl.ds(b, 1), :], axis=1, keepdims=True) for qm, b in zip(qms, bs)],
            axis=0)
        m = jnp.maximum(jnp.maximum(jnp.max(s, axis=1, keepdims=True), s_self), sink)
        e = jnp.exp(s - m)
        e_self = jnp.exp(s_self - m)
        inv_den = 1.0 / (jnp.sum(e, axis=1, keepdims=True) + e_self + jnp.exp(sink - m))
        for j, b in enumerate(bs):
            rows = slice(j * N_HEADS, (j + 1) * N_HEADS)
            r = jnp.dot(e[rows].astype(BF16), cv_ref[b].astype(BF16), preferred_element_type=F32)
            r = r + e_self[rows] * vnew_ref[pl.ds(b, 1), :]
            r_buf[b] = r * inv_den[rows]
        return carry

    lax.fori_loop(0, nb // unroll, batch_group, 0)

    for c in range(N_HEADS // 2):
        halves = []
        for a in range(2):
            hd = 2 * c + a
            t = r_buf[:, hd, :]
            if (hd // GQA_GROUP) != a:
                t = pltpu.roll(t, HEAD_DIM, axis=1)
            halves.append(t)
        obuf[:, c * LANES:(c + 1) * LANES] = jnp.where(lo, halves[0], halves[1]).astype(BF16)

    for gi, w in enumerate(POOL_WINDOWS):
        cols = slice(gi * POOL_GROUP_DIM, (gi + 1) * POOL_GROUP_DIM)
        pg = p[:, cols]
        acc = pg
        for d in range(1, w):
            acc = acc + sp_ref[:, POOL_STATE - d, cols]
        pooled = acc * (1.0 / w) - pg
        y = jnp.dot(pooled.astype(BF16), wpool_ref[gi], preferred_element_type=F32) * pscale_ref[:, cols]
        obuf[:, ATTN_WIDTH + gi * POOL_GROUP_DIM:ATTN_WIDTH + (gi + 1) * POOL_GROUP_DIM] = y.astype(BF16)

    x1 = x + jnp.dot(obuf[...], wout_ref[...], preferred_element_type=F32)
    x1_ref[...] = x1
    h2 = _rms(x1, gffn_ref[...])
    h2_hi = h2.astype(BF16)
    h2_ref[...] = h2_hi
    lgt_ref[...] = _router_logits(rwt_ref, h2, h2_hi) + rb_ref[...]


def _sample_mixer(x, ck, cv, sp, gattn, win, wpool, pscale, wout, gffn, rwt, rb, sinks):
    nb = x.shape[0]
    rows = ck.shape[1]
    slopes = np.exp2(-8.0 * np.arange(1, N_HEADS + 1) / N_HEADS)
    dist = np.concatenate([np.zeros(N_META), WINDOW - np.arange(WINDOW)])
    bias = jnp.asarray((-slopes[:, None] * dist[None, :]).astype(np.float32))
    vm = pl.BlockSpec(memory_space=pltpu.VMEM)
    return pl.pallas_call(
        _sample_kernel,
        in_specs=[vm] * 14,
        out_specs=[vm] * 6,
        out_shape=[
            jax.ShapeDtypeStruct((nb, D_MODEL), F32),
            jax.ShapeDtypeStruct((nb, D_MODEL), BF16),
            jax.ShapeDtypeStruct((N_EXPERTS, nb), F32),
            jax.ShapeDtypeStruct((nb, KV_WIDTH), F32),
            jax.ShapeDtypeStruct((nb, KV_WIDTH), F32),
            jax.ShapeDtypeStruct((nb, POOL_WIDTH), F32),
        ],
        scratch_shapes=[
            pltpu.VMEM((nb, N_HEADS, LANES), F32),
            pltpu.VMEM((nb, N_HEADS, LANES), F32),
            pltpu.VMEM((nb, D_MODEL), BF16),
        ],
        compiler_params=pltpu.CompilerParams(vmem_limit_bytes=VMEM_LIMIT),
        name="sample_mixer",
    )(x, ck, cv, sp, gattn, win, wpool, pscale, wout, gffn, rwt, rb,
      sinks.astype(F32).reshape(N_HEADS, 1), bias)


def _router_kernel(lg_ref, tri_ref, low_ref, gate_ref, col_ref, lrank_ref, lpos_ref,
                   tcar_ref, cnt_ref, carry):
    tr = lg_ref.shape[1]

    @pl.when(pl.program_id(0) == 0)
    def _():
        carry[...] = jnp.zeros_like(carry)

    work = lg_ref[...]
    eio = lax.broadcasted_iota(jnp.int32, work.shape, 0).astype(F32)
    sels, vals, idxs = [], [], []
    for _k in range(TOP_K):
        mx = jnp.max(work, axis=0, keepdims=True)
        idx = jnp.min(jnp.where(work == mx, eio, float(N_EXPERTS)), axis=0, keepdims=True)
        sel = eio == idx
        sels.append(sel)
        vals.append(mx)
        idxs.append(idx)
        work = jnp.where(sel, -jnp.inf, work)
    exps = [jnp.exp(vk - vals[0]) for vk in vals]
    tot = exps[0] + exps[1] + exps[2] + exps[3]
    onehot = jnp.zeros(work.shape, F32)
    for sel in sels:
        onehot = onehot + sel.astype(F32)
    before = jnp.dot(onehot.astype(BF16), tri_ref[...], preferred_element_type=F32) + carry[...]
    for kk in range(TOP_K):
        gate_ref[pl.ds(kk, 1), :] = exps[kk] / tot
    for j in range(tr // COMBINE_TILE):
        cols = slice(j * COMBINE_TILE, (j + 1) * COMBINE_TILE)
        tc = before[:, j * COMBINE_TILE:j * COMBINE_TILE + 1]
        tcar_ref[j] = tc.astype(jnp.int32)
        slack = tc - WIN_ALIGN * jnp.floor(tc * (1.0 / WIN_ALIGN))
        local = before[:, cols] - tc
        for kk in range(TOP_K):
            selk = sels[kk][:, cols]
            lr = jnp.sum(jnp.where(selk, local, 0.0), axis=0, keepdims=True)
            sl = jnp.sum(jnp.where(selk, slack, 0.0), axis=0, keepdims=True)
            lrank_ref[pl.ds(kk, 1), cols] = lr.astype(jnp.int32)
            col_ref[pl.ds(kk, 1), cols] = (idxs[kk][:, cols] * float(WIN_ROWS) + sl + lr).astype(jnp.int32)
    for j in range(tr // DISPATCH_TILE):
        cols = slice(j * DISPATCH_TILE, (j + 1) * DISPATCH_TILE)
        local = before[:, cols] - before[:, j * DISPATCH_TILE:j * DISPATCH_TILE + 1]
        tile_cnt = jnp.broadcast_to(jnp.sum(onehot[:, cols], axis=1, keepdims=True), local.shape)
        cnt_hi = jnp.floor(tile_cnt * (1.0 / 256.0))
        cnt_lo = tile_cnt - 256.0 * cnt_hi
        run_start = (256.0 * jnp.dot(low_ref[...], cnt_hi.astype(BF16), preferred_element_type=F32)
                     + jnp.dot(low_ref[...], cnt_lo.astype(BF16), preferred_element_type=F32))
        for kk in range(TOP_K):
            lp = jnp.sum(jnp.where(sels[kk][:, cols], run_start + local, 0.0), axis=0, keepdims=True)
            lpos_ref[pl.ds(kk, 1), cols] = lp.astype(jnp.int32)
    carry[...] = carry[...] + jnp.sum(onehot, axis=1, keepdims=True)
    cnt_ref[...] = carry[...].astype(jnp.int32)


def _router(logits_t):
    n = logits_t.shape[1]
    tr = ROUTE_BLOCK
    assert n % tr == 0
    tri = jnp.asarray(np.triu(np.ones((tr, tr), np.float32), k=1), BF16)
    low = jnp.asarray(np.tril(np.ones((N_EXPERTS, N_EXPERTS), np.float32), k=-1), BF16)
    per_tok = pl.BlockSpec((TOP_K, tr), lambda i: (0, i))
    return pl.pallas_call(
        _router_kernel,
        grid=(n // tr,),
        in_specs=[pl.BlockSpec((N_EXPERTS, tr), lambda i: (0, i)),
                  pl.BlockSpec((tr, tr), lambda i: (0, 0)),
                  pl.BlockSpec((N_EXPERTS, N_EXPERTS), lambda i: (0, 0))],
        out_specs=[per_tok, per_tok, per_tok, per_tok,
                   pl.BlockSpec((tr // COMBINE_TILE, N_EXPERTS, 1), lambda i: (i, 0, 0)),
                   pl.BlockSpec((N_EXPERTS, 1), lambda i: (0, 0))],
        out_shape=[jax.ShapeDtypeStruct((TOP_K, n), F32),
                   jax.ShapeDtypeStruct((TOP_K, n), jnp.int32),
                   jax.ShapeDtypeStruct((TOP_K, n), jnp.int32),
                   jax.ShapeDtypeStruct((TOP_K, n), jnp.int32),
                   jax.ShapeDtypeStruct((n // COMBINE_TILE, N_EXPERTS, 1), jnp.int32),
                   jax.ShapeDtypeStruct((N_EXPERTS, 1), jnp.int32)],
        scratch_shapes=[pltpu.VMEM((N_EXPERTS, 1), F32)],
        compiler_params=pltpu.CompilerParams(dimension_semantics=("arbitrary",)),
        name="router",
    )(logits_t, tri, low)


def _dispatch_kernel(lstart_ref, cnt_ref, dst_ref, h2_ref, lpos_ref, xs_hbm, stg0, stg1, sem):
    i = pl.program_id(0)
    n_tiles = pl.num_programs(0)
    dt = h2_ref.shape[0]
    rows = dt * TOP_K
    slot = i % 2
    stgs = (stg0, stg1)
    pieces = [p for p in (256, 128, 64, 32, 16, 8, 4, 2, 1) if p <= dt]
    assert dt < 512

    def drain(s):
        pltpu.make_async_copy(stgs[s], xs_hbm.at[pl.ds(0, rows)], sem.at[s]).wait()

    def issue_runs(tile, live, s):
        for e in range(N_EXPERTS):
            n = jnp.where(live, cnt_ref[tile * N_EXPERTS + e], 0)
            src0 = lstart_ref[tile * N_EXPERTS + e]
            dst0 = dst_ref[tile * N_EXPERTS + e]
            for piece in pieces:
                off = n & ~jnp.int32(2 * piece - 1)

                @pl.when((n & piece) != 0)
                def _(off=off, piece=piece, src0=src0, dst0=dst0):
                    pltpu.make_async_copy(stgs[s].at[pl.ds(src0 + off, piece)],
                                          xs_hbm.at[pl.ds(dst0 + off, piece)], sem.at[s]).start()

    def sort_tile(s):
        rid = lax.broadcasted_iota(jnp.int32, (rows, dt), 0)
        hit = rid == lpos_ref[0:1, :]
        for kk in range(1, TOP_K):
            hit = jnp.logical_or(hit, rid == lpos_ref[kk:kk + 1, :])
        perm = jnp.where(hit, 1.0, 0.0).astype(BF16)
        srt = jnp.dot(perm, h2_ref[...], preferred_element_type=F32)
        packed = _pack_bf16_pairs(srt)
        for c in range(PACK_CHUNKS):
            stgs[s][:, c, :] = packed[:, c * LANES:(c + 1) * LANES]

    for s in range(2):
        @pl.when(slot == s)
        def _(s=s):
            @pl.when(i >= 2)
            def _():
                drain(s)

            issue_runs(jnp.maximum(i - 1, 0), i >= 1, 1 - s)
            sort_tile(s)

            @pl.when(i == n_tiles - 1)
            def _():
                issue_runs(i, True, s)
                drain(s)

                @pl.when(n_tiles >= 2)
                def _():
                    drain(1 - s)


def _dispatch(h2, lpos, lstart, cnt, dst, cap):
    n_tok = h2.shape[0]
    dt = DISPATCH_TILE
    assert n_tok % dt == 0
    grid_spec = pltpu.PrefetchScalarGridSpec(
        num_scalar_prefetch=3,
        grid=(n_tok // dt,),
        in_specs=[pl.BlockSpec((dt, D_MODEL), lambda i, a, b, c: (i, 0)),
                  pl.BlockSpec((TOP_K, dt), lambda i, a, b, c: (0, i)),
                  ],
        out_specs=pl.BlockSpec(memory_space=pl.ANY),
        scratch_shapes=[pltpu.VMEM((dt * TOP_K, PACK_CHUNKS, LANES), jnp.uint32),
                        pltpu.VMEM((dt * TOP_K, PACK_CHUNKS, LANES), jnp.uint32),
                        pltpu.SemaphoreType.DMA((2,))],
    )
    return pl.pallas_call(
        _dispatch_kernel,
        grid_spec=grid_spec,
        out_shape=jax.ShapeDtypeStruct((cap, PACK_CHUNKS, LANES), jnp.uint32),
        compiler_params=pltpu.CompilerParams(dimension_semantics=("arbitrary",),
                                             vmem_limit_bytes=VMEM_LIMIT),
        name="dispatch",
    )(lstart, cnt, dst, h2, lpos)


def _expert_kernel(n_xblocks, bexp_ref, nvalid_ref, epos_ref, elist_ref,
                   x_hbm, wgu_hbm, bgu_ref, wd_hbm, bd_ref, y_ref,
                   wgu_f32, wd_f32, wgu_bf, wd_bf, xbuf, xsem, wsem):
    i = pl.program_id(0)
    rb = y_ref.shape[0]
    nvalid = nvalid_ref[i]
    pos = epos_ref[i]
    fresh = jnp.logical_or(i == 0, pos != epos_ref[jnp.maximum(i - 1, 0)])
    slot = i % 2

    def x_copies(blk, s):
        return [pltpu.make_async_copy(x_hbm.at[pl.ds(blk * rb, rb), c, :],
                                      xbuf.at[s, :, pl.ds(c * LANES, LANES)], xsem.at[s])
                for c in range(PACK_CHUNKS)]

    def w_copies(p):
        e = elist_ref[p]
        s = p % 2
        return [pltpu.make_async_copy(wgu_hbm.at[e], wgu_f32.at[s], wsem.at[s, 0]),
                pltpu.make_async_copy(wd_hbm.at[e], wd_f32.at[s], wsem.at[s, 1])]

    @pl.when(i == 0)
    def _():
        for cp in x_copies(0, 0):
            cp.start()

        @pl.when(nvalid > 0)
        def _():
            for cp in w_copies(0):
                cp.start()

    @pl.when(i + 1 < n_xblocks)
    def _():
        for cp in x_copies(i + 1, 1 - slot):
            cp.start()

    @pl.when(jnp.logical_and(fresh, nvalid > 0))
    def _():
        @pl.when(elist_ref[pos + 1] >= 0)
        def _():
            for cp in w_copies(pos + 1):
                cp.start()

        for cp in w_copies(pos):
            cp.wait()
        ws = pos % 2
        chunk = 32

        def cast_gu(r, c):
            r0 = pl.multiple_of(r * chunk, chunk)
            wgu_bf[pl.ds(r0, chunk), :] = wgu_f32[ws, pl.ds(r0, chunk), :].astype(BF16)
            return c

        def cast_d(r, c):
            r0 = pl.multiple_of(r * chunk, chunk)
            wd_bf[pl.ds(r0, chunk), :] = wd_f32[ws, pl.ds(r0, chunk), :].astype(BF16)
            return c

        lax.fori_loop(0, D_MODEL // chunk, cast_gu, 0)
        lax.fori_loop(0, D_EXPERT // chunk, cast_d, 0)

    @pl.when(i < n_xblocks)
    def _():
        for cp in x_copies(i, slot):
            cp.wait()

    def ffn(rows):
        xw = xbuf[slot, 0:rows, :]
        xw = jnp.where(lax.broadcasted_iota(jnp.int32, xw.shape, 0) < nvalid, xw, jnp.uint32(0))
        xlo, xhi = _unpack_bf16_pairs(xw)
        half = D_MODEL // 2
        g = (jnp.dot(xlo, wgu_bf[0:half, 0:D_EXPERT], preferred_element_type=F32)
             + jnp.dot(xhi, wgu_bf[half:, 0:D_EXPERT], preferred_element_type=F32)
             + bgu_ref[0, :, 0:D_EXPERT])
        u = (jnp.dot(xlo, wgu_bf[0:half, D_EXPERT:], preferred_element_type=F32)
             + jnp.dot(xhi, wgu_bf[half:, D_EXPERT:], preferred_element_type=F32)
             + bgu_ref[0, :, D_EXPERT:])
        g = jnp.minimum(g, SWIGLU_LIMIT)
        u = jnp.clip(u, -SWIGLU_LIMIT, SWIGLU_LIMIT)
        act = g * (1.0 / (1.0 + jnp.exp(-SWIGLU_ALPHA * g))) * (u + 1.0)
        y = jnp.dot(act.astype(BF16), wd_bf[...], preferred_element_type=F32) + bd_ref[0]
        row = lax.broadcasted_iota(jnp.int32, y.shape, 0)
        y_ref[0:rows, :] = jnp.where(row < nvalid, y, 0.0).astype(BF16)
        if rows < rb:
            y_ref[rows:, :] = jnp.zeros((rb - rows, D_MODEL), BF16)

    @pl.when(nvalid > rb // 2)
    def _():
        ffn(rb)

    @pl.when(jnp.logical_and(nvalid > 0, nvalid <= rb // 2))
    def _():
        ffn(rb // 2)

    @pl.when(nvalid == 0)
    def _():
        y_ref[...] = jnp.zeros_like(y_ref)


def _experts(xs, block_expert, nvalid, block_pos, expert_list, wgu, bgu, wd, bd):
    rb = EXPERT_ROWS
    n_xblocks = xs.shape[0] // rb
    nblk = n_xblocks + 1
    any_space = pl.BlockSpec(memory_space=pl.ANY)
    grid_spec = pltpu.PrefetchScalarGridSpec(
        num_scalar_prefetch=4,
        grid=(nblk,),
        in_specs=[
            any_space,
            any_space,
            pl.BlockSpec((1, 1, 2 * D_EXPERT), lambda i, be, nu, ep, el: (be[i], 0, 0)),
            any_space,
            pl.BlockSpec((1, 1, D_MODEL), lambda i, be, nu, ep, el: (be[i], 0, 0)),
        ],
        out_specs=pl.BlockSpec((rb, D_MODEL), lambda i, be, nu, ep, el: (i, 0)),
        scratch_shapes=[pltpu.VMEM((2, D_MODEL, 2 * D_EXPERT), F32),
                        pltpu.VMEM((2, D_EXPERT, D_MODEL), F32),
                        pltpu.VMEM((D_MODEL, 2 * D_EXPERT), BF16),
                        pltpu.VMEM((D_EXPERT, D_MODEL), BF16),
                        pltpu.VMEM((2, rb, D_MODEL // 2), jnp.uint32),
                        pltpu.SemaphoreType.DMA((2,)),
                        pltpu.SemaphoreType.DMA((2, 2))],
    )
    return pl.pallas_call(
        functools.partial(_expert_kernel, n_xblocks),
        grid_spec=grid_spec,
        out_shape=jax.ShapeDtypeStruct((nblk * rb, D_MODEL), BF16),
        compiler_params=pltpu.CompilerParams(dimension_semantics=("arbitrary",),
                                             vmem_limit_bytes=VMEM_LIMIT),
        name="experts",
    )(block_expert, nvalid, block_pos, expert_list, xs, wgu, bgu.reshape(N_EXPERTS, 1, 2 * D_EXPERT), wd,
      bd.reshape(N_EXPERTS, 1, D_MODEL))


def _combine_kernel(n_prompt_tiles, offa_ref, nchunk_ref,
                    x1p_ref, x1s_ref, info_ref, info_next_ref, gfin_ref, yb_hbm,
                    outp_ref, outs_ref, ybuf, gbuf0, gbuf1, acc_ref, sem):
    i = pl.program_id(0)
    n_tiles = pl.num_programs(0)

    def window_copy(tile, chunk, e, slot):
        base = pl.multiple_of(offa_ref[tile * N_EXPERTS + e] + chunk * WIN, WIN_ALIGN)
        return pltpu.make_async_copy(yb_hbm.at[pl.ds(base, WIN_ROWS), :],
                                     ybuf.at[slot, pl.ds(e * WIN_ROWS, WIN_ROWS), :],
                                     sem.at[slot])

    def start_windows(tile, chunk, slot):
        for e in range(N_EXPERTS):
            window_copy(tile, chunk, e, slot).start()

    def wait_windows(slot):
        pltpu.make_async_copy(yb_hbm.at[pl.ds(0, N_EXPERTS * WIN_ROWS), :], ybuf.at[slot],
                              sem.at[slot]).wait()

    slot = i % 2

    @pl.when(i == 0)
    def _():
        start_windows(0, 0, 0)

    lane = lax.broadcasted_iota(jnp.int32, (COMBINE_TILE, N_EXPERTS * WIN_ROWS), 1)

    def gate_matrix(ref, chunk):
        g = jnp.zeros(lane.shape, F32)
        for kk in range(TOP_K):
            lr = ref[:, TOP_K + kk:TOP_K + kk + 1]
            in_chunk = jnp.logical_and(lr >= chunk * WIN, lr < chunk * WIN + WIN)
            colk = jnp.where(in_chunk, ref[:, kk:kk + 1] - chunk * WIN, -1.0).astype(jnp.int32)
            g = jnp.where(lane == colk, ref[:, 2 * TOP_K + kk:2 * TOP_K + kk + 1], g)
        return g.astype(BF16)

    def moe_rows(gm, buf):
        return jnp.dot(gm, ybuf[buf], preferred_element_type=F32)

    @pl.when(i == 0)
    def _():
        gbuf0[...] = gate_matrix(info_ref, 0)

    def main(s):
        start_windows(jnp.minimum(i + 1, n_tiles - 1), 0, 1 - s)
        wait_windows(s)
        g_cur, g_nxt = (gbuf0, gbuf1) if s == 0 else (gbuf1, gbuf0)
        acc_ref[...] = moe_rows(g_cur[...], s)
        g_nxt[...] = gate_matrix(info_next_ref, 0)

    for s in range(2):
        @pl.when(slot == s)
        def _(s=s):
            main(s)

    @pl.when(i == n_tiles - 1)
    def _():
        wait_windows(1 - slot)

    def extra_chunk(j, c):
        start_windows(i, j, 2)
        wait_windows(2)
        acc_ref[...] += moe_rows(gate_matrix(info_ref, j), 2)
        return c

    lax.fori_loop(1, nchunk_ref[i], extra_chunk, 0)

    @pl.when(i < n_prompt_tiles)
    def _():
        outp_ref[...] = _rms(x1p_ref[...] + acc_ref[...], gfin_ref[...])

    @pl.when(i >= n_prompt_tiles)
    def _():
        outs_ref[...] = _rms(x1s_ref[...] + acc_ref[...], gfin_ref[...])


def _combine(x1_p, x1_s, tok_info, gfin, yb, offa, nchunk):
    ct = COMBINE_TILE
    n_p, n_s = x1_p.shape[0] // ct, x1_s.shape[0] // ct
    assert x1_p.shape[0] % ct == 0 and x1_s.shape[0] % ct == 0 and n_s >= 1
    n_info = tok_info.shape[1]
    grid_spec = pltpu.PrefetchScalarGridSpec(
        num_scalar_prefetch=2,
        grid=(n_p + n_s,),
        in_specs=[
            pl.BlockSpec((ct, D_MODEL), lambda i, o, c: (jnp.minimum(i, n_p - 1), 0)),
            pl.BlockSpec((ct, D_MODEL), lambda i, o, c: (jnp.maximum(i - n_p, 0), 0)),
            pl.BlockSpec((ct, n_info), lambda i, o, c: (i, 0)),
            pl.BlockSpec((ct, n_info), lambda i, o, c: (jnp.minimum(i + 1, n_p + n_s - 1), 0)),
            pl.BlockSpec((1, D_MODEL), lambda i, o, c: (0, 0)),
            pl.BlockSpec(memory_space=pl.ANY),
        ],
        out_specs=[
            pl.BlockSpec((ct, D_MODEL), lambda i, o, c: (jnp.minimum(i, n_p - 1), 0)),
            pl.BlockSpec((ct, D_MODEL), lambda i, o, c: (jnp.maximum(i - n_p, 0), 0)),
        ],
        scratch_shapes=[pltpu.VMEM((3, N_EXPERTS * WIN_ROWS, D_MODEL), BF16),
                        pltpu.VMEM((ct, N_EXPERTS * WIN_ROWS), BF16),
                        pltpu.VMEM((ct, N_EXPERTS * WIN_ROWS), BF16),
                        pltpu.VMEM((ct, D_MODEL), F32),
                        pltpu.SemaphoreType.DMA((3,))],
    )
    return pl.pallas_call(
        functools.partial(_combine_kernel, n_p),
        grid_spec=grid_spec,
        out_shape=[jax.ShapeDtypeStruct(x1_p.shape, F32), jax.ShapeDtypeStruct(x1_s.shape, F32)],
        compiler_params=pltpu.CompilerParams(dimension_semantics=("arbitrary",),
                                             vmem_limit_bytes=VMEM_LIMIT),
        name="combine",
    )(offa, nchunk, x1_p, x1_s, tok_info, tok_info, gfin, yb)


def kernel(x_prompt, x_sample, cache_k, cache_v, state_pool, meta_tokens, norm_attn, w_in, attn_sinks,
           w_pool, pool_scale, w_out, norm_ffn, router_w, router_b, w_gate_up, b_gate_up, w_down, b_down,
           norm_final):
    assert w_in.shape[0] == 1, "single-layer trunk"
    bsz, seq, _ = x_prompt.shape
    assert bsz == 1
    nb = x_sample.shape[0]
    n_tok = seq + nb
    gattn = norm_attn[0].reshape(1, D_MODEL)
    gffn = norm_ffn[0].reshape(1, D_MODEL)
    win = w_in[0].astype(BF16)
    wpool = w_pool[0].astype(BF16)
    wout = w_out[0].astype(BF16)
    pscale = pool_scale[0].reshape(1, POOL_WIDTH)
    rw_t = router_w[0].T
    rw_hi = rw_t.astype(BF16)
    rwt = jnp.stack([rw_hi, (rw_t - rw_hi.astype(F32)).astype(BF16)])
    rb = router_b[0].reshape(N_EXPERTS, 1)
    sinks = attn_sinks[0]

    ck = cache_k[0].reshape(nb, N_META + WINDOW, KV_WIDTH)
    cv = cache_v[0].reshape(nb, N_META + WINDOW, KV_WIDTH)
    (x1_s, h2_s, lgt_s, knew, vnew, pnew) = _sample_mixer(
        x_sample[:, 0], ck, cv, state_pool[0], gattn, win, wpool, pscale, wout, gffn, rwt, rb, sinks)
    (x1_p, h2_all, lgt_all, kmeta, vmeta, ktail, vtail, ptail) = _prompt_mixer(
        x_prompt[0], meta_tokens, gattn, win, wpool, pscale, wout, gffn, rwt, rb, sinks, h2_s, lgt_s)

    gates, col, lrank, lpos, tcar, counts = _router(lgt_all)
    counts = counts[:, 0]
    tcar = tcar[:, :, 0]
    rbk = EXPERT_ROWS
    eids = jnp.arange(N_EXPERTS, dtype=jnp.int32)
    earlier = eids[None, :] < eids[:, None]
    excl_sum = lambda a: jnp.sum(jnp.where(earlier, a[..., None, :], 0), axis=-1)
    padded = (counts + rbk - 1) // rbk * rbk
    pad_start = excl_sum(padded).astype(jnp.int32)
    pad_end = pad_start + padded
    nblk = -(-(n_tok * TOP_K) // rbk) + N_EXPERTS
    cap = nblk * rbk
    block_start = jnp.arange(nblk + 1, dtype=jnp.int32) * rbk
    owns = (pad_start[None, :] <= block_start[:, None]) & (block_start[:, None] < pad_end[None, :])
    nvalid = jnp.sum(jnp.where(owns, jnp.clip(counts[None, :] - (block_start[:, None] - pad_start[None, :]),
                                              0, rbk), 0), axis=1).astype(jnp.int32)
    has_rows = counts > 0
    last_e = jnp.max(jnp.where(has_rows, eids, 0))
    block_expert = jnp.where(jnp.any(owns, axis=1), jnp.sum(jnp.where(owns, eids[None, :], 0), axis=1),
                             last_e).astype(jnp.int32)

    run_len = jnp.concatenate([tcar[1:], counts[None, :]], axis=0) - tcar
    dcar = tcar[::DISPATCH_TILE // COMBINE_TILE]
    drun_len = jnp.concatenate([dcar[1:], counts[None, :]], axis=0) - dcar
    flat = lambda a: a.astype(jnp.int32).reshape(-1)
    xs = _dispatch(h2_all, lpos, flat(excl_sum(drun_len)), flat(drun_len), flat(pad_start[None, :] + dcar),
                   cap)
    expert_pos = excl_sum(has_rows.astype(jnp.int32))
    at_pos = has_rows[None, :] & (expert_pos[None, :] == jnp.arange(N_EXPERTS + 1, dtype=jnp.int32)[:, None])
    expert_list = jnp.where(jnp.any(at_pos, axis=1), jnp.sum(jnp.where(at_pos, eids[None, :], 0), axis=1),
                            -1).astype(jnp.int32)
    block_pos = jnp.sum(jnp.where(block_expert[:, None] == eids[None, :], expert_pos[None, :], 0),
                        axis=1).astype(jnp.int32)
    yb = _experts(xs, block_expert, nvalid, block_pos, expert_list,
                  w_gate_up[0], b_gate_up[0], w_down[0], b_down[0])

    offa = (pad_start[None, :] + (tcar - tcar % WIN_ALIGN)).astype(jnp.int32).reshape(-1)
    nchunk = jnp.maximum(jnp.max((run_len + WIN - 1) // WIN, axis=1), 1).astype(jnp.int32)
    gfin = norm_final.reshape(1, D_MODEL)
    tok_info = jnp.concatenate([col.astype(F32), lrank.astype(F32), gates], axis=0).T
    y_prompt, y_sample = _combine(x1_p, x1_s, tok_info, gfin, yb, offa, nchunk)

    kv_shape = (1, 1, N_META + WINDOW, N_KV_HEADS, HEAD_DIM)
    new_k_p = jnp.concatenate([kmeta, ktail], axis=0).reshape(kv_shape)
    new_v_p = jnp.concatenate([vmeta, vtail], axis=0).reshape(kv_shape)
    new_pool_p = ptail[16 - POOL_STATE:].reshape(1, 1, POOL_STATE, POOL_WIDTH)
    new_k_s = jnp.concatenate([ck[:, :N_META], ck[:, N_META + 1:], knew[:, None]], axis=1).reshape(
        (1, nb, N_META + WINDOW, N_KV_HEADS, HEAD_DIM))
    new_v_s = jnp.concatenate([cv[:, :N_META], cv[:, N_META + 1:], vnew[:, None]], axis=1).reshape(
        (1, nb, N_META + WINDOW, N_KV_HEADS, HEAD_DIM))
    new_pool_s = jnp.concatenate([state_pool[0][:, 1:], pnew[:, None]], axis=1)[None]
    return (y_prompt[None], y_sample[:, None], new_k_p, new_v_p, new_pool_p, new_k_s, new_v_s, new_pool_s)
```

```python
import functools

import jax
import jax.numpy as jnp
import numpy as np
from jax import lax
from jax.experimental import pallas as pl
from jax.experimental.pallas import tpu as pltpu

F32 = jnp.float32
BF16 = jnp.bfloat16

D_MODEL = 1024
N_META = 16
N_HEADS = 8
HEAD_DIM = 64
N_KV_HEADS = 2
GQA_GROUP = N_HEADS // N_KV_HEADS
ATTN_WIDTH = N_HEADS * HEAD_DIM
KV_WIDTH = N_KV_HEADS * HEAD_DIM
WINDOW = 128
POOL_WIDTH = D_MODEL - ATTN_WIDTH
POOL_WINDOWS = (2, 4, 8, 16)
POOL_GROUP_DIM = POOL_WIDTH // len(POOL_WINDOWS)
POOL_STATE = max(POOL_WINDOWS) - 1
N_EXPERTS = 32
TOP_K = 4
D_EXPERT = D_MODEL
SWIGLU_ALPHA = 1.702
SWIGLU_LIMIT = 7.0
NORM_EPS = 1e-5
PAST_LEN = 16384

LANES = 128
QSUB = 64
KEYS_SUB = QSUB + WINDOW
META_PAD = 64
NKEY = META_PAD + KEYS_SUB
MASKED = -1e30
PROMPT_BLOCK = 1024
ROUTE_BLOCK = 384
EXPERT_ROWS = 512
DISPATCH_TILE = 384
PACK_CHUNKS = D_MODEL // 2 // LANES
COMBINE_TILE = 128
WIN = 32
WIN_ALIGN = 16
WIN_ROWS = WIN + WIN_ALIGN
WIN_HEAD = 32
VMEM_LIMIT = 56 * 1024 * 1024


def _rms(x, g):
    return x * lax.rsqrt(jnp.mean(x * x, axis=-1, keepdims=True) + NORM_EPS) * g


def _router_logits(rwt_ref, h2, h2_hi):
    nt = (((1,), (1,)), ((), ()))
    h2_lo = (h2 - h2_hi.astype(F32)).astype(BF16)
    return (lax.dot_general(rwt_ref[0], h2_hi, nt, preferred_element_type=F32)
            + lax.dot_general(rwt_ref[0], h2_lo, nt, preferred_element_type=F32)
            + lax.dot_general(rwt_ref[1], h2_hi, nt, preferred_element_type=F32))


def _dup_halves(a):
    lane = lax.broadcasted_iota(jnp.int32, a.shape, 1)
    r = pltpu.roll(a, HEAD_DIM, axis=1)
    lo = lane < HEAD_DIM
    return jnp.where(lo, a, r), jnp.where(lo, r, a)


def _pool_means(pext_ref, n):
    outs = []
    for gi, w in enumerate(POOL_WINDOWS):
        xg = pext_ref[:, gi * POOL_GROUP_DIM:(gi + 1) * POOL_GROUP_DIM]
        s = xg
        sh = 1
        while sh < w:
            s = s + pltpu.roll(s, sh, axis=0)
            sh *= 2
        outs.append(s[16:] * (1.0 / w) - xg[16:])
    return outs


def _pack_bf16_pairs(h):
    m = h.shape[1] // 2
    lo = pltpu.bitcast(h[:, :m].astype(BF16).astype(F32), jnp.uint32)
    hi = pltpu.bitcast(h[:, m:].astype(BF16).astype(F32), jnp.uint32)
    return lax.shift_right_logical(lo, jnp.uint32(16)) | (hi & jnp.uint32(0xFFFF0000))


def _unpack_bf16_pairs(w):
    lo = pltpu.bitcast(lax.shift_left(w, jnp.uint32(16)), F32).astype(BF16)
    hi = pltpu.bitcast(w & jnp.uint32(0xFFFF0000), F32).astype(BF16)
    return lo, hi


def _prompt_kernel(x_ref, meta_ref, gattn_ref, win_ref, wpool_ref, pscale_ref, wout_ref, gffn_ref,
                   rwt_ref, rb_ref, sink_ref, tbl_ref, tail_h2_ref, tail_lgt_ref,
                   x1_ref, h2_ref, lgt_ref, kmeta_ref, vmeta_ref, ktail_ref, vtail_ref, ptail_ref,
                   k2buf, v2buf, km2, vm2, qbuf, obuf, pext):
    pid = pl.program_id(0)
    n_main = pl.num_programs(0) - 1
    refs = (x_ref, meta_ref, gattn_ref, win_ref, wpool_ref, pscale_ref, wout_ref, gffn_ref,
            rwt_ref, rb_ref, sink_ref, tbl_ref,
            x1_ref, h2_ref, lgt_ref, kmeta_ref, vmeta_ref, ktail_ref, vtail_ref, ptail_ref,
            k2buf, v2buf, km2, vm2, qbuf, obuf, pext)

    @pl.when(pid < n_main)
    def _():
        _prompt_block(*refs)

    @pl.when(pid == n_main)
    def _():
        h2_ref[0:tail_h2_ref.shape[0], :] = tail_h2_ref[...]
        lgt_ref[:, 0:tail_lgt_ref.shape[1]] = tail_lgt_ref[...]


def _prompt_block(x_ref, meta_ref, gattn_ref, win_ref, wpool_ref, pscale_ref, wout_ref, gffn_ref,
                  rwt_ref, rb_ref, sink_ref, tbl_ref,
                  x1_ref, h2_ref, lgt_ref, kmeta_ref, vmeta_ref, ktail_ref, vtail_ref, ptail_ref,
                  k2buf, v2buf, km2, vm2, qbuf, obuf, pext):
    tb = x_ref.shape[0]
    pid = pl.program_id(0)

    @pl.when(pid == 0)
    def _():
        hm = _rms(meta_ref[...], gattn_ref[...]).astype(BF16)
        km = jnp.dot(hm, win_ref[:, ATTN_WIDTH:ATTN_WIDTH + KV_WIDTH], preferred_element_type=F32)
        vm = jnp.dot(hm, win_ref[:, ATTN_WIDTH + KV_WIDTH:ATTN_WIDTH + 2 * KV_WIDTH],
                     preferred_element_type=F32)
        pm = jnp.dot(hm, win_ref[:, ATTN_WIDTH + 2 * KV_WIDTH:], preferred_element_type=F32)
        kmeta_ref[...] = km
        vmeta_ref[...] = vm
        zpad = jnp.zeros((META_PAD - N_META, LANES), F32)
        k0, k1 = _dup_halves(jnp.concatenate([km, zpad], axis=0))
        v0, v1 = _dup_halves(jnp.concatenate([vm, zpad], axis=0))
        km2[0] = k0.astype(BF16)
        km2[1] = k1.astype(BF16)
        vm2[0, :, 0:LANES] = v0.astype(BF16)
        vm2[1, :, 0:LANES] = v1.astype(BF16)
        vm2[:, :, LANES:] = jnp.ones((N_KV_HEADS, META_PAD, LANES), BF16)
        k2buf[:, 0:WINDOW, :] = jnp.zeros((2, WINDOW, LANES), BF16)
        v2buf[:, 0:WINDOW, 0:LANES] = jnp.zeros((2, WINDOW, LANES), BF16)
        v2buf[:, :, LANES:] = jnp.ones((N_KV_HEADS, WINDOW + tb, LANES), BF16)
        pext[0:16, :] = pm

    h = _rms(x_ref[...], gattn_ref[...]).astype(BF16)
    q = jnp.dot(h, win_ref[:, 0:ATTN_WIDTH], preferred_element_type=F32) * (HEAD_DIM ** -0.5)
    lane_t = lax.broadcasted_iota(jnp.int32, (tb, LANES), 1)
    for c in range(N_HEADS // 2):
        tile = q[:, c * LANES:(c + 1) * LANES]
        for a in range(2):
            keep = (lane_t < HEAD_DIM) if a == 0 else (lane_t >= HEAD_DIM)
            piece = jnp.where(keep, tile, 0.0).astype(BF16).reshape(tb // QSUB, QSUB, LANES)
            row = ((c % 2) * 2 + a) * QSUB
            qbuf[c // 2, :, row:row + QSUB, :] = piece
    k = jnp.dot(h, win_ref[:, ATTN_WIDTH:ATTN_WIDTH + KV_WIDTH], preferred_element_type=F32)
    v = jnp.dot(h, win_ref[:, ATTN_WIDTH + KV_WIDTH:ATTN_WIDTH + 2 * KV_WIDTH], preferred_element_type=F32)
    p = jnp.dot(h, win_ref[:, ATTN_WIDTH + 2 * KV_WIDTH:], preferred_element_type=F32)
    ktail_ref[...] = k[tb - WINDOW:]
    vtail_ref[...] = v[tb - WINDOW:]
    ptail_ref[...] = p[tb - 16:]
    k0, k1 = _dup_halves(k)
    v0, v1 = _dup_halves(v)
    k2buf[0, WINDOW:, :] = k0.astype(BF16)
    k2buf[1, WINDOW:, :] = k1.astype(BF16)
    v2buf[0, WINDOW:, 0:LANES] = v0.astype(BF16)
    v2buf[1, WINDOW:, 0:LANES] = v1.astype(BF16)
    pext[16:, :] = p

    lane_q = lax.broadcasted_iota(jnp.int32, (QSUB, LANES), 1)
    lo_q = lane_q < HEAD_DIM

    for u in range(tb // QSUB):
        r0 = u * QSUB
        sel = jnp.where(pid == 0, u + 1, 0) if u < WINDOW // QSUB else 0
        for g in range(N_KV_HEADS):
            qm = qbuf[g, u]
            kwin = jnp.concatenate([km2[g], k2buf[g, r0:r0 + KEYS_SUB, :]], axis=0)
            vwin = jnp.concatenate([vm2[g], v2buf[g, r0:r0 + KEYS_SUB, :]], axis=0)
            s = lax.dot_general(qm, kwin, (((1,), (1,)), ((), ())), preferred_element_type=F32)
            s = s + tbl_ref[sel, g]
            sink = sink_ref[g]
            m = jnp.maximum(jnp.max(s, axis=1, keepdims=True), sink)
            e = jnp.exp(s - m).astype(BF16)
            r = jnp.dot(e, vwin, preferred_element_type=F32)
            o = r[:, 0:LANES] / (r[:, LANES:] + jnp.exp(sink - m))
            o0 = jnp.where(lo_q, o[0:QSUB], o[QSUB:2 * QSUB])
            o1 = jnp.where(lo_q, o[2 * QSUB:3 * QSUB], o[3 * QSUB:])
            obuf[r0:r0 + QSUB, (2 * g) * LANES:(2 * g + 1) * LANES] = o0.astype(BF16)
            obuf[r0:r0 + QSUB, (2 * g + 1) * LANES:(2 * g + 2) * LANES] = o1.astype(BF16)

    pooled = _pool_means(pext, tb)
    for gi in range(len(POOL_WINDOWS)):
        y = jnp.dot(pooled[gi].astype(BF16), wpool_ref[gi], preferred_element_type=F32)
        y = y * pscale_ref[:, gi * POOL_GROUP_DIM:(gi + 1) * POOL_GROUP_DIM]
        obuf[:, ATTN_WIDTH + gi * POOL_GROUP_DIM:ATTN_WIDTH + (gi + 1) * POOL_GROUP_DIM] = y.astype(BF16)

    k2buf[:, 0:WINDOW, :] = k2buf[:, tb:tb + WINDOW, :]
    v2buf[:, 0:WINDOW, 0:LANES] = v2buf[:, tb:tb + WINDOW, 0:LANES]
    pext[0:16, :] = pext[tb:tb + 16, :]

    x1 = x_ref[...] + jnp.dot(obuf[...], wout_ref[...], preferred_element_type=F32)
    x1_ref[...] = x1
    h2 = _rms(x1, gffn_ref[...])
    h2_hi = h2.astype(BF16)
    h2_ref[...] = h2_hi
    lgt_ref[...] = _router_logits(rwt_ref, h2, h2_hi) + rb_ref[...]


def _attn_tables(sinks):
    i = np.arange(QSUB)[:, None]
    j = np.arange(NKEY)[None, :]
    jb = j - META_PAD
    rel = i + WINDOW - jb
    band_ok = (jb >= 0) & (rel >= 0) & (rel <= WINDOW)
    meta_ok = (j < N_META) & (i >= 0)
    slopes = np.exp2(-8.0 * np.arange(1, N_HEADS + 1) / N_HEADS)
    tbl = np.empty((3, N_KV_HEADS, GQA_GROUP * QSUB, NKEY), np.float32)
    for var in range(3):
        ok = band_ok if var == 0 else band_ok & (jb >= WINDOW - (var - 1) * QSUB)
        for g in range(N_KV_HEADS):
            for a in range(GQA_GROUP):
                hd = g * GQA_GROUP + a
                bias = np.where(ok, -slopes[hd] * rel, MASKED)
                bias = np.where(meta_ok, 0.0, bias)
                tbl[var, g, a * QSUB:(a + 1) * QSUB] = bias
    sink_col = jnp.repeat(sinks.astype(F32).reshape(N_KV_HEADS, GQA_GROUP, 1), QSUB, axis=2)
    return jnp.asarray(tbl), sink_col.reshape(N_KV_HEADS, GQA_GROUP * QSUB, 1)


def _prompt_mixer(x, meta, gattn, win, wpool, pscale, wout, gffn, rwt, rb, sinks, tail_h2, tail_lgt):
    seq = x.shape[0]
    tb = PROMPT_BLOCK
    n_tail = tail_h2.shape[0]
    assert seq % tb == 0 and tb % WINDOW == 0 and n_tail <= tb
    nblk = seq // tb
    n_tok = seq + n_tail
    tbl, sink_col = _attn_tables(sinks)
    full = lambda *shape: pl.BlockSpec(shape, lambda i: (0,) * len(shape))
    main = lambda i: (jnp.minimum(i, nblk - 1), 0)
    in_width = win.shape[1]
    return pl.pallas_call(
        _prompt_kernel,
        grid=(nblk + 1,),
        in_specs=[
            pl.BlockSpec((tb, D_MODEL), main),
            full(N_META, D_MODEL), full(1, D_MODEL), full(D_MODEL, in_width),
            full(len(POOL_WINDOWS), POOL_GROUP_DIM, POOL_GROUP_DIM), full(1, POOL_WIDTH),
            full(D_MODEL, D_MODEL), full(1, D_MODEL), full(2, N_EXPERTS, D_MODEL), full(N_EXPERTS, 1),
            full(N_KV_HEADS, GQA_GROUP * QSUB, 1), full(3, N_KV_HEADS, GQA_GROUP * QSUB, NKEY),
            full(n_tail, D_MODEL), full(N_EXPERTS, n_tail),
        ],
        out_specs=[
            pl.BlockSpec((tb, D_MODEL), main),
            pl.BlockSpec((tb, D_MODEL), lambda i: (i, 0)),
            pl.BlockSpec((N_EXPERTS, tb), lambda i: (0, i)),
            full(N_META, KV_WIDTH), full(N_META, KV_WIDTH),
            full(WINDOW, KV_WIDTH), full(WINDOW, KV_WIDTH), full(16, POOL_WIDTH),
        ],
        out_shape=[
            jax.ShapeDtypeStruct((seq, D_MODEL), F32),
            jax.ShapeDtypeStruct((n_tok, D_MODEL), BF16),
            jax.ShapeDtypeStruct((N_EXPERTS, n_tok), F32),
            jax.ShapeDtypeStruct((N_META, KV_WIDTH), F32),
            jax.ShapeDtypeStruct((N_META, KV_WIDTH), F32),
            jax.ShapeDtypeStruct((WINDOW, KV_WIDTH), F32),
            jax.ShapeDtypeStruct((WINDOW, KV_WIDTH), F32),
            jax.ShapeDtypeStruct((16, POOL_WIDTH), F32),
        ],
        scratch_shapes=[
            pltpu.VMEM((N_KV_HEADS, WINDOW + tb, LANES), BF16),
            pltpu.VMEM((N_KV_HEADS, WINDOW + tb, 2 * LANES), BF16),
            pltpu.VMEM((N_KV_HEADS, META_PAD, LANES), BF16),
            pltpu.VMEM((N_KV_HEADS, META_PAD, 2 * LANES), BF16),
            pltpu.VMEM((N_KV_HEADS, tb // QSUB, GQA_GROUP * QSUB, LANES), BF16),
            pltpu.VMEM((tb, D_MODEL), BF16),
            pltpu.VMEM((16 + tb, POOL_WIDTH), F32),
        ],
        compiler_params=pltpu.CompilerParams(dimension_semantics=("arbitrary",),
                                             vmem_limit_bytes=VMEM_LIMIT),
        name="prompt_mixer",
    )(x, meta, gattn, win, wpool, pscale, wout, gffn, rwt, rb, sink_col, tbl, tail_h2, tail_lgt)


def _sample_kernel(x_ref, ck_ref, cv_ref, sp_ref, gattn_ref, win_ref, wpool_ref, pscale_ref, wout_ref,
                   gffn_ref, rwt_ref, rb_ref, sinkc_ref, bias_ref,
                   x1_ref, h2_ref, lgt_ref, knew_ref, vnew_ref, pnew_ref,
                   qm_buf, r_buf, obuf):
    nb = x_ref.shape[0]
    x = x_ref[...]
    h = _rms(x, gattn_ref[...]).astype(BF16)
    q = jnp.dot(h, win_ref[:, 0:ATTN_WIDTH], preferred_element_type=F32) * (HEAD_DIM ** -0.5)
    k = jnp.dot(h, win_ref[:, ATTN_WIDTH:ATTN_WIDTH + KV_WIDTH], preferred_element_type=F32)
    v = jnp.dot(h, win_ref[:, ATTN_WIDTH + KV_WIDTH:ATTN_WIDTH + 2 * KV_WIDTH], preferred_element_type=F32)
    p = jnp.dot(h, win_ref[:, ATTN_WIDTH + 2 * KV_WIDTH:], preferred_element_type=F32)
    knew_ref[...] = k
    vnew_ref[...] = v
    pnew_ref[...] = p

    lane = lax.broadcasted_iota(jnp.int32, (nb, LANES), 1)
    lo = lane < HEAD_DIM
    for hd in range(N_HEADS):
        tile = q[:, (hd // 2) * LANES:(hd // 2 + 1) * LANES]
        if (hd % 2) != (hd // GQA_GROUP):
            tile = pltpu.roll(tile, HEAD_DIM, axis=1)
        keep_lo = (hd // GQA_GROUP) == 0
        qm_buf[:, hd, :] = jnp.where(lo if keep_lo else jnp.logical_not(lo), tile, 0.0)

    unroll = 8
    assert nb % unroll == 0
    nt = (((1,), (1,)), ((), ()))
    bias = jnp.concatenate([bias_ref[...]] * unroll, axis=0)
    sink = jnp.concatenate([sinkc_ref[...]] * unroll, axis=0)

    def batch_group(gidx, carry):
        bs = [gidx * unroll + j for j in range(unroll)]
        qms = [qm_buf[b] for b in bs]
        s = jnp.concatenate(
            [lax.dot_general(qm.astype(BF16), ck_ref[b].astype(BF16), nt, preferred_element_type=F32)
             for qm, b in zip(qms, bs)], axis=0) + bias
        s_self = jnp.concatenate(
            [jnp.sum(qm * knew_ref[pl.ds(b, 1), :], axis=1, keepdims=True) for qm, b in zip(qms, bs)],
            axis=0)
        m = jnp.maximum(jnp.maximum(jnp.max(s, axis=1, keepdims=True), s_self), sink)
        e = jnp.exp(s - m)
        e_self = jnp.exp(s_self - m)
        inv_den = 1.0 / (jnp.sum(e, axis=1, keepdims=True) + e_self + jnp.exp(sink - m))
        for j, b in enumerate(bs):
            rows = slice(j * N_HEADS, (j + 1) * N_HEADS)
            r = jnp.dot(e[rows].astype(BF16), cv_ref[b].astype(BF16), preferred_element_type=F32)
            r = r + e_self[rows] * vnew_ref[pl.ds(b, 1), :]
            r_buf[b] = r * inv_den[rows]
        return carry

    lax.fori_loop(0, nb // unroll, batch_group, 0)

    for c in range(N_HEADS // 2):
        halves = []
        for a in range(2):
            hd = 2 * c + a
            t = r_buf[:, hd, :]
            if (hd // GQA_GROUP) != a:
                t = pltpu.roll(t, HEAD_DIM, axis=1)
            halves.append(t)
        obuf[:, c * LANES:(c + 1) * LANES] = jnp.where(lo, halves[0], halves[1]).astype(BF16)

    for gi, w in enumerate(POOL_WINDOWS):
        cols = slice(gi * POOL_GROUP_DIM, (gi + 1) * POOL_GROUP_DIM)
        pg = p[:, cols]
        acc = pg
        for d in range(1, w):
            acc = acc + sp_ref[:, POOL_STATE - d, cols]
        pooled = acc * (1.0 / w) - pg
        y = jnp.dot(pooled.astype(BF16), wpool_ref[gi], preferred_element_type=F32) * pscale_ref[:, cols]
        obuf[:, ATTN_WIDTH + gi * POOL_GROUP_DIM:ATTN_WIDTH + (gi + 1) * POOL_GROUP_DIM] = y.astype(BF16)

    x1 = x + jnp.dot(obuf[...], wout_ref[...], preferred_element_type=F32)
    x1_ref[...] = x1
    h2 = _rms(x1, gffn_ref[...])
    h2_hi = h2.astype(BF16)
    h2_ref[...] = h2_hi
    lgt_ref[...] = _router_logits(rwt_ref, h2, h2_hi) + rb_ref[...]


def _sample_mixer(x, ck, cv, sp, gattn, win, wpool, pscale, wout, gffn, rwt, rb, sinks):
    nb = x.shape[0]
    rows = ck.shape[1]
    slopes = np.exp2(-8.0 * np.arange(1, N_HEADS + 1) / N_HEADS)
    dist = np.concatenate([np.zeros(N_META), WINDOW - np.arange(WINDOW)])
    bias = jnp.asarray((-slopes[:, None] * dist[None, :]).astype(np.float32))
    vm = pl.BlockSpec(memory_space=pltpu.VMEM)
    return pl.pallas_call(
        _sample_kernel,
        in_specs=[vm] * 14,
        out_specs=[vm] * 6,
        out_shape=[
            jax.ShapeDtypeStruct((nb, D_MODEL), F32),
            jax.ShapeDtypeStruct((nb, D_MODEL), BF16),
            jax.ShapeDtypeStruct((N_EXPERTS, nb), F32),
            jax.ShapeDtypeStruct((nb, KV_WIDTH), F32),
            jax.ShapeDtypeStruct((nb, KV_WIDTH), F32),
            jax.ShapeDtypeStruct((nb, POOL_WIDTH), F32),
        ],
        scratch_shapes=[
            pltpu.VMEM((nb, N_HEADS, LANES), F32),
            pltpu.VMEM((nb, N_HEADS, LANES), F32),
            pltpu.VMEM((nb, D_MODEL), BF16),
        ],
        compiler_params=pltpu.CompilerParams(vmem_limit_bytes=VMEM_LIMIT),
        name="sample_mixer",
    )(x, ck, cv, sp, gattn, win, wpool, pscale, wout, gffn, rwt, rb,
      sinks.astype(F32).reshape(N_HEADS, 1), bias)


def _router_kernel(lg_ref, tri_ref, low_ref, gate_ref, col_ref, lrank_ref, lpos_ref,
                   tcar_ref, cnt_ref, carry):
    tr = lg_ref.shape[1]

    @pl.when(pl.program_id(0) == 0)
    def _():
        carry[...] = jnp.zeros_like(carry)

    work = lg_ref[...]
    eio = lax.broadcasted_iota(jnp.int32, work.shape, 0).astype(F32)
    sels, vals, idxs = [], [], []
    for _k in range(TOP_K):
        mx = jnp.max(work, axis=0, keepdims=True)
        idx = jnp.min(jnp.where(work == mx, eio, float(N_EXPERTS)), axis=0, keepdims=True)
        sel = eio == idx
        sels.append(sel)
        vals.append(mx)
        idxs.append(idx)
        work = jnp.where(sel, -jnp.inf, work)
    exps = [jnp.exp(vk - vals[0]) for vk in vals]
    tot = exps[0] + exps[1] + exps[2] + exps[3]
    onehot = jnp.zeros(work.shape, F32)
    for sel in sels:
        onehot = onehot + sel.astype(F32)
    before = jnp.dot(onehot.astype(BF16), tri_ref[...], preferred_element_type=F32) + carry[...]
    for kk in range(TOP_K):
        gate_ref[pl.ds(kk, 1), :] = exps[kk] / tot
    for j in range(tr // COMBINE_TILE):
        cols = slice(j * COMBINE_TILE, (j + 1) * COMBINE_TILE)
        tc = before[:, j * COMBINE_TILE:j * COMBINE_TILE + 1]
        tcar_ref[j] = tc.astype(jnp.int32)
        slack = tc - WIN_ALIGN * jnp.floor(tc * (1.0 / WIN_ALIGN))
        local = before[:, cols] - tc
        for kk in range(TOP_K):
            selk = sels[kk][:, cols]
            lr = jnp.sum(jnp.where(selk, local, 0.0), axis=0, keepdims=True)
            sl = jnp.sum(jnp.where(selk, slack, 0.0), axis=0, keepdims=True)
            lrank_ref[pl.ds(kk, 1), cols] = lr.astype(jnp.int32)
            col_ref[pl.ds(kk, 1), cols] = (idxs[kk][:, cols] * float(WIN_ROWS) + sl + lr).astype(jnp.int32)
    for j in range(tr // DISPATCH_TILE):
        cols = slice(j * DISPATCH_TILE, (j + 1) * DISPATCH_TILE)
        local = before[:, cols] - before[:, j * DISPATCH_TILE:j * DISPATCH_TILE + 1]
        tile_cnt = jnp.broadcast_to(jnp.sum(onehot[:, cols], axis=1, keepdims=True), local.shape)
        cnt_hi = jnp.floor(tile_cnt * (1.0 / 256.0))
        cnt_lo = tile_cnt - 256.0 * cnt_hi
        run_start = (256.0 * jnp.dot(low_ref[...], cnt_hi.astype(BF16), preferred_element_type=F32)
                     + jnp.dot(low_ref[...], cnt_lo.astype(BF16), preferred_element_type=F32))
        for kk in range(TOP_K):
            lp = jnp.sum(jnp.where(sels[kk][:, cols], run_start + local, 0.0), axis=0, keepdims=True)
            lpos_ref[pl.ds(kk, 1), cols] = lp.astype(jnp.int32)
    carry[...] = carry[...] + jnp.sum(onehot, axis=1, keepdims=True)
    cnt_ref[...] = carry[...].astype(jnp.int32)


def _router(logits_t):
    n = logits_t.shape[1]
    tr = ROUTE_BLOCK
    assert n % tr == 0
    tri = jnp.asarray(np.triu(np.ones((tr, tr), np.float32), k=1), BF16)
    low = jnp.asarray(np.tril(np.ones((N_EXPERTS, N_EXPERTS), np.float32), k=-1), BF16)
    per_tok = pl.BlockSpec((TOP_K, tr), lambda i: (0, i))
    return pl.pallas_call(
        _router_kernel,
        grid=(n // tr,),
        in_specs=[pl.BlockSpec((N_EXPERTS, tr), lambda i: (0, i)),
                  pl.BlockSpec((tr, tr), lambda i: (0, 0)),
                  pl.BlockSpec((N_EXPERTS, N_EXPERTS), lambda i: (0, 0))],
        out_specs=[per_tok, per_tok, per_tok, per_tok,
                   pl.BlockSpec((tr // COMBINE_TILE, N_EXPERTS, 1), lambda i: (i, 0, 0)),
                   pl.BlockSpec((N_EXPERTS, 1), lambda i: (0, 0))],
        out_shape=[jax.ShapeDtypeStruct((TOP_K, n), F32),
                   jax.ShapeDtypeStruct((TOP_K, n), jnp.int32),
                   jax.ShapeDtypeStruct((TOP_K, n), jnp.int32),
                   jax.ShapeDtypeStruct((TOP_K, n), jnp.int32),
                   jax.ShapeDtypeStruct((n // COMBINE_TILE, N_EXPERTS, 1), jnp.int32),
                   jax.ShapeDtypeStruct((N_EXPERTS, 1), jnp.int32)],
        scratch_shapes=[pltpu.VMEM((N_EXPERTS, 1), F32)],
        compiler_params=pltpu.CompilerParams(dimension_semantics=("arbitrary",)),
        name="router",
    )(logits_t, tri, low)


def _dispatch_kernel(lstart_ref, cnt_ref, dst_ref, h2_ref, lpos_ref, xs_hbm, stg0, stg1, sem):
    i = pl.program_id(0)
    n_tiles = pl.num_programs(0)
    dt = h2_ref.shape[0]
    rows = dt * TOP_K
    slot = i % 2
    stgs = (stg0, stg1)
    pieces = [p for p in (256, 128, 64, 32, 16, 8, 4, 2, 1) if p <= dt]
    assert dt < 512

    def drain(s):
        pltpu.make_async_copy(stgs[s], xs_hbm.at[pl.ds(0, rows)], sem.at[s]).wait()

    def issue_runs(tile, live, s):
        for e in range(N_EXPERTS):
            n = jnp.where(live, cnt_ref[tile * N_EXPERTS + e], 0)
            src0 = lstart_ref[tile * N_EXPERTS + e]
            dst0 = dst_ref[tile * N_EXPERTS + e]
            for piece in pieces:
                off = n & ~jnp.int32(2 * piece - 1)

                @pl.when((n & piece) != 0)
                def _(off=off, piece=piece, src0=src0, dst0=dst0):
                    pltpu.make_async_copy(stgs[s].at[pl.ds(src0 + off, piece)],
                                          xs_hbm.at[pl.ds(dst0 + off, piece)], sem.at[s]).start()

    def sort_tile(s):
        rid = lax.broadcasted_iota(jnp.int32, (rows, dt), 0)
        hit = rid == lpos_ref[0:1, :]
        for kk in range(1, TOP_K):
            hit = jnp.logical_or(hit, rid == lpos_ref[kk:kk + 1, :])
        perm = jnp.where(hit, 1.0, 0.0).astype(BF16)
        srt = jnp.dot(perm, h2_ref[...], preferred_element_type=F32)
        packed = _pack_bf16_pairs(srt)
        for c in range(PACK_CHUNKS):
            stgs[s][:, c, :] = packed[:, c * LANES:(c + 1) * LANES]

    for s in range(2):
        @pl.when(slot == s)
        def _(s=s):
            @pl.when(i >= 2)
            def _():
                drain(s)

            issue_runs(jnp.maximum(i - 1, 0), i >= 1, 1 - s)
            sort_tile(s)

            @pl.when(i == n_tiles - 1)
            def _():
                issue_runs(i, True, s)
                drain(s)

                @pl.when(n_tiles >= 2)
                def _():
                    drain(1 - s)


def _dispatch(h2, lpos, lstart, cnt, dst, cap):
    n_tok = h2.shape[0]
    dt = DISPATCH_TILE
    assert n_tok % dt == 0
    grid_spec = pltpu.PrefetchScalarGridSpec(
        num_scalar_prefetch=3,
        grid=(n_tok // dt,),
        in_specs=[pl.BlockSpec((dt, D_MODEL), lambda i, a, b, c: (i, 0)),
                  pl.BlockSpec((TOP_K, dt), lambda i, a, b, c: (0, i)),
                  ],
        out_specs=pl.BlockSpec(memory_space=pl.ANY),
        scratch_shapes=[pltpu.VMEM((dt * TOP_K, PACK_CHUNKS, LANES), jnp.uint32),
                        pltpu.VMEM((dt * TOP_K, PACK_CHUNKS, LANES), jnp.uint32),
                        pltpu.SemaphoreType.DMA((2,))],
    )
    return pl.pallas_call(
        _dispatch_kernel,
        grid_spec=grid_spec,
        out_shape=jax.ShapeDtypeStruct((cap, PACK_CHUNKS, LANES), jnp.uint32),
        compiler_params=pltpu.CompilerParams(dimension_semantics=("arbitrary",),
                                             vmem_limit_bytes=VMEM_LIMIT),
        name="dispatch",
    )(lstart, cnt, dst, h2, lpos)


def _expert_kernel(n_xblocks, bexp_ref, nvalid_ref, epos_ref, elist_ref,
                   x_hbm, wgu_hbm, bgu_ref, wd_hbm, bd_ref, y_ref,
                   wgu_f32, wd_f32, wgu_bf, wd_bf, xbuf, xsem, wsem):
    i = pl.program_id(0)
    rb = y_ref.shape[0]
    nvalid = nvalid_ref[i]
    pos = epos_ref[i]
    fresh = jnp.logical_or(i == 0, pos != epos_ref[jnp.maximum(i - 1, 0)])
    slot = i % 2

    def x_copies(blk, s):
        return [pltpu.make_async_copy(x_hbm.at[pl.ds(blk * rb, rb), c, :],
                                      xbuf.at[s, :, pl.ds(c * LANES, LANES)], xsem.at[s])
                for c in range(PACK_CHUNKS)]

    def w_copies(p):
        e = elist_ref[p]
        s = p % 2
        return [pltpu.make_async_copy(wgu_hbm.at[e], wgu_f32.at[s], wsem.at[s, 0]),
                pltpu.make_async_copy(wd_hbm.at[e], wd_f32.at[s], wsem.at[s, 1])]

    @pl.when(i == 0)
    def _():
        for cp in x_copies(0, 0):
            cp.start()

        @pl.when(nvalid > 0)
        def _():
            for cp in w_copies(0):
                cp.start()

    @pl.when(i + 1 < n_xblocks)
    def _():
        for cp in x_copies(i + 1, 1 - slot):
            cp.start()

    @pl.when(jnp.logical_and(fresh, nvalid > 0))
    def _():
        @pl.when(elist_ref[pos + 1] >= 0)
        def _():
            for cp in w_copies(pos + 1):
                cp.start()

        for cp in w_copies(pos):
            cp.wait()
        ws = pos % 2
        chunk = 32

        def cast_gu(r, c):
            r0 = pl.multiple_of(r * chunk, chunk)
            wgu_bf[pl.ds(r0, chunk), :] = wgu_f32[ws, pl.ds(r0, chunk), :].astype(BF16)
            return c

        def cast_d(r, c):
            r0 = pl.multiple_of(r * chunk, chunk)
            wd_bf[pl.ds(r0, chunk), :] = wd_f32[ws, pl.ds(r0, chunk), :].astype(BF16)
            return c

        lax.fori_loop(0, D_MODEL // chunk, cast_gu, 0)
        lax.fori_loop(0, D_EXPERT // chunk, cast_d, 0)

    @pl.when(i < n_xblocks)
    def _():
        for cp in x_copies(i, slot):
            cp.wait()

    def ffn(rows):
        xw = xbuf[slot, 0:rows, :]
        xw = jnp.where(lax.broadcasted_iota(jnp.int32, xw.shape, 0) < nvalid, xw, jnp.uint32(0))
        xlo, xhi = _unpack_bf16_pairs(xw)
        half = D_MODEL // 2
        g = (jnp.dot(xlo, wgu_bf[0:half, 0:D_EXPERT], preferred_element_type=F32)
             + jnp.dot(xhi, wgu_bf[half:, 0:D_EXPERT], preferred_element_type=F32)
             + bgu_ref[0, :, 0:D_EXPERT])
        u = (jnp.dot(xlo, wgu_bf[0:half, D_EXPERT:], preferred_element_type=F32)
             + jnp.dot(xhi, wgu_bf[half:, D_EXPERT:], preferred_element_type=F32)
             + bgu_ref[0, :, D_EXPERT:])
        g = jnp.minimum(g, SWIGLU_LIMIT)
        u = jnp.clip(u, -SWIGLU_LIMIT, SWIGLU_LIMIT)
        act = g * (1.0 / (1.0 + jnp.exp(-SWIGLU_ALPHA * g))) * (u + 1.0)
        y = jnp.dot(act.astype(BF16), wd_bf[...], preferred_element_type=F32) + bd_ref[0]
        row = lax.broadcasted_iota(jnp.int32, y.shape, 0)
        y_ref[0:rows, :] = jnp.where(row < nvalid, y, 0.0).astype(BF16)
        if rows < rb:
            y_ref[rows:, :] = jnp.zeros((rb - rows, D_MODEL), BF16)

    @pl.when(nvalid > rb // 2)
    def _():
        ffn(rb)

    @pl.when(jnp.logical_and(nvalid > 0, nvalid <= rb // 2))
    def _():
        ffn(rb // 2)

    @pl.when(nvalid == 0)
    def _():
        y_ref[...] = jnp.zeros_like(y_ref)


def _experts(xs, block_expert, nvalid, block_pos, expert_list, wgu, bgu, wd, bd):
    rb = EXPERT_ROWS
    n_xblocks = xs.shape[0] // rb
    nblk = n_xblocks + 1
    any_space = pl.BlockSpec(memory_space=pl.ANY)
    grid_spec = pltpu.PrefetchScalarGridSpec(
        num_scalar_prefetch=4,
        grid=(nblk,),
        in_specs=[
            any_space,
            any_space,
            pl.BlockSpec((1, 1, 2 * D_EXPERT), lambda i, be, nu, ep, el: (be[i], 0, 0)),
            any_space,
            pl.BlockSpec((1, 1, D_MODEL), lambda i, be, nu, ep, el: (be[i], 0, 0)),
        ],
        out_specs=pl.BlockSpec((rb, D_MODEL), lambda i, be, nu, ep, el: (i, 0)),
        scratch_shapes=[pltpu.VMEM((2, D_MODEL, 2 * D_EXPERT), F32),
                        pltpu.VMEM((2, D_EXPERT, D_MODEL), F32),
                        pltpu.VMEM((D_MODEL, 2 * D_EXPERT), BF16),
                        pltpu.VMEM((D_EXPERT, D_MODEL), BF16),
                        pltpu.VMEM((2, rb, D_MODEL // 2), jnp.uint32),
                        pltpu.SemaphoreType.DMA((2,)),
                        pltpu.SemaphoreType.DMA((2, 2))],
    )
    return pl.pallas_call(
        functools.partial(_expert_kernel, n_xblocks),
        grid_spec=grid_spec,
        out_shape=jax.ShapeDtypeStruct((nblk * rb, D_MODEL), BF16),
        compiler_params=pltpu.CompilerParams(dimension_semantics=("arbitrary",),
                                             vmem_limit_bytes=VMEM_LIMIT),
        name="experts",
    )(block_expert, nvalid, block_pos, expert_list, xs, wgu, bgu.reshape(N_EXPERTS, 1, 2 * D_EXPERT), wd,
      bd.reshape(N_EXPERTS, 1, D_MODEL))


def _combine_kernel(n_prompt_tiles, offa_ref, nchunk_ref, tail_ref, ntail_ref,
                    x1p_ref, x1s_ref, info_ref, info_next_ref, gfin_ref, yb_hbm,
                    outp_ref, outs_ref, ybuf, gbuf0, gbuf1, acc_ref, sem, tsem):
    i = pl.program_id(0)
    n_tiles = pl.num_programs(0)

    def window_copy(tile, chunk, e, slot):
        base = pl.multiple_of(offa_ref[tile * N_EXPERTS + e] + chunk * WIN, WIN_ALIGN)
        return pltpu.make_async_copy(yb_hbm.at[pl.ds(base, WIN_ROWS), :],
                                     ybuf.at[slot, pl.ds(e * WIN_ROWS, WIN_ROWS), :],
                                     sem.at[slot])

    def start_windows(tile, chunk, slot):
        for e in range(N_EXPERTS):
            window_copy(tile, chunk, e, slot).start()

    def wait_windows(slot):
        pltpu.make_async_copy(yb_hbm.at[pl.ds(0, N_EXPERTS * WIN_ROWS), :], ybuf.at[slot],
                              sem.at[slot]).wait()

    def tail_copy(tile, e, slot):
        base = pl.multiple_of(offa_ref[tile * N_EXPERTS + e] + WIN_HEAD, WIN_ALIGN)
        return pltpu.make_async_copy(yb_hbm.at[pl.ds(base, WIN_ROWS - WIN_HEAD), :],
                                     ybuf.at[slot, pl.ds(e * WIN_ROWS + WIN_HEAD, WIN_ROWS - WIN_HEAD), :],
                                     tsem.at[slot])

    def start_first_chunk(tile, slot):
        for e in range(N_EXPERTS):
            base = pl.multiple_of(offa_ref[tile * N_EXPERTS + e], WIN_ALIGN)
            pltpu.make_async_copy(yb_hbm.at[pl.ds(base, WIN_HEAD), :],
                                  ybuf.at[slot, pl.ds(e * WIN_ROWS, WIN_HEAD), :], sem.at[slot]).start()

            @pl.when(tail_ref[tile * N_EXPERTS + e] != 0)
            def _(e=e):
                tail_copy(tile, e, slot).start()

    def wait_first_chunk(tile, slot):
        pltpu.make_async_copy(yb_hbm.at[pl.ds(0, N_EXPERTS * WIN_HEAD), :],
                              ybuf.at[slot, pl.ds(0, N_EXPERTS * WIN_HEAD), :], sem.at[slot]).wait()

        def one_tail(t, c):
            tail_copy(tile, 0, slot).wait()
            return c

        lax.fori_loop(0, ntail_ref[tile], one_tail, 0)

    slot = i % 2

    @pl.when(i == 0)
    def _():
        ybuf[...] = jnp.zeros_like(ybuf)
        start_first_chunk(0, 0)

    lane = lax.broadcasted_iota(jnp.int32, (COMBINE_TILE, N_EXPERTS * WIN_ROWS), 1)

    def gate_matrix(ref, chunk):
        g = jnp.zeros(lane.shape, F32)
        for kk in range(TOP_K):
            lr = ref[:, TOP_K + kk:TOP_K + kk + 1]
            in_chunk = jnp.logical_and(lr >= chunk * WIN, lr < chunk * WIN + WIN)
            colk = jnp.where(in_chunk, ref[:, kk:kk + 1] - chunk * WIN, -1.0).astype(jnp.int32)
            g = jnp.where(lane == colk, ref[:, 2 * TOP_K + kk:2 * TOP_K + kk + 1], g)
        return g.astype(BF16)

    def moe_rows(gm, buf):
        return jnp.dot(gm, ybuf[buf], preferred_element_type=F32)

    @pl.when(i == 0)
    def _():
        gbuf0[...] = gate_matrix(info_ref, 0)

    def main(s):
        start_first_chunk(jnp.minimum(i + 1, n_tiles - 1), 1 - s)
        wait_first_chunk(i, s)
        g_cur, g_nxt = (gbuf0, gbuf1) if s == 0 else (gbuf1, gbuf0)
        acc_ref[...] = moe_rows(g_cur[...], s)
        g_nxt[...] = gate_matrix(info_next_ref, 0)

    for s in range(2):
        @pl.when(slot == s)
        def _(s=s):
            main(s)

    @pl.when(i == n_tiles - 1)
    def _():
        for s in range(2):
            @pl.when(slot == s)
            def _(s=s):
                wait_first_chunk(i, 1 - s)

    def extra_chunk(j, c):
        start_windows(i, j, 2)
        wait_windows(2)
        acc_ref[...] += moe_rows(gate_matrix(info_ref, j), 2)
        return c

    lax.fori_loop(1, nchunk_ref[i], extra_chunk, 0)

    @pl.when(i < n_prompt_tiles)
    def _():
        outp_ref[...] = _rms(x1p_ref[...] + acc_ref[...], gfin_ref[...])

    @pl.when(i >= n_prompt_tiles)
    def _():
        outs_ref[...] = _rms(x1s_ref[...] + acc_ref[...], gfin_ref[...])


def _combine(x1_p, x1_s, tok_info, gfin, yb, offa, nchunk, need_tail, ntail):
    ct = COMBINE_TILE
    n_p, n_s = x1_p.shape[0] // ct, x1_s.shape[0] // ct
    assert x1_p.shape[0] % ct == 0 and x1_s.shape[0] % ct == 0 and n_s >= 1
    n_info = tok_info.shape[1]
    grid_spec = pltpu.PrefetchScalarGridSpec(
        num_scalar_prefetch=4,
        grid=(n_p + n_s,),
        in_specs=[
            pl.BlockSpec((ct, D_MODEL), lambda i, *_: (jnp.minimum(i, n_p - 1), 0)),
            pl.BlockSpec((ct, D_MODEL), lambda i, *_: (jnp.maximum(i - n_p, 0), 0)),
            pl.BlockSpec((ct, n_info), lambda i, *_: (i, 0)),
            pl.BlockSpec((ct, n_info), lambda i, *_: (jnp.minimum(i + 1, n_p + n_s - 1), 0)),
            pl.BlockSpec((1, D_MODEL), lambda i, *_: (0, 0)),
            pl.BlockSpec(memory_space=pl.ANY),
        ],
        out_specs=[
            pl.BlockSpec((ct, D_MODEL), lambda i, *_: (jnp.minimum(i, n_p - 1), 0)),
            pl.BlockSpec((ct, D_MODEL), lambda i, *_: (jnp.maximum(i - n_p, 0), 0)),
        ],
        scratch_shapes=[pltpu.VMEM((3, N_EXPERTS * WIN_ROWS, D_MODEL), BF16),
                        pltpu.VMEM((ct, N_EXPERTS * WIN_ROWS), BF16),
                        pltpu.VMEM((ct, N_EXPERTS * WIN_ROWS), BF16),
                        pltpu.VMEM((ct, D_MODEL), F32),
                        pltpu.SemaphoreType.DMA((3,)),
                        pltpu.SemaphoreType.DMA((2,))],
    )
    return pl.pallas_call(
        functools.partial(_combine_kernel, n_p),
        grid_spec=grid_spec,
        out_shape=[jax.ShapeDtypeStruct(x1_p.shape, F32), jax.ShapeDtypeStruct(x1_s.shape, F32)],
        compiler_params=pltpu.CompilerParams(dimension_semantics=("arbitrary",),
                                             vmem_limit_bytes=VMEM_LIMIT),
        name="combine",
    )(offa, nchunk, need_tail, ntail, x1_p, x1_s, tok_info, tok_info, gfin, yb)


def kernel(x_prompt, x_sample, cache_k, cache_v, state_pool, meta_tokens, norm_attn, w_in, attn_sinks,
           w_pool, pool_scale, w_out, norm_ffn, router_w, router_b, w_gate_up, b_gate_up, w_down, b_down,
           norm_final):
    assert w_in.shape[0] == 1, "single-layer trunk"
    bsz, seq, _ = x_prompt.shape
    assert bsz == 1
    nb = x_sample.shape[0]
    n_tok = seq + nb
    gattn = norm_attn[0].reshape(1, D_MODEL)
    gffn = norm_ffn[0].reshape(1, D_MODEL)
    win = w_in[0].astype(BF16)
    wpool = w_pool[0].astype(BF16)
    wout = w_out[0].astype(BF16)
    pscale = pool_scale[0].reshape(1, POOL_WIDTH)
    rw_t = router_w[0].T
    rw_hi = rw_t.astype(BF16)
    rwt = jnp.stack([rw_hi, (rw_t - rw_hi.astype(F32)).astype(BF16)])
    rb = router_b[0].reshape(N_EXPERTS, 1)
    sinks = attn_sinks[0]

    ck = cache_k[0].reshape(nb, N_META + WINDOW, KV_WIDTH)
    cv = cache_v[0].reshape(nb, N_META + WINDOW, KV_WIDTH)
    (x1_s, h2_s, lgt_s, knew, vnew, pnew) = _sample_mixer(
        x_sample[:, 0], ck, cv, state_pool[0], gattn, win, wpool, pscale, wout, gffn, rwt, rb, sinks)
    (x1_p, h2_all, lgt_all, kmeta, vmeta, ktail, vtail, ptail) = _prompt_mixer(
        x_prompt[0], meta_tokens, gattn, win, wpool, pscale, wout, gffn, rwt, rb, sinks, h2_s, lgt_s)

    gates, col, lrank, lpos, tcar, counts = _router(lgt_all)
    counts = counts[:, 0]
    tcar = tcar[:, :, 0]
    rbk = EXPERT_ROWS
    eids = jnp.arange(N_EXPERTS, dtype=jnp.int32)
    earlier = eids[None, :] < eids[:, None]
    excl_sum = lambda a: jnp.sum(jnp.where(earlier, a[..., None, :], 0), axis=-1)
    padded = (counts + rbk - 1) // rbk * rbk
    pad_start = excl_sum(padded).astype(jnp.int32)
    pad_end = pad_start + padded
    nblk = -(-(n_tok * TOP_K) // rbk) + N_EXPERTS
    cap = nblk * rbk
    block_start = jnp.arange(nblk + 1, dtype=jnp.int32) * rbk
    owns = (pad_start[None, :] <= block_start[:, None]) & (block_start[:, None] < pad_end[None, :])
    nvalid = jnp.sum(jnp.where(owns, jnp.clip(counts[None, :] - (block_start[:, None] - pad_start[None, :]),
                                              0, rbk), 0), axis=1).astype(jnp.int32)
    has_rows = counts > 0
    last_e = jnp.max(jnp.where(has_rows, eids, 0))
    block_expert = jnp.where(jnp.any(owns, axis=1), jnp.sum(jnp.where(owns, eids[None, :], 0), axis=1),
                             last_e).astype(jnp.int32)

    run_len = jnp.concatenate([tcar[1:], counts[None, :]], axis=0) - tcar
    dcar = tcar[::DISPATCH_TILE // COMBINE_TILE]
    drun_len = jnp.concatenate([dcar[1:], counts[None, :]], axis=0) - dcar
    flat = lambda a: a.astype(jnp.int32).reshape(-1)
    xs = _dispatch(h2_all, lpos, flat(excl_sum(drun_len)), flat(drun_len), flat(pad_start[None, :] + dcar),
                   cap)
    expert_pos = excl_sum(has_rows.astype(jnp.int32))
    at_pos = has_rows[None, :] & (expert_pos[None, :] == jnp.arange(N_EXPERTS + 1, dtype=jnp.int32)[:, None])
    expert_list = jnp.where(jnp.any(at_pos, axis=1), jnp.sum(jnp.where(at_pos, eids[None, :], 0), axis=1),
                            -1).astype(jnp.int32)
    block_pos = jnp.sum(jnp.where(block_expert[:, None] == eids[None, :], expert_pos[None, :], 0),
                        axis=1).astype(jnp.int32)
    yb = _experts(xs, block_expert, nvalid, block_pos, expert_list,
                  w_gate_up[0], b_gate_up[0], w_down[0], b_down[0])

    offa = (pad_start[None, :] + (tcar - tcar % WIN_ALIGN)).astype(jnp.int32).reshape(-1)
    nchunk = jnp.maximum(jnp.max((run_len + WIN - 1) // WIN, axis=1), 1).astype(jnp.int32)
    need_tail = (tcar % WIN_ALIGN + jnp.minimum(run_len, WIN)) > WIN_HEAD
    ntail = jnp.sum(need_tail, axis=1).astype(jnp.int32)
    gfin = norm_final.reshape(1, D_MODEL)
    tok_info = jnp.concatenate([col.astype(F32), lrank.astype(F32), gates], axis=0).T
    y_prompt, y_sample = _combine(x1_p, x1_s, tok_info, gfin, yb, offa, nchunk,
                                  need_tail.astype(jnp.int32).reshape(-1), ntail)

    kv_shape = (1, 1, N_META + WINDOW, N_KV_HEADS, HEAD_DIM)
    new_k_p = jnp.concatenate([kmeta, ktail], axis=0).reshape(kv_shape)
    new_v_p = jnp.concatenate([vmeta, vtail], axis=0).reshape(kv_shape)
    new_pool_p = ptail[16 - POOL_STATE:].reshape(1, 1, POOL_STATE, POOL_WIDTH)
    new_k_s = jnp.concatenate([ck[:, :N_META], ck[:, N_META + 1:], knew[:, None]], axis=1).reshape(
        (1, nb, N_META + WINDOW, N_KV_HEADS, HEAD_DIM))
    new_v_s = jnp.concatenate([cv[:, :N_META], cv[:, N_META + 1:], vnew[:, None]], axis=1).reshape(
        (1, nb, N_META + WINDOW, N_KV_HEADS, HEAD_DIM))
    new_pool_s = jnp.concatenate([state_pool[0][:, 1:], pnew[:, None]], axis=1)[None]
    return (y_prompt[None], y_sample[:, None], new_k_p, new_v_p, new_pool_p, new_k_s, new_v_s, new_pool_s)
```

```python
import functools

import jax
import jax.numpy as jnp
import numpy as np
from jax import lax
from jax.experimental import pallas as pl
from jax.experimental.pallas import tpu as pltpu

F32 = jnp.float32
BF16 = jnp.bfloat16

D_MODEL = 1024
N_META = 16
N_HEADS = 8
HEAD_DIM = 64
N_KV_HEADS = 2
GQA_GROUP = N_HEADS // N_KV_HEADS
ATTN_WIDTH = N_HEADS * HEAD_DIM
KV_WIDTH = N_KV_HEADS * HEAD_DIM
WINDOW = 128
POOL_WIDTH = D_MODEL - ATTN_WIDTH
POOL_WINDOWS = (2, 4, 8, 16)
POOL_GROUP_DIM = POOL_WIDTH // len(POOL_WINDOWS)
POOL_STATE = max(POOL_WINDOWS) - 1
N_EXPERTS = 32
TOP_K = 4
D_EXPERT = D_MODEL
SWIGLU_ALPHA = 1.702
SWIGLU_LIMIT = 7.0
NORM_EPS = 1e-5
PAST_LEN = 16384

LANES = 128
QSUB = 64
KEYS_SUB = QSUB + WINDOW
META_PAD = 64
NKEY = META_PAD + KEYS_SUB
MASKED = -1e30
PROMPT_BLOCK = 1024
ROUTE_BLOCK = 384
EXPERT_ROWS = 512
DISPATCH_TILE = 384
PACK_CHUNKS = D_MODEL // 2 // LANES
COMBINE_TILE = 128
WIN = 32
WIN_ALIGN = 16
WIN_ROWS = WIN + WIN_ALIGN
WIN_HEAD = 32
VMEM_LIMIT = 56 * 1024 * 1024


def _rms(x, g):
    return x * lax.rsqrt(jnp.mean(x * x, axis=-1, keepdims=True) + NORM_EPS) * g


def _router_logits(rwt_ref, h2, h2_hi):
    nt = (((1,), (1,)), ((), ()))
    h2_lo = (h2 - h2_hi.astype(F32)).astype(BF16)
    return (lax.dot_general(rwt_ref[0], h2_hi, nt, preferred_element_type=F32)
            + lax.dot_general(rwt_ref[0], h2_lo, nt, preferred_element_type=F32)
            + lax.dot_general(rwt_ref[1], h2_hi, nt, preferred_element_type=F32))


def _dup_halves(a):
    lane = lax.broadcasted_iota(jnp.int32, a.shape, 1)
    r = pltpu.roll(a, HEAD_DIM, axis=1)
    lo = lane < HEAD_DIM
    return jnp.where(lo, a, r), jnp.where(lo, r, a)


def _pool_means(pext_ref, n):
    outs = []
    for gi, w in enumerate(POOL_WINDOWS):
        xg = pext_ref[:, gi * POOL_GROUP_DIM:(gi + 1) * POOL_GROUP_DIM]
        s = xg
        sh = 1
        while sh < w:
            s = s + pltpu.roll(s, sh, axis=0)
            sh *= 2
        outs.append(s[16:] * (1.0 / w) - xg[16:])
    return outs


def _pack_exact_bf16_pairs(h):
    m = h.shape[1] // 2
    return (lax.shift_right_logical(pltpu.bitcast(h[:, :m], jnp.uint32), jnp.uint32(16))
            | (pltpu.bitcast(h[:, m:], jnp.uint32) & jnp.uint32(0xFFFF0000)))


def _unpack_bf16_pairs(w):
    lo = pltpu.bitcast(lax.shift_left(w, jnp.uint32(16)), F32).astype(BF16)
    hi = pltpu.bitcast(w & jnp.uint32(0xFFFF0000), F32).astype(BF16)
    return lo, hi


def _prompt_kernel(x_ref, meta_ref, gattn_ref, win_ref, wpool_ref, pscale_ref, wout_ref, gffn_ref,
                   rwt_ref, rb_ref, sink_ref, tbl_ref, tail_h2_ref, tail_lgt_ref,
                   x1_ref, h2_ref, lgt_ref, kmeta_ref, vmeta_ref, ktail_ref, vtail_ref, ptail_ref,
                   k2buf, v2buf, km2, vm2, qbuf, obuf, pext):
    pid = pl.program_id(0)
    n_main = pl.num_programs(0) - 1
    refs = (x_ref, meta_ref, gattn_ref, win_ref, wpool_ref, pscale_ref, wout_ref, gffn_ref,
            rwt_ref, rb_ref, sink_ref, tbl_ref,
            x1_ref, h2_ref, lgt_ref, kmeta_ref, vmeta_ref, ktail_ref, vtail_ref, ptail_ref,
            k2buf, v2buf, km2, vm2, qbuf, obuf, pext)

    @pl.when(pid < n_main)
    def _():
        _prompt_block(*refs)

    @pl.when(pid == n_main)
    def _():
        h2_ref[0:tail_h2_ref.shape[0], :] = tail_h2_ref[...]
        lgt_ref[:, 0:tail_lgt_ref.shape[1]] = tail_lgt_ref[...]


def _prompt_block(x_ref, meta_ref, gattn_ref, win_ref, wpool_ref, pscale_ref, wout_ref, gffn_ref,
                  rwt_ref, rb_ref, sink_ref, tbl_ref,
                  x1_ref, h2_ref, lgt_ref, kmeta_ref, vmeta_ref, ktail_ref, vtail_ref, ptail_ref,
                  k2buf, v2buf, km2, vm2, qbuf, obuf, pext):
    tb = x_ref.shape[0]
    pid = pl.program_id(0)

    @pl.when(pid == 0)
    def _():
        hm = _rms(meta_ref[...], gattn_ref[...]).astype(BF16)
        km = jnp.dot(hm, win_ref[:, ATTN_WIDTH:ATTN_WIDTH + KV_WIDTH], preferred_element_type=F32)
        vm = jnp.dot(hm, win_ref[:, ATTN_WIDTH + KV_WIDTH:ATTN_WIDTH + 2 * KV_WIDTH],
                     preferred_element_type=F32)
        pm = jnp.dot(hm, win_ref[:, ATTN_WIDTH + 2 * KV_WIDTH:], preferred_element_type=F32)
        kmeta_ref[...] = km
        vmeta_ref[...] = vm
        zpad = jnp.zeros((META_PAD - N_META, LANES), F32)
        k0, k1 = _dup_halves(jnp.concatenate([km, zpad], axis=0))
        v0, v1 = _dup_halves(jnp.concatenate([vm, zpad], axis=0))
        km2[0] = k0.astype(BF16)
        km2[1] = k1.astype(BF16)
        vm2[0, :, 0:LANES] = v0.astype(BF16)
        vm2[1, :, 0:LANES] = v1.astype(BF16)
        vm2[:, :, LANES:] = jnp.ones((N_KV_HEADS, META_PAD, LANES), BF16)
        k2buf[:, 0:WINDOW, :] = jnp.zeros((2, WINDOW, LANES), BF16)
        v2buf[:, 0:WINDOW, 0:LANES] = jnp.zeros((2, WINDOW, LANES), BF16)
        v2buf[:, :, LANES:] = jnp.ones((N_KV_HEADS, WINDOW + tb, LANES), BF16)
        pext[0:16, :] = pm

    h = _rms(x_ref[...], gattn_ref[...]).astype(BF16)
    q = jnp.dot(h, win_ref[:, 0:ATTN_WIDTH], preferred_element_type=F32) * (HEAD_DIM ** -0.5)
    lane_t = lax.broadcasted_iota(jnp.int32, (tb, LANES), 1)
    for c in range(N_HEADS // 2):
        tile = q[:, c * LANES:(c + 1) * LANES]
        for a in range(2):
            keep = (lane_t < HEAD_DIM) if a == 0 else (lane_t >= HEAD_DIM)
            piece = jnp.where(keep, tile, 0.0).astype(BF16).reshape(tb // QSUB, QSUB, LANES)
            row = ((c % 2) * 2 + a) * QSUB
            qbuf[c // 2, :, row:row + QSUB, :] = piece
    k = jnp.dot(h, win_ref[:, ATTN_WIDTH:ATTN_WIDTH + KV_WIDTH], preferred_element_type=F32)
    v = jnp.dot(h, win_ref[:, ATTN_WIDTH + KV_WIDTH:ATTN_WIDTH + 2 * KV_WIDTH], preferred_element_type=F32)
    p = jnp.dot(h, win_ref[:, ATTN_WIDTH + 2 * KV_WIDTH:], preferred_element_type=F32)
    ktail_ref[...] = k[tb - WINDOW:]
    vtail_ref[...] = v[tb - WINDOW:]
    ptail_ref[...] = p[tb - 16:]
    k0, k1 = _dup_halves(k)
    v0, v1 = _dup_halves(v)
    k2buf[0, WINDOW:, :] = k0.astype(BF16)
    k2buf[1, WINDOW:, :] = k1.astype(BF16)
    v2buf[0, WINDOW:, 0:LANES] = v0.astype(BF16)
    v2buf[1, WINDOW:, 0:LANES] = v1.astype(BF16)
    pext[16:, :] = p

    lane_q = lax.broadcasted_iota(jnp.int32, (QSUB, LANES), 1)
    lo_q = lane_q < HEAD_DIM

    for u in range(tb // QSUB):
        r0 = u * QSUB
        sel = jnp.where(pid == 0, u + 1, 0) if u < WINDOW // QSUB else 0
        for g in range(N_KV_HEADS):
            qm = qbuf[g, u]
            kwin = jnp.concatenate([km2[g], k2buf[g, r0:r0 + KEYS_SUB, :]], axis=0)
            vwin = jnp.concatenate([vm2[g], v2buf[g, r0:r0 + KEYS_SUB, :]], axis=0)
            s = lax.dot_general(qm, kwin, (((1,), (1,)), ((), ())), preferred_element_type=F32)
            s = s + tbl_ref[sel, g]
            sink = sink_ref[g]
            m = jnp.maximum(jnp.max(s, axis=1, keepdims=True), sink)
            e = jnp.exp(s - m).astype(BF16)
            r = jnp.dot(e, vwin, preferred_element_type=F32)
            o = r[:, 0:LANES] / (r[:, LANES:] + jnp.exp(sink - m))
            o0 = jnp.where(lo_q, o[0:QSUB], o[QSUB:2 * QSUB])
            o1 = jnp.where(lo_q, o[2 * QSUB:3 * QSUB], o[3 * QSUB:])
            obuf[r0:r0 + QSUB, (2 * g) * LANES:(2 * g + 1) * LANES] = o0.astype(BF16)
            obuf[r0:r0 + QSUB, (2 * g + 1) * LANES:(2 * g + 2) * LANES] = o1.astype(BF16)

    pooled = _pool_means(pext, tb)
    for gi in range(len(POOL_WINDOWS)):
        y = jnp.dot(pooled[gi].astype(BF16), wpool_ref[gi], preferred_element_type=F32)
        y = y * pscale_ref[:, gi * POOL_GROUP_DIM:(gi + 1) * POOL_GROUP_DIM]
        obuf[:, ATTN_WIDTH + gi * POOL_GROUP_DIM:ATTN_WIDTH + (gi + 1) * POOL_GROUP_DIM] = y.astype(BF16)

    k2buf[:, 0:WINDOW, :] = k2buf[:, tb:tb + WINDOW, :]
    v2buf[:, 0:WINDOW, 0:LANES] = v2buf[:, tb:tb + WINDOW, 0:LANES]
    pext[0:16, :] = pext[tb:tb + 16, :]

    x1 = x_ref[...] + jnp.dot(obuf[...], wout_ref[...], preferred_element_type=F32)
    x1_ref[...] = x1
    h2 = _rms(x1, gffn_ref[...])
    h2_hi = h2.astype(BF16)
    h2_ref[...] = h2_hi
    lgt_ref[...] = _router_logits(rwt_ref, h2, h2_hi) + rb_ref[...]


def _attn_tables(sinks):
    i = np.arange(QSUB)[:, None]
    j = np.arange(NKEY)[None, :]
    jb = j - META_PAD
    rel = i + WINDOW - jb
    band_ok = (jb >= 0) & (rel >= 0) & (rel <= WINDOW)
    meta_ok = (j < N_META) & (i >= 0)
    slopes = np.exp2(-8.0 * np.arange(1, N_HEADS + 1) / N_HEADS)
    tbl = np.empty((3, N_KV_HEADS, GQA_GROUP * QSUB, NKEY), np.float32)
    for var in range(3):
        ok = band_ok if var == 0 else band_ok & (jb >= WINDOW - (var - 1) * QSUB)
        for g in range(N_KV_HEADS):
            for a in range(GQA_GROUP):
                hd = g * GQA_GROUP + a
                bias = np.where(ok, -slopes[hd] * rel, MASKED)
                bias = np.where(meta_ok, 0.0, bias)
                tbl[var, g, a * QSUB:(a + 1) * QSUB] = bias
    sink_col = jnp.repeat(sinks.astype(F32).reshape(N_KV_HEADS, GQA_GROUP, 1), QSUB, axis=2)
    return jnp.asarray(tbl), sink_col.reshape(N_KV_HEADS, GQA_GROUP * QSUB, 1)


def _prompt_mixer(x, meta, gattn, win, wpool, pscale, wout, gffn, rwt, rb, sinks, tail_h2, tail_lgt):
    seq = x.shape[0]
    tb = PROMPT_BLOCK
    n_tail = tail_h2.shape[0]
    assert seq % tb == 0 and tb % WINDOW == 0 and n_tail <= tb
    nblk = seq // tb
    n_tok = seq + n_tail
    tbl, sink_col = _attn_tables(sinks)
    full = lambda *shape: pl.BlockSpec(shape, lambda i: (0,) * len(shape))
    main = lambda i: (jnp.minimum(i, nblk - 1), 0)
    in_width = win.shape[1]
    return pl.pallas_call(
        _prompt_kernel,
        grid=(nblk + 1,),
        in_specs=[
            pl.BlockSpec((tb, D_MODEL), main),
            full(N_META, D_MODEL), full(1, D_MODEL), full(D_MODEL, in_width),
            full(len(POOL_WINDOWS), POOL_GROUP_DIM, POOL_GROUP_DIM), full(1, POOL_WIDTH),
            full(D_MODEL, D_MODEL), full(1, D_MODEL), full(2, N_EXPERTS, D_MODEL), full(N_EXPERTS, 1),
            full(N_KV_HEADS, GQA_GROUP * QSUB, 1), full(3, N_KV_HEADS, GQA_GROUP * QSUB, NKEY),
            full(n_tail, D_MODEL), full(N_EXPERTS, n_tail),
        ],
        out_specs=[
            pl.BlockSpec((tb, D_MODEL), main),
            pl.BlockSpec((tb, D_MODEL), lambda i: (i, 0)),
            pl.BlockSpec((N_EXPERTS, tb), lambda i: (0, i)),
            full(N_META, KV_WIDTH), full(N_META, KV_WIDTH),
            full(WINDOW, KV_WIDTH), full(WINDOW, KV_WIDTH), full(16, POOL_WIDTH),
        ],
        out_shape=[
            jax.ShapeDtypeStruct((seq, D_MODEL), F32),
            jax.ShapeDtypeStruct((n_tok, D_MODEL), BF16),
            jax.ShapeDtypeStruct((N_EXPERTS, n_tok), F32),
            jax.ShapeDtypeStruct((N_META, KV_WIDTH), F32),
            jax.ShapeDtypeStruct((N_META, KV_WIDTH), F32),
            jax.ShapeDtypeStruct((WINDOW, KV_WIDTH), F32),
            jax.ShapeDtypeStruct((WINDOW, KV_WIDTH), F32),
            jax.ShapeDtypeStruct((16, POOL_WIDTH), F32),
        ],
        scratch_shapes=[
            pltpu.VMEM((N_KV_HEADS, WINDOW + tb, LANES), BF16),
            pltpu.VMEM((N_KV_HEADS, WINDOW + tb, 2 * LANES), BF16),
            pltpu.VMEM((N_KV_HEADS, META_PAD, LANES), BF16),
            pltpu.VMEM((N_KV_HEADS, META_PAD, 2 * LANES), BF16),
            pltpu.VMEM((N_KV_HEADS, tb // QSUB, GQA_GROUP * QSUB, LANES), BF16),
            pltpu.VMEM((tb, D_MODEL), BF16),
            pltpu.VMEM((16 + tb, POOL_WIDTH), F32),
        ],
        compiler_params=pltpu.CompilerParams(dimension_semantics=("arbitrary",),
                                             vmem_limit_bytes=VMEM_LIMIT),
        name="prompt_mixer",
    )(x, meta, gattn, win, wpool, pscale, wout, gffn, rwt, rb, sink_col, tbl, tail_h2, tail_lgt)


def _sample_kernel(x_ref, ck_ref, cv_ref, sp_ref, gattn_ref, win_ref, wpool_ref, pscale_ref, wout_ref,
                   gffn_ref, rwt_ref, rb_ref, sinkc_ref, bias_ref,
                   x1_ref, h2_ref, lgt_ref, knew_ref, vnew_ref, pnew_ref,
                   qm_buf, r_buf, obuf):
    nb = x_ref.shape[0]
    x = x_ref[...]
    h = _rms(x, gattn_ref[...]).astype(BF16)
    q = jnp.dot(h, win_ref[:, 0:ATTN_WIDTH], preferred_element_type=F32) * (HEAD_DIM ** -0.5)
    k = jnp.dot(h, win_ref[:, ATTN_WIDTH:ATTN_WIDTH + KV_WIDTH], preferred_element_type=F32)
    v = jnp.dot(h, win_ref[:, ATTN_WIDTH + KV_WIDTH:ATTN_WIDTH + 2 * KV_WIDTH], preferred_element_type=F32)
    p = jnp.dot(h, win_ref[:, ATTN_WIDTH + 2 * KV_WIDTH:], preferred_element_type=F32)
    knew_ref[...] = k
    vnew_ref[...] = v
    pnew_ref[...] = p

    lane = lax.broadcasted_iota(jnp.int32, (nb, LANES), 1)
    lo = lane < HEAD_DIM
    for hd in range(N_HEADS):
        tile = q[:, (hd // 2) * LANES:(hd // 2 + 1) * LANES]
        if (hd % 2) != (hd // GQA_GROUP):
            tile = pltpu.roll(tile, HEAD_DIM, axis=1)
        keep_lo = (hd // GQA_GROUP) == 0
        qm_buf[:, hd, :] = jnp.where(lo if keep_lo else jnp.logical_not(lo), tile, 0.0)

    unroll = 8
    assert nb % unroll == 0
    nt = (((1,), (1,)), ((), ()))
    bias = jnp.concatenate([bias_ref[...]] * unroll, axis=0)
    sink = jnp.concatenate([sinkc_ref[...]] * unroll, axis=0)

    def batch_group(gidx, carry):
        bs = [gidx * unroll + j for j in range(unroll)]
        qms = [qm_buf[b] for b in bs]
        s = jnp.concatenate(
            [lax.dot_general(qm.astype(BF16), ck_ref[b].astype(BF16), nt, preferred_element_type=F32)
             for qm, b in zip(qms, bs)], axis=0) + bias
        s_self = jnp.concatenate(
            [jnp.sum(qm * knew_ref[pl.ds(b, 1), :], axis=1, keepdims=True) for qm, b in zip(qms, bs)],
            axis=0)
        m = jnp.maximum(jnp.maximum(jnp.max(s, axis=1, keepdims=True), s_self), sink)
        e = jnp.exp(s - m)
        e_self = jnp.exp(s_self - m)
        inv_den = 1.0 / (jnp.sum(e, axis=1, keepdims=True) + e_self + jnp.exp(sink - m))
        for j, b in enumerate(bs):
            rows = slice(j * N_HEADS, (j + 1) * N_HEADS)
            r = jnp.dot(e[rows].astype(BF16), cv_ref[b].astype(BF16), preferred_element_type=F32)
            r = r + e_self[rows] * vnew_ref[pl.ds(b, 1), :]
            r_buf[b] = r * inv_den[rows]
        return carry

    lax.fori_loop(0, nb // unroll, batch_group, 0)

    for c in range(N_HEADS // 2):
        halves = []
        for a in range(2):
            hd = 2 * c + a
            t = r_buf[:, hd, :]
            if (hd // GQA_GROUP) != a:
                t = pltpu.roll(t, HEAD_DIM, axis=1)
            halves.append(t)
        obuf[:, c * LANES:(c + 1) * LANES] = jnp.where(lo, halves[0], halves[1]).astype(BF16)

    for gi, w in enumerate(POOL_WINDOWS):
        cols = slice(gi * POOL_GROUP_DIM, (gi + 1) * POOL_GROUP_DIM)
        pg = p[:, cols]
        acc = pg
        for d in range(1, w):
            acc = acc + sp_ref[:, POOL_STATE - d, cols]
        pooled = acc * (1.0 / w) - pg
        y = jnp.dot(pooled.astype(BF16), wpool_ref[gi], preferred_element_type=F32) * pscale_ref[:, cols]
        obuf[:, ATTN_WIDTH + gi * POOL_GROUP_DIM:ATTN_WIDTH + (gi + 1) * POOL_GROUP_DIM] = y.astype(BF16)

    x1 = x + jnp.dot(obuf[...], wout_ref[...], preferred_element_type=F32)
    x1_ref[...] = x1
    h2 = _rms(x1, gffn_ref[...])
    h2_hi = h2.astype(BF16)
    h2_ref[...] = h2_hi
    lgt_ref[...] = _router_logits(rwt_ref, h2, h2_hi) + rb_ref[...]


def _sample_mixer(x, ck, cv, sp, gattn, win, wpool, pscale, wout, gffn, rwt, rb, sinks):
    nb = x.shape[0]
    rows = ck.shape[1]
    slopes = np.exp2(-8.0 * np.arange(1, N_HEADS + 1) / N_HEADS)
    dist = np.concatenate([np.zeros(N_META), WINDOW - np.arange(WINDOW)])
    bias = jnp.asarray((-slopes[:, None] * dist[None, :]).astype(np.float32))
    vm = pl.BlockSpec(memory_space=pltpu.VMEM)
    return pl.pallas_call(
        _sample_kernel,
        in_specs=[vm] * 14,
        out_specs=[vm] * 6,
        out_shape=[
            jax.ShapeDtypeStruct((nb, D_MODEL), F32),
            jax.ShapeDtypeStruct((nb, D_MODEL), BF16),
            jax.ShapeDtypeStruct((N_EXPERTS, nb), F32),
            jax.ShapeDtypeStruct((nb, KV_WIDTH), F32),
            jax.ShapeDtypeStruct((nb, KV_WIDTH), F32),
            jax.ShapeDtypeStruct((nb, POOL_WIDTH), F32),
        ],
        scratch_shapes=[
            pltpu.VMEM((nb, N_HEADS, LANES), F32),
            pltpu.VMEM((nb, N_HEADS, LANES), F32),
            pltpu.VMEM((nb, D_MODEL), BF16),
        ],
        compiler_params=pltpu.CompilerParams(vmem_limit_bytes=VMEM_LIMIT),
        name="sample_mixer",
    )(x, ck, cv, sp, gattn, win, wpool, pscale, wout, gffn, rwt, rb,
      sinks.astype(F32).reshape(N_HEADS, 1), bias)


def _router_kernel(lg_ref, tri_ref, low_ref, gate_ref, col_ref, lrank_ref, lpos_ref,
                   tcar_ref, cnt_ref, carry):
    tr = lg_ref.shape[1]

    @pl.when(pl.program_id(0) == 0)
    def _():
        carry[...] = jnp.zeros_like(carry)

    work = lg_ref[...]
    eio = lax.broadcasted_iota(jnp.int32, work.shape, 0).astype(F32)
    sels, vals, idxs = [], [], []
    for _k in range(TOP_K):
        mx = jnp.max(work, axis=0, keepdims=True)
        idx = jnp.min(jnp.where(work == mx, eio, float(N_EXPERTS)), axis=0, keepdims=True)
        sel = eio == idx
        sels.append(sel)
        vals.append(mx)
        idxs.append(idx)
        work = jnp.where(sel, -jnp.inf, work)
    exps = [jnp.exp(vk - vals[0]) for vk in vals]
    tot = exps[0] + exps[1] + exps[2] + exps[3]
    onehot = jnp.zeros(work.shape, F32)
    for sel in sels:
        onehot = onehot + sel.astype(F32)
    before = jnp.dot(onehot.astype(BF16), tri_ref[...], preferred_element_type=F32) + carry[...]
    for kk in range(TOP_K):
        gate_ref[pl.ds(kk, 1), :] = exps[kk] / tot
    for j in range(tr // COMBINE_TILE):
        cols = slice(j * COMBINE_TILE, (j + 1) * COMBINE_TILE)
        tc = before[:, j * COMBINE_TILE:j * COMBINE_TILE + 1]
        tcar_ref[j] = tc.astype(jnp.int32)
        slack = tc - WIN_ALIGN * jnp.floor(tc * (1.0 / WIN_ALIGN))
        local = before[:, cols] - tc
        for kk in range(TOP_K):
            selk = sels[kk][:, cols]
            lr = jnp.sum(jnp.where(selk, local, 0.0), axis=0, keepdims=True)
            sl = jnp.sum(jnp.where(selk, slack, 0.0), axis=0, keepdims=True)
            lrank_ref[pl.ds(kk, 1), cols] = lr.astype(jnp.int32)
            col_ref[pl.ds(kk, 1), cols] = (idxs[kk][:, cols] * float(WIN_ROWS) + sl + lr).astype(jnp.int32)
    for j in range(tr // DISPATCH_TILE):
        cols = slice(j * DISPATCH_TILE, (j + 1) * DISPATCH_TILE)
        local = before[:, cols] - before[:, j * DISPATCH_TILE:j * DISPATCH_TILE + 1]
        tile_cnt = jnp.broadcast_to(jnp.sum(onehot[:, cols], axis=1, keepdims=True), local.shape)
        cnt_hi = jnp.floor(tile_cnt * (1.0 / 256.0))
        cnt_lo = tile_cnt - 256.0 * cnt_hi
        run_start = (256.0 * jnp.dot(low_ref[...], cnt_hi.astype(BF16), preferred_element_type=F32)
                     + jnp.dot(low_ref[...], cnt_lo.astype(BF16), preferred_element_type=F32))
        for kk in range(TOP_K):
            lp = jnp.sum(jnp.where(sels[kk][:, cols], run_start + local, 0.0), axis=0, keepdims=True)
            lpos_ref[pl.ds(kk, 1), cols] = lp.astype(jnp.int32)
    carry[...] = carry[...] + jnp.sum(onehot, axis=1, keepdims=True)
    cnt_ref[...] = carry[...].astype(jnp.int32)


def _router(logits_t):
    n = logits_t.shape[1]
    tr = ROUTE_BLOCK
    assert n % tr == 0
    tri = jnp.asarray(np.triu(np.ones((tr, tr), np.float32), k=1), BF16)
    low = jnp.asarray(np.tril(np.ones((N_EXPERTS, N_EXPERTS), np.float32), k=-1), BF16)
    per_tok = pl.BlockSpec((TOP_K, tr), lambda i: (0, i))
    return pl.pallas_call(
        _router_kernel,
        grid=(n // tr,),
        in_specs=[pl.BlockSpec((N_EXPERTS, tr), lambda i: (0, i)),
                  pl.BlockSpec((tr, tr), lambda i: (0, 0)),
                  pl.BlockSpec((N_EXPERTS, N_EXPERTS), lambda i: (0, 0))],
        out_specs=[per_tok, per_tok, per_tok, per_tok,
                   pl.BlockSpec((tr // COMBINE_TILE, N_EXPERTS, 1), lambda i: (i, 0, 0)),
                   pl.BlockSpec((N_EXPERTS, 1), lambda i: (0, 0))],
        out_shape=[jax.ShapeDtypeStruct((TOP_K, n), F32),
                   jax.ShapeDtypeStruct((TOP_K, n), jnp.int32),
                   jax.ShapeDtypeStruct((TOP_K, n), jnp.int32),
                   jax.ShapeDtypeStruct((TOP_K, n), jnp.int32),
                   jax.ShapeDtypeStruct((n // COMBINE_TILE, N_EXPERTS, 1), jnp.int32),
                   jax.ShapeDtypeStruct((N_EXPERTS, 1), jnp.int32)],
        scratch_shapes=[pltpu.VMEM((N_EXPERTS, 1), F32)],
        compiler_params=pltpu.CompilerParams(dimension_semantics=("arbitrary",)),
        name="router",
    )(logits_t, tri, low)


def _dispatch_kernel(lstart_ref, cnt_ref, dst_ref, h2_ref, lpos_ref, xs_hbm, stg0, stg1, sem):
    i = pl.program_id(0)
    n_tiles = pl.num_programs(0)
    dt = h2_ref.shape[0]
    rows = dt * TOP_K
    slot = i % 2
    stgs = (stg0, stg1)
    pieces = [p for p in (256, 128, 64, 32, 16, 8, 4, 2, 1) if p <= dt]
    assert dt < 512

    def drain(s):
        pltpu.make_async_copy(stgs[s], xs_hbm.at[pl.ds(0, rows)], sem.at[s]).wait()

    def issue_runs(tile, live, s):
        for e in range(N_EXPERTS):
            n = jnp.where(live, cnt_ref[tile * N_EXPERTS + e], 0)
            src0 = lstart_ref[tile * N_EXPERTS + e]
            dst0 = dst_ref[tile * N_EXPERTS + e]
            for piece in pieces:
                off = n & ~jnp.int32(2 * piece - 1)

                @pl.when((n & piece) != 0)
                def _(off=off, piece=piece, src0=src0, dst0=dst0):
                    pltpu.make_async_copy(stgs[s].at[pl.ds(src0 + off, piece)],
                                          xs_hbm.at[pl.ds(dst0 + off, piece)], sem.at[s]).start()

    def sort_tile(s):
        rid = lax.broadcasted_iota(jnp.int32, (rows, dt), 0)
        hit = rid == lpos_ref[0:1, :]
        for kk in range(1, TOP_K):
            hit = jnp.logical_or(hit, rid == lpos_ref[kk:kk + 1, :])
        perm = jnp.where(hit, 1.0, 0.0).astype(BF16)
        srt = jnp.dot(perm, h2_ref[...], preferred_element_type=F32)
        packed = _pack_exact_bf16_pairs(srt)
        for c in range(PACK_CHUNKS):
            stgs[s][:, c, :] = packed[:, c * LANES:(c + 1) * LANES]

    for s in range(2):
        @pl.when(slot == s)
        def _(s=s):
            @pl.when(i >= 2)
            def _():
                drain(s)

            issue_runs(jnp.maximum(i - 1, 0), i >= 1, 1 - s)
            sort_tile(s)

            @pl.when(i == n_tiles - 1)
            def _():
                issue_runs(i, True, s)
                drain(s)

                @pl.when(n_tiles >= 2)
                def _():
                    drain(1 - s)


def _dispatch(h2, lpos, lstart, cnt, dst, cap):
    n_tok = h2.shape[0]
    dt = DISPATCH_TILE
    assert n_tok % dt == 0
    grid_spec = pltpu.PrefetchScalarGridSpec(
        num_scalar_prefetch=3,
        grid=(n_tok // dt,),
        in_specs=[pl.BlockSpec((dt, D_MODEL), lambda i, a, b, c: (i, 0)),
                  pl.BlockSpec((TOP_K, dt), lambda i, a, b, c: (0, i)),
                  ],
        out_specs=pl.BlockSpec(memory_space=pl.ANY),
        scratch_shapes=[pltpu.VMEM((dt * TOP_K, PACK_CHUNKS, LANES), jnp.uint32),
                        pltpu.VMEM((dt * TOP_K, PACK_CHUNKS, LANES), jnp.uint32),
                        pltpu.SemaphoreType.DMA((2,))],
    )
    return pl.pallas_call(
        _dispatch_kernel,
        grid_spec=grid_spec,
        out_shape=jax.ShapeDtypeStruct((cap, PACK_CHUNKS, LANES), jnp.uint32),
        compiler_params=pltpu.CompilerParams(dimension_semantics=("arbitrary",),
                                             vmem_limit_bytes=VMEM_LIMIT),
        name="dispatch",
    )(lstart, cnt, dst, h2, lpos)


def _expert_kernel(n_xblocks, bexp_ref, nvalid_ref, epos_ref, elist_ref,
                   x_hbm, wgu_hbm, bgu_ref, wd_hbm, bd_ref, y_ref,
                   wgu_f32, wd_f32, wgu_bf, wd_bf, xbuf, xsem, wsem):
    i = pl.program_id(0)
    rb = y_ref.shape[0]
    nvalid = nvalid_ref[i]
    pos = epos_ref[i]
    fresh = jnp.logical_or(i == 0, pos != epos_ref[jnp.maximum(i - 1, 0)])
    slot = i % 2

    def x_copies(blk, s):
        return [pltpu.make_async_copy(x_hbm.at[pl.ds(blk * rb, rb), c, :],
                                      xbuf.at[s, :, pl.ds(c * LANES, LANES)], xsem.at[s])
                for c in range(PACK_CHUNKS)]

    def w_copies(p):
        e = elist_ref[p]
        s = p % 2
        return [pltpu.make_async_copy(wgu_hbm.at[e], wgu_f32.at[s], wsem.at[s, 0]),
                pltpu.make_async_copy(wd_hbm.at[e], wd_f32.at[s], wsem.at[s, 1])]

    @pl.when(i == 0)
    def _():
        for cp in x_copies(0, 0):
            cp.start()

        @pl.when(nvalid > 0)
        def _():
            for cp in w_copies(0):
                cp.start()

    @pl.when(i + 1 < n_xblocks)
    def _():
        for cp in x_copies(i + 1, 1 - slot):
            cp.start()

    @pl.when(jnp.logical_and(fresh, nvalid > 0))
    def _():
        @pl.when(elist_ref[pos + 1] >= 0)
        def _():
            for cp in w_copies(pos + 1):
                cp.start()

        for cp in w_copies(pos):
            cp.wait()
        ws = pos % 2
        chunk = 32

        def cast_gu(r, c):
            r0 = pl.multiple_of(r * chunk, chunk)
            wgu_bf[pl.ds(r0, chunk), :] = wgu_f32[ws, pl.ds(r0, chunk), :].astype(BF16)
            return c

        def cast_d(r, c):
            r0 = pl.multiple_of(r * chunk, chunk)
            wd_bf[pl.ds(r0, chunk), :] = wd_f32[ws, pl.ds(r0, chunk), :].astype(BF16)
            return c

        lax.fori_loop(0, D_MODEL // chunk, cast_gu, 0)
        lax.fori_loop(0, D_EXPERT // chunk, cast_d, 0)

    @pl.when(i < n_xblocks)
    def _():
        for cp in x_copies(i, slot):
            cp.wait()

    def ffn(rows):
        xw = xbuf[slot, 0:rows, :]
        xw = jnp.where(lax.broadcasted_iota(jnp.int32, xw.shape, 0) < nvalid, xw, jnp.uint32(0))
        xlo, xhi = _unpack_bf16_pairs(xw)
        half = D_MODEL // 2
        g = (jnp.dot(xlo, wgu_bf[0:half, 0:D_EXPERT], preferred_element_type=F32)
             + jnp.dot(xhi, wgu_bf[half:, 0:D_EXPERT], preferred_element_type=F32)
             + bgu_ref[0, :, 0:D_EXPERT])
        u = (jnp.dot(xlo, wgu_bf[0:half, D_EXPERT:], preferred_element_type=F32)
             + jnp.dot(xhi, wgu_bf[half:, D_EXPERT:], preferred_element_type=F32)
             + bgu_ref[0, :, D_EXPERT:])
        g = jnp.minimum(g, SWIGLU_LIMIT)
        u = jnp.clip(u, -SWIGLU_LIMIT, SWIGLU_LIMIT)
        act = g * (1.0 / (1.0 + jnp.exp(-SWIGLU_ALPHA * g))) * (u + 1.0)
        y = jnp.dot(act.astype(BF16), wd_bf[...], preferred_element_type=F32) + bd_ref[0]
        row = lax.broadcasted_iota(jnp.int32, y.shape, 0)
        y_ref[0:rows, :] = jnp.where(row < nvalid, y, 0.0).astype(BF16)
        if rows < rb:
            y_ref[rows:, :] = jnp.zeros((rb - rows, D_MODEL), BF16)

    @pl.when(nvalid > rb // 2)
    def _():
        ffn(rb)

    @pl.when(jnp.logical_and(nvalid > 0, nvalid <= rb // 2))
    def _():
        ffn(rb // 2)

    @pl.when(nvalid == 0)
    def _():
        y_ref[...] = jnp.zeros_like(y_ref)


def _experts(xs, block_expert, nvalid, block_pos, expert_list, wgu, bgu, wd, bd):
    rb = EXPERT_ROWS
    n_xblocks = xs.shape[0] // rb
    nblk = n_xblocks + 1
    any_space = pl.BlockSpec(memory_space=pl.ANY)
    grid_spec = pltpu.PrefetchScalarGridSpec(
        num_scalar_prefetch=4,
        grid=(nblk,),
        in_specs=[
            any_space,
            any_space,
            pl.BlockSpec((1, 1, 2 * D_EXPERT), lambda i, be, nu, ep, el: (be[i], 0, 0)),
            any_space,
            pl.BlockSpec((1, 1, D_MODEL), lambda i, be, nu, ep, el: (be[i], 0, 0)),
        ],
        out_specs=pl.BlockSpec((rb, D_MODEL), lambda i, be, nu, ep, el: (i, 0)),
        scratch_shapes=[pltpu.VMEM((2, D_MODEL, 2 * D_EXPERT), F32),
                        pltpu.VMEM((2, D_EXPERT, D_MODEL), F32),
                        pltpu.VMEM((D_MODEL, 2 * D_EXPERT), BF16),
                        pltpu.VMEM((D_EXPERT, D_MODEL), BF16),
                        pltpu.VMEM((2, rb, D_MODEL // 2), jnp.uint32),
                        pltpu.SemaphoreType.DMA((2,)),
                        pltpu.SemaphoreType.DMA((2, 2))],
    )
    return pl.pallas_call(
        functools.partial(_expert_kernel, n_xblocks),
        grid_spec=grid_spec,
        out_shape=jax.ShapeDtypeStruct((nblk * rb, D_MODEL), BF16),
        compiler_params=pltpu.CompilerParams(dimension_semantics=("arbitrary",),
                                             vmem_limit_bytes=VMEM_LIMIT),
        name="experts",
    )(block_expert, nvalid, block_pos, expert_list, xs, wgu, bgu.reshape(N_EXPERTS, 1, 2 * D_EXPERT), wd,
      bd.reshape(N_EXPERTS, 1, D_MODEL))


def _combine_kernel(n_prompt_tiles, offa_ref, nchunk_ref, tail_ref, ntail_ref,
                    x1p_ref, x1s_ref, info_ref, info_next_ref, gfin_ref, yb_hbm,
                    outp_ref, outs_ref, ybuf, gbuf0, gbuf1, acc_ref, sem, tsem):
    i = pl.program_id(0)
    n_tiles = pl.num_programs(0)

    def window_copy(tile, chunk, e, slot):
        base = pl.multiple_of(offa_ref[tile * N_EXPERTS + e] + chunk * WIN, WIN_ALIGN)
        return pltpu.make_async_copy(yb_hbm.at[pl.ds(base, WIN_ROWS), :],
                                     ybuf.at[slot, pl.ds(e * WIN_ROWS, WIN_ROWS), :],
                                     sem.at[slot])

    def start_windows(tile, chunk, slot):
        for e in range(N_EXPERTS):
            window_copy(tile, chunk, e, slot).start()

    def wait_windows(slot):
        pltpu.make_async_copy(yb_hbm.at[pl.ds(0, N_EXPERTS * WIN_ROWS), :], ybuf.at[slot],
                              sem.at[slot]).wait()

    def tail_copy(tile, e, slot):
        base = pl.multiple_of(offa_ref[tile * N_EXPERTS + e] + WIN_HEAD, WIN_ALIGN)
        return pltpu.make_async_copy(yb_hbm.at[pl.ds(base, WIN_ROWS - WIN_HEAD), :],
                                     ybuf.at[slot, pl.ds(e * WIN_ROWS + WIN_HEAD, WIN_ROWS - WIN_HEAD), :],
                                     tsem.at[slot])

    def start_first_chunk(tile, slot):
        for e in range(N_EXPERTS):
            base = pl.multiple_of(offa_ref[tile * N_EXPERTS + e], WIN_ALIGN)
            pltpu.make_async_copy(yb_hbm.at[pl.ds(base, WIN_HEAD), :],
                                  ybuf.at[slot, pl.ds(e * WIN_ROWS, WIN_HEAD), :], sem.at[slot]).start()

            @pl.when(tail_ref[tile * N_EXPERTS + e] != 0)
            def _(e=e):
                tail_copy(tile, e, slot).start()

    def wait_first_chunk(tile, slot):
        pltpu.make_async_copy(yb_hbm.at[pl.ds(0, N_EXPERTS * WIN_HEAD), :],
                              ybuf.at[slot, pl.ds(0, N_EXPERTS * WIN_HEAD), :], sem.at[slot]).wait()

        def one_tail(t, c):
            tail_copy(tile, 0, slot).wait()
            return c

        lax.fori_loop(0, ntail_ref[tile], one_tail, 0)

    slot = i % 2

    @pl.when(i == 0)
    def _():
        ybuf[...] = jnp.zeros_like(ybuf)
        start_first_chunk(0, 0)

    lane = lax.broadcasted_iota(jnp.int32, (COMBINE_TILE, N_EXPERTS * WIN_ROWS), 1)

    def gate_matrix(ref, chunk):
        g = jnp.zeros(lane.shape, F32)
        for kk in range(TOP_K):
            lr = ref[:, TOP_K + kk:TOP_K + kk + 1]
            in_chunk = jnp.logical_and(lr >= chunk * WIN, lr < chunk * WIN + WIN)
            colk = jnp.where(in_chunk, ref[:, kk:kk + 1] - chunk * WIN, -1.0).astype(jnp.int32)
            g = jnp.where(lane == colk, ref[:, 2 * TOP_K + kk:2 * TOP_K + kk + 1], g)
        return g.astype(BF16)

    def moe_rows(gm, buf):
        return jnp.dot(gm, ybuf[buf], preferred_element_type=F32)

    @pl.when(i == 0)
    def _():
        gbuf0[...] = gate_matrix(info_ref, 0)

    def main(s):
        start_first_chunk(jnp.minimum(i + 1, n_tiles - 1), 1 - s)
        wait_first_chunk(i, s)
        g_cur, g_nxt = (gbuf0, gbuf1) if s == 0 else (gbuf1, gbuf0)
        acc_ref[...] = moe_rows(g_cur[...], s)
        g_nxt[...] = gate_matrix(info_next_ref, 0)

    for s in range(2):
        @pl.when(slot == s)
        def _(s=s):
            main(s)

    @pl.when(i == n_tiles - 1)
    def _():
        for s in range(2):
            @pl.when(slot == s)
            def _(s=s):
                wait_first_chunk(i, 1 - s)

    def extra_chunk(j, c):
        start_windows(i, j, 2)
        wait_windows(2)
        acc_ref[...] += moe_rows(gate_matrix(info_ref, j), 2)
        return c

    lax.fori_loop(1, nchunk_ref[i], extra_chunk, 0)

    @pl.when(i < n_prompt_tiles)
    def _():
        outp_ref[...] = _rms(x1p_ref[...] + acc_ref[...], gfin_ref[...])

    @pl.when(i >= n_prompt_tiles)
    def _():
        outs_ref[...] = _rms(x1s_ref[...] + acc_ref[...], gfin_ref[...])


def _combine(x1_p, x1_s, tok_info, gfin, yb, offa, nchunk, need_tail, ntail):
    ct = COMBINE_TILE
    n_p, n_s = x1_p.shape[0] // ct, x1_s.shape[0] // ct
    assert x1_p.shape[0] % ct == 0 and x1_s.shape[0] % ct == 0 and n_s >= 1
    n_info = tok_info.shape[1]
    grid_spec = pltpu.PrefetchScalarGridSpec(
        num_scalar_prefetch=4,
        grid=(n_p + n_s,),
        in_specs=[
            pl.BlockSpec((ct, D_MODEL), lambda i, *_: (jnp.minimum(i, n_p - 1), 0)),
            pl.BlockSpec((ct, D_MODEL), lambda i, *_: (jnp.maximum(i - n_p, 0), 0)),
            pl.BlockSpec((ct, n_info), lambda i, *_: (i, 0)),
            pl.BlockSpec((ct, n_info), lambda i, *_: (jnp.minimum(i + 1, n_p + n_s - 1), 0)),
            pl.BlockSpec((1, D_MODEL), lambda i, *_: (0, 0)),
            pl.BlockSpec(memory_space=pl.ANY),
        ],
        out_specs=[
            pl.BlockSpec((ct, D_MODEL), lambda i, *_: (jnp.minimum(i, n_p - 1), 0)),
            pl.BlockSpec((ct, D_MODEL), lambda i, *_: (jnp.maximum(i - n_p, 0), 0)),
        ],
        scratch_shapes=[pltpu.VMEM((3, N_EXPERTS * WIN_ROWS, D_MODEL), BF16),
                        pltpu.VMEM((ct, N_EXPERTS * WIN_ROWS), BF16),
                        pltpu.VMEM((ct, N_EXPERTS * WIN_ROWS), BF16),
                        pltpu.VMEM((ct, D_MODEL), F32),
                        pltpu.SemaphoreType.DMA((3,)),
                        pltpu.SemaphoreType.DMA((2,))],
    )
    return pl.pallas_call(
        functools.partial(_combine_kernel, n_p),
        grid_spec=grid_spec,
        out_shape=[jax.ShapeDtypeStruct(x1_p.shape, F32), jax.ShapeDtypeStruct(x1_s.shape, F32)],
        compiler_params=pltpu.CompilerParams(dimension_semantics=("arbitrary",),
                                             vmem_limit_bytes=VMEM_LIMIT),
        name="combine",
    )(offa, nchunk, need_tail, ntail, x1_p, x1_s, tok_info, tok_info, gfin, yb)


def kernel(x_prompt, x_sample, cache_k, cache_v, state_pool, meta_tokens, norm_attn, w_in, attn_sinks,
           w_pool, pool_scale, w_out, norm_ffn, router_w, router_b, w_gate_up, b_gate_up, w_down, b_down,
           norm_final):
    assert w_in.shape[0] == 1, "single-layer trunk"
    bsz, seq, _ = x_prompt.shape
    assert bsz == 1
    nb = x_sample.shape[0]
    n_tok = seq + nb
    gattn = norm_attn[0].reshape(1, D_MODEL)
    gffn = norm_ffn[0].reshape(1, D_MODEL)
    win = w_in[0].astype(BF16)
    wpool = w_pool[0].astype(BF16)
    wout = w_out[0].astype(BF16)
    pscale = pool_scale[0].reshape(1, POOL_WIDTH)
    rw_t = router_w[0].T
    rw_hi = rw_t.astype(BF16)
    rwt = jnp.stack([rw_hi, (rw_t - rw_hi.astype(F32)).astype(BF16)])
    rb = router_b[0].reshape(N_EXPERTS, 1)
    sinks = attn_sinks[0]

    ck = cache_k[0].reshape(nb, N_META + WINDOW, KV_WIDTH)
    cv = cache_v[0].reshape(nb, N_META + WINDOW, KV_WIDTH)
    (x1_s, h2_s, lgt_s, knew, vnew, pnew) = _sample_mixer(
        x_sample[:, 0], ck, cv, state_pool[0], gattn, win, wpool, pscale, wout, gffn, rwt, rb, sinks)
    (x1_p, h2_all, lgt_all, kmeta, vmeta, ktail, vtail, ptail) = _prompt_mixer(
        x_prompt[0], meta_tokens, gattn, win, wpool, pscale, wout, gffn, rwt, rb, sinks, h2_s, lgt_s)

    gates, col, lrank, lpos, tcar, counts = _router(lgt_all)
    counts = counts[:, 0]
    tcar = tcar[:, :, 0]
    rbk = EXPERT_ROWS
    eids = jnp.arange(N_EXPERTS, dtype=jnp.int32)
    earlier = eids[None, :] < eids[:, None]
    excl_sum = lambda a: jnp.sum(jnp.where(earlier, a[..., None, :], 0), axis=-1)
    padded = (counts + rbk - 1) // rbk * rbk
    pad_start = excl_sum(padded).astype(jnp.int32)
    pad_end = pad_start + padded
    nblk = -(-(n_tok * TOP_K) // rbk) + N_EXPERTS
    cap = nblk * rbk
    block_start = jnp.arange(nblk + 1, dtype=jnp.int32) * rbk
    owns = (pad_start[None, :] <= block_start[:, None]) & (block_start[:, None] < pad_end[None, :])
    nvalid = jnp.sum(jnp.where(owns, jnp.clip(counts[None, :] - (block_start[:, None] - pad_start[None, :]),
                                              0, rbk), 0), axis=1).astype(jnp.int32)
    has_rows = counts > 0
    last_e = jnp.max(jnp.where(has_rows, eids, 0))
    block_expert = jnp.where(jnp.any(owns, axis=1), jnp.sum(jnp.where(owns, eids[None, :], 0), axis=1),
                             last_e).astype(jnp.int32)

    run_len = jnp.concatenate([tcar[1:], counts[None, :]], axis=0) - tcar
    dcar = tcar[::DISPATCH_TILE // COMBINE_TILE]
    drun_len = jnp.concatenate([dcar[1:], counts[None, :]], axis=0) - dcar
    flat = lambda a: a.astype(jnp.int32).reshape(-1)
    xs = _dispatch(h2_all, lpos, flat(excl_sum(drun_len)), flat(drun_len), flat(pad_start[None, :] + dcar),
                   cap)
    expert_pos = excl_sum(has_rows.astype(jnp.int32))
    at_pos = has_rows[None, :] & (expert_pos[None, :] == jnp.arange(N_EXPERTS + 1, dtype=jnp.int32)[:, None])
    expert_list = jnp.where(jnp.any(at_pos, axis=1), jnp.sum(jnp.where(at_pos, eids[None, :], 0), axis=1),
                            -1).astype(jnp.int32)
    block_pos = jnp.sum(jnp.where(block_expert[:, None] == eids[None, :], expert_pos[None, :], 0),
                        axis=1).astype(jnp.int32)
    yb = _experts(xs, block_expert, nvalid, block_pos, expert_list,
                  w_gate_up[0], b_gate_up[0], w_down[0], b_down[0])

    offa = (pad_start[None, :] + (tcar - tcar % WIN_ALIGN)).astype(jnp.int32).reshape(-1)
    nchunk = jnp.maximum(jnp.max((run_len + WIN - 1) // WIN, axis=1), 1).astype(jnp.int32)
    need_tail = (tcar % WIN_ALIGN + jnp.minimum(run_len, WIN)) > WIN_HEAD
    ntail = jnp.sum(need_tail, axis=1).astype(jnp.int32)
    gfin = norm_final.reshape(1, D_MODEL)
    tok_info = jnp.concatenate([col.astype(F32), lrank.astype(F32), gates], axis=0).T
    y_prompt, y_sample = _combine(x1_p, x1_s, tok_info, gfin, yb, offa, nchunk,
                                  need_tail.astype(jnp.int32).reshape(-1), ntail)

    kv_shape = (1, 1, N_META + WINDOW, N_KV_HEADS, HEAD_DIM)
    new_k_p = jnp.concatenate([kmeta, ktail], axis=0).reshape(kv_shape)
    new_v_p = jnp.concatenate([vmeta, vtail], axis=0).reshape(kv_shape)
    new_pool_p = ptail[16 - POOL_STATE:].reshape(1, 1, POOL_STATE, POOL_WIDTH)
    new_k_s = jnp.concatenate([ck[:, :N_META], ck[:, N_META + 1:], knew[:, None]], axis=1).reshape(
        (1, nb, N_META + WINDOW, N_KV_HEADS, HEAD_DIM))
    new_v_s = jnp.concatenate([cv[:, :N_META], cv[:, N_META + 1:], vnew[:, None]], axis=1).reshape(
        (1, nb, N_META + WINDOW, N_KV_HEADS, HEAD_DIM))
    new_pool_s = jnp.concatenate([state_pool[0][:, 1:], pnew[:, None]], axis=1)[None]
    return (y_prompt[None], y_sample[:, None], new_k_p, new_v_p, new_pool_p, new_k_s, new_v_s, new_pool_s)
```

```python
import functools

import jax
import jax.numpy as jnp
import numpy as np
from jax import lax
from jax.experimental import pallas as pl
from jax.experimental.pallas import tpu as pltpu

F32 = jnp.float32
BF16 = jnp.bfloat16

D_MODEL = 1024
N_META = 16
N_HEADS = 8
HEAD_DIM = 64
N_KV_HEADS = 2
GQA_GROUP = N_HEADS // N_KV_HEADS
ATTN_WIDTH = N_HEADS * HEAD_DIM
KV_WIDTH = N_KV_HEADS * HEAD_DIM
WINDOW = 128
POOL_WIDTH = D_MODEL - ATTN_WIDTH
POOL_WINDOWS = (2, 4, 8, 16)
POOL_GROUP_DIM = POOL_WIDTH // len(POOL_WINDOWS)
POOL_STATE = max(POOL_WINDOWS) - 1
N_EXPERTS = 32
TOP_K = 4
D_EXPERT = D_MODEL
SWIGLU_ALPHA = 1.702
SWIGLU_LIMIT = 7.0
NORM_EPS = 1e-5
PAST_LEN = 16384

LANES = 128
QSUB = 64
KEYS_SUB = QSUB + WINDOW
META_PAD = 64
NKEY = META_PAD + KEYS_SUB
MASKED = -1e30
PROMPT_BLOCK = 1024
ROUTE_BLOCK = 384
EXPERT_ROWS = 512
DISPATCH_TILE = 384
PACK_CHUNKS = D_MODEL // 2 // LANES
COMBINE_TILE = 128
WIN = 32
WIN_ALIGN = 16
WIN_ROWS = WIN + WIN_ALIGN
WIN_HEAD = 32
VMEM_LIMIT = 56 * 1024 * 1024


def _rms(x, g):
    return x * lax.rsqrt(jnp.mean(x * x, axis=-1, keepdims=True) + NORM_EPS) * g


def _router_logits(rwt_ref, h2, h2_hi):
    nt = (((1,), (1,)), ((), ()))
    h2_lo = (h2 - h2_hi.astype(F32)).astype(BF16)
    return (lax.dot_general(rwt_ref[0], h2_hi, nt, preferred_element_type=F32)
            + lax.dot_general(rwt_ref[0], h2_lo, nt, preferred_element_type=F32)
            + lax.dot_general(rwt_ref[1], h2_hi, nt, preferred_element_type=F32))


def _dup_halves(a):
    lane = lax.broadcasted_iota(jnp.int32, a.shape, 1)
    r = pltpu.roll(a, HEAD_DIM, axis=1)
    lo = lane < HEAD_DIM
    return jnp.where(lo, a, r), jnp.where(lo, r, a)


def _pool_means(pext_ref, n):
    outs = []
    for gi, w in enumerate(POOL_WINDOWS):
        xg = pext_ref[:, gi * POOL_GROUP_DIM:(gi + 1) * POOL_GROUP_DIM]
        s = xg
        sh = 1
        while sh < w:
            s = s + pltpu.roll(s, sh, axis=0)
            sh *= 2
        outs.append(s[16:] * (1.0 / w) - xg[16:])
    return outs


def _pack_exact_bf16_pairs(h):
    m = h.shape[1] // 2
    return (lax.shift_right_logical(pltpu.bitcast(h[:, :m], jnp.uint32), jnp.uint32(16))
            | (pltpu.bitcast(h[:, m:], jnp.uint32) & jnp.uint32(0xFFFF0000)))


def _unpack_bf16_pairs(w):
    lo = pltpu.bitcast(lax.shift_left(w, jnp.uint32(16)), F32).astype(BF16)
    hi = pltpu.bitcast(w & jnp.uint32(0xFFFF0000), F32).astype(BF16)
    return lo, hi


def _prompt_kernel(x_ref, meta_ref, gattn_ref, win_ref, wpool_ref, pscale_ref, wout_ref, gffn_ref,
                   rwt_ref, rb_ref, sink_ref, tbl_ref, tail_h2_ref, tail_lgt_ref,
                   x1_ref, h2_ref, lgt_ref, kmeta_ref, vmeta_ref, ktail_ref, vtail_ref, ptail_ref,
                   k2buf, v2buf, km2, vm2, qbuf, obuf, pext):
    pid = pl.program_id(0)
    n_main = pl.num_programs(0) - 1
    refs = (x_ref, meta_ref, gattn_ref, win_ref, wpool_ref, pscale_ref, wout_ref, gffn_ref,
            rwt_ref, rb_ref, sink_ref, tbl_ref,
            x1_ref, h2_ref, lgt_ref, kmeta_ref, vmeta_ref, ktail_ref, vtail_ref, ptail_ref,
            k2buf, v2buf, km2, vm2, qbuf, obuf, pext)

    @pl.when(pid < n_main)
    def _():
        _prompt_block(*refs)

    @pl.when(pid == n_main)
    def _():
        h2_ref[0:tail_h2_ref.shape[0], :] = tail_h2_ref[...]
        lgt_ref[:, 0:tail_lgt_ref.shape[1]] = tail_lgt_ref[...]


def _prompt_block(x_ref, meta_ref, gattn_ref, win_ref, wpool_ref, pscale_ref, wout_ref, gffn_ref,
                  rwt_ref, rb_ref, sink_ref, tbl_ref,
                  x1_ref, h2_ref, lgt_ref, kmeta_ref, vmeta_ref, ktail_ref, vtail_ref, ptail_ref,
                  k2buf, v2buf, km2, vm2, qbuf, obuf, pext):
    tb = x_ref.shape[0]
    pid = pl.program_id(0)

    @pl.when(pid == 0)
    def _():
        hm = _rms(meta_ref[...], gattn_ref[...]).astype(BF16)
        km = jnp.dot(hm, win_ref[:, ATTN_WIDTH:ATTN_WIDTH + KV_WIDTH], preferred_element_type=F32)
        vm = jnp.dot(hm, win_ref[:, ATTN_WIDTH + KV_WIDTH:ATTN_WIDTH + 2 * KV_WIDTH],
                     preferred_element_type=F32)
        pm = jnp.dot(hm, win_ref[:, ATTN_WIDTH + 2 * KV_WIDTH:], preferred_element_type=F32)
        kmeta_ref[...] = km
        vmeta_ref[...] = vm
        zpad = jnp.zeros((META_PAD - N_META, LANES), F32)
        k0, k1 = _dup_halves(jnp.concatenate([km, zpad], axis=0))
        v0, v1 = _dup_halves(jnp.concatenate([vm, zpad], axis=0))
        km2[0] = k0.astype(BF16)
        km2[1] = k1.astype(BF16)
        vm2[0, :, 0:LANES] = v0.astype(BF16)
        vm2[1, :, 0:LANES] = v1.astype(BF16)
        vm2[:, :, LANES:] = jnp.ones((N_KV_HEADS, META_PAD, LANES), BF16)
        k2buf[:, 0:WINDOW, :] = jnp.zeros((2, WINDOW, LANES), BF16)
        v2buf[:, 0:WINDOW, 0:LANES] = jnp.zeros((2, WINDOW, LANES), BF16)
        v2buf[:, :, LANES:] = jnp.ones((N_KV_HEADS, WINDOW + tb, LANES), BF16)
        pext[0:16, :] = pm

    h = _rms(x_ref[...], gattn_ref[...]).astype(BF16)
    q = jnp.dot(h, win_ref[:, 0:ATTN_WIDTH], preferred_element_type=F32) * (HEAD_DIM ** -0.5)
    lane_t = lax.broadcasted_iota(jnp.int32, (tb, LANES), 1)
    for c in range(N_HEADS // 2):
        tile = q[:, c * LANES:(c + 1) * LANES]
        for a in range(2):
            keep = (lane_t < HEAD_DIM) if a == 0 else (lane_t >= HEAD_DIM)
            piece = jnp.where(keep, tile, 0.0).astype(BF16).reshape(tb // QSUB, QSUB, LANES)
            row = ((c % 2) * 2 + a) * QSUB
            qbuf[c // 2, :, row:row + QSUB, :] = piece
    k = jnp.dot(h, win_ref[:, ATTN_WIDTH:ATTN_WIDTH + KV_WIDTH], preferred_element_type=F32)
    v = jnp.dot(h, win_ref[:, ATTN_WIDTH + KV_WIDTH:ATTN_WIDTH + 2 * KV_WIDTH], preferred_element_type=F32)
    p = jnp.dot(h, win_ref[:, ATTN_WIDTH + 2 * KV_WIDTH:], preferred_element_type=F32)
    ktail_ref[...] = k[tb - WINDOW:]
    vtail_ref[...] = v[tb - WINDOW:]
    ptail_ref[...] = p[tb - 16:]
    k0, k1 = _dup_halves(k)
    v0, v1 = _dup_halves(v)
    k2buf[0, WINDOW:, :] = k0.astype(BF16)
    k2buf[1, WINDOW:, :] = k1.astype(BF16)
    v2buf[0, WINDOW:, 0:LANES] = v0.astype(BF16)
    v2buf[1, WINDOW:, 0:LANES] = v1.astype(BF16)
    pext[16:, :] = p

    lane_q = lax.broadcasted_iota(jnp.int32, (QSUB, LANES), 1)
    lo_q = lane_q < HEAD_DIM

    for u in range(tb // QSUB):
        r0 = u * QSUB
        sel = jnp.where(pid == 0, u + 1, 0) if u < WINDOW // QSUB else 0
        for g in range(N_KV_HEADS):
            qm = qbuf[g, u]
            kwin = jnp.concatenate([km2[g], k2buf[g, r0:r0 + KEYS_SUB, :]], axis=0)
            vwin = jnp.concatenate([vm2[g], v2buf[g, r0:r0 + KEYS_SUB, :]], axis=0)
            s = lax.dot_general(qm, kwin, (((1,), (1,)), ((), ())), preferred_element_type=F32)
            s = s + tbl_ref[sel, g]
            sink = sink_ref[g]
            m = jnp.maximum(jnp.max(s, axis=1, keepdims=True), sink)
            e = jnp.exp(s - m).astype(BF16)
            r = jnp.dot(e, vwin, preferred_element_type=F32)
            o = r[:, 0:LANES] / (r[:, LANES:] + jnp.exp(sink - m))
            o0 = jnp.where(lo_q, o[0:QSUB], o[QSUB:2 * QSUB])
            o1 = jnp.where(lo_q, o[2 * QSUB:3 * QSUB], o[3 * QSUB:])
            obuf[r0:r0 + QSUB, (2 * g) * LANES:(2 * g + 1) * LANES] = o0.astype(BF16)
            obuf[r0:r0 + QSUB, (2 * g + 1) * LANES:(2 * g + 2) * LANES] = o1.astype(BF16)

    pooled = _pool_means(pext, tb)
    for gi in range(len(POOL_WINDOWS)):
        y = jnp.dot(pooled[gi].astype(BF16), wpool_ref[gi], preferred_element_type=F32)
        y = y * pscale_ref[:, gi * POOL_GROUP_DIM:(gi + 1) * POOL_GROUP_DIM]
        obuf[:, ATTN_WIDTH + gi * POOL_GROUP_DIM:ATTN_WIDTH + (gi + 1) * POOL_GROUP_DIM] = y.astype(BF16)

    k2buf[:, 0:WINDOW, :] = k2buf[:, tb:tb + WINDOW, :]
    v2buf[:, 0:WINDOW, 0:LANES] = v2buf[:, tb:tb + WINDOW, 0:LANES]
    pext[0:16, :] = pext[tb:tb + 16, :]

    x1 = x_ref[...] + jnp.dot(obuf[...], wout_ref[...], preferred_element_type=F32)
    x1_ref[...] = x1
    h2 = _rms(x1, gffn_ref[...])
    h2_hi = h2.astype(BF16)
    h2_ref[...] = h2_hi
    lgt_ref[...] = _router_logits(rwt_ref, h2, h2_hi) + rb_ref[...]


def _attn_tables(sinks):
    i = np.arange(QSUB)[:, None]
    j = np.arange(NKEY)[None, :]
    jb = j - META_PAD
    rel = i + WINDOW - jb
    band_ok = (jb >= 0) & (rel >= 0) & (rel <= WINDOW)
    meta_ok = (j < N_META) & (i >= 0)
    slopes = np.exp2(-8.0 * np.arange(1, N_HEADS + 1) / N_HEADS)
    tbl = np.empty((3, N_KV_HEADS, GQA_GROUP * QSUB, NKEY), np.float32)
    for var in range(3):
        ok = band_ok if var == 0 else band_ok & (jb >= WINDOW - (var - 1) * QSUB)
        for g in range(N_KV_HEADS):
            for a in range(GQA_GROUP):
                hd = g * GQA_GROUP + a
                bias = np.where(ok, -slopes[hd] * rel, MASKED)
                bias = np.where(meta_ok, 0.0, bias)
                tbl[var, g, a * QSUB:(a + 1) * QSUB] = bias
    sink_col = jnp.repeat(sinks.astype(F32).reshape(N_KV_HEADS, GQA_GROUP, 1), QSUB, axis=2)
    return jnp.asarray(tbl), sink_col.reshape(N_KV_HEADS, GQA_GROUP * QSUB, 1)


def _prompt_mixer(x, meta, gattn, win, wpool, pscale, wout, gffn, rwt, rb, sinks, tail_h2, tail_lgt):
    seq = x.shape[0]
    tb = PROMPT_BLOCK
    n_tail = tail_h2.shape[0]
    assert seq % tb == 0 and tb % WINDOW == 0 and n_tail <= tb
    nblk = seq // tb
    n_tok = seq + n_tail
    tbl, sink_col = _attn_tables(sinks)
    full = lambda *shape: pl.BlockSpec(shape, lambda i: (0,) * len(shape))
    main = lambda i: (jnp.minimum(i, nblk - 1), 0)
    in_width = win.shape[1]
    return pl.pallas_call(
        _prompt_kernel,
        grid=(nblk + 1,),
        in_specs=[
            pl.BlockSpec((tb, D_MODEL), main),
            full(N_META, D_MODEL), full(1, D_MODEL), full(D_MODEL, in_width),
            full(len(POOL_WINDOWS), POOL_GROUP_DIM, POOL_GROUP_DIM), full(1, POOL_WIDTH),
            full(D_MODEL, D_MODEL), full(1, D_MODEL), full(2, N_EXPERTS, D_MODEL), full(N_EXPERTS, 1),
            full(N_KV_HEADS, GQA_GROUP * QSUB, 1), full(3, N_KV_HEADS, GQA_GROUP * QSUB, NKEY),
            full(n_tail, D_MODEL), full(N_EXPERTS, n_tail),
        ],
        out_specs=[
            pl.BlockSpec((tb, D_MODEL), main),
            pl.BlockSpec((tb, D_MODEL), lambda i: (i, 0)),
            pl.BlockSpec((N_EXPERTS, tb), lambda i: (0, i)),
            full(N_META, KV_WIDTH), full(N_META, KV_WIDTH),
            full(WINDOW, KV_WIDTH), full(WINDOW, KV_WIDTH), full(16, POOL_WIDTH),
        ],
        out_shape=[
            jax.ShapeDtypeStruct((seq, D_MODEL), F32),
            jax.ShapeDtypeStruct((n_tok, D_MODEL), BF16),
            jax.ShapeDtypeStruct((N_EXPERTS, n_tok), F32),
            jax.ShapeDtypeStruct((N_META, KV_WIDTH), F32),
            jax.ShapeDtypeStruct((N_META, KV_WIDTH), F32),
            jax.ShapeDtypeStruct((WINDOW, KV_WIDTH), F32),
            jax.ShapeDtypeStruct((WINDOW, KV_WIDTH), F32),
            jax.ShapeDtypeStruct((16, POOL_WIDTH), F32),
        ],
        scratch_shapes=[
            pltpu.VMEM((N_KV_HEADS, WINDOW + tb, LANES), BF16),
            pltpu.VMEM((N_KV_HEADS, WINDOW + tb, 2 * LANES), BF16),
            pltpu.VMEM((N_KV_HEADS, META_PAD, LANES), BF16),
            pltpu.VMEM((N_KV_HEADS, META_PAD, 2 * LANES), BF16),
            pltpu.VMEM((N_KV_HEADS, tb // QSUB, GQA_GROUP * QSUB, LANES), BF16),
            pltpu.VMEM((tb, D_MODEL), BF16),
            pltpu.VMEM((16 + tb, POOL_WIDTH), F32),
        ],
        compiler_params=pltpu.CompilerParams(dimension_semantics=("arbitrary",),
                                             vmem_limit_bytes=VMEM_LIMIT),
        name="prompt_mixer",
    )(x, meta, gattn, win, wpool, pscale, wout, gffn, rwt, rb, sink_col, tbl, tail_h2, tail_lgt)


def _sample_kernel(x_ref, ck_ref, cv_ref, sp_ref, gattn_ref, win_ref, wqkv_t_ref, wpool_ref, pscale_ref,
                   wout_ref, gffn_ref, rwt_ref, rb_ref, sink_ref, bias_ref,
                   x1_ref, h2_ref, lgt_ref, kout_ref, vout_ref, pnew_ref,
                   qt_ref, sc_ref, ot_ref, obuf):
    nb = x_ref.shape[0]
    rows = ck_ref.shape[0]
    x = x_ref[...]
    h = _rms(x, gattn_ref[...]).astype(BF16)
    nt = (((1,), (1,)), ((), ()))
    qkv_t = lax.dot_general(wqkv_t_ref[...], h, nt, preferred_element_type=F32)
    qt_ref[...] = qkv_t[0:ATTN_WIDTH] * (HEAD_DIM ** -0.5)
    kt = qkv_t[ATTN_WIDTH:ATTN_WIDTH + KV_WIDTH]
    vt = qkv_t[ATTN_WIDTH + KV_WIDTH:]
    p = jnp.dot(h, win_ref[:, ATTN_WIDTH + 2 * KV_WIDTH:], preferred_element_type=F32)
    pnew_ref[...] = p

    kout_ref[0:N_META] = ck_ref[0:N_META]
    vout_ref[0:N_META] = cv_ref[0:N_META]
    kout_ref[N_META:rows - 1] = ck_ref[N_META + 1:rows]
    vout_ref[N_META:rows - 1] = cv_ref[N_META + 1:rows]
    kout_ref[rows - 1] = kt
    vout_ref[rows - 1] = vt

    def kv_rows(hd):
        g = hd // GQA_GROUP
        return slice(g * HEAD_DIM, (g + 1) * HEAD_DIM)

    def score_row(key_tile, hd):
        prod = qt_ref[hd * HEAD_DIM:(hd + 1) * HEAD_DIM, :] * key_tile[kv_rows(hd), :]
        return jnp.sum(prod, axis=0, keepdims=True)

    def score_pass(s, carry):
        key_tile = ck_ref[s]
        for hd in range(N_HEADS):
            sc_ref[hd, pl.ds(s, 1), :] = score_row(key_tile, hd)
        return carry

    lax.fori_loop(0, rows, score_pass, 0)
    pad_rows = sc_ref.shape[1] - rows - 1
    for hd in range(N_HEADS):
        sc_ref[hd, rows:rows + 1, :] = score_row(kt, hd)
        sc_ref[hd, rows + 1:, :] = jnp.full((pad_rows, nb), MASKED, F32)

    for hd in range(N_HEADS):
        s = sc_ref[hd] + bias_ref[hd]
        sink = sink_ref[hd]
        m = jnp.maximum(jnp.max(s, axis=0, keepdims=True), sink)
        e = jnp.exp(s - m)
        sc_ref[hd] = e * (1.0 / (jnp.sum(e, axis=0, keepdims=True) + jnp.exp(sink - m)))

    for half in range(2):
        heads = range(half * (N_HEADS // 2), (half + 1) * (N_HEADS // 2))

        def value_pass(s, accs, heads=heads):
            val_tile = cv_ref[s]
            return tuple(acc + sc_ref[hd, pl.ds(s, 1), :] * val_tile[kv_rows(hd), :]
                         for acc, hd in zip(accs, heads))

        accs = lax.fori_loop(0, rows, value_pass,
                             tuple(jnp.zeros((HEAD_DIM, nb), F32) for _ in heads))
        for acc, hd in zip(accs, heads):
            ot_ref[hd * HEAD_DIM:(hd + 1) * HEAD_DIM, :] = acc + sc_ref[hd, rows:rows + 1, :] * vt[kv_rows(hd), :]

    for c in range(ATTN_WIDTH // LANES):
        obuf[:, c * LANES:(c + 1) * LANES] = ot_ref[c * LANES:(c + 1) * LANES, :].T.astype(BF16)

    for gi, w in enumerate(POOL_WINDOWS):
        cols = slice(gi * POOL_GROUP_DIM, (gi + 1) * POOL_GROUP_DIM)
        pg = p[:, cols]
        acc = pg
        for d in range(1, w):
            acc = acc + sp_ref[:, POOL_STATE - d, cols]
        pooled = acc * (1.0 / w) - pg
        y = jnp.dot(pooled.astype(BF16), wpool_ref[gi], preferred_element_type=F32) * pscale_ref[:, cols]
        obuf[:, ATTN_WIDTH + gi * POOL_GROUP_DIM:ATTN_WIDTH + (gi + 1) * POOL_GROUP_DIM] = y.astype(BF16)

    x1 = x + jnp.dot(obuf[...], wout_ref[...], preferred_element_type=F32)
    x1_ref[...] = x1
    h2 = _rms(x1, gffn_ref[...])
    h2_hi = h2.astype(BF16)
    h2_ref[...] = h2_hi
    lgt_ref[...] = _router_logits(rwt_ref, h2, h2_hi) + rb_ref[...]


def _sample_mixer(x, ck_t, cv_t, sp, gattn, win, wpool, pscale, wout, gffn, rwt, rb, sinks):
    nb = x.shape[0]
    rows = ck_t.shape[0]
    assert nb == LANES and rows == N_META + WINDOW
    n_keys = -(-(rows + 1) // 8) * 8
    slopes = np.exp2(-8.0 * np.arange(1, N_HEADS + 1) / N_HEADS)
    dist = np.concatenate([np.zeros(N_META), WINDOW - np.arange(WINDOW), np.zeros(1)])
    bias = np.full((N_HEADS, n_keys, 1), MASKED, np.float32)
    bias[:, :rows + 1, 0] = -slopes[:, None] * dist[None, :]
    vm = pl.BlockSpec(memory_space=pltpu.VMEM)
    return pl.pallas_call(
        _sample_kernel,
        in_specs=[vm] * 15,
        out_specs=[vm] * 6,
        out_shape=[
            jax.ShapeDtypeStruct((nb, D_MODEL), F32),
            jax.ShapeDtypeStruct((nb, D_MODEL), BF16),
            jax.ShapeDtypeStruct((N_EXPERTS, nb), F32),
            jax.ShapeDtypeStruct(ck_t.shape, F32),
            jax.ShapeDtypeStruct(cv_t.shape, F32),
            jax.ShapeDtypeStruct((nb, POOL_WIDTH), F32),
        ],
        scratch_shapes=[
            pltpu.VMEM((ATTN_WIDTH, nb), F32),
            pltpu.VMEM((N_HEADS, n_keys, nb), F32),
            pltpu.VMEM((ATTN_WIDTH, nb), F32),
            pltpu.VMEM((nb, D_MODEL), BF16),
        ],
        compiler_params=pltpu.CompilerParams(vmem_limit_bytes=VMEM_LIMIT),
        name="sample_mixer",
    )(x, ck_t, cv_t, sp, gattn, win, win[:, 0:ATTN_WIDTH + 2 * KV_WIDTH].T, wpool, pscale, wout, gffn, rwt, rb,
      sinks.astype(F32).reshape(N_HEADS, 1, 1), jnp.asarray(bias))


def _router_kernel(lg_ref, tri_ref, low_ref, gate_ref, col_ref, lrank_ref, lpos_ref,
                   tcar_ref, cnt_ref, carry):
    tr = lg_ref.shape[1]

    @pl.when(pl.program_id(0) == 0)
    def _():
        carry[...] = jnp.zeros_like(carry)

    work = lg_ref[...]
    eio = lax.broadcasted_iota(jnp.int32, work.shape, 0).astype(F32)
    sels, vals, idxs = [], [], []
    for _k in range(TOP_K):
        mx = jnp.max(work, axis=0, keepdims=True)
        idx = jnp.min(jnp.where(work == mx, eio, float(N_EXPERTS)), axis=0, keepdims=True)
        sel = eio == idx
        sels.append(sel)
        vals.append(mx)
        idxs.append(idx)
        work = jnp.where(sel, -jnp.inf, work)
    exps = [jnp.exp(vk - vals[0]) for vk in vals]
    tot = exps[0] + exps[1] + exps[2] + exps[3]
    onehot = jnp.zeros(work.shape, F32)
    for sel in sels:
        onehot = onehot + sel.astype(F32)
    before = jnp.dot(onehot.astype(BF16), tri_ref[...], preferred_element_type=F32) + carry[...]
    for kk in range(TOP_K):
        gate_ref[pl.ds(kk, 1), :] = exps[kk] / tot
    for j in range(tr // COMBINE_TILE):
        cols = slice(j * COMBINE_TILE, (j + 1) * COMBINE_TILE)
        tc = before[:, j * COMBINE_TILE:j * COMBINE_TILE + 1]
        tcar_ref[j] = tc.astype(jnp.int32)
        slack = tc - WIN_ALIGN * jnp.floor(tc * (1.0 / WIN_ALIGN))
        local = before[:, cols] - tc
        for kk in range(TOP_K):
            selk = sels[kk][:, cols]
            lr = jnp.sum(jnp.where(selk, local, 0.0), axis=0, keepdims=True)
            sl = jnp.sum(jnp.where(selk, slack, 0.0), axis=0, keepdims=True)
            lrank_ref[pl.ds(kk, 1), cols] = lr.astype(jnp.int32)
            col_ref[pl.ds(kk, 1), cols] = (idxs[kk][:, cols] * float(WIN_ROWS) + sl + lr).astype(jnp.int32)
    for j in range(tr // DISPATCH_TILE):
        cols = slice(j * DISPATCH_TILE, (j + 1) * DISPATCH_TILE)
        local = before[:, cols] - before[:, j * DISPATCH_TILE:j * DISPATCH_TILE + 1]
        tile_cnt = jnp.broadcast_to(jnp.sum(onehot[:, cols], axis=1, keepdims=True), local.shape)
        cnt_hi = jnp.floor(tile_cnt * (1.0 / 256.0))
        cnt_lo = tile_cnt - 256.0 * cnt_hi
        run_start = (256.0 * jnp.dot(low_ref[...], cnt_hi.astype(BF16), preferred_element_type=F32)
                     + jnp.dot(low_ref[...], cnt_lo.astype(BF16), preferred_element_type=F32))
        for kk in range(TOP_K):
            lp = jnp.sum(jnp.where(sels[kk][:, cols], run_start + local, 0.0), axis=0, keepdims=True)
            lpos_ref[pl.ds(kk, 1), cols] = lp.astype(jnp.int32)
    carry[...] = carry[...] + jnp.sum(onehot, axis=1, keepdims=True)
    cnt_ref[...] = carry[...].astype(jnp.int32)


def _router(logits_t):
    n = logits_t.shape[1]
    tr = ROUTE_BLOCK
    assert n % tr == 0
    tri = jnp.asarray(np.triu(np.ones((tr, tr), np.float32), k=1), BF16)
    low = jnp.asarray(np.tril(np.ones((N_EXPERTS, N_EXPERTS), np.float32), k=-1), BF16)
    per_tok = pl.BlockSpec((TOP_K, tr), lambda i: (0, i))
    return pl.pallas_call(
        _router_kernel,
        grid=(n // tr,),
        in_specs=[pl.BlockSpec((N_EXPERTS, tr), lambda i: (0, i)),
                  pl.BlockSpec((tr, tr), lambda i: (0, 0)),
                  pl.BlockSpec((N_EXPERTS, N_EXPERTS), lambda i: (0, 0))],
        out_specs=[per_tok, per_tok, per_tok, per_tok,
                   pl.BlockSpec((tr // COMBINE_TILE, N_EXPERTS, 1), lambda i: (i, 0, 0)),
                   pl.BlockSpec((N_EXPERTS, 1), lambda i: (0, 0))],
        out_shape=[jax.ShapeDtypeStruct((TOP_K, n), F32),
                   jax.ShapeDtypeStruct((TOP_K, n), jnp.int32),
                   jax.ShapeDtypeStruct((TOP_K, n), jnp.int32),
                   jax.ShapeDtypeStruct((TOP_K, n), jnp.int32),
                   jax.ShapeDtypeStruct((n // COMBINE_TILE, N_EXPERTS, 1), jnp.int32),
                   jax.ShapeDtypeStruct((N_EXPERTS, 1), jnp.int32)],
        scratch_shapes=[pltpu.VMEM((N_EXPERTS, 1), F32)],
        compiler_params=pltpu.CompilerParams(dimension_semantics=("arbitrary",)),
        name="router",
    )(logits_t, tri, low)


def _dispatch_kernel(lstart_ref, cnt_ref, dst_ref, h2_ref, lpos_ref, xs_hbm, stg0, stg1, sem):
    i = pl.program_id(0)
    n_tiles = pl.num_programs(0)
    dt = h2_ref.shape[0]
    rows = dt * TOP_K
    slot = i % 2
    stgs = (stg0, stg1)
    pieces = [p for p in (256, 128, 64, 32, 16, 8, 4, 2, 1) if p <= dt]
    assert dt < 512

    def drain(s):
        pltpu.make_async_copy(stgs[s], xs_hbm.at[pl.ds(0, rows)], sem.at[s]).wait()

    def issue_runs(tile, live, s):
        for e in range(N_EXPERTS):
            n = jnp.where(live, cnt_ref[tile * N_EXPERTS + e], 0)
            src0 = lstart_ref[tile * N_EXPERTS + e]
            dst0 = dst_ref[tile * N_EXPERTS + e]
            for piece in pieces:
                off = n & ~jnp.int32(2 * piece - 1)

                @pl.when((n & piece) != 0)
                def _(off=off, piece=piece, src0=src0, dst0=dst0):
                    pltpu.make_async_copy(stgs[s].at[pl.ds(src0 + off, piece)],
                                          xs_hbm.at[pl.ds(dst0 + off, piece)], sem.at[s]).start()

    def sort_tile(s):
        rid = lax.broadcasted_iota(jnp.int32, (rows, dt), 0)
        hit = rid == lpos_ref[0:1, :]
        for kk in range(1, TOP_K):
            hit = jnp.logical_or(hit, rid == lpos_ref[kk:kk + 1, :])
        perm = jnp.where(hit, 1.0, 0.0).astype(BF16)
        srt = jnp.dot(perm, h2_ref[...], preferred_element_type=F32)
        packed = _pack_exact_bf16_pairs(srt)
        for c in range(PACK_CHUNKS):
            stgs[s][:, c, :] = packed[:, c * LANES:(c + 1) * LANES]

    for s in range(2):
        @pl.when(slot == s)
        def _(s=s):
            @pl.when(i >= 2)
            def _():
                drain(s)

            issue_runs(jnp.maximum(i - 1, 0), i >= 1, 1 - s)
            sort_tile(s)

            @pl.when(i == n_tiles - 1)
            def _():
                issue_runs(i, True, s)
                drain(s)

                @pl.when(n_tiles >= 2)
                def _():
                    drain(1 - s)


def _dispatch(h2, lpos, lstart, cnt, dst, cap):
    n_tok = h2.shape[0]
    dt = DISPATCH_TILE
    assert n_tok % dt == 0
    grid_spec = pltpu.PrefetchScalarGridSpec(
        num_scalar_prefetch=3,
        grid=(n_tok // dt,),
        in_specs=[pl.BlockSpec((dt, D_MODEL), lambda i, a, b, c: (i, 0)),
                  pl.BlockSpec((TOP_K, dt), lambda i, a, b, c: (0, i)),
                  ],
        out_specs=pl.BlockSpec(memory_space=pl.ANY),
        scratch_shapes=[pltpu.VMEM((dt * TOP_K, PACK_CHUNKS, LANES), jnp.uint32),
                        pltpu.VMEM((dt * TOP_K, PACK_CHUNKS, LANES), jnp.uint32),
                        pltpu.SemaphoreType.DMA((2,))],
    )
    return pl.pallas_call(
        _dispatch_kernel,
        grid_spec=grid_spec,
        out_shape=jax.ShapeDtypeStruct((cap, PACK_CHUNKS, LANES), jnp.uint32),
        compiler_params=pltpu.CompilerParams(dimension_semantics=("arbitrary",),
                                             vmem_limit_bytes=VMEM_LIMIT),
        name="dispatch",
    )(lstart, cnt, dst, h2, lpos)


def _expert_kernel(n_xblocks, bexp_ref, nvalid_ref, epos_ref, elist_ref,
                   x_hbm, wgu_hbm, bgu_ref, wd_hbm, bd_ref, y_ref,
                   wgu_f32, wd_f32, wgu_bf, wd_bf, xbuf, xsem, wsem):
    i = pl.program_id(0)
    rb = y_ref.shape[0]
    nvalid = nvalid_ref[i]
    pos = epos_ref[i]
    fresh = jnp.logical_or(i == 0, pos != epos_ref[jnp.maximum(i - 1, 0)])
    slot = i % 2

    def x_copies(blk, s):
        return [pltpu.make_async_copy(x_hbm.at[pl.ds(blk * rb, rb), c, :],
                                      xbuf.at[s, :, pl.ds(c * LANES, LANES)], xsem.at[s])
                for c in range(PACK_CHUNKS)]

    def w_copies(p):
        e = elist_ref[p]
        s = p % 2
        return [pltpu.make_async_copy(wgu_hbm.at[e], wgu_f32.at[s], wsem.at[s, 0]),
                pltpu.make_async_copy(wd_hbm.at[e], wd_f32.at[s], wsem.at[s, 1])]

    @pl.when(i == 0)
    def _():
        for cp in x_copies(0, 0):
            cp.start()

        @pl.when(nvalid > 0)
        def _():
            for cp in w_copies(0):
                cp.start()

    @pl.when(i + 1 < n_xblocks)
    def _():
        for cp in x_copies(i + 1, 1 - slot):
            cp.start()

    @pl.when(jnp.logical_and(fresh, nvalid > 0))
    def _():
        @pl.when(elist_ref[pos + 1] >= 0)
        def _():
            for cp in w_copies(pos + 1):
                cp.start()

        for cp in w_copies(pos):
            cp.wait()
        ws = pos % 2
        chunk = 32

        def cast_gu(r, c):
            r0 = pl.multiple_of(r * chunk, chunk)
            wgu_bf[pl.ds(r0, chunk), :] = wgu_f32[ws, pl.ds(r0, chunk), :].astype(BF16)
            return c

        def cast_d(r, c):
            r0 = pl.multiple_of(r * chunk, chunk)
            wd_bf[pl.ds(r0, chunk), :] = wd_f32[ws, pl.ds(r0, chunk), :].astype(BF16)
            return c

        lax.fori_loop(0, D_MODEL // chunk, cast_gu, 0)
        lax.fori_loop(0, D_EXPERT // chunk, cast_d, 0)

    @pl.when(i < n_xblocks)
    def _():
        for cp in x_copies(i, slot):
            cp.wait()

    def ffn(rows):
        xw = xbuf[slot, 0:rows, :]
        xw = jnp.where(lax.broadcasted_iota(jnp.int32, xw.shape, 0) < nvalid, xw, jnp.uint32(0))
        xlo, xhi = _unpack_bf16_pairs(xw)
        half = D_MODEL // 2
        g = (jnp.dot(xlo, wgu_bf[0:half, 0:D_EXPERT], preferred_element_type=F32)
             + jnp.dot(xhi, wgu_bf[half:, 0:D_EXPERT], preferred_element_type=F32)
             + bgu_ref[0, :, 0:D_EXPERT])
        u = (jnp.dot(xlo, wgu_bf[0:half, D_EXPERT:], preferred_element_type=F32)
             + jnp.dot(xhi, wgu_bf[half:, D_EXPERT:], preferred_element_type=F32)
             + bgu_ref[0, :, D_EXPERT:])
        g = jnp.minimum(g, SWIGLU_LIMIT)
        u = jnp.clip(u, -SWIGLU_LIMIT, SWIGLU_LIMIT)
        act = g * (1.0 / (1.0 + jnp.exp(-SWIGLU_ALPHA * g))) * (u + 1.0)
        y = jnp.dot(act.astype(BF16), wd_bf[...], preferred_element_type=F32) + bd_ref[0]
        row = lax.broadcasted_iota(jnp.int32, y.shape, 0)
        y_ref[0:rows, :] = jnp.where(row < nvalid, y, 0.0).astype(BF16)
        if rows < rb:
            y_ref[rows:, :] = jnp.zeros((rb - rows, D_MODEL), BF16)

    @pl.when(nvalid > rb // 2)
    def _():
        ffn(rb)

    @pl.when(jnp.logical_and(nvalid > 0, nvalid <= rb // 2))
    def _():
        ffn(rb // 2)

    @pl.when(nvalid == 0)
    def _():
        y_ref[...] = jnp.zeros_like(y_ref)


def _experts(xs, block_expert, nvalid, block_pos, expert_list, wgu, bgu, wd, bd):
    rb = EXPERT_ROWS
    n_xblocks = xs.shape[0] // rb
    nblk = n_xblocks + 1
    any_space = pl.BlockSpec(memory_space=pl.ANY)
    grid_spec = pltpu.PrefetchScalarGridSpec(
        num_scalar_prefetch=4,
        grid=(nblk,),
        in_specs=[
            any_space,
            any_space,
            pl.BlockSpec((1, 1, 2 * D_EXPERT), lambda i, be, nu, ep, el: (be[i], 0, 0)),
            any_space,
            pl.BlockSpec((1, 1, D_MODEL), lambda i, be, nu, ep, el: (be[i], 0, 0)),
        ],
        out_specs=pl.BlockSpec((rb, D_MODEL), lambda i, be, nu, ep, el: (i, 0)),
        scratch_shapes=[pltpu.VMEM((2, D_MODEL, 2 * D_EXPERT), F32),
                        pltpu.VMEM((2, D_EXPERT, D_MODEL), F32),
                        pltpu.VMEM((D_MODEL, 2 * D_EXPERT), BF16),
                        pltpu.VMEM((D_EXPERT, D_MODEL), BF16),
                        pltpu.VMEM((2, rb, D_MODEL // 2), jnp.uint32),
                        pltpu.SemaphoreType.DMA((2,)),
                        pltpu.SemaphoreType.DMA((2, 2))],
    )
    return pl.pallas_call(
        functools.partial(_expert_kernel, n_xblocks),
        grid_spec=grid_spec,
        out_shape=jax.ShapeDtypeStruct((nblk * rb, D_MODEL), BF16),
        compiler_params=pltpu.CompilerParams(dimension_semantics=("arbitrary",),
                                             vmem_limit_bytes=VMEM_LIMIT),
        name="experts",
    )(block_expert, nvalid, block_pos, expert_list, xs, wgu, bgu.reshape(N_EXPERTS, 1, 2 * D_EXPERT), wd,
      bd.reshape(N_EXPERTS, 1, D_MODEL))


def _combine_kernel(n_prompt_tiles, offa_ref, nchunk_ref, tail_ref, ntail_ref,
                    x1p_ref, x1s_ref, info_ref, info_next_ref, gfin_ref, yb_hbm,
                    outp_ref, outs_ref, ybuf, gbuf0, gbuf1, acc_ref, sem, tsem):
    i = pl.program_id(0)
    n_tiles = pl.num_programs(0)

    def window_copy(tile, chunk, e, slot):
        base = pl.multiple_of(offa_ref[tile * N_EXPERTS + e] + chunk * WIN, WIN_ALIGN)
        return pltpu.make_async_copy(yb_hbm.at[pl.ds(base, WIN_ROWS), :],
                                     ybuf.at[slot, pl.ds(e * WIN_ROWS, WIN_ROWS), :],
                                     sem.at[slot])

    def start_windows(tile, chunk, slot):
        for e in range(N_EXPERTS):
            window_copy(tile, chunk, e, slot).start()

    def wait_windows(slot):
        pltpu.make_async_copy(yb_hbm.at[pl.ds(0, N_EXPERTS * WIN_ROWS), :], ybuf.at[slot],
                              sem.at[slot]).wait()

    def tail_copy(tile, e, slot):
        base = pl.multiple_of(offa_ref[tile * N_EXPERTS + e] + WIN_HEAD, WIN_ALIGN)
        return pltpu.make_async_copy(yb_hbm.at[pl.ds(base, WIN_ROWS - WIN_HEAD), :],
                                     ybuf.at[slot, pl.ds(e * WIN_ROWS + WIN_HEAD, WIN_ROWS - WIN_HEAD), :],
                                     tsem.at[slot])

    def start_first_chunk(tile, slot):
        for e in range(N_EXPERTS):
            base = pl.multiple_of(offa_ref[tile * N_EXPERTS + e], WIN_ALIGN)
            pltpu.make_async_copy(yb_hbm.at[pl.ds(base, WIN_HEAD), :],
                                  ybuf.at[slot, pl.ds(e * WIN_ROWS, WIN_HEAD), :], sem.at[slot]).start()

            @pl.when(tail_ref[tile * N_EXPERTS + e] != 0)
            def _(e=e):
                tail_copy(tile, e, slot).start()

    def wait_first_chunk(tile, slot):
        pltpu.make_async_copy(yb_hbm.at[pl.ds(0, N_EXPERTS * WIN_HEAD), :],
                              ybuf.at[slot, pl.ds(0, N_EXPERTS * WIN_HEAD), :], sem.at[slot]).wait()

        def one_tail(t, c):
            tail_copy(tile, 0, slot).wait()
            return c

        lax.fori_loop(0, ntail_ref[tile], one_tail, 0)

    slot = i % 2

    @pl.when(i == 0)
    def _():
        ybuf[...] = jnp.zeros_like(ybuf)
        start_first_chunk(0, 0)

    lane = lax.broadcasted_iota(jnp.int32, (COMBINE_TILE, N_EXPERTS * WIN_ROWS), 1)

    def gate_matrix(ref, chunk):
        g = jnp.zeros(lane.shape, F32)
        for kk in range(TOP_K):
            lr = ref[:, TOP_K + kk:TOP_K + kk + 1]
            in_chunk = jnp.logical_and(lr >= chunk * WIN, lr < chunk * WIN + WIN)
            colk = jnp.where(in_chunk, ref[:, kk:kk + 1] - chunk * WIN, -1.0).astype(jnp.int32)
            g = jnp.where(lane == colk, ref[:, 2 * TOP_K + kk:2 * TOP_K + kk + 1], g)
        return g.astype(BF16)

    def moe_rows(gm, buf):
        return jnp.dot(gm, ybuf[buf], preferred_element_type=F32)

    @pl.when(i == 0)
    def _():
        gbuf0[...] = gate_matrix(info_ref, 0)

    def main(s):
        start_first_chunk(jnp.minimum(i + 1, n_tiles - 1), 1 - s)
        wait_first_chunk(i, s)
        g_cur, g_nxt = (gbuf0, gbuf1) if s == 0 else (gbuf1, gbuf0)
        acc_ref[...] = moe_rows(g_cur[...], s)
        g_nxt[...] = gate_matrix(info_next_ref, 0)

    for s in range(2):
        @pl.when(slot == s)
        def _(s=s):
            main(s)

    @pl.when(i == n_tiles - 1)
    def _():
        for s in range(2):
            @pl.when(slot == s)
            def _(s=s):
                wait_first_chunk(i, 1 - s)

    def extra_chunk(j, c):
        start_windows(i, j, 2)
        wait_windows(2)
        acc_ref[...] += moe_rows(gate_matrix(info_ref, j), 2)
        return c

    lax.fori_loop(1, nchunk_ref[i], extra_chunk, 0)

    @pl.when(i < n_prompt_tiles)
    def _():
        outp_ref[...] = _rms(x1p_ref[...] + acc_ref[...], gfin_ref[...])

    @pl.when(i >= n_prompt_tiles)
    def _():
        outs_ref[...] = _rms(x1s_ref[...] + acc_ref[...], gfin_ref[...])


def _combine(x1_p, x1_s, tok_info, gfin, yb, offa, nchunk, need_tail, ntail):
    ct = COMBINE_TILE
    n_p, n_s = x1_p.shape[0] // ct, x1_s.shape[0] // ct
    assert x1_p.shape[0] % ct == 0 and x1_s.shape[0] % ct == 0 and n_s >= 1
    n_info = tok_info.shape[1]
    grid_spec = pltpu.PrefetchScalarGridSpec(
        num_scalar_prefetch=4,
        grid=(n_p + n_s,),
        in_specs=[
            pl.BlockSpec((ct, D_MODEL), lambda i, *_: (jnp.minimum(i, n_p - 1), 0)),
            pl.BlockSpec((ct, D_MODEL), lambda i, *_: (jnp.maximum(i - n_p, 0), 0)),
            pl.BlockSpec((ct, n_info), lambda i, *_: (i, 0)),
            pl.BlockSpec((ct, n_info), lambda i, *_: (jnp.minimum(i + 1, n_p + n_s - 1), 0)),
            pl.BlockSpec((1, D_MODEL), lambda i, *_: (0, 0)),
            pl.BlockSpec(memory_space=pl.ANY),
        ],
        out_specs=[
            pl.BlockSpec((ct, D_MODEL), lambda i, *_: (jnp.minimum(i, n_p - 1), 0)),
            pl.BlockSpec((ct, D_MODEL), lambda i, *_: (jnp.maximum(i - n_p, 0), 0)),
        ],
        scratch_shapes=[pltpu.VMEM((3, N_EXPERTS * WIN_ROWS, D_MODEL), BF16),
                        pltpu.VMEM((ct, N_EXPERTS * WIN_ROWS), BF16),
                        pltpu.VMEM((ct, N_EXPERTS * WIN_ROWS), BF16),
                        pltpu.VMEM((ct, D_MODEL), F32),
                        pltpu.SemaphoreType.DMA((3,)),
                        pltpu.SemaphoreType.DMA((2,))],
    )
    return pl.pallas_call(
        functools.partial(_combine_kernel, n_p),
        grid_spec=grid_spec,
        out_shape=[jax.ShapeDtypeStruct(x1_p.shape, F32), jax.ShapeDtypeStruct(x1_s.shape, F32)],
        compiler_params=pltpu.CompilerParams(dimension_semantics=("arbitrary",),
                                             vmem_limit_bytes=VMEM_LIMIT),
        name="combine",
    )(offa, nchunk, need_tail, ntail, x1_p, x1_s, tok_info, tok_info, gfin, yb)


def kernel(x_prompt, x_sample, cache_k, cache_v, state_pool, meta_tokens, norm_attn, w_in, attn_sinks,
           w_pool, pool_scale, w_out, norm_ffn, router_w, router_b, w_gate_up, b_gate_up, w_down, b_down,
           norm_final):
    assert w_in.shape[0] == 1, "single-layer trunk"
    bsz, seq, _ = x_prompt.shape
    assert bsz == 1
    nb = x_sample.shape[0]
    n_tok = seq + nb
    gattn = norm_attn[0].reshape(1, D_MODEL)
    gffn = norm_ffn[0].reshape(1, D_MODEL)
    win = w_in[0].astype(BF16)
    wpool = w_pool[0].astype(BF16)
    wout = w_out[0].astype(BF16)
    pscale = pool_scale[0].reshape(1, POOL_WIDTH)
    rw_t = router_w[0].T
    rw_hi = rw_t.astype(BF16)
    rwt = jnp.stack([rw_hi, (rw_t - rw_hi.astype(F32)).astype(BF16)])
    rb = router_b[0].reshape(N_EXPERTS, 1)
    sinks = attn_sinks[0]

    cache_rows = N_META + WINDOW
    to_batch_minor = lambda c: jnp.transpose(c[0], (1, 2, 3, 0)).reshape(cache_rows, KV_WIDTH, nb)
    from_batch_minor = lambda c: jnp.transpose(
        c.reshape(cache_rows, N_KV_HEADS, HEAD_DIM, nb), (3, 0, 1, 2))[None]
    (x1_s, h2_s, lgt_s, kout_t, vout_t, pnew) = _sample_mixer(
        x_sample[:, 0], to_batch_minor(cache_k), to_batch_minor(cache_v), state_pool[0],
        gattn, win, wpool, pscale, wout, gffn, rwt, rb, sinks)
    (x1_p, h2_all, lgt_all, kmeta, vmeta, ktail, vtail, ptail) = _prompt_mixer(
        x_prompt[0], meta_tokens, gattn, win, wpool, pscale, wout, gffn, rwt, rb, sinks, h2_s, lgt_s)

    gates, col, lrank, lpos, tcar, counts = _router(lgt_all)
    counts = counts[:, 0]
    tcar = tcar[:, :, 0]
    rbk = EXPERT_ROWS
    eids = jnp.arange(N_EXPERTS, dtype=jnp.int32)
    earlier = eids[None, :] < eids[:, None]
    excl_sum = lambda a: jnp.sum(jnp.where(earlier, a[..., None, :], 0), axis=-1)
    padded = (counts + rbk - 1) // rbk * rbk
    pad_start = excl_sum(padded).astype(jnp.int32)
    pad_end = pad_start + padded
    nblk = -(-(n_tok * TOP_K) // rbk) + N_EXPERTS
    cap = nblk * rbk
    block_start = jnp.arange(nblk + 1, dtype=jnp.int32) * rbk
    owns = (pad_start[None, :] <= block_start[:, None]) & (block_start[:, None] < pad_end[None, :])
    nvalid = jnp.sum(jnp.where(owns, jnp.clip(counts[None, :] - (block_start[:, None] - pad_start[None, :]),
                                              0, rbk), 0), axis=1).astype(jnp.int32)
    has_rows = counts > 0
    last_e = jnp.max(jnp.where(has_rows, eids, 0))
    block_expert = jnp.where(jnp.any(owns, axis=1), jnp.sum(jnp.where(owns, eids[None, :], 0), axis=1),
                             last_e).astype(jnp.int32)

    run_len = jnp.concatenate([tcar[1:], counts[None, :]], axis=0) - tcar
    dcar = tcar[::DISPATCH_TILE // COMBINE_TILE]
    drun_len = jnp.concatenate([dcar[1:], counts[None, :]], axis=0) - dcar
    flat = lambda a: a.astype(jnp.int32).reshape(-1)
    xs = _dispatch(h2_all, lpos, flat(excl_sum(drun_len)), flat(drun_len), flat(pad_start[None, :] + dcar),
                   cap)
    expert_pos = excl_sum(has_rows.astype(jnp.int32))
    at_pos = has_rows[None, :] & (expert_pos[None, :] == jnp.arange(N_EXPERTS + 1, dtype=jnp.int32)[:, None])
    expert_list = jnp.where(jnp.any(at_pos, axis=1), jnp.sum(jnp.where(at_pos, eids[None, :], 0), axis=1),
                            -1).astype(jnp.int32)
    block_pos = jnp.sum(jnp.where(block_expert[:, None] == eids[None, :], expert_pos[None, :], 0),
                        axis=1).astype(jnp.int32)
    yb = _experts(xs, block_expert, nvalid, block_pos, expert_list,
                  w_gate_up[0], b_gate_up[0], w_down[0], b_down[0])

    offa = (pad_start[None, :] + (tcar - tcar % WIN_ALIGN)).astype(jnp.int32).reshape(-1)
    nchunk = jnp.maximum(jnp.max((run_len + WIN - 1) // WIN, axis=1), 1).astype(jnp.int32)
    need_tail = (tcar % WIN_ALIGN + jnp.minimum(run_len, WIN)) > WIN_HEAD
    ntail = jnp.sum(need_tail, axis=1).astype(jnp.int32)
    gfin = norm_final.reshape(1, D_MODEL)
    tok_info = jnp.concatenate([col.astype(F32), lrank.astype(F32), gates], axis=0).T
    y_prompt, y_sample = _combine(x1_p, x1_s, tok_info, gfin, yb, offa, nchunk,
                                  need_tail.astype(jnp.int32).reshape(-1), ntail)

    kv_shape = (1, 1, N_META + WINDOW, N_KV_HEADS, HEAD_DIM)
    new_k_p = jnp.concatenate([kmeta, ktail], axis=0).reshape(kv_shape)
    new_v_p = jnp.concatenate([vmeta, vtail], axis=0).reshape(kv_shape)
    new_pool_p = ptail[16 - POOL_STATE:].reshape(1, 1, POOL_STATE, POOL_WIDTH)
    new_k_s = from_batch_minor(kout_t)
    new_v_s = from_batch_minor(vout_t)
    new_pool_s = jnp.concatenate([state_pool[0][:, 1:], pnew[:, None]], axis=1)[None]
    return (y_prompt[None], y_sample[:, None], new_k_p, new_v_p, new_pool_p, new_k_s, new_v_s, new_pool_s)
```

```python
import functools

import jax
import jax.numpy as jnp
import numpy as np
from jax import lax
from jax.experimental import pallas as pl
from jax.experimental.pallas import tpu as pltpu

F32 = jnp.float32
BF16 = jnp.bfloat16

D_MODEL = 1024
N_META = 16
N_HEADS = 8
HEAD_DIM = 64
N_KV_HEADS = 2
GQA_GROUP = N_HEADS // N_KV_HEADS
ATTN_WIDTH = N_HEADS * HEAD_DIM
KV_WIDTH = N_KV_HEADS * HEAD_DIM
WINDOW = 128
POOL_WIDTH = D_MODEL - ATTN_WIDTH
POOL_WINDOWS = (2, 4, 8, 16)
POOL_GROUP_DIM = POOL_WIDTH // len(POOL_WINDOWS)
POOL_STATE = max(POOL_WINDOWS) - 1
N_EXPERTS = 32
TOP_K = 4
D_EXPERT = D_MODEL
SWIGLU_ALPHA = 1.702
SWIGLU_LIMIT = 7.0
NORM_EPS = 1e-5
PAST_LEN = 16384

LANES = 128
QSUB = 64
KEYS_SUB = QSUB + WINDOW
META_PAD = 64
NKEY = META_PAD + KEYS_SUB
MASKED = -1e30
PROMPT_BLOCK = 1024
ROUTE_BLOCK = 384
EXPERT_ROWS = 512
DISPATCH_TILE = 384
PACK_CHUNKS = D_MODEL // 2 // LANES
COMBINE_TILE = 128
WIN = 32
WIN_ALIGN = 16
WIN_ROWS = WIN + WIN_ALIGN
WIN_HEAD = 32
VMEM_LIMIT = 56 * 1024 * 1024


def _rms(x, g):
    return x * lax.rsqrt(jnp.mean(x * x, axis=-1, keepdims=True) + NORM_EPS) * g


def _router_logits(rwt_ref, h2, h2_hi):
    nt = (((1,), (1,)), ((), ()))
    h2_lo = (h2 - h2_hi.astype(F32)).astype(BF16)
    return (lax.dot_general(rwt_ref[0], h2_hi, nt, preferred_element_type=F32)
            + lax.dot_general(rwt_ref[0], h2_lo, nt, preferred_element_type=F32)
            + lax.dot_general(rwt_ref[1], h2_hi, nt, preferred_element_type=F32))


def _dup_halves(a):
    lane = lax.broadcasted_iota(jnp.int32, a.shape, 1)
    r = pltpu.roll(a, HEAD_DIM, axis=1)
    lo = lane < HEAD_DIM
    return jnp.where(lo, a, r), jnp.where(lo, r, a)


def _pool_means(pext_ref, n):
    outs = []
    for gi, w in enumerate(POOL_WINDOWS):
        xg = pext_ref[:, gi * POOL_GROUP_DIM:(gi + 1) * POOL_GROUP_DIM]
        s = xg
        sh = 1
        while sh < w:
            s = s + pltpu.roll(s, sh, axis=0)
            sh *= 2
        outs.append(s[16:] * (1.0 / w) - xg[16:])
    return outs


def _pack_exact_bf16_pairs(h):
    m = h.shape[1] // 2
    return (lax.shift_right_logical(pltpu.bitcast(h[:, :m], jnp.uint32), jnp.uint32(16))
            | (pltpu.bitcast(h[:, m:], jnp.uint32) & jnp.uint32(0xFFFF0000)))


def _unpack_bf16_pairs(w):
    lo = pltpu.bitcast(lax.shift_left(w, jnp.uint32(16)), F32).astype(BF16)
    hi = pltpu.bitcast(w & jnp.uint32(0xFFFF0000), F32).astype(BF16)
    return lo, hi


def _prompt_kernel(x_ref, meta_ref, gattn_ref, win_ref, wpool_ref, pscale_ref, wout_ref, gffn_ref,
                   rwt_ref, rb_ref, sink_ref, tbl_ref, tail_h2_ref, tail_lgt_ref,
                   x1_ref, h2_ref, lgt_ref, kmeta_ref, vmeta_ref, ktail_ref, vtail_ref, ptail_ref,
                   k2buf, v2buf, km2, vm2, qbuf, obuf, pext):
    pid = pl.program_id(0)
    n_main = pl.num_programs(0) - 1
    refs = (x_ref, meta_ref, gattn_ref, win_ref, wpool_ref, pscale_ref, wout_ref, gffn_ref,
            rwt_ref, rb_ref, sink_ref, tbl_ref,
            x1_ref, h2_ref, lgt_ref, kmeta_ref, vmeta_ref, ktail_ref, vtail_ref, ptail_ref,
            k2buf, v2buf, km2, vm2, qbuf, obuf, pext)

    @pl.when(pid < n_main)
    def _():
        _prompt_block(*refs)

    @pl.when(pid == n_main)
    def _():
        h2_ref[0:tail_h2_ref.shape[0], :] = tail_h2_ref[...]
        lgt_ref[:, 0:tail_lgt_ref.shape[1]] = tail_lgt_ref[...]


def _prompt_block(x_ref, meta_ref, gattn_ref, win_ref, wpool_ref, pscale_ref, wout_ref, gffn_ref,
                  rwt_ref, rb_ref, sink_ref, tbl_ref,
                  x1_ref, h2_ref, lgt_ref, kmeta_ref, vmeta_ref, ktail_ref, vtail_ref, ptail_ref,
                  k2buf, v2buf, km2, vm2, qbuf, obuf, pext):
    tb = x_ref.shape[0]
    pid = pl.program_id(0)

    @pl.when(pid == 0)
    def _():
        hm = _rms(meta_ref[...], gattn_ref[...]).astype(BF16)
        km = jnp.dot(hm, win_ref[:, ATTN_WIDTH:ATTN_WIDTH + KV_WIDTH], preferred_element_type=F32)
        vm = jnp.dot(hm, win_ref[:, ATTN_WIDTH + KV_WIDTH:ATTN_WIDTH + 2 * KV_WIDTH],
                     preferred_element_type=F32)
        pm = jnp.dot(hm, win_ref[:, ATTN_WIDTH + 2 * KV_WIDTH:], preferred_element_type=F32)
        kmeta_ref[...] = km
        vmeta_ref[...] = vm
        zpad = jnp.zeros((META_PAD - N_META, LANES), F32)
        k0, k1 = _dup_halves(jnp.concatenate([km, zpad], axis=0))
        v0, v1 = _dup_halves(jnp.concatenate([vm, zpad], axis=0))
        km2[0] = k0.astype(BF16)
        km2[1] = k1.astype(BF16)
        vm2[0, :, 0:LANES] = v0.astype(BF16)
        vm2[1, :, 0:LANES] = v1.astype(BF16)
        vm2[:, :, LANES:] = jnp.ones((N_KV_HEADS, META_PAD, LANES), BF16)
        k2buf[:, 0:WINDOW, :] = jnp.zeros((2, WINDOW, LANES), BF16)
        v2buf[:, 0:WINDOW, 0:LANES] = jnp.zeros((2, WINDOW, LANES), BF16)
        v2buf[:, :, LANES:] = jnp.ones((N_KV_HEADS, WINDOW + tb, LANES), BF16)
        pext[0:16, :] = pm

    h = _rms(x_ref[...], gattn_ref[...]).astype(BF16)
    q = jnp.dot(h, win_ref[:, 0:ATTN_WIDTH], preferred_element_type=F32) * (HEAD_DIM ** -0.5)
    lane_t = lax.broadcasted_iota(jnp.int32, (tb, LANES), 1)
    for c in range(N_HEADS // 2):
        tile = q[:, c * LANES:(c + 1) * LANES]
        for a in range(2):
            keep = (lane_t < HEAD_DIM) if a == 0 else (lane_t >= HEAD_DIM)
            piece = jnp.where(keep, tile, 0.0).astype(BF16).reshape(tb // QSUB, QSUB, LANES)
            row = ((c % 2) * 2 + a) * QSUB
            qbuf[c // 2, :, row:row + QSUB, :] = piece
    k = jnp.dot(h, win_ref[:, ATTN_WIDTH:ATTN_WIDTH + KV_WIDTH], preferred_element_type=F32)
    v = jnp.dot(h, win_ref[:, ATTN_WIDTH + KV_WIDTH:ATTN_WIDTH + 2 * KV_WIDTH], preferred_element_type=F32)
    p = jnp.dot(h, win_ref[:, ATTN_WIDTH + 2 * KV_WIDTH:], preferred_element_type=F32)
    ktail_ref[...] = k[tb - WINDOW:]
    vtail_ref[...] = v[tb - WINDOW:]
    ptail_ref[...] = p[tb - 16:]
    k0, k1 = _dup_halves(k)
    v0, v1 = _dup_halves(v)
    k2buf[0, WINDOW:, :] = k0.astype(BF16)
    k2buf[1, WINDOW:, :] = k1.astype(BF16)
    v2buf[0, WINDOW:, 0:LANES] = v0.astype(BF16)
    v2buf[1, WINDOW:, 0:LANES] = v1.astype(BF16)
    pext[16:, :] = p

    lane_q = lax.broadcasted_iota(jnp.int32, (QSUB, LANES), 1)
    lo_q = lane_q < HEAD_DIM

    for u in range(tb // QSUB):
        r0 = u * QSUB
        sel = jnp.where(pid == 0, u + 1, 0) if u < WINDOW // QSUB else 0
        for g in range(N_KV_HEADS):
            qm = qbuf[g, u]
            kwin = jnp.concatenate([km2[g], k2buf[g, r0:r0 + KEYS_SUB, :]], axis=0)
            vwin = jnp.concatenate([vm2[g], v2buf[g, r0:r0 + KEYS_SUB, :]], axis=0)
            s = lax.dot_general(qm, kwin, (((1,), (1,)), ((), ())), preferred_element_type=F32)
            s = s + tbl_ref[sel, g]
            sink = sink_ref[g]
            m = jnp.maximum(jnp.max(s, axis=1, keepdims=True), sink)
            e = jnp.exp(s - m).astype(BF16)
            r = jnp.dot(e, vwin, preferred_element_type=F32)
            o = r[:, 0:LANES] / (r[:, LANES:] + jnp.exp(sink - m))
            o0 = jnp.where(lo_q, o[0:QSUB], o[QSUB:2 * QSUB])
            o1 = jnp.where(lo_q, o[2 * QSUB:3 * QSUB], o[3 * QSUB:])
            obuf[r0:r0 + QSUB, (2 * g) * LANES:(2 * g + 1) * LANES] = o0.astype(BF16)
            obuf[r0:r0 + QSUB, (2 * g + 1) * LANES:(2 * g + 2) * LANES] = o1.astype(BF16)

    pooled = _pool_means(pext, tb)
    for gi in range(len(POOL_WINDOWS)):
        y = jnp.dot(pooled[gi].astype(BF16), wpool_ref[gi], preferred_element_type=F32)
        y = y * pscale_ref[:, gi * POOL_GROUP_DIM:(gi + 1) * POOL_GROUP_DIM]
        obuf[:, ATTN_WIDTH + gi * POOL_GROUP_DIM:ATTN_WIDTH + (gi + 1) * POOL_GROUP_DIM] = y.astype(BF16)

    k2buf[:, 0:WINDOW, :] = k2buf[:, tb:tb + WINDOW, :]
    v2buf[:, 0:WINDOW, 0:LANES] = v2buf[:, tb:tb + WINDOW, 0:LANES]
    pext[0:16, :] = pext[tb:tb + 16, :]

    x1 = x_ref[...] + jnp.dot(obuf[...], wout_ref[...], preferred_element_type=F32)
    x1_ref[...] = x1
    h2 = _rms(x1, gffn_ref[...])
    h2_hi = h2.astype(BF16)
    h2_ref[...] = h2_hi
    lgt_ref[...] = _router_logits(rwt_ref, h2, h2_hi) + rb_ref[...]


def _attn_tables(sinks):
    i = np.arange(QSUB)[:, None]
    j = np.arange(NKEY)[None, :]
    jb = j - META_PAD
    rel = i + WINDOW - jb
    band_ok = (jb >= 0) & (rel >= 0) & (rel <= WINDOW)
    meta_ok = (j < N_META) & (i >= 0)
    slopes = np.exp2(-8.0 * np.arange(1, N_HEADS + 1) / N_HEADS)
    tbl = np.empty((3, N_KV_HEADS, GQA_GROUP * QSUB, NKEY), np.float32)
    for var in range(3):
        ok = band_ok if var == 0 else band_ok & (jb >= WINDOW - (var - 1) * QSUB)
        for g in range(N_KV_HEADS):
            for a in range(GQA_GROUP):
                hd = g * GQA_GROUP + a
                bias = np.where(ok, -slopes[hd] * rel, MASKED)
                bias = np.where(meta_ok, 0.0, bias)
                tbl[var, g, a * QSUB:(a + 1) * QSUB] = bias
    sink_col = jnp.repeat(sinks.astype(F32).reshape(N_KV_HEADS, GQA_GROUP, 1), QSUB, axis=2)
    return jnp.asarray(tbl), sink_col.reshape(N_KV_HEADS, GQA_GROUP * QSUB, 1)


def _prompt_mixer(x, meta, gattn, win, wpool, pscale, wout, gffn, rwt, rb, sinks, tail_h2, tail_lgt):
    seq = x.shape[0]
    tb = PROMPT_BLOCK
    n_tail = tail_h2.shape[0]
    assert seq % tb == 0 and tb % WINDOW == 0 and n_tail <= tb
    nblk = seq // tb
    n_tok = seq + n_tail
    tbl, sink_col = _attn_tables(sinks)
    full = lambda *shape: pl.BlockSpec(shape, lambda i: (0,) * len(shape))
    main = lambda i: (jnp.minimum(i, nblk - 1), 0)
    in_width = win.shape[1]
    return pl.pallas_call(
        _prompt_kernel,
        grid=(nblk + 1,),
        in_specs=[
            pl.BlockSpec((tb, D_MODEL), main),
            full(N_META, D_MODEL), full(1, D_MODEL), full(D_MODEL, in_width),
            full(len(POOL_WINDOWS), POOL_GROUP_DIM, POOL_GROUP_DIM), full(1, POOL_WIDTH),
            full(D_MODEL, D_MODEL), full(1, D_MODEL), full(2, N_EXPERTS, D_MODEL), full(N_EXPERTS, 1),
            full(N_KV_HEADS, GQA_GROUP * QSUB, 1), full(3, N_KV_HEADS, GQA_GROUP * QSUB, NKEY),
            full(n_tail, D_MODEL), full(N_EXPERTS, n_tail),
        ],
        out_specs=[
            pl.BlockSpec((tb, D_MODEL), main),
            pl.BlockSpec((tb, D_MODEL), lambda i: (i, 0)),
            pl.BlockSpec((N_EXPERTS, tb), lambda i: (0, i)),
            full(N_META, KV_WIDTH), full(N_META, KV_WIDTH),
            full(WINDOW, KV_WIDTH), full(WINDOW, KV_WIDTH), full(16, POOL_WIDTH),
        ],
        out_shape=[
            jax.ShapeDtypeStruct((seq, D_MODEL), F32),
            jax.ShapeDtypeStruct((n_tok, D_MODEL), BF16),
            jax.ShapeDtypeStruct((N_EXPERTS, n_tok), F32),
            jax.ShapeDtypeStruct((N_META, KV_WIDTH), F32),
            jax.ShapeDtypeStruct((N_META, KV_WIDTH), F32),
            jax.ShapeDtypeStruct((WINDOW, KV_WIDTH), F32),
            jax.ShapeDtypeStruct((WINDOW, KV_WIDTH), F32),
            jax.ShapeDtypeStruct((16, POOL_WIDTH), F32),
        ],
        scratch_shapes=[
            pltpu.VMEM((N_KV_HEADS, WINDOW + tb, LANES), BF16),
            pltpu.VMEM((N_KV_HEADS, WINDOW + tb, 2 * LANES), BF16),
            pltpu.VMEM((N_KV_HEADS, META_PAD, LANES), BF16),
            pltpu.VMEM((N_KV_HEADS, META_PAD, 2 * LANES), BF16),
            pltpu.VMEM((N_KV_HEADS, tb // QSUB, GQA_GROUP * QSUB, LANES), BF16),
            pltpu.VMEM((tb, D_MODEL), BF16),
            pltpu.VMEM((16 + tb, POOL_WIDTH), F32),
        ],
        compiler_params=pltpu.CompilerParams(dimension_semantics=("arbitrary",),
                                             vmem_limit_bytes=VMEM_LIMIT),
        name="prompt_mixer",
    )(x, meta, gattn, win, wpool, pscale, wout, gffn, rwt, rb, sink_col, tbl, tail_h2, tail_lgt)


def _sample_kernel(x_ref, ck_ref, cv_ref, sp_ref, gattn_ref, win_ref, wqkv_t_ref, wpool_ref, pscale_ref,
                   wout_ref, gffn_ref, rwt_ref, rb_ref, sink_ref, bias_ref,
                   x1_ref, h2_ref, lgt_ref, kout_ref, vout_ref, pnew_ref,
                   qt_ref, sc_ref, ot_ref, obuf):
    nb = x_ref.shape[0]
    rows = ck_ref.shape[0]
    x = x_ref[...]
    h = _rms(x, gattn_ref[...]).astype(BF16)
    nt = (((1,), (1,)), ((), ()))
    qkv_t = lax.dot_general(wqkv_t_ref[...], h, nt, preferred_element_type=F32)
    qt_ref[...] = qkv_t[0:ATTN_WIDTH] * (HEAD_DIM ** -0.5)
    kt = qkv_t[ATTN_WIDTH:ATTN_WIDTH + KV_WIDTH]
    vt = qkv_t[ATTN_WIDTH + KV_WIDTH:]
    p = jnp.dot(h, win_ref[:, ATTN_WIDTH + 2 * KV_WIDTH:], preferred_element_type=F32)
    pnew_ref[...] = p

    kout_ref[0:N_META] = ck_ref[0:N_META]
    vout_ref[0:N_META] = cv_ref[0:N_META]
    kout_ref[N_META:rows - 1] = ck_ref[N_META + 1:rows]
    vout_ref[N_META:rows - 1] = cv_ref[N_META + 1:rows]
    kout_ref[rows - 1] = kt
    vout_ref[rows - 1] = vt

    def kv_rows(hd):
        g = hd // GQA_GROUP
        return slice(g * HEAD_DIM, (g + 1) * HEAD_DIM)

    def score_row(key_tile, hd):
        prod = qt_ref[hd * HEAD_DIM:(hd + 1) * HEAD_DIM, :] * key_tile[kv_rows(hd), :]
        return jnp.sum(prod, axis=0, keepdims=True)

    def score_pass(s, carry):
        key_tile = ck_ref[s]
        for hd in range(N_HEADS):
            sc_ref[hd, pl.ds(s, 1), :] = score_row(key_tile, hd)
        return carry

    lax.fori_loop(0, rows, score_pass, 0)
    pad_rows = sc_ref.shape[1] - rows - 1
    for hd in range(N_HEADS):
        sc_ref[hd, rows:rows + 1, :] = score_row(kt, hd)
        sc_ref[hd, rows + 1:, :] = jnp.full((pad_rows, nb), MASKED, F32)

    for hd in range(N_HEADS):
        s = sc_ref[hd] + bias_ref[hd]
        sink = sink_ref[hd]
        m = jnp.maximum(jnp.max(s, axis=0, keepdims=True), sink)
        e = jnp.exp(s - m)
        sc_ref[hd] = e * (1.0 / (jnp.sum(e, axis=0, keepdims=True) + jnp.exp(sink - m)))

    for half in range(2):
        heads = range(half * (N_HEADS // 2), (half + 1) * (N_HEADS // 2))

        def value_pass(s, accs, heads=heads):
            val_tile = cv_ref[s]
            return tuple(acc + sc_ref[hd, pl.ds(s, 1), :] * val_tile[kv_rows(hd), :]
                         for acc, hd in zip(accs, heads))

        accs = lax.fori_loop(0, rows, value_pass,
                             tuple(jnp.zeros((HEAD_DIM, nb), F32) for _ in heads))
        for acc, hd in zip(accs, heads):
            ot_ref[hd * HEAD_DIM:(hd + 1) * HEAD_DIM, :] = acc + sc_ref[hd, rows:rows + 1, :] * vt[kv_rows(hd), :]

    for c in range(ATTN_WIDTH // LANES):
        obuf[:, c * LANES:(c + 1) * LANES] = ot_ref[c * LANES:(c + 1) * LANES, :].T.astype(BF16)

    for gi, w in enumerate(POOL_WINDOWS):
        cols = slice(gi * POOL_GROUP_DIM, (gi + 1) * POOL_GROUP_DIM)
        pg = p[:, cols]
        acc = pg
        for d in range(1, w):
            acc = acc + sp_ref[:, POOL_STATE - d, cols]
        pooled = acc * (1.0 / w) - pg
        y = jnp.dot(pooled.astype(BF16), wpool_ref[gi], preferred_element_type=F32) * pscale_ref[:, cols]
        obuf[:, ATTN_WIDTH + gi * POOL_GROUP_DIM:ATTN_WIDTH + (gi + 1) * POOL_GROUP_DIM] = y.astype(BF16)

    x1 = x + jnp.dot(obuf[...], wout_ref[...], preferred_element_type=F32)
    x1_ref[...] = x1
    h2 = _rms(x1, gffn_ref[...])
    h2_hi = h2.astype(BF16)
    h2_ref[...] = h2_hi
    lgt_ref[...] = _router_logits(rwt_ref, h2, h2_hi) + rb_ref[...]


def _sample_mixer(x, ck_t, cv_t, sp, gattn, win, wpool, pscale, wout, gffn, rwt, rb, sinks):
    nb = x.shape[0]
    rows = ck_t.shape[0]
    assert nb == LANES and rows == N_META + WINDOW
    n_keys = -(-(rows + 1) // 8) * 8
    slopes = np.exp2(-8.0 * np.arange(1, N_HEADS + 1) / N_HEADS)
    dist = np.concatenate([np.zeros(N_META), WINDOW - np.arange(WINDOW), np.zeros(1)])
    bias = np.full((N_HEADS, n_keys, 1), MASKED, np.float32)
    bias[:, :rows + 1, 0] = -slopes[:, None] * dist[None, :]
    vm = pl.BlockSpec(memory_space=pltpu.VMEM)
    return pl.pallas_call(
        _sample_kernel,
        in_specs=[vm] * 15,
        out_specs=[vm] * 6,
        out_shape=[
            jax.ShapeDtypeStruct((nb, D_MODEL), F32),
            jax.ShapeDtypeStruct((nb, D_MODEL), BF16),
            jax.ShapeDtypeStruct((N_EXPERTS, nb), F32),
            jax.ShapeDtypeStruct(ck_t.shape, F32),
            jax.ShapeDtypeStruct(cv_t.shape, F32),
            jax.ShapeDtypeStruct((nb, POOL_WIDTH), F32),
        ],
        scratch_shapes=[
            pltpu.VMEM((ATTN_WIDTH, nb), F32),
            pltpu.VMEM((N_HEADS, n_keys, nb), F32),
            pltpu.VMEM((ATTN_WIDTH, nb), F32),
            pltpu.VMEM((nb, D_MODEL), BF16),
        ],
        compiler_params=pltpu.CompilerParams(vmem_limit_bytes=VMEM_LIMIT),
        name="sample_mixer",
    )(x, ck_t, cv_t, sp, gattn, win, win[:, 0:ATTN_WIDTH + 2 * KV_WIDTH].T, wpool, pscale, wout, gffn, rwt, rb,
      sinks.astype(F32).reshape(N_HEADS, 1, 1), jnp.asarray(bias))


def _router_kernel(lg_ref, tri_ref, low_ref, gate_ref, col_ref, lrank_ref, lpos_ref,
                   tcar_ref, cnt_ref, carry):
    tr = lg_ref.shape[1]

    @pl.when(pl.program_id(0) == 0)
    def _():
        carry[...] = jnp.zeros_like(carry)

    work = lg_ref[...]
    eio = lax.broadcasted_iota(jnp.int32, work.shape, 0).astype(F32)
    sels, vals, idxs = [], [], []
    for _k in range(TOP_K):
        mx = jnp.max(work, axis=0, keepdims=True)
        idx = jnp.min(jnp.where(work == mx, eio, float(N_EXPERTS)), axis=0, keepdims=True)
        sel = eio == idx
        sels.append(sel)
        vals.append(mx)
        idxs.append(idx)
        work = jnp.where(sel, -jnp.inf, work)
    exps = [jnp.exp(vk - vals[0]) for vk in vals]
    tot = exps[0] + exps[1] + exps[2] + exps[3]
    onehot = jnp.zeros(work.shape, F32)
    for sel in sels:
        onehot = onehot + sel.astype(F32)
    before = jnp.dot(onehot.astype(BF16), tri_ref[...], preferred_element_type=F32) + carry[...]
    for kk in range(TOP_K):
        gate_ref[pl.ds(kk, 1), :] = exps[kk] / tot
    for j in range(tr // COMBINE_TILE):
        cols = slice(j * COMBINE_TILE, (j + 1) * COMBINE_TILE)
        tc = before[:, j * COMBINE_TILE:j * COMBINE_TILE + 1]
        tcar_ref[j] = tc.astype(jnp.int32)
        slack = tc - WIN_ALIGN * jnp.floor(tc * (1.0 / WIN_ALIGN))
        local = before[:, cols] - tc
        for kk in range(TOP_K):
            selk = sels[kk][:, cols]
            lr = jnp.sum(jnp.where(selk, local, 0.0), axis=0, keepdims=True)
            sl = jnp.sum(jnp.where(selk, slack, 0.0), axis=0, keepdims=True)
            lrank_ref[pl.ds(kk, 1), cols] = lr.astype(jnp.int32)
            col_ref[pl.ds(kk, 1), cols] = (idxs[kk][:, cols] * float(WIN_ROWS) + sl + lr).astype(jnp.int32)
    for j in range(tr // DISPATCH_TILE):
        cols = slice(j * DISPATCH_TILE, (j + 1) * DISPATCH_TILE)
        local = before[:, cols] - before[:, j * DISPATCH_TILE:j * DISPATCH_TILE + 1]
        tile_cnt = jnp.broadcast_to(jnp.sum(onehot[:, cols], axis=1, keepdims=True), local.shape)
        cnt_hi = jnp.floor(tile_cnt * (1.0 / 256.0))
        cnt_lo = tile_cnt - 256.0 * cnt_hi
        run_start = (256.0 * jnp.dot(low_ref[...], cnt_hi.astype(BF16), preferred_element_type=F32)
                     + jnp.dot(low_ref[...], cnt_lo.astype(BF16), preferred_element_type=F32))
        for kk in range(TOP_K):
            lp = jnp.sum(jnp.where(sels[kk][:, cols], run_start + local, 0.0), axis=0, keepdims=True)
            lpos_ref[pl.ds(kk, 1), cols] = lp.astype(jnp.int32)
    carry[...] = carry[...] + jnp.sum(onehot, axis=1, keepdims=True)
    cnt_ref[...] = carry[...].astype(jnp.int32)


def _router(logits_t):
    n = logits_t.shape[1]
    tr = ROUTE_BLOCK
    assert n % tr == 0
    tri = jnp.asarray(np.triu(np.ones((tr, tr), np.float32), k=1), BF16)
    low = jnp.asarray(np.tril(np.ones((N_EXPERTS, N_EXPERTS), np.float32), k=-1), BF16)
    per_tok = pl.BlockSpec((TOP_K, tr), lambda i: (0, i))
    return pl.pallas_call(
        _router_kernel,
        grid=(n // tr,),
        in_specs=[pl.BlockSpec((N_EXPERTS, tr), lambda i: (0, i)),
                  pl.BlockSpec((tr, tr), lambda i: (0, 0)),
                  pl.BlockSpec((N_EXPERTS, N_EXPERTS), lambda i: (0, 0))],
        out_specs=[per_tok, per_tok, per_tok, per_tok,
                   pl.BlockSpec((tr // COMBINE_TILE, N_EXPERTS, 1), lambda i: (i, 0, 0)),
                   pl.BlockSpec((N_EXPERTS, 1), lambda i: (0, 0))],
        out_shape=[jax.ShapeDtypeStruct((TOP_K, n), F32),
                   jax.ShapeDtypeStruct((TOP_K, n), jnp.int32),
                   jax.ShapeDtypeStruct((TOP_K, n), jnp.int32),
                   jax.ShapeDtypeStruct((TOP_K, n), jnp.int32),
                   jax.ShapeDtypeStruct((n // COMBINE_TILE, N_EXPERTS, 1), jnp.int32),
                   jax.ShapeDtypeStruct((N_EXPERTS, 1), jnp.int32)],
        scratch_shapes=[pltpu.VMEM((N_EXPERTS, 1), F32)],
        compiler_params=pltpu.CompilerParams(dimension_semantics=("arbitrary",)),
        name="router",
    )(logits_t, tri, low)


def _dispatch_kernel(lstart_ref, cnt_ref, dst_ref, h2_ref, lpos_ref, xs_hbm, stg0, stg1, sem):
    i = pl.program_id(0)
    n_tiles = pl.num_programs(0)
    dt = h2_ref.shape[0]
    rows = dt * TOP_K
    slot = i % 2
    stgs = (stg0, stg1)
    pieces = [p for p in (256, 128, 64, 32, 16, 8, 4, 2, 1) if p <= dt]
    assert dt < 512

    def drain(s):
        pltpu.make_async_copy(stgs[s], xs_hbm.at[pl.ds(0, rows)], sem.at[s]).wait()

    def issue_runs(tile, live, s):
        for e in range(N_EXPERTS):
            n = jnp.where(live, cnt_ref[tile * N_EXPERTS + e], 0)
            src0 = lstart_ref[tile * N_EXPERTS + e]
            dst0 = dst_ref[tile * N_EXPERTS + e]
            for piece in pieces:
                off = n & ~jnp.int32(2 * piece - 1)

                @pl.when((n & piece) != 0)
                def _(off=off, piece=piece, src0=src0, dst0=dst0):
                    pltpu.make_async_copy(stgs[s].at[pl.ds(src0 + off, piece)],
                                          xs_hbm.at[pl.ds(dst0 + off, piece)], sem.at[s]).start()

    def sort_tile(s):
        rid = lax.broadcasted_iota(jnp.int32, (rows, dt), 0)
        hit = rid == lpos_ref[0:1, :]
        for kk in range(1, TOP_K):
            hit = jnp.logical_or(hit, rid == lpos_ref[kk:kk + 1, :])
        perm = jnp.where(hit, 1.0, 0.0).astype(BF16)
        srt = jnp.dot(perm, h2_ref[...], preferred_element_type=F32)
        packed = _pack_exact_bf16_pairs(srt)
        stgs[s][...] = packed.reshape(rows, PACK_CHUNKS, LANES)

    for s in range(2):
        @pl.when(slot == s)
        def _(s=s):
            @pl.when(i >= 2)
            def _():
                drain(s)

            issue_runs(jnp.maximum(i - 1, 0), i >= 1, 1 - s)
            sort_tile(s)

            @pl.when(i == n_tiles - 1)
            def _():
                issue_runs(i, True, s)
                drain(s)

                @pl.when(n_tiles >= 2)
                def _():
                    drain(1 - s)


def _dispatch(h2, lpos, lstart, cnt, dst, cap):
    n_tok = h2.shape[0]
    dt = DISPATCH_TILE
    assert n_tok % dt == 0
    grid_spec = pltpu.PrefetchScalarGridSpec(
        num_scalar_prefetch=3,
        grid=(n_tok // dt,),
        in_specs=[pl.BlockSpec((dt, D_MODEL), lambda i, a, b, c: (i, 0)),
                  pl.BlockSpec((TOP_K, dt), lambda i, a, b, c: (0, i)),
                  ],
        out_specs=pl.BlockSpec(memory_space=pl.ANY),
        scratch_shapes=[pltpu.VMEM((dt * TOP_K, PACK_CHUNKS, LANES), jnp.uint32),
                        pltpu.VMEM((dt * TOP_K, PACK_CHUNKS, LANES), jnp.uint32),
                        pltpu.SemaphoreType.DMA((2,))],
    )
    return pl.pallas_call(
        _dispatch_kernel,
        grid_spec=grid_spec,
        out_shape=jax.ShapeDtypeStruct((cap, PACK_CHUNKS, LANES), jnp.uint32),
        compiler_params=pltpu.CompilerParams(dimension_semantics=("arbitrary",),
                                             vmem_limit_bytes=VMEM_LIMIT),
        name="dispatch",
    )(lstart, cnt, dst, h2, lpos)


def _expert_kernel(n_xblocks, bexp_ref, nvalid_ref, epos_ref, elist_ref,
                   x_hbm, wgu_hbm, bgu_ref, wd_hbm, bd_ref, y_ref,
                   wgu_f32, wd_f32, wgu_bf, wd_bf, xbuf, xsem, wsem):
    i = pl.program_id(0)
    rb = y_ref.shape[0]
    nvalid = nvalid_ref[i]
    pos = epos_ref[i]
    fresh = jnp.logical_or(i == 0, pos != epos_ref[jnp.maximum(i - 1, 0)])
    slot = i % 2

    def x_copies(blk, s):
        return [pltpu.make_async_copy(x_hbm.at[pl.ds(blk * rb, rb), c, :],
                                      xbuf.at[s, :, pl.ds(c * LANES, LANES)], xsem.at[s])
                for c in range(PACK_CHUNKS)]

    def w_copies(p):
        e = elist_ref[p]
        s = p % 2
        return [pltpu.make_async_copy(wgu_hbm.at[e], wgu_f32.at[s], wsem.at[s, 0]),
                pltpu.make_async_copy(wd_hbm.at[e], wd_f32.at[s], wsem.at[s, 1])]

    @pl.when(i == 0)
    def _():
        for cp in x_copies(0, 0):
            cp.start()

        @pl.when(nvalid > 0)
        def _():
            for cp in w_copies(0):
                cp.start()

    @pl.when(i + 1 < n_xblocks)
    def _():
        for cp in x_copies(i + 1, 1 - slot):
            cp.start()

    @pl.when(jnp.logical_and(fresh, nvalid > 0))
    def _():
        @pl.when(elist_ref[pos + 1] >= 0)
        def _():
            for cp in w_copies(pos + 1):
                cp.start()

        for cp in w_copies(pos):
            cp.wait()
        ws = pos % 2
        chunk = 32

        def cast_gu(r, c):
            r0 = pl.multiple_of(r * chunk, chunk)
            wgu_bf[pl.ds(r0, chunk), :] = wgu_f32[ws, pl.ds(r0, chunk), :].astype(BF16)
            return c

        def cast_d(r, c):
            r0 = pl.multiple_of(r * chunk, chunk)
            wd_bf[pl.ds(r0, chunk), :] = wd_f32[ws, pl.ds(r0, chunk), :].astype(BF16)
            return c

        lax.fori_loop(0, D_MODEL // chunk, cast_gu, 0)
        lax.fori_loop(0, D_EXPERT // chunk, cast_d, 0)

    @pl.when(i < n_xblocks)
    def _():
        for cp in x_copies(i, slot):
            cp.wait()

    def ffn(rows):
        xw = xbuf[slot, 0:rows, :]
        xw = jnp.where(lax.broadcasted_iota(jnp.int32, xw.shape, 0) < nvalid, xw, jnp.uint32(0))
        xlo, xhi = _unpack_bf16_pairs(xw)
        half = D_MODEL // 2
        g = (jnp.dot(xlo, wgu_bf[0:half, 0:D_EXPERT], preferred_element_type=F32)
             + jnp.dot(xhi, wgu_bf[half:, 0:D_EXPERT], preferred_element_type=F32)
             + bgu_ref[0, :, 0:D_EXPERT])
        u = (jnp.dot(xlo, wgu_bf[0:half, D_EXPERT:], preferred_element_type=F32)
             + jnp.dot(xhi, wgu_bf[half:, D_EXPERT:], preferred_element_type=F32)
             + bgu_ref[0, :, D_EXPERT:])
        g = jnp.minimum(g, SWIGLU_LIMIT)
        u = jnp.clip(u, -SWIGLU_LIMIT, SWIGLU_LIMIT)
        act = g * (1.0 / (1.0 + jnp.exp(-SWIGLU_ALPHA * g))) * (u + 1.0)
        y = jnp.dot(act.astype(BF16), wd_bf[...], preferred_element_type=F32) + bd_ref[0]
        row = lax.broadcasted_iota(jnp.int32, y.shape, 0)
        y_ref[0:rows, :] = jnp.where(row < nvalid, y, 0.0).astype(BF16)
        if rows < rb:
            y_ref[rows:, :] = jnp.zeros((rb - rows, D_MODEL), BF16)

    @pl.when(nvalid > rb // 2)
    def _():
        ffn(rb)

    @pl.when(jnp.logical_and(nvalid > 0, nvalid <= rb // 2))
    def _():
        ffn(rb // 2)

    @pl.when(nvalid == 0)
    def _():
        y_ref[...] = jnp.zeros_like(y_ref)


def _experts(xs, block_expert, nvalid, block_pos, expert_list, wgu, bgu, wd, bd):
    rb = EXPERT_ROWS
    n_xblocks = xs.shape[0] // rb
    nblk = n_xblocks + 1
    any_space = pl.BlockSpec(memory_space=pl.ANY)
    grid_spec = pltpu.PrefetchScalarGridSpec(
        num_scalar_prefetch=4,
        grid=(nblk,),
        in_specs=[
            any_space,
            any_space,
            pl.BlockSpec((1, 1, 2 * D_EXPERT), lambda i, be, nu, ep, el: (be[i], 0, 0)),
            any_space,
            pl.BlockSpec((1, 1, D_MODEL), lambda i, be, nu, ep, el: (be[i], 0, 0)),
        ],
        out_specs=pl.BlockSpec((rb, D_MODEL), lambda i, be, nu, ep, el: (i, 0)),
        scratch_shapes=[pltpu.VMEM((2, D_MODEL, 2 * D_EXPERT), F32),
                        pltpu.VMEM((2, D_EXPERT, D_MODEL), F32),
                        pltpu.VMEM((D_MODEL, 2 * D_EXPERT), BF16),
                        pltpu.VMEM((D_EXPERT, D_MODEL), BF16),
                        pltpu.VMEM((2, rb, D_MODEL // 2), jnp.uint32),
                        pltpu.SemaphoreType.DMA((2,)),
                        pltpu.SemaphoreType.DMA((2, 2))],
    )
    return pl.pallas_call(
        functools.partial(_expert_kernel, n_xblocks),
        grid_spec=grid_spec,
        out_shape=jax.ShapeDtypeStruct((nblk * rb, D_MODEL), BF16),
        compiler_params=pltpu.CompilerParams(dimension_semantics=("arbitrary",),
                                             vmem_limit_bytes=VMEM_LIMIT),
        name="experts",
    )(block_expert, nvalid, block_pos, expert_list, xs, wgu, bgu.reshape(N_EXPERTS, 1, 2 * D_EXPERT), wd,
      bd.reshape(N_EXPERTS, 1, D_MODEL))


def _combine_kernel(n_prompt_tiles, offa_ref, nchunk_ref, tail_ref, ntail_ref,
                    x1p_ref, x1s_ref, info_ref, info_next_ref, gfin_ref, yb_hbm,
                    outp_ref, outs_ref, ybuf, gbuf0, gbuf1, acc_ref, sem, tsem):
    i = pl.program_id(0)
    n_tiles = pl.num_programs(0)

    def window_copy(tile, chunk, e, slot):
        base = pl.multiple_of(offa_ref[tile * N_EXPERTS + e] + chunk * WIN, WIN_ALIGN)
        return pltpu.make_async_copy(yb_hbm.at[pl.ds(base, WIN_ROWS), :],
                                     ybuf.at[slot, pl.ds(e * WIN_ROWS, WIN_ROWS), :],
                                     sem.at[slot])

    def start_windows(tile, chunk, slot):
        for e in range(N_EXPERTS):
            window_copy(tile, chunk, e, slot).start()

    def wait_windows(slot):
        pltpu.make_async_copy(yb_hbm.at[pl.ds(0, N_EXPERTS * WIN_ROWS), :], ybuf.at[slot],
                              sem.at[slot]).wait()

    def tail_copy(tile, e, slot):
        base = pl.multiple_of(offa_ref[tile * N_EXPERTS + e] + WIN_HEAD, WIN_ALIGN)
        return pltpu.make_async_copy(yb_hbm.at[pl.ds(base, WIN_ROWS - WIN_HEAD), :],
                                     ybuf.at[slot, pl.ds(e * WIN_ROWS + WIN_HEAD, WIN_ROWS - WIN_HEAD), :],
                                     tsem.at[slot])

    def start_first_chunk(tile, slot):
        for e in range(N_EXPERTS):
            base = pl.multiple_of(offa_ref[tile * N_EXPERTS + e], WIN_ALIGN)
            pltpu.make_async_copy(yb_hbm.at[pl.ds(base, WIN_HEAD), :],
                                  ybuf.at[slot, pl.ds(e * WIN_ROWS, WIN_HEAD), :], sem.at[slot]).start()

            @pl.when(tail_ref[tile * N_EXPERTS + e] != 0)
            def _(e=e):
                tail_copy(tile, e, slot).start()

    def wait_first_chunk(tile, slot):
        pltpu.make_async_copy(yb_hbm.at[pl.ds(0, N_EXPERTS * WIN_HEAD), :],
                              ybuf.at[slot, pl.ds(0, N_EXPERTS * WIN_HEAD), :], sem.at[slot]).wait()

        def one_tail(t, c):
            tail_copy(tile, 0, slot).wait()
            return c

        lax.fori_loop(0, ntail_ref[tile], one_tail, 0)

    slot = i % 2

    @pl.when(i == 0)
    def _():
        ybuf[...] = jnp.zeros_like(ybuf)
        start_first_chunk(0, 0)

    lane = lax.broadcasted_iota(jnp.int32, (COMBINE_TILE, N_EXPERTS * WIN_ROWS), 1)

    def gate_matrix(ref, chunk):
        g = jnp.zeros(lane.shape, F32)
        for kk in range(TOP_K):
            lr = ref[:, TOP_K + kk:TOP_K + kk + 1]
            in_chunk = jnp.logical_and(lr >= chunk * WIN, lr < chunk * WIN + WIN)
            colk = jnp.where(in_chunk, ref[:, kk:kk + 1] - chunk * WIN, -1.0).astype(jnp.int32)
            g = jnp.where(lane == colk, ref[:, 2 * TOP_K + kk:2 * TOP_K + kk + 1], g)
        return g.astype(BF16)

    def moe_rows(gm, buf):
        return jnp.dot(gm, ybuf[buf], preferred_element_type=F32)

    @pl.when(i == 0)
    def _():
        gbuf0[...] = gate_matrix(info_ref, 0)

    def main(s):
        start_first_chunk(jnp.minimum(i + 1, n_tiles - 1), 1 - s)
        wait_first_chunk(i, s)
        g_cur, g_nxt = (gbuf0, gbuf1) if s == 0 else (gbuf1, gbuf0)
        acc_ref[...] = moe_rows(g_cur[...], s)
        g_nxt[...] = gate_matrix(info_next_ref, 0)

    for s in range(2):
        @pl.when(slot == s)
        def _(s=s):
            main(s)

    @pl.when(i == n_tiles - 1)
    def _():
        for s in range(2):
            @pl.when(slot == s)
            def _(s=s):
                wait_first_chunk(i, 1 - s)

    def extra_chunk(j, c):
        start_windows(i, j, 2)
        wait_windows(2)
        acc_ref[...] += moe_rows(gate_matrix(info_ref, j), 2)
        return c

    lax.fori_loop(1, nchunk_ref[i], extra_chunk, 0)

    @pl.when(i < n_prompt_tiles)
    def _():
        outp_ref[...] = _rms(x1p_ref[...] + acc_ref[...], gfin_ref[...])

    @pl.when(i >= n_prompt_tiles)
    def _():
        outs_ref[...] = _rms(x1s_ref[...] + acc_ref[...], gfin_ref[...])


def _combine(x1_p, x1_s, tok_info, gfin, yb, offa, nchunk, need_tail, ntail):
    ct = COMBINE_TILE
    n_p, n_s = x1_p.shape[0] // ct, x1_s.shape[0] // ct
    assert x1_p.shape[0] % ct == 0 and x1_s.shape[0] % ct == 0 and n_s >= 1
    n_info = tok_info.shape[1]
    grid_spec = pltpu.PrefetchScalarGridSpec(
        num_scalar_prefetch=4,
        grid=(n_p + n_s,),
        in_specs=[
            pl.BlockSpec((ct, D_MODEL), lambda i, *_: (jnp.minimum(i, n_p - 1), 0)),
            pl.BlockSpec((ct, D_MODEL), lambda i, *_: (jnp.maximum(i - n_p, 0), 0)),
            pl.BlockSpec((ct, n_info), lambda i, *_: (i, 0)),
            pl.BlockSpec((ct, n_info), lambda i, *_: (jnp.minimum(i + 1, n_p + n_s - 1), 0)),
            pl.BlockSpec((1, D_MODEL), lambda i, *_: (0, 0)),
            pl.BlockSpec(memory_space=pl.ANY),
        ],
        out_specs=[
            pl.BlockSpec((ct, D_MODEL), lambda i, *_: (jnp.minimum(i, n_p - 1), 0)),
            pl.BlockSpec((ct, D_MODEL), lambda i, *_: (jnp.maximum(i - n_p, 0), 0)),
        ],
        scratch_shapes=[pltpu.VMEM((3, N_EXPERTS * WIN_ROWS, D_MODEL), BF16),
                        pltpu.VMEM((ct, N_EXPERTS * WIN_ROWS), BF16),
                        pltpu.VMEM((ct, N_EXPERTS * WIN_ROWS), BF16),
                        pltpu.VMEM((ct, D_MODEL), F32),
                        pltpu.SemaphoreType.DMA((3,)),
                        pltpu.SemaphoreType.DMA((2,))],
    )
    return pl.pallas_call(
        functools.partial(_combine_kernel, n_p),
        grid_spec=grid_spec,
        out_shape=[jax.ShapeDtypeStruct(x1_p.shape, F32), jax.ShapeDtypeStruct(x1_s.shape, F32)],
        compiler_params=pltpu.CompilerParams(dimension_semantics=("arbitrary",),
                                             vmem_limit_bytes=VMEM_LIMIT),
        name="combine",
    )(offa, nchunk, need_tail, ntail, x1_p, x1_s, tok_info, tok_info, gfin, yb)


def kernel(x_prompt, x_sample, cache_k, cache_v, state_pool, meta_tokens, norm_attn, w_in, attn_sinks,
           w_pool, pool_scale, w_out, norm_ffn, router_w, router_b, w_gate_up, b_gate_up, w_down, b_down,
           norm_final):
    assert w_in.shape[0] == 1, "single-layer trunk"
    bsz, seq, _ = x_prompt.shape
    assert bsz == 1
    nb = x_sample.shape[0]
    n_tok = seq + nb
    gattn = norm_attn[0].reshape(1, D_MODEL)
    gffn = norm_ffn[0].reshape(1, D_MODEL)
    win = w_in[0].astype(BF16)
    wpool = w_pool[0].astype(BF16)
    wout = w_out[0].astype(BF16)
    pscale = pool_scale[0].reshape(1, POOL_WIDTH)
    rw_t = router_w[0].T
    rw_hi = rw_t.astype(BF16)
    rwt = jnp.stack([rw_hi, (rw_t - rw_hi.astype(F32)).astype(BF16)])
    rb = router_b[0].reshape(N_EXPERTS, 1)
    sinks = attn_sinks[0]

    cache_rows = N_META + WINDOW
    to_batch_minor = lambda c: jnp.transpose(c[0], (1, 2, 3, 0)).reshape(cache_rows, KV_WIDTH, nb)
    from_batch_minor = lambda c: jnp.transpose(
        c.reshape(cache_rows, N_KV_HEADS, HEAD_DIM, nb), (3, 0, 1, 2))[None]
    (x1_s, h2_s, lgt_s, kout_t, vout_t, pnew) = _sample_mixer(
        x_sample[:, 0], to_batch_minor(cache_k), to_batch_minor(cache_v), state_pool[0],
        gattn, win, wpool, pscale, wout, gffn, rwt, rb, sinks)
    (x1_p, h2_all, lgt_all, kmeta, vmeta, ktail, vtail, ptail) = _prompt_mixer(
        x_prompt[0], meta_tokens, gattn, win, wpool, pscale, wout, gffn, rwt, rb, sinks, h2_s, lgt_s)

    gates, col, lrank, lpos, tcar, counts = _router(lgt_all)
    counts = counts[:, 0]
    tcar = tcar[:, :, 0]
    rbk = EXPERT_ROWS
    eids = jnp.arange(N_EXPERTS, dtype=jnp.int32)
    earlier = eids[None, :] < eids[:, None]
    excl_sum = lambda a: jnp.sum(jnp.where(earlier, a[..., None, :], 0), axis=-1)
    padded = (counts + rbk - 1) // rbk * rbk
    pad_start = excl_sum(padded).astype(jnp.int32)
    pad_end = pad_start + padded
    nblk = -(-(n_tok * TOP_K) // rbk) + N_EXPERTS
    cap = nblk * rbk
    block_start = jnp.arange(nblk + 1, dtype=jnp.int32) * rbk
    owns = (pad_start[None, :] <= block_start[:, None]) & (block_start[:, None] < pad_end[None, :])
    nvalid = jnp.sum(jnp.where(owns, jnp.clip(counts[None, :] - (block_start[:, None] - pad_start[None, :]),
                                              0, rbk), 0), axis=1).astype(jnp.int32)
    has_rows = counts > 0
    last_e = jnp.max(jnp.where(has_rows, eids, 0))
    block_expert = jnp.where(jnp.any(owns, axis=1), jnp.sum(jnp.where(owns, eids[None, :], 0), axis=1),
                             last_e).astype(jnp.int32)

    run_len = jnp.concatenate([tcar[1:], counts[None, :]], axis=0) - tcar
    dcar = tcar[::DISPATCH_TILE // COMBINE_TILE]
    drun_len = jnp.concatenate([dcar[1:], counts[None, :]], axis=0) - dcar
    flat = lambda a: a.astype(jnp.int32).reshape(-1)
    xs = _dispatch(h2_all, lpos, flat(excl_sum(drun_len)), flat(drun_len), flat(pad_start[None, :] + dcar),
                   cap)
    expert_pos = excl_sum(has_rows.astype(jnp.int32))
    at_pos = has_rows[None, :] & (expert_pos[None, :] == jnp.arange(N_EXPERTS + 1, dtype=jnp.int32)[:, None])
    expert_list = jnp.where(jnp.any(at_pos, axis=1), jnp.sum(jnp.where(at_pos, eids[None, :], 0), axis=1),
                            -1).astype(jnp.int32)
    block_pos = jnp.sum(jnp.where(block_expert[:, None] == eids[None, :], expert_pos[None, :], 0),
                        axis=1).astype(jnp.int32)
    yb = _experts(xs, block_expert, nvalid, block_pos, expert_list,
                  w_gate_up[0], b_gate_up[0], w_down[0], b_down[0])

    offa = (pad_start[None, :] + (tcar - tcar % WIN_ALIGN)).astype(jnp.int32).reshape(-1)
    nchunk = jnp.maximum(jnp.max((run_len + WIN - 1) // WIN, axis=1), 1).astype(jnp.int32)
    need_tail = (tcar % WIN_ALIGN + jnp.minimum(run_len, WIN)) > WIN_HEAD
    ntail = jnp.sum(need_tail, axis=1).astype(jnp.int32)
    gfin = norm_final.reshape(1, D_MODEL)
    tok_info = jnp.concatenate([col.astype(F32), lrank.astype(F32), gates], axis=0).T
    y_prompt, y_sample = _combine(x1_p, x1_s, tok_info, gfin, yb, offa, nchunk,
                                  need_tail.astype(jnp.int32).reshape(-1), ntail)

    kv_shape = (1, 1, N_META + WINDOW, N_KV_HEADS, HEAD_DIM)
    new_k_p = jnp.concatenate([kmeta, ktail], axis=0).reshape(kv_shape)
    new_v_p = jnp.concatenate([vmeta, vtail], axis=0).reshape(kv_shape)
    new_pool_p = ptail[16 - POOL_STATE:].reshape(1, 1, POOL_STATE, POOL_WIDTH)
    new_k_s = from_batch_minor(kout_t)
    new_v_s = from_batch_minor(vout_t)
    new_pool_s = jnp.concatenate([state_pool[0][:, 1:], pnew[:, None]], axis=1)[None]
    return (y_prompt[None], y_sample[:, None], new_k_p, new_v_p, new_pool_p, new_k_s, new_v_s, new_pool_s)
```

```python
import functools

import jax
import jax.numpy as jnp
import numpy as np
from jax import lax
from jax.experimental import pallas as pl
from jax.experimental.pallas import tpu as pltpu

F32 = jnp.float32
BF16 = jnp.bfloat16

D_MODEL = 1024
N_META = 16
N_HEADS = 8
HEAD_DIM = 64
N_KV_HEADS = 2
GQA_GROUP = N_HEADS // N_KV_HEADS
ATTN_WIDTH = N_HEADS * HEAD_DIM
KV_WIDTH = N_KV_HEADS * HEAD_DIM
WINDOW = 128
POOL_WIDTH = D_MODEL - ATTN_WIDTH
POOL_WINDOWS = (2, 4, 8, 16)
POOL_GROUP_DIM = POOL_WIDTH // len(POOL_WINDOWS)
POOL_STATE = max(POOL_WINDOWS) - 1
N_EXPERTS = 32
TOP_K = 4
D_EXPERT = D_MODEL
SWIGLU_ALPHA = 1.702
SWIGLU_LIMIT = 7.0
NORM_EPS = 1e-5
PAST_LEN = 16384

LANES = 128
QSUB = 64
KEYS_SUB = QSUB + WINDOW
META_PAD = 64
NKEY = META_PAD + KEYS_SUB
MASKED = -1e30
PROMPT_BLOCK = 1024
ROUTE_BLOCK = 384
EXPERT_ROWS = 512
DISPATCH_TILE = 384
PACK_CHUNKS = D_MODEL // 2 // LANES
COMBINE_TILE = 128
WIN = 32
WIN_ALIGN = 16
WIN_ROWS = WIN + WIN_ALIGN
WIN_HEAD = 32
VMEM_LIMIT = 56 * 1024 * 1024


def _rms(x, g):
    return x * lax.rsqrt(jnp.mean(x * x, axis=-1, keepdims=True) + NORM_EPS) * g


def _router_logits(rwt_ref, h2, h2_hi):
    nt = (((1,), (1,)), ((), ()))
    h2_lo = (h2 - h2_hi.astype(F32)).astype(BF16)
    return (lax.dot_general(rwt_ref[0], h2_hi, nt, preferred_element_type=F32)
            + lax.dot_general(rwt_ref[0], h2_lo, nt, preferred_element_type=F32)
            + lax.dot_general(rwt_ref[1], h2_hi, nt, preferred_element_type=F32))


def _dup_halves(a):
    lane = lax.broadcasted_iota(jnp.int32, a.shape, 1)
    r = pltpu.roll(a, HEAD_DIM, axis=1)
    lo = lane < HEAD_DIM
    return jnp.where(lo, a, r), jnp.where(lo, r, a)


def _pool_means(pext_ref, n):
    outs = []
    for gi, w in enumerate(POOL_WINDOWS):
        xg = pext_ref[:, gi * POOL_GROUP_DIM:(gi + 1) * POOL_GROUP_DIM]
        s = xg
        sh = 1
        while sh < w:
            s = s + pltpu.roll(s, sh, axis=0)
            sh *= 2
        outs.append(s[16:] * (1.0 / w) - xg[16:])
    return outs


def _pack_exact_bf16_pairs(h):
    m = h.shape[1] // 2
    return (lax.shift_right_logical(pltpu.bitcast(h[:, :m], jnp.uint32), jnp.uint32(16))
            | (pltpu.bitcast(h[:, m:], jnp.uint32) & jnp.uint32(0xFFFF0000)))


def _unpack_bf16_pairs(w):
    lo = pltpu.bitcast(lax.shift_left(w, jnp.uint32(16)), F32).astype(BF16)
    hi = pltpu.bitcast(w & jnp.uint32(0xFFFF0000), F32).astype(BF16)
    return lo, hi


def _prompt_kernel(x_ref, meta_ref, gattn_ref, win_ref, wpool_ref, pscale_ref, wout_ref, gffn_ref,
                   rwt_ref, rb_ref, sink_ref, tbl_ref, tail_h2_ref, tail_lgt_ref,
                   x1_ref, h2_ref, lgt_ref, kmeta_ref, vmeta_ref, ktail_ref, vtail_ref, ptail_ref,
                   k2buf, v2buf, km2, vm2, qbuf, obuf, pext):
    pid = pl.program_id(0)
    n_main = pl.num_programs(0) - 1
    refs = (x_ref, meta_ref, gattn_ref, win_ref, wpool_ref, pscale_ref, wout_ref, gffn_ref,
            rwt_ref, rb_ref, sink_ref, tbl_ref,
            x1_ref, h2_ref, lgt_ref, kmeta_ref, vmeta_ref, ktail_ref, vtail_ref, ptail_ref,
            k2buf, v2buf, km2, vm2, qbuf, obuf, pext)

    @pl.when(pid < n_main)
    def _():
        _prompt_block(*refs)

    @pl.when(pid == n_main)
    def _():
        h2_ref[0:tail_h2_ref.shape[0], :] = tail_h2_ref[...]
        lgt_ref[:, 0:tail_lgt_ref.shape[1]] = tail_lgt_ref[...]


def _prompt_block(x_ref, meta_ref, gattn_ref, win_ref, wpool_ref, pscale_ref, wout_ref, gffn_ref,
                  rwt_ref, rb_ref, sink_ref, tbl_ref,
                  x1_ref, h2_ref, lgt_ref, kmeta_ref, vmeta_ref, ktail_ref, vtail_ref, ptail_ref,
                  k2buf, v2buf, km2, vm2, qbuf, obuf, pext):
    tb = x_ref.shape[0]
    pid = pl.program_id(0)

    @pl.when(pid == 0)
    def _():
        hm = _rms(meta_ref[...], gattn_ref[...]).astype(BF16)
        km = jnp.dot(hm, win_ref[:, ATTN_WIDTH:ATTN_WIDTH + KV_WIDTH], preferred_element_type=F32)
        vm = jnp.dot(hm, win_ref[:, ATTN_WIDTH + KV_WIDTH:ATTN_WIDTH + 2 * KV_WIDTH],
                     preferred_element_type=F32)
        pm = jnp.dot(hm, win_ref[:, ATTN_WIDTH + 2 * KV_WIDTH:], preferred_element_type=F32)
        kmeta_ref[...] = km
        vmeta_ref[...] = vm
        zpad = jnp.zeros((META_PAD - N_META, LANES), F32)
        k0, k1 = _dup_halves(jnp.concatenate([km, zpad], axis=0))
        v0, v1 = _dup_halves(jnp.concatenate([vm, zpad], axis=0))
        km2[0] = k0.astype(BF16)
        km2[1] = k1.astype(BF16)
        vm2[0, :, 0:LANES] = v0.astype(BF16)
        vm2[1, :, 0:LANES] = v1.astype(BF16)
        vm2[:, :, LANES:] = jnp.ones((N_KV_HEADS, META_PAD, LANES), BF16)
        k2buf[:, 0:WINDOW, :] = jnp.zeros((2, WINDOW, LANES), BF16)
        v2buf[:, 0:WINDOW, 0:LANES] = jnp.zeros((2, WINDOW, LANES), BF16)
        v2buf[:, :, LANES:] = jnp.ones((N_KV_HEADS, WINDOW + tb, LANES), BF16)
        pext[0:16, :] = pm

    h = _rms(x_ref[...], gattn_ref[...]).astype(BF16)
    q = jnp.dot(h, win_ref[:, 0:ATTN_WIDTH], preferred_element_type=F32) * (HEAD_DIM ** -0.5)
    lane_t = lax.broadcasted_iota(jnp.int32, (tb, LANES), 1)
    for c in range(N_HEADS // 2):
        tile = q[:, c * LANES:(c + 1) * LANES]
        for a in range(2):
            keep = (lane_t < HEAD_DIM) if a == 0 else (lane_t >= HEAD_DIM)
            piece = jnp.where(keep, tile, 0.0).astype(BF16).reshape(tb // QSUB, QSUB, LANES)
            row = ((c % 2) * 2 + a) * QSUB
            qbuf[c // 2, :, row:row + QSUB, :] = piece
    k = jnp.dot(h, win_ref[:, ATTN_WIDTH:ATTN_WIDTH + KV_WIDTH], preferred_element_type=F32)
    v = jnp.dot(h, win_ref[:, ATTN_WIDTH + KV_WIDTH:ATTN_WIDTH + 2 * KV_WIDTH], preferred_element_type=F32)
    p = jnp.dot(h, win_ref[:, ATTN_WIDTH + 2 * KV_WIDTH:], preferred_element_type=F32)
    ktail_ref[...] = k[tb - WINDOW:]
    vtail_ref[...] = v[tb - WINDOW:]
    ptail_ref[...] = p[tb - 16:]
    k0, k1 = _dup_halves(k)
    v0, v1 = _dup_halves(v)
    k2buf[0, WINDOW:, :] = k0.astype(BF16)
    k2buf[1, WINDOW:, :] = k1.astype(BF16)
    v2buf[0, WINDOW:, 0:LANES] = v0.astype(BF16)
    v2buf[1, WINDOW:, 0:LANES] = v1.astype(BF16)
    pext[16:, :] = p

    lane_q = lax.broadcasted_iota(jnp.int32, (QSUB, LANES), 1)
    lo_q = lane_q < HEAD_DIM

    for u in range(tb // QSUB):
        r0 = u * QSUB
        sel = jnp.where(pid == 0, u + 1, 0) if u < WINDOW // QSUB else 0
        for g in range(N_KV_HEADS):
            qm = qbuf[g, u]
            kwin = jnp.concatenate([km2[g], k2buf[g, r0:r0 + KEYS_SUB, :]], axis=0)
            vwin = jnp.concatenate([vm2[g], v2buf[g, r0:r0 + KEYS_SUB, :]], axis=0)
            s = lax.dot_general(qm, kwin, (((1,), (1,)), ((), ())), preferred_element_type=F32)
            s = s + tbl_ref[sel, g]
            sink = sink_ref[g]
            m = jnp.maximum(jnp.max(s, axis=1, keepdims=True), sink)
            e = jnp.exp(s - m).astype(BF16)
            r = jnp.dot(e, vwin, preferred_element_type=F32)
            o = r[:, 0:LANES] / (r[:, LANES:] + jnp.exp(sink - m))
            o0 = jnp.where(lo_q, o[0:QSUB], o[QSUB:2 * QSUB])
            o1 = jnp.where(lo_q, o[2 * QSUB:3 * QSUB], o[3 * QSUB:])
            obuf[r0:r0 + QSUB, (2 * g) * LANES:(2 * g + 1) * LANES] = o0.astype(BF16)
            obuf[r0:r0 + QSUB, (2 * g + 1) * LANES:(2 * g + 2) * LANES] = o1.astype(BF16)

    pooled = _pool_means(pext, tb)
    for gi in range(len(POOL_WINDOWS)):
        y = jnp.dot(pooled[gi].astype(BF16), wpool_ref[gi], preferred_element_type=F32)
        y = y * pscale_ref[:, gi * POOL_GROUP_DIM:(gi + 1) * POOL_GROUP_DIM]
        obuf[:, ATTN_WIDTH + gi * POOL_GROUP_DIM:ATTN_WIDTH + (gi + 1) * POOL_GROUP_DIM] = y.astype(BF16)

    k2buf[:, 0:WINDOW, :] = k2buf[:, tb:tb + WINDOW, :]
    v2buf[:, 0:WINDOW, 0:LANES] = v2buf[:, tb:tb + WINDOW, 0:LANES]
    pext[0:16, :] = pext[tb:tb + 16, :]

    x1 = x_ref[...] + jnp.dot(obuf[...], wout_ref[...], preferred_element_type=F32)
    x1_ref[...] = x1
    h2 = _rms(x1, gffn_ref[...])
    h2_hi = h2.astype(BF16)
    h2_ref[...] = h2_hi
    lgt_ref[...] = _router_logits(rwt_ref, h2, h2_hi) + rb_ref[...]


def _attn_tables(sinks):
    i = np.arange(QSUB)[:, None]
    j = np.arange(NKEY)[None, :]
    jb = j - META_PAD
    rel = i + WINDOW - jb
    band_ok = (jb >= 0) & (rel >= 0) & (rel <= WINDOW)
    meta_ok = (j < N_META) & (i >= 0)
    slopes = np.exp2(-8.0 * np.arange(1, N_HEADS + 1) / N_HEADS)
    tbl = np.empty((3, N_KV_HEADS, GQA_GROUP * QSUB, NKEY), np.float32)
    for var in range(3):
        ok = band_ok if var == 0 else band_ok & (jb >= WINDOW - (var - 1) * QSUB)
        for g in range(N_KV_HEADS):
            for a in range(GQA_GROUP):
                hd = g * GQA_GROUP + a
                bias = np.where(ok, -slopes[hd] * rel, MASKED)
                bias = np.where(meta_ok, 0.0, bias)
                tbl[var, g, a * QSUB:(a + 1) * QSUB] = bias
    sink_col = jnp.repeat(sinks.astype(F32).reshape(N_KV_HEADS, GQA_GROUP, 1), QSUB, axis=2)
    return jnp.asarray(tbl), sink_col.reshape(N_KV_HEADS, GQA_GROUP * QSUB, 1)


def _prompt_mixer(x, meta, gattn, win, wpool, pscale, wout, gffn, rwt, rb, sinks, tail_h2, tail_lgt):
    seq = x.shape[0]
    tb = PROMPT_BLOCK
    n_tail = tail_h2.shape[0]
    assert seq % tb == 0 and tb % WINDOW == 0 and n_tail <= tb
    nblk = seq // tb
    n_tok = seq + n_tail
    tbl, sink_col = _attn_tables(sinks)
    full = lambda *shape: pl.BlockSpec(shape, lambda i: (0,) * len(shape))
    main = lambda i: (jnp.minimum(i, nblk - 1), 0)
    in_width = win.shape[1]
    return pl.pallas_call(
        _prompt_kernel,
        grid=(nblk + 1,),
        in_specs=[
            pl.BlockSpec((tb, D_MODEL), main),
            full(N_META, D_MODEL), full(1, D_MODEL), full(D_MODEL, in_width),
            full(len(POOL_WINDOWS), POOL_GROUP_DIM, POOL_GROUP_DIM), full(1, POOL_WIDTH),
            full(D_MODEL, D_MODEL), full(1, D_MODEL), full(2, N_EXPERTS, D_MODEL), full(N_EXPERTS, 1),
            full(N_KV_HEADS, GQA_GROUP * QSUB, 1), full(3, N_KV_HEADS, GQA_GROUP * QSUB, NKEY),
            full(n_tail, D_MODEL), full(N_EXPERTS, n_tail),
        ],
        out_specs=[
            pl.BlockSpec((tb, D_MODEL), main),
            pl.BlockSpec((tb, D_MODEL), lambda i: (i, 0)),
            pl.BlockSpec((N_EXPERTS, tb), lambda i: (0, i)),
            full(N_META, KV_WIDTH), full(N_META, KV_WIDTH),
            full(WINDOW, KV_WIDTH), full(WINDOW, KV_WIDTH), full(16, POOL_WIDTH),
        ],
        out_shape=[
            jax.ShapeDtypeStruct((seq, D_MODEL), F32),
            jax.ShapeDtypeStruct((n_tok, D_MODEL), BF16),
            jax.ShapeDtypeStruct((N_EXPERTS, n_tok), F32),
            jax.ShapeDtypeStruct((N_META, KV_WIDTH), F32),
            jax.ShapeDtypeStruct((N_META, KV_WIDTH), F32),
            jax.ShapeDtypeStruct((WINDOW, KV_WIDTH), F32),
            jax.ShapeDtypeStruct((WINDOW, KV_WIDTH), F32),
            jax.ShapeDtypeStruct((16, POOL_WIDTH), F32),
        ],
        scratch_shapes=[
            pltpu.VMEM((N_KV_HEADS, WINDOW + tb, LANES), BF16),
            pltpu.VMEM((N_KV_HEADS, WINDOW + tb, 2 * LANES), BF16),
            pltpu.VMEM((N_KV_HEADS, META_PAD, LANES), BF16),
            pltpu.VMEM((N_KV_HEADS, META_PAD, 2 * LANES), BF16),
            pltpu.VMEM((N_KV_HEADS, tb // QSUB, GQA_GROUP * QSUB, LANES), BF16),
            pltpu.VMEM((tb, D_MODEL), BF16),
            pltpu.VMEM((16 + tb, POOL_WIDTH), F32),
        ],
        compiler_params=pltpu.CompilerParams(dimension_semantics=("arbitrary",),
                                             vmem_limit_bytes=VMEM_LIMIT),
        name="prompt_mixer",
    )(x, meta, gattn, win, wpool, pscale, wout, gffn, rwt, rb, sink_col, tbl, tail_h2, tail_lgt)


def _sample_kernel(x_ref, ck_ref, cv_ref, sp_ref, gattn_ref, win_ref, wqkv_t_ref, wpool_ref, pscale_ref,
                   wout_ref, gffn_ref, rwt_ref, rb_ref, sink_ref, bias_ref,
                   x1_ref, h2_ref, lgt_ref, kout_ref, vout_ref, pnew_ref,
                   qt_ref, sc_ref, ot_ref, obuf):
    nb = x_ref.shape[0]
    rows = ck_ref.shape[0]
    x = x_ref[...]
    h = _rms(x, gattn_ref[...]).astype(BF16)
    nt = (((1,), (1,)), ((), ()))
    qkv_t = lax.dot_general(wqkv_t_ref[...], h, nt, preferred_element_type=F32)
    qt_ref[...] = qkv_t[0:ATTN_WIDTH] * (HEAD_DIM ** -0.5)
    kt = qkv_t[ATTN_WIDTH:ATTN_WIDTH + KV_WIDTH]
    vt = qkv_t[ATTN_WIDTH + KV_WIDTH:]
    p = jnp.dot(h, win_ref[:, ATTN_WIDTH + 2 * KV_WIDTH:], preferred_element_type=F32)
    pnew_ref[0:POOL_STATE - 1] = sp_ref[1:POOL_STATE]
    pnew_ref[POOL_STATE - 1] = p

    kout_ref[0:N_META] = ck_ref[0:N_META]
    vout_ref[0:N_META] = cv_ref[0:N_META]
    kout_ref[N_META:rows - 1] = ck_ref[N_META + 1:rows]
    vout_ref[N_META:rows - 1] = cv_ref[N_META + 1:rows]
    kout_ref[rows - 1] = kt
    vout_ref[rows - 1] = vt

    def kv_rows(hd):
        g = hd // GQA_GROUP
        return slice(g * HEAD_DIM, (g + 1) * HEAD_DIM)

    def score_row(key_tile, hd):
        prod = qt_ref[hd * HEAD_DIM:(hd + 1) * HEAD_DIM, :] * key_tile[kv_rows(hd), :]
        return jnp.sum(prod, axis=0, keepdims=True)

    def score_pass(s, carry):
        key_tile = ck_ref[s]
        for hd in range(N_HEADS):
            sc_ref[hd, pl.ds(s, 1), :] = score_row(key_tile, hd)
        return carry

    lax.fori_loop(0, rows, score_pass, 0)
    pad_rows = sc_ref.shape[1] - rows - 1
    for hd in range(N_HEADS):
        sc_ref[hd, rows:rows + 1, :] = score_row(kt, hd)
        sc_ref[hd, rows + 1:, :] = jnp.full((pad_rows, nb), MASKED, F32)

    for hd in range(N_HEADS):
        s = sc_ref[hd] + bias_ref[hd]
        sink = sink_ref[hd]
        m = jnp.maximum(jnp.max(s, axis=0, keepdims=True), sink)
        e = jnp.exp(s - m)
        sc_ref[hd] = e * (1.0 / (jnp.sum(e, axis=0, keepdims=True) + jnp.exp(sink - m)))

    for half in range(2):
        heads = range(half * (N_HEADS // 2), (half + 1) * (N_HEADS // 2))

        def value_pass(s, accs, heads=heads):
            val_tile = cv_ref[s]
            return tuple(acc + sc_ref[hd, pl.ds(s, 1), :] * val_tile[kv_rows(hd), :]
                         for acc, hd in zip(accs, heads))

        accs = lax.fori_loop(0, rows, value_pass,
                             tuple(jnp.zeros((HEAD_DIM, nb), F32) for _ in heads))
        for acc, hd in zip(accs, heads):
            ot_ref[hd * HEAD_DIM:(hd + 1) * HEAD_DIM, :] = acc + sc_ref[hd, rows:rows + 1, :] * vt[kv_rows(hd), :]

    for c in range(ATTN_WIDTH // LANES):
        obuf[:, c * LANES:(c + 1) * LANES] = ot_ref[c * LANES:(c + 1) * LANES, :].T.astype(BF16)

    for gi, w in enumerate(POOL_WINDOWS):
        cols = slice(gi * POOL_GROUP_DIM, (gi + 1) * POOL_GROUP_DIM)
        pg = p[:, cols]
        acc = pg
        for d in range(1, w):
            acc = acc + sp_ref[POOL_STATE - d, :, cols]
        pooled = acc * (1.0 / w) - pg
        y = jnp.dot(pooled.astype(BF16), wpool_ref[gi], preferred_element_type=F32) * pscale_ref[:, cols]
        obuf[:, ATTN_WIDTH + gi * POOL_GROUP_DIM:ATTN_WIDTH + (gi + 1) * POOL_GROUP_DIM] = y.astype(BF16)

    x1 = x + jnp.dot(obuf[...], wout_ref[...], preferred_element_type=F32)
    x1_ref[...] = x1
    h2 = _rms(x1, gffn_ref[...])
    h2_hi = h2.astype(BF16)
    h2_ref[...] = h2_hi
    lgt_ref[...] = _router_logits(rwt_ref, h2, h2_hi) + rb_ref[...]


def _sample_mixer(x, ck_t, cv_t, sp, gattn, win, wpool, pscale, wout, gffn, rwt, rb, sinks):
    nb = x.shape[0]
    rows = ck_t.shape[0]
    assert nb == LANES and rows == N_META + WINDOW
    n_keys = -(-(rows + 1) // 8) * 8
    slopes = np.exp2(-8.0 * np.arange(1, N_HEADS + 1) / N_HEADS)
    dist = np.concatenate([np.zeros(N_META), WINDOW - np.arange(WINDOW), np.zeros(1)])
    bias = np.full((N_HEADS, n_keys, 1), MASKED, np.float32)
    bias[:, :rows + 1, 0] = -slopes[:, None] * dist[None, :]
    vm = pl.BlockSpec(memory_space=pltpu.VMEM)
    return pl.pallas_call(
        _sample_kernel,
        in_specs=[vm] * 15,
        out_specs=[vm] * 6,
        out_shape=[
            jax.ShapeDtypeStruct((nb, D_MODEL), F32),
            jax.ShapeDtypeStruct((nb, D_MODEL), BF16),
            jax.ShapeDtypeStruct((N_EXPERTS, nb), F32),
            jax.ShapeDtypeStruct(ck_t.shape, F32),
            jax.ShapeDtypeStruct(cv_t.shape, F32),
            jax.ShapeDtypeStruct(sp.shape, F32),
        ],
        scratch_shapes=[
            pltpu.VMEM((ATTN_WIDTH, nb), F32),
            pltpu.VMEM((N_HEADS, n_keys, nb), F32),
            pltpu.VMEM((ATTN_WIDTH, nb), F32),
            pltpu.VMEM((nb, D_MODEL), BF16),
        ],
        compiler_params=pltpu.CompilerParams(vmem_limit_bytes=VMEM_LIMIT),
        name="sample_mixer",
    )(x, ck_t, cv_t, sp, gattn, win, win[:, 0:ATTN_WIDTH + 2 * KV_WIDTH].T, wpool, pscale, wout, gffn, rwt, rb,
      sinks.astype(F32).reshape(N_HEADS, 1, 1), jnp.asarray(bias))


def _router_kernel(lg_ref, tri_ref, low_ref, gate_ref, col_ref, lrank_ref, lpos_ref,
                   tcar_ref, cnt_ref, carry):
    tr = lg_ref.shape[1]

    @pl.when(pl.program_id(0) == 0)
    def _():
        carry[...] = jnp.zeros_like(carry)

    work = lg_ref[...]
    eio = lax.broadcasted_iota(jnp.int32, work.shape, 0).astype(F32)
    sels, vals, idxs = [], [], []
    for _k in range(TOP_K):
        mx = jnp.max(work, axis=0, keepdims=True)
        idx = jnp.min(jnp.where(work == mx, eio, float(N_EXPERTS)), axis=0, keepdims=True)
        sel = eio == idx
        sels.append(sel)
        vals.append(mx)
        idxs.append(idx)
        work = jnp.where(sel, -jnp.inf, work)
    exps = [jnp.exp(vk - vals[0]) for vk in vals]
    tot = exps[0] + exps[1] + exps[2] + exps[3]
    onehot = jnp.zeros(work.shape, F32)
    for sel in sels:
        onehot = onehot + sel.astype(F32)
    before = jnp.dot(onehot.astype(BF16), tri_ref[...], preferred_element_type=F32) + carry[...]
    for kk in range(TOP_K):
        gate_ref[pl.ds(kk, 1), :] = exps[kk] / tot
    for j in range(tr // COMBINE_TILE):
        cols = slice(j * COMBINE_TILE, (j + 1) * COMBINE_TILE)
        tc = before[:, j * COMBINE_TILE:j * COMBINE_TILE + 1]
        tcar_ref[j] = tc.astype(jnp.int32)
        slack = tc - WIN_ALIGN * jnp.floor(tc * (1.0 / WIN_ALIGN))
        local = before[:, cols] - tc
        for kk in range(TOP_K):
            selk = sels[kk][:, cols]
            lr = jnp.sum(jnp.where(selk, local, 0.0), axis=0, keepdims=True)
            sl = jnp.sum(jnp.where(selk, slack, 0.0), axis=0, keepdims=True)
            lrank_ref[pl.ds(kk, 1), cols] = lr.astype(jnp.int32)
            col_ref[pl.ds(kk, 1), cols] = (idxs[kk][:, cols] * float(WIN_ROWS) + sl + lr).astype(jnp.int32)
    for j in range(tr // DISPATCH_TILE):
        cols = slice(j * DISPATCH_TILE, (j + 1) * DISPATCH_TILE)
        local = before[:, cols] - before[:, j * DISPATCH_TILE:j * DISPATCH_TILE + 1]
        tile_cnt = jnp.broadcast_to(jnp.sum(onehot[:, cols], axis=1, keepdims=True), local.shape)
        cnt_hi = jnp.floor(tile_cnt * (1.0 / 256.0))
        cnt_lo = tile_cnt - 256.0 * cnt_hi
        run_start = (256.0 * jnp.dot(low_ref[...], cnt_hi.astype(BF16), preferred_element_type=F32)
                     + jnp.dot(low_ref[...], cnt_lo.astype(BF16), preferred_element_type=F32))
        for kk in range(TOP_K):
            lp = jnp.sum(jnp.where(sels[kk][:, cols], run_start + local, 0.0), axis=0, keepdims=True)
            lpos_ref[pl.ds(kk, 1), cols] = lp.astype(jnp.int32)
    carry[...] = carry[...] + jnp.sum(onehot, axis=1, keepdims=True)
    cnt_ref[...] = carry[...].astype(jnp.int32)


def _router(logits_t):
    n = logits_t.shape[1]
    tr = ROUTE_BLOCK
    assert n % tr == 0
    tri = jnp.asarray(np.triu(np.ones((tr, tr), np.float32), k=1), BF16)
    low = jnp.asarray(np.tril(np.ones((N_EXPERTS, N_EXPERTS), np.float32), k=-1), BF16)
    per_tok = pl.BlockSpec((TOP_K, tr), lambda i: (0, i))
    return pl.pallas_call(
        _router_kernel,
        grid=(n // tr,),
        in_specs=[pl.BlockSpec((N_EXPERTS, tr), lambda i: (0, i)),
                  pl.BlockSpec((tr, tr), lambda i: (0, 0)),
                  pl.BlockSpec((N_EXPERTS, N_EXPERTS), lambda i: (0, 0))],
        out_specs=[per_tok, per_tok, per_tok, per_tok,
                   pl.BlockSpec((tr // COMBINE_TILE, N_EXPERTS, 1), lambda i: (i, 0, 0)),
                   pl.BlockSpec((N_EXPERTS, 1), lambda i: (0, 0))],
        out_shape=[jax.ShapeDtypeStruct((TOP_K, n), F32),
                   jax.ShapeDtypeStruct((TOP_K, n), jnp.int32),
                   jax.ShapeDtypeStruct((TOP_K, n), jnp.int32),
                   jax.ShapeDtypeStruct((TOP_K, n), jnp.int32),
                   jax.ShapeDtypeStruct((n // COMBINE_TILE, N_EXPERTS, 1), jnp.int32),
                   jax.ShapeDtypeStruct((N_EXPERTS, 1), jnp.int32)],
        scratch_shapes=[pltpu.VMEM((N_EXPERTS, 1), F32)],
        compiler_params=pltpu.CompilerParams(dimension_semantics=("arbitrary",)),
        name="router",
    )(logits_t, tri, low)


def _dispatch_kernel(lstart_ref, cnt_ref, dst_ref, h2_ref, lpos_ref, xs_hbm, stg0, stg1, sem):
    i = pl.program_id(0)
    n_tiles = pl.num_programs(0)
    dt = h2_ref.shape[0]
    rows = dt * TOP_K
    slot = i % 2
    stgs = (stg0, stg1)
    pieces = [p for p in (256, 128, 64, 32, 16, 8, 4, 2, 1) if p <= dt]
    assert dt < 512

    def drain(s):
        pltpu.make_async_copy(stgs[s], xs_hbm.at[pl.ds(0, rows)], sem.at[s]).wait()

    def issue_runs(tile, live, s):
        for e in range(N_EXPERTS):
            n = jnp.where(live, cnt_ref[tile * N_EXPERTS + e], 0)
            src0 = lstart_ref[tile * N_EXPERTS + e]
            dst0 = dst_ref[tile * N_EXPERTS + e]
            for piece in pieces:
                off = n & ~jnp.int32(2 * piece - 1)

                @pl.when((n & piece) != 0)
                def _(off=off, piece=piece, src0=src0, dst0=dst0):
                    pltpu.make_async_copy(stgs[s].at[pl.ds(src0 + off, piece)],
                                          xs_hbm.at[pl.ds(dst0 + off, piece)], sem.at[s]).start()

    def sort_tile(s):
        rid = lax.broadcasted_iota(jnp.int32, (rows, dt), 0)
        hit = rid == lpos_ref[0:1, :]
        for kk in range(1, TOP_K):
            hit = jnp.logical_or(hit, rid == lpos_ref[kk:kk + 1, :])
        perm = jnp.where(hit, 1.0, 0.0).astype(BF16)
        srt = jnp.dot(perm, h2_ref[...], preferred_element_type=F32)
        packed = _pack_exact_bf16_pairs(srt)
        stgs[s][...] = packed.reshape(rows, PACK_CHUNKS, LANES)

    for s in range(2):
        @pl.when(slot == s)
        def _(s=s):
            @pl.when(i >= 2)
            def _():
                drain(s)

            issue_runs(jnp.maximum(i - 1, 0), i >= 1, 1 - s)
            sort_tile(s)

            @pl.when(i == n_tiles - 1)
            def _():
                issue_runs(i, True, s)
                drain(s)

                @pl.when(n_tiles >= 2)
                def _():
                    drain(1 - s)


def _dispatch(h2, lpos, lstart, cnt, dst, cap):
    n_tok = h2.shape[0]
    dt = DISPATCH_TILE
    assert n_tok % dt == 0
    grid_spec = pltpu.PrefetchScalarGridSpec(
        num_scalar_prefetch=3,
        grid=(n_tok // dt,),
        in_specs=[pl.BlockSpec((dt, D_MODEL), lambda i, a, b, c: (i, 0)),
                  pl.BlockSpec((TOP_K, dt), lambda i, a, b, c: (0, i)),
                  ],
        out_specs=pl.BlockSpec(memory_space=pl.ANY),
        scratch_shapes=[pltpu.VMEM((dt * TOP_K, PACK_CHUNKS, LANES), jnp.uint32),
                        pltpu.VMEM((dt * TOP_K, PACK_CHUNKS, LANES), jnp.uint32),
                        pltpu.SemaphoreType.DMA((2,))],
    )
    return pl.pallas_call(
        _dispatch_kernel,
        grid_spec=grid_spec,
        out_shape=jax.ShapeDtypeStruct((cap, PACK_CHUNKS, LANES), jnp.uint32),
        compiler_params=pltpu.CompilerParams(dimension_semantics=("arbitrary",),
                                             vmem_limit_bytes=VMEM_LIMIT),
        name="dispatch",
    )(lstart, cnt, dst, h2, lpos)


def _expert_kernel(n_xblocks, bexp_ref, nvalid_ref, epos_ref, elist_ref,
                   x_hbm, wgu_hbm, bgu_ref, wd_hbm, bd_ref, y_ref,
                   wgu_f32, wd_f32, wgu_bf, wd_bf, xbuf, xsem, wsem):
    i = pl.program_id(0)
    rb = y_ref.shape[0]
    nvalid = nvalid_ref[i]
    pos = epos_ref[i]
    fresh = jnp.logical_or(i == 0, pos != epos_ref[jnp.maximum(i - 1, 0)])
    slot = i % 2

    def x_copies(blk, s):
        return [pltpu.make_async_copy(x_hbm.at[pl.ds(blk * rb, rb), c, :],
                                      xbuf.at[s, :, pl.ds(c * LANES, LANES)], xsem.at[s])
                for c in range(PACK_CHUNKS)]

    def w_copies(p):
        e = elist_ref[p]
        s = p % 2
        return [pltpu.make_async_copy(wgu_hbm.at[e], wgu_f32.at[s], wsem.at[s, 0]),
                pltpu.make_async_copy(wd_hbm.at[e], wd_f32.at[s], wsem.at[s, 1])]

    @pl.when(i == 0)
    def _():
        for cp in x_copies(0, 0):
            cp.start()

        @pl.when(nvalid > 0)
        def _():
            for cp in w_copies(0):
                cp.start()

    @pl.when(i + 1 < n_xblocks)
    def _():
        for cp in x_copies(i + 1, 1 - slot):
            cp.start()

    @pl.when(jnp.logical_and(fresh, nvalid > 0))
    def _():
        @pl.when(elist_ref[pos + 1] >= 0)
        def _():
            for cp in w_copies(pos + 1):
                cp.start()

        for cp in w_copies(pos):
            cp.wait()
        ws = pos % 2
        chunk = 32

        def cast_gu(r, c):
            r0 = pl.multiple_of(r * chunk, chunk)
            wgu_bf[pl.ds(r0, chunk), :] = wgu_f32[ws, pl.ds(r0, chunk), :].astype(BF16)
            return c

        def cast_d(r, c):
            r0 = pl.multiple_of(r * chunk, chunk)
            wd_bf[pl.ds(r0, chunk), :] = wd_f32[ws, pl.ds(r0, chunk), :].astype(BF16)
            return c

        lax.fori_loop(0, D_MODEL // chunk, cast_gu, 0)
        lax.fori_loop(0, D_EXPERT // chunk, cast_d, 0)

    @pl.when(i < n_xblocks)
    def _():
        for cp in x_copies(i, slot):
            cp.wait()

    def ffn(rows):
        xw = xbuf[slot, 0:rows, :]
        xw = jnp.where(lax.broadcasted_iota(jnp.int32, xw.shape, 0) < nvalid, xw, jnp.uint32(0))
        xlo, xhi = _unpack_bf16_pairs(xw)
        half = D_MODEL // 2
        g = (jnp.dot(xlo, wgu_bf[0:half, 0:D_EXPERT], preferred_element_type=F32)
             + jnp.dot(xhi, wgu_bf[half:, 0:D_EXPERT], preferred_element_type=F32)
             + bgu_ref[0, :, 0:D_EXPERT])
        u = (jnp.dot(xlo, wgu_bf[0:half, D_EXPERT:], preferred_element_type=F32)
             + jnp.dot(xhi, wgu_bf[half:, D_EXPERT:], preferred_element_type=F32)
             + bgu_ref[0, :, D_EXPERT:])
        g = jnp.minimum(g, SWIGLU_LIMIT)
        u = jnp.clip(u, -SWIGLU_LIMIT, SWIGLU_LIMIT)
        act = g * (1.0 / (1.0 + jnp.exp(-SWIGLU_ALPHA * g))) * (u + 1.0)
        y = jnp.dot(act.astype(BF16), wd_bf[...], preferred_element_type=F32) + bd_ref[0]
        row = lax.broadcasted_iota(jnp.int32, y.shape, 0)
        y_ref[0:rows, :] = jnp.where(row < nvalid, y, 0.0).astype(BF16)
        if rows < rb:
            y_ref[rows:, :] = jnp.zeros((rb - rows, D_MODEL), BF16)

    @pl.when(nvalid > rb // 2)
    def _():
        ffn(rb)

    @pl.when(jnp.logical_and(nvalid > 0, nvalid <= rb // 2))
    def _():
        ffn(rb // 2)

    @pl.when(nvalid == 0)
    def _():
        y_ref[...] = jnp.zeros_like(y_ref)


def _experts(xs, block_expert, nvalid, block_pos, expert_list, wgu, bgu, wd, bd):
    rb = EXPERT_ROWS
    n_xblocks = xs.shape[0] // rb
    nblk = n_xblocks + 1
    any_space = pl.BlockSpec(memory_space=pl.ANY)
    grid_spec = pltpu.PrefetchScalarGridSpec(
        num_scalar_prefetch=4,
        grid=(nblk,),
        in_specs=[
            any_space,
            any_space,
            pl.BlockSpec((1, 1, 2 * D_EXPERT), lambda i, be, nu, ep, el: (be[i], 0, 0)),
            any_space,
            pl.BlockSpec((1, 1, D_MODEL), lambda i, be, nu, ep, el: (be[i], 0, 0)),
        ],
        out_specs=pl.BlockSpec((rb, D_MODEL), lambda i, be, nu, ep, el: (i, 0)),
        scratch_shapes=[pltpu.VMEM((2, D_MODEL, 2 * D_EXPERT), F32),
                        pltpu.VMEM((2, D_EXPERT, D_MODEL), F32),
                        pltpu.VMEM((D_MODEL, 2 * D_EXPERT), BF16),
                        pltpu.VMEM((D_EXPERT, D_MODEL), BF16),
                        pltpu.VMEM((2, rb, D_MODEL // 2), jnp.uint32),
                        pltpu.SemaphoreType.DMA((2,)),
                        pltpu.SemaphoreType.DMA((2, 2))],
    )
    return pl.pallas_call(
        functools.partial(_expert_kernel, n_xblocks),
        grid_spec=grid_spec,
        out_shape=jax.ShapeDtypeStruct((nblk * rb, D_MODEL), BF16),
        compiler_params=pltpu.CompilerParams(dimension_semantics=("arbitrary",),
                                             vmem_limit_bytes=VMEM_LIMIT),
        name="experts",
    )(block_expert, nvalid, block_pos, expert_list, xs, wgu, bgu.reshape(N_EXPERTS, 1, 2 * D_EXPERT), wd,
      bd.reshape(N_EXPERTS, 1, D_MODEL))


def _combine_kernel(n_prompt_tiles, offa_ref, nchunk_ref, tail_ref, ntail_ref,
                    x1p_ref, x1s_ref, info_ref, info_next_ref, gfin_ref, yb_hbm,
                    outp_ref, outs_ref, ybuf, gbuf0, gbuf1, acc_ref, sem, tsem):
    i = pl.program_id(0)
    n_tiles = pl.num_programs(0)

    def window_copy(tile, chunk, e, slot):
        base = pl.multiple_of(offa_ref[tile * N_EXPERTS + e] + chunk * WIN, WIN_ALIGN)
        return pltpu.make_async_copy(yb_hbm.at[pl.ds(base, WIN_ROWS), :],
                                     ybuf.at[slot, pl.ds(e * WIN_ROWS, WIN_ROWS), :],
                                     sem.at[slot])

    def start_windows(tile, chunk, slot):
        for e in range(N_EXPERTS):
            window_copy(tile, chunk, e, slot).start()

    def wait_windows(slot):
        pltpu.make_async_copy(yb_hbm.at[pl.ds(0, N_EXPERTS * WIN_ROWS), :], ybuf.at[slot],
                              sem.at[slot]).wait()

    def tail_copy(tile, e, slot):
        base = pl.multiple_of(offa_ref[tile * N_EXPERTS + e] + WIN_HEAD, WIN_ALIGN)
        return pltpu.make_async_copy(yb_hbm.at[pl.ds(base, WIN_ROWS - WIN_HEAD), :],
                                     ybuf.at[slot, pl.ds(e * WIN_ROWS + WIN_HEAD, WIN_ROWS - WIN_HEAD), :],
                                     tsem.at[slot])

    def start_first_chunk(tile, slot):
        for e in range(N_EXPERTS):
            base = pl.multiple_of(offa_ref[tile * N_EXPERTS + e], WIN_ALIGN)
            pltpu.make_async_copy(yb_hbm.at[pl.ds(base, WIN_HEAD), :],
                                  ybuf.at[slot, pl.ds(e * WIN_ROWS, WIN_HEAD), :], sem.at[slot]).start()

            @pl.when(tail_ref[tile * N_EXPERTS + e] != 0)
            def _(e=e):
                tail_copy(tile, e, slot).start()

    def wait_first_chunk(tile, slot):
        pltpu.make_async_copy(yb_hbm.at[pl.ds(0, N_EXPERTS * WIN_HEAD), :],
                              ybuf.at[slot, pl.ds(0, N_EXPERTS * WIN_HEAD), :], sem.at[slot]).wait()

        def one_tail(t, c):
            tail_copy(tile, 0, slot).wait()
            return c

        lax.fori_loop(0, ntail_ref[tile], one_tail, 0)

    slot = i % 2

    @pl.when(i == 0)
    def _():
        ybuf[...] = jnp.zeros_like(ybuf)
        start_first_chunk(0, 0)

    lane = lax.broadcasted_iota(jnp.int32, (COMBINE_TILE, N_EXPERTS * WIN_ROWS), 1)

    def gate_matrix(ref, chunk):
        g = jnp.zeros(lane.shape, F32)
        for kk in range(TOP_K):
            lr = ref[:, TOP_K + kk:TOP_K + kk + 1]
            in_chunk = jnp.logical_and(lr >= chunk * WIN, lr < chunk * WIN + WIN)
            colk = jnp.where(in_chunk, ref[:, kk:kk + 1] - chunk * WIN, -1.0).astype(jnp.int32)
            g = jnp.where(lane == colk, ref[:, 2 * TOP_K + kk:2 * TOP_K + kk + 1], g)
        return g.astype(BF16)

    def moe_rows(gm, buf):
        return jnp.dot(gm, ybuf[buf], preferred_element_type=F32)

    @pl.when(i == 0)
    def _():
        gbuf0[...] = gate_matrix(info_ref, 0)

    def main(s):
        start_first_chunk(jnp.minimum(i + 1, n_tiles - 1), 1 - s)
        wait_first_chunk(i, s)
        g_cur, g_nxt = (gbuf0, gbuf1) if s == 0 else (gbuf1, gbuf0)
        acc_ref[...] = moe_rows(g_cur[...], s)
        g_nxt[...] = gate_matrix(info_next_ref, 0)

    for s in range(2):
        @pl.when(slot == s)
        def _(s=s):
            main(s)

    @pl.when(i == n_tiles - 1)
    def _():
        for s in range(2):
            @pl.when(slot == s)
            def _(s=s):
                wait_first_chunk(i, 1 - s)

    def extra_chunk(j, c):
        start_windows(i, j, 2)
        wait_windows(2)
        acc_ref[...] += moe_rows(gate_matrix(info_ref, j), 2)
        return c

    lax.fori_loop(1, nchunk_ref[i], extra_chunk, 0)

    @pl.when(i < n_prompt_tiles)
    def _():
        outp_ref[...] = _rms(x1p_ref[...] + acc_ref[...], gfin_ref[...])

    @pl.when(i >= n_prompt_tiles)
    def _():
        outs_ref[...] = _rms(x1s_ref[...] + acc_ref[...], gfin_ref[...])


def _combine(x1_p, x1_s, tok_info, gfin, yb, offa, nchunk, need_tail, ntail):
    ct = COMBINE_TILE
    n_p, n_s = x1_p.shape[0] // ct, x1_s.shape[0] // ct
    assert x1_p.shape[0] % ct == 0 and x1_s.shape[0] % ct == 0 and n_s >= 1
    n_info = tok_info.shape[1]
    grid_spec = pltpu.PrefetchScalarGridSpec(
        num_scalar_prefetch=4,
        grid=(n_p + n_s,),
        in_specs=[
            pl.BlockSpec((ct, D_MODEL), lambda i, *_: (jnp.minimum(i, n_p - 1), 0)),
            pl.BlockSpec((ct, D_MODEL), lambda i, *_: (jnp.maximum(i - n_p, 0), 0)),
            pl.BlockSpec((ct, n_info), lambda i, *_: (i, 0)),
            pl.BlockSpec((ct, n_info), lambda i, *_: (jnp.minimum(i + 1, n_p + n_s - 1), 0)),
            pl.BlockSpec((1, D_MODEL), lambda i, *_: (0, 0)),
            pl.BlockSpec(memory_space=pl.ANY),
        ],
        out_specs=[
            pl.BlockSpec((ct, D_MODEL), lambda i, *_: (jnp.minimum(i, n_p - 1), 0)),
            pl.BlockSpec((ct, D_MODEL), lambda i, *_: (jnp.maximum(i - n_p, 0), 0)),
        ],
        scratch_shapes=[pltpu.VMEM((3, N_EXPERTS * WIN_ROWS, D_MODEL), BF16),
                        pltpu.VMEM((ct, N_EXPERTS * WIN_ROWS), BF16),
                        pltpu.VMEM((ct, N_EXPERTS * WIN_ROWS), BF16),
                        pltpu.VMEM((ct, D_MODEL), F32),
                        pltpu.SemaphoreType.DMA((3,)),
                        pltpu.SemaphoreType.DMA((2,))],
    )
    return pl.pallas_call(
        functools.partial(_combine_kernel, n_p),
        grid_spec=grid_spec,
        out_shape=[jax.ShapeDtypeStruct(x1_p.shape, F32), jax.ShapeDtypeStruct(x1_s.shape, F32)],
        compiler_params=pltpu.CompilerParams(dimension_semantics=("arbitrary",),
                                             vmem_limit_bytes=VMEM_LIMIT),
        name="combine",
    )(offa, nchunk, need_tail, ntail, x1_p, x1_s, tok_info, tok_info, gfin, yb)


def kernel(x_prompt, x_sample, cache_k, cache_v, state_pool, meta_tokens, norm_attn, w_in, attn_sinks,
           w_pool, pool_scale, w_out, norm_ffn, router_w, router_b, w_gate_up, b_gate_up, w_down, b_down,
           norm_final):
    assert w_in.shape[0] == 1, "single-layer trunk"
    bsz, seq, _ = x_prompt.shape
    assert bsz == 1
    nb = x_sample.shape[0]
    n_tok = seq + nb
    gattn = norm_attn[0].reshape(1, D_MODEL)
    gffn = norm_ffn[0].reshape(1, D_MODEL)
    win = w_in[0].astype(BF16)
    wpool = w_pool[0].astype(BF16)
    wout = w_out[0].astype(BF16)
    pscale = pool_scale[0].reshape(1, POOL_WIDTH)
    rw_t = router_w[0].T
    rw_hi = rw_t.astype(BF16)
    rwt = jnp.stack([rw_hi, (rw_t - rw_hi.astype(F32)).astype(BF16)])
    rb = router_b[0].reshape(N_EXPERTS, 1)
    sinks = attn_sinks[0]

    cache_rows = N_META + WINDOW
    to_batch_minor = lambda c: jnp.transpose(c[0], (1, 2, 3, 0)).reshape(cache_rows, KV_WIDTH, nb)
    from_batch_minor = lambda c: jnp.transpose(
        c.reshape(cache_rows, N_KV_HEADS, HEAD_DIM, nb), (3, 0, 1, 2))[None]
    (x1_s, h2_s, lgt_s, kout_t, vout_t, pool_t) = _sample_mixer(
        x_sample[:, 0], to_batch_minor(cache_k), to_batch_minor(cache_v),
        jnp.transpose(state_pool[0], (1, 0, 2)), gattn, win, wpool, pscale, wout, gffn, rwt, rb, sinks)
    (x1_p, h2_all, lgt_all, kmeta, vmeta, ktail, vtail, ptail) = _prompt_mixer(
        x_prompt[0], meta_tokens, gattn, win, wpool, pscale, wout, gffn, rwt, rb, sinks, h2_s, lgt_s)

    gates, col, lrank, lpos, tcar, counts = _router(lgt_all)
    counts = counts[:, 0]
    tcar = tcar[:, :, 0]
    rbk = EXPERT_ROWS
    eids = jnp.arange(N_EXPERTS, dtype=jnp.int32)
    earlier = eids[None, :] < eids[:, None]
    excl_sum = lambda a: jnp.sum(jnp.where(earlier, a[..., None, :], 0), axis=-1)
    padded = (counts + rbk - 1) // rbk * rbk
    pad_start = excl_sum(padded).astype(jnp.int32)
    pad_end = pad_start + padded
    nblk = -(-(n_tok * TOP_K) // rbk) + N_EXPERTS
    cap = nblk * rbk
    block_start = jnp.arange(nblk + 1, dtype=jnp.int32) * rbk
    owns = (pad_start[None, :] <= block_start[:, None]) & (block_start[:, None] < pad_end[None, :])
    nvalid = jnp.sum(jnp.where(owns, jnp.clip(counts[None, :] - (block_start[:, None] - pad_start[None, :]),
                                              0, rbk), 0), axis=1).astype(jnp.int32)
    has_rows = counts > 0
    last_e = jnp.max(jnp.where(has_rows, eids, 0))
    block_expert = jnp.where(jnp.any(owns, axis=1), jnp.sum(jnp.where(owns, eids[None, :], 0), axis=1),
                             last_e).astype(jnp.int32)

    run_len = jnp.concatenate([tcar[1:], counts[None, :]], axis=0) - tcar
    dcar = tcar[::DISPATCH_TILE // COMBINE_TILE]
    drun_len = jnp.concatenate([dcar[1:], counts[None, :]], axis=0) - dcar
    flat = lambda a: a.astype(jnp.int32).reshape(-1)
    xs = _dispatch(h2_all, lpos, flat(excl_sum(drun_len)), flat(drun_len), flat(pad_start[None, :] + dcar),
                   cap)
    expert_pos = excl_sum(has_rows.astype(jnp.int32))
    at_pos = has_rows[None, :] & (expert_pos[None, :] == jnp.arange(N_EXPERTS + 1, dtype=jnp.int32)[:, None])
    expert_list = jnp.where(jnp.any(at_pos, axis=1), jnp.sum(jnp.where(at_pos, eids[None, :], 0), axis=1),
                            -1).astype(jnp.int32)
    block_pos = jnp.sum(jnp.where(block_expert[:, None] == eids[None, :], expert_pos[None, :], 0),
                        axis=1).astype(jnp.int32)
    yb = _experts(xs, block_expert, nvalid, block_pos, expert_list,
                  w_gate_up[0], b_gate_up[0], w_down[0], b_down[0])

    offa = (pad_start[None, :] + (tcar - tcar % WIN_ALIGN)).astype(jnp.int32).reshape(-1)
    nchunk = jnp.maximum(jnp.max((run_len + WIN - 1) // WIN, axis=1), 1).astype(jnp.int32)
    need_tail = (tcar % WIN_ALIGN + jnp.minimum(run_len, WIN)) > WIN_HEAD
    ntail = jnp.sum(need_tail, axis=1).astype(jnp.int32)
    gfin = norm_final.reshape(1, D_MODEL)
    tok_info = jnp.concatenate([col.astype(F32), lrank.astype(F32), gates], axis=0).T
    y_prompt, y_sample = _combine(x1_p, x1_s, tok_info, gfin, yb, offa, nchunk,
                                  need_tail.astype(jnp.int32).reshape(-1), ntail)

    kv_shape = (1, 1, N_META + WINDOW, N_KV_HEADS, HEAD_DIM)
    new_k_p = jnp.concatenate([kmeta, ktail], axis=0).reshape(kv_shape)
    new_v_p = jnp.concatenate([vmeta, vtail], axis=0).reshape(kv_shape)
    new_pool_p = ptail[16 - POOL_STATE:].reshape(1, 1, POOL_STATE, POOL_WIDTH)
    new_k_s = from_batch_minor(kout_t)
    new_v_s = from_batch_minor(vout_t)
    new_pool_s = jnp.transpose(pool_t, (1, 0, 2))[None]
    return (y_prompt[None], y_sample[:, None], new_k_p, new_v_p, new_pool_p, new_k_s, new_v_s, new_pool_s)
```

```python
import functools

import jax
import jax.numpy as jnp
import numpy as np
from jax import lax
from jax.experimental import pallas as pl
from jax.experimental.pallas import tpu as pltpu

F32 = jnp.float32
BF16 = jnp.bfloat16

D_MODEL = 1024
N_META = 16
N_HEADS = 8
HEAD_DIM = 64
N_KV_HEADS = 2
GQA_GROUP = N_HEADS // N_KV_HEADS
ATTN_WIDTH = N_HEADS * HEAD_DIM
KV_WIDTH = N_KV_HEADS * HEAD_DIM
WINDOW = 128
POOL_WIDTH = D_MODEL - ATTN_WIDTH
POOL_WINDOWS = (2, 4, 8, 16)
POOL_GROUP_DIM = POOL_WIDTH // len(POOL_WINDOWS)
POOL_STATE = max(POOL_WINDOWS) - 1
N_EXPERTS = 32
TOP_K = 4
D_EXPERT = D_MODEL
SWIGLU_ALPHA = 1.702
SWIGLU_LIMIT = 7.0
NORM_EPS = 1e-5
PAST_LEN = 16384

LANES = 128
QSUB = 64
KEYS_SUB = QSUB + WINDOW
META_PAD = 64
NKEY = META_PAD + KEYS_SUB
MASKED = -1e30
PROMPT_BLOCK = 1024
ROUTE_BLOCK = 384
EXPERT_ROWS = 512
DISPATCH_TILE = 384
PACK_CHUNKS = D_MODEL // 2 // LANES
COMBINE_TILE = 128
WIN = 32
WIN_ALIGN = 16
WIN_ROWS = WIN + WIN_ALIGN
WIN_HEAD = 32
VMEM_LIMIT = 56 * 1024 * 1024


def _rms(x, g):
    return x * lax.rsqrt(jnp.mean(x * x, axis=-1, keepdims=True) + NORM_EPS) * g


def _router_logits(rwt_ref, h2, h2_hi):
    nt = (((1,), (1,)), ((), ()))
    h2_lo = (h2 - h2_hi.astype(F32)).astype(BF16)
    return (lax.dot_general(rwt_ref[0], h2_hi, nt, preferred_element_type=F32)
            + lax.dot_general(rwt_ref[0], h2_lo, nt, preferred_element_type=F32)
            + lax.dot_general(rwt_ref[1], h2_hi, nt, preferred_element_type=F32))


def _dup_halves(a):
    lane = lax.broadcasted_iota(jnp.int32, a.shape, 1)
    r = pltpu.roll(a, HEAD_DIM, axis=1)
    lo = lane < HEAD_DIM
    return jnp.where(lo, a, r), jnp.where(lo, r, a)


def _pool_means(pext_ref, n):
    outs = []
    for gi, w in enumerate(POOL_WINDOWS):
        xg = pext_ref[:, gi * POOL_GROUP_DIM:(gi + 1) * POOL_GROUP_DIM]
        s = xg
        sh = 1
        while sh < w:
            s = s + pltpu.roll(s, sh, axis=0)
            sh *= 2
        outs.append(s[16:] * (1.0 / w) - xg[16:])
    return outs


def _pack_exact_bf16_pairs(h):
    m = h.shape[1] // 2
    return (lax.shift_right_logical(pltpu.bitcast(h[:, :m], jnp.uint32), jnp.uint32(16))
            | (pltpu.bitcast(h[:, m:], jnp.uint32) & jnp.uint32(0xFFFF0000)))


def _unpack_bf16_pairs(w):
    lo = pltpu.bitcast(lax.shift_left(w, jnp.uint32(16)), F32).astype(BF16)
    hi = pltpu.bitcast(w & jnp.uint32(0xFFFF0000), F32).astype(BF16)
    return lo, hi


def _prompt_kernel(x_ref, meta_ref, gattn_ref, win_ref, wpool_ref, pscale_ref, wout_ref, gffn_ref,
                   rwt_ref, rb_ref, sink_ref, tbl_ref, tail_h2_ref, tail_lgt_ref,
                   x1_ref, h2_ref, lgt_ref, kmeta_ref, vmeta_ref, ktail_ref, vtail_ref, ptail_ref,
                   k2buf, v2buf, km2, vm2, qbuf, obuf, pext):
    pid = pl.program_id(0)
    n_main = pl.num_programs(0) - 1
    refs = (x_ref, meta_ref, gattn_ref, win_ref, wpool_ref, pscale_ref, wout_ref, gffn_ref,
            rwt_ref, rb_ref, sink_ref, tbl_ref,
            x1_ref, h2_ref, lgt_ref, kmeta_ref, vmeta_ref, ktail_ref, vtail_ref, ptail_ref,
            k2buf, v2buf, km2, vm2, qbuf, obuf, pext)

    @pl.when(pid < n_main)
    def _():
        _prompt_block(*refs)

    @pl.when(pid == n_main)
    def _():
        h2_ref[0:tail_h2_ref.shape[0], :] = tail_h2_ref[...]
        lgt_ref[:, 0:tail_lgt_ref.shape[1]] = tail_lgt_ref[...]


def _prompt_block(x_ref, meta_ref, gattn_ref, win_ref, wpool_ref, pscale_ref, wout_ref, gffn_ref,
                  rwt_ref, rb_ref, sink_ref, tbl_ref,
                  x1_ref, h2_ref, lgt_ref, kmeta_ref, vmeta_ref, ktail_ref, vtail_ref, ptail_ref,
                  k2buf, v2buf, km2, vm2, qbuf, obuf, pext):
    tb = x_ref.shape[0]
    pid = pl.program_id(0)

    @pl.when(pid == 0)
    def _():
        hm = _rms(meta_ref[...], gattn_ref[...]).astype(BF16)
        km = jnp.dot(hm, win_ref[:, ATTN_WIDTH:ATTN_WIDTH + KV_WIDTH], preferred_element_type=F32)
        vm = jnp.dot(hm, win_ref[:, ATTN_WIDTH + KV_WIDTH:ATTN_WIDTH + 2 * KV_WIDTH],
                     preferred_element_type=F32)
        pm = jnp.dot(hm, win_ref[:, ATTN_WIDTH + 2 * KV_WIDTH:], preferred_element_type=F32)
        kmeta_ref[...] = km
        vmeta_ref[...] = vm
        zpad = jnp.zeros((META_PAD - N_META, LANES), F32)
        k0, k1 = _dup_halves(jnp.concatenate([km, zpad], axis=0))
        v0, v1 = _dup_halves(jnp.concatenate([vm, zpad], axis=0))
        km2[0] = k0.astype(BF16)
        km2[1] = k1.astype(BF16)
        vm2[0, :, 0:LANES] = v0.astype(BF16)
        vm2[1, :, 0:LANES] = v1.astype(BF16)
        vm2[:, :, LANES:] = jnp.ones((N_KV_HEADS, META_PAD, LANES), BF16)
        k2buf[:, 0:WINDOW, :] = jnp.zeros((2, WINDOW, LANES), BF16)
        v2buf[:, 0:WINDOW, 0:LANES] = jnp.zeros((2, WINDOW, LANES), BF16)
        v2buf[:, :, LANES:] = jnp.ones((N_KV_HEADS, WINDOW + tb, LANES), BF16)
        pext[0:16, :] = pm

    h = _rms(x_ref[...], gattn_ref[...]).astype(BF16)
    q = jnp.dot(h, win_ref[:, 0:ATTN_WIDTH], preferred_element_type=F32) * (HEAD_DIM ** -0.5)
    lane_t = lax.broadcasted_iota(jnp.int32, (tb, LANES), 1)
    for c in range(N_HEADS // 2):
        tile = q[:, c * LANES:(c + 1) * LANES]
        for a in range(2):
            keep = (lane_t < HEAD_DIM) if a == 0 else (lane_t >= HEAD_DIM)
            piece = jnp.where(keep, tile, 0.0).astype(BF16).reshape(tb // QSUB, QSUB, LANES)
            row = ((c % 2) * 2 + a) * QSUB
            qbuf[c // 2, :, row:row + QSUB, :] = piece
    k = jnp.dot(h, win_ref[:, ATTN_WIDTH:ATTN_WIDTH + KV_WIDTH], preferred_element_type=F32)
    v = jnp.dot(h, win_ref[:, ATTN_WIDTH + KV_WIDTH:ATTN_WIDTH + 2 * KV_WIDTH], preferred_element_type=F32)
    p = jnp.dot(h, win_ref[:, ATTN_WIDTH + 2 * KV_WIDTH:], preferred_element_type=F32)
    ktail_ref[...] = k[tb - WINDOW:]
    vtail_ref[...] = v[tb - WINDOW:]
    ptail_ref[...] = p[tb - 16:]
    k0, k1 = _dup_halves(k)
    v0, v1 = _dup_halves(v)
    k2buf[0, WINDOW:, :] = k0.astype(BF16)
    k2buf[1, WINDOW:, :] = k1.astype(BF16)
    v2buf[0, WINDOW:, 0:LANES] = v0.astype(BF16)
    v2buf[1, WINDOW:, 0:LANES] = v1.astype(BF16)
    pext[16:, :] = p

    lane_q = lax.broadcasted_iota(jnp.int32, (QSUB, LANES), 1)
    lo_q = lane_q < HEAD_DIM

    for u in range(tb // QSUB):
        r0 = u * QSUB
        sel = jnp.where(pid == 0, u + 1, 0) if u < WINDOW // QSUB else 0
        for g in range(N_KV_HEADS):
            qm = qbuf[g, u]
            kwin = jnp.concatenate([km2[g], k2buf[g, r0:r0 + KEYS_SUB, :]], axis=0)
            vwin = jnp.concatenate([vm2[g], v2buf[g, r0:r0 + KEYS_SUB, :]], axis=0)
            s = lax.dot_general(qm, kwin, (((1,), (1,)), ((), ())), preferred_element_type=F32)
            s = s + tbl_ref[sel, g]
            sink = sink_ref[g]
            m = jnp.maximum(jnp.max(s, axis=1, keepdims=True), sink)
            e = jnp.exp(s - m).astype(BF16)
            r = jnp.dot(e, vwin, preferred_element_type=F32)
            o = r[:, 0:LANES] / (r[:, LANES:] + jnp.exp(sink - m))
            o0 = jnp.where(lo_q, o[0:QSUB], o[QSUB:2 * QSUB])
            o1 = jnp.where(lo_q, o[2 * QSUB:3 * QSUB], o[3 * QSUB:])
            obuf[r0:r0 + QSUB, (2 * g) * LANES:(2 * g + 1) * LANES] = o0.astype(BF16)
            obuf[r0:r0 + QSUB, (2 * g + 1) * LANES:(2 * g + 2) * LANES] = o1.astype(BF16)

    pooled = _pool_means(pext, tb)
    for gi in range(len(POOL_WINDOWS)):
        y = jnp.dot(pooled[gi].astype(BF16), wpool_ref[gi], preferred_element_type=F32)
        y = y * pscale_ref[:, gi * POOL_GROUP_DIM:(gi + 1) * POOL_GROUP_DIM]
        obuf[:, ATTN_WIDTH + gi * POOL_GROUP_DIM:ATTN_WIDTH + (gi + 1) * POOL_GROUP_DIM] = y.astype(BF16)

    k2buf[:, 0:WINDOW, :] = k2buf[:, tb:tb + WINDOW, :]
    v2buf[:, 0:WINDOW, 0:LANES] = v2buf[:, tb:tb + WINDOW, 0:LANES]
    pext[0:16, :] = pext[tb:tb + 16, :]

    x1 = x_ref[...] + jnp.dot(obuf[...], wout_ref[...], preferred_element_type=F32)
    x1_ref[...] = x1
    h2 = _rms(x1, gffn_ref[...])
    h2_hi = h2.astype(BF16)
    h2_ref[...] = h2_hi
    lgt_ref[...] = _router_logits(rwt_ref, h2, h2_hi) + rb_ref[...]


def _attn_tables(sinks):
    i = np.arange(QSUB)[:, None]
    j = np.arange(NKEY)[None, :]
    jb = j - META_PAD
    rel = i + WINDOW - jb
    band_ok = (jb >= 0) & (rel >= 0) & (rel <= WINDOW)
    meta_ok = (j < N_META) & (i >= 0)
    slopes = np.exp2(-8.0 * np.arange(1, N_HEADS + 1) / N_HEADS)
    tbl = np.empty((3, N_KV_HEADS, GQA_GROUP * QSUB, NKEY), np.float32)
    for var in range(3):
        ok = band_ok if var == 0 else band_ok & (jb >= WINDOW - (var - 1) * QSUB)
        for g in range(N_KV_HEADS):
            for a in range(GQA_GROUP):
                hd = g * GQA_GROUP + a
                bias = np.where(ok, -slopes[hd] * rel, MASKED)
                bias = np.where(meta_ok, 0.0, bias)
                tbl[var, g, a * QSUB:(a + 1) * QSUB] = bias
    sink_col = jnp.repeat(sinks.astype(F32).reshape(N_KV_HEADS, GQA_GROUP, 1), QSUB, axis=2)
    return jnp.asarray(tbl), sink_col.reshape(N_KV_HEADS, GQA_GROUP * QSUB, 1)


def _prompt_mixer(x, meta, gattn, win, wpool, pscale, wout, gffn, rwt, rb, sinks, tail_h2, tail_lgt):
    seq = x.shape[0]
    tb = PROMPT_BLOCK
    n_tail = tail_h2.shape[0]
    assert seq % tb == 0 and tb % WINDOW == 0 and n_tail <= tb
    nblk = seq // tb
    n_tok = seq + n_tail
    tbl, sink_col = _attn_tables(sinks)
    full = lambda *shape: pl.BlockSpec(shape, lambda i: (0,) * len(shape))
    main = lambda i: (jnp.minimum(i, nblk - 1), 0)
    in_width = win.shape[1]
    return pl.pallas_call(
        _prompt_kernel,
        grid=(nblk + 1,),
        in_specs=[
            pl.BlockSpec((tb, D_MODEL), main),
            full(N_META, D_MODEL), full(1, D_MODEL), full(D_MODEL, in_width),
            full(len(POOL_WINDOWS), POOL_GROUP_DIM, POOL_GROUP_DIM), full(1, POOL_WIDTH),
            full(D_MODEL, D_MODEL), full(1, D_MODEL), full(2, N_EXPERTS, D_MODEL), full(N_EXPERTS, 1),
            full(N_KV_HEADS, GQA_GROUP * QSUB, 1), full(3, N_KV_HEADS, GQA_GROUP * QSUB, NKEY),
            full(n_tail, D_MODEL), full(N_EXPERTS, n_tail),
        ],
        out_specs=[
            pl.BlockSpec((tb, D_MODEL), main),
            pl.BlockSpec((tb, D_MODEL), lambda i: (i, 0)),
            pl.BlockSpec((N_EXPERTS, tb), lambda i: (0, i)),
            full(N_META, KV_WIDTH), full(N_META, KV_WIDTH),
            full(WINDOW, KV_WIDTH), full(WINDOW, KV_WIDTH), full(16, POOL_WIDTH),
        ],
        out_shape=[
            jax.ShapeDtypeStruct((seq, D_MODEL), F32),
            jax.ShapeDtypeStruct((n_tok, D_MODEL), BF16),
            jax.ShapeDtypeStruct((N_EXPERTS, n_tok), F32),
            jax.ShapeDtypeStruct((N_META, KV_WIDTH), F32),
            jax.ShapeDtypeStruct((N_META, KV_WIDTH), F32),
            jax.ShapeDtypeStruct((WINDOW, KV_WIDTH), F32),
            jax.ShapeDtypeStruct((WINDOW, KV_WIDTH), F32),
            jax.ShapeDtypeStruct((16, POOL_WIDTH), F32),
        ],
        scratch_shapes=[
            pltpu.VMEM((N_KV_HEADS, WINDOW + tb, LANES), BF16),
            pltpu.VMEM((N_KV_HEADS, WINDOW + tb, 2 * LANES), BF16),
            pltpu.VMEM((N_KV_HEADS, META_PAD, LANES), BF16),
            pltpu.VMEM((N_KV_HEADS, META_PAD, 2 * LANES), BF16),
            pltpu.VMEM((N_KV_HEADS, tb // QSUB, GQA_GROUP * QSUB, LANES), BF16),
            pltpu.VMEM((tb, D_MODEL), BF16),
            pltpu.VMEM((16 + tb, POOL_WIDTH), F32),
        ],
        compiler_params=pltpu.CompilerParams(dimension_semantics=("arbitrary",),
                                             vmem_limit_bytes=VMEM_LIMIT),
        name="prompt_mixer",
    )(x, meta, gattn, win, wpool, pscale, wout, gffn, rwt, rb, sink_col, tbl, tail_h2, tail_lgt)


def _sample_kernel(x_ref, ck_ref, cv_ref, sp_ref, gattn_ref, win_ref, wqkv_t_ref, wpool_ref, pscale_ref,
                   wout_ref, gffn_ref, rwt_ref, rb_ref, sink_ref, bias_ref,
                   x1_ref, h2_ref, lgt_ref, kout_ref, vout_ref, pnew_ref,
                   qt_ref, sc_ref, ot_ref, obuf):
    nb = x_ref.shape[0]
    rows = ck_ref.shape[0]
    x = x_ref[...]
    h = _rms(x, gattn_ref[...]).astype(BF16)
    nt = (((1,), (1,)), ((), ()))
    qkv_t = lax.dot_general(wqkv_t_ref[...], h, nt, preferred_element_type=F32)
    qt_ref[...] = qkv_t[0:ATTN_WIDTH] * (HEAD_DIM ** -0.5)
    kt = qkv_t[ATTN_WIDTH:ATTN_WIDTH + KV_WIDTH]
    vt = qkv_t[ATTN_WIDTH + KV_WIDTH:]
    p = jnp.dot(h, win_ref[:, ATTN_WIDTH + 2 * KV_WIDTH:], preferred_element_type=F32)
    pnew_ref[0:POOL_STATE - 1] = sp_ref[1:POOL_STATE]
    pnew_ref[POOL_STATE - 1] = p

    kout_ref[0:N_META] = ck_ref[0:N_META]
    vout_ref[0:N_META] = cv_ref[0:N_META]
    kout_ref[N_META:rows - 1] = ck_ref[N_META + 1:rows]
    vout_ref[N_META:rows - 1] = cv_ref[N_META + 1:rows]
    kout_ref[rows - 1] = kt
    vout_ref[rows - 1] = vt

    def kv_rows(hd):
        g = hd // GQA_GROUP
        return slice(g * HEAD_DIM, (g + 1) * HEAD_DIM)

    def score_row(key_tile, hd):
        prod = qt_ref[hd * HEAD_DIM:(hd + 1) * HEAD_DIM, :] * key_tile[kv_rows(hd), :]
        return jnp.sum(prod, axis=0, keepdims=True)

    def score_pass(s, carry):
        key_tile = ck_ref[s]
        for hd in range(N_HEADS):
            sc_ref[hd, pl.ds(s, 1), :] = score_row(key_tile, hd)
        return carry

    lax.fori_loop(0, rows, score_pass, 0)
    pad_rows = sc_ref.shape[1] - rows - 1
    for hd in range(N_HEADS):
        sc_ref[hd, rows:rows + 1, :] = score_row(kt, hd)
        sc_ref[hd, rows + 1:, :] = jnp.full((pad_rows, nb), MASKED, F32)

    for hd in range(N_HEADS):
        s = sc_ref[hd] + bias_ref[hd]
        sink = sink_ref[hd]
        m = jnp.maximum(jnp.max(s, axis=0, keepdims=True), sink)
        e = jnp.exp(s - m)
        sc_ref[hd] = e * (1.0 / (jnp.sum(e, axis=0, keepdims=True) + jnp.exp(sink - m)))

    for half in range(2):
        heads = range(half * (N_HEADS // 2), (half + 1) * (N_HEADS // 2))

        def value_pass(s, accs, heads=heads):
            val_tile = cv_ref[s]
            return tuple(acc + sc_ref[hd, pl.ds(s, 1), :] * val_tile[kv_rows(hd), :]
                         for acc, hd in zip(accs, heads))

        accs = lax.fori_loop(0, rows, value_pass,
                             tuple(jnp.zeros((HEAD_DIM, nb), F32) for _ in heads))
        for acc, hd in zip(accs, heads):
            ot_ref[hd * HEAD_DIM:(hd + 1) * HEAD_DIM, :] = acc + sc_ref[hd, rows:rows + 1, :] * vt[kv_rows(hd), :]

    for c in range(ATTN_WIDTH // LANES):
        obuf[:, c * LANES:(c + 1) * LANES] = ot_ref[c * LANES:(c + 1) * LANES, :].T.astype(BF16)

    for gi, w in enumerate(POOL_WINDOWS):
        cols = slice(gi * POOL_GROUP_DIM, (gi + 1) * POOL_GROUP_DIM)
        pg = p[:, cols]
        acc = pg
        for d in range(1, w):
            acc = acc + sp_ref[POOL_STATE - d, :, cols]
        pooled = acc * (1.0 / w) - pg
        y = jnp.dot(pooled.astype(BF16), wpool_ref[gi], preferred_element_type=F32) * pscale_ref[:, cols]
        obuf[:, ATTN_WIDTH + gi * POOL_GROUP_DIM:ATTN_WIDTH + (gi + 1) * POOL_GROUP_DIM] = y.astype(BF16)

    x1 = x + jnp.dot(obuf[...], wout_ref[...], preferred_element_type=F32)
    x1_ref[...] = x1
    h2 = _rms(x1, gffn_ref[...])
    h2_hi = h2.astype(BF16)
    h2_ref[...] = h2_hi
    lgt_ref[...] = _router_logits(rwt_ref, h2, h2_hi) + rb_ref[...]


def _sample_mixer(x, ck_t, cv_t, sp, gattn, win, wpool, pscale, wout, gffn, rwt, rb, sinks):
    nb = x.shape[0]
    rows = ck_t.shape[0]
    assert nb == LANES and rows == N_META + WINDOW
    n_keys = -(-(rows + 1) // 8) * 8
    slopes = np.exp2(-8.0 * np.arange(1, N_HEADS + 1) / N_HEADS)
    dist = np.concatenate([np.zeros(N_META), WINDOW - np.arange(WINDOW), np.zeros(1)])
    bias = np.full((N_HEADS, n_keys, 1), MASKED, np.float32)
    bias[:, :rows + 1, 0] = -slopes[:, None] * dist[None, :]
    vm = pl.BlockSpec(memory_space=pltpu.VMEM)
    return pl.pallas_call(
        _sample_kernel,
        in_specs=[vm] * 15,
        out_specs=[vm] * 6,
        out_shape=[
            jax.ShapeDtypeStruct((nb, D_MODEL), F32),
            jax.ShapeDtypeStruct((nb, D_MODEL), BF16),
            jax.ShapeDtypeStruct((N_EXPERTS, nb), F32),
            jax.ShapeDtypeStruct(ck_t.shape, F32),
            jax.ShapeDtypeStruct(cv_t.shape, F32),
            jax.ShapeDtypeStruct(sp.shape, F32),
        ],
        scratch_shapes=[
            pltpu.VMEM((ATTN_WIDTH, nb), F32),
            pltpu.VMEM((N_HEADS, n_keys, nb), F32),
            pltpu.VMEM((ATTN_WIDTH, nb), F32),
            pltpu.VMEM((nb, D_MODEL), BF16),
        ],
        compiler_params=pltpu.CompilerParams(vmem_limit_bytes=VMEM_LIMIT),
        name="sample_mixer",
    )(x, ck_t, cv_t, sp, gattn, win, win[:, 0:ATTN_WIDTH + 2 * KV_WIDTH].T, wpool, pscale, wout, gffn, rwt, rb,
      sinks.astype(F32).reshape(N_HEADS, 1, 1), jnp.asarray(bias))


def _router_kernel(lg_ref, tri_ref, low_ref, gate_ref, col_ref, lrank_ref, lpos_ref,
                   tcar_ref, cnt_ref, carry):
    tr = lg_ref.shape[1]

    @pl.when(pl.program_id(0) == 0)
    def _():
        carry[...] = jnp.zeros_like(carry)

    work = lg_ref[...]
    eio = lax.broadcasted_iota(jnp.int32, work.shape, 0).astype(F32)
    sels, vals, idxs = [], [], []
    for _k in range(TOP_K):
        mx = jnp.max(work, axis=0, keepdims=True)
        idx = jnp.min(jnp.where(work == mx, eio, float(N_EXPERTS)), axis=0, keepdims=True)
        sel = eio == idx
        sels.append(sel)
        vals.append(mx)
        idxs.append(idx)
        work = jnp.where(sel, -jnp.inf, work)
    exps = [jnp.exp(vk - vals[0]) for vk in vals]
    tot = exps[0] + exps[1] + exps[2] + exps[3]
    onehot = jnp.zeros(work.shape, F32)
    for sel in sels:
        onehot = onehot + sel.astype(F32)
    before = jnp.dot(onehot.astype(BF16), tri_ref[...], preferred_element_type=F32) + carry[...]
    for kk in range(TOP_K):
        gate_ref[pl.ds(kk, 1), :] = exps[kk] / tot
    for j in range(tr // COMBINE_TILE):
        cols = slice(j * COMBINE_TILE, (j + 1) * COMBINE_TILE)
        tc = before[:, j * COMBINE_TILE:j * COMBINE_TILE + 1]
        tcar_ref[j] = tc.astype(jnp.int32)
        slack = tc - WIN_ALIGN * jnp.floor(tc * (1.0 / WIN_ALIGN))
        local = before[:, cols] - tc
        for kk in range(TOP_K):
            selk = sels[kk][:, cols]
            lr = jnp.sum(jnp.where(selk, local, 0.0), axis=0, keepdims=True)
            sl = jnp.sum(jnp.where(selk, slack, 0.0), axis=0, keepdims=True)
            lrank_ref[pl.ds(kk, 1), cols] = lr.astype(jnp.int32)
            col_ref[pl.ds(kk, 1), cols] = (idxs[kk][:, cols] * float(WIN_ROWS) + sl + lr).astype(jnp.int32)
    for j in range(tr // DISPATCH_TILE):
        cols = slice(j * DISPATCH_TILE, (j + 1) * DISPATCH_TILE)
        local = before[:, cols] - before[:, j * DISPATCH_TILE:j * DISPATCH_TILE + 1]
        tile_cnt = jnp.broadcast_to(jnp.sum(onehot[:, cols], axis=1, keepdims=True), local.shape)
        cnt_hi = jnp.floor(tile_cnt * (1.0 / 256.0))
        cnt_lo = tile_cnt - 256.0 * cnt_hi
        run_start = (256.0 * jnp.dot(low_ref[...], cnt_hi.astype(BF16), preferred_element_type=F32)
                     + jnp.dot(low_ref[...], cnt_lo.astype(BF16), preferred_element_type=F32))
        for kk in range(TOP_K):
            lp = jnp.sum(jnp.where(sels[kk][:, cols], run_start + local, 0.0), axis=0, keepdims=True)
            lpos_ref[pl.ds(kk, 1), cols] = lp.astype(jnp.int32)
    carry[...] = carry[...] + jnp.sum(onehot, axis=1, keepdims=True)
    cnt_ref[...] = carry[...].astype(jnp.int32)


def _router(logits_t):
    n = logits_t.shape[1]
    tr = ROUTE_BLOCK
    assert n % tr == 0
    tri = jnp.asarray(np.triu(np.ones((tr, tr), np.float32), k=1), BF16)
    low = jnp.asarray(np.tril(np.ones((N_EXPERTS, N_EXPERTS), np.float32), k=-1), BF16)
    per_tok = pl.BlockSpec((TOP_K, tr), lambda i: (0, i))
    return pl.pallas_call(
        _router_kernel,
        grid=(n // tr,),
        in_specs=[pl.BlockSpec((N_EXPERTS, tr), lambda i: (0, i)),
                  pl.BlockSpec((tr, tr), lambda i: (0, 0)),
                  pl.BlockSpec((N_EXPERTS, N_EXPERTS), lambda i: (0, 0))],
        out_specs=[per_tok, per_tok, per_tok, per_tok,
                   pl.BlockSpec((tr // COMBINE_TILE, N_EXPERTS, 1), lambda i: (i, 0, 0)),
                   pl.BlockSpec((N_EXPERTS, 1), lambda i: (0, 0))],
        out_shape=[jax.ShapeDtypeStruct((TOP_K, n), F32),
                   jax.ShapeDtypeStruct((TOP_K, n), jnp.int32),
                   jax.ShapeDtypeStruct((TOP_K, n), jnp.int32),
                   jax.ShapeDtypeStruct((TOP_K, n), jnp.int32),
                   jax.ShapeDtypeStruct((n // COMBINE_TILE, N_EXPERTS, 1), jnp.int32),
                   jax.ShapeDtypeStruct((N_EXPERTS, 1), jnp.int32)],
        scratch_shapes=[pltpu.VMEM((N_EXPERTS, 1), F32)],
        compiler_params=pltpu.CompilerParams(dimension_semantics=("arbitrary",)),
        name="router",
    )(logits_t, tri, low)


def _dispatch_kernel(lstart_ref, cnt_ref, dst_ref, h2_ref, lpos_ref, xs_hbm, stg0, stg1, sem):
    i = pl.program_id(0)
    n_tiles = pl.num_programs(0)
    dt = h2_ref.shape[0]
    rows = dt * TOP_K
    slot = i % 2
    stgs = (stg0, stg1)
    pieces = [p for p in (256, 128, 64, 32, 16, 8, 4, 2, 1) if p <= dt]
    assert dt < 512

    def drain(s):
        pltpu.make_async_copy(stgs[s], xs_hbm.at[pl.ds(0, rows)], sem.at[s]).wait()

    def issue_runs(tile, live, s):
        for e in range(N_EXPERTS):
            n = jnp.where(live, cnt_ref[tile * N_EXPERTS + e], 0)
            src0 = lstart_ref[tile * N_EXPERTS + e]
            dst0 = dst_ref[tile * N_EXPERTS + e]
            for piece in pieces:
                off = n & ~jnp.int32(2 * piece - 1)

                @pl.when((n & piece) != 0)
                def _(off=off, piece=piece, src0=src0, dst0=dst0):
                    pltpu.make_async_copy(stgs[s].at[pl.ds(src0 + off, piece)],
                                          xs_hbm.at[pl.ds(dst0 + off, piece)], sem.at[s]).start()

    def sort_tile(s):
        rid = lax.broadcasted_iota(jnp.int32, (rows, dt), 0)
        hit = rid == lpos_ref[0:1, :]
        for kk in range(1, TOP_K):
            hit = jnp.logical_or(hit, rid == lpos_ref[kk:kk + 1, :])
        perm = jnp.where(hit, 1.0, 0.0).astype(BF16)
        srt = jnp.dot(perm, h2_ref[...], preferred_element_type=F32)
        packed = _pack_exact_bf16_pairs(srt)
        stgs[s][...] = packed.reshape(rows, PACK_CHUNKS, LANES)

    for s in range(2):
        @pl.when(slot == s)
        def _(s=s):
            @pl.when(i >= 2)
            def _():
                drain(s)

            issue_runs(jnp.maximum(i - 1, 0), i >= 1, 1 - s)
            sort_tile(s)

            @pl.when(i == n_tiles - 1)
            def _():
                issue_runs(i, True, s)
                drain(s)

                @pl.when(n_tiles >= 2)
                def _():
                    drain(1 - s)


def _dispatch(h2, lpos, lstart, cnt, dst, cap):
    n_tok = h2.shape[0]
    dt = DISPATCH_TILE
    assert n_tok % dt == 0
    grid_spec = pltpu.PrefetchScalarGridSpec(
        num_scalar_prefetch=3,
        grid=(n_tok // dt,),
        in_specs=[pl.BlockSpec((dt, D_MODEL), lambda i, a, b, c: (i, 0)),
                  pl.BlockSpec((TOP_K, dt), lambda i, a, b, c: (0, i)),
                  ],
        out_specs=pl.BlockSpec(memory_space=pl.ANY),
        scratch_shapes=[pltpu.VMEM((dt * TOP_K, PACK_CHUNKS, LANES), jnp.uint32),
                        pltpu.VMEM((dt * TOP_K, PACK_CHUNKS, LANES), jnp.uint32),
                        pltpu.SemaphoreType.DMA((2,))],
    )
    return pl.pallas_call(
        _dispatch_kernel,
        grid_spec=grid_spec,
        out_shape=jax.ShapeDtypeStruct((cap, PACK_CHUNKS, LANES), jnp.uint32),
        compiler_params=pltpu.CompilerParams(dimension_semantics=("arbitrary",),
                                             vmem_limit_bytes=VMEM_LIMIT),
        name="dispatch",
    )(lstart, cnt, dst, h2, lpos)


def _expert_kernel(n_xblocks, bexp_ref, nvalid_ref, epos_ref, elist_ref,
                   x_hbm, wgu_hbm, bgu_ref, wd_hbm, bd_ref, y_ref,
                   wgu_f32, wd_f32, wgu_bf, wd_bf, xbuf, xsem, wsem):
    i = pl.program_id(0)
    rb = y_ref.shape[0]
    nvalid = nvalid_ref[i]
    pos = epos_ref[i]
    fresh = jnp.logical_or(i == 0, pos != epos_ref[jnp.maximum(i - 1, 0)])
    slot = i % 2

    def x_copies(blk, s):
        return [pltpu.make_async_copy(x_hbm.at[pl.ds(blk * rb, rb), c, :],
                                      xbuf.at[s, :, pl.ds(c * LANES, LANES)], xsem.at[s])
                for c in range(PACK_CHUNKS)]

    def w_copies(p):
        e = elist_ref[p]
        s = p % 2
        return [pltpu.make_async_copy(wgu_hbm.at[e], wgu_f32.at[s], wsem.at[s, 0]),
                pltpu.make_async_copy(wd_hbm.at[e], wd_f32.at[s], wsem.at[s, 1])]

    @pl.when(i == 0)
    def _():
        for cp in x_copies(0, 0):
            cp.start()

        @pl.when(nvalid > 0)
        def _():
            for cp in w_copies(0):
                cp.start()

    @pl.when(i + 1 < n_xblocks)
    def _():
        for cp in x_copies(i + 1, 1 - slot):
            cp.start()

    @pl.when(jnp.logical_and(fresh, nvalid > 0))
    def _():
        @pl.when(elist_ref[pos + 1] >= 0)
        def _():
            for cp in w_copies(pos + 1):
                cp.start()

        for cp in w_copies(pos):
            cp.wait()
        ws = pos % 2
        chunk = 32

        def cast_gu(r, c):
            r0 = pl.multiple_of(r * chunk, chunk)
            wgu_bf[pl.ds(r0, chunk), :] = wgu_f32[ws, pl.ds(r0, chunk), :].astype(BF16)
            return c

        def cast_d(r, c):
            r0 = pl.multiple_of(r * chunk, chunk)
            wd_bf[pl.ds(r0, chunk), :] = wd_f32[ws, pl.ds(r0, chunk), :].astype(BF16)
            return c

        lax.fori_loop(0, D_MODEL // chunk, cast_gu, 0)
        lax.fori_loop(0, D_EXPERT // chunk, cast_d, 0)

    @pl.when(i < n_xblocks)
    def _():
        for cp in x_copies(i, slot):
            cp.wait()

    def ffn(rows):
        xw = xbuf[slot, 0:rows, :]
        xw = jnp.where(lax.broadcasted_iota(jnp.int32, xw.shape, 0) < nvalid, xw, jnp.uint32(0))
        xlo, xhi = _unpack_bf16_pairs(xw)
        x = jnp.concatenate([xlo, xhi], axis=1)
        g = jnp.dot(x, wgu_bf[:, 0:D_EXPERT], preferred_element_type=F32) + bgu_ref[0, :, 0:D_EXPERT]
        u = jnp.dot(x, wgu_bf[:, D_EXPERT:], preferred_element_type=F32) + bgu_ref[0, :, D_EXPERT:]
        g = jnp.minimum(g, SWIGLU_LIMIT)
        u = jnp.clip(u, -SWIGLU_LIMIT, SWIGLU_LIMIT)
        act = g * (1.0 / (1.0 + jnp.exp(-SWIGLU_ALPHA * g))) * (u + 1.0)
        y = jnp.dot(act.astype(BF16), wd_bf[...], preferred_element_type=F32) + bd_ref[0]
        row = lax.broadcasted_iota(jnp.int32, y.shape, 0)
        y_ref[0:rows, :] = jnp.where(row < nvalid, y, 0.0).astype(BF16)
        if rows < rb:
            y_ref[rows:, :] = jnp.zeros((rb - rows, D_MODEL), BF16)

    @pl.when(nvalid > rb // 2)
    def _():
        ffn(rb)

    @pl.when(jnp.logical_and(nvalid > 0, nvalid <= rb // 2))
    def _():
        ffn(rb // 2)

    @pl.when(nvalid == 0)
    def _():
        y_ref[...] = jnp.zeros_like(y_ref)


def _experts(xs, block_expert, nvalid, block_pos, expert_list, wgu, bgu, wd, bd):
    rb = EXPERT_ROWS
    n_xblocks = xs.shape[0] // rb
    nblk = n_xblocks + 1
    any_space = pl.BlockSpec(memory_space=pl.ANY)
    grid_spec = pltpu.PrefetchScalarGridSpec(
        num_scalar_prefetch=4,
        grid=(nblk,),
        in_specs=[
            any_space,
            any_space,
            pl.BlockSpec((1, 1, 2 * D_EXPERT), lambda i, be, nu, ep, el: (be[i], 0, 0)),
            any_space,
            pl.BlockSpec((1, 1, D_MODEL), lambda i, be, nu, ep, el: (be[i], 0, 0)),
        ],
        out_specs=pl.BlockSpec((rb, D_MODEL), lambda i, be, nu, ep, el: (i, 0)),
        scratch_shapes=[pltpu.VMEM((2, D_MODEL, 2 * D_EXPERT), F32),
                        pltpu.VMEM((2, D_EXPERT, D_MODEL), F32),
                        pltpu.VMEM((D_MODEL, 2 * D_EXPERT), BF16),
                        pltpu.VMEM((D_EXPERT, D_MODEL), BF16),
                        pltpu.VMEM((2, rb, D_MODEL // 2), jnp.uint32),
                        pltpu.SemaphoreType.DMA((2,)),
                        pltpu.SemaphoreType.DMA((2, 2))],
    )
    return pl.pallas_call(
        functools.partial(_expert_kernel, n_xblocks),
        grid_spec=grid_spec,
        out_shape=jax.ShapeDtypeStruct((nblk * rb, D_MODEL), BF16),
        compiler_params=pltpu.CompilerParams(dimension_semantics=("arbitrary",),
                                             vmem_limit_bytes=VMEM_LIMIT),
        name="experts",
    )(block_expert, nvalid, block_pos, expert_list, xs, wgu, bgu.reshape(N_EXPERTS, 1, 2 * D_EXPERT), wd,
      bd.reshape(N_EXPERTS, 1, D_MODEL))


def _combine_kernel(n_prompt_tiles, offa_ref, nchunk_ref, tail_ref, ntail_ref,
                    x1p_ref, x1s_ref, info_ref, info_next_ref, gfin_ref, yb_hbm,
                    outp_ref, outs_ref, ybuf, gbuf0, gbuf1, acc_ref, sem, tsem):
    i = pl.program_id(0)
    n_tiles = pl.num_programs(0)

    def window_copy(tile, chunk, e, slot):
        base = pl.multiple_of(offa_ref[tile * N_EXPERTS + e] + chunk * WIN, WIN_ALIGN)
        return pltpu.make_async_copy(yb_hbm.at[pl.ds(base, WIN_ROWS), :],
                                     ybuf.at[slot, pl.ds(e * WIN_ROWS, WIN_ROWS), :],
                                     sem.at[slot])

    def start_windows(tile, chunk, slot):
        for e in range(N_EXPERTS):
            window_copy(tile, chunk, e, slot).start()

    def wait_windows(slot):
        pltpu.make_async_copy(yb_hbm.at[pl.ds(0, N_EXPERTS * WIN_ROWS), :], ybuf.at[slot],
                              sem.at[slot]).wait()

    def tail_copy(tile, e, slot):
        base = pl.multiple_of(offa_ref[tile * N_EXPERTS + e] + WIN_HEAD, WIN_ALIGN)
        return pltpu.make_async_copy(yb_hbm.at[pl.ds(base, WIN_ROWS - WIN_HEAD), :],
                                     ybuf.at[slot, pl.ds(e * WIN_ROWS + WIN_HEAD, WIN_ROWS - WIN_HEAD), :],
                                     tsem.at[slot])

    def start_first_chunk(tile, slot):
        for e in range(N_EXPERTS):
            base = pl.multiple_of(offa_ref[tile * N_EXPERTS + e], WIN_ALIGN)
            pltpu.make_async_copy(yb_hbm.at[pl.ds(base, WIN_HEAD), :],
                                  ybuf.at[slot, pl.ds(e * WIN_ROWS, WIN_HEAD), :], sem.at[slot]).start()

            @pl.when(tail_ref[tile * N_EXPERTS + e] != 0)
            def _(e=e):
                tail_copy(tile, e, slot).start()

    def wait_first_chunk(tile, slot):
        pltpu.make_async_copy(yb_hbm.at[pl.ds(0, N_EXPERTS * WIN_HEAD), :],
                              ybuf.at[slot, pl.ds(0, N_EXPERTS * WIN_HEAD), :], sem.at[slot]).wait()

        def one_tail(t, c):
            tail_copy(tile, 0, slot).wait()
            return c

        lax.fori_loop(0, ntail_ref[tile], one_tail, 0)

    slot = i % 2

    @pl.when(i == 0)
    def _():
        ybuf[...] = jnp.zeros_like(ybuf)
        start_first_chunk(0, 0)

    lane = lax.broadcasted_iota(jnp.int32, (COMBINE_TILE, N_EXPERTS * WIN_ROWS), 1)

    def gate_matrix(ref, chunk):
        g = jnp.zeros(lane.shape, F32)
        for kk in range(TOP_K):
            lr = ref[:, TOP_K + kk:TOP_K + kk + 1]
            in_chunk = jnp.logical_and(lr >= chunk * WIN, lr < chunk * WIN + WIN)
            colk = jnp.where(in_chunk, ref[:, kk:kk + 1] - chunk * WIN, -1.0).astype(jnp.int32)
            g = jnp.where(lane == colk, ref[:, 2 * TOP_K + kk:2 * TOP_K + kk + 1], g)
        return g.astype(BF16)

    def moe_rows(gm, buf):
        return jnp.dot(gm, ybuf[buf], preferred_element_type=F32)

    @pl.when(i == 0)
    def _():
        gbuf0[...] = gate_matrix(info_ref, 0)

    def main(s):
        start_first_chunk(jnp.minimum(i + 1, n_tiles - 1), 1 - s)
        wait_first_chunk(i, s)
        g_cur, g_nxt = (gbuf0, gbuf1) if s == 0 else (gbuf1, gbuf0)
        acc_ref[...] = moe_rows(g_cur[...], s)
        g_nxt[...] = gate_matrix(info_next_ref, 0)

    for s in range(2):
        @pl.when(slot == s)
        def _(s=s):
            main(s)

    @pl.when(i == n_tiles - 1)
    def _():
        for s in range(2):
            @pl.when(slot == s)
            def _(s=s):
                wait_first_chunk(i, 1 - s)

    def extra_chunk(j, c):
        start_windows(i, j, 2)
        wait_windows(2)
        acc_ref[...] += moe_rows(gate_matrix(info_ref, j), 2)
        return c

    lax.fori_loop(1, nchunk_ref[i], extra_chunk, 0)

    @pl.when(i < n_prompt_tiles)
    def _():
        outp_ref[...] = _rms(x1p_ref[...] + acc_ref[...], gfin_ref[...])

    @pl.when(i >= n_prompt_tiles)
    def _():
        outs_ref[...] = _rms(x1s_ref[...] + acc_ref[...], gfin_ref[...])


def _combine(x1_p, x1_s, tok_info, gfin, yb, offa, nchunk, need_tail, ntail):
    ct = COMBINE_TILE
    n_p, n_s = x1_p.shape[0] // ct, x1_s.shape[0] // ct
    assert x1_p.shape[0] % ct == 0 and x1_s.shape[0] % ct == 0 and n_s >= 1
    n_info = tok_info.shape[1]
    grid_spec = pltpu.PrefetchScalarGridSpec(
        num_scalar_prefetch=4,
        grid=(n_p + n_s,),
        in_specs=[
            pl.BlockSpec((ct, D_MODEL), lambda i, *_: (jnp.minimum(i, n_p - 1), 0)),
            pl.BlockSpec((ct, D_MODEL), lambda i, *_: (jnp.maximum(i - n_p, 0), 0)),
            pl.BlockSpec((ct, n_info), lambda i, *_: (i, 0)),
            pl.BlockSpec((ct, n_info), lambda i, *_: (jnp.minimum(i + 1, n_p + n_s - 1), 0)),
            pl.BlockSpec((1, D_MODEL), lambda i, *_: (0, 0)),
            pl.BlockSpec(memory_space=pl.ANY),
        ],
        out_specs=[
            pl.BlockSpec((ct, D_MODEL), lambda i, *_: (jnp.minimum(i, n_p - 1), 0)),
            pl.BlockSpec((ct, D_MODEL), lambda i, *_: (jnp.maximum(i - n_p, 0), 0)),
        ],
        scratch_shapes=[pltpu.VMEM((3, N_EXPERTS * WIN_ROWS, D_MODEL), BF16),
                        pltpu.VMEM((ct, N_EXPERTS * WIN_ROWS), BF16),
                        pltpu.VMEM((ct, N_EXPERTS * WIN_ROWS), BF16),
                        pltpu.VMEM((ct, D_MODEL), F32),
                        pltpu.SemaphoreType.DMA((3,)),
                        pltpu.SemaphoreType.DMA((2,))],
    )
    return pl.pallas_call(
        functools.partial(_combine_kernel, n_p),
        grid_spec=grid_spec,
        out_shape=[jax.ShapeDtypeStruct(x1_p.shape, F32), jax.ShapeDtypeStruct(x1_s.shape, F32)],
        compiler_params=pltpu.CompilerParams(dimension_semantics=("arbitrary",),
                                             vmem_limit_bytes=VMEM_LIMIT),
        name="combine",
    )(offa, nchunk, need_tail, ntail, x1_p, x1_s, tok_info, tok_info, gfin, yb)


def kernel(x_prompt, x_sample, cache_k, cache_v, state_pool, meta_tokens, norm_attn, w_in, attn_sinks,
           w_pool, pool_scale, w_out, norm_ffn, router_w, router_b, w_gate_up, b_gate_up, w_down, b_down,
           norm_final):
    assert w_in.shape[0] == 1, "single-layer trunk"
    bsz, seq, _ = x_prompt.shape
    assert bsz == 1
    nb = x_sample.shape[0]
    n_tok = seq + nb
    gattn = norm_attn[0].reshape(1, D_MODEL)
    gffn = norm_ffn[0].reshape(1, D_MODEL)
    win = w_in[0].astype(BF16)
    wpool = w_pool[0].astype(BF16)
    wout = w_out[0].astype(BF16)
    pscale = pool_scale[0].reshape(1, POOL_WIDTH)
    rw_t = router_w[0].T
    rw_hi = rw_t.astype(BF16)
    rwt = jnp.stack([rw_hi, (rw_t - rw_hi.astype(F32)).astype(BF16)])
    rb = router_b[0].reshape(N_EXPERTS, 1)
    sinks = attn_sinks[0]

    cache_rows = N_META + WINDOW
    to_batch_minor = lambda c: jnp.transpose(c[0], (1, 2, 3, 0)).reshape(cache_rows, KV_WIDTH, nb)
    from_batch_minor = lambda c: jnp.transpose(
        c.reshape(cache_rows, N_KV_HEADS, HEAD_DIM, nb), (3, 0, 1, 2))[None]
    (x1_s, h2_s, lgt_s, kout_t, vout_t, pool_t) = _sample_mixer(
        x_sample[:, 0], to_batch_minor(cache_k), to_batch_minor(cache_v),
        jnp.transpose(state_pool[0], (1, 0, 2)), gattn, win, wpool, pscale, wout, gffn, rwt, rb, sinks)
    (x1_p, h2_all, lgt_all, kmeta, vmeta, ktail, vtail, ptail) = _prompt_mixer(
        x_prompt[0], meta_tokens, gattn, win, wpool, pscale, wout, gffn, rwt, rb, sinks, h2_s, lgt_s)

    gates, col, lrank, lpos, tcar, counts = _router(lgt_all)
    counts = counts[:, 0]
    tcar = tcar[:, :, 0]
    rbk = EXPERT_ROWS
    eids = jnp.arange(N_EXPERTS, dtype=jnp.int32)
    earlier = eids[None, :] < eids[:, None]
    excl_sum = lambda a: jnp.sum(jnp.where(earlier, a[..., None, :], 0), axis=-1)
    padded = (counts + rbk - 1) // rbk * rbk
    pad_start = excl_sum(padded).astype(jnp.int32)
    pad_end = pad_start + padded
    nblk = -(-(n_tok * TOP_K) // rbk) + N_EXPERTS
    cap = nblk * rbk
    block_start = jnp.arange(nblk + 1, dtype=jnp.int32) * rbk
    owns = (pad_start[None, :] <= block_start[:, None]) & (block_start[:, None] < pad_end[None, :])
    nvalid = jnp.sum(jnp.where(owns, jnp.clip(counts[None, :] - (block_start[:, None] - pad_start[None, :]),
                                              0, rbk), 0), axis=1).astype(jnp.int32)
    has_rows = counts > 0
    last_e = jnp.max(jnp.where(has_rows, eids, 0))
    block_expert = jnp.where(jnp.any(owns, axis=1), jnp.sum(jnp.where(owns, eids[None, :], 0), axis=1),
                             last_e).astype(jnp.int32)

    run_len = jnp.concatenate([tcar[1:], counts[None, :]], axis=0) - tcar
    dcar = tcar[::DISPATCH_TILE // COMBINE_TILE]
    drun_len = jnp.concatenate([dcar[1:], counts[None, :]], axis=0) - dcar
    flat = lambda a: a.astype(jnp.int32).reshape(-1)
    xs = _dispatch(h2_all, lpos, flat(excl_sum(drun_len)), flat(drun_len), flat(pad_start[None, :] + dcar),
                   cap)
    expert_pos = excl_sum(has_rows.astype(jnp.int32))
    at_pos = has_rows[None, :] & (expert_pos[None, :] == jnp.arange(N_EXPERTS + 1, dtype=jnp.int32)[:, None])
    expert_list = jnp.where(jnp.any(at_pos, axis=1), jnp.sum(jnp.where(at_pos, eids[None, :], 0), axis=1),
                            -1).astype(jnp.int32)
    block_pos = jnp.sum(jnp.where(block_expert[:, None] == eids[None, :], expert_pos[None, :], 0),
                        axis=1).astype(jnp.int32)
    yb = _experts(xs, block_expert, nvalid, block_pos, expert_list,
                  w_gate_up[0], b_gate_up[0], w_down[0], b_down[0])

    offa = (pad_start[None, :] + (tcar - tcar % WIN_ALIGN)).astype(jnp.int32).reshape(-1)
    nchunk = jnp.maximum(jnp.max((run_len + WIN - 1) // WIN, axis=1), 1).astype(jnp.int32)
    need_tail = (tcar % WIN_ALIGN + jnp.minimum(run_len, WIN)) > WIN_HEAD
    ntail = jnp.sum(need_tail, axis=1).astype(jnp.int32)
    gfin = norm_final.reshape(1, D_MODEL)
    tok_info = jnp.concatenate([col.astype(F32), lrank.astype(F32), gates], axis=0).T
    y_prompt, y_sample = _combine(x1_p, x1_s, tok_info, gfin, yb, offa, nchunk,
                                  need_tail.astype(jnp.int32).reshape(-1), ntail)

    kv_shape = (1, 1, N_META + WINDOW, N_KV_HEADS, HEAD_DIM)
    new_k_p = jnp.concatenate([kmeta, ktail], axis=0).reshape(kv_shape)
    new_v_p = jnp.concatenate([vmeta, vtail], axis=0).reshape(kv_shape)
    new_pool_p = ptail[16 - POOL_STATE:].reshape(1, 1, POOL_STATE, POOL_WIDTH)
    new_k_s = from_batch_minor(kout_t)
    new_v_s = from_batch_minor(vout_t)
    new_pool_s = jnp.transpose(pool_t, (1, 0, 2))[None]
    return (y_prompt[None], y_sample[:, None], new_k_p, new_v_p, new_pool_p, new_k_s, new_v_s, new_pool_s)
```

```python
import functools

import jax
import jax.numpy as jnp
import numpy as np
from jax import lax
from jax.experimental import pallas as pl
from jax.experimental.pallas import tpu as pltpu

F32 = jnp.float32
BF16 = jnp.bfloat16

D_MODEL = 1024
N_META = 16
N_HEADS = 8
HEAD_DIM = 64
N_KV_HEADS = 2
GQA_GROUP = N_HEADS // N_KV_HEADS
ATTN_WIDTH = N_HEADS * HEAD_DIM
KV_WIDTH = N_KV_HEADS * HEAD_DIM
WINDOW = 128
POOL_WIDTH = D_MODEL - ATTN_WIDTH
POOL_WINDOWS = (2, 4, 8, 16)
POOL_GROUP_DIM = POOL_WIDTH // len(POOL_WINDOWS)
POOL_STATE = max(POOL_WINDOWS) - 1
N_EXPERTS = 32
TOP_K = 4
D_EXPERT = D_MODEL
SWIGLU_ALPHA = 1.702
SWIGLU_LIMIT = 7.0
NORM_EPS = 1e-5
PAST_LEN = 16384

LANES = 128
QSUB = 64
KEYS_SUB = QSUB + WINDOW
META_PAD = 64
NKEY = META_PAD + KEYS_SUB
MASKED = -1e30
PROMPT_BLOCK = 1024
ROUTE_BLOCK = 384
EXPERT_ROWS = 512
DISPATCH_TILE = 384
PACK_CHUNKS = D_MODEL // 2 // LANES
COMBINE_TILE = 128
WIN = 32
WIN_ALIGN = 16
WIN_ROWS = WIN + WIN_ALIGN
WIN_HEAD = 32
VMEM_LIMIT = 56 * 1024 * 1024


def _rms(x, g):
    return x * lax.rsqrt(jnp.mean(x * x, axis=-1, keepdims=True) + NORM_EPS) * g


def _router_logits(rwt_ref, h2, h2_hi):
    nt = (((1,), (1,)), ((), ()))
    h2_lo = (h2 - h2_hi.astype(F32)).astype(BF16)
    return (lax.dot_general(rwt_ref[0], h2_hi, nt, preferred_element_type=F32)
            + lax.dot_general(rwt_ref[0], h2_lo, nt, preferred_element_type=F32)
            + lax.dot_general(rwt_ref[1], h2_hi, nt, preferred_element_type=F32))


def _dup_halves(a):
    lane = lax.broadcasted_iota(jnp.int32, a.shape, 1)
    r = pltpu.roll(a, HEAD_DIM, axis=1)
    lo = lane < HEAD_DIM
    return jnp.where(lo, a, r), jnp.where(lo, r, a)


def _pool_means(pext_ref, n):
    outs = []
    for gi, w in enumerate(POOL_WINDOWS):
        xg = pext_ref[:, gi * POOL_GROUP_DIM:(gi + 1) * POOL_GROUP_DIM]
        s = xg
        sh = 1
        while sh < w:
            s = s + pltpu.roll(s, sh, axis=0)
            sh *= 2
        outs.append(s[16:] * (1.0 / w) - xg[16:])
    return outs


def _pack_exact_bf16_pairs(h):
    m = h.shape[1] // 2
    return (lax.shift_right_logical(pltpu.bitcast(h[:, :m], jnp.uint32), jnp.uint32(16))
            | (pltpu.bitcast(h[:, m:], jnp.uint32) & jnp.uint32(0xFFFF0000)))


def _unpack_bf16_pairs(w):
    lo = pltpu.bitcast(lax.shift_left(w, jnp.uint32(16)), F32).astype(BF16)
    hi = pltpu.bitcast(w & jnp.uint32(0xFFFF0000), F32).astype(BF16)
    return lo, hi


def _prompt_kernel(x_ref, meta_ref, gattn_ref, win_ref, wpool_ref, pscale_ref, wout_ref, gffn_ref,
                   rwt_ref, rb_ref, sink_ref, tbl_ref, tail_h2_ref, tail_lgt_ref,
                   x1_ref, h2_ref, lgt_ref, kmeta_ref, vmeta_ref, ktail_ref, vtail_ref, ptail_ref,
                   k2buf, v2buf, km2, vm2, qbuf, obuf, pext):
    pid = pl.program_id(0)
    n_main = pl.num_programs(0) - 1
    refs = (x_ref, meta_ref, gattn_ref, win_ref, wpool_ref, pscale_ref, wout_ref, gffn_ref,
            rwt_ref, rb_ref, sink_ref, tbl_ref,
            x1_ref, h2_ref, lgt_ref, kmeta_ref, vmeta_ref, ktail_ref, vtail_ref, ptail_ref,
            k2buf, v2buf, km2, vm2, qbuf, obuf, pext)

    @pl.when(pid < n_main)
    def _():
        _prompt_block(*refs)

    @pl.when(pid == n_main)
    def _():
        h2_ref[0:tail_h2_ref.shape[0], :] = tail_h2_ref[...]
        lgt_ref[:, 0:tail_lgt_ref.shape[1]] = tail_lgt_ref[...]


def _prompt_block(x_ref, meta_ref, gattn_ref, win_ref, wpool_ref, pscale_ref, wout_ref, gffn_ref,
                  rwt_ref, rb_ref, sink_ref, tbl_ref,
                  x1_ref, h2_ref, lgt_ref, kmeta_ref, vmeta_ref, ktail_ref, vtail_ref, ptail_ref,
                  k2buf, v2buf, km2, vm2, qbuf, obuf, pext):
    tb = x_ref.shape[0]
    pid = pl.program_id(0)

    @pl.when(pid == 0)
    def _():
        hm = _rms(meta_ref[...], gattn_ref[...]).astype(BF16)
        km = jnp.dot(hm, win_ref[:, ATTN_WIDTH:ATTN_WIDTH + KV_WIDTH], preferred_element_type=F32)
        vm = jnp.dot(hm, win_ref[:, ATTN_WIDTH + KV_WIDTH:ATTN_WIDTH + 2 * KV_WIDTH],
                     preferred_element_type=F32)
        pm = jnp.dot(hm, win_ref[:, ATTN_WIDTH + 2 * KV_WIDTH:], preferred_element_type=F32)
        kmeta_ref[...] = km
        vmeta_ref[...] = vm
        zpad = jnp.zeros((META_PAD - N_META, LANES), F32)
        k0, k1 = _dup_halves(jnp.concatenate([km, zpad], axis=0))
        v0, v1 = _dup_halves(jnp.concatenate([vm, zpad], axis=0))
        km2[0] = k0.astype(BF16)
        km2[1] = k1.astype(BF16)
        vm2[0, :, 0:LANES] = v0.astype(BF16)
        vm2[1, :, 0:LANES] = v1.astype(BF16)
        vm2[:, :, LANES:] = jnp.ones((N_KV_HEADS, META_PAD, LANES), BF16)
        k2buf[:, 0:WINDOW, :] = jnp.zeros((2, WINDOW, LANES), BF16)
        v2buf[:, 0:WINDOW, 0:LANES] = jnp.zeros((2, WINDOW, LANES), BF16)
        v2buf[:, :, LANES:] = jnp.ones((N_KV_HEADS, WINDOW + tb, LANES), BF16)
        pext[0:16, :] = pm

    h = _rms(x_ref[...], gattn_ref[...]).astype(BF16)
    q = jnp.dot(h, win_ref[:, 0:ATTN_WIDTH], preferred_element_type=F32) * (HEAD_DIM ** -0.5)
    lane_t = lax.broadcasted_iota(jnp.int32, (tb, LANES), 1)
    for c in range(N_HEADS // 2):
        tile = q[:, c * LANES:(c + 1) * LANES]
        for a in range(2):
            keep = (lane_t < HEAD_DIM) if a == 0 else (lane_t >= HEAD_DIM)
            piece = jnp.where(keep, tile, 0.0).astype(BF16).reshape(tb // QSUB, QSUB, LANES)
            row = ((c % 2) * 2 + a) * QSUB
            qbuf[c // 2, :, row:row + QSUB, :] = piece
    k = jnp.dot(h, win_ref[:, ATTN_WIDTH:ATTN_WIDTH + KV_WIDTH], preferred_element_type=F32)
    v = jnp.dot(h, win_ref[:, ATTN_WIDTH + KV_WIDTH:ATTN_WIDTH + 2 * KV_WIDTH], preferred_element_type=F32)
    p = jnp.dot(h, win_ref[:, ATTN_WIDTH + 2 * KV_WIDTH:], preferred_element_type=F32)
    ktail_ref[...] = k[tb - WINDOW:]
    vtail_ref[...] = v[tb - WINDOW:]
    ptail_ref[...] = p[tb - 16:]
    k0, k1 = _dup_halves(k)
    v0, v1 = _dup_halves(v)
    k2buf[0, WINDOW:, :] = k0.astype(BF16)
    k2buf[1, WINDOW:, :] = k1.astype(BF16)
    v2buf[0, WINDOW:, 0:LANES] = v0.astype(BF16)
    v2buf[1, WINDOW:, 0:LANES] = v1.astype(BF16)
    pext[16:, :] = p

    lane_q = lax.broadcasted_iota(jnp.int32, (QSUB, LANES), 1)
    lo_q = lane_q < HEAD_DIM

    for u in range(tb // QSUB):
        r0 = u * QSUB
        sel = jnp.where(pid == 0, u + 1, 0) if u < WINDOW // QSUB else 0
        for g in range(N_KV_HEADS):
            qm = qbuf[g, u]
            kwin = jnp.concatenate([km2[g], k2buf[g, r0:r0 + KEYS_SUB, :]], axis=0)
            vwin = jnp.concatenate([vm2[g], v2buf[g, r0:r0 + KEYS_SUB, :]], axis=0)
            s = lax.dot_general(qm, kwin, (((1,), (1,)), ((), ())), preferred_element_type=F32)
            s = s + tbl_ref[sel, g]
            sink = sink_ref[g]
            m = jnp.maximum(jnp.max(s, axis=1, keepdims=True), sink)
            e = jnp.exp(s - m).astype(BF16)
            r = jnp.dot(e, vwin, preferred_element_type=F32)
            o = r[:, 0:LANES] / (r[:, LANES:] + jnp.exp(sink - m))
            o0 = jnp.where(lo_q, o[0:QSUB], o[QSUB:2 * QSUB])
            o1 = jnp.where(lo_q, o[2 * QSUB:3 * QSUB], o[3 * QSUB:])
            obuf[r0:r0 + QSUB, (2 * g) * LANES:(2 * g + 1) * LANES] = o0.astype(BF16)
            obuf[r0:r0 + QSUB, (2 * g + 1) * LANES:(2 * g + 2) * LANES] = o1.astype(BF16)

    pooled = _pool_means(pext, tb)
    for gi in range(len(POOL_WINDOWS)):
        y = jnp.dot(pooled[gi].astype(BF16), wpool_ref[gi], preferred_element_type=F32)
        y = y * pscale_ref[:, gi * POOL_GROUP_DIM:(gi + 1) * POOL_GROUP_DIM]
        obuf[:, ATTN_WIDTH + gi * POOL_GROUP_DIM:ATTN_WIDTH + (gi + 1) * POOL_GROUP_DIM] = y.astype(BF16)

    k2buf[:, 0:WINDOW, :] = k2buf[:, tb:tb + WINDOW, :]
    v2buf[:, 0:WINDOW, 0:LANES] = v2buf[:, tb:tb + WINDOW, 0:LANES]
    pext[0:16, :] = pext[tb:tb + 16, :]

    x1 = x_ref[...] + jnp.dot(obuf[...], wout_ref[...], preferred_element_type=F32)
    x1_ref[...] = x1
    h2 = _rms(x1, gffn_ref[...])
    h2_hi = h2.astype(BF16)
    h2_ref[...] = h2_hi
    lgt_ref[...] = _router_logits(rwt_ref, h2, h2_hi) + rb_ref[...]


def _attn_tables(sinks):
    i = np.arange(QSUB)[:, None]
    j = np.arange(NKEY)[None, :]
    jb = j - META_PAD
    rel = i + WINDOW - jb
    band_ok = (jb >= 0) & (rel >= 0) & (rel <= WINDOW)
    meta_ok = (j < N_META) & (i >= 0)
    slopes = np.exp2(-8.0 * np.arange(1, N_HEADS + 1) / N_HEADS)
    tbl = np.empty((3, N_KV_HEADS, GQA_GROUP * QSUB, NKEY), np.float32)
    for var in range(3):
        ok = band_ok if var == 0 else band_ok & (jb >= WINDOW - (var - 1) * QSUB)
        for g in range(N_KV_HEADS):
            for a in range(GQA_GROUP):
                hd = g * GQA_GROUP + a
                bias = np.where(ok, -slopes[hd] * rel, MASKED)
                bias = np.where(meta_ok, 0.0, bias)
                tbl[var, g, a * QSUB:(a + 1) * QSUB] = bias
    sink_col = jnp.repeat(sinks.astype(F32).reshape(N_KV_HEADS, GQA_GROUP, 1), QSUB, axis=2)
    return jnp.asarray(tbl), sink_col.reshape(N_KV_HEADS, GQA_GROUP * QSUB, 1)


def _prompt_mixer(x, meta, gattn, win, wpool, pscale, wout, gffn, rwt, rb, sinks, tail_h2, tail_lgt):
    seq = x.shape[0]
    tb = PROMPT_BLOCK
    n_tail = tail_h2.shape[0]
    assert seq % tb == 0 and tb % WINDOW == 0 and n_tail <= tb
    nblk = seq // tb
    n_tok = seq + n_tail
    tbl, sink_col = _attn_tables(sinks)
    full = lambda *shape: pl.BlockSpec(shape, lambda i: (0,) * len(shape))
    main = lambda i: (jnp.minimum(i, nblk - 1), 0)
    in_width = win.shape[1]
    return pl.pallas_call(
        _prompt_kernel,
        grid=(nblk + 1,),
        in_specs=[
            pl.BlockSpec((tb, D_MODEL), main),
            full(N_META, D_MODEL), full(1, D_MODEL), full(D_MODEL, in_width),
            full(len(POOL_WINDOWS), POOL_GROUP_DIM, POOL_GROUP_DIM), full(1, POOL_WIDTH),
            full(D_MODEL, D_MODEL), full(1, D_MODEL), full(2, N_EXPERTS, D_MODEL), full(N_EXPERTS, 1),
            full(N_KV_HEADS, GQA_GROUP * QSUB, 1), full(3, N_KV_HEADS, GQA_GROUP * QSUB, NKEY),
            full(n_tail, D_MODEL), full(N_EXPERTS, n_tail),
        ],
        out_specs=[
            pl.BlockSpec((tb, D_MODEL), main),
            pl.BlockSpec((tb, D_MODEL), lambda i: (i, 0)),
            pl.BlockSpec((N_EXPERTS, tb), lambda i: (0, i)),
            full(N_META, KV_WIDTH), full(N_META, KV_WIDTH),
            full(WINDOW, KV_WIDTH), full(WINDOW, KV_WIDTH), full(16, POOL_WIDTH),
        ],
        out_shape=[
            jax.ShapeDtypeStruct((seq, D_MODEL), F32),
            jax.ShapeDtypeStruct((n_tok, D_MODEL), BF16),
            jax.ShapeDtypeStruct((N_EXPERTS, n_tok), F32),
            jax.ShapeDtypeStruct((N_META, KV_WIDTH), F32),
            jax.ShapeDtypeStruct((N_META, KV_WIDTH), F32),
            jax.ShapeDtypeStruct((WINDOW, KV_WIDTH), F32),
            jax.ShapeDtypeStruct((WINDOW, KV_WIDTH), F32),
            jax.ShapeDtypeStruct((16, POOL_WIDTH), F32),
        ],
        scratch_shapes=[
            pltpu.VMEM((N_KV_HEADS, WINDOW + tb, LANES), BF16),
            pltpu.VMEM((N_KV_HEADS, WINDOW + tb, 2 * LANES), BF16),
            pltpu.VMEM((N_KV_HEADS, META_PAD, LANES), BF16),
            pltpu.VMEM((N_KV_HEADS, META_PAD, 2 * LANES), BF16),
            pltpu.VMEM((N_KV_HEADS, tb // QSUB, GQA_GROUP * QSUB, LANES), BF16),
            pltpu.VMEM((tb, D_MODEL), BF16),
            pltpu.VMEM((16 + tb, POOL_WIDTH), F32),
        ],
        compiler_params=pltpu.CompilerParams(dimension_semantics=("arbitrary",),
                                             vmem_limit_bytes=VMEM_LIMIT),
        name="prompt_mixer",
    )(x, meta, gattn, win, wpool, pscale, wout, gffn, rwt, rb, sink_col, tbl, tail_h2, tail_lgt)


def _sample_kernel(x_ref, ck_ref, cv_ref, sp_ref, gattn_ref, win_ref, wqkv_t_ref, wpool_ref, pscale_ref,
                   wout_ref, gffn_ref, rwt_ref, rb_ref, sink_ref, bias_ref,
                   x1_ref, h2_ref, lgt_ref, kout_ref, vout_ref, pnew_ref,
                   qt_ref, sc_ref, ot_ref, obuf):
    nb = x_ref.shape[0]
    rows = ck_ref.shape[0]
    x = x_ref[...]
    h = _rms(x, gattn_ref[...]).astype(BF16)
    nt = (((1,), (1,)), ((), ()))
    qkv_t = lax.dot_general(wqkv_t_ref[...], h, nt, preferred_element_type=F32)
    qt_ref[...] = qkv_t[0:ATTN_WIDTH] * (HEAD_DIM ** -0.5)
    kt = qkv_t[ATTN_WIDTH:ATTN_WIDTH + KV_WIDTH]
    vt = qkv_t[ATTN_WIDTH + KV_WIDTH:]
    p = jnp.dot(h, win_ref[:, ATTN_WIDTH + 2 * KV_WIDTH:], preferred_element_type=F32)
    pnew_ref[0:POOL_STATE - 1] = sp_ref[1:POOL_STATE]
    pnew_ref[POOL_STATE - 1] = p

    kout_ref[0:N_META] = ck_ref[0:N_META]
    vout_ref[0:N_META] = cv_ref[0:N_META]
    kout_ref[N_META:rows - 1] = ck_ref[N_META + 1:rows]
    vout_ref[N_META:rows - 1] = cv_ref[N_META + 1:rows]
    kout_ref[rows - 1] = kt
    vout_ref[rows - 1] = vt

    def kv_rows(hd):
        g = hd // GQA_GROUP
        return slice(g * HEAD_DIM, (g + 1) * HEAD_DIM)

    def score_row(key_tile, hd):
        prod = qt_ref[hd * HEAD_DIM:(hd + 1) * HEAD_DIM, :] * key_tile[kv_rows(hd), :]
        return jnp.sum(prod, axis=0, keepdims=True)

    def score_pass(s, carry):
        key_tile = ck_ref[s]
        for hd in range(N_HEADS):
            sc_ref[hd, pl.ds(s, 1), :] = score_row(key_tile, hd)
        return carry

    lax.fori_loop(0, rows, score_pass, 0)
    pad_rows = sc_ref.shape[1] - rows - 1
    for hd in range(N_HEADS):
        sc_ref[hd, rows:rows + 1, :] = score_row(kt, hd)
        sc_ref[hd, rows + 1:, :] = jnp.full((pad_rows, nb), MASKED, F32)

    for hd in range(N_HEADS):
        s = sc_ref[hd] + bias_ref[hd]
        sink = sink_ref[hd]
        m = jnp.maximum(jnp.max(s, axis=0, keepdims=True), sink)
        e = jnp.exp(s - m)
        sc_ref[hd] = e * (1.0 / (jnp.sum(e, axis=0, keepdims=True) + jnp.exp(sink - m)))

    for half in range(2):
        heads = range(half * (N_HEADS // 2), (half + 1) * (N_HEADS // 2))

        def value_pass(s, accs, heads=heads):
            val_tile = cv_ref[s]
            return tuple(acc + sc_ref[hd, pl.ds(s, 1), :] * val_tile[kv_rows(hd), :]
                         for acc, hd in zip(accs, heads))

        accs = lax.fori_loop(0, rows, value_pass,
                             tuple(jnp.zeros((HEAD_DIM, nb), F32) for _ in heads))
        for acc, hd in zip(accs, heads):
            ot_ref[hd * HEAD_DIM:(hd + 1) * HEAD_DIM, :] = acc + sc_ref[hd, rows:rows + 1, :] * vt[kv_rows(hd), :]

    for c in range(ATTN_WIDTH // LANES):
        obuf[:, c * LANES:(c + 1) * LANES] = ot_ref[c * LANES:(c + 1) * LANES, :].T.astype(BF16)

    for gi, w in enumerate(POOL_WINDOWS):
        cols = slice(gi * POOL_GROUP_DIM, (gi + 1) * POOL_GROUP_DIM)
        pg = p[:, cols]
        acc = pg
        for d in range(1, w):
            acc = acc + sp_ref[POOL_STATE - d, :, cols]
        pooled = acc * (1.0 / w) - pg
        y = jnp.dot(pooled.astype(BF16), wpool_ref[gi], preferred_element_type=F32) * pscale_ref[:, cols]
        obuf[:, ATTN_WIDTH + gi * POOL_GROUP_DIM:ATTN_WIDTH + (gi + 1) * POOL_GROUP_DIM] = y.astype(BF16)

    x1 = x + jnp.dot(obuf[...], wout_ref[...], preferred_element_type=F32)
    x1_ref[...] = x1
    h2 = _rms(x1, gffn_ref[...])
    h2_hi = h2.astype(BF16)
    h2_ref[...] = h2_hi
    lgt_ref[...] = _router_logits(rwt_ref, h2, h2_hi) + rb_ref[...]


def _sample_mixer(x, ck_t, cv_t, sp, gattn, win, wpool, pscale, wout, gffn, rwt, rb, sinks):
    nb = x.shape[0]
    rows = ck_t.shape[0]
    assert nb == LANES and rows == N_META + WINDOW
    n_keys = -(-(rows + 1) // 8) * 8
    slopes = np.exp2(-8.0 * np.arange(1, N_HEADS + 1) / N_HEADS)
    dist = np.concatenate([np.zeros(N_META), WINDOW - np.arange(WINDOW), np.zeros(1)])
    bias = np.full((N_HEADS, n_keys, 1), MASKED, np.float32)
    bias[:, :rows + 1, 0] = -slopes[:, None] * dist[None, :]
    vm = pl.BlockSpec(memory_space=pltpu.VMEM)
    return pl.pallas_call(
        _sample_kernel,
        in_specs=[vm] * 15,
        out_specs=[vm] * 6,
        out_shape=[
            jax.ShapeDtypeStruct((nb, D_MODEL), F32),
            jax.ShapeDtypeStruct((nb, D_MODEL), BF16),
            jax.ShapeDtypeStruct((N_EXPERTS, nb), F32),
            jax.ShapeDtypeStruct(ck_t.shape, F32),
            jax.ShapeDtypeStruct(cv_t.shape, F32),
            jax.ShapeDtypeStruct(sp.shape, F32),
        ],
        scratch_shapes=[
            pltpu.VMEM((ATTN_WIDTH, nb), F32),
            pltpu.VMEM((N_HEADS, n_keys, nb), F32),
            pltpu.VMEM((ATTN_WIDTH, nb), F32),
            pltpu.VMEM((nb, D_MODEL), BF16),
        ],
        compiler_params=pltpu.CompilerParams(vmem_limit_bytes=VMEM_LIMIT),
        name="sample_mixer",
    )(x, ck_t, cv_t, sp, gattn, win, win[:, 0:ATTN_WIDTH + 2 * KV_WIDTH].T, wpool, pscale, wout, gffn, rwt, rb,
      sinks.astype(F32).reshape(N_HEADS, 1, 1), jnp.asarray(bias))


def _router_kernel(lg_ref, tri_ref, low_ref, gate_ref, col_ref, lrank_ref, lpos_ref,
                   tcar_ref, cnt_ref, carry):
    tr = ROUTE_BLOCK
    carry[...] = jnp.zeros_like(carry)

    def route_block(blk, c):
        _route_block(blk, tr, lg_ref, tri_ref, low_ref, gate_ref, col_ref, lrank_ref, lpos_ref, tcar_ref, carry)
        return c

    lax.fori_loop(0, lg_ref.shape[1] // tr, route_block, 0)
    cnt_ref[...] = carry[...].astype(jnp.int32)


def _route_block(blk, tr, lg_ref, tri_ref, low_ref, gate_ref, col_ref, lrank_ref, lpos_ref, tcar_ref, carry):
    base = pl.multiple_of(blk * tr, LANES)

    def at(j, width):
        return pl.ds(pl.multiple_of(base + j * width, LANES), width)

    work = lg_ref[:, pl.ds(base, tr)]
    eio = lax.broadcasted_iota(jnp.int32, work.shape, 0).astype(F32)
    sels, vals, idxs = [], [], []
    for _k in range(TOP_K):
        mx = jnp.max(work, axis=0, keepdims=True)
        idx = jnp.min(jnp.where(work == mx, eio, float(N_EXPERTS)), axis=0, keepdims=True)
        sel = eio == idx
        sels.append(sel)
        vals.append(mx)
        idxs.append(idx)
        work = jnp.where(sel, -jnp.inf, work)
    exps = [jnp.exp(vk - vals[0]) for vk in vals]
    tot = exps[0] + exps[1] + exps[2] + exps[3]
    onehot = jnp.zeros(work.shape, F32)
    for sel in sels:
        onehot = onehot + sel.astype(F32)
    before = jnp.dot(onehot.astype(BF16), tri_ref[...], preferred_element_type=F32) + carry[...]
    for kk in range(TOP_K):
        gate_ref[pl.ds(kk, 1), pl.ds(base, tr)] = exps[kk] / tot
    for j in range(tr // COMBINE_TILE):
        cols = slice(j * COMBINE_TILE, (j + 1) * COMBINE_TILE)
        tc = before[:, j * COMBINE_TILE:j * COMBINE_TILE + 1]
        tcar_ref[blk * (tr // COMBINE_TILE) + j] = tc.astype(jnp.int32)
        slack = tc - WIN_ALIGN * jnp.floor(tc * (1.0 / WIN_ALIGN))
        local = before[:, cols] - tc
        for kk in range(TOP_K):
            selk = sels[kk][:, cols]
            lr = jnp.sum(jnp.where(selk, local, 0.0), axis=0, keepdims=True)
            sl = jnp.sum(jnp.where(selk, slack, 0.0), axis=0, keepdims=True)
            lrank_ref[pl.ds(kk, 1), at(j, COMBINE_TILE)] = lr.astype(jnp.int32)
            col_ref[pl.ds(kk, 1), at(j, COMBINE_TILE)] = (
                idxs[kk][:, cols] * float(WIN_ROWS) + sl + lr).astype(jnp.int32)
    for j in range(tr // DISPATCH_TILE):
        cols = slice(j * DISPATCH_TILE, (j + 1) * DISPATCH_TILE)
        local = before[:, cols] - before[:, j * DISPATCH_TILE:j * DISPATCH_TILE + 1]
        tile_cnt = jnp.broadcast_to(jnp.sum(onehot[:, cols], axis=1, keepdims=True), local.shape)
        cnt_hi = jnp.floor(tile_cnt * (1.0 / 256.0))
        cnt_lo = tile_cnt - 256.0 * cnt_hi
        run_start = (256.0 * jnp.dot(low_ref[...], cnt_hi.astype(BF16), preferred_element_type=F32)
                     + jnp.dot(low_ref[...], cnt_lo.astype(BF16), preferred_element_type=F32))
        for kk in range(TOP_K):
            lp = jnp.sum(jnp.where(sels[kk][:, cols], run_start + local, 0.0), axis=0, keepdims=True)
            lpos_ref[pl.ds(kk, 1), at(j, DISPATCH_TILE)] = lp.astype(jnp.int32)
    carry[...] = carry[...] + jnp.sum(onehot, axis=1, keepdims=True)


def _router(logits_t):
    n = logits_t.shape[1]
    tr = ROUTE_BLOCK
    assert n % tr == 0
    tri = jnp.asarray(np.triu(np.ones((tr, tr), np.float32), k=1), BF16)
    low = jnp.asarray(np.tril(np.ones((N_EXPERTS, N_EXPERTS), np.float32), k=-1), BF16)
    vm = pl.BlockSpec(memory_space=pltpu.VMEM)
    return pl.pallas_call(
        _router_kernel,
        in_specs=[vm, vm, vm],
        out_specs=[vm] * 6,
        out_shape=[jax.ShapeDtypeStruct((TOP_K, n), F32),
                   jax.ShapeDtypeStruct((TOP_K, n), jnp.int32),
                   jax.ShapeDtypeStruct((TOP_K, n), jnp.int32),
                   jax.ShapeDtypeStruct((TOP_K, n), jnp.int32),
                   jax.ShapeDtypeStruct((n // COMBINE_TILE, N_EXPERTS, 1), jnp.int32),
                   jax.ShapeDtypeStruct((N_EXPERTS, 1), jnp.int32)],
        scratch_shapes=[pltpu.VMEM((N_EXPERTS, 1), F32)],
        name="router",
    )(logits_t, tri, low)


def _dispatch_kernel(lstart_ref, cnt_ref, dst_ref, h2_ref, lpos_ref, xs_hbm, stg0, stg1, sem):
    i = pl.program_id(0)
    n_tiles = pl.num_programs(0)
    dt = h2_ref.shape[0]
    rows = dt * TOP_K
    slot = i % 2
    stgs = (stg0, stg1)
    pieces = [p for p in (256, 128, 64, 32, 16, 8, 4, 2, 1) if p <= dt]
    assert dt < 512

    def drain(s):
        pltpu.make_async_copy(stgs[s], xs_hbm.at[pl.ds(0, rows)], sem.at[s]).wait()

    def issue_runs(tile, live, s):
        for e in range(N_EXPERTS):
            n = jnp.where(live, cnt_ref[tile * N_EXPERTS + e], 0)
            src0 = lstart_ref[tile * N_EXPERTS + e]
            dst0 = dst_ref[tile * N_EXPERTS + e]
            for piece in pieces:
                off = n & ~jnp.int32(2 * piece - 1)

                @pl.when((n & piece) != 0)
                def _(off=off, piece=piece, src0=src0, dst0=dst0):
                    pltpu.make_async_copy(stgs[s].at[pl.ds(src0 + off, piece)],
                                          xs_hbm.at[pl.ds(dst0 + off, piece)], sem.at[s]).start()

    def sort_tile(s):
        rid = lax.broadcasted_iota(jnp.int32, (rows, dt), 0)
        hit = rid == lpos_ref[0:1, :]
        for kk in range(1, TOP_K):
            hit = jnp.logical_or(hit, rid == lpos_ref[kk:kk + 1, :])
        perm = jnp.where(hit, 1.0, 0.0).astype(BF16)
        srt = jnp.dot(perm, h2_ref[...], preferred_element_type=F32)
        packed = _pack_exact_bf16_pairs(srt)
        stgs[s][...] = packed.reshape(rows, PACK_CHUNKS, LANES)

    for s in range(2):
        @pl.when(slot == s)
        def _(s=s):
            @pl.when(i >= 2)
            def _():
                drain(s)

            issue_runs(jnp.maximum(i - 1, 0), i >= 1, 1 - s)
            sort_tile(s)

            @pl.when(i == n_tiles - 1)
            def _():
                issue_runs(i, True, s)
                drain(s)

                @pl.when(n_tiles >= 2)
                def _():
                    drain(1 - s)


def _dispatch(h2, lpos, lstart, cnt, dst, cap):
    n_tok = h2.shape[0]
    dt = DISPATCH_TILE
    assert n_tok % dt == 0
    grid_spec = pltpu.PrefetchScalarGridSpec(
        num_scalar_prefetch=3,
        grid=(n_tok // dt,),
        in_specs=[pl.BlockSpec((dt, D_MODEL), lambda i, a, b, c: (i, 0)),
                  pl.BlockSpec((TOP_K, dt), lambda i, a, b, c: (0, i)),
                  ],
        out_specs=pl.BlockSpec(memory_space=pl.ANY),
        scratch_shapes=[pltpu.VMEM((dt * TOP_K, PACK_CHUNKS, LANES), jnp.uint32),
                        pltpu.VMEM((dt * TOP_K, PACK_CHUNKS, LANES), jnp.uint32),
                        pltpu.SemaphoreType.DMA((2,))],
    )
    return pl.pallas_call(
        _dispatch_kernel,
        grid_spec=grid_spec,
        out_shape=jax.ShapeDtypeStruct((cap, PACK_CHUNKS, LANES), jnp.uint32),
        compiler_params=pltpu.CompilerParams(dimension_semantics=("arbitrary",),
                                             vmem_limit_bytes=VMEM_LIMIT),
        name="dispatch",
    )(lstart, cnt, dst, h2, lpos)


def _expert_kernel(n_xblocks, bexp_ref, nvalid_ref, epos_ref, elist_ref,
                   x_hbm, wgu_hbm, bgu_ref, wd_hbm, bd_ref, y_ref,
                   wgu_f32, wd_f32, wgu_bf, wd_bf, xbuf, xsem, wsem):
    i = pl.program_id(0)
    rb = y_ref.shape[0]
    nvalid = nvalid_ref[i]
    pos = epos_ref[i]
    fresh = jnp.logical_or(i == 0, pos != epos_ref[jnp.maximum(i - 1, 0)])
    slot = i % 2

    def x_copies(blk, s):
        return [pltpu.make_async_copy(x_hbm.at[pl.ds(blk * rb, rb), c, :],
                                      xbuf.at[s, :, pl.ds(c * LANES, LANES)], xsem.at[s])
                for c in range(PACK_CHUNKS)]

    def w_copies(p):
        e = elist_ref[p]
        s = p % 2
        return [pltpu.make_async_copy(wgu_hbm.at[e], wgu_f32.at[s], wsem.at[s, 0]),
                pltpu.make_async_copy(wd_hbm.at[e], wd_f32.at[s], wsem.at[s, 1])]

    @pl.when(i == 0)
    def _():
        for cp in x_copies(0, 0):
            cp.start()

        @pl.when(nvalid > 0)
        def _():
            for cp in w_copies(0):
                cp.start()

    @pl.when(i + 1 < n_xblocks)
    def _():
        for cp in x_copies(i + 1, 1 - slot):
            cp.start()

    @pl.when(jnp.logical_and(fresh, nvalid > 0))
    def _():
        @pl.when(elist_ref[pos + 1] >= 0)
        def _():
            for cp in w_copies(pos + 1):
                cp.start()

        for cp in w_copies(pos):
            cp.wait()
        ws = pos % 2
        chunk = 32

        def cast_gu(r, c):
            r0 = pl.multiple_of(r * chunk, chunk)
            wgu_bf[pl.ds(r0, chunk), :] = wgu_f32[ws, pl.ds(r0, chunk), :].astype(BF16)
            return c

        def cast_d(r, c):
            r0 = pl.multiple_of(r * chunk, chunk)
            wd_bf[pl.ds(r0, chunk), :] = wd_f32[ws, pl.ds(r0, chunk), :].astype(BF16)
            return c

        lax.fori_loop(0, D_MODEL // chunk, cast_gu, 0)
        lax.fori_loop(0, D_EXPERT // chunk, cast_d, 0)

    @pl.when(i < n_xblocks)
    def _():
        for cp in x_copies(i, slot):
            cp.wait()

    def ffn(rows):
        xw = xbuf[slot, 0:rows, :]
        xw = jnp.where(lax.broadcasted_iota(jnp.int32, xw.shape, 0) < nvalid, xw, jnp.uint32(0))
        xlo, xhi = _unpack_bf16_pairs(xw)
        x = jnp.concatenate([xlo, xhi], axis=1)
        g = jnp.dot(x, wgu_bf[:, 0:D_EXPERT], preferred_element_type=F32) + bgu_ref[0, :, 0:D_EXPERT]
        u = jnp.dot(x, wgu_bf[:, D_EXPERT:], preferred_element_type=F32) + bgu_ref[0, :, D_EXPERT:]
        g = jnp.minimum(g, SWIGLU_LIMIT)
        u = jnp.clip(u, -SWIGLU_LIMIT, SWIGLU_LIMIT)
        act = g * (1.0 / (1.0 + jnp.exp(-SWIGLU_ALPHA * g))) * (u + 1.0)
        y = jnp.dot(act.astype(BF16), wd_bf[...], preferred_element_type=F32) + bd_ref[0]
        row = lax.broadcasted_iota(jnp.int32, y.shape, 0)
        y_ref[0:rows, :] = jnp.where(row < nvalid, y, 0.0).astype(BF16)
        if rows < rb:
            y_ref[rows:, :] = jnp.zeros((rb - rows, D_MODEL), BF16)

    @pl.when(nvalid > rb // 2)
    def _():
        ffn(rb)

    @pl.when(jnp.logical_and(nvalid > 0, nvalid <= rb // 2))
    def _():
        ffn(rb // 2)

    @pl.when(nvalid == 0)
    def _():
        y_ref[...] = jnp.zeros_like(y_ref)


def _experts(xs, block_expert, nvalid, block_pos, expert_list, wgu, bgu, wd, bd):
    rb = EXPERT_ROWS
    n_xblocks = xs.shape[0] // rb
    nblk = n_xblocks + 1
    any_space = pl.BlockSpec(memory_space=pl.ANY)
    grid_spec = pltpu.PrefetchScalarGridSpec(
        num_scalar_prefetch=4,
        grid=(nblk,),
        in_specs=[
            any_space,
            any_space,
            pl.BlockSpec((1, 1, 2 * D_EXPERT), lambda i, be, nu, ep, el: (be[i], 0, 0)),
            any_space,
            pl.BlockSpec((1, 1, D_MODEL), lambda i, be, nu, ep, el: (be[i], 0, 0)),
        ],
        out_specs=pl.BlockSpec((rb, D_MODEL), lambda i, be, nu, ep, el: (i, 0)),
        scratch_shapes=[pltpu.VMEM((2, D_MODEL, 2 * D_EXPERT), F32),
                        pltpu.VMEM((2, D_EXPERT, D_MODEL), F32),
                        pltpu.VMEM((D_MODEL, 2 * D_EXPERT), BF16),
                        pltpu.VMEM((D_EXPERT, D_MODEL), BF16),
                        pltpu.VMEM((2, rb, D_MODEL // 2), jnp.uint32),
                        pltpu.SemaphoreType.DMA((2,)),
                        pltpu.SemaphoreType.DMA((2, 2))],
    )
    return pl.pallas_call(
        functools.partial(_expert_kernel, n_xblocks),
        grid_spec=grid_spec,
        out_shape=jax.ShapeDtypeStruct((nblk * rb, D_MODEL), BF16),
        compiler_params=pltpu.CompilerParams(dimension_semantics=("arbitrary",),
                                             vmem_limit_bytes=VMEM_LIMIT),
        name="experts",
    )(block_expert, nvalid, block_pos, expert_list, xs, wgu, bgu.reshape(N_EXPERTS, 1, 2 * D_EXPERT), wd,
      bd.reshape(N_EXPERTS, 1, D_MODEL))


def _combine_kernel(n_prompt_tiles, offa_ref, nchunk_ref, tail_ref, ntail_ref,
                    x1p_ref, x1s_ref, info_ref, info_next_ref, gfin_ref, yb_hbm,
                    outp_ref, outs_ref, ybuf, gbuf0, gbuf1, acc_ref, sem, tsem):
    i = pl.program_id(0)
    n_tiles = pl.num_programs(0)

    def window_copy(tile, chunk, e, slot):
        base = pl.multiple_of(offa_ref[tile * N_EXPERTS + e] + chunk * WIN, WIN_ALIGN)
        return pltpu.make_async_copy(yb_hbm.at[pl.ds(base, WIN_ROWS), :],
                                     ybuf.at[slot, pl.ds(e * WIN_ROWS, WIN_ROWS), :],
                                     sem.at[slot])

    def start_windows(tile, chunk, slot):
        for e in range(N_EXPERTS):
            window_copy(tile, chunk, e, slot).start()

    def wait_windows(slot):
        pltpu.make_async_copy(yb_hbm.at[pl.ds(0, N_EXPERTS * WIN_ROWS), :], ybuf.at[slot],
                              sem.at[slot]).wait()

    def tail_copy(tile, e, slot):
        base = pl.multiple_of(offa_ref[tile * N_EXPERTS + e] + WIN_HEAD, WIN_ALIGN)
        return pltpu.make_async_copy(yb_hbm.at[pl.ds(base, WIN_ROWS - WIN_HEAD), :],
                                     ybuf.at[slot, pl.ds(e * WIN_ROWS + WIN_HEAD, WIN_ROWS - WIN_HEAD), :],
                                     tsem.at[slot])

    def start_first_chunk(tile, slot):
        for e in range(N_EXPERTS):
            base = pl.multiple_of(offa_ref[tile * N_EXPERTS + e], WIN_ALIGN)
            pltpu.make_async_copy(yb_hbm.at[pl.ds(base, WIN_HEAD), :],
                                  ybuf.at[slot, pl.ds(e * WIN_ROWS, WIN_HEAD), :], sem.at[slot]).start()

            @pl.when(tail_ref[tile * N_EXPERTS + e] != 0)
            def _(e=e):
                tail_copy(tile, e, slot).start()

    def wait_first_chunk(tile, slot):
        pltpu.make_async_copy(yb_hbm.at[pl.ds(0, N_EXPERTS * WIN_HEAD), :],
                              ybuf.at[slot, pl.ds(0, N_EXPERTS * WIN_HEAD), :], sem.at[slot]).wait()

        def one_tail(t, c):
            tail_copy(tile, 0, slot).wait()
            return c

        lax.fori_loop(0, ntail_ref[tile], one_tail, 0)

    slot = i % 2

    @pl.when(i == 0)
    def _():
        ybuf[...] = jnp.zeros_like(ybuf)
        start_first_chunk(0, 0)

    lane = lax.broadcasted_iota(jnp.int32, (COMBINE_TILE, N_EXPERTS * WIN_ROWS), 1)

    def gate_matrix(ref, chunk):
        g = jnp.zeros(lane.shape, F32)
        for kk in range(TOP_K):
            lr = ref[:, TOP_K + kk:TOP_K + kk + 1]
            in_chunk = jnp.logical_and(lr >= chunk * WIN, lr < chunk * WIN + WIN)
            colk = jnp.where(in_chunk, ref[:, kk:kk + 1] - chunk * WIN, -1.0).astype(jnp.int32)
            g = jnp.where(lane == colk, ref[:, 2 * TOP_K + kk:2 * TOP_K + kk + 1], g)
        return g.astype(BF16)

    def moe_rows(gm, buf):
        return jnp.dot(gm, ybuf[buf], preferred_element_type=F32)

    @pl.when(i == 0)
    def _():
        gbuf0[...] = gate_matrix(info_ref, 0)

    def main(s):
        start_first_chunk(jnp.minimum(i + 1, n_tiles - 1), 1 - s)
        wait_first_chunk(i, s)
        g_cur, g_nxt = (gbuf0, gbuf1) if s == 0 else (gbuf1, gbuf0)
        acc_ref[...] = moe_rows(g_cur[...], s)
        g_nxt[...] = gate_matrix(info_next_ref, 0)

    for s in range(2):
        @pl.when(slot == s)
        def _(s=s):
            main(s)

    @pl.when(i == n_tiles - 1)
    def _():
        for s in range(2):
            @pl.when(slot == s)
            def _(s=s):
                wait_first_chunk(i, 1 - s)

    def extra_chunk(j, c):
        start_windows(i, j, 2)
        wait_windows(2)
        acc_ref[...] += moe_rows(gate_matrix(info_ref, j), 2)
        return c

    lax.fori_loop(1, nchunk_ref[i], extra_chunk, 0)

    @pl.when(i < n_prompt_tiles)
    def _():
        outp_ref[...] = _rms(x1p_ref[...] + acc_ref[...], gfin_ref[...])

    @pl.when(i >= n_prompt_tiles)
    def _():
        outs_ref[...] = _rms(x1s_ref[...] + acc_ref[...], gfin_ref[...])


def _combine(x1_p, x1_s, tok_info, gfin, yb, offa, nchunk, need_tail, ntail):
    ct = COMBINE_TILE
    n_p, n_s = x1_p.shape[0] // ct, x1_s.shape[0] // ct
    assert x1_p.shape[0] % ct == 0 and x1_s.shape[0] % ct == 0 and n_s >= 1
    n_info = tok_info.shape[1]
    grid_spec = pltpu.PrefetchScalarGridSpec(
        num_scalar_prefetch=4,
        grid=(n_p + n_s,),
        in_specs=[
            pl.BlockSpec((ct, D_MODEL), lambda i, *_: (jnp.minimum(i, n_p - 1), 0)),
            pl.BlockSpec((ct, D_MODEL), lambda i, *_: (jnp.maximum(i - n_p, 0), 0)),
            pl.BlockSpec((ct, n_info), lambda i, *_: (i, 0)),
            pl.BlockSpec((ct, n_info), lambda i, *_: (jnp.minimum(i + 1, n_p + n_s - 1), 0)),
            pl.BlockSpec((1, D_MODEL), lambda i, *_: (0, 0)),
            pl.BlockSpec(memory_space=pl.ANY),
        ],
        out_specs=[
            pl.BlockSpec((ct, D_MODEL), lambda i, *_: (jnp.minimum(i, n_p - 1), 0)),
            pl.BlockSpec((ct, D_MODEL), lambda i, *_: (jnp.maximum(i - n_p, 0), 0)),
        ],
        scratch_shapes=[pltpu.VMEM((3, N_EXPERTS * WIN_ROWS, D_MODEL), BF16),
                        pltpu.VMEM((ct, N_EXPERTS * WIN_ROWS), BF16),
                        pltpu.VMEM((ct, N_EXPERTS * WIN_ROWS), BF16),
                        pltpu.VMEM((ct, D_MODEL), F32),
                        pltpu.SemaphoreType.DMA((3,)),
                        pltpu.SemaphoreType.DMA((2,))],
    )
    return pl.pallas_call(
        functools.partial(_combine_kernel, n_p),
        grid_spec=grid_spec,
        out_shape=[jax.ShapeDtypeStruct(x1_p.shape, F32), jax.ShapeDtypeStruct(x1_s.shape, F32)],
        compiler_params=pltpu.CompilerParams(dimension_semantics=("arbitrary",),
                                             vmem_limit_bytes=VMEM_LIMIT),
        name="combine",
    )(offa, nchunk, need_tail, ntail, x1_p, x1_s, tok_info, tok_info, gfin, yb)


def kernel(x_prompt, x_sample, cache_k, cache_v, state_pool, meta_tokens, norm_attn, w_in, attn_sinks,
           w_pool, pool_scale, w_out, norm_ffn, router_w, router_b, w_gate_up, b_gate_up, w_down, b_down,
           norm_final):
    assert w_in.shape[0] == 1, "single-layer trunk"
    bsz, seq, _ = x_prompt.shape
    assert bsz == 1
    nb = x_sample.shape[0]
    n_tok = seq + nb
    gattn = norm_attn[0].reshape(1, D_MODEL)
    gffn = norm_ffn[0].reshape(1, D_MODEL)
    win = w_in[0].astype(BF16)
    wpool = w_pool[0].astype(BF16)
    wout = w_out[0].astype(BF16)
    pscale = pool_scale[0].reshape(1, POOL_WIDTH)
    rw_t = router_w[0].T
    rw_hi = rw_t.astype(BF16)
    rwt = jnp.stack([rw_hi, (rw_t - rw_hi.astype(F32)).astype(BF16)])
    rb = router_b[0].reshape(N_EXPERTS, 1)
    sinks = attn_sinks[0]

    cache_rows = N_META + WINDOW
    to_batch_minor = lambda c: jnp.transpose(c[0], (1, 2, 3, 0)).reshape(cache_rows, KV_WIDTH, nb)
    from_batch_minor = lambda c: jnp.transpose(
        c.reshape(cache_rows, N_KV_HEADS, HEAD_DIM, nb), (3, 0, 1, 2))[None]
    (x1_s, h2_s, lgt_s, kout_t, vout_t, pool_t) = _sample_mixer(
        x_sample[:, 0], to_batch_minor(cache_k), to_batch_minor(cache_v),
        jnp.transpose(state_pool[0], (1, 0, 2)), gattn, win, wpool, pscale, wout, gffn, rwt, rb, sinks)
    (x1_p, h2_all, lgt_all, kmeta, vmeta, ktail, vtail, ptail) = _prompt_mixer(
        x_prompt[0], meta_tokens, gattn, win, wpool, pscale, wout, gffn, rwt, rb, sinks, h2_s, lgt_s)

    gates, col, lrank, lpos, tcar, counts = _router(lgt_all)
    counts = counts[:, 0]
    tcar = tcar[:, :, 0]
    rbk = EXPERT_ROWS
    eids = jnp.arange(N_EXPERTS, dtype=jnp.int32)
    earlier = eids[None, :] < eids[:, None]
    excl_sum = lambda a: jnp.sum(jnp.where(earlier, a[..., None, :], 0), axis=-1)
    padded = (counts + rbk - 1) // rbk * rbk
    pad_start = excl_sum(padded).astype(jnp.int32)
    pad_end = pad_start + padded
    nblk = -(-(n_tok * TOP_K) // rbk) + N_EXPERTS
    cap = nblk * rbk
    block_start = jnp.arange(nblk + 1, dtype=jnp.int32) * rbk
    owns = (pad_start[None, :] <= block_start[:, None]) & (block_start[:, None] < pad_end[None, :])
    nvalid = jnp.sum(jnp.where(owns, jnp.clip(counts[None, :] - (block_start[:, None] - pad_start[None, :]),
                                              0, rbk), 0), axis=1).astype(jnp.int32)
    has_rows = counts > 0
    last_e = jnp.max(jnp.where(has_rows, eids, 0))
    block_expert = jnp.where(jnp.any(owns, axis=1), jnp.sum(jnp.where(owns, eids[None, :], 0), axis=1),
                             last_e).astype(jnp.int32)

    run_len = jnp.concatenate([tcar[1:], counts[None, :]], axis=0) - tcar
    dcar = tcar[::DISPATCH_TILE // COMBINE_TILE]
    drun_len = jnp.concatenate([dcar[1:], counts[None, :]], axis=0) - dcar
    flat = lambda a: a.astype(jnp.int32).reshape(-1)
    xs = _dispatch(h2_all, lpos, flat(excl_sum(drun_len)), flat(drun_len), flat(pad_start[None, :] + dcar),
                   cap)
    expert_pos = excl_sum(has_rows.astype(jnp.int32))
    at_pos = has_rows[None, :] & (expert_pos[None, :] == jnp.arange(N_EXPERTS + 1, dtype=jnp.int32)[:, None])
    expert_list = jnp.where(jnp.any(at_pos, axis=1), jnp.sum(jnp.where(at_pos, eids[None, :], 0), axis=1),
                            -1).astype(jnp.int32)
    block_pos = jnp.sum(jnp.where(block_expert[:, None] == eids[None, :], expert_pos[None, :], 0),
                        axis=1).astype(jnp.int32)
    yb = _experts(xs, block_expert, nvalid, block_pos, expert_list,
                  w_gate_up[0], b_gate_up[0], w_down[0], b_down[0])

    offa = (pad_start[None, :] + (tcar - tcar % WIN_ALIGN)).astype(jnp.int32).reshape(-1)
    nchunk = jnp.maximum(jnp.max((run_len + WIN - 1) // WIN, axis=1), 1).astype(jnp.int32)
    need_tail = (tcar % WIN_ALIGN + jnp.minimum(run_len, WIN)) > WIN_HEAD
    ntail = jnp.sum(need_tail, axis=1).astype(jnp.int32)
    gfin = norm_final.reshape(1, D_MODEL)
    tok_info = jnp.concatenate([col.astype(F32), lrank.astype(F32), gates], axis=0).T
    y_prompt, y_sample = _combine(x1_p, x1_s, tok_info, gfin, yb, offa, nchunk,
                                  need_tail.astype(jnp.int32).reshape(-1), ntail)

    kv_shape = (1, 1, N_META + WINDOW, N_KV_HEADS, HEAD_DIM)
    new_k_p = jnp.concatenate([kmeta, ktail], axis=0).reshape(kv_shape)
    new_v_p = jnp.concatenate([vmeta, vtail], axis=0).reshape(kv_shape)
    new_pool_p = ptail[16 - POOL_STATE:].reshape(1, 1, POOL_STATE, POOL_WIDTH)
    new_k_s = from_batch_minor(kout_t)
    new_v_s = from_batch_minor(vout_t)
    new_pool_s = jnp.transpose(pool_t, (1, 0, 2))[None]
    return (y_prompt[None], y_sample[:, None], new_k_p, new_v_p, new_pool_p, new_k_s, new_v_s, new_pool_s)
```

```python
import functools

import jax
import jax.numpy as jnp
import numpy as np
from jax import lax
from jax.experimental import pallas as pl
from jax.experimental.pallas import tpu as pltpu

F32 = jnp.float32
BF16 = jnp.bfloat16

D_MODEL = 1024
N_META = 16
N_HEADS = 8
HEAD_DIM = 64
N_KV_HEADS = 2
GQA_GROUP = N_HEADS // N_KV_HEADS
ATTN_WIDTH = N_HEADS * HEAD_DIM
KV_WIDTH = N_KV_HEADS * HEAD_DIM
WINDOW = 128
POOL_WIDTH = D_MODEL - ATTN_WIDTH
POOL_WINDOWS = (2, 4, 8, 16)
POOL_GROUP_DIM = POOL_WIDTH // len(POOL_WINDOWS)
POOL_STATE = max(POOL_WINDOWS) - 1
N_EXPERTS = 32
TOP_K = 4
D_EXPERT = D_MODEL
SWIGLU_ALPHA = 1.702
SWIGLU_LIMIT = 7.0
NORM_EPS = 1e-5

LANES = 128
QSUB = 64
KEYS_SUB = QSUB + WINDOW
META_PAD = 64
NKEY = META_PAD + KEYS_SUB
MASKED = -1e30
PROMPT_BLOCK = 1024
ROUTE_BLOCK = 384
EXPERT_ROWS = 512
DISPATCH_TILE = 384
PACK_CHUNKS = D_MODEL // 2 // LANES
COMBINE_TILE = 128
WIN = 32
WIN_ALIGN = 16
WIN_ROWS = WIN + WIN_ALIGN
WIN_HEAD = 32
VMEM_LIMIT = 56 * 1024 * 1024


def _rms(x, g):
    return x * lax.rsqrt(jnp.mean(x * x, axis=-1, keepdims=True) + NORM_EPS) * g


def _router_logits(rwt_ref, h2, h2_hi):
    nt = (((1,), (1,)), ((), ()))
    h2_lo = (h2 - h2_hi.astype(F32)).astype(BF16)
    return (lax.dot_general(rwt_ref[0], h2_hi, nt, preferred_element_type=F32)
            + lax.dot_general(rwt_ref[0], h2_lo, nt, preferred_element_type=F32)
            + lax.dot_general(rwt_ref[1], h2_hi, nt, preferred_element_type=F32))


def _dup_halves(a):
    lane = lax.broadcasted_iota(jnp.int32, a.shape, 1)
    r = pltpu.roll(a, HEAD_DIM, axis=1)
    lo = lane < HEAD_DIM
    return jnp.where(lo, a, r), jnp.where(lo, r, a)


def _pool_means(pext_ref, n):
    outs = []
    for gi, w in enumerate(POOL_WINDOWS):
        xg = pext_ref[:, gi * POOL_GROUP_DIM:(gi + 1) * POOL_GROUP_DIM]
        s = xg
        sh = 1
        while sh < w:
            s = s + pltpu.roll(s, sh, axis=0)
            sh *= 2
        outs.append(s[16:] * (1.0 / w) - xg[16:])
    return outs


def _pack_exact_bf16_pairs(h):
    m = h.shape[1] // 2
    return (lax.shift_right_logical(pltpu.bitcast(h[:, :m], jnp.uint32), jnp.uint32(16))
            | (pltpu.bitcast(h[:, m:], jnp.uint32) & jnp.uint32(0xFFFF0000)))


def _unpack_bf16_pairs(w):
    lo = pltpu.bitcast(lax.shift_left(w, jnp.uint32(16)), F32).astype(BF16)
    hi = pltpu.bitcast(w & jnp.uint32(0xFFFF0000), F32).astype(BF16)
    return lo, hi


def _prompt_kernel(x_ref, meta_ref, gattn_ref, win_ref, wpool_ref, pscale_ref, wout_ref, gffn_ref,
                   rwt_ref, rb_ref, sink_ref, tbl_ref, tail_h2_ref, tail_lgt_ref,
                   x1_ref, h2_ref, lgt_ref, kmeta_ref, vmeta_ref, ktail_ref, vtail_ref, ptail_ref,
                   k2buf, v2buf, km2, vm2, qbuf, obuf, pext):
    pid = pl.program_id(0)
    n_main = pl.num_programs(0) - 1
    refs = (x_ref, meta_ref, gattn_ref, win_ref, wpool_ref, pscale_ref, wout_ref, gffn_ref,
            rwt_ref, rb_ref, sink_ref, tbl_ref,
            x1_ref, h2_ref, lgt_ref, kmeta_ref, vmeta_ref, ktail_ref, vtail_ref, ptail_ref,
            k2buf, v2buf, km2, vm2, qbuf, obuf, pext)

    @pl.when(pid < n_main)
    def _():
        _prompt_block(*refs)

    @pl.when(pid == n_main)
    def _():
        h2_ref[0:tail_h2_ref.shape[0], :] = tail_h2_ref[...]
        lgt_ref[:, 0:tail_lgt_ref.shape[1]] = tail_lgt_ref[...]


def _prompt_block(x_ref, meta_ref, gattn_ref, win_ref, wpool_ref, pscale_ref, wout_ref, gffn_ref,
                  rwt_ref, rb_ref, sink_ref, tbl_ref,
                  x1_ref, h2_ref, lgt_ref, kmeta_ref, vmeta_ref, ktail_ref, vtail_ref, ptail_ref,
                  k2buf, v2buf, km2, vm2, qbuf, obuf, pext):
    tb = x_ref.shape[0]
    pid = pl.program_id(0)

    @pl.when(pid == 0)
    def _():
        hm = _rms(meta_ref[...], gattn_ref[...]).astype(BF16)
        km = jnp.dot(hm, win_ref[:, ATTN_WIDTH:ATTN_WIDTH + KV_WIDTH], preferred_element_type=F32)
        vm = jnp.dot(hm, win_ref[:, ATTN_WIDTH + KV_WIDTH:ATTN_WIDTH + 2 * KV_WIDTH],
                     preferred_element_type=F32)
        pm = jnp.dot(hm, win_ref[:, ATTN_WIDTH + 2 * KV_WIDTH:], preferred_element_type=F32)
        kmeta_ref[...] = km
        vmeta_ref[...] = vm
        zpad = jnp.zeros((META_PAD - N_META, LANES), F32)
        k0, k1 = _dup_halves(jnp.concatenate([km, zpad], axis=0))
        v0, v1 = _dup_halves(jnp.concatenate([vm, zpad], axis=0))
        km2[0] = k0.astype(BF16)
        km2[1] = k1.astype(BF16)
        vm2[0, :, 0:LANES] = v0.astype(BF16)
        vm2[1, :, 0:LANES] = v1.astype(BF16)
        vm2[:, :, LANES:] = jnp.ones((N_KV_HEADS, META_PAD, LANES), BF16)
        k2buf[:, 0:WINDOW, :] = jnp.zeros((2, WINDOW, LANES), BF16)
        v2buf[:, 0:WINDOW, 0:LANES] = jnp.zeros((2, WINDOW, LANES), BF16)
        v2buf[:, :, LANES:] = jnp.ones((N_KV_HEADS, WINDOW + tb, LANES), BF16)
        pext[0:16, :] = pm

    h = _rms(x_ref[...], gattn_ref[...]).astype(BF16)
    q = jnp.dot(h, win_ref[:, 0:ATTN_WIDTH], preferred_element_type=F32) * (HEAD_DIM ** -0.5)
    lane_t = lax.broadcasted_iota(jnp.int32, (tb, LANES), 1)
    for c in range(N_HEADS // 2):
        tile = q[:, c * LANES:(c + 1) * LANES]
        for a in range(2):
            keep = (lane_t < HEAD_DIM) if a == 0 else (lane_t >= HEAD_DIM)
            piece = jnp.where(keep, tile, 0.0).astype(BF16).reshape(tb // QSUB, QSUB, LANES)
            row = ((c % 2) * 2 + a) * QSUB
            qbuf[c // 2, :, row:row + QSUB, :] = piece
    k = jnp.dot(h, win_ref[:, ATTN_WIDTH:ATTN_WIDTH + KV_WIDTH], preferred_element_type=F32)
    v = jnp.dot(h, win_ref[:, ATTN_WIDTH + KV_WIDTH:ATTN_WIDTH + 2 * KV_WIDTH], preferred_element_type=F32)
    p = jnp.dot(h, win_ref[:, ATTN_WIDTH + 2 * KV_WIDTH:], preferred_element_type=F32)
    ktail_ref[...] = k[tb - WINDOW:]
    vtail_ref[...] = v[tb - WINDOW:]
    ptail_ref[...] = p[tb - 16:]
    k0, k1 = _dup_halves(k)
    v0, v1 = _dup_halves(v)
    k2buf[0, WINDOW:, :] = k0.astype(BF16)
    k2buf[1, WINDOW:, :] = k1.astype(BF16)
    v2buf[0, WINDOW:, 0:LANES] = v0.astype(BF16)
    v2buf[1, WINDOW:, 0:LANES] = v1.astype(BF16)
    pext[16:, :] = p

    lane_q = lax.broadcasted_iota(jnp.int32, (QSUB, LANES), 1)
    lo_q = lane_q < HEAD_DIM

    for u in range(tb // QSUB):
        r0 = u * QSUB
        sel = jnp.where(pid == 0, u + 1, 0) if u < WINDOW // QSUB else 0
        for g in range(N_KV_HEADS):
            qm = qbuf[g, u]
            kwin = jnp.concatenate([km2[g], k2buf[g, r0:r0 + KEYS_SUB, :]], axis=0)
            vwin = jnp.concatenate([vm2[g], v2buf[g, r0:r0 + KEYS_SUB, :]], axis=0)
            s = lax.dot_general(qm, kwin, (((1,), (1,)), ((), ())), preferred_element_type=F32)
            s = s + tbl_ref[sel, g]
            sink = sink_ref[g]
            m = jnp.maximum(jnp.max(s, axis=1, keepdims=True), sink)
            e = jnp.exp(s - m).astype(BF16)
            r = jnp.dot(e, vwin, preferred_element_type=F32)
            o = r[:, 0:LANES] / (r[:, LANES:] + jnp.exp(sink - m))
            o0 = jnp.where(lo_q, o[0:QSUB], o[QSUB:2 * QSUB])
            o1 = jnp.where(lo_q, o[2 * QSUB:3 * QSUB], o[3 * QSUB:])
            obuf[r0:r0 + QSUB, (2 * g) * LANES:(2 * g + 1) * LANES] = o0.astype(BF16)
            obuf[r0:r0 + QSUB, (2 * g + 1) * LANES:(2 * g + 2) * LANES] = o1.astype(BF16)

    pooled = _pool_means(pext, tb)
    for gi in range(len(POOL_WINDOWS)):
        y = jnp.dot(pooled[gi].astype(BF16), wpool_ref[gi], preferred_element_type=F32)
        y = y * pscale_ref[:, gi * POOL_GROUP_DIM:(gi + 1) * POOL_GROUP_DIM]
        obuf[:, ATTN_WIDTH + gi * POOL_GROUP_DIM:ATTN_WIDTH + (gi + 1) * POOL_GROUP_DIM] = y.astype(BF16)

    k2buf[:, 0:WINDOW, :] = k2buf[:, tb:tb + WINDOW, :]
    v2buf[:, 0:WINDOW, 0:LANES] = v2buf[:, tb:tb + WINDOW, 0:LANES]
    pext[0:16, :] = pext[tb:tb + 16, :]

    x1 = x_ref[...] + jnp.dot(obuf[...], wout_ref[...], preferred_element_type=F32)
    x1_ref[...] = x1
    h2 = _rms(x1, gffn_ref[...])
    h2_hi = h2.astype(BF16)
    h2_ref[...] = h2_hi
    lgt_ref[...] = _router_logits(rwt_ref, h2, h2_hi) + rb_ref[...]


def _attn_tables(sinks):
    i = np.arange(QSUB)[:, None]
    j = np.arange(NKEY)[None, :]
    jb = j - META_PAD
    rel = i + WINDOW - jb
    band_ok = (jb >= 0) & (rel >= 0) & (rel <= WINDOW)
    meta_ok = (j < N_META) & (i >= 0)
    slopes = np.exp2(-8.0 * np.arange(1, N_HEADS + 1) / N_HEADS)
    tbl = np.empty((3, N_KV_HEADS, GQA_GROUP * QSUB, NKEY), np.float32)
    for var in range(3):
        ok = band_ok if var == 0 else band_ok & (jb >= WINDOW - (var - 1) * QSUB)
        for g in range(N_KV_HEADS):
            for a in range(GQA_GROUP):
                hd = g * GQA_GROUP + a
                bias = np.where(ok, -slopes[hd] * rel, MASKED)
                bias = np.where(meta_ok, 0.0, bias)
                tbl[var, g, a * QSUB:(a + 1) * QSUB] = bias
    sink_col = jnp.repeat(sinks.astype(F32).reshape(N_KV_HEADS, GQA_GROUP, 1), QSUB, axis=2)
    return jnp.asarray(tbl), sink_col.reshape(N_KV_HEADS, GQA_GROUP * QSUB, 1)


def _prompt_mixer(x, meta, gattn, win, wpool, pscale, wout, gffn, rwt, rb, sinks, tail_h2, tail_lgt):
    seq = x.shape[0]
    tb = PROMPT_BLOCK
    n_tail = tail_h2.shape[0]
    assert seq % tb == 0 and tb % WINDOW == 0 and n_tail <= tb
    nblk = seq // tb
    n_tok = seq + n_tail
    tbl, sink_col = _attn_tables(sinks)
    full = lambda *shape: pl.BlockSpec(shape, lambda i: (0,) * len(shape))
    main = lambda i: (jnp.minimum(i, nblk - 1), 0)
    in_width = win.shape[1]
    return pl.pallas_call(
        _prompt_kernel,
        grid=(nblk + 1,),
        in_specs=[
            pl.BlockSpec((tb, D_MODEL), main),
            full(N_META, D_MODEL), full(1, D_MODEL), full(D_MODEL, in_width),
            full(len(POOL_WINDOWS), POOL_GROUP_DIM, POOL_GROUP_DIM), full(1, POOL_WIDTH),
            full(D_MODEL, D_MODEL), full(1, D_MODEL), full(2, N_EXPERTS, D_MODEL), full(N_EXPERTS, 1),
            full(N_KV_HEADS, GQA_GROUP * QSUB, 1), full(3, N_KV_HEADS, GQA_GROUP * QSUB, NKEY),
            full(n_tail, D_MODEL), full(N_EXPERTS, n_tail),
        ],
        out_specs=[
            pl.BlockSpec((tb, D_MODEL), main),
            pl.BlockSpec((tb, D_MODEL), lambda i: (i, 0)),
            pl.BlockSpec((N_EXPERTS, tb), lambda i: (0, i)),
            full(N_META, KV_WIDTH), full(N_META, KV_WIDTH),
            full(WINDOW, KV_WIDTH), full(WINDOW, KV_WIDTH), full(16, POOL_WIDTH),
        ],
        out_shape=[
            jax.ShapeDtypeStruct((seq, D_MODEL), F32),
            jax.ShapeDtypeStruct((n_tok, D_MODEL), BF16),
            jax.ShapeDtypeStruct((N_EXPERTS, n_tok), F32),
            jax.ShapeDtypeStruct((N_META, KV_WIDTH), F32),
            jax.ShapeDtypeStruct((N_META, KV_WIDTH), F32),
            jax.ShapeDtypeStruct((WINDOW, KV_WIDTH), F32),
            jax.ShapeDtypeStruct((WINDOW, KV_WIDTH), F32),
            jax.ShapeDtypeStruct((16, POOL_WIDTH), F32),
        ],
        scratch_shapes=[
            pltpu.VMEM((N_KV_HEADS, WINDOW + tb, LANES), BF16),
            pltpu.VMEM((N_KV_HEADS, WINDOW + tb, 2 * LANES), BF16),
            pltpu.VMEM((N_KV_HEADS, META_PAD, LANES), BF16),
            pltpu.VMEM((N_KV_HEADS, META_PAD, 2 * LANES), BF16),
            pltpu.VMEM((N_KV_HEADS, tb // QSUB, GQA_GROUP * QSUB, LANES), BF16),
            pltpu.VMEM((tb, D_MODEL), BF16),
            pltpu.VMEM((16 + tb, POOL_WIDTH), F32),
        ],
        compiler_params=pltpu.CompilerParams(dimension_semantics=("arbitrary",),
                                             vmem_limit_bytes=VMEM_LIMIT),
        name="prompt_mixer",
    )(x, meta, gattn, win, wpool, pscale, wout, gffn, rwt, rb, sink_col, tbl, tail_h2, tail_lgt)


def _sample_kernel(x_ref, ck_ref, cv_ref, sp_ref, gattn_ref, win_ref, wqkv_t_ref, wpool_ref, pscale_ref,
                   wout_ref, gffn_ref, rwt_ref, rb_ref, sink_ref, bias_ref, ck_hbm, cv_hbm,
                   x1_ref, h2_ref, lgt_ref, kout_hbm, vout_hbm, pnew_ref,
                   qt_ref, sc_ref, ot_ref, obuf, newrow, csem):
    nb = x_ref.shape[0]
    rows = ck_ref.shape[0]
    shift_copies = []
    for j, (src, dst) in enumerate(((ck_hbm, kout_hbm), (cv_hbm, vout_hbm))):
        shift_copies.append(pltpu.make_async_copy(src.at[pl.ds(0, N_META)], dst.at[pl.ds(0, N_META)],
                                                  csem.at[3 * j]))
        shift_copies.append(pltpu.make_async_copy(src.at[pl.ds(N_META + 1, rows - N_META - 1)],
                                                  dst.at[pl.ds(N_META, rows - N_META - 1)],
                                                  csem.at[3 * j + 1]))
    row_copies = [pltpu.make_async_copy(newrow.at[j], dst.at[rows - 1], csem.at[3 * j + 2])
                  for j, dst in enumerate((kout_hbm, vout_hbm))]
    for cp in shift_copies:
        cp.start()
    x = x_ref[...]
    h = _rms(x, gattn_ref[...]).astype(BF16)
    nt = (((1,), (1,)), ((), ()))
    qkv_t = lax.dot_general(wqkv_t_ref[...], h, nt, preferred_element_type=F32)
    qt_ref[...] = qkv_t[0:ATTN_WIDTH] * (HEAD_DIM ** -0.5)
    kt = qkv_t[ATTN_WIDTH:ATTN_WIDTH + KV_WIDTH]
    vt = qkv_t[ATTN_WIDTH + KV_WIDTH:]
    p = jnp.dot(h, win_ref[:, ATTN_WIDTH + 2 * KV_WIDTH:], preferred_element_type=F32)
    pnew_ref[0:POOL_STATE - 1] = sp_ref[1:POOL_STATE]
    pnew_ref[POOL_STATE - 1] = p

    newrow[0] = kt
    newrow[1] = vt
    for cp in row_copies:
        cp.start()

    def kv_rows(hd):
        g = hd // GQA_GROUP
        return slice(g * HEAD_DIM, (g + 1) * HEAD_DIM)

    def score_row(key_tile, hd):
        prod = qt_ref[hd * HEAD_DIM:(hd + 1) * HEAD_DIM, :] * key_tile[kv_rows(hd), :]
        return jnp.sum(prod, axis=0, keepdims=True)

    def score_pass(s, carry):
        key_tile = ck_ref[s]
        for hd in range(N_HEADS):
            sc_ref[hd, pl.ds(s, 1), :] = score_row(key_tile, hd)
        return carry

    lax.fori_loop(0, rows, score_pass, 0)
    pad_rows = sc_ref.shape[1] - rows - 1
    for hd in range(N_HEADS):
        sc_ref[hd, rows:rows + 1, :] = score_row(kt, hd)
        sc_ref[hd, rows + 1:, :] = jnp.full((pad_rows, nb), MASKED, F32)

    for hd in range(N_HEADS):
        s = sc_ref[hd] + bias_ref[hd]
        sink = sink_ref[hd]
        m = jnp.maximum(jnp.max(s, axis=0, keepdims=True), sink)
        e = jnp.exp(s - m)
        sc_ref[hd] = e * (1.0 / (jnp.sum(e, axis=0, keepdims=True) + jnp.exp(sink - m)))

    for half in range(2):
        heads = range(half * (N_HEADS // 2), (half + 1) * (N_HEADS // 2))

        def value_pass(s, accs, heads=heads):
            val_tile = cv_ref[s]
            return tuple(acc + sc_ref[hd, pl.ds(s, 1), :] * val_tile[kv_rows(hd), :]
                         for acc, hd in zip(accs, heads))

        accs = lax.fori_loop(0, rows, value_pass,
                             tuple(jnp.zeros((HEAD_DIM, nb), F32) for _ in heads))
        for acc, hd in zip(accs, heads):
            ot_ref[hd * HEAD_DIM:(hd + 1) * HEAD_DIM, :] = acc + sc_ref[hd, rows:rows + 1, :] * vt[kv_rows(hd), :]

    for c in range(ATTN_WIDTH // LANES):
        obuf[:, c * LANES:(c + 1) * LANES] = ot_ref[c * LANES:(c + 1) * LANES, :].T.astype(BF16)

    for gi, w in enumerate(POOL_WINDOWS):
        cols = slice(gi * POOL_GROUP_DIM, (gi + 1) * POOL_GROUP_DIM)
        pg = p[:, cols]
        acc = pg
        for d in range(1, w):
            acc = acc + sp_ref[POOL_STATE - d, :, cols]
        pooled = acc * (1.0 / w) - pg
        y = jnp.dot(pooled.astype(BF16), wpool_ref[gi], preferred_element_type=F32) * pscale_ref[:, cols]
        obuf[:, ATTN_WIDTH + gi * POOL_GROUP_DIM:ATTN_WIDTH + (gi + 1) * POOL_GROUP_DIM] = y.astype(BF16)

    x1 = x + jnp.dot(obuf[...], wout_ref[...], preferred_element_type=F32)
    x1_ref[...] = x1
    h2 = _rms(x1, gffn_ref[...])
    h2_hi = h2.astype(BF16)
    h2_ref[...] = h2_hi
    lgt_ref[...] = _router_logits(rwt_ref, h2, h2_hi) + rb_ref[...]
    for cp in shift_copies + row_copies:
        cp.wait()


def _sample_mixer(x, ck_t, cv_t, sp, gattn, win, wpool, pscale, wout, gffn, rwt, rb, sinks):
    nb = x.shape[0]
    rows = ck_t.shape[0]
    assert nb == LANES and rows == N_META + WINDOW
    n_keys = -(-(rows + 1) // 8) * 8
    slopes = np.exp2(-8.0 * np.arange(1, N_HEADS + 1) / N_HEADS)
    dist = np.concatenate([np.zeros(N_META), WINDOW - np.arange(WINDOW), np.zeros(1)])
    bias = np.full((N_HEADS, n_keys, 1), MASKED, np.float32)
    bias[:, :rows + 1, 0] = -slopes[:, None] * dist[None, :]
    vm = pl.BlockSpec(memory_space=pltpu.VMEM)
    hbm = pl.BlockSpec(memory_space=pl.ANY)
    return pl.pallas_call(
        _sample_kernel,
        in_specs=[vm] * 15 + [hbm, hbm],
        out_specs=[vm, vm, vm, hbm, hbm, vm],
        out_shape=[
            jax.ShapeDtypeStruct((nb, D_MODEL), F32),
            jax.ShapeDtypeStruct((nb, D_MODEL), BF16),
            jax.ShapeDtypeStruct((N_EXPERTS, nb), F32),
            jax.ShapeDtypeStruct(ck_t.shape, F32),
            jax.ShapeDtypeStruct(cv_t.shape, F32),
            jax.ShapeDtypeStruct(sp.shape, F32),
        ],
        scratch_shapes=[
            pltpu.VMEM((ATTN_WIDTH, nb), F32),
            pltpu.VMEM((N_HEADS, n_keys, nb), F32),
            pltpu.VMEM((ATTN_WIDTH, nb), F32),
            pltpu.VMEM((nb, D_MODEL), BF16),
            pltpu.VMEM((2, KV_WIDTH, nb), F32),
            pltpu.SemaphoreType.DMA((6,)),
        ],
        compiler_params=pltpu.CompilerParams(vmem_limit_bytes=VMEM_LIMIT),
        name="sample_mixer",
    )(x, ck_t, cv_t, sp, gattn, win, win[:, 0:ATTN_WIDTH + 2 * KV_WIDTH].T, wpool, pscale, wout, gffn, rwt, rb,
      sinks.astype(F32).reshape(N_HEADS, 1, 1), jnp.asarray(bias), ck_t, cv_t)


def _router_kernel(lg_ref, tri_ref, low_ref, gate_ref, col_ref, lrank_ref, lpos_ref,
                   tcar_ref, cnt_ref, carry):
    tr = ROUTE_BLOCK
    carry[...] = jnp.zeros_like(carry)

    def route_block(blk, c):
        _route_block(blk, tr, lg_ref, tri_ref, low_ref, gate_ref, col_ref, lrank_ref, lpos_ref, tcar_ref, carry)
        return c

    lax.fori_loop(0, lg_ref.shape[1] // tr, route_block, 0)
    cnt_ref[...] = carry[...].astype(jnp.int32)


def _route_block(blk, tr, lg_ref, tri_ref, low_ref, gate_ref, col_ref, lrank_ref, lpos_ref, tcar_ref, carry):
    base = pl.multiple_of(blk * tr, LANES)

    def at(j, width):
        return pl.ds(pl.multiple_of(base + j * width, LANES), width)

    work = lg_ref[:, pl.ds(base, tr)]
    eio = lax.broadcasted_iota(jnp.int32, work.shape, 0).astype(F32)
    sels, vals, idxs = [], [], []
    for _k in range(TOP_K):
        mx = jnp.max(work, axis=0, keepdims=True)
        idx = jnp.min(jnp.where(work == mx, eio, float(N_EXPERTS)), axis=0, keepdims=True)
        sel = eio == idx
        sels.append(sel)
        vals.append(mx)
        idxs.append(idx)
        work = jnp.where(sel, -jnp.inf, work)
    exps = [jnp.exp(vk - vals[0]) for vk in vals]
    tot = exps[0] + exps[1] + exps[2] + exps[3]
    onehot = jnp.zeros(work.shape, F32)
    for sel in sels:
        onehot = onehot + sel.astype(F32)
    before = jnp.dot(onehot.astype(BF16), tri_ref[...], preferred_element_type=F32) + carry[...]
    for kk in range(TOP_K):
        gate_ref[pl.ds(kk, 1), pl.ds(base, tr)] = exps[kk] / tot
    for j in range(tr // COMBINE_TILE):
        cols = slice(j * COMBINE_TILE, (j + 1) * COMBINE_TILE)
        tc = before[:, j * COMBINE_TILE:j * COMBINE_TILE + 1]
        tcar_ref[blk * (tr // COMBINE_TILE) + j] = tc.astype(jnp.int32)
        slack = tc - WIN_ALIGN * jnp.floor(tc * (1.0 / WIN_ALIGN))
        local = before[:, cols] - tc
        for kk in range(TOP_K):
            selk = sels[kk][:, cols]
            lr = jnp.sum(jnp.where(selk, local, 0.0), axis=0, keepdims=True)
            sl = jnp.sum(jnp.where(selk, slack, 0.0), axis=0, keepdims=True)
            lrank_ref[pl.ds(kk, 1), at(j, COMBINE_TILE)] = lr.astype(jnp.int32)
            col_ref[pl.ds(kk, 1), at(j, COMBINE_TILE)] = (
                idxs[kk][:, cols] * float(WIN_ROWS) + sl + lr).astype(jnp.int32)
    for j in range(tr // DISPATCH_TILE):
        cols = slice(j * DISPATCH_TILE, (j + 1) * DISPATCH_TILE)
        local = before[:, cols] - before[:, j * DISPATCH_TILE:j * DISPATCH_TILE + 1]
        tile_cnt = jnp.broadcast_to(jnp.sum(onehot[:, cols], axis=1, keepdims=True), local.shape)
        cnt_hi = jnp.floor(tile_cnt * (1.0 / 256.0))
        cnt_lo = tile_cnt - 256.0 * cnt_hi
        run_start = (256.0 * jnp.dot(low_ref[...], cnt_hi.astype(BF16), preferred_element_type=F32)
                     + jnp.dot(low_ref[...], cnt_lo.astype(BF16), preferred_element_type=F32))
        for kk in range(TOP_K):
            lp = jnp.sum(jnp.where(sels[kk][:, cols], run_start + local, 0.0), axis=0, keepdims=True)
            lpos_ref[pl.ds(kk, 1), at(j, DISPATCH_TILE)] = lp.astype(jnp.int32)
    carry[...] = carry[...] + jnp.sum(onehot, axis=1, keepdims=True)


def _router(logits_t):
    n = logits_t.shape[1]
    tr = ROUTE_BLOCK
    assert n % tr == 0
    tri = jnp.asarray(np.triu(np.ones((tr, tr), np.float32), k=1), BF16)
    low = jnp.asarray(np.tril(np.ones((N_EXPERTS, N_EXPERTS), np.float32), k=-1), BF16)
    vm = pl.BlockSpec(memory_space=pltpu.VMEM)
    return pl.pallas_call(
        _router_kernel,
        in_specs=[vm, vm, vm],
        out_specs=[vm] * 6,
        out_shape=[jax.ShapeDtypeStruct((TOP_K, n), F32),
                   jax.ShapeDtypeStruct((TOP_K, n), jnp.int32),
                   jax.ShapeDtypeStruct((TOP_K, n), jnp.int32),
                   jax.ShapeDtypeStruct((TOP_K, n), jnp.int32),
                   jax.ShapeDtypeStruct((n // COMBINE_TILE, N_EXPERTS, 1), jnp.int32),
                   jax.ShapeDtypeStruct((N_EXPERTS, 1), jnp.int32)],
        scratch_shapes=[pltpu.VMEM((N_EXPERTS, 1), F32)],
        name="router",
    )(logits_t, tri, low)


def _dispatch_kernel(lstart_ref, cnt_ref, dst_ref, h2_ref, lpos_ref, xs_hbm, stg0, stg1, sem):
    i = pl.program_id(0)
    n_tiles = pl.num_programs(0)
    dt = h2_ref.shape[0]
    rows = dt * TOP_K
    slot = i % 2
    stgs = (stg0, stg1)
    pieces = [p for p in (256, 128, 64, 32, 16, 8, 4, 2, 1) if p <= dt]
    assert dt < 512

    def drain(s):
        pltpu.make_async_copy(stgs[s], xs_hbm.at[pl.ds(0, rows)], sem.at[s]).wait()

    def issue_runs(tile, live, s):
        for e in range(N_EXPERTS):
            n = jnp.where(live, cnt_ref[tile * N_EXPERTS + e], 0)
            src0 = lstart_ref[tile * N_EXPERTS + e]
            dst0 = dst_ref[tile * N_EXPERTS + e]
            for piece in pieces:
                off = n & ~jnp.int32(2 * piece - 1)

                @pl.when((n & piece) != 0)
                def _(off=off, piece=piece, src0=src0, dst0=dst0):
                    pltpu.make_async_copy(stgs[s].at[pl.ds(src0 + off, piece)],
                                          xs_hbm.at[pl.ds(dst0 + off, piece)], sem.at[s]).start()

    def sort_tile(s):
        rid = lax.broadcasted_iota(jnp.int32, (rows, dt), 0)
        hit = rid == lpos_ref[0:1, :]
        for kk in range(1, TOP_K):
            hit = jnp.logical_or(hit, rid == lpos_ref[kk:kk + 1, :])
        perm = jnp.where(hit, 1.0, 0.0).astype(BF16)
        srt = jnp.dot(perm, h2_ref[...], preferred_element_type=F32)
        packed = _pack_exact_bf16_pairs(srt)
        stgs[s][...] = packed.reshape(rows, PACK_CHUNKS, LANES)

    for s in range(2):
        @pl.when(slot == s)
        def _(s=s):
            @pl.when(i >= 2)
            def _():
                drain(s)

            issue_runs(jnp.maximum(i - 1, 0), i >= 1, 1 - s)
            sort_tile(s)

            @pl.when(i == n_tiles - 1)
            def _():
                issue_runs(i, True, s)
                drain(s)

                @pl.when(n_tiles >= 2)
                def _():
                    drain(1 - s)


def _dispatch(h2, lpos, lstart, cnt, dst, cap):
    n_tok = h2.shape[0]
    dt = DISPATCH_TILE
    assert n_tok % dt == 0
    grid_spec = pltpu.PrefetchScalarGridSpec(
        num_scalar_prefetch=3,
        grid=(n_tok // dt,),
        in_specs=[pl.BlockSpec((dt, D_MODEL), lambda i, a, b, c: (i, 0)),
                  pl.BlockSpec((TOP_K, dt), lambda i, a, b, c: (0, i)),
                  ],
        out_specs=pl.BlockSpec(memory_space=pl.ANY),
        scratch_shapes=[pltpu.VMEM((dt * TOP_K, PACK_CHUNKS, LANES), jnp.uint32),
                        pltpu.VMEM((dt * TOP_K, PACK_CHUNKS, LANES), jnp.uint32),
                        pltpu.SemaphoreType.DMA((2,))],
    )
    return pl.pallas_call(
        _dispatch_kernel,
        grid_spec=grid_spec,
        out_shape=jax.ShapeDtypeStruct((cap, PACK_CHUNKS, LANES), jnp.uint32),
        compiler_params=pltpu.CompilerParams(dimension_semantics=("arbitrary",),
                                             vmem_limit_bytes=VMEM_LIMIT),
        name="dispatch",
    )(lstart, cnt, dst, h2, lpos)


def _expert_kernel(n_xblocks, bexp_ref, nvalid_ref, epos_ref, elist_ref,
                   x_hbm, wgu_hbm, bgu_ref, wd_hbm, bd_ref, y_ref,
                   wgu_f32, wd_f32, wgu_bf, wd_bf, xbuf, xsem, wsem):
    i = pl.program_id(0)
    rb = y_ref.shape[0]
    nvalid = nvalid_ref[i]
    pos = epos_ref[i]
    fresh = jnp.logical_or(i == 0, pos != epos_ref[jnp.maximum(i - 1, 0)])
    slot = i % 2

    def x_copies(blk, s):
        return [pltpu.make_async_copy(x_hbm.at[pl.ds(blk * rb, rb), c, :],
                                      xbuf.at[s, :, pl.ds(c * LANES, LANES)], xsem.at[s])
                for c in range(PACK_CHUNKS)]

    def w_copies(p):
        e = elist_ref[p]
        s = p % 2
        return [pltpu.make_async_copy(wgu_hbm.at[e], wgu_f32.at[s], wsem.at[s, 0]),
                pltpu.make_async_copy(wd_hbm.at[e], wd_f32.at[s], wsem.at[s, 1])]

    @pl.when(i == 0)
    def _():
        for cp in x_copies(0, 0):
            cp.start()

        @pl.when(nvalid > 0)
        def _():
            for cp in w_copies(0):
                cp.start()

    @pl.when(i + 1 < n_xblocks)
    def _():
        for cp in x_copies(i + 1, 1 - slot):
            cp.start()

    @pl.when(jnp.logical_and(fresh, nvalid > 0))
    def _():
        @pl.when(elist_ref[pos + 1] >= 0)
        def _():
            for cp in w_copies(pos + 1):
                cp.start()

        for cp in w_copies(pos):
            cp.wait()
        ws = pos % 2
        chunk = 32

        def cast_gu(r, c):
            r0 = pl.multiple_of(r * chunk, chunk)
            wgu_bf[pl.ds(r0, chunk), :] = wgu_f32[ws, pl.ds(r0, chunk), :].astype(BF16)
            return c

        def cast_d(r, c):
            r0 = pl.multiple_of(r * chunk, chunk)
            wd_bf[pl.ds(r0, chunk), :] = wd_f32[ws, pl.ds(r0, chunk), :].astype(BF16)
            return c

        lax.fori_loop(0, D_MODEL // chunk, cast_gu, 0)
        lax.fori_loop(0, D_EXPERT // chunk, cast_d, 0)

    @pl.when(i < n_xblocks)
    def _():
        for cp in x_copies(i, slot):
            cp.wait()

    def ffn(rows):
        xw = xbuf[slot, 0:rows, :]
        xw = jnp.where(lax.broadcasted_iota(jnp.int32, xw.shape, 0) < nvalid, xw, jnp.uint32(0))
        xlo, xhi = _unpack_bf16_pairs(xw)
        x = jnp.concatenate([xlo, xhi], axis=1)
        g = jnp.dot(x, wgu_bf[:, 0:D_EXPERT], preferred_element_type=F32) + bgu_ref[0, :, 0:D_EXPERT]
        u = jnp.dot(x, wgu_bf[:, D_EXPERT:], preferred_element_type=F32) + bgu_ref[0, :, D_EXPERT:]
        g = jnp.minimum(g, SWIGLU_LIMIT)
        u = jnp.clip(u, -SWIGLU_LIMIT, SWIGLU_LIMIT)
        act = g * (1.0 / (1.0 + jnp.exp(-SWIGLU_ALPHA * g))) * (u + 1.0)
        y = jnp.dot(act.astype(BF16), wd_bf[...], preferred_element_type=F32) + bd_ref[0]
        row = lax.broadcasted_iota(jnp.int32, y.shape, 0)
        y_ref[0:rows, :] = jnp.where(row < nvalid, y, 0.0).astype(BF16)
        if rows < rb:
            y_ref[rows:, :] = jnp.zeros((rb - rows, D_MODEL), BF16)

    @pl.when(nvalid > rb // 2)
    def _():
        ffn(rb)

    @pl.when(jnp.logical_and(nvalid > 0, nvalid <= rb // 2))
    def _():
        ffn(rb // 2)

    @pl.when(nvalid == 0)
    def _():
        y_ref[...] = jnp.zeros_like(y_ref)


def _experts(xs, block_expert, nvalid, block_pos, expert_list, wgu, bgu, wd, bd):
    rb = EXPERT_ROWS
    n_xblocks = xs.shape[0] // rb
    nblk = n_xblocks + 1
    any_space = pl.BlockSpec(memory_space=pl.ANY)
    grid_spec = pltpu.PrefetchScalarGridSpec(
        num_scalar_prefetch=4,
        grid=(nblk,),
        in_specs=[
            any_space,
            any_space,
            pl.BlockSpec((1, 1, 2 * D_EXPERT), lambda i, be, nu, ep, el: (be[i], 0, 0)),
            any_space,
            pl.BlockSpec((1, 1, D_MODEL), lambda i, be, nu, ep, el: (be[i], 0, 0)),
        ],
        out_specs=pl.BlockSpec((rb, D_MODEL), lambda i, be, nu, ep, el: (i, 0)),
        scratch_shapes=[pltpu.VMEM((2, D_MODEL, 2 * D_EXPERT), F32),
                        pltpu.VMEM((2, D_EXPERT, D_MODEL), F32),
                        pltpu.VMEM((D_MODEL, 2 * D_EXPERT), BF16),
                        pltpu.VMEM((D_EXPERT, D_MODEL), BF16),
                        pltpu.VMEM((2, rb, D_MODEL // 2), jnp.uint32),
                        pltpu.SemaphoreType.DMA((2,)),
                        pltpu.SemaphoreType.DMA((2, 2))],
    )
    return pl.pallas_call(
        functools.partial(_expert_kernel, n_xblocks),
        grid_spec=grid_spec,
        out_shape=jax.ShapeDtypeStruct((nblk * rb, D_MODEL), BF16),
        compiler_params=pltpu.CompilerParams(dimension_semantics=("arbitrary",),
                                             vmem_limit_bytes=VMEM_LIMIT),
        name="experts",
    )(block_expert, nvalid, block_pos, expert_list, xs, wgu, bgu.reshape(N_EXPERTS, 1, 2 * D_EXPERT), wd,
      bd.reshape(N_EXPERTS, 1, D_MODEL))


def _combine_kernel(n_prompt_tiles, offa_ref, nchunk_ref, tail_ref, ntail_ref,
                    x1p_ref, x1s_ref, info_ref, info_next_ref, gfin_ref, yb_hbm,
                    outp_ref, outs_ref, ybuf, gbuf0, gbuf1, acc_ref, sem, tsem):
    i = pl.program_id(0)
    n_tiles = pl.num_programs(0)

    def window_copy(tile, chunk, e, slot):
        base = pl.multiple_of(offa_ref[tile * N_EXPERTS + e] + chunk * WIN, WIN_ALIGN)
        return pltpu.make_async_copy(yb_hbm.at[pl.ds(base, WIN_ROWS), :],
                                     ybuf.at[slot, pl.ds(e * WIN_ROWS, WIN_ROWS), :],
                                     sem.at[slot])

    def start_windows(tile, chunk, slot):
        for e in range(N_EXPERTS):
            window_copy(tile, chunk, e, slot).start()

    def wait_windows(slot):
        pltpu.make_async_copy(yb_hbm.at[pl.ds(0, N_EXPERTS * WIN_ROWS), :], ybuf.at[slot],
                              sem.at[slot]).wait()

    def tail_copy(tile, e, slot):
        base = pl.multiple_of(offa_ref[tile * N_EXPERTS + e] + WIN_HEAD, WIN_ALIGN)
        return pltpu.make_async_copy(yb_hbm.at[pl.ds(base, WIN_ROWS - WIN_HEAD), :],
                                     ybuf.at[slot, pl.ds(e * WIN_ROWS + WIN_HEAD, WIN_ROWS - WIN_HEAD), :],
                                     tsem.at[slot])

    def start_first_chunk(tile, slot):
        for e in range(N_EXPERTS):
            base = pl.multiple_of(offa_ref[tile * N_EXPERTS + e], WIN_ALIGN)
            pltpu.make_async_copy(yb_hbm.at[pl.ds(base, WIN_HEAD), :],
                                  ybuf.at[slot, pl.ds(e * WIN_ROWS, WIN_HEAD), :], sem.at[slot]).start()

            @pl.when(tail_ref[tile * N_EXPERTS + e] != 0)
            def _(e=e):
                tail_copy(tile, e, slot).start()

    def wait_first_chunk(tile, slot):
        pltpu.make_async_copy(yb_hbm.at[pl.ds(0, N_EXPERTS * WIN_HEAD), :],
                              ybuf.at[slot, pl.ds(0, N_EXPERTS * WIN_HEAD), :], sem.at[slot]).wait()

        def one_tail(t, c):
            tail_copy(tile, 0, slot).wait()
            return c

        lax.fori_loop(0, ntail_ref[tile], one_tail, 0)

    slot = i % 2

    @pl.when(i == 0)
    def _():
        ybuf[...] = jnp.zeros_like(ybuf)
        start_first_chunk(0, 0)

    lane = lax.broadcasted_iota(jnp.int32, (COMBINE_TILE, N_EXPERTS * WIN_ROWS), 1)

    def gate_matrix(ref, chunk):
        g = jnp.zeros(lane.shape, F32)
        for kk in range(TOP_K):
            lr = ref[:, TOP_K + kk:TOP_K + kk + 1]
            in_chunk = jnp.logical_and(lr >= chunk * WIN, lr < chunk * WIN + WIN)
            colk = jnp.where(in_chunk, ref[:, kk:kk + 1] - chunk * WIN, -1.0).astype(jnp.int32)
            g = jnp.where(lane == colk, ref[:, 2 * TOP_K + kk:2 * TOP_K + kk + 1], g)
        return g.astype(BF16)

    def moe_rows(gm, buf):
        return jnp.dot(gm, ybuf[buf], preferred_element_type=F32)

    @pl.when(i == 0)
    def _():
        gbuf0[...] = gate_matrix(info_ref, 0)

    def main(s):
        start_first_chunk(jnp.minimum(i + 1, n_tiles - 1), 1 - s)
        wait_first_chunk(i, s)
        g_cur, g_nxt = (gbuf0, gbuf1) if s == 0 else (gbuf1, gbuf0)
        acc_ref[...] = moe_rows(g_cur[...], s)
        g_nxt[...] = gate_matrix(info_next_ref, 0)

    for s in range(2):
        @pl.when(slot == s)
        def _(s=s):
            main(s)

    @pl.when(i == n_tiles - 1)
    def _():
        for s in range(2):
            @pl.when(slot == s)
            def _(s=s):
                wait_first_chunk(i, 1 - s)

    def extra_chunk(j, c):
        start_windows(i, j, 2)
        wait_windows(2)
        acc_ref[...] += moe_rows(gate_matrix(info_ref, j), 2)
        return c

    lax.fori_loop(1, nchunk_ref[i], extra_chunk, 0)

    @pl.when(i < n_prompt_tiles)
    def _():
        outp_ref[...] = _rms(x1p_ref[...] + acc_ref[...], gfin_ref[...])

    @pl.when(i >= n_prompt_tiles)
    def _():
        outs_ref[...] = _rms(x1s_ref[...] + acc_ref[...], gfin_ref[...])


def _combine(x1_p, x1_s, tok_info, gfin, yb, offa, nchunk, need_tail, ntail):
    ct = COMBINE_TILE
    n_p, n_s = x1_p.shape[0] // ct, x1_s.shape[0] // ct
    assert x1_p.shape[0] % ct == 0 and x1_s.shape[0] % ct == 0 and n_s >= 1
    n_info = tok_info.shape[1]
    grid_spec = pltpu.PrefetchScalarGridSpec(
        num_scalar_prefetch=4,
        grid=(n_p + n_s,),
        in_specs=[
            pl.BlockSpec((ct, D_MODEL), lambda i, *_: (jnp.minimum(i, n_p - 1), 0)),
            pl.BlockSpec((ct, D_MODEL), lambda i, *_: (jnp.maximum(i - n_p, 0), 0)),
            pl.BlockSpec((ct, n_info), lambda i, *_: (i, 0)),
            pl.BlockSpec((ct, n_info), lambda i, *_: (jnp.minimum(i + 1, n_p + n_s - 1), 0)),
            pl.BlockSpec((1, D_MODEL), lambda i, *_: (0, 0)),
            pl.BlockSpec(memory_space=pl.ANY),
        ],
        out_specs=[
            pl.BlockSpec((ct, D_MODEL), lambda i, *_: (jnp.minimum(i, n_p - 1), 0)),
            pl.BlockSpec((ct, D_MODEL), lambda i, *_: (jnp.maximum(i - n_p, 0), 0)),
        ],
        scratch_shapes=[pltpu.VMEM((3, N_EXPERTS * WIN_ROWS, D_MODEL), BF16),
                        pltpu.VMEM((ct, N_EXPERTS * WIN_ROWS), BF16),
                        pltpu.VMEM((ct, N_EXPERTS * WIN_ROWS), BF16),
                        pltpu.VMEM((ct, D_MODEL), F32),
                        pltpu.SemaphoreType.DMA((3,)),
                        pltpu.SemaphoreType.DMA((2,))],
    )
    return pl.pallas_call(
        functools.partial(_combine_kernel, n_p),
        grid_spec=grid_spec,
        out_shape=[jax.ShapeDtypeStruct(x1_p.shape, F32), jax.ShapeDtypeStruct(x1_s.shape, F32)],
        compiler_params=pltpu.CompilerParams(dimension_semantics=("arbitrary",),
                                             vmem_limit_bytes=VMEM_LIMIT),
        name="combine",
    )(offa, nchunk, need_tail, ntail, x1_p, x1_s, tok_info, tok_info, gfin, yb)


def kernel(x_prompt, x_sample, cache_k, cache_v, state_pool, meta_tokens, norm_attn, w_in, attn_sinks,
           w_pool, pool_scale, w_out, norm_ffn, router_w, router_b, w_gate_up, b_gate_up, w_down, b_down,
           norm_final):
    assert w_in.shape[0] == 1, "single-layer trunk"
    bsz, seq, _ = x_prompt.shape
    assert bsz == 1
    nb = x_sample.shape[0]
    n_tok = seq + nb
    gattn = norm_attn[0].reshape(1, D_MODEL)
    gffn = norm_ffn[0].reshape(1, D_MODEL)
    win = w_in[0].astype(BF16)
    wpool = w_pool[0].astype(BF16)
    wout = w_out[0].astype(BF16)
    pscale = pool_scale[0].reshape(1, POOL_WIDTH)
    rw_t = router_w[0].T
    rw_hi = rw_t.astype(BF16)
    rwt = jnp.stack([rw_hi, (rw_t - rw_hi.astype(F32)).astype(BF16)])
    rb = router_b[0].reshape(N_EXPERTS, 1)
    sinks = attn_sinks[0]

    cache_rows = N_META + WINDOW
    to_batch_minor = lambda c: jnp.transpose(c[0], (1, 2, 3, 0)).reshape(cache_rows, KV_WIDTH, nb)
    from_batch_minor = lambda c: jnp.transpose(
        c.reshape(cache_rows, N_KV_HEADS, HEAD_DIM, nb), (3, 0, 1, 2))[None]
    (x1_s, h2_s, lgt_s, kout_t, vout_t, pool_t) = _sample_mixer(
        x_sample[:, 0], to_batch_minor(cache_k), to_batch_minor(cache_v),
        jnp.transpose(state_pool[0], (1, 0, 2)), gattn, win, wpool, pscale, wout, gffn, rwt, rb, sinks)
    (x1_p, h2_all, lgt_all, kmeta, vmeta, ktail, vtail, ptail) = _prompt_mixer(
        x_prompt[0], meta_tokens, gattn, win, wpool, pscale, wout, gffn, rwt, rb, sinks, h2_s, lgt_s)

    gates, col, lrank, lpos, tcar, counts = _router(lgt_all)
    counts = counts[:, 0]
    tcar = tcar[:, :, 0]
    rbk = EXPERT_ROWS
    eids = jnp.arange(N_EXPERTS, dtype=jnp.int32)
    earlier = eids[None, :] < eids[:, None]
    excl_sum = lambda a: jnp.sum(jnp.where(earlier, a[..., None, :], 0), axis=-1)
    padded = (counts + rbk - 1) // rbk * rbk
    pad_start = excl_sum(padded).astype(jnp.int32)
    pad_end = pad_start + padded
    nblk = -(-(n_tok * TOP_K) // rbk) + N_EXPERTS
    cap = nblk * rbk
    block_start = jnp.arange(nblk + 1, dtype=jnp.int32) * rbk
    owns = (pad_start[None, :] <= block_start[:, None]) & (block_start[:, None] < pad_end[None, :])
    nvalid = jnp.sum(jnp.where(owns, jnp.clip(counts[None, :] - (block_start[:, None] - pad_start[None, :]),
                                              0, rbk), 0), axis=1).astype(jnp.int32)
    has_rows = counts > 0
    last_e = jnp.max(jnp.where(has_rows, eids, 0))
    block_expert = jnp.where(jnp.any(owns, axis=1), jnp.sum(jnp.where(owns, eids[None, :], 0), axis=1),
                             last_e).astype(jnp.int32)

    run_len = jnp.concatenate([tcar[1:], counts[None, :]], axis=0) - tcar
    dcar = tcar[::DISPATCH_TILE // COMBINE_TILE]
    drun_len = jnp.concatenate([dcar[1:], counts[None, :]], axis=0) - dcar
    flat = lambda a: a.astype(jnp.int32).reshape(-1)
    xs = _dispatch(h2_all, lpos, flat(excl_sum(drun_len)), flat(drun_len), flat(pad_start[None, :] + dcar),
                   cap)
    expert_pos = excl_sum(has_rows.astype(jnp.int32))
    at_pos = has_rows[None, :] & (expert_pos[None, :] == jnp.arange(N_EXPERTS + 1, dtype=jnp.int32)[:, None])
    expert_list = jnp.where(jnp.any(at_pos, axis=1), jnp.sum(jnp.where(at_pos, eids[None, :], 0), axis=1),
                            -1).astype(jnp.int32)
    block_pos = jnp.sum(jnp.where(block_expert[:, None] == eids[None, :], expert_pos[None, :], 0),
                        axis=1).astype(jnp.int32)
    yb = _experts(xs, block_expert, nvalid, block_pos, expert_list,
                  w_gate_up[0], b_gate_up[0], w_down[0], b_down[0])

    offa = (pad_start[None, :] + (tcar - tcar % WIN_ALIGN)).astype(jnp.int32).reshape(-1)
    nchunk = jnp.maximum(jnp.max((run_len + WIN - 1) // WIN, axis=1), 1).astype(jnp.int32)
    need_tail = (tcar % WIN_ALIGN + jnp.minimum(run_len, WIN)) > WIN_HEAD
    ntail = jnp.sum(need_tail, axis=1).astype(jnp.int32)
    gfin = norm_final.reshape(1, D_MODEL)
    tok_info = jnp.concatenate([col.astype(F32), lrank.astype(F32), gates], axis=0).T
    y_prompt, y_sample = _combine(x1_p, x1_s, tok_info, gfin, yb, offa, nchunk,
                                  need_tail.astype(jnp.int32).reshape(-1), ntail)

    kv_shape = (1, 1, N_META + WINDOW, N_KV_HEADS, HEAD_DIM)
    new_k_p = jnp.concatenate([kmeta, ktail], axis=0).reshape(kv_shape)
    new_v_p = jnp.concatenate([vmeta, vtail], axis=0).reshape(kv_shape)
    new_pool_p = ptail[16 - POOL_STATE:].reshape(1, 1, POOL_STATE, POOL_WIDTH)
    new_k_s = from_batch_minor(kout_t)
    new_v_s = from_batch_minor(vout_t)
    new_pool_s = jnp.transpose(pool_t, (1, 0, 2))[None]
    return (y_prompt[None], y_sample[:, None], new_k_p, new_v_p, new_pool_p, new_k_s, new_v_s, new_pool_s)
```

```python
import functools

import jax
import jax.numpy as jnp
import numpy as np
from jax import lax
from jax.experimental import pallas as pl
from jax.experimental.pallas import tpu as pltpu

F32 = jnp.float32
BF16 = jnp.bfloat16

D_MODEL = 1024
N_META = 16
N_HEADS = 8
HEAD_DIM = 64
N_KV_HEADS = 2
GQA_GROUP = N_HEADS // N_KV_HEADS
ATTN_WIDTH = N_HEADS * HEAD_DIM
KV_WIDTH = N_KV_HEADS * HEAD_DIM
WINDOW = 128
POOL_WIDTH = D_MODEL - ATTN_WIDTH
POOL_WINDOWS = (2, 4, 8, 16)
POOL_GROUP_DIM = POOL_WIDTH // len(POOL_WINDOWS)
POOL_STATE = max(POOL_WINDOWS) - 1
N_EXPERTS = 32
TOP_K = 4
D_EXPERT = D_MODEL
SWIGLU_ALPHA = 1.702
SWIGLU_LIMIT = 7.0
NORM_EPS = 1e-5

LANES = 128
QSUB = 64
KEYS_SUB = QSUB + WINDOW
META_PAD = 64
NKEY = META_PAD + KEYS_SUB
MASKED = -1e30
PROMPT_BLOCK = 1024
ROUTE_BLOCK = 384
EXPERT_ROWS = 512
DISPATCH_TILE = 384
PACK_CHUNKS = D_MODEL // 2 // LANES
COMBINE_TILE = 128
WIN = 32
WIN_ALIGN = 16
WIN_ROWS = WIN + WIN_ALIGN
WIN_HEAD = 32
VMEM_LIMIT = 56 * 1024 * 1024


def _rms(x, g):
    return x * lax.rsqrt(jnp.mean(x * x, axis=-1, keepdims=True) + NORM_EPS) * g


def _router_logits(rwt_ref, h2, h2_hi):
    nt = (((1,), (1,)), ((), ()))
    h2_lo = (h2 - h2_hi.astype(F32)).astype(BF16)
    return (lax.dot_general(rwt_ref[0], h2_hi, nt, preferred_element_type=F32)
            + lax.dot_general(rwt_ref[0], h2_lo, nt, preferred_element_type=F32)
            + lax.dot_general(rwt_ref[1], h2_hi, nt, preferred_element_type=F32))


def _dup_halves(a):
    lane = lax.broadcasted_iota(jnp.int32, a.shape, 1)
    r = pltpu.roll(a, HEAD_DIM, axis=1)
    lo = lane < HEAD_DIM
    return jnp.where(lo, a, r), jnp.where(lo, r, a)


def _pool_means(pext_ref, n):
    outs = []
    for gi, w in enumerate(POOL_WINDOWS):
        xg = pext_ref[:, gi * POOL_GROUP_DIM:(gi + 1) * POOL_GROUP_DIM]
        s = xg
        sh = 1
        while sh < w:
            s = s + pltpu.roll(s, sh, axis=0)
            sh *= 2
        outs.append(s[16:] * (1.0 / w) - xg[16:])
    return outs


def _pack_exact_bf16_pairs(h):
    m = h.shape[1] // 2
    return (lax.shift_right_logical(pltpu.bitcast(h[:, :m], jnp.uint32), jnp.uint32(16))
            | (pltpu.bitcast(h[:, m:], jnp.uint32) & jnp.uint32(0xFFFF0000)))


def _unpack_bf16_pairs(w):
    lo = pltpu.bitcast(lax.shift_left(w, jnp.uint32(16)), F32).astype(BF16)
    hi = pltpu.bitcast(w & jnp.uint32(0xFFFF0000), F32).astype(BF16)
    return lo, hi


def _prompt_kernel(x_ref, meta_ref, gattn_ref, win_ref, wpool_ref, pscale_ref, wout_ref, gffn_ref,
                   rwt_ref, rb_ref, sink_ref, tbl_ref, tail_h2_ref, tail_lgt_ref,
                   x1_ref, h2_ref, lgt_ref, kmeta_ref, vmeta_ref, ktail_ref, vtail_ref, ptail_ref,
                   k2buf, v2buf, km2, vm2, qbuf, obuf, pext):
    pid = pl.program_id(0)
    n_main = pl.num_programs(0) - 1
    refs = (x_ref, meta_ref, gattn_ref, win_ref, wpool_ref, pscale_ref, wout_ref, gffn_ref,
            rwt_ref, rb_ref, sink_ref, tbl_ref,
            x1_ref, h2_ref, lgt_ref, kmeta_ref, vmeta_ref, ktail_ref, vtail_ref, ptail_ref,
            k2buf, v2buf, km2, vm2, qbuf, obuf, pext)

    @pl.when(pid < n_main)
    def _():
        _prompt_block(*refs)

    @pl.when(pid == n_main)
    def _():
        h2_ref[0:tail_h2_ref.shape[0], :] = tail_h2_ref[...]
        lgt_ref[:, 0:tail_lgt_ref.shape[1]] = tail_lgt_ref[...]


def _prompt_block(x_ref, meta_ref, gattn_ref, win_ref, wpool_ref, pscale_ref, wout_ref, gffn_ref,
                  rwt_ref, rb_ref, sink_ref, tbl_ref,
                  x1_ref, h2_ref, lgt_ref, kmeta_ref, vmeta_ref, ktail_ref, vtail_ref, ptail_ref,
                  k2buf, v2buf, km2, vm2, qbuf, obuf, pext):
    tb = x_ref.shape[0]
    pid = pl.program_id(0)

    @pl.when(pid == 0)
    def _():
        hm = _rms(meta_ref[...], gattn_ref[...]).astype(BF16)
        km = jnp.dot(hm, win_ref[:, ATTN_WIDTH:ATTN_WIDTH + KV_WIDTH], preferred_element_type=F32)
        vm = jnp.dot(hm, win_ref[:, ATTN_WIDTH + KV_WIDTH:ATTN_WIDTH + 2 * KV_WIDTH],
                     preferred_element_type=F32)
        pm = jnp.dot(hm, win_ref[:, ATTN_WIDTH + 2 * KV_WIDTH:], preferred_element_type=F32)
        kmeta_ref[...] = km
        vmeta_ref[...] = vm
        zpad = jnp.zeros((META_PAD - N_META, LANES), F32)
        k0, k1 = _dup_halves(jnp.concatenate([km, zpad], axis=0))
        v0, v1 = _dup_halves(jnp.concatenate([vm, zpad], axis=0))
        km2[0] = k0.astype(BF16)
        km2[1] = k1.astype(BF16)
        vm2[0, :, 0:LANES] = v0.astype(BF16)
        vm2[1, :, 0:LANES] = v1.astype(BF16)
        vm2[:, :, LANES:] = jnp.ones((N_KV_HEADS, META_PAD, LANES), BF16)
        k2buf[:, 0:WINDOW, :] = jnp.zeros((2, WINDOW, LANES), BF16)
        v2buf[:, 0:WINDOW, 0:LANES] = jnp.zeros((2, WINDOW, LANES), BF16)
        v2buf[:, :, LANES:] = jnp.ones((N_KV_HEADS, WINDOW + tb, LANES), BF16)
        pext[0:16, :] = pm

    h = _rms(x_ref[...], gattn_ref[...]).astype(BF16)
    q = jnp.dot(h, win_ref[:, 0:ATTN_WIDTH], preferred_element_type=F32) * (HEAD_DIM ** -0.5)
    lane_t = lax.broadcasted_iota(jnp.int32, (tb, LANES), 1)
    for c in range(N_HEADS // 2):
        tile = q[:, c * LANES:(c + 1) * LANES]
        for a in range(2):
            keep = (lane_t < HEAD_DIM) if a == 0 else (lane_t >= HEAD_DIM)
            piece = jnp.where(keep, tile, 0.0).astype(BF16).reshape(tb // QSUB, QSUB, LANES)
            row = ((c % 2) * 2 + a) * QSUB
            qbuf[c // 2, :, row:row + QSUB, :] = piece
    k = jnp.dot(h, win_ref[:, ATTN_WIDTH:ATTN_WIDTH + KV_WIDTH], preferred_element_type=F32)
    v = jnp.dot(h, win_ref[:, ATTN_WIDTH + KV_WIDTH:ATTN_WIDTH + 2 * KV_WIDTH], preferred_element_type=F32)
    p = jnp.dot(h, win_ref[:, ATTN_WIDTH + 2 * KV_WIDTH:], preferred_element_type=F32)
    ktail_ref[...] = k[tb - WINDOW:]
    vtail_ref[...] = v[tb - WINDOW:]
    ptail_ref[...] = p[tb - 16:]
    k0, k1 = _dup_halves(k)
    v0, v1 = _dup_halves(v)
    k2buf[0, WINDOW:, :] = k0.astype(BF16)
    k2buf[1, WINDOW:, :] = k1.astype(BF16)
    v2buf[0, WINDOW:, 0:LANES] = v0.astype(BF16)
    v2buf[1, WINDOW:, 0:LANES] = v1.astype(BF16)
    pext[16:, :] = p

    lane_q = lax.broadcasted_iota(jnp.int32, (QSUB, LANES), 1)
    lo_q = lane_q < HEAD_DIM

    for u in range(tb // QSUB):
        r0 = u * QSUB
        sel = jnp.where(pid == 0, u + 1, 0) if u < WINDOW // QSUB else 0
        for g in range(N_KV_HEADS):
            qm = qbuf[g, u]
            kwin = jnp.concatenate([km2[g], k2buf[g, r0:r0 + KEYS_SUB, :]], axis=0)
            vwin = jnp.concatenate([vm2[g], v2buf[g, r0:r0 + KEYS_SUB, :]], axis=0)
            s = lax.dot_general(qm, kwin, (((1,), (1,)), ((), ())), preferred_element_type=F32)
            s = s + tbl_ref[sel, g]
            sink = sink_ref[g]
            m = jnp.maximum(jnp.max(s, axis=1, keepdims=True), sink)
            e = jnp.exp(s - m).astype(BF16)
            r = jnp.dot(e, vwin, preferred_element_type=F32)
            o = r[:, 0:LANES] / (r[:, LANES:] + jnp.exp(sink - m))
            o0 = jnp.where(lo_q, o[0:QSUB], o[QSUB:2 * QSUB])
            o1 = jnp.where(lo_q, o[2 * QSUB:3 * QSUB], o[3 * QSUB:])
            obuf[r0:r0 + QSUB, (2 * g) * LANES:(2 * g + 1) * LANES] = o0.astype(BF16)
            obuf[r0:r0 + QSUB, (2 * g + 1) * LANES:(2 * g + 2) * LANES] = o1.astype(BF16)

    pooled = _pool_means(pext, tb)
    for gi in range(len(POOL_WINDOWS)):
        y = jnp.dot(pooled[gi].astype(BF16), wpool_ref[gi], preferred_element_type=F32)
        y = y * pscale_ref[:, gi * POOL_GROUP_DIM:(gi + 1) * POOL_GROUP_DIM]
        obuf[:, ATTN_WIDTH + gi * POOL_GROUP_DIM:ATTN_WIDTH + (gi + 1) * POOL_GROUP_DIM] = y.astype(BF16)

    k2buf[:, 0:WINDOW, :] = k2buf[:, tb:tb + WINDOW, :]
    v2buf[:, 0:WINDOW, 0:LANES] = v2buf[:, tb:tb + WINDOW, 0:LANES]
    pext[0:16, :] = pext[tb:tb + 16, :]

    x1 = x_ref[...] + jnp.dot(obuf[...], wout_ref[...], preferred_element_type=F32)
    x1_ref[...] = x1
    h2 = _rms(x1, gffn_ref[...])
    h2_hi = h2.astype(BF16)
    h2_ref[...] = h2_hi
    lgt_ref[...] = _router_logits(rwt_ref, h2, h2_hi) + rb_ref[...]


def _attn_tables(sinks):
    i = np.arange(QSUB)[:, None]
    j = np.arange(NKEY)[None, :]
    jb = j - META_PAD
    rel = i + WINDOW - jb
    band_ok = (jb >= 0) & (rel >= 0) & (rel <= WINDOW)
    meta_ok = (j < N_META) & (i >= 0)
    slopes = np.exp2(-8.0 * np.arange(1, N_HEADS + 1) / N_HEADS)
    tbl = np.empty((3, N_KV_HEADS, GQA_GROUP * QSUB, NKEY), np.float32)
    for var in range(3):
        ok = band_ok if var == 0 else band_ok & (jb >= WINDOW - (var - 1) * QSUB)
        for g in range(N_KV_HEADS):
            for a in range(GQA_GROUP):
                hd = g * GQA_GROUP + a
                bias = np.where(ok, -slopes[hd] * rel, MASKED)
                bias = np.where(meta_ok, 0.0, bias)
                tbl[var, g, a * QSUB:(a + 1) * QSUB] = bias
    sink_col = jnp.repeat(sinks.astype(F32).reshape(N_KV_HEADS, GQA_GROUP, 1), QSUB, axis=2)
    return jnp.asarray(tbl), sink_col.reshape(N_KV_HEADS, GQA_GROUP * QSUB, 1)


def _prompt_mixer(x, meta, gattn, win, wpool, pscale, wout, gffn, rwt, rb, sinks, tail_h2, tail_lgt):
    seq = x.shape[0]
    tb = PROMPT_BLOCK
    n_tail = tail_h2.shape[0]
    assert seq % tb == 0 and tb % WINDOW == 0 and n_tail <= tb
    nblk = seq // tb
    n_tok = seq + n_tail
    tbl, sink_col = _attn_tables(sinks)
    full = lambda *shape: pl.BlockSpec(shape, lambda i: (0,) * len(shape))
    main = lambda i: (jnp.minimum(i, nblk - 1), 0)
    in_width = win.shape[1]
    return pl.pallas_call(
        _prompt_kernel,
        grid=(nblk + 1,),
        in_specs=[
            pl.BlockSpec((tb, D_MODEL), main),
            full(N_META, D_MODEL), full(1, D_MODEL), full(D_MODEL, in_width),
            full(len(POOL_WINDOWS), POOL_GROUP_DIM, POOL_GROUP_DIM), full(1, POOL_WIDTH),
            full(D_MODEL, D_MODEL), full(1, D_MODEL), full(2, N_EXPERTS, D_MODEL), full(N_EXPERTS, 1),
            full(N_KV_HEADS, GQA_GROUP * QSUB, 1), full(3, N_KV_HEADS, GQA_GROUP * QSUB, NKEY),
            full(n_tail, D_MODEL), full(N_EXPERTS, n_tail),
        ],
        out_specs=[
            pl.BlockSpec((tb, D_MODEL), main),
            pl.BlockSpec((tb, D_MODEL), lambda i: (i, 0)),
            pl.BlockSpec((N_EXPERTS, tb), lambda i: (0, i)),
            full(N_META, KV_WIDTH), full(N_META, KV_WIDTH),
            full(WINDOW, KV_WIDTH), full(WINDOW, KV_WIDTH), full(16, POOL_WIDTH),
        ],
        out_shape=[
            jax.ShapeDtypeStruct((seq, D_MODEL), F32),
            jax.ShapeDtypeStruct((n_tok, D_MODEL), BF16),
            jax.ShapeDtypeStruct((N_EXPERTS, n_tok), F32),
            jax.ShapeDtypeStruct((N_META, KV_WIDTH), F32),
            jax.ShapeDtypeStruct((N_META, KV_WIDTH), F32),
            jax.ShapeDtypeStruct((WINDOW, KV_WIDTH), F32),
            jax.ShapeDtypeStruct((WINDOW, KV_WIDTH), F32),
            jax.ShapeDtypeStruct((16, POOL_WIDTH), F32),
        ],
        scratch_shapes=[
            pltpu.VMEM((N_KV_HEADS, WINDOW + tb, LANES), BF16),
            pltpu.VMEM((N_KV_HEADS, WINDOW + tb, 2 * LANES), BF16),
            pltpu.VMEM((N_KV_HEADS, META_PAD, LANES), BF16),
            pltpu.VMEM((N_KV_HEADS, META_PAD, 2 * LANES), BF16),
            pltpu.VMEM((N_KV_HEADS, tb // QSUB, GQA_GROUP * QSUB, LANES), BF16),
            pltpu.VMEM((tb, D_MODEL), BF16),
            pltpu.VMEM((16 + tb, POOL_WIDTH), F32),
        ],
        compiler_params=pltpu.CompilerParams(dimension_semantics=("arbitrary",),
                                             vmem_limit_bytes=VMEM_LIMIT),
        name="prompt_mixer",
    )(x, meta, gattn, win, wpool, pscale, wout, gffn, rwt, rb, sink_col, tbl, tail_h2, tail_lgt)


def _sample_kernel(x_ref, ck_ref, cv_ref, sp_ref, gattn_ref, win_ref, wqkv_t_ref, wpool_ref, pscale_ref,
                   wout_ref, gffn_ref, rwt_ref, rb_ref, sink_ref, bias_ref,
                   x1_ref, h2_ref, lgt_ref, kout_ref, vout_ref, pnew_ref,
                   qt_ref, sc_ref, ot_ref, obuf):
    nb = x_ref.shape[0]
    rows = ck_ref.shape[0]
    x = x_ref[...]
    h = _rms(x, gattn_ref[...]).astype(BF16)
    nt = (((1,), (1,)), ((), ()))
    qkv_t = lax.dot_general(wqkv_t_ref[...], h, nt, preferred_element_type=F32)
    qt_ref[...] = qkv_t[0:ATTN_WIDTH] * (HEAD_DIM ** -0.5)
    kt = qkv_t[ATTN_WIDTH:ATTN_WIDTH + KV_WIDTH]
    vt = qkv_t[ATTN_WIDTH + KV_WIDTH:]
    p = jnp.dot(h, win_ref[:, ATTN_WIDTH + 2 * KV_WIDTH:], preferred_element_type=F32)
    pnew_ref[0:POOL_STATE - 1] = sp_ref[1:POOL_STATE]
    pnew_ref[POOL_STATE - 1] = p

    kout_ref[0:N_META] = ck_ref[0:N_META]
    vout_ref[0:N_META] = cv_ref[0:N_META]
    kout_ref[N_META:rows - 1] = ck_ref[N_META + 1:rows]
    vout_ref[N_META:rows - 1] = cv_ref[N_META + 1:rows]
    kout_ref[rows - 1] = kt
    vout_ref[rows - 1] = vt

    def kv_rows(hd):
        g = hd // GQA_GROUP
        return slice(g * HEAD_DIM, (g + 1) * HEAD_DIM)

    def score_row(key_tile, hd):
        prod = qt_ref[hd * HEAD_DIM:(hd + 1) * HEAD_DIM, :] * key_tile[kv_rows(hd), :]
        return jnp.sum(prod, axis=0, keepdims=True)

    def score_pass(s, carry):
        key_tile = ck_ref[s]
        for hd in range(N_HEADS):
            sc_ref[hd, pl.ds(s, 1), :] = score_row(key_tile, hd)
        return carry

    lax.fori_loop(0, rows, score_pass, 0)
    pad_rows = sc_ref.shape[1] - rows - 1
    for hd in range(N_HEADS):
        sc_ref[hd, rows:rows + 1, :] = score_row(kt, hd)
        sc_ref[hd, rows + 1:, :] = jnp.full((pad_rows, nb), MASKED, F32)

    for hd in range(N_HEADS):
        s = sc_ref[hd] + bias_ref[hd]
        sink = sink_ref[hd]
        m = jnp.maximum(jnp.max(s, axis=0, keepdims=True), sink)
        e = jnp.exp(s - m)
        sc_ref[hd] = e * (1.0 / (jnp.sum(e, axis=0, keepdims=True) + jnp.exp(sink - m)))

    for half in range(2):
        heads = range(half * (N_HEADS // 2), (half + 1) * (N_HEADS // 2))

        def value_pass(s, accs, heads=heads):
            val_tile = cv_ref[s]
            return tuple(acc + sc_ref[hd, pl.ds(s, 1), :] * val_tile[kv_rows(hd), :]
                         for acc, hd in zip(accs, heads))

        accs = lax.fori_loop(0, rows, value_pass,
                             tuple(jnp.zeros((HEAD_DIM, nb), F32) for _ in heads))
        for acc, hd in zip(accs, heads):
            ot_ref[hd * HEAD_DIM:(hd + 1) * HEAD_DIM, :] = acc + sc_ref[hd, rows:rows + 1, :] * vt[kv_rows(hd), :]

    for c in range(ATTN_WIDTH // LANES):
        obuf[:, c * LANES:(c + 1) * LANES] = ot_ref[c * LANES:(c + 1) * LANES, :].T.astype(BF16)

    for gi, w in enumerate(POOL_WINDOWS):
        cols = slice(gi * POOL_GROUP_DIM, (gi + 1) * POOL_GROUP_DIM)
        pg = p[:, cols]
        acc = pg
        for d in range(1, w):
            acc = acc + sp_ref[POOL_STATE - d, :, cols]
        pooled = acc * (1.0 / w) - pg
        y = jnp.dot(pooled.astype(BF16), wpool_ref[gi], preferred_element_type=F32) * pscale_ref[:, cols]
        obuf[:, ATTN_WIDTH + gi * POOL_GROUP_DIM:ATTN_WIDTH + (gi + 1) * POOL_GROUP_DIM] = y.astype(BF16)

    x1 = x + jnp.dot(obuf[...], wout_ref[...], preferred_element_type=F32)
    x1_ref[...] = x1
    h2 = _rms(x1, gffn_ref[...])
    h2_hi = h2.astype(BF16)
    h2_ref[...] = h2_hi
    lgt_ref[...] = _router_logits(rwt_ref, h2, h2_hi) + rb_ref[...]


def _sample_mixer(x, ck_t, cv_t, sp, gattn, win, wpool, pscale, wout, gffn, rwt, rb, sinks):
    nb = x.shape[0]
    rows = ck_t.shape[0]
    assert nb == LANES and rows == N_META + WINDOW
    n_keys = -(-(rows + 1) // 8) * 8
    slopes = np.exp2(-8.0 * np.arange(1, N_HEADS + 1) / N_HEADS)
    dist = np.concatenate([np.zeros(N_META), WINDOW - np.arange(WINDOW), np.zeros(1)])
    bias = np.full((N_HEADS, n_keys, 1), MASKED, np.float32)
    bias[:, :rows + 1, 0] = -slopes[:, None] * dist[None, :]
    vm = pl.BlockSpec(memory_space=pltpu.VMEM)
    return pl.pallas_call(
        _sample_kernel,
        in_specs=[vm] * 15,
        out_specs=[vm] * 6,
        out_shape=[
            jax.ShapeDtypeStruct((nb, D_MODEL), F32),
            jax.ShapeDtypeStruct((nb, D_MODEL), BF16),
            jax.ShapeDtypeStruct((N_EXPERTS, nb), F32),
            jax.ShapeDtypeStruct(ck_t.shape, F32),
            jax.ShapeDtypeStruct(cv_t.shape, F32),
            jax.ShapeDtypeStruct(sp.shape, F32),
        ],
        scratch_shapes=[
            pltpu.VMEM((ATTN_WIDTH, nb), F32),
            pltpu.VMEM((N_HEADS, n_keys, nb), F32),
            pltpu.VMEM((ATTN_WIDTH, nb), F32),
            pltpu.VMEM((nb, D_MODEL), BF16),
        ],
        compiler_params=pltpu.CompilerParams(vmem_limit_bytes=VMEM_LIMIT),
        name="sample_mixer",
    )(x, ck_t, cv_t, sp, gattn, win, win[:, 0:ATTN_WIDTH + 2 * KV_WIDTH].T, wpool, pscale, wout, gffn, rwt, rb,
      sinks.astype(F32).reshape(N_HEADS, 1, 1), jnp.asarray(bias))


def _router_kernel(lg_ref, tri_ref, low_ref, gate_ref, col_ref, lrank_ref, lpos_ref,
                   tcar_ref, cnt_ref, carry):
    tr = ROUTE_BLOCK
    carry[...] = jnp.zeros_like(carry)

    def route_block(blk, c):
        _route_block(blk, tr, lg_ref, tri_ref, low_ref, gate_ref, col_ref, lrank_ref, lpos_ref, tcar_ref, carry)
        return c

    lax.fori_loop(0, lg_ref.shape[1] // tr, route_block, 0)
    cnt_ref[...] = carry[...].astype(jnp.int32)


def _route_block(blk, tr, lg_ref, tri_ref, low_ref, gate_ref, col_ref, lrank_ref, lpos_ref, tcar_ref, carry):
    base = pl.multiple_of(blk * tr, LANES)

    def at(j, width):
        return pl.ds(pl.multiple_of(base + j * width, LANES), width)

    work = lg_ref[:, pl.ds(base, tr)]
    eio = lax.broadcasted_iota(jnp.int32, work.shape, 0).astype(F32)
    sels, vals, idxs = [], [], []
    for _k in range(TOP_K):
        mx = jnp.max(work, axis=0, keepdims=True)
        idx = jnp.min(jnp.where(work == mx, eio, float(N_EXPERTS)), axis=0, keepdims=True)
        sel = eio == idx
        sels.append(sel)
        vals.append(mx)
        idxs.append(idx)
        work = jnp.where(sel, -jnp.inf, work)
    exps = [jnp.exp(vk - vals[0]) for vk in vals]
    tot = exps[0] + exps[1] + exps[2] + exps[3]
    onehot = jnp.zeros(work.shape, F32)
    for sel in sels:
        onehot = onehot + sel.astype(F32)
    before = jnp.dot(onehot.astype(BF16), tri_ref[...], preferred_element_type=F32) + carry[...]
    for kk in range(TOP_K):
        gate_ref[pl.ds(kk, 1), pl.ds(base, tr)] = exps[kk] / tot
    for j in range(tr // COMBINE_TILE):
        cols = slice(j * COMBINE_TILE, (j + 1) * COMBINE_TILE)
        tc = before[:, j * COMBINE_TILE:j * COMBINE_TILE + 1]
        tcar_ref[blk * (tr // COMBINE_TILE) + j] = tc.astype(jnp.int32)
        slack = tc - WIN_ALIGN * jnp.floor(tc * (1.0 / WIN_ALIGN))
        local = before[:, cols] - tc
        for kk in range(TOP_K):
            selk = sels[kk][:, cols]
            lr = jnp.sum(jnp.where(selk, local, 0.0), axis=0, keepdims=True)
            sl = jnp.sum(jnp.where(selk, slack, 0.0), axis=0, keepdims=True)
            lrank_ref[pl.ds(kk, 1), at(j, COMBINE_TILE)] = lr.astype(jnp.int32)
            col_ref[pl.ds(kk, 1), at(j, COMBINE_TILE)] = (
                idxs[kk][:, cols] * float(WIN_ROWS) + sl + lr).astype(jnp.int32)
    for j in range(tr // DISPATCH_TILE):
        cols = slice(j * DISPATCH_TILE, (j + 1) * DISPATCH_TILE)
        local = before[:, cols] - before[:, j * DISPATCH_TILE:j * DISPATCH_TILE + 1]
        tile_cnt = jnp.broadcast_to(jnp.sum(onehot[:, cols], axis=1, keepdims=True), local.shape)
        cnt_hi = jnp.floor(tile_cnt * (1.0 / 256.0))
        cnt_lo = tile_cnt - 256.0 * cnt_hi
        run_start = (256.0 * jnp.dot(low_ref[...], cnt_hi.astype(BF16), preferred_element_type=F32)
                     + jnp.dot(low_ref[...], cnt_lo.astype(BF16), preferred_element_type=F32))
        for kk in range(TOP_K):
            lp = jnp.sum(jnp.where(sels[kk][:, cols], run_start + local, 0.0), axis=0, keepdims=True)
            lpos_ref[pl.ds(kk, 1), at(j, DISPATCH_TILE)] = lp.astype(jnp.int32)
    carry[...] = carry[...] + jnp.sum(onehot, axis=1, keepdims=True)


def _router(logits_t):
    n = logits_t.shape[1]
    tr = ROUTE_BLOCK
    assert n % tr == 0
    tri = jnp.asarray(np.triu(np.ones((tr, tr), np.float32), k=1), BF16)
    low = jnp.asarray(np.tril(np.ones((N_EXPERTS, N_EXPERTS), np.float32), k=-1), BF16)
    vm = pl.BlockSpec(memory_space=pltpu.VMEM)
    return pl.pallas_call(
        _router_kernel,
        in_specs=[vm, vm, vm],
        out_specs=[vm] * 6,
        out_shape=[jax.ShapeDtypeStruct((TOP_K, n), F32),
                   jax.ShapeDtypeStruct((TOP_K, n), jnp.int32),
                   jax.ShapeDtypeStruct((TOP_K, n), jnp.int32),
                   jax.ShapeDtypeStruct((TOP_K, n), jnp.int32),
                   jax.ShapeDtypeStruct((n // COMBINE_TILE, N_EXPERTS, 1), jnp.int32),
                   jax.ShapeDtypeStruct((N_EXPERTS, 1), jnp.int32)],
        scratch_shapes=[pltpu.VMEM((N_EXPERTS, 1), F32)],
        name="router",
    )(logits_t, tri, low)


def _dispatch_kernel(lstart_ref, cnt_ref, dst_ref, h2_ref, lpos_ref, xs_hbm, stg0, stg1, sem):
    i = pl.program_id(0)
    n_tiles = pl.num_programs(0)
    dt = h2_ref.shape[0]
    rows = dt * TOP_K
    slot = i % 2
    stgs = (stg0, stg1)
    pieces = [p for p in (256, 128, 64, 32, 16, 8, 4, 2, 1) if p <= dt]
    assert dt < 512

    def drain(s):
        pltpu.make_async_copy(stgs[s], xs_hbm.at[pl.ds(0, rows)], sem.at[s]).wait()

    def issue_runs(tile, live, s):
        for e in range(N_EXPERTS):
            n = jnp.where(live, cnt_ref[tile * N_EXPERTS + e], 0)
            src0 = lstart_ref[tile * N_EXPERTS + e]
            dst0 = dst_ref[tile * N_EXPERTS + e]
            for piece in pieces:
                off = n & ~jnp.int32(2 * piece - 1)

                @pl.when((n & piece) != 0)
                def _(off=off, piece=piece, src0=src0, dst0=dst0):
                    pltpu.make_async_copy(stgs[s].at[pl.ds(src0 + off, piece)],
                                          xs_hbm.at[pl.ds(dst0 + off, piece)], sem.at[s]).start()

    def sort_tile(s):
        rid = lax.broadcasted_iota(jnp.int32, (rows, dt), 0)
        hit = rid == lpos_ref[0:1, :]
        for kk in range(1, TOP_K):
            hit = jnp.logical_or(hit, rid == lpos_ref[kk:kk + 1, :])
        perm = jnp.where(hit, 1.0, 0.0).astype(BF16)
        srt = jnp.dot(perm, h2_ref[...], preferred_element_type=F32)
        packed = _pack_exact_bf16_pairs(srt)
        stgs[s][...] = packed.reshape(rows, PACK_CHUNKS, LANES)

    for s in range(2):
        @pl.when(slot == s)
        def _(s=s):
            @pl.when(i >= 2)
            def _():
                drain(s)

            issue_runs(jnp.maximum(i - 1, 0), i >= 1, 1 - s)
            sort_tile(s)

            @pl.when(i == n_tiles - 1)
            def _():
                issue_runs(i, True, s)
                drain(s)

                @pl.when(n_tiles >= 2)
                def _():
                    drain(1 - s)


def _dispatch(h2, lpos, lstart, cnt, dst, cap):
    n_tok = h2.shape[0]
    dt = DISPATCH_TILE
    assert n_tok % dt == 0
    grid_spec = pltpu.PrefetchScalarGridSpec(
        num_scalar_prefetch=3,
        grid=(n_tok // dt,),
        in_specs=[pl.BlockSpec((dt, D_MODEL), lambda i, a, b, c: (i, 0)),
                  pl.BlockSpec((TOP_K, dt), lambda i, a, b, c: (0, i)),
                  ],
        out_specs=pl.BlockSpec(memory_space=pl.ANY),
        scratch_shapes=[pltpu.VMEM((dt * TOP_K, PACK_CHUNKS, LANES), jnp.uint32),
                        pltpu.VMEM((dt * TOP_K, PACK_CHUNKS, LANES), jnp.uint32),
                        pltpu.SemaphoreType.DMA((2,))],
    )
    return pl.pallas_call(
        _dispatch_kernel,
        grid_spec=grid_spec,
        out_shape=jax.ShapeDtypeStruct((cap, PACK_CHUNKS, LANES), jnp.uint32),
        compiler_params=pltpu.CompilerParams(dimension_semantics=("arbitrary",),
                                             vmem_limit_bytes=VMEM_LIMIT),
        name="dispatch",
    )(lstart, cnt, dst, h2, lpos)


def _expert_kernel(n_xblocks, bexp_ref, nvalid_ref, epos_ref, elist_ref,
                   x_hbm, wgu_hbm, bgu_ref, wd_hbm, bd_ref, y_ref,
                   wgu_f32, wd_f32, wgu_bf, wd_bf, xbuf, xsem, wsem):
    i = pl.program_id(0)
    rb = y_ref.shape[0]
    nvalid = nvalid_ref[i]
    pos = epos_ref[i]
    fresh = jnp.logical_or(i == 0, pos != epos_ref[jnp.maximum(i - 1, 0)])
    slot = i % 2

    def x_copies(blk, s):
        return [pltpu.make_async_copy(x_hbm.at[pl.ds(blk * rb, rb), c, :],
                                      xbuf.at[s, :, pl.ds(c * LANES, LANES)], xsem.at[s])
                for c in range(PACK_CHUNKS)]

    def w_copies(p):
        e = elist_ref[p]
        s = p % 2
        return [pltpu.make_async_copy(wgu_hbm.at[e], wgu_f32.at[s], wsem.at[s, 0]),
                pltpu.make_async_copy(wd_hbm.at[e], wd_f32.at[s], wsem.at[s, 1])]

    @pl.when(i == 0)
    def _():
        for cp in x_copies(0, 0):
            cp.start()

        @pl.when(nvalid > 0)
        def _():
            for cp in w_copies(0):
                cp.start()

    @pl.when(i + 1 < n_xblocks)
    def _():
        for cp in x_copies(i + 1, 1 - slot):
            cp.start()

    @pl.when(jnp.logical_and(fresh, nvalid > 0))
    def _():
        @pl.when(elist_ref[pos + 1] >= 0)
        def _():
            for cp in w_copies(pos + 1):
                cp.start()

        for cp in w_copies(pos):
            cp.wait()
        ws = pos % 2
        chunk = 32

        def cast_gu(r, c):
            r0 = pl.multiple_of(r * chunk, chunk)
            wgu_bf[pl.ds(r0, chunk), :] = wgu_f32[ws, pl.ds(r0, chunk), :].astype(BF16)
            return c

        def cast_d(r, c):
            r0 = pl.multiple_of(r * chunk, chunk)
            wd_bf[pl.ds(r0, chunk), :] = wd_f32[ws, pl.ds(r0, chunk), :].astype(BF16)
            return c

        lax.fori_loop(0, D_MODEL // chunk, cast_gu, 0)
        lax.fori_loop(0, D_EXPERT // chunk, cast_d, 0)

    @pl.when(i < n_xblocks)
    def _():
        for cp in x_copies(i, slot):
            cp.wait()

    def ffn(rows):
        xw = xbuf[slot, 0:rows, :]
        xw = jnp.where(lax.broadcasted_iota(jnp.int32, xw.shape, 0) < nvalid, xw, jnp.uint32(0))
        xlo, xhi = _unpack_bf16_pairs(xw)
        x = jnp.concatenate([xlo, xhi], axis=1)
        g = jnp.dot(x, wgu_bf[:, 0:D_EXPERT], preferred_element_type=F32) + bgu_ref[0, :, 0:D_EXPERT]
        u = jnp.dot(x, wgu_bf[:, D_EXPERT:], preferred_element_type=F32) + bgu_ref[0, :, D_EXPERT:]
        g = jnp.minimum(g, SWIGLU_LIMIT)
        u = jnp.clip(u, -SWIGLU_LIMIT, SWIGLU_LIMIT)
        act = g * (1.0 / (1.0 + jnp.exp(-SWIGLU_ALPHA * g))) * (u + 1.0)
        y = jnp.dot(act.astype(BF16), wd_bf[...], preferred_element_type=F32) + bd_ref[0]
        row = lax.broadcasted_iota(jnp.int32, y.shape, 0)
        y_ref[0:rows, :] = jnp.where(row < nvalid, y, 0.0).astype(BF16)
        if rows < rb:
            y_ref[rows:, :] = jnp.zeros((rb - rows, D_MODEL), BF16)

    @pl.when(nvalid > rb // 2)
    def _():
        ffn(rb)

    @pl.when(jnp.logical_and(nvalid > 0, nvalid <= rb // 2))
    def _():
        ffn(rb // 2)

    @pl.when(nvalid == 0)
    def _():
        y_ref[...] = jnp.zeros_like(y_ref)


def _experts(xs, block_expert, nvalid, block_pos, expert_list, wgu, bgu, wd, bd):
    rb = EXPERT_ROWS
    n_xblocks = xs.shape[0] // rb
    nblk = n_xblocks + 1
    any_space = pl.BlockSpec(memory_space=pl.ANY)
    grid_spec = pltpu.PrefetchScalarGridSpec(
        num_scalar_prefetch=4,
        grid=(nblk,),
        in_specs=[
            any_space,
            any_space,
            pl.BlockSpec((1, 1, 2 * D_EXPERT), lambda i, be, nu, ep, el: (be[i], 0, 0)),
            any_space,
            pl.BlockSpec((1, 1, D_MODEL), lambda i, be, nu, ep, el: (be[i], 0, 0)),
        ],
        out_specs=pl.BlockSpec((rb, D_MODEL), lambda i, be, nu, ep, el: (i, 0)),
        scratch_shapes=[pltpu.VMEM((2, D_MODEL, 2 * D_EXPERT), F32),
                        pltpu.VMEM((2, D_EXPERT, D_MODEL), F32),
                        pltpu.VMEM((D_MODEL, 2 * D_EXPERT), BF16),
                        pltpu.VMEM((D_EXPERT, D_MODEL), BF16),
                        pltpu.VMEM((2, rb, D_MODEL // 2), jnp.uint32),
                        pltpu.SemaphoreType.DMA((2,)),
                        pltpu.SemaphoreType.DMA((2, 2))],
    )
    return pl.pallas_call(
        functools.partial(_expert_kernel, n_xblocks),
        grid_spec=grid_spec,
        out_shape=jax.ShapeDtypeStruct((nblk * rb, D_MODEL), BF16),
        compiler_params=pltpu.CompilerParams(dimension_semantics=("arbitrary",),
                                             vmem_limit_bytes=VMEM_LIMIT),
        name="experts",
    )(block_expert, nvalid, block_pos, expert_list, xs, wgu, bgu.reshape(N_EXPERTS, 1, 2 * D_EXPERT), wd,
      bd.reshape(N_EXPERTS, 1, D_MODEL))


def _combine_kernel(n_prompt_tiles, offa_ref, nchunk_ref, tail_ref, ntail_ref,
                    x1p_ref, x1s_ref, info_ref, info_next_ref, gfin_ref, yb_hbm,
                    outp_ref, outs_ref, ybuf, gbuf0, gbuf1, acc_ref, sem, tsem):
    i = pl.program_id(0)
    n_tiles = pl.num_programs(0)

    def window_copy(tile, chunk, e, slot):
        base = pl.multiple_of(offa_ref[tile * N_EXPERTS + e] + chunk * WIN, WIN_ALIGN)
        return pltpu.make_async_copy(yb_hbm.at[pl.ds(base, WIN_ROWS), :],
                                     ybuf.at[slot, pl.ds(e * WIN_ROWS, WIN_ROWS), :],
                                     sem.at[slot])

    def start_windows(tile, chunk, slot):
        for e in range(N_EXPERTS):
            window_copy(tile, chunk, e, slot).start()

    def wait_windows(slot):
        pltpu.make_async_copy(yb_hbm.at[pl.ds(0, N_EXPERTS * WIN_ROWS), :], ybuf.at[slot],
                              sem.at[slot]).wait()

    def tail_copy(tile, e, slot):
        base = pl.multiple_of(offa_ref[tile * N_EXPERTS + e] + WIN_HEAD, WIN_ALIGN)
        return pltpu.make_async_copy(yb_hbm.at[pl.ds(base, WIN_ROWS - WIN_HEAD), :],
                                     ybuf.at[slot, pl.ds(e * WIN_ROWS + WIN_HEAD, WIN_ROWS - WIN_HEAD), :],
                                     tsem.at[slot])

    def start_first_chunk(tile, slot):
        for e in range(N_EXPERTS):
            base = pl.multiple_of(offa_ref[tile * N_EXPERTS + e], WIN_ALIGN)
            pltpu.make_async_copy(yb_hbm.at[pl.ds(base, WIN_HEAD), :],
                                  ybuf.at[slot, pl.ds(e * WIN_ROWS, WIN_HEAD), :], sem.at[slot]).start()

            @pl.when(tail_ref[tile * N_EXPERTS + e] != 0)
            def _(e=e):
                tail_copy(tile, e, slot).start()

    def wait_first_chunk(tile, slot):
        pltpu.make_async_copy(yb_hbm.at[pl.ds(0, N_EXPERTS * WIN_HEAD), :],
                              ybuf.at[slot, pl.ds(0, N_EXPERTS * WIN_HEAD), :], sem.at[slot]).wait()

        def one_tail(t, c):
            tail_copy(tile, 0, slot).wait()
            return c

        lax.fori_loop(0, ntail_ref[tile], one_tail, 0)

    slot = i % 2

    @pl.when(i == 0)
    def _():
        ybuf[...] = jnp.zeros_like(ybuf)
        start_first_chunk(0, 0)

    lane = lax.broadcasted_iota(jnp.int32, (COMBINE_TILE, N_EXPERTS * WIN_ROWS), 1)

    def gate_matrix(ref, chunk):
        g = jnp.zeros(lane.shape, F32)
        for kk in range(TOP_K):
            lr = ref[:, TOP_K + kk:TOP_K + kk + 1]
            in_chunk = jnp.logical_and(lr >= chunk * WIN, lr < chunk * WIN + WIN)
            colk = jnp.where(in_chunk, ref[:, kk:kk + 1] - chunk * WIN, -1.0).astype(jnp.int32)
            g = jnp.where(lane == colk, ref[:, 2 * TOP_K + kk:2 * TOP_K + kk + 1], g)
        return g.astype(BF16)

    def moe_rows(gm, buf):
        return jnp.dot(gm, ybuf[buf], preferred_element_type=F32)

    @pl.when(i == 0)
    def _():
        gbuf0[...] = gate_matrix(info_ref, 0)

    def main(s):
        start_first_chunk(jnp.minimum(i + 1, n_tiles - 1), 1 - s)
        wait_first_chunk(i, s)
        g_cur, g_nxt = (gbuf0, gbuf1) if s == 0 else (gbuf1, gbuf0)
        acc_ref[...] = moe_rows(g_cur[...], s)
        g_nxt[...] = gate_matrix(info_next_ref, 0)

    for s in range(2):
        @pl.when(slot == s)
        def _(s=s):
            main(s)

    @pl.when(i == n_tiles - 1)
    def _():
        for s in range(2):
            @pl.when(slot == s)
            def _(s=s):
                wait_first_chunk(i, 1 - s)

    def extra_chunk(j, c):
        start_windows(i, j, 2)
        wait_windows(2)
        acc_ref[...] += moe_rows(gate_matrix(info_ref, j), 2)
        return c

    lax.fori_loop(1, nchunk_ref[i], extra_chunk, 0)

    @pl.when(i < n_prompt_tiles)
    def _():
        outp_ref[...] = _rms(x1p_ref[...] + acc_ref[...], gfin_ref[...])

    @pl.when(i >= n_prompt_tiles)
    def _():
        outs_ref[...] = _rms(x1s_ref[...] + acc_ref[...], gfin_ref[...])


def _combine(x1_p, x1_s, tok_info, gfin, yb, offa, nchunk, need_tail, ntail):
    ct = COMBINE_TILE
    n_p, n_s = x1_p.shape[0] // ct, x1_s.shape[0] // ct
    assert x1_p.shape[0] % ct == 0 and x1_s.shape[0] % ct == 0 and n_s >= 1
    n_info = tok_info.shape[1]
    grid_spec = pltpu.PrefetchScalarGridSpec(
        num_scalar_prefetch=4,
        grid=(n_p + n_s,),
        in_specs=[
            pl.BlockSpec((ct, D_MODEL), lambda i, *_: (jnp.minimum(i, n_p - 1), 0)),
            pl.BlockSpec((ct, D_MODEL), lambda i, *_: (jnp.maximum(i - n_p, 0), 0)),
            pl.BlockSpec((ct, n_info), lambda i, *_: (i, 0)),
            pl.BlockSpec((ct, n_info), lambda i, *_: (jnp.minimum(i + 1, n_p + n_s - 1), 0)),
            pl.BlockSpec((1, D_MODEL), lambda i, *_: (0, 0)),
            pl.BlockSpec(memory_space=pl.ANY),
        ],
        out_specs=[
            pl.BlockSpec((ct, D_MODEL), lambda i, *_: (jnp.minimum(i, n_p - 1), 0)),
            pl.BlockSpec((ct, D_MODEL), lambda i, *_: (jnp.maximum(i - n_p, 0), 0)),
        ],
        scratch_shapes=[pltpu.VMEM((3, N_EXPERTS * WIN_ROWS, D_MODEL), BF16),
                        pltpu.VMEM((ct, N_EXPERTS * WIN_ROWS), BF16),
                        pltpu.VMEM((ct, N_EXPERTS * WIN_ROWS), BF16),
                        pltpu.VMEM((ct, D_MODEL), F32),
                        pltpu.SemaphoreType.DMA((3,)),
                        pltpu.SemaphoreType.DMA((2,))],
    )
    return pl.pallas_call(
        functools.partial(_combine_kernel, n_p),
        grid_spec=grid_spec,
        out_shape=[jax.ShapeDtypeStruct(x1_p.shape, F32), jax.ShapeDtypeStruct(x1_s.shape, F32)],
        compiler_params=pltpu.CompilerParams(dimension_semantics=("arbitrary",),
                                             vmem_limit_bytes=VMEM_LIMIT),
        name="combine",
    )(offa, nchunk, need_tail, ntail, x1_p, x1_s, tok_info, tok_info, gfin, yb)


def kernel(x_prompt, x_sample, cache_k, cache_v, state_pool, meta_tokens, norm_attn, w_in, attn_sinks,
           w_pool, pool_scale, w_out, norm_ffn, router_w, router_b, w_gate_up, b_gate_up, w_down, b_down,
           norm_final):
    assert w_in.shape[0] == 1, "single-layer trunk"
    bsz, seq, _ = x_prompt.shape
    assert bsz == 1
    nb = x_sample.shape[0]
    n_tok = seq + nb
    gattn = norm_attn[0].reshape(1, D_MODEL)
    gffn = norm_ffn[0].reshape(1, D_MODEL)
    win = w_in[0].astype(BF16)
    wpool = w_pool[0].astype(BF16)
    wout = w_out[0].astype(BF16)
    pscale = pool_scale[0].reshape(1, POOL_WIDTH)
    rw_t = router_w[0].T
    rw_hi = rw_t.astype(BF16)
    rwt = jnp.stack([rw_hi, (rw_t - rw_hi.astype(F32)).astype(BF16)])
    rb = router_b[0].reshape(N_EXPERTS, 1)
    sinks = attn_sinks[0]

    cache_rows = N_META + WINDOW
    to_batch_minor = lambda c: jnp.transpose(c[0], (1, 2, 3, 0)).reshape(cache_rows, KV_WIDTH, nb)
    from_batch_minor = lambda c: jnp.transpose(
        c.reshape(cache_rows, N_KV_HEADS, HEAD_DIM, nb), (3, 0, 1, 2))[None]
    (x1_s, h2_s, lgt_s, kout_t, vout_t, pool_t) = _sample_mixer(
        x_sample[:, 0], to_batch_minor(cache_k), to_batch_minor(cache_v),
        jnp.transpose(state_pool[0], (1, 0, 2)), gattn, win, wpool, pscale, wout, gffn, rwt, rb, sinks)
    (x1_p, h2_all, lgt_all, kmeta, vmeta, ktail, vtail, ptail) = _prompt_mixer(
        x_prompt[0], meta_tokens, gattn, win, wpool, pscale, wout, gffn, rwt, rb, sinks, h2_s, lgt_s)

    gates, col, lrank, lpos, tcar, counts = _router(lgt_all)
    counts = counts[:, 0]
    tcar = tcar[:, :, 0]
    rbk = EXPERT_ROWS
    eids = jnp.arange(N_EXPERTS, dtype=jnp.int32)
    earlier = eids[None, :] < eids[:, None]
    excl_sum = lambda a: jnp.sum(jnp.where(earlier, a[..., None, :], 0), axis=-1)
    padded = (counts + rbk - 1) // rbk * rbk
    pad_start = excl_sum(padded).astype(jnp.int32)
    pad_end = pad_start + padded
    nblk = -(-(n_tok * TOP_K) // rbk) + N_EXPERTS
    cap = nblk * rbk
    block_start = jnp.arange(nblk + 1, dtype=jnp.int32) * rbk
    owns = (pad_start[None, :] <= block_start[:, None]) & (block_start[:, None] < pad_end[None, :])
    nvalid = jnp.sum(jnp.where(owns, jnp.clip(counts[None, :] - (block_start[:, None] - pad_start[None, :]),
                                              0, rbk), 0), axis=1).astype(jnp.int32)
    has_rows = counts > 0
    last_e = jnp.max(jnp.where(has_rows, eids, 0))
    block_expert = jnp.where(jnp.any(owns, axis=1), jnp.sum(jnp.where(owns, eids[None, :], 0), axis=1),
                             last_e).astype(jnp.int32)

    run_len = jnp.concatenate([tcar[1:], counts[None, :]], axis=0) - tcar
    dcar = tcar[::DISPATCH_TILE // COMBINE_TILE]
    drun_len = jnp.concatenate([dcar[1:], counts[None, :]], axis=0) - dcar
    flat = lambda a: a.astype(jnp.int32).reshape(-1)
    xs = _dispatch(h2_all, lpos, flat(excl_sum(drun_len)), flat(drun_len), flat(pad_start[None, :] + dcar),
                   cap)
    expert_pos = excl_sum(has_rows.astype(jnp.int32))
    at_pos = has_rows[None, :] & (expert_pos[None, :] == jnp.arange(N_EXPERTS + 1, dtype=jnp.int32)[:, None])
    expert_list = jnp.where(jnp.any(at_pos, axis=1), jnp.sum(jnp.where(at_pos, eids[None, :], 0), axis=1),
                            -1).astype(jnp.int32)
    block_pos = jnp.sum(jnp.where(block_expert[:, None] == eids[None, :], expert_pos[None, :], 0),
                        axis=1).astype(jnp.int32)
    yb = _experts(xs, block_expert, nvalid, block_pos, expert_list,
                  w_gate_up[0], b_gate_up[0], w_down[0], b_down[0])

    offa = (pad_start[None, :] + (tcar - tcar % WIN_ALIGN)).astype(jnp.int32).reshape(-1)
    nchunk = jnp.maximum(jnp.max((run_len + WIN - 1) // WIN, axis=1), 1).astype(jnp.int32)
    need_tail = (tcar % WIN_ALIGN + jnp.minimum(run_len, WIN)) > WIN_HEAD
    ntail = jnp.sum(need_tail, axis=1).astype(jnp.int32)
    gfin = norm_final.reshape(1, D_MODEL)
    tok_info = jnp.concatenate([col.astype(F32), lrank.astype(F32), gates], axis=0).T
    y_prompt, y_sample = _combine(x1_p, x1_s, tok_info, gfin, yb, offa, nchunk,
                                  need_tail.astype(jnp.int32).reshape(-1), ntail)

    kv_shape = (1, 1, N_META + WINDOW, N_KV_HEADS, HEAD_DIM)
    new_k_p = jnp.concatenate([kmeta, ktail], axis=0).reshape(kv_shape)
    new_v_p = jnp.concatenate([vmeta, vtail], axis=0).reshape(kv_shape)
    new_pool_p = ptail[16 - POOL_STATE:].reshape(1, 1, POOL_STATE, POOL_WIDTH)
    new_k_s = from_batch_minor(kout_t)
    new_v_s = from_batch_minor(vout_t)
    new_pool_s = jnp.transpose(pool_t, (1, 0, 2))[None]
    return (y_prompt[None], y_sample[:, None], new_k_p, new_v_p, new_pool_p, new_k_s, new_v_s, new_pool_s)
```

```python
import functools

import jax
import jax.numpy as jnp
import numpy as np
from jax import lax
from jax.experimental import pallas as pl
from jax.experimental.pallas import tpu as pltpu

F32 = jnp.float32
BF16 = jnp.bfloat16

D_MODEL = 1024
N_META = 16
N_HEADS = 8
HEAD_DIM = 64
N_KV_HEADS = 2
GQA_GROUP = N_HEADS // N_KV_HEADS
ATTN_WIDTH = N_HEADS * HEAD_DIM
KV_WIDTH = N_KV_HEADS * HEAD_DIM
WINDOW = 128
POOL_WIDTH = D_MODEL - ATTN_WIDTH
POOL_WINDOWS = (2, 4, 8, 16)
POOL_GROUP_DIM = POOL_WIDTH // len(POOL_WINDOWS)
POOL_STATE = max(POOL_WINDOWS) - 1
N_EXPERTS = 32
TOP_K = 4
D_EXPERT = D_MODEL
SWIGLU_ALPHA = 1.702
SWIGLU_LIMIT = 7.0
NORM_EPS = 1e-5

LANES = 128
QSUB = 64
KEYS_SUB = QSUB + WINDOW
META_PAD = 64
NKEY = META_PAD + KEYS_SUB
MASKED = -1e30
PROMPT_BLOCK = 1024
ROUTE_BLOCK = 384
EXPERT_ROWS = 512
EXPERT_PATHS = 4
DISPATCH_TILE = 384
PACK_CHUNKS = D_MODEL // 2 // LANES
COMBINE_TILE = 128
WIN = 32
WIN_ALIGN = 16
WIN_ROWS = WIN + WIN_ALIGN
WIN_HEAD = 32
VMEM_LIMIT = 56 * 1024 * 1024


def _rms(x, g):
    return x * lax.rsqrt(jnp.mean(x * x, axis=-1, keepdims=True) + NORM_EPS) * g


def _router_logits(rwt_ref, h2, h2_hi):
    nt = (((1,), (1,)), ((), ()))
    h2_lo = (h2 - h2_hi.astype(F32)).astype(BF16)
    return (lax.dot_general(rwt_ref[0], h2_hi, nt, preferred_element_type=F32)
            + lax.dot_general(rwt_ref[0], h2_lo, nt, preferred_element_type=F32)
            + lax.dot_general(rwt_ref[1], h2_hi, nt, preferred_element_type=F32))


def _dup_halves(a):
    lane = lax.broadcasted_iota(jnp.int32, a.shape, 1)
    r = pltpu.roll(a, HEAD_DIM, axis=1)
    lo = lane < HEAD_DIM
    return jnp.where(lo, a, r), jnp.where(lo, r, a)


def _pool_means(pext_ref, n):
    outs = []
    for gi, w in enumerate(POOL_WINDOWS):
        xg = pext_ref[:, gi * POOL_GROUP_DIM:(gi + 1) * POOL_GROUP_DIM]
        s = xg
        sh = 1
        while sh < w:
            s = s + pltpu.roll(s, sh, axis=0)
            sh *= 2
        outs.append(s[16:] * (1.0 / w) - xg[16:])
    return outs


def _pack_exact_bf16_pairs(h):
    m = h.shape[1] // 2
    return (lax.shift_right_logical(pltpu.bitcast(h[:, :m], jnp.uint32), jnp.uint32(16))
            | (pltpu.bitcast(h[:, m:], jnp.uint32) & jnp.uint32(0xFFFF0000)))


def _unpack_bf16_pairs(w):
    lo = pltpu.bitcast(lax.shift_left(w, jnp.uint32(16)), F32).astype(BF16)
    hi = pltpu.bitcast(w & jnp.uint32(0xFFFF0000), F32).astype(BF16)
    return lo, hi


def _prompt_kernel(x_ref, meta_ref, gattn_ref, win_ref, wpool_ref, pscale_ref, wout_ref, gffn_ref,
                   rwt_ref, rb_ref, sink_ref, tbl_ref, tail_h2_ref, tail_lgt_ref,
                   x1_ref, h2_ref, lgt_ref, kmeta_ref, vmeta_ref, ktail_ref, vtail_ref, ptail_ref,
                   k2buf, v2buf, km2, vm2, qbuf, obuf, pext):
    pid = pl.program_id(0)
    n_main = pl.num_programs(0) - 1
    refs = (x_ref, meta_ref, gattn_ref, win_ref, wpool_ref, pscale_ref, wout_ref, gffn_ref,
            rwt_ref, rb_ref, sink_ref, tbl_ref,
            x1_ref, h2_ref, lgt_ref, kmeta_ref, vmeta_ref, ktail_ref, vtail_ref, ptail_ref,
            k2buf, v2buf, km2, vm2, qbuf, obuf, pext)

    @pl.when(pid < n_main)
    def _():
        _prompt_block(*refs)

    @pl.when(pid == n_main)
    def _():
        h2_ref[0:tail_h2_ref.shape[0], :] = tail_h2_ref[...]
        lgt_ref[:, 0:tail_lgt_ref.shape[1]] = tail_lgt_ref[...]


def _prompt_block(x_ref, meta_ref, gattn_ref, win_ref, wpool_ref, pscale_ref, wout_ref, gffn_ref,
                  rwt_ref, rb_ref, sink_ref, tbl_ref,
                  x1_ref, h2_ref, lgt_ref, kmeta_ref, vmeta_ref, ktail_ref, vtail_ref, ptail_ref,
                  k2buf, v2buf, km2, vm2, qbuf, obuf, pext):
    tb = x_ref.shape[0]
    pid = pl.program_id(0)

    @pl.when(pid == 0)
    def _():
        hm = _rms(meta_ref[...], gattn_ref[...]).astype(BF16)
        km = jnp.dot(hm, win_ref[:, ATTN_WIDTH:ATTN_WIDTH + KV_WIDTH], preferred_element_type=F32)
        vm = jnp.dot(hm, win_ref[:, ATTN_WIDTH + KV_WIDTH:ATTN_WIDTH + 2 * KV_WIDTH],
                     preferred_element_type=F32)
        pm = jnp.dot(hm, win_ref[:, ATTN_WIDTH + 2 * KV_WIDTH:], preferred_element_type=F32)
        kmeta_ref[...] = km
        vmeta_ref[...] = vm
        zpad = jnp.zeros((META_PAD - N_META, LANES), F32)
        k0, k1 = _dup_halves(jnp.concatenate([km, zpad], axis=0))
        v0, v1 = _dup_halves(jnp.concatenate([vm, zpad], axis=0))
        km2[0] = k0.astype(BF16)
        km2[1] = k1.astype(BF16)
        vm2[0, :, 0:LANES] = v0.astype(BF16)
        vm2[1, :, 0:LANES] = v1.astype(BF16)
        vm2[:, :, LANES:] = jnp.ones((N_KV_HEADS, META_PAD, LANES), BF16)
        k2buf[:, 0:WINDOW, :] = jnp.zeros((2, WINDOW, LANES), BF16)
        v2buf[:, 0:WINDOW, 0:LANES] = jnp.zeros((2, WINDOW, LANES), BF16)
        v2buf[:, :, LANES:] = jnp.ones((N_KV_HEADS, WINDOW + tb, LANES), BF16)
        pext[0:16, :] = pm

    h = _rms(x_ref[...], gattn_ref[...]).astype(BF16)
    q = jnp.dot(h, win_ref[:, 0:ATTN_WIDTH], preferred_element_type=F32) * (HEAD_DIM ** -0.5)
    lane_t = lax.broadcasted_iota(jnp.int32, (tb, LANES), 1)
    for c in range(N_HEADS // 2):
        tile = q[:, c * LANES:(c + 1) * LANES]
        for a in range(2):
            keep = (lane_t < HEAD_DIM) if a == 0 else (lane_t >= HEAD_DIM)
            piece = jnp.where(keep, tile, 0.0).astype(BF16).reshape(tb // QSUB, QSUB, LANES)
            row = ((c % 2) * 2 + a) * QSUB
            qbuf[c // 2, :, row:row + QSUB, :] = piece
    k = jnp.dot(h, win_ref[:, ATTN_WIDTH:ATTN_WIDTH + KV_WIDTH], preferred_element_type=F32)
    v = jnp.dot(h, win_ref[:, ATTN_WIDTH + KV_WIDTH:ATTN_WIDTH + 2 * KV_WIDTH], preferred_element_type=F32)
    p = jnp.dot(h, win_ref[:, ATTN_WIDTH + 2 * KV_WIDTH:], preferred_element_type=F32)
    ktail_ref[...] = k[tb - WINDOW:]
    vtail_ref[...] = v[tb - WINDOW:]
    ptail_ref[...] = p[tb - 16:]
    k0, k1 = _dup_halves(k)
    v0, v1 = _dup_halves(v)
    k2buf[0, WINDOW:, :] = k0.astype(BF16)
    k2buf[1, WINDOW:, :] = k1.astype(BF16)
    v2buf[0, WINDOW:, 0:LANES] = v0.astype(BF16)
    v2buf[1, WINDOW:, 0:LANES] = v1.astype(BF16)
    pext[16:, :] = p

    lane_q = lax.broadcasted_iota(jnp.int32, (QSUB, LANES), 1)
    lo_q = lane_q < HEAD_DIM

    for u in range(tb // QSUB):
        r0 = u * QSUB
        sel = jnp.where(pid == 0, u + 1, 0) if u < WINDOW // QSUB else 0
        for g in range(N_KV_HEADS):
            qm = qbuf[g, u]
            kwin = jnp.concatenate([km2[g], k2buf[g, r0:r0 + KEYS_SUB, :]], axis=0)
            vwin = jnp.concatenate([vm2[g], v2buf[g, r0:r0 + KEYS_SUB, :]], axis=0)
            s = lax.dot_general(qm, kwin, (((1,), (1,)), ((), ())), preferred_element_type=F32)
            s = s + tbl_ref[sel, g]
            sink = sink_ref[g]
            m = jnp.maximum(jnp.max(s, axis=1, keepdims=True), sink)
            e = jnp.exp(s - m).astype(BF16)
            r = jnp.dot(e, vwin, preferred_element_type=F32)
            o = r[:, 0:LANES] / (r[:, LANES:] + jnp.exp(sink - m))
            o0 = jnp.where(lo_q, o[0:QSUB], o[QSUB:2 * QSUB])
            o1 = jnp.where(lo_q, o[2 * QSUB:3 * QSUB], o[3 * QSUB:])
            obuf[r0:r0 + QSUB, (2 * g) * LANES:(2 * g + 1) * LANES] = o0.astype(BF16)
            obuf[r0:r0 + QSUB, (2 * g + 1) * LANES:(2 * g + 2) * LANES] = o1.astype(BF16)

    pooled = _pool_means(pext, tb)
    for gi in range(len(POOL_WINDOWS)):
        y = jnp.dot(pooled[gi].astype(BF16), wpool_ref[gi], preferred_element_type=F32)
        y = y * pscale_ref[:, gi * POOL_GROUP_DIM:(gi + 1) * POOL_GROUP_DIM]
        obuf[:, ATTN_WIDTH + gi * POOL_GROUP_DIM:ATTN_WIDTH + (gi + 1) * POOL_GROUP_DIM] = y.astype(BF16)

    k2buf[:, 0:WINDOW, :] = k2buf[:, tb:tb + WINDOW, :]
    v2buf[:, 0:WINDOW, 0:LANES] = v2buf[:, tb:tb + WINDOW, 0:LANES]
    pext[0:16, :] = pext[tb:tb + 16, :]

    x1 = x_ref[...] + jnp.dot(obuf[...], wout_ref[...], preferred_element_type=F32)
    x1_ref[...] = x1
    h2 = _rms(x1, gffn_ref[...])
    h2_hi = h2.astype(BF16)
    h2_ref[...] = h2_hi
    lgt_ref[...] = _router_logits(rwt_ref, h2, h2_hi) + rb_ref[...]


def _attn_tables(sinks):
    i = np.arange(QSUB)[:, None]
    j = np.arange(NKEY)[None, :]
    jb = j - META_PAD
    rel = i + WINDOW - jb
    band_ok = (jb >= 0) & (rel >= 0) & (rel <= WINDOW)
    meta_ok = (j < N_META) & (i >= 0)
    slopes = np.exp2(-8.0 * np.arange(1, N_HEADS + 1) / N_HEADS)
    tbl = np.empty((3, N_KV_HEADS, GQA_GROUP * QSUB, NKEY), np.float32)
    for var in range(3):
        ok = band_ok if var == 0 else band_ok & (jb >= WINDOW - (var - 1) * QSUB)
        for g in range(N_KV_HEADS):
            for a in range(GQA_GROUP):
                hd = g * GQA_GROUP + a
                bias = np.where(ok, -slopes[hd] * rel, MASKED)
                bias = np.where(meta_ok, 0.0, bias)
                tbl[var, g, a * QSUB:(a + 1) * QSUB] = bias
    sink_col = jnp.repeat(sinks.astype(F32).reshape(N_KV_HEADS, GQA_GROUP, 1), QSUB, axis=2)
    return jnp.asarray(tbl), sink_col.reshape(N_KV_HEADS, GQA_GROUP * QSUB, 1)


def _prompt_mixer(x, meta, gattn, win, wpool, pscale, wout, gffn, rwt, rb, sinks, tail_h2, tail_lgt):
    seq = x.shape[0]
    tb = PROMPT_BLOCK
    n_tail = tail_h2.shape[0]
    assert seq % tb == 0 and tb % WINDOW == 0 and n_tail <= tb
    nblk = seq // tb
    n_tok = seq + n_tail
    tbl, sink_col = _attn_tables(sinks)
    full = lambda *shape: pl.BlockSpec(shape, lambda i: (0,) * len(shape))
    main = lambda i: (jnp.minimum(i, nblk - 1), 0)
    in_width = win.shape[1]
    return pl.pallas_call(
        _prompt_kernel,
        grid=(nblk + 1,),
        in_specs=[
            pl.BlockSpec((tb, D_MODEL), main),
            full(N_META, D_MODEL), full(1, D_MODEL), full(D_MODEL, in_width),
            full(len(POOL_WINDOWS), POOL_GROUP_DIM, POOL_GROUP_DIM), full(1, POOL_WIDTH),
            full(D_MODEL, D_MODEL), full(1, D_MODEL), full(2, N_EXPERTS, D_MODEL), full(N_EXPERTS, 1),
            full(N_KV_HEADS, GQA_GROUP * QSUB, 1), full(3, N_KV_HEADS, GQA_GROUP * QSUB, NKEY),
            full(n_tail, D_MODEL), full(N_EXPERTS, n_tail),
        ],
        out_specs=[
            pl.BlockSpec((tb, D_MODEL), main),
            pl.BlockSpec((tb, D_MODEL), lambda i: (i, 0)),
            pl.BlockSpec((N_EXPERTS, tb), lambda i: (0, i)),
            full(N_META, KV_WIDTH), full(N_META, KV_WIDTH),
            full(WINDOW, KV_WIDTH), full(WINDOW, KV_WIDTH), full(16, POOL_WIDTH),
        ],
        out_shape=[
            jax.ShapeDtypeStruct((seq, D_MODEL), F32),
            jax.ShapeDtypeStruct((n_tok, D_MODEL), BF16),
            jax.ShapeDtypeStruct((N_EXPERTS, n_tok), F32),
            jax.ShapeDtypeStruct((N_META, KV_WIDTH), F32),
            jax.ShapeDtypeStruct((N_META, KV_WIDTH), F32),
            jax.ShapeDtypeStruct((WINDOW, KV_WIDTH), F32),
            jax.ShapeDtypeStruct((WINDOW, KV_WIDTH), F32),
            jax.ShapeDtypeStruct((16, POOL_WIDTH), F32),
        ],
        scratch_shapes=[
            pltpu.VMEM((N_KV_HEADS, WINDOW + tb, LANES), BF16),
            pltpu.VMEM((N_KV_HEADS, WINDOW + tb, 2 * LANES), BF16),
            pltpu.VMEM((N_KV_HEADS, META_PAD, LANES), BF16),
            pltpu.VMEM((N_KV_HEADS, META_PAD, 2 * LANES), BF16),
            pltpu.VMEM((N_KV_HEADS, tb // QSUB, GQA_GROUP * QSUB, LANES), BF16),
            pltpu.VMEM((tb, D_MODEL), BF16),
            pltpu.VMEM((16 + tb, POOL_WIDTH), F32),
        ],
        compiler_params=pltpu.CompilerParams(dimension_semantics=("arbitrary",),
                                             vmem_limit_bytes=VMEM_LIMIT),
        name="prompt_mixer",
    )(x, meta, gattn, win, wpool, pscale, wout, gffn, rwt, rb, sink_col, tbl, tail_h2, tail_lgt)


def _sample_kernel(x_ref, ck_ref, cv_ref, sp_ref, gattn_ref, win_ref, wqkv_t_ref, wpool_ref, pscale_ref,
                   wout_ref, gffn_ref, rwt_ref, rb_ref, sink_ref, bias_ref,
                   x1_ref, h2_ref, lgt_ref, kout_ref, vout_ref, pnew_ref,
                   qt_ref, sc_ref, ot_ref, obuf):
    nb = x_ref.shape[0]
    rows = ck_ref.shape[0]
    x = x_ref[...]
    h = _rms(x, gattn_ref[...]).astype(BF16)
    nt = (((1,), (1,)), ((), ()))
    qkv_t = lax.dot_general(wqkv_t_ref[...], h, nt, preferred_element_type=F32)
    qt_ref[...] = qkv_t[0:ATTN_WIDTH] * (HEAD_DIM ** -0.5)
    kt = qkv_t[ATTN_WIDTH:ATTN_WIDTH + KV_WIDTH]
    vt = qkv_t[ATTN_WIDTH + KV_WIDTH:]
    p = jnp.dot(h, win_ref[:, ATTN_WIDTH + 2 * KV_WIDTH:], preferred_element_type=F32)
    pnew_ref[0:POOL_STATE - 1] = sp_ref[1:POOL_STATE]
    pnew_ref[POOL_STATE - 1] = p

    kout_ref[0:N_META] = ck_ref[0:N_META]
    vout_ref[0:N_META] = cv_ref[0:N_META]
    kout_ref[N_META:rows - 1] = ck_ref[N_META + 1:rows]
    vout_ref[N_META:rows - 1] = cv_ref[N_META + 1:rows]
    kout_ref[rows - 1] = kt
    vout_ref[rows - 1] = vt

    def kv_rows(hd):
        g = hd // GQA_GROUP
        return slice(g * HEAD_DIM, (g + 1) * HEAD_DIM)

    def score_row(key_tile, hd):
        prod = qt_ref[hd * HEAD_DIM:(hd + 1) * HEAD_DIM, :] * key_tile[kv_rows(hd), :]
        return jnp.sum(prod, axis=0, keepdims=True)

    def score_pass(s, carry):
        key_tile = ck_ref[s]
        for hd in range(N_HEADS):
            sc_ref[hd, pl.ds(s, 1), :] = score_row(key_tile, hd)
        return carry

    lax.fori_loop(0, rows, score_pass, 0)
    pad_rows = sc_ref.shape[1] - rows - 1
    for hd in range(N_HEADS):
        sc_ref[hd, rows:rows + 1, :] = score_row(kt, hd)
        sc_ref[hd, rows + 1:, :] = jnp.full((pad_rows, nb), MASKED, F32)

    for hd in range(N_HEADS):
        s = sc_ref[hd] + bias_ref[hd]
        sink = sink_ref[hd]
        m = jnp.maximum(jnp.max(s, axis=0, keepdims=True), sink)
        e = jnp.exp(s - m)
        sc_ref[hd] = e * (1.0 / (jnp.sum(e, axis=0, keepdims=True) + jnp.exp(sink - m)))

    for half in range(2):
        heads = range(half * (N_HEADS // 2), (half + 1) * (N_HEADS // 2))

        def value_pass(s, accs, heads=heads):
            val_tile = cv_ref[s]
            return tuple(acc + sc_ref[hd, pl.ds(s, 1), :] * val_tile[kv_rows(hd), :]
                         for acc, hd in zip(accs, heads))

        accs = lax.fori_loop(0, rows, value_pass,
                             tuple(jnp.zeros((HEAD_DIM, nb), F32) for _ in heads))
        for acc, hd in zip(accs, heads):
            ot_ref[hd * HEAD_DIM:(hd + 1) * HEAD_DIM, :] = acc + sc_ref[hd, rows:rows + 1, :] * vt[kv_rows(hd), :]

    for c in range(ATTN_WIDTH // LANES):
        obuf[:, c * LANES:(c + 1) * LANES] = ot_ref[c * LANES:(c + 1) * LANES, :].T.astype(BF16)

    for gi, w in enumerate(POOL_WINDOWS):
        cols = slice(gi * POOL_GROUP_DIM, (gi + 1) * POOL_GROUP_DIM)
        pg = p[:, cols]
        acc = pg
        for d in range(1, w):
            acc = acc + sp_ref[POOL_STATE - d, :, cols]
        pooled = acc * (1.0 / w) - pg
        y = jnp.dot(pooled.astype(BF16), wpool_ref[gi], preferred_element_type=F32) * pscale_ref[:, cols]
        obuf[:, ATTN_WIDTH + gi * POOL_GROUP_DIM:ATTN_WIDTH + (gi + 1) * POOL_GROUP_DIM] = y.astype(BF16)

    x1 = x + jnp.dot(obuf[...], wout_ref[...], preferred_element_type=F32)
    x1_ref[...] = x1
    h2 = _rms(x1, gffn_ref[...])
    h2_hi = h2.astype(BF16)
    h2_ref[...] = h2_hi
    lgt_ref[...] = _router_logits(rwt_ref, h2, h2_hi) + rb_ref[...]


def _sample_mixer(x, ck_t, cv_t, sp, gattn, win, wpool, pscale, wout, gffn, rwt, rb, sinks):
    nb = x.shape[0]
    rows = ck_t.shape[0]
    assert nb == LANES and rows == N_META + WINDOW
    n_keys = -(-(rows + 1) // 8) * 8
    slopes = np.exp2(-8.0 * np.arange(1, N_HEADS + 1) / N_HEADS)
    dist = np.concatenate([np.zeros(N_META), WINDOW - np.arange(WINDOW), np.zeros(1)])
    bias = np.full((N_HEADS, n_keys, 1), MASKED, np.float32)
    bias[:, :rows + 1, 0] = -slopes[:, None] * dist[None, :]
    vm = pl.BlockSpec(memory_space=pltpu.VMEM)
    return pl.pallas_call(
        _sample_kernel,
        in_specs=[vm] * 15,
        out_specs=[vm] * 6,
        out_shape=[
            jax.ShapeDtypeStruct((nb, D_MODEL), F32),
            jax.ShapeDtypeStruct((nb, D_MODEL), BF16),
            jax.ShapeDtypeStruct((N_EXPERTS, nb), F32),
            jax.ShapeDtypeStruct(ck_t.shape, F32),
            jax.ShapeDtypeStruct(cv_t.shape, F32),
            jax.ShapeDtypeStruct(sp.shape, F32),
        ],
        scratch_shapes=[
            pltpu.VMEM((ATTN_WIDTH, nb), F32),
            pltpu.VMEM((N_HEADS, n_keys, nb), F32),
            pltpu.VMEM((ATTN_WIDTH, nb), F32),
            pltpu.VMEM((nb, D_MODEL), BF16),
        ],
        compiler_params=pltpu.CompilerParams(vmem_limit_bytes=VMEM_LIMIT),
        name="sample_mixer",
    )(x, ck_t, cv_t, sp, gattn, win, win[:, 0:ATTN_WIDTH + 2 * KV_WIDTH].T, wpool, pscale, wout, gffn, rwt, rb,
      sinks.astype(F32).reshape(N_HEADS, 1, 1), jnp.asarray(bias))


def _router_kernel(lg_ref, tri_ref, low_ref, gate_ref, col_ref, lrank_ref, lpos_ref,
                   tcar_ref, cnt_ref, carry):
    tr = ROUTE_BLOCK
    carry[...] = jnp.zeros_like(carry)

    def route_block(blk, c):
        _route_block(blk, tr, lg_ref, tri_ref, low_ref, gate_ref, col_ref, lrank_ref, lpos_ref, tcar_ref, carry)
        return c

    lax.fori_loop(0, lg_ref.shape[1] // tr, route_block, 0)
    cnt_ref[...] = carry[...].astype(jnp.int32)


def _route_block(blk, tr, lg_ref, tri_ref, low_ref, gate_ref, col_ref, lrank_ref, lpos_ref, tcar_ref, carry):
    base = pl.multiple_of(blk * tr, LANES)

    def at(j, width):
        return pl.ds(pl.multiple_of(base + j * width, LANES), width)

    work = lg_ref[:, pl.ds(base, tr)]
    eio = lax.broadcasted_iota(jnp.int32, work.shape, 0).astype(F32)
    sels, vals, idxs = [], [], []
    for _k in range(TOP_K):
        mx = jnp.max(work, axis=0, keepdims=True)
        idx = jnp.min(jnp.where(work == mx, eio, float(N_EXPERTS)), axis=0, keepdims=True)
        sel = eio == idx
        sels.append(sel)
        vals.append(mx)
        idxs.append(idx)
        work = jnp.where(sel, -jnp.inf, work)
    exps = [jnp.exp(vk - vals[0]) for vk in vals]
    tot = exps[0] + exps[1] + exps[2] + exps[3]
    onehot = jnp.zeros(work.shape, F32)
    for sel in sels:
        onehot = onehot + sel.astype(F32)
    before = jnp.dot(onehot.astype(BF16), tri_ref[...], preferred_element_type=F32) + carry[...]
    for kk in range(TOP_K):
        gate_ref[pl.ds(kk, 1), pl.ds(base, tr)] = exps[kk] / tot
    for j in range(tr // COMBINE_TILE):
        cols = slice(j * COMBINE_TILE, (j + 1) * COMBINE_TILE)
        tc = before[:, j * COMBINE_TILE:j * COMBINE_TILE + 1]
        tcar_ref[blk * (tr // COMBINE_TILE) + j] = tc.astype(jnp.int32)
        slack = tc - WIN_ALIGN * jnp.floor(tc * (1.0 / WIN_ALIGN))
        local = before[:, cols] - tc
        for kk in range(TOP_K):
            selk = sels[kk][:, cols]
            lr = jnp.sum(jnp.where(selk, local, 0.0), axis=0, keepdims=True)
            sl = jnp.sum(jnp.where(selk, slack, 0.0), axis=0, keepdims=True)
            lrank_ref[pl.ds(kk, 1), at(j, COMBINE_TILE)] = lr.astype(jnp.int32)
            col_ref[pl.ds(kk, 1), at(j, COMBINE_TILE)] = (
                idxs[kk][:, cols] * float(WIN_ROWS) + sl + lr).astype(jnp.int32)
    for j in range(tr // DISPATCH_TILE):
        cols = slice(j * DISPATCH_TILE, (j + 1) * DISPATCH_TILE)
        local = before[:, cols] - before[:, j * DISPATCH_TILE:j * DISPATCH_TILE + 1]
        tile_cnt = jnp.broadcast_to(jnp.sum(onehot[:, cols], axis=1, keepdims=True), local.shape)
        cnt_hi = jnp.floor(tile_cnt * (1.0 / 256.0))
        cnt_lo = tile_cnt - 256.0 * cnt_hi
        run_start = (256.0 * jnp.dot(low_ref[...], cnt_hi.astype(BF16), preferred_element_type=F32)
                     + jnp.dot(low_ref[...], cnt_lo.astype(BF16), preferred_element_type=F32))
        for kk in range(TOP_K):
            lp = jnp.sum(jnp.where(sels[kk][:, cols], run_start + local, 0.0), axis=0, keepdims=True)
            lpos_ref[pl.ds(kk, 1), at(j, DISPATCH_TILE)] = lp.astype(jnp.int32)
    carry[...] = carry[...] + jnp.sum(onehot, axis=1, keepdims=True)


def _router(logits_t):
    n = logits_t.shape[1]
    tr = ROUTE_BLOCK
    assert n % tr == 0
    tri = jnp.asarray(np.triu(np.ones((tr, tr), np.float32), k=1), BF16)
    low = jnp.asarray(np.tril(np.ones((N_EXPERTS, N_EXPERTS), np.float32), k=-1), BF16)
    vm = pl.BlockSpec(memory_space=pltpu.VMEM)
    return pl.pallas_call(
        _router_kernel,
        in_specs=[vm, vm, vm],
        out_specs=[vm] * 6,
        out_shape=[jax.ShapeDtypeStruct((TOP_K, n), F32),
                   jax.ShapeDtypeStruct((TOP_K, n), jnp.int32),
                   jax.ShapeDtypeStruct((TOP_K, n), jnp.int32),
                   jax.ShapeDtypeStruct((TOP_K, n), jnp.int32),
                   jax.ShapeDtypeStruct((n // COMBINE_TILE, N_EXPERTS, 1), jnp.int32),
                   jax.ShapeDtypeStruct((N_EXPERTS, 1), jnp.int32)],
        scratch_shapes=[pltpu.VMEM((N_EXPERTS, 1), F32)],
        name="router",
    )(logits_t, tri, low)


def _dispatch_kernel(lstart_ref, cnt_ref, dst_ref, h2_ref, lpos_ref, xs_hbm, stg0, stg1, sem):
    i = pl.program_id(0)
    n_tiles = pl.num_programs(0)
    dt = h2_ref.shape[0]
    rows = dt * TOP_K
    slot = i % 2
    stgs = (stg0, stg1)
    pieces = [p for p in (256, 128, 64, 32, 16, 8, 4, 2, 1) if p <= dt]
    assert dt < 512

    def drain(s):
        pltpu.make_async_copy(stgs[s], xs_hbm.at[pl.ds(0, rows)], sem.at[s]).wait()

    def issue_runs(tile, live, s):
        for e in range(N_EXPERTS):
            n = jnp.where(live, cnt_ref[tile * N_EXPERTS + e], 0)
            src0 = lstart_ref[tile * N_EXPERTS + e]
            dst0 = dst_ref[tile * N_EXPERTS + e]
            for piece in pieces:
                off = n & ~jnp.int32(2 * piece - 1)

                @pl.when((n & piece) != 0)
                def _(off=off, piece=piece, src0=src0, dst0=dst0):
                    pltpu.make_async_copy(stgs[s].at[pl.ds(src0 + off, piece)],
                                          xs_hbm.at[pl.ds(dst0 + off, piece)], sem.at[s]).start()

    def sort_tile(s):
        rid = lax.broadcasted_iota(jnp.int32, (rows, dt), 0)
        hit = rid == lpos_ref[0:1, :]
        for kk in range(1, TOP_K):
            hit = jnp.logical_or(hit, rid == lpos_ref[kk:kk + 1, :])
        perm = jnp.where(hit, 1.0, 0.0).astype(BF16)
        srt = jnp.dot(perm, h2_ref[...], preferred_element_type=F32)
        packed = _pack_exact_bf16_pairs(srt)
        stgs[s][...] = packed.reshape(rows, PACK_CHUNKS, LANES)

    for s in range(2):
        @pl.when(slot == s)
        def _(s=s):
            @pl.when(i >= 2)
            def _():
                drain(s)

            issue_runs(jnp.maximum(i - 1, 0), i >= 1, 1 - s)
            sort_tile(s)

            @pl.when(i == n_tiles - 1)
            def _():
                issue_runs(i, True, s)
                drain(s)

                @pl.when(n_tiles >= 2)
                def _():
                    drain(1 - s)


def _dispatch(h2, lpos, lstart, cnt, dst, cap):
    n_tok = h2.shape[0]
    dt = DISPATCH_TILE
    assert n_tok % dt == 0
    grid_spec = pltpu.PrefetchScalarGridSpec(
        num_scalar_prefetch=3,
        grid=(n_tok // dt,),
        in_specs=[pl.BlockSpec((dt, D_MODEL), lambda i, a, b, c: (i, 0)),
                  pl.BlockSpec((TOP_K, dt), lambda i, a, b, c: (0, i)),
                  ],
        out_specs=pl.BlockSpec(memory_space=pl.ANY),
        scratch_shapes=[pltpu.VMEM((dt * TOP_K, PACK_CHUNKS, LANES), jnp.uint32),
                        pltpu.VMEM((dt * TOP_K, PACK_CHUNKS, LANES), jnp.uint32),
                        pltpu.SemaphoreType.DMA((2,))],
    )
    return pl.pallas_call(
        _dispatch_kernel,
        grid_spec=grid_spec,
        out_shape=jax.ShapeDtypeStruct((cap, PACK_CHUNKS, LANES), jnp.uint32),
        compiler_params=pltpu.CompilerParams(dimension_semantics=("arbitrary",),
                                             vmem_limit_bytes=VMEM_LIMIT),
        name="dispatch",
    )(lstart, cnt, dst, h2, lpos)


def _expert_kernel(n_xblocks, bexp_ref, nvalid_ref, epos_ref, elist_ref,
                   x_hbm, wgu_hbm, bgu_ref, wd_hbm, bd_ref, y_ref,
                   wgu_f32, wd_f32, wgu_bf, wd_bf, xbuf, xsem, wsem):
    i = pl.program_id(0)
    rb = y_ref.shape[0]
    nvalid = nvalid_ref[i]
    pos = epos_ref[i]
    fresh = jnp.logical_or(i == 0, pos != epos_ref[jnp.maximum(i - 1, 0)])
    slot = i % 2

    def x_copies(blk, s):
        return [pltpu.make_async_copy(x_hbm.at[pl.ds(blk * rb, rb), c, :],
                                      xbuf.at[s, :, pl.ds(c * LANES, LANES)], xsem.at[s])
                for c in range(PACK_CHUNKS)]

    def w_copies(p):
        e = elist_ref[p]
        s = p % 2
        return [pltpu.make_async_copy(wgu_hbm.at[e], wgu_f32.at[s], wsem.at[s, 0]),
                pltpu.make_async_copy(wd_hbm.at[e], wd_f32.at[s], wsem.at[s, 1])]

    @pl.when(i == 0)
    def _():
        for cp in x_copies(0, 0):
            cp.start()

        @pl.when(nvalid > 0)
        def _():
            for cp in w_copies(0):
                cp.start()

    @pl.when(i + 1 < n_xblocks)
    def _():
        for cp in x_copies(i + 1, 1 - slot):
            cp.start()

    @pl.when(jnp.logical_and(fresh, nvalid > 0))
    def _():
        @pl.when(elist_ref[pos + 1] >= 0)
        def _():
            for cp in w_copies(pos + 1):
                cp.start()

        for cp in w_copies(pos):
            cp.wait()
        ws = pos % 2
        chunk = 32

        def cast_gu(r, c):
            r0 = pl.multiple_of(r * chunk, chunk)
            wgu_bf[pl.ds(r0, chunk), :] = wgu_f32[ws, pl.ds(r0, chunk), :].astype(BF16)
            return c

        def cast_d(r, c):
            r0 = pl.multiple_of(r * chunk, chunk)
            wd_bf[pl.ds(r0, chunk), :] = wd_f32[ws, pl.ds(r0, chunk), :].astype(BF16)
            return c

        lax.fori_loop(0, D_MODEL // chunk, cast_gu, 0)
        lax.fori_loop(0, D_EXPERT // chunk, cast_d, 0)

    @pl.when(i < n_xblocks)
    def _():
        for cp in x_copies(i, slot):
            cp.wait()

    def ffn(rows):
        xw = xbuf[slot, 0:rows, :]
        xw = jnp.where(lax.broadcasted_iota(jnp.int32, xw.shape, 0) < nvalid, xw, jnp.uint32(0))
        xlo, xhi = _unpack_bf16_pairs(xw)
        x = jnp.concatenate([xlo, xhi], axis=1)
        g = jnp.dot(x, wgu_bf[:, 0:D_EXPERT], preferred_element_type=F32) + bgu_ref[0, :, 0:D_EXPERT]
        u = jnp.dot(x, wgu_bf[:, D_EXPERT:], preferred_element_type=F32) + bgu_ref[0, :, D_EXPERT:]
        g = jnp.minimum(g, SWIGLU_LIMIT)
        u = jnp.clip(u, -SWIGLU_LIMIT, SWIGLU_LIMIT)
        act = g * (1.0 / (1.0 + jnp.exp(-SWIGLU_ALPHA * g))) * (u + 1.0)
        y = jnp.dot(act.astype(BF16), wd_bf[...], preferred_element_type=F32) + bd_ref[0]
        row = lax.broadcasted_iota(jnp.int32, y.shape, 0)
        y_ref[0:rows, :] = jnp.where(row < nvalid, y, 0.0).astype(BF16)
        if rows < rb:
            y_ref[rows:, :] = jnp.zeros((rb - rows, D_MODEL), BF16)

    quarter = rb // EXPERT_PATHS
    for part in range(1, EXPERT_PATHS + 1):
        @pl.when(jnp.logical_and(nvalid > (part - 1) * quarter, nvalid <= part * quarter))
        def _(part=part):
            ffn(part * quarter)

    @pl.when(nvalid == 0)
    def _():
        y_ref[...] = jnp.zeros_like(y_ref)


def _experts(xs, block_expert, nvalid, block_pos, expert_list, wgu, bgu, wd, bd):
    rb = EXPERT_ROWS
    n_xblocks = xs.shape[0] // rb
    nblk = n_xblocks + 1
    any_space = pl.BlockSpec(memory_space=pl.ANY)
    grid_spec = pltpu.PrefetchScalarGridSpec(
        num_scalar_prefetch=4,
        grid=(nblk,),
        in_specs=[
            any_space,
            any_space,
            pl.BlockSpec((1, 1, 2 * D_EXPERT), lambda i, be, nu, ep, el: (be[i], 0, 0)),
            any_space,
            pl.BlockSpec((1, 1, D_MODEL), lambda i, be, nu, ep, el: (be[i], 0, 0)),
        ],
        out_specs=pl.BlockSpec((rb, D_MODEL), lambda i, be, nu, ep, el: (i, 0)),
        scratch_shapes=[pltpu.VMEM((2, D_MODEL, 2 * D_EXPERT), F32),
                        pltpu.VMEM((2, D_EXPERT, D_MODEL), F32),
                        pltpu.VMEM((D_MODEL, 2 * D_EXPERT), BF16),
                        pltpu.VMEM((D_EXPERT, D_MODEL), BF16),
                        pltpu.VMEM((2, rb, D_MODEL // 2), jnp.uint32),
                        pltpu.SemaphoreType.DMA((2,)),
                        pltpu.SemaphoreType.DMA((2, 2))],
    )
    return pl.pallas_call(
        functools.partial(_expert_kernel, n_xblocks),
        grid_spec=grid_spec,
        out_shape=jax.ShapeDtypeStruct((nblk * rb, D_MODEL), BF16),
        compiler_params=pltpu.CompilerParams(dimension_semantics=("arbitrary",),
                                             vmem_limit_bytes=VMEM_LIMIT),
        name="experts",
    )(block_expert, nvalid, block_pos, expert_list, xs, wgu, bgu.reshape(N_EXPERTS, 1, 2 * D_EXPERT), wd,
      bd.reshape(N_EXPERTS, 1, D_MODEL))


def _combine_kernel(n_prompt_tiles, offa_ref, nchunk_ref, tail_ref, ntail_ref,
                    x1p_ref, x1s_ref, info_ref, info_next_ref, gfin_ref, yb_hbm,
                    outp_ref, outs_ref, ybuf, gbuf0, gbuf1, acc_ref, sem, tsem):
    i = pl.program_id(0)
    n_tiles = pl.num_programs(0)

    def window_copy(tile, chunk, e, slot):
        base = pl.multiple_of(offa_ref[tile * N_EXPERTS + e] + chunk * WIN, WIN_ALIGN)
        return pltpu.make_async_copy(yb_hbm.at[pl.ds(base, WIN_ROWS), :],
                                     ybuf.at[slot, pl.ds(e * WIN_ROWS, WIN_ROWS), :],
                                     sem.at[slot])

    def start_windows(tile, chunk, slot):
        for e in range(N_EXPERTS):
            window_copy(tile, chunk, e, slot).start()

    def wait_windows(slot):
        pltpu.make_async_copy(yb_hbm.at[pl.ds(0, N_EXPERTS * WIN_ROWS), :], ybuf.at[slot],
                              sem.at[slot]).wait()

    def tail_copy(tile, e, slot):
        base = pl.multiple_of(offa_ref[tile * N_EXPERTS + e] + WIN_HEAD, WIN_ALIGN)
        return pltpu.make_async_copy(yb_hbm.at[pl.ds(base, WIN_ROWS - WIN_HEAD), :],
                                     ybuf.at[slot, pl.ds(e * WIN_ROWS + WIN_HEAD, WIN_ROWS - WIN_HEAD), :],
                                     tsem.at[slot])

    def start_first_chunk(tile, slot):
        for e in range(N_EXPERTS):
            base = pl.multiple_of(offa_ref[tile * N_EXPERTS + e], WIN_ALIGN)
            pltpu.make_async_copy(yb_hbm.at[pl.ds(base, WIN_HEAD), :],
                                  ybuf.at[slot, pl.ds(e * WIN_ROWS, WIN_HEAD), :], sem.at[slot]).start()

            @pl.when(tail_ref[tile * N_EXPERTS + e] != 0)
            def _(e=e):
                tail_copy(tile, e, slot).start()

    def wait_first_chunk(tile, slot):
        pltpu.make_async_copy(yb_hbm.at[pl.ds(0, N_EXPERTS * WIN_HEAD), :],
                              ybuf.at[slot, pl.ds(0, N_EXPERTS * WIN_HEAD), :], sem.at[slot]).wait()

        def one_tail(t, c):
            tail_copy(tile, 0, slot).wait()
            return c

        lax.fori_loop(0, ntail_ref[tile], one_tail, 0)

    slot = i % 2

    @pl.when(i == 0)
    def _():
        ybuf[...] = jnp.zeros_like(ybuf)
        start_first_chunk(0, 0)

    lane = lax.broadcasted_iota(jnp.int32, (COMBINE_TILE, N_EXPERTS * WIN_ROWS), 1)

    def gate_matrix(ref, chunk):
        g = jnp.zeros(lane.shape, F32)
        for kk in range(TOP_K):
            lr = ref[:, TOP_K + kk:TOP_K + kk + 1]
            in_chunk = jnp.logical_and(lr >= chunk * WIN, lr < chunk * WIN + WIN)
            colk = jnp.where(in_chunk, ref[:, kk:kk + 1] - chunk * WIN, -1.0).astype(jnp.int32)
            g = jnp.where(lane == colk, ref[:, 2 * TOP_K + kk:2 * TOP_K + kk + 1], g)
        return g.astype(BF16)

    def moe_rows(gm, buf):
        return jnp.dot(gm, ybuf[buf], preferred_element_type=F32)

    @pl.when(i == 0)
    def _():
        gbuf0[...] = gate_matrix(info_ref, 0)

    def main(s):
        start_first_chunk(jnp.minimum(i + 1, n_tiles - 1), 1 - s)
        wait_first_chunk(i, s)
        g_cur, g_nxt = (gbuf0, gbuf1) if s == 0 else (gbuf1, gbuf0)
        acc_ref[...] = moe_rows(g_cur[...], s)
        g_nxt[...] = gate_matrix(info_next_ref, 0)

    for s in range(2):
        @pl.when(slot == s)
        def _(s=s):
            main(s)

    @pl.when(i == n_tiles - 1)
    def _():
        for s in range(2):
            @pl.when(slot == s)
            def _(s=s):
                wait_first_chunk(i, 1 - s)

    def extra_chunk(j, c):
        start_windows(i, j, 2)
        wait_windows(2)
        acc_ref[...] += moe_rows(gate_matrix(info_ref, j), 2)
        return c

    lax.fori_loop(1, nchunk_ref[i], extra_chunk, 0)

    @pl.when(i < n_prompt_tiles)
    def _():
        outp_ref[...] = _rms(x1p_ref[...] + acc_ref[...], gfin_ref[...])

    @pl.when(i >= n_prompt_tiles)
    def _():
        outs_ref[...] = _rms(x1s_ref[...] + acc_ref[...], gfin_ref[...])


def _combine(x1_p, x1_s, tok_info, gfin, yb, offa, nchunk, need_tail, ntail):
    ct = COMBINE_TILE
    n_p, n_s = x1_p.shape[0] // ct, x1_s.shape[0] // ct
    assert x1_p.shape[0] % ct == 0 and x1_s.shape[0] % ct == 0 and n_s >= 1
    n_info = tok_info.shape[1]
    grid_spec = pltpu.PrefetchScalarGridSpec(
        num_scalar_prefetch=4,
        grid=(n_p + n_s,),
        in_specs=[
            pl.BlockSpec((ct, D_MODEL), lambda i, *_: (jnp.minimum(i, n_p - 1), 0)),
            pl.BlockSpec((ct, D_MODEL), lambda i, *_: (jnp.maximum(i - n_p, 0), 0)),
            pl.BlockSpec((ct, n_info), lambda i, *_: (i, 0)),
            pl.BlockSpec((ct, n_info), lambda i, *_: (jnp.minimum(i + 1, n_p + n_s - 1), 0)),
            pl.BlockSpec((1, D_MODEL), lambda i, *_: (0, 0)),
            pl.BlockSpec(memory_space=pl.ANY),
        ],
        out_specs=[
            pl.BlockSpec((ct, D_MODEL), lambda i, *_: (jnp.minimum(i, n_p - 1), 0)),
            pl.BlockSpec((ct, D_MODEL), lambda i, *_: (jnp.maximum(i - n_p, 0), 0)),
        ],
        scratch_shapes=[pltpu.VMEM((3, N_EXPERTS * WIN_ROWS, D_MODEL), BF16),
                        pltpu.VMEM((ct, N_EXPERTS * WIN_ROWS), BF16),
                        pltpu.VMEM((ct, N_EXPERTS * WIN_ROWS), BF16),
                        pltpu.VMEM((ct, D_MODEL), F32),
                        pltpu.SemaphoreType.DMA((3,)),
                        pltpu.SemaphoreType.DMA((2,))],
    )
    return pl.pallas_call(
        functools.partial(_combine_kernel, n_p),
        grid_spec=grid_spec,
        out_shape=[jax.ShapeDtypeStruct(x1_p.shape, F32), jax.ShapeDtypeStruct(x1_s.shape, F32)],
        compiler_params=pltpu.CompilerParams(dimension_semantics=("arbitrary",),
                                             vmem_limit_bytes=VMEM_LIMIT),
        name="combine",
    )(offa, nchunk, need_tail, ntail, x1_p, x1_s, tok_info, tok_info, gfin, yb)


def kernel(x_prompt, x_sample, cache_k, cache_v, state_pool, meta_tokens, norm_attn, w_in, attn_sinks,
           w_pool, pool_scale, w_out, norm_ffn, router_w, router_b, w_gate_up, b_gate_up, w_down, b_down,
           norm_final):
    assert w_in.shape[0] == 1, "single-layer trunk"
    bsz, seq, _ = x_prompt.shape
    assert bsz == 1
    nb = x_sample.shape[0]
    n_tok = seq + nb
    gattn = norm_attn[0].reshape(1, D_MODEL)
    gffn = norm_ffn[0].reshape(1, D_MODEL)
    win = w_in[0].astype(BF16)
    wpool = w_pool[0].astype(BF16)
    wout = w_out[0].astype(BF16)
    pscale = pool_scale[0].reshape(1, POOL_WIDTH)
    rw_t = router_w[0].T
    rw_hi = rw_t.astype(BF16)
    rwt = jnp.stack([rw_hi, (rw_t - rw_hi.astype(F32)).astype(BF16)])
    rb = router_b[0].reshape(N_EXPERTS, 1)
    sinks = attn_sinks[0]

    cache_rows = N_META + WINDOW
    to_batch_minor = lambda c: jnp.transpose(c[0], (1, 2, 3, 0)).reshape(cache_rows, KV_WIDTH, nb)
    from_batch_minor = lambda c: jnp.transpose(
        c.reshape(cache_rows, N_KV_HEADS, HEAD_DIM, nb), (3, 0, 1, 2))[None]
    (x1_s, h2_s, lgt_s, kout_t, vout_t, pool_t) = _sample_mixer(
        x_sample[:, 0], to_batch_minor(cache_k), to_batch_minor(cache_v),
        jnp.transpose(state_pool[0], (1, 0, 2)), gattn, win, wpool, pscale, wout, gffn, rwt, rb, sinks)
    (x1_p, h2_all, lgt_all, kmeta, vmeta, ktail, vtail, ptail) = _prompt_mixer(
        x_prompt[0], meta_tokens, gattn, win, wpool, pscale, wout, gffn, rwt, rb, sinks, h2_s, lgt_s)

    gates, col, lrank, lpos, tcar, counts = _router(lgt_all)
    counts = counts[:, 0]
    tcar = tcar[:, :, 0]
    rbk = EXPERT_ROWS
    eids = jnp.arange(N_EXPERTS, dtype=jnp.int32)
    earlier = eids[None, :] < eids[:, None]
    excl_sum = lambda a: jnp.sum(jnp.where(earlier, a[..., None, :], 0), axis=-1)
    padded = (counts + rbk - 1) // rbk * rbk
    pad_start = excl_sum(padded).astype(jnp.int32)
    pad_end = pad_start + padded
    nblk = -(-(n_tok * TOP_K) // rbk) + N_EXPERTS
    cap = nblk * rbk
    block_start = jnp.arange(nblk + 1, dtype=jnp.int32) * rbk
    owns = (pad_start[None, :] <= block_start[:, None]) & (block_start[:, None] < pad_end[None, :])
    nvalid = jnp.sum(jnp.where(owns, jnp.clip(counts[None, :] - (block_start[:, None] - pad_start[None, :]),
                                              0, rbk), 0), axis=1).astype(jnp.int32)
    has_rows = counts > 0
    last_e = jnp.max(jnp.where(has_rows, eids, 0))
    block_expert = jnp.where(jnp.any(owns, axis=1), jnp.sum(jnp.where(owns, eids[None, :], 0), axis=1),
                             last_e).astype(jnp.int32)

    run_len = jnp.concatenate([tcar[1:], counts[None, :]], axis=0) - tcar
    dcar = tcar[::DISPATCH_TILE // COMBINE_TILE]
    drun_len = jnp.concatenate([dcar[1:], counts[None, :]], axis=0) - dcar
    flat = lambda a: a.astype(jnp.int32).reshape(-1)
    xs = _dispatch(h2_all, lpos, flat(excl_sum(drun_len)), flat(drun_len), flat(pad_start[None, :] + dcar),
                   cap)
    expert_pos = excl_sum(has_rows.astype(jnp.int32))
    at_pos = has_rows[None, :] & (expert_pos[None, :] == jnp.arange(N_EXPERTS + 1, dtype=jnp.int32)[:, None])
    expert_list = jnp.where(jnp.any(at_pos, axis=1), jnp.sum(jnp.where(at_pos, eids[None, :], 0), axis=1),
                            -1).astype(jnp.int32)
    block_pos = jnp.sum(jnp.where(block_expert[:, None] == eids[None, :], expert_pos[None, :], 0),
                        axis=1).astype(jnp.int32)
    yb = _experts(xs, block_expert, nvalid, block_pos, expert_list,
                  w_gate_up[0], b_gate_up[0], w_down[0], b_down[0])

    offa = (pad_start[None, :] + (tcar - tcar % WIN_ALIGN)).astype(jnp.int32).reshape(-1)
    nchunk = jnp.maximum(jnp.max((run_len + WIN - 1) // WIN, axis=1), 1).astype(jnp.int32)
    need_tail = (tcar % WIN_ALIGN + jnp.minimum(run_len, WIN)) > WIN_HEAD
    ntail = jnp.sum(need_tail, axis=1).astype(jnp.int32)
    gfin = norm_final.reshape(1, D_MODEL)
    tok_info = jnp.concatenate([col.astype(F32), lrank.astype(F32), gates], axis=0).T
    y_prompt, y_sample = _combine(x1_p, x1_s, tok_info, gfin, yb, offa, nchunk,
                                  need_tail.astype(jnp.int32).reshape(-1), ntail)

    kv_shape = (1, 1, N_META + WINDOW, N_KV_HEADS, HEAD_DIM)
    new_k_p = jnp.concatenate([kmeta, ktail], axis=0).reshape(kv_shape)
    new_v_p = jnp.concatenate([vmeta, vtail], axis=0).reshape(kv_shape)
    new_pool_p = ptail[16 - POOL_STATE:].reshape(1, 1, POOL_STATE, POOL_WIDTH)
    new_k_s = from_batch_minor(kout_t)
    new_v_s = from_batch_minor(vout_t)
    new_pool_s = jnp.transpose(pool_t, (1, 0, 2))[None]
    return (y_prompt[None], y_sample[:, None], new_k_p, new_v_p, new_pool_p, new_k_s, new_v_s, new_pool_s)
```

```python
import functools

import jax
import jax.numpy as jnp
import numpy as np
from jax import lax
from jax.experimental import pallas as pl
from jax.experimental.pallas import tpu as pltpu

F32 = jnp.float32
BF16 = jnp.bfloat16

D_MODEL = 1024
N_META = 16
N_HEADS = 8
HEAD_DIM = 64
N_KV_HEADS = 2
GQA_GROUP = N_HEADS // N_KV_HEADS
ATTN_WIDTH = N_HEADS * HEAD_DIM
KV_WIDTH = N_KV_HEADS * HEAD_DIM
WINDOW = 128
POOL_WIDTH = D_MODEL - ATTN_WIDTH
POOL_WINDOWS = (2, 4, 8, 16)
POOL_GROUP_DIM = POOL_WIDTH // len(POOL_WINDOWS)
POOL_STATE = max(POOL_WINDOWS) - 1
N_EXPERTS = 32
TOP_K = 4
D_EXPERT = D_MODEL
SWIGLU_ALPHA = 1.702
SWIGLU_LIMIT = 7.0
NORM_EPS = 1e-5

LANES = 128
QSUB = 64
KEYS_SUB = QSUB + WINDOW
META_PAD = 64
NKEY = META_PAD + KEYS_SUB
MASKED = -1e30
PROMPT_BLOCK = 1024
ROUTE_BLOCK = 384
EXPERT_ROWS = 512
EXPERT_PATHS = 4
DISPATCH_TILE = 384
PACK_CHUNKS = D_MODEL // 2 // LANES
COMBINE_TILE = 128
WIN = 32
WIN_ALIGN = 16
WIN_ROWS = WIN + WIN_ALIGN
WIN_HEAD = 32
VMEM_LIMIT = 56 * 1024 * 1024


def _rms(x, g):
    return x * lax.rsqrt(jnp.mean(x * x, axis=-1, keepdims=True) + NORM_EPS) * g


def _router_logits(rwt_ref, h2, h2_hi):
    nt = (((1,), (1,)), ((), ()))
    h2_lo = (h2 - h2_hi.astype(F32)).astype(BF16)
    return (lax.dot_general(rwt_ref[0], h2_hi, nt, preferred_element_type=F32)
            + lax.dot_general(rwt_ref[0], h2_lo, nt, preferred_element_type=F32)
            + lax.dot_general(rwt_ref[1], h2_hi, nt, preferred_element_type=F32))


def _dup_halves(a):
    lane = lax.broadcasted_iota(jnp.int32, a.shape, 1)
    r = pltpu.roll(a, HEAD_DIM, axis=1)
    lo = lane < HEAD_DIM
    return jnp.where(lo, a, r), jnp.where(lo, r, a)


def _pool_means(pext_ref, n):
    outs = []
    for gi, w in enumerate(POOL_WINDOWS):
        xg = pext_ref[:, gi * POOL_GROUP_DIM:(gi + 1) * POOL_GROUP_DIM]
        s = xg
        sh = 1
        while sh < w:
            s = s + pltpu.roll(s, sh, axis=0)
            sh *= 2
        outs.append(s[16:] * (1.0 / w) - xg[16:])
    return outs


def _pack_exact_bf16_pairs(h):
    m = h.shape[1] // 2
    return (lax.shift_right_logical(pltpu.bitcast(h[:, :m], jnp.uint32), jnp.uint32(16))
            | (pltpu.bitcast(h[:, m:], jnp.uint32) & jnp.uint32(0xFFFF0000)))


def _unpack_bf16_pairs(w):
    lo = pltpu.bitcast(lax.shift_left(w, jnp.uint32(16)), F32).astype(BF16)
    hi = pltpu.bitcast(w & jnp.uint32(0xFFFF0000), F32).astype(BF16)
    return lo, hi


def _prompt_kernel(x_ref, meta_ref, gattn_ref, win_ref, wpool_ref, pscale_ref, wout_ref, gffn_ref,
                   rwt_ref, rb_ref, sink_ref, tbl_ref, tail_h2_ref, tail_lgt_ref,
                   x1_ref, h2_ref, lgt_ref, kmeta_ref, vmeta_ref, ktail_ref, vtail_ref, ptail_ref,
                   k2buf, v2buf, km2, vm2, qbuf, obuf, pext):
    pid = pl.program_id(0)
    n_main = pl.num_programs(0) - 1
    refs = (x_ref, meta_ref, gattn_ref, win_ref, wpool_ref, pscale_ref, wout_ref, gffn_ref,
            rwt_ref, rb_ref, sink_ref, tbl_ref,
            x1_ref, h2_ref, lgt_ref, kmeta_ref, vmeta_ref, ktail_ref, vtail_ref, ptail_ref,
            k2buf, v2buf, km2, vm2, qbuf, obuf, pext)

    @pl.when(pid < n_main)
    def _():
        _prompt_block(*refs)

    @pl.when(pid == n_main)
    def _():
        h2_ref[0:tail_h2_ref.shape[0], :] = tail_h2_ref[...]
        lgt_ref[:, 0:tail_lgt_ref.shape[1]] = tail_lgt_ref[...]


def _prompt_block(x_ref, meta_ref, gattn_ref, win_ref, wpool_ref, pscale_ref, wout_ref, gffn_ref,
                  rwt_ref, rb_ref, sink_ref, tbl_ref,
                  x1_ref, h2_ref, lgt_ref, kmeta_ref, vmeta_ref, ktail_ref, vtail_ref, ptail_ref,
                  k2buf, v2buf, km2, vm2, qbuf, obuf, pext):
    tb = x_ref.shape[0]
    pid = pl.program_id(0)

    @pl.when(pid == 0)
    def _():
        hm = _rms(meta_ref[...], gattn_ref[...]).astype(BF16)
        km = jnp.dot(hm, win_ref[:, ATTN_WIDTH:ATTN_WIDTH + KV_WIDTH], preferred_element_type=F32)
        vm = jnp.dot(hm, win_ref[:, ATTN_WIDTH + KV_WIDTH:ATTN_WIDTH + 2 * KV_WIDTH],
                     preferred_element_type=F32)
        pm = jnp.dot(hm, win_ref[:, ATTN_WIDTH + 2 * KV_WIDTH:], preferred_element_type=F32)
        kmeta_ref[...] = km
        vmeta_ref[...] = vm
        zpad = jnp.zeros((META_PAD - N_META, LANES), F32)
        k0, k1 = _dup_halves(jnp.concatenate([km, zpad], axis=0))
        v0, v1 = _dup_halves(jnp.concatenate([vm, zpad], axis=0))
        km2[0] = k0.astype(BF16)
        km2[1] = k1.astype(BF16)
        vm2[0, :, 0:LANES] = v0.astype(BF16)
        vm2[1, :, 0:LANES] = v1.astype(BF16)
        vm2[:, :, LANES:] = jnp.ones((N_KV_HEADS, META_PAD, LANES), BF16)
        k2buf[:, 0:WINDOW, :] = jnp.zeros((2, WINDOW, LANES), BF16)
        v2buf[:, 0:WINDOW, 0:LANES] = jnp.zeros((2, WINDOW, LANES), BF16)
        v2buf[:, :, LANES:] = jnp.ones((N_KV_HEADS, WINDOW + tb, LANES), BF16)
        pext[0:16, :] = pm

    h = _rms(x_ref[...], gattn_ref[...]).astype(BF16)
    q = jnp.dot(h, win_ref[:, 0:ATTN_WIDTH], preferred_element_type=F32) * (HEAD_DIM ** -0.5)
    lane_t = lax.broadcasted_iota(jnp.int32, (tb, LANES), 1)
    for c in range(N_HEADS // 2):
        tile = q[:, c * LANES:(c + 1) * LANES]
        for a in range(2):
            keep = (lane_t < HEAD_DIM) if a == 0 else (lane_t >= HEAD_DIM)
            piece = jnp.where(keep, tile, 0.0).astype(BF16).reshape(tb // QSUB, QSUB, LANES)
            row = ((c % 2) * 2 + a) * QSUB
            qbuf[c // 2, :, row:row + QSUB, :] = piece
    k = jnp.dot(h, win_ref[:, ATTN_WIDTH:ATTN_WIDTH + KV_WIDTH], preferred_element_type=F32)
    v = jnp.dot(h, win_ref[:, ATTN_WIDTH + KV_WIDTH:ATTN_WIDTH + 2 * KV_WIDTH], preferred_element_type=F32)
    p = jnp.dot(h, win_ref[:, ATTN_WIDTH + 2 * KV_WIDTH:], preferred_element_type=F32)
    ktail_ref[...] = k[tb - WINDOW:]
    vtail_ref[...] = v[tb - WINDOW:]
    ptail_ref[...] = p[tb - 16:]
    k0, k1 = _dup_halves(k)
    v0, v1 = _dup_halves(v)
    k2buf[0, WINDOW:, :] = k0.astype(BF16)
    k2buf[1, WINDOW:, :] = k1.astype(BF16)
    v2buf[0, WINDOW:, 0:LANES] = v0.astype(BF16)
    v2buf[1, WINDOW:, 0:LANES] = v1.astype(BF16)
    pext[16:, :] = p

    lane_q = lax.broadcasted_iota(jnp.int32, (QSUB, LANES), 1)
    lo_q = lane_q < HEAD_DIM

    for u in range(tb // QSUB):
        r0 = u * QSUB
        sel = jnp.where(pid == 0, u + 1, 0) if u < WINDOW // QSUB else 0
        for g in range(N_KV_HEADS):
            qm = qbuf[g, u]
            kwin = jnp.concatenate([km2[g], k2buf[g, r0:r0 + KEYS_SUB, :]], axis=0)
            vwin = jnp.concatenate([vm2[g], v2buf[g, r0:r0 + KEYS_SUB, :]], axis=0)
            s = lax.dot_general(qm, kwin, (((1,), (1,)), ((), ())), preferred_element_type=F32)
            s = s + tbl_ref[sel, g]
            sink = sink_ref[g]
            m = jnp.maximum(jnp.max(s, axis=1, keepdims=True), sink)
            e = jnp.exp(s - m).astype(BF16)
            r = jnp.dot(e, vwin, preferred_element_type=F32)
            o = r[:, 0:LANES] / (r[:, LANES:] + jnp.exp(sink - m))
            o0 = jnp.where(lo_q, o[0:QSUB], o[QSUB:2 * QSUB])
            o1 = jnp.where(lo_q, o[2 * QSUB:3 * QSUB], o[3 * QSUB:])
            obuf[r0:r0 + QSUB, (2 * g) * LANES:(2 * g + 1) * LANES] = o0.astype(BF16)
            obuf[r0:r0 + QSUB, (2 * g + 1) * LANES:(2 * g + 2) * LANES] = o1.astype(BF16)

    pooled = _pool_means(pext, tb)
    for gi in range(len(POOL_WINDOWS)):
        y = jnp.dot(pooled[gi].astype(BF16), wpool_ref[gi], preferred_element_type=F32)
        y = y * pscale_ref[:, gi * POOL_GROUP_DIM:(gi + 1) * POOL_GROUP_DIM]
        obuf[:, ATTN_WIDTH + gi * POOL_GROUP_DIM:ATTN_WIDTH + (gi + 1) * POOL_GROUP_DIM] = y.astype(BF16)

    k2buf[:, 0:WINDOW, :] = k2buf[:, tb:tb + WINDOW, :]
    v2buf[:, 0:WINDOW, 0:LANES] = v2buf[:, tb:tb + WINDOW, 0:LANES]
    pext[0:16, :] = pext[tb:tb + 16, :]

    x1 = x_ref[...] + jnp.dot(obuf[...], wout_ref[...], preferred_element_type=F32)
    x1_ref[...] = x1
    h2 = _rms(x1, gffn_ref[...])
    h2_hi = h2.astype(BF16)
    h2_ref[...] = h2_hi
    lgt_ref[...] = _router_logits(rwt_ref, h2, h2_hi) + rb_ref[...]


def _attn_tables(sinks):
    i = np.arange(QSUB)[:, None]
    j = np.arange(NKEY)[None, :]
    jb = j - META_PAD
    rel = i + WINDOW - jb
    band_ok = (jb >= 0) & (rel >= 0) & (rel <= WINDOW)
    meta_ok = (j < N_META) & (i >= 0)
    slopes = np.exp2(-8.0 * np.arange(1, N_HEADS + 1) / N_HEADS)
    tbl = np.empty((3, N_KV_HEADS, GQA_GROUP * QSUB, NKEY), np.float32)
    for var in range(3):
        ok = band_ok if var == 0 else band_ok & (jb >= WINDOW - (var - 1) * QSUB)
        for g in range(N_KV_HEADS):
            for a in range(GQA_GROUP):
                hd = g * GQA_GROUP + a
                bias = np.where(ok, -slopes[hd] * rel, MASKED)
                bias = np.where(meta_ok, 0.0, bias)
                tbl[var, g, a * QSUB:(a + 1) * QSUB] = bias
    sink_col = jnp.repeat(sinks.astype(F32).reshape(N_KV_HEADS, GQA_GROUP, 1), QSUB, axis=2)
    return jnp.asarray(tbl), sink_col.reshape(N_KV_HEADS, GQA_GROUP * QSUB, 1)


def _prompt_mixer(x, meta, gattn, win, wpool, pscale, wout, gffn, rwt, rb, sinks, tail_h2, tail_lgt):
    seq = x.shape[0]
    tb = PROMPT_BLOCK
    n_tail = tail_h2.shape[0]
    assert seq % tb == 0 and tb % WINDOW == 0 and n_tail <= tb
    nblk = seq // tb
    n_tok = seq + n_tail
    tbl, sink_col = _attn_tables(sinks)
    full = lambda *shape: pl.BlockSpec(shape, lambda i: (0,) * len(shape))
    main = lambda i: (jnp.minimum(i, nblk - 1), 0)
    in_width = win.shape[1]
    return pl.pallas_call(
        _prompt_kernel,
        grid=(nblk + 1,),
        in_specs=[
            pl.BlockSpec((tb, D_MODEL), main),
            full(N_META, D_MODEL), full(1, D_MODEL), full(D_MODEL, in_width),
            full(len(POOL_WINDOWS), POOL_GROUP_DIM, POOL_GROUP_DIM), full(1, POOL_WIDTH),
            full(D_MODEL, D_MODEL), full(1, D_MODEL), full(2, N_EXPERTS, D_MODEL), full(N_EXPERTS, 1),
            full(N_KV_HEADS, GQA_GROUP * QSUB, 1), full(3, N_KV_HEADS, GQA_GROUP * QSUB, NKEY),
            full(n_tail, D_MODEL), full(N_EXPERTS, n_tail),
        ],
        out_specs=[
            pl.BlockSpec((tb, D_MODEL), main),
            pl.BlockSpec((tb, D_MODEL), lambda i: (i, 0)),
            pl.BlockSpec((N_EXPERTS, tb), lambda i: (0, i)),
            full(N_META, KV_WIDTH), full(N_META, KV_WIDTH),
            full(WINDOW, KV_WIDTH), full(WINDOW, KV_WIDTH), full(16, POOL_WIDTH),
        ],
        out_shape=[
            jax.ShapeDtypeStruct((seq, D_MODEL), F32),
            jax.ShapeDtypeStruct((n_tok, D_MODEL), BF16),
            jax.ShapeDtypeStruct((N_EXPERTS, n_tok), F32),
            jax.ShapeDtypeStruct((N_META, KV_WIDTH), F32),
            jax.ShapeDtypeStruct((N_META, KV_WIDTH), F32),
            jax.ShapeDtypeStruct((WINDOW, KV_WIDTH), F32),
            jax.ShapeDtypeStruct((WINDOW, KV_WIDTH), F32),
            jax.ShapeDtypeStruct((16, POOL_WIDTH), F32),
        ],
        scratch_shapes=[
            pltpu.VMEM((N_KV_HEADS, WINDOW + tb, LANES), BF16),
            pltpu.VMEM((N_KV_HEADS, WINDOW + tb, 2 * LANES), BF16),
            pltpu.VMEM((N_KV_HEADS, META_PAD, LANES), BF16),
            pltpu.VMEM((N_KV_HEADS, META_PAD, 2 * LANES), BF16),
            pltpu.VMEM((N_KV_HEADS, tb // QSUB, GQA_GROUP * QSUB, LANES), BF16),
            pltpu.VMEM((tb, D_MODEL), BF16),
            pltpu.VMEM((16 + tb, POOL_WIDTH), F32),
        ],
        compiler_params=pltpu.CompilerParams(dimension_semantics=("arbitrary",),
                                             vmem_limit_bytes=VMEM_LIMIT),
        name="prompt_mixer",
    )(x, meta, gattn, win, wpool, pscale, wout, gffn, rwt, rb, sink_col, tbl, tail_h2, tail_lgt)


def _sample_kernel(x_ref, ck_ref, cv_ref, sp_ref, gattn_ref, win_ref, wqkv_t_ref, wpool_ref, pscale_ref,
                   wout_ref, gffn_ref, rwt_ref, rb_ref, sink_ref, bias_ref,
                   x1_ref, h2_ref, lgt_ref, kout_ref, vout_ref, pnew_ref,
                   qt_ref, sc_ref, ot_ref, obuf):
    nb = x_ref.shape[0]
    rows = ck_ref.shape[0]
    x = x_ref[...]
    h = _rms(x, gattn_ref[...]).astype(BF16)
    nt = (((1,), (1,)), ((), ()))
    qkv_t = lax.dot_general(wqkv_t_ref[...], h, nt, preferred_element_type=F32)
    qt_ref[...] = qkv_t[0:ATTN_WIDTH] * (HEAD_DIM ** -0.5)
    kt = qkv_t[ATTN_WIDTH:ATTN_WIDTH + KV_WIDTH]
    vt = qkv_t[ATTN_WIDTH + KV_WIDTH:]
    p = jnp.dot(h, win_ref[:, ATTN_WIDTH + 2 * KV_WIDTH:], preferred_element_type=F32)
    pnew_ref[0:POOL_STATE - 1] = sp_ref[1:POOL_STATE]
    pnew_ref[POOL_STATE - 1] = p

    kout_ref[0:N_META] = ck_ref[0:N_META]
    vout_ref[0:N_META] = cv_ref[0:N_META]
    kout_ref[N_META:rows - 1] = ck_ref[N_META + 1:rows]
    vout_ref[N_META:rows - 1] = cv_ref[N_META + 1:rows]
    kout_ref[rows - 1] = kt
    vout_ref[rows - 1] = vt

    def kv_rows(hd):
        g = hd // GQA_GROUP
        return slice(g * HEAD_DIM, (g + 1) * HEAD_DIM)

    def score_row(key_tile, hd):
        prod = qt_ref[hd * HEAD_DIM:(hd + 1) * HEAD_DIM, :] * key_tile[kv_rows(hd), :]
        return jnp.sum(prod, axis=0, keepdims=True)

    def score_pass(s, carry):
        key_tile = ck_ref[s]
        for hd in range(N_HEADS):
            sc_ref[hd, pl.ds(s, 1), :] = score_row(key_tile, hd)
        return carry

    lax.fori_loop(0, rows, score_pass, 0)
    pad_rows = sc_ref.shape[1] - rows - 1
    for hd in range(N_HEADS):
        sc_ref[hd, rows:rows + 1, :] = score_row(kt, hd)
        sc_ref[hd, rows + 1:, :] = jnp.full((pad_rows, nb), MASKED, F32)

    for hd in range(N_HEADS):
        s = sc_ref[hd] + bias_ref[hd]
        sink = sink_ref[hd]
        m = jnp.maximum(jnp.max(s, axis=0, keepdims=True), sink)
        e = jnp.exp(s - m)
        sc_ref[hd] = e * (1.0 / (jnp.sum(e, axis=0, keepdims=True) + jnp.exp(sink - m)))

    for half in range(2):
        heads = range(half * (N_HEADS // 2), (half + 1) * (N_HEADS // 2))

        def value_pass(s, accs, heads=heads):
            val_tile = cv_ref[s]
            return tuple(acc + sc_ref[hd, pl.ds(s, 1), :] * val_tile[kv_rows(hd), :]
                         for acc, hd in zip(accs, heads))

        accs = lax.fori_loop(0, rows, value_pass,
                             tuple(jnp.zeros((HEAD_DIM, nb), F32) for _ in heads))
        for acc, hd in zip(accs, heads):
            ot_ref[hd * HEAD_DIM:(hd + 1) * HEAD_DIM, :] = acc + sc_ref[hd, rows:rows + 1, :] * vt[kv_rows(hd), :]

    for c in range(ATTN_WIDTH // LANES):
        obuf[:, c * LANES:(c + 1) * LANES] = ot_ref[c * LANES:(c + 1) * LANES, :].T.astype(BF16)

    for gi, w in enumerate(POOL_WINDOWS):
        cols = slice(gi * POOL_GROUP_DIM, (gi + 1) * POOL_GROUP_DIM)
        pg = p[:, cols]
        acc = pg
        for d in range(1, w):
            acc = acc + sp_ref[POOL_STATE - d, :, cols]
        pooled = acc * (1.0 / w) - pg
        y = jnp.dot(pooled.astype(BF16), wpool_ref[gi], preferred_element_type=F32) * pscale_ref[:, cols]
        obuf[:, ATTN_WIDTH + gi * POOL_GROUP_DIM:ATTN_WIDTH + (gi + 1) * POOL_GROUP_DIM] = y.astype(BF16)

    x1 = x + jnp.dot(obuf[...], wout_ref[...], preferred_element_type=F32)
    x1_ref[...] = x1
    h2 = _rms(x1, gffn_ref[...])
    h2_hi = h2.astype(BF16)
    h2_ref[...] = h2_hi
    lgt_ref[...] = _router_logits(rwt_ref, h2, h2_hi) + rb_ref[...]


def _sample_mixer(x, ck_t, cv_t, sp, gattn, win, wpool, pscale, wout, gffn, rwt, rb, sinks):
    nb = x.shape[0]
    rows = ck_t.shape[0]
    assert nb == LANES and rows == N_META + WINDOW
    n_keys = -(-(rows + 1) // 8) * 8
    slopes = np.exp2(-8.0 * np.arange(1, N_HEADS + 1) / N_HEADS)
    dist = np.concatenate([np.zeros(N_META), WINDOW - np.arange(WINDOW), np.zeros(1)])
    bias = np.full((N_HEADS, n_keys, 1), MASKED, np.float32)
    bias[:, :rows + 1, 0] = -slopes[:, None] * dist[None, :]
    vm = pl.BlockSpec(memory_space=pltpu.VMEM)
    return pl.pallas_call(
        _sample_kernel,
        in_specs=[vm] * 15,
        out_specs=[vm] * 6,
        out_shape=[
            jax.ShapeDtypeStruct((nb, D_MODEL), F32),
            jax.ShapeDtypeStruct((nb, D_MODEL), BF16),
            jax.ShapeDtypeStruct((N_EXPERTS, nb), F32),
            jax.ShapeDtypeStruct(ck_t.shape, F32),
            jax.ShapeDtypeStruct(cv_t.shape, F32),
            jax.ShapeDtypeStruct(sp.shape, F32),
        ],
        scratch_shapes=[
            pltpu.VMEM((ATTN_WIDTH, nb), F32),
            pltpu.VMEM((N_HEADS, n_keys, nb), F32),
            pltpu.VMEM((ATTN_WIDTH, nb), F32),
            pltpu.VMEM((nb, D_MODEL), BF16),
        ],
        compiler_params=pltpu.CompilerParams(vmem_limit_bytes=VMEM_LIMIT),
        name="sample_mixer",
    )(x, ck_t, cv_t, sp, gattn, win, win[:, 0:ATTN_WIDTH + 2 * KV_WIDTH].T, wpool, pscale, wout, gffn, rwt, rb,
      sinks.astype(F32).reshape(N_HEADS, 1, 1), jnp.asarray(bias))


def _router_kernel(lg_ref, tri_ref, low_ref, gate_ref, col_ref, lrank_ref, lpos_ref,
                   tcar_ref, cnt_ref, carry):
    tr = ROUTE_BLOCK
    carry[...] = jnp.zeros_like(carry)

    def route_block(blk, c):
        _route_block(blk, tr, lg_ref, tri_ref, low_ref, gate_ref, col_ref, lrank_ref, lpos_ref, tcar_ref, carry)
        return c

    lax.fori_loop(0, lg_ref.shape[1] // tr, route_block, 0)
    cnt_ref[...] = carry[...].astype(jnp.int32)


def _route_block(blk, tr, lg_ref, tri_ref, low_ref, gate_ref, col_ref, lrank_ref, lpos_ref, tcar_ref, carry):
    base = pl.multiple_of(blk * tr, LANES)

    def at(j, width):
        return pl.ds(pl.multiple_of(base + j * width, LANES), width)

    work = lg_ref[:, pl.ds(base, tr)]
    eio = lax.broadcasted_iota(jnp.int32, work.shape, 0).astype(F32)
    sels, vals, idxs = [], [], []
    for _k in range(TOP_K):
        mx = jnp.max(work, axis=0, keepdims=True)
        idx = jnp.min(jnp.where(work == mx, eio, float(N_EXPERTS)), axis=0, keepdims=True)
        sel = eio == idx
        sels.append(sel)
        vals.append(mx)
        idxs.append(idx)
        work = jnp.where(sel, -jnp.inf, work)
    exps = [jnp.exp(vk - vals[0]) for vk in vals]
    tot = exps[0] + exps[1] + exps[2] + exps[3]
    onehot = jnp.zeros(work.shape, F32)
    for sel in sels:
        onehot = onehot + sel.astype(F32)
    before = jnp.dot(onehot.astype(BF16), tri_ref[...], preferred_element_type=F32) + carry[...]
    for kk in range(TOP_K):
        gate_ref[pl.ds(kk, 1), pl.ds(base, tr)] = exps[kk] / tot
    for j in range(tr // COMBINE_TILE):
        cols = slice(j * COMBINE_TILE, (j + 1) * COMBINE_TILE)
        tc = before[:, j * COMBINE_TILE:j * COMBINE_TILE + 1]
        tcar_ref[blk * (tr // COMBINE_TILE) + j] = tc.astype(jnp.int32)
        slack = tc - WIN_ALIGN * jnp.floor(tc * (1.0 / WIN_ALIGN))
        local = before[:, cols] - tc
        for kk in range(TOP_K):
            selk = sels[kk][:, cols]
            lr = jnp.sum(jnp.where(selk, local, 0.0), axis=0, keepdims=True)
            sl = jnp.sum(jnp.where(selk, slack, 0.0), axis=0, keepdims=True)
            lrank_ref[pl.ds(kk, 1), at(j, COMBINE_TILE)] = lr.astype(jnp.int32)
            col_ref[pl.ds(kk, 1), at(j, COMBINE_TILE)] = (
                idxs[kk][:, cols] * float(WIN_ROWS) + sl + lr).astype(jnp.int32)
    for j in range(tr // DISPATCH_TILE):
        cols = slice(j * DISPATCH_TILE, (j + 1) * DISPATCH_TILE)
        local = before[:, cols] - before[:, j * DISPATCH_TILE:j * DISPATCH_TILE + 1]
        tile_cnt = jnp.broadcast_to(jnp.sum(onehot[:, cols], axis=1, keepdims=True), local.shape)
        cnt_hi = jnp.floor(tile_cnt * (1.0 / 256.0))
        cnt_lo = tile_cnt - 256.0 * cnt_hi
        run_start = (256.0 * jnp.dot(low_ref[...], cnt_hi.astype(BF16), preferred_element_type=F32)
                     + jnp.dot(low_ref[...], cnt_lo.astype(BF16), preferred_element_type=F32))
        for kk in range(TOP_K):
            lp = jnp.sum(jnp.where(sels[kk][:, cols], run_start + local, 0.0), axis=0, keepdims=True)
            lpos_ref[pl.ds(kk, 1), at(j, DISPATCH_TILE)] = lp.astype(jnp.int32)
    carry[...] = carry[...] + jnp.sum(onehot, axis=1, keepdims=True)


def _router(logits_t):
    n = logits_t.shape[1]
    tr = ROUTE_BLOCK
    assert n % tr == 0
    tri = jnp.asarray(np.triu(np.ones((tr, tr), np.float32), k=1), BF16)
    low = jnp.asarray(np.tril(np.ones((N_EXPERTS, N_EXPERTS), np.float32), k=-1), BF16)
    vm = pl.BlockSpec(memory_space=pltpu.VMEM)
    return pl.pallas_call(
        _router_kernel,
        in_specs=[vm, vm, vm],
        out_specs=[vm] * 6,
        out_shape=[jax.ShapeDtypeStruct((TOP_K, n), F32),
                   jax.ShapeDtypeStruct((TOP_K, n), jnp.int32),
                   jax.ShapeDtypeStruct((TOP_K, n), jnp.int32),
                   jax.ShapeDtypeStruct((TOP_K, n), jnp.int32),
                   jax.ShapeDtypeStruct((n // COMBINE_TILE, N_EXPERTS, 1), jnp.int32),
                   jax.ShapeDtypeStruct((N_EXPERTS, 1), jnp.int32)],
        scratch_shapes=[pltpu.VMEM((N_EXPERTS, 1), F32)],
        name="router",
    )(logits_t, tri, low)


def _dispatch_kernel(lstart_ref, cnt_ref, dst_ref, h2_ref, lpos_ref, xs_hbm, stg0, stg1, sem):
    i = pl.program_id(0)
    n_tiles = pl.num_programs(0)
    dt = h2_ref.shape[0]
    rows = dt * TOP_K
    slot = i % 2
    stgs = (stg0, stg1)
    pieces = [p for p in (256, 128, 64, 32, 16, 8, 4, 2, 1) if p <= dt]
    assert dt < 512

    def drain(s):
        pltpu.make_async_copy(stgs[s], xs_hbm.at[pl.ds(0, rows)], sem.at[s]).wait()

    def issue_runs(tile, live, s):
        for e in range(N_EXPERTS):
            n = jnp.where(live, cnt_ref[tile * N_EXPERTS + e], 0)
            src0 = lstart_ref[tile * N_EXPERTS + e]
            dst0 = dst_ref[tile * N_EXPERTS + e]
            for pi, piece in enumerate(pieces):
                off = n & ~jnp.int32(2 * piece - 1)

                @pl.when((n & piece) != 0)
                def _(off=off, piece=piece, src0=src0, dst0=dst0, prio=(e + pi) % 2):
                    pltpu.make_async_copy(stgs[s].at[pl.ds(src0 + off, piece)],
                                          xs_hbm.at[pl.ds(dst0 + off, piece)], sem.at[s]).start(priority=prio)

    def sort_tile(s):
        rid = lax.broadcasted_iota(jnp.int32, (rows, dt), 0)
        hit = rid == lpos_ref[0:1, :]
        for kk in range(1, TOP_K):
            hit = jnp.logical_or(hit, rid == lpos_ref[kk:kk + 1, :])
        perm = jnp.where(hit, 1.0, 0.0).astype(BF16)
        srt = jnp.dot(perm, h2_ref[...], preferred_element_type=F32)
        packed = _pack_exact_bf16_pairs(srt)
        stgs[s][...] = packed.reshape(rows, PACK_CHUNKS, LANES)

    for s in range(2):
        @pl.when(slot == s)
        def _(s=s):
            @pl.when(i >= 2)
            def _():
                drain(s)

            issue_runs(jnp.maximum(i - 1, 0), i >= 1, 1 - s)
            sort_tile(s)

            @pl.when(i == n_tiles - 1)
            def _():
                issue_runs(i, True, s)
                drain(s)

                @pl.when(n_tiles >= 2)
                def _():
                    drain(1 - s)


def _dispatch(h2, lpos, lstart, cnt, dst, cap):
    n_tok = h2.shape[0]
    dt = DISPATCH_TILE
    assert n_tok % dt == 0
    grid_spec = pltpu.PrefetchScalarGridSpec(
        num_scalar_prefetch=3,
        grid=(n_tok // dt,),
        in_specs=[pl.BlockSpec((dt, D_MODEL), lambda i, a, b, c: (i, 0)),
                  pl.BlockSpec((TOP_K, dt), lambda i, a, b, c: (0, i)),
                  ],
        out_specs=pl.BlockSpec(memory_space=pl.ANY),
        scratch_shapes=[pltpu.VMEM((dt * TOP_K, PACK_CHUNKS, LANES), jnp.uint32),
                        pltpu.VMEM((dt * TOP_K, PACK_CHUNKS, LANES), jnp.uint32),
                        pltpu.SemaphoreType.DMA((2,))],
    )
    return pl.pallas_call(
        _dispatch_kernel,
        grid_spec=grid_spec,
        out_shape=jax.ShapeDtypeStruct((cap, PACK_CHUNKS, LANES), jnp.uint32),
        compiler_params=pltpu.CompilerParams(dimension_semantics=("arbitrary",),
                                             vmem_limit_bytes=VMEM_LIMIT),
        name="dispatch",
    )(lstart, cnt, dst, h2, lpos)


def _expert_kernel(n_xblocks, bexp_ref, nvalid_ref, epos_ref, elist_ref,
                   x_hbm, wgu_hbm, bgu_ref, wd_hbm, bd_ref, y_ref,
                   wgu_f32, wd_f32, wgu_bf, wd_bf, xbuf, xsem, wsem):
    i = pl.program_id(0)
    rb = y_ref.shape[0]
    nvalid = nvalid_ref[i]
    pos = epos_ref[i]
    fresh = jnp.logical_or(i == 0, pos != epos_ref[jnp.maximum(i - 1, 0)])
    slot = i % 2

    def x_copies(blk, s):
        return [pltpu.make_async_copy(x_hbm.at[pl.ds(blk * rb, rb), c, :],
                                      xbuf.at[s, :, pl.ds(c * LANES, LANES)], xsem.at[s])
                for c in range(PACK_CHUNKS)]

    def w_copies(p):
        e = elist_ref[p]
        s = p % 2
        return [pltpu.make_async_copy(wgu_hbm.at[e], wgu_f32.at[s], wsem.at[s, 0]),
                pltpu.make_async_copy(wd_hbm.at[e], wd_f32.at[s], wsem.at[s, 1])]

    @pl.when(i == 0)
    def _():
        for cp in x_copies(0, 0):
            cp.start()

        @pl.when(nvalid > 0)
        def _():
            for cp in w_copies(0):
                cp.start()

    @pl.when(i + 1 < n_xblocks)
    def _():
        for cp in x_copies(i + 1, 1 - slot):
            cp.start()

    @pl.when(jnp.logical_and(fresh, nvalid > 0))
    def _():
        @pl.when(elist_ref[pos + 1] >= 0)
        def _():
            for cp in w_copies(pos + 1):
                cp.start()

        for cp in w_copies(pos):
            cp.wait()
        ws = pos % 2
        chunk = 32

        def cast_gu(r, c):
            r0 = pl.multiple_of(r * chunk, chunk)
            wgu_bf[pl.ds(r0, chunk), :] = wgu_f32[ws, pl.ds(r0, chunk), :].astype(BF16)
            return c

        def cast_d(r, c):
            r0 = pl.multiple_of(r * chunk, chunk)
            wd_bf[pl.ds(r0, chunk), :] = wd_f32[ws, pl.ds(r0, chunk), :].astype(BF16)
            return c

        lax.fori_loop(0, D_MODEL // chunk, cast_gu, 0)
        lax.fori_loop(0, D_EXPERT // chunk, cast_d, 0)

    @pl.when(i < n_xblocks)
    def _():
        for cp in x_copies(i, slot):
            cp.wait()

    def ffn(rows):
        xw = xbuf[slot, 0:rows, :]
        xw = jnp.where(lax.broadcasted_iota(jnp.int32, xw.shape, 0) < nvalid, xw, jnp.uint32(0))
        xlo, xhi = _unpack_bf16_pairs(xw)
        x = jnp.concatenate([xlo, xhi], axis=1)
        g = jnp.dot(x, wgu_bf[:, 0:D_EXPERT], preferred_element_type=F32) + bgu_ref[0, :, 0:D_EXPERT]
        u = jnp.dot(x, wgu_bf[:, D_EXPERT:], preferred_element_type=F32) + bgu_ref[0, :, D_EXPERT:]
        g = jnp.minimum(g, SWIGLU_LIMIT)
        u = jnp.clip(u, -SWIGLU_LIMIT, SWIGLU_LIMIT)
        act = g * (1.0 / (1.0 + jnp.exp(-SWIGLU_ALPHA * g))) * (u + 1.0)
        y = jnp.dot(act.astype(BF16), wd_bf[...], preferred_element_type=F32) + bd_ref[0]
        row = lax.broadcasted_iota(jnp.int32, y.shape, 0)
        y_ref[0:rows, :] = jnp.where(row < nvalid, y, 0.0).astype(BF16)
        if rows < rb:
            y_ref[rows:, :] = jnp.zeros((rb - rows, D_MODEL), BF16)

    quarter = rb // EXPERT_PATHS
    for part in range(1, EXPERT_PATHS + 1):
        @pl.when(jnp.logical_and(nvalid > (part - 1) * quarter, nvalid <= part * quarter))
        def _(part=part):
            ffn(part * quarter)

    @pl.when(nvalid == 0)
    def _():
        y_ref[...] = jnp.zeros_like(y_ref)


def _experts(xs, block_expert, nvalid, block_pos, expert_list, wgu, bgu, wd, bd):
    rb = EXPERT_ROWS
    n_xblocks = xs.shape[0] // rb
    nblk = n_xblocks + 1
    any_space = pl.BlockSpec(memory_space=pl.ANY)
    grid_spec = pltpu.PrefetchScalarGridSpec(
        num_scalar_prefetch=4,
        grid=(nblk,),
        in_specs=[
            any_space,
            any_space,
            pl.BlockSpec((1, 1, 2 * D_EXPERT), lambda i, be, nu, ep, el: (be[i], 0, 0)),
            any_space,
            pl.BlockSpec((1, 1, D_MODEL), lambda i, be, nu, ep, el: (be[i], 0, 0)),
        ],
        out_specs=pl.BlockSpec((rb, D_MODEL), lambda i, be, nu, ep, el: (i, 0)),
        scratch_shapes=[pltpu.VMEM((2, D_MODEL, 2 * D_EXPERT), F32),
                        pltpu.VMEM((2, D_EXPERT, D_MODEL), F32),
                        pltpu.VMEM((D_MODEL, 2 * D_EXPERT), BF16),
                        pltpu.VMEM((D_EXPERT, D_MODEL), BF16),
                        pltpu.VMEM((2, rb, D_MODEL // 2), jnp.uint32),
                        pltpu.SemaphoreType.DMA((2,)),
                        pltpu.SemaphoreType.DMA((2, 2))],
    )
    return pl.pallas_call(
        functools.partial(_expert_kernel, n_xblocks),
        grid_spec=grid_spec,
        out_shape=jax.ShapeDtypeStruct((nblk * rb, D_MODEL), BF16),
        compiler_params=pltpu.CompilerParams(dimension_semantics=("arbitrary",),
                                             vmem_limit_bytes=VMEM_LIMIT),
        name="experts",
    )(block_expert, nvalid, block_pos, expert_list, xs, wgu, bgu.reshape(N_EXPERTS, 1, 2 * D_EXPERT), wd,
      bd.reshape(N_EXPERTS, 1, D_MODEL))


def _combine_kernel(n_prompt_tiles, offa_ref, nchunk_ref, tail_ref, ntail_ref,
                    x1p_ref, x1s_ref, info_ref, info_next_ref, gfin_ref, yb_hbm,
                    outp_ref, outs_ref, ybuf, gbuf0, gbuf1, acc_ref, sem, tsem):
    i = pl.program_id(0)
    n_tiles = pl.num_programs(0)

    def window_copy(tile, chunk, e, slot):
        base = pl.multiple_of(offa_ref[tile * N_EXPERTS + e] + chunk * WIN, WIN_ALIGN)
        return pltpu.make_async_copy(yb_hbm.at[pl.ds(base, WIN_ROWS), :],
                                     ybuf.at[slot, pl.ds(e * WIN_ROWS, WIN_ROWS), :],
                                     sem.at[slot])

    def start_windows(tile, chunk, slot):
        for e in range(N_EXPERTS):
            window_copy(tile, chunk, e, slot).start()

    def wait_windows(slot):
        pltpu.make_async_copy(yb_hbm.at[pl.ds(0, N_EXPERTS * WIN_ROWS), :], ybuf.at[slot],
                              sem.at[slot]).wait()

    def tail_copy(tile, e, slot):
        base = pl.multiple_of(offa_ref[tile * N_EXPERTS + e] + WIN_HEAD, WIN_ALIGN)
        return pltpu.make_async_copy(yb_hbm.at[pl.ds(base, WIN_ROWS - WIN_HEAD), :],
                                     ybuf.at[slot, pl.ds(e * WIN_ROWS + WIN_HEAD, WIN_ROWS - WIN_HEAD), :],
                                     tsem.at[slot])

    def start_first_chunk(tile, slot):
        for e in range(N_EXPERTS):
            base = pl.multiple_of(offa_ref[tile * N_EXPERTS + e], WIN_ALIGN)
            pltpu.make_async_copy(yb_hbm.at[pl.ds(base, WIN_HEAD), :],
                                  ybuf.at[slot, pl.ds(e * WIN_ROWS, WIN_HEAD), :],
                                  sem.at[slot]).start(priority=e % 2)

            @pl.when(tail_ref[tile * N_EXPERTS + e] != 0)
            def _(e=e):
                tail_copy(tile, e, slot).start(priority=(e + 1) % 2)

    def wait_first_chunk(tile, slot):
        pltpu.make_async_copy(yb_hbm.at[pl.ds(0, N_EXPERTS * WIN_HEAD), :],
                              ybuf.at[slot, pl.ds(0, N_EXPERTS * WIN_HEAD), :], sem.at[slot]).wait()

        def one_tail(t, c):
            tail_copy(tile, 0, slot).wait()
            return c

        lax.fori_loop(0, ntail_ref[tile], one_tail, 0)

    slot = i % 2

    @pl.when(i == 0)
    def _():
        ybuf[...] = jnp.zeros_like(ybuf)
        start_first_chunk(0, 0)

    lane = lax.broadcasted_iota(jnp.int32, (COMBINE_TILE, N_EXPERTS * WIN_ROWS), 1)

    def gate_matrix(ref, chunk):
        g = jnp.zeros(lane.shape, F32)
        for kk in range(TOP_K):
            lr = ref[:, TOP_K + kk:TOP_K + kk + 1]
            in_chunk = jnp.logical_and(lr >= chunk * WIN, lr < chunk * WIN + WIN)
            colk = jnp.where(in_chunk, ref[:, kk:kk + 1] - chunk * WIN, -1.0).astype(jnp.int32)
            g = jnp.where(lane == colk, ref[:, 2 * TOP_K + kk:2 * TOP_K + kk + 1], g)
        return g.astype(BF16)

    def moe_rows(gm, buf):
        return jnp.dot(gm, ybuf[buf], preferred_element_type=F32)

    @pl.when(i == 0)
    def _():
        gbuf0[...] = gate_matrix(info_ref, 0)

    def main(s):
        start_first_chunk(jnp.minimum(i + 1, n_tiles - 1), 1 - s)
        wait_first_chunk(i, s)
        g_cur, g_nxt = (gbuf0, gbuf1) if s == 0 else (gbuf1, gbuf0)
        acc_ref[...] = moe_rows(g_cur[...], s)
        g_nxt[...] = gate_matrix(info_next_ref, 0)

    for s in range(2):
        @pl.when(slot == s)
        def _(s=s):
            main(s)

    @pl.when(i == n_tiles - 1)
    def _():
        for s in range(2):
            @pl.when(slot == s)
            def _(s=s):
                wait_first_chunk(i, 1 - s)

    def extra_chunk(j, c):
        start_windows(i, j, 2)
        wait_windows(2)
        acc_ref[...] += moe_rows(gate_matrix(info_ref, j), 2)
        return c

    lax.fori_loop(1, nchunk_ref[i], extra_chunk, 0)

    @pl.when(i < n_prompt_tiles)
    def _():
        outp_ref[...] = _rms(x1p_ref[...] + acc_ref[...], gfin_ref[...])

    @pl.when(i >= n_prompt_tiles)
    def _():
        outs_ref[...] = _rms(x1s_ref[...] + acc_ref[...], gfin_ref[...])


def _combine(x1_p, x1_s, tok_info, gfin, yb, offa, nchunk, need_tail, ntail):
    ct = COMBINE_TILE
    n_p, n_s = x1_p.shape[0] // ct, x1_s.shape[0] // ct
    assert x1_p.shape[0] % ct == 0 and x1_s.shape[0] % ct == 0 and n_s >= 1
    n_info = tok_info.shape[1]
    grid_spec = pltpu.PrefetchScalarGridSpec(
        num_scalar_prefetch=4,
        grid=(n_p + n_s,),
        in_specs=[
            pl.BlockSpec((ct, D_MODEL), lambda i, *_: (jnp.minimum(i, n_p - 1), 0)),
            pl.BlockSpec((ct, D_MODEL), lambda i, *_: (jnp.maximum(i - n_p, 0), 0)),
            pl.BlockSpec((ct, n_info), lambda i, *_: (i, 0)),
            pl.BlockSpec((ct, n_info), lambda i, *_: (jnp.minimum(i + 1, n_p + n_s - 1), 0)),
            pl.BlockSpec((1, D_MODEL), lambda i, *_: (0, 0)),
            pl.BlockSpec(memory_space=pl.ANY),
        ],
        out_specs=[
            pl.BlockSpec((ct, D_MODEL), lambda i, *_: (jnp.minimum(i, n_p - 1), 0)),
            pl.BlockSpec((ct, D_MODEL), lambda i, *_: (jnp.maximum(i - n_p, 0), 0)),
        ],
        scratch_shapes=[pltpu.VMEM((3, N_EXPERTS * WIN_ROWS, D_MODEL), BF16),
                        pltpu.VMEM((ct, N_EXPERTS * WIN_ROWS), BF16),
                        pltpu.VMEM((ct, N_EXPERTS * WIN_ROWS), BF16),
                        pltpu.VMEM((ct, D_MODEL), F32),
                        pltpu.SemaphoreType.DMA((3,)),
                        pltpu.SemaphoreType.DMA((2,))],
    )
    return pl.pallas_call(
        functools.partial(_combine_kernel, n_p),
        grid_spec=grid_spec,
        out_shape=[jax.ShapeDtypeStruct(x1_p.shape, F32), jax.ShapeDtypeStruct(x1_s.shape, F32)],
        compiler_params=pltpu.CompilerParams(dimension_semantics=("arbitrary",),
                                             vmem_limit_bytes=VMEM_LIMIT),
        name="combine",
    )(offa, nchunk, need_tail, ntail, x1_p, x1_s, tok_info, tok_info, gfin, yb)


def kernel(x_prompt, x_sample, cache_k, cache_v, state_pool, meta_tokens, norm_attn, w_in, attn_sinks,
           w_pool, pool_scale, w_out, norm_ffn, router_w, router_b, w_gate_up, b_gate_up, w_down, b_down,
           norm_final):
    assert w_in.shape[0] == 1, "single-layer trunk"
    bsz, seq, _ = x_prompt.shape
    assert bsz == 1
    nb = x_sample.shape[0]
    n_tok = seq + nb
    gattn = norm_attn[0].reshape(1, D_MODEL)
    gffn = norm_ffn[0].reshape(1, D_MODEL)
    win = w_in[0].astype(BF16)
    wpool = w_pool[0].astype(BF16)
    wout = w_out[0].astype(BF16)
    pscale = pool_scale[0].reshape(1, POOL_WIDTH)
    rw_t = router_w[0].T
    rw_hi = rw_t.astype(BF16)
    rwt = jnp.stack([rw_hi, (rw_t - rw_hi.astype(F32)).astype(BF16)])
    rb = router_b[0].reshape(N_EXPERTS, 1)
    sinks = attn_sinks[0]

    cache_rows = N_META + WINDOW
    to_batch_minor = lambda c: jnp.transpose(c[0], (1, 2, 3, 0)).reshape(cache_rows, KV_WIDTH, nb)
    from_batch_minor = lambda c: jnp.transpose(
        c.reshape(cache_rows, N_KV_HEADS, HEAD_DIM, nb), (3, 0, 1, 2))[None]
    (x1_s, h2_s, lgt_s, kout_t, vout_t, pool_t) = _sample_mixer(
        x_sample[:, 0], to_batch_minor(cache_k), to_batch_minor(cache_v),
        jnp.transpose(state_pool[0], (1, 0, 2)), gattn, win, wpool, pscale, wout, gffn, rwt, rb, sinks)
    (x1_p, h2_all, lgt_all, kmeta, vmeta, ktail, vtail, ptail) = _prompt_mixer(
        x_prompt[0], meta_tokens, gattn, win, wpool, pscale, wout, gffn, rwt, rb, sinks, h2_s, lgt_s)

    gates, col, lrank, lpos, tcar, counts = _router(lgt_all)
    counts = counts[:, 0]
    tcar = tcar[:, :, 0]
    rbk = EXPERT_ROWS
    eids = jnp.arange(N_EXPERTS, dtype=jnp.int32)
    earlier = eids[None, :] < eids[:, None]
    excl_sum = lambda a: jnp.sum(jnp.where(earlier, a[..., None, :], 0), axis=-1)
    padded = (counts + rbk - 1) // rbk * rbk
    pad_start = excl_sum(padded).astype(jnp.int32)
    pad_end = pad_start + padded
    nblk = -(-(n_tok * TOP_K) // rbk) + N_EXPERTS
    cap = nblk * rbk
    block_start = jnp.arange(nblk + 1, dtype=jnp.int32) * rbk
    owns = (pad_start[None, :] <= block_start[:, None]) & (block_start[:, None] < pad_end[None, :])
    nvalid = jnp.sum(jnp.where(owns, jnp.clip(counts[None, :] - (block_start[:, None] - pad_start[None, :]),
                                              0, rbk), 0), axis=1).astype(jnp.int32)
    has_rows = counts > 0
    last_e = jnp.max(jnp.where(has_rows, eids, 0))
    block_expert = jnp.where(jnp.any(owns, axis=1), jnp.sum(jnp.where(owns, eids[None, :], 0), axis=1),
                             last_e).astype(jnp.int32)

    run_len = jnp.concatenate([tcar[1:], counts[None, :]], axis=0) - tcar
    dcar = tcar[::DISPATCH_TILE // COMBINE_TILE]
    drun_len = jnp.concatenate([dcar[1:], counts[None, :]], axis=0) - dcar
    flat = lambda a: a.astype(jnp.int32).reshape(-1)
    xs = _dispatch(h2_all, lpos, flat(excl_sum(drun_len)), flat(drun_len), flat(pad_start[None, :] + dcar),
                   cap)
    expert_pos = excl_sum(has_rows.astype(jnp.int32))
    at_pos = has_rows[None, :] & (expert_pos[None, :] == jnp.arange(N_EXPERTS + 1, dtype=jnp.int32)[:, None])
    expert_list = jnp.where(jnp.any(at_pos, axis=1), jnp.sum(jnp.where(at_pos, eids[None, :], 0), axis=1),
                            -1).astype(jnp.int32)
    block_pos = jnp.sum(jnp.where(block_expert[:, None] == eids[None, :], expert_pos[None, :], 0),
                        axis=1).astype(jnp.int32)
    yb = _experts(xs, block_expert, nvalid, block_pos, expert_list,
                  w_gate_up[0], b_gate_up[0], w_down[0], b_down[0])

    offa = (pad_start[None, :] + (tcar - tcar % WIN_ALIGN)).astype(jnp.int32).reshape(-1)
    nchunk = jnp.maximum(jnp.max((run_len + WIN - 1) // WIN, axis=1), 1).astype(jnp.int32)
    need_tail = (tcar % WIN_ALIGN + jnp.minimum(run_len, WIN)) > WIN_HEAD
    ntail = jnp.sum(need_tail, axis=1).astype(jnp.int32)
    gfin = norm_final.reshape(1, D_MODEL)
    tok_info = jnp.concatenate([col.astype(F32), lrank.astype(F32), gates], axis=0).T
    y_prompt, y_sample = _combine(x1_p, x1_s, tok_info, gfin, yb, offa, nchunk,
                                  need_tail.astype(jnp.int32).reshape(-1), ntail)

    kv_shape = (1, 1, N_META + WINDOW, N_KV_HEADS, HEAD_DIM)
    new_k_p = jnp.concatenate([kmeta, ktail], axis=0).reshape(kv_shape)
    new_v_p = jnp.concatenate([vmeta, vtail], axis=0).reshape(kv_shape)
    new_pool_p = ptail[16 - POOL_STATE:].reshape(1, 1, POOL_STATE, POOL_WIDTH)
    new_k_s = from_batch_minor(kout_t)
    new_v_s = from_batch_minor(vout_t)
    new_pool_s = jnp.transpose(pool_t, (1, 0, 2))[None]
    return (y_prompt[None], y_sample[:, None], new_k_p, new_v_p, new_pool_p, new_k_s, new_v_s, new_pool_s)
```
